```python
import jax, jax.numpy as jnp
from jax import lax
import numpy as np

D_MODEL = 1024
BATCH = 8
SEQ = 4096
DEPTH = 2

CHUNK = 64
Q_BLOCK = 128
D_FF = 2816
C_CONV = 512
CONV_WIDTH = 31
N_HEADS = 8
NOPE_DIM = 64
ROPE_DIM = 32
QK_DIM = NOPE_DIM + ROPE_DIM
V_DIM = 64
Q_LORA = 256
KV_LORA = 256
ROPE_THETA = 10000.0
EPS = 1e-6
D_IN = 2 * C_CONV + Q_LORA + KV_LORA + ROPE_DIM + 2 * D_MODEL

kernel_name = 'hybrid_conformer_mla_gated'


def _rms_norm(x, g):
    x32 = x.astype(jnp.float32)
    y = x32 * lax.rsqrt(jnp.mean(x32 * x32, axis=-1, keepdims=True) + EPS)
    return (y * g.astype(jnp.float32)).astype(x.dtype)


def _layer_norm(x, g, b):
    x32 = x.astype(jnp.float32)
    mu = jnp.mean(x32, axis=-1, keepdims=True)
    xc = x32 - mu
    y = xc * lax.rsqrt(jnp.mean(xc * xc, axis=-1, keepdims=True) + EPS)
    return (y * g.astype(jnp.float32) + b.astype(jnp.float32)).astype(x.dtype)


def _swiglu(h, w_gate, w_up, w_down):
    return (jax.nn.silu(h @ w_gate) * (h @ w_up)) @ w_down


def _rope(x, cos, sin):
    x1, x2 = jnp.split(x.astype(jnp.float32), 2, axis=-1)
    return jnp.concatenate([x1 * cos - x2 * sin, x2 * cos + x1 * sin], axis=-1).astype(x.dtype)


def _conv_module(u2, conv_w, conv_b, ln_g, ln_b, w_conv_out):
    a, gate = jnp.split(u2, 2, axis=-1)
    u = a * jax.nn.sigmoid(gate)
    u = lax.conv_general_dilated(
        u, conv_w[:, None, :], window_strides=(1,), padding=[(CONV_WIDTH - 1, 0)],
        dimension_numbers=('NWC', 'WIO', 'NWC'), feature_group_count=C_CONV) + conv_b
    u = jax.nn.silu(_layer_norm(u, ln_g, ln_b))
    return u @ w_conv_out


def _chunk_causal_attention(q, k, v):
    b, s, h, dq = q.shape
    nb = s // Q_BLOCK
    scale = dq ** -0.5
    key_chunk = jnp.arange(s) // CHUNK
    q_blocks = q.reshape(b, nb, Q_BLOCK, h, dq).transpose(1, 0, 2, 3, 4)

    def one_block(args):
        qi, i = args
        scores = jnp.einsum('bqhd,bkhd->bhqk', qi, k).astype(jnp.float32) * scale
        q_chunk = (i * Q_BLOCK + jnp.arange(Q_BLOCK)) // CHUNK
        mask = key_chunk[None, :] <= q_chunk[:, None]
        p = jax.nn.softmax(jnp.where(mask, scores, -jnp.inf), axis=-1)
        return jnp.einsum('bhqk,bkhd->bqhd', p.astype(v.dtype), v)

    out = lax.map(one_block, (q_blocks, jnp.arange(nb)))
    return out.transpose(1, 0, 2, 3, 4).reshape(b, s, h, v.shape[-1])


def _mla(c_q, c_kv, k_rope, cos, sin, cq_norm, ckv_norm, w_uq, w_ukv, q_norm, k_norm, w_mla_out):
    b, s, _ = c_q.shape
    q = (_rms_norm(c_q, cq_norm) @ w_uq).reshape(b, s, N_HEADS, QK_DIM)
    kv = (_rms_norm(c_kv, ckv_norm) @ w_ukv).reshape(b, s, N_HEADS, NOPE_DIM + V_DIM)
    k_nope, v = kv[..., :NOPE_DIM], kv[..., NOPE_DIM:]
    k_r = jnp.broadcast_to(k_rope[:, :, None, :], (b, s, N_HEADS, ROPE_DIM))
    k = jnp.concatenate([k_nope, k_r], axis=-1)
    q = _rms_norm(q, q_norm)
    k = _rms_norm(k, k_norm)
    q = jnp.concatenate([q[..., :NOPE_DIM], _rope(q[..., NOPE_DIM:], cos, sin)], axis=-1)
    k = jnp.concatenate([k[..., :NOPE_DIM], _rope(k[..., NOPE_DIM:], cos, sin)], axis=-1)
    o = _chunk_causal_attention(q, k, v)
    return o.reshape(b, s, N_HEADS * V_DIM) @ w_mla_out


def _fwd_setup_inputs(seed: int = 0) -> dict:
    key = jax.random.key(seed)
    ks = jax.random.split(key, 32)

    def nrm(k, shape, fan_in):
        return jax.random.normal(k, shape, jnp.float32) * fan_in ** -0.5

    def gain(k, shape):
        return 1.0 + 0.05 * jax.random.normal(k, shape, jnp.float32)

    def small(k, shape, s):
        return s * jax.random.normal(k, shape, jnp.float32)

    x = jax.random.normal(ks[0], (BATCH, SEQ, D_MODEL), jnp.float32)
    offset = jax.random.randint(ks[1], (BATCH, 1), 0, 1024, dtype=jnp.int32)
    positions = offset + jnp.arange(SEQ, dtype=jnp.int32)[None, :]
    return {
        'x': x,
        'positions': positions,
        'ffn1_norm': gain(ks[2], (DEPTH, D_MODEL)),
        'ffn1_w_gate': nrm(ks[3], (DEPTH, D_MODEL, D_FF), D_MODEL),
        'ffn1_w_up': nrm(ks[4], (DEPTH, D_MODEL, D_FF), D_MODEL),
        'ffn1_w_down': nrm(ks[5], (DEPTH, D_FF, D_MODEL), D_FF),
        'mix_norm': gain(ks[6], (DEPTH, D_MODEL)),
        'w_in': nrm(ks[7], (DEPTH, D_MODEL, D_IN), D_MODEL),
        'gate_bias': small(ks[8], (DEPTH, 2, D_MODEL), 0.1),
        'conv_w': nrm(ks[9], (DEPTH, CONV_WIDTH, C_CONV), CONV_WIDTH),
        'conv_b': small(ks[10], (DEPTH, C_CONV), 0.02),
        'conv_ln_g': gain(ks[11], (DEPTH, C_CONV)),
        'conv_ln_b': small(ks[12], (DEPTH, C_CONV), 0.02),
        'w_conv_out': nrm(ks[13], (DEPTH, C_CONV, D_MODEL), C_CONV),
        'cq_norm': gain(ks[14], (DEPTH, Q_LORA)),
        'ckv_norm': gain(ks[15], (DEPTH, KV_LORA)),
        'w_uq': nrm(ks[16], (DEPTH, Q_LORA, N_HEADS * QK_DIM), Q_LORA),
        'w_ukv': nrm(ks[17], (DEPTH, KV_LORA, N_HEADS * (NOPE_DIM + V_DIM)), KV_LORA),
        'q_norm': gain(ks[18], (DEPTH, QK_DIM)),
        'k_norm': gain(ks[19], (DEPTH, QK_DIM)),
        'w_mla_out': nrm(ks[20], (DEPTH, N_HEADS * V_DIM, D_MODEL), N_HEADS * V_DIM),
        'w_out': nrm(ks[21], (DEPTH, D_MODEL, D_MODEL), D_MODEL),
        'ffn2_norm': gain(ks[22], (DEPTH, D_MODEL)),
        'ffn2_w_gate': nrm(ks[23], (DEPTH, D_MODEL, D_FF), D_MODEL),
        'ffn2_w_up': nrm(ks[24], (DEPTH, D_MODEL, D_FF), D_MODEL),
        'ffn2_w_down': nrm(ks[25], (DEPTH, D_FF, D_MODEL), D_FF),
    }


def _fwd_reference(x, positions, ffn1_norm, ffn1_w_gate, ffn1_w_up, ffn1_w_down, mix_norm, w_in,
              gate_bias, conv_w, conv_b, conv_ln_g, conv_ln_b, w_conv_out, cq_norm, ckv_norm,
              w_uq, w_ukv, q_norm, k_norm, w_mla_out, w_out, ffn2_norm, ffn2_w_gate, ffn2_w_up,
              ffn2_w_down):
    b, s, _ = x.shape
    inv_freq = ROPE_THETA ** (-jnp.arange(0, ROPE_DIM, 2, dtype=jnp.float32) / ROPE_DIM)
    ang = positions.astype(jnp.float32)[..., None] * inv_freq
    cos = jnp.cos(ang)[:, :, None, :]
    sin = jnp.sin(ang)[:, :, None, :]
    o1 = 2 * C_CONV
    o2 = o1 + Q_LORA
    o3 = o2 + KV_LORA
    o4 = o3 + ROPE_DIM
    for l in range(DEPTH):
        x = x + 0.5 * _swiglu(_rms_norm(x, ffn1_norm[l]), ffn1_w_gate[l], ffn1_w_up[l], ffn1_w_down[l])
        h = _rms_norm(x, mix_norm[l])
        proj = h @ w_in[l]
        y_conv = _conv_module(proj[..., :o1], conv_w[l], conv_b[l], conv_ln_g[l], conv_ln_b[l],
                              w_conv_out[l])
        y_mla = _mla(proj[..., o1:o2], proj[..., o2:o3], proj[..., o3:o4], cos, sin,
                     cq_norm[l], ckv_norm[l], w_uq[l], w_ukv[l], q_norm[l], k_norm[l], w_mla_out[l])
        gates = jax.nn.sigmoid(proj[..., o4:].reshape(b, s, 2, D_MODEL) + gate_bias[l])
        y = gates[:, :, 0] * y_conv + gates[:, :, 1] * y_mla
        x = x + y @ w_out[l]
        x = x + 0.5 * _swiglu(_rms_norm(x, ffn2_norm[l]), ffn2_w_gate[l], ffn2_w_up[l], ffn2_w_down[l])
    return x


import jax as _jax
import jax.numpy as _jnp

TWIN_FORMAT = 'train_step'
FWD_PARAMS = ['x', 'positions', 'ffn1_norm', 'ffn1_w_gate', 'ffn1_w_up', 'ffn1_w_down', 'mix_norm', 'w_in', 'gate_bias', 'conv_w', 'conv_b', 'conv_ln_g', 'conv_ln_b', 'w_conv_out', 'cq_norm', 'ckv_norm', 'w_uq', 'w_ukv', 'q_norm', 'k_norm', 'w_mla_out', 'w_out', 'ffn2_norm', 'ffn2_w_gate', 'ffn2_w_up', 'ffn2_w_down']
TWIN_WEIGHTS = ['ffn1_norm', 'ffn1_w_gate', 'ffn1_w_up', 'ffn1_w_down', 'mix_norm', 'w_in', 'gate_bias', 'conv_w', 'conv_b', 'conv_ln_g', 'conv_ln_b', 'w_conv_out', 'cq_norm', 'ckv_norm', 'w_uq', 'w_ukv', 'q_norm', 'k_norm', 'w_mla_out', 'w_out', 'ffn2_norm', 'ffn2_w_gate', 'ffn2_w_up', 'ffn2_w_down']
TWIN_DIFF_INPUT = 'x'
TWIN_INPUTS = ['x', 'positions', 'ffn1_norm', 'ffn1_w_gate', 'ffn1_w_up', 'ffn1_w_down', 'mix_norm', 'w_in', 'gate_bias', 'conv_w', 'conv_b', 'conv_ln_g', 'conv_ln_b', 'w_conv_out', 'cq_norm', 'ckv_norm', 'w_uq', 'w_ukv', 'q_norm', 'k_norm', 'w_mla_out', 'w_out', 'ffn2_norm', 'ffn2_w_gate', 'ffn2_w_up', 'ffn2_w_down', 'loss_target', 'm_ffn1_norm', 'm_ffn1_w_gate', 'm_ffn1_w_up', 'm_ffn1_w_down', 'm_mix_norm', 'm_w_in', 'm_gate_bias', 'm_conv_w', 'm_conv_b', 'm_conv_ln_g', 'm_conv_ln_b', 'm_w_conv_out', 'm_cq_norm', 'm_ckv_norm', 'm_w_uq', 'm_w_ukv', 'm_q_norm', 'm_k_norm', 'm_w_mla_out', 'm_w_out', 'm_ffn2_norm', 'm_ffn2_w_gate', 'm_ffn2_w_up', 'm_ffn2_w_down', 'v_ffn1_norm', 'v_ffn1_w_gate', 'v_ffn1_w_up', 'v_ffn1_w_down', 'v_mix_norm', 'v_w_in', 'v_gate_bias', 'v_conv_w', 'v_conv_b', 'v_conv_ln_g', 'v_conv_ln_b', 'v_w_conv_out', 'v_cq_norm', 'v_ckv_norm', 'v_w_uq', 'v_w_ukv', 'v_q_norm', 'v_k_norm', 'v_w_mla_out', 'v_w_out', 'v_ffn2_norm', 'v_ffn2_w_gate', 'v_ffn2_w_up', 'v_ffn2_w_down']
TWIN_OUTPUTS = ['loss', 'grad_x', 'grad_ffn1_norm', 'grad_ffn1_w_gate', 'grad_ffn1_w_up', 'grad_ffn1_w_down', 'grad_mix_norm', 'grad_w_in', 'grad_gate_bias', 'grad_conv_w', 'grad_conv_b', 'grad_conv_ln_g', 'grad_conv_ln_b', 'grad_w_conv_out', 'grad_cq_norm', 'grad_ckv_norm', 'grad_w_uq', 'grad_w_ukv', 'grad_q_norm', 'grad_k_norm', 'grad_w_mla_out', 'grad_w_out', 'grad_ffn2_norm', 'grad_ffn2_w_gate', 'grad_ffn2_w_up', 'grad_ffn2_w_down', 'delta_ffn1_norm', 'delta_ffn1_w_gate', 'delta_ffn1_w_up', 'delta_ffn1_w_down', 'delta_mix_norm', 'delta_w_in', 'delta_gate_bias', 'delta_conv_w', 'delta_conv_b', 'delta_conv_ln_g', 'delta_conv_ln_b', 'delta_w_conv_out', 'delta_cq_norm', 'delta_ckv_norm', 'delta_w_uq', 'delta_w_ukv', 'delta_q_norm', 'delta_k_norm', 'delta_w_mla_out', 'delta_w_out', 'delta_ffn2_norm', 'delta_ffn2_w_gate', 'delta_ffn2_w_up', 'delta_ffn2_w_down', 'new_m_ffn1_norm', 'new_m_ffn1_w_gate', 'new_m_ffn1_w_up', 'new_m_ffn1_w_down', 'new_m_mix_norm', 'new_m_w_in', 'new_m_gate_bias', 'new_m_conv_w', 'new_m_conv_b', 'new_m_conv_ln_g', 'new_m_conv_ln_b', 'new_m_w_conv_out', 'new_m_cq_norm', 'new_m_ckv_norm', 'new_m_w_uq', 'new_m_w_ukv', 'new_m_q_norm', 'new_m_k_norm', 'new_m_w_mla_out', 'new_m_w_out', 'new_m_ffn2_norm', 'new_m_ffn2_w_gate', 'new_m_ffn2_w_up', 'new_m_ffn2_w_down', 'new_v_ffn1_norm', 'new_v_ffn1_w_gate', 'new_v_ffn1_w_up', 'new_v_ffn1_w_down', 'new_v_mix_norm', 'new_v_w_in', 'new_v_gate_bias', 'new_v_conv_w', 'new_v_conv_b', 'new_v_conv_ln_g', 'new_v_conv_ln_b', 'new_v_w_conv_out', 'new_v_cq_norm', 'new_v_ckv_norm', 'new_v_w_uq', 'new_v_w_ukv', 'new_v_q_norm', 'new_v_k_norm', 'new_v_w_mla_out', 'new_v_w_out', 'new_v_ffn2_norm', 'new_v_ffn2_w_gate', 'new_v_ffn2_w_up', 'new_v_ffn2_w_down']
TWIN_LEAF_KINDS = {'loss': 'loss', 'grad_x': 'grad_x', 'grad_ffn1_norm': 'grad_w', 'grad_ffn1_w_gate': 'grad_w', 'grad_ffn1_w_up': 'grad_w', 'grad_ffn1_w_down': 'grad_w', 'grad_mix_norm': 'grad_w', 'grad_w_in': 'grad_w', 'grad_gate_bias': 'grad_w', 'grad_conv_w': 'grad_w', 'grad_conv_b': 'grad_w', 'grad_conv_ln_g': 'grad_w', 'grad_conv_ln_b': 'grad_w', 'grad_w_conv_out': 'grad_w', 'grad_cq_norm': 'grad_w', 'grad_ckv_norm': 'grad_w', 'grad_w_uq': 'grad_w', 'grad_w_ukv': 'grad_w', 'grad_q_norm': 'grad_w', 'grad_k_norm': 'grad_w', 'grad_w_mla_out': 'grad_w', 'grad_w_out': 'grad_w', 'grad_ffn2_norm': 'grad_w', 'grad_ffn2_w_gate': 'grad_w', 'grad_ffn2_w_up': 'grad_w', 'grad_ffn2_w_down': 'grad_w', 'delta_ffn1_norm': 'delta_w', 'delta_ffn1_w_gate': 'delta_w', 'delta_ffn1_w_up': 'delta_w', 'delta_ffn1_w_down': 'delta_w', 'delta_mix_norm': 'delta_w', 'delta_w_in': 'delta_w', 'delta_gate_bias': 'delta_w', 'delta_conv_w': 'delta_w', 'delta_conv_b': 'delta_w', 'delta_conv_ln_g': 'delta_w', 'delta_conv_ln_b': 'delta_w', 'delta_w_conv_out': 'delta_w', 'delta_cq_norm': 'delta_w', 'delta_ckv_norm': 'delta_w', 'delta_w_uq': 'delta_w', 'delta_w_ukv': 'delta_w', 'delta_q_norm': 'delta_w', 'delta_k_norm': 'delta_w', 'delta_w_mla_out': 'delta_w', 'delta_w_out': 'delta_w', 'delta_ffn2_norm': 'delta_w', 'delta_ffn2_w_gate': 'delta_w', 'delta_ffn2_w_up': 'delta_w', 'delta_ffn2_w_down': 'delta_w', 'new_m_ffn1_norm': 'new_m', 'new_m_ffn1_w_gate': 'new_m', 'new_m_ffn1_w_up': 'new_m', 'new_m_ffn1_w_down': 'new_m', 'new_m_mix_norm': 'new_m', 'new_m_w_in': 'new_m', 'new_m_gate_bias': 'new_m', 'new_m_conv_w': 'new_m', 'new_m_conv_b': 'new_m', 'new_m_conv_ln_g': 'new_m', 'new_m_conv_ln_b': 'new_m', 'new_m_w_conv_out': 'new_m', 'new_m_cq_norm': 'new_m', 'new_m_ckv_norm': 'new_m', 'new_m_w_uq': 'new_m', 'new_m_w_ukv': 'new_m', 'new_m_q_norm': 'new_m', 'new_m_k_norm': 'new_m', 'new_m_w_mla_out': 'new_m', 'new_m_w_out': 'new_m', 'new_m_ffn2_norm': 'new_m', 'new_m_ffn2_w_gate': 'new_m', 'new_m_ffn2_w_up': 'new_m', 'new_m_ffn2_w_down': 'new_m', 'new_v_ffn1_norm': 'new_v', 'new_v_ffn1_w_gate': 'new_v', 'new_v_ffn1_w_up': 'new_v', 'new_v_ffn1_w_down': 'new_v', 'new_v_mix_norm': 'new_v', 'new_v_w_in': 'new_v', 'new_v_gate_bias': 'new_v', 'new_v_conv_w': 'new_v', 'new_v_conv_b': 'new_v', 'new_v_conv_ln_g': 'new_v', 'new_v_conv_ln_b': 'new_v', 'new_v_w_conv_out': 'new_v', 'new_v_cq_norm': 'new_v', 'new_v_ckv_norm': 'new_v', 'new_v_w_uq': 'new_v', 'new_v_w_ukv': 'new_v', 'new_v_q_norm': 'new_v', 'new_v_k_norm': 'new_v', 'new_v_w_mla_out': 'new_v', 'new_v_w_out': 'new_v', 'new_v_ffn2_norm': 'new_v', 'new_v_ffn2_w_gate': 'new_v', 'new_v_ffn2_w_up': 'new_v', 'new_v_ffn2_w_down': 'new_v'}


def _forward(args):
    return _fwd_reference(*[args[k] for k in FWD_PARAMS])


def _output_shape():
    out = _jax.eval_shape(lambda: _forward(_fwd_setup_inputs(0)))
    return out.shape, out.dtype

N_MICROBATCH = 1
ADAM_LR = 0.001
ADAM_B1 = 0.9
ADAM_B2 = 0.999
ADAM_EPS = 1e-08
ADAM_WD = 0.01
ADAM_STEP = 10
PER_EXAMPLE_BATCH_AXIS = {'x': 0, 'positions': 0, 'loss_target': 0}
SHARED_INPUTS = []
_WEIGHT_DTYPES = {'ffn1_norm': _jnp.float32, 'ffn1_w_gate': _jnp.float32, 'ffn1_w_up': _jnp.float32, 'ffn1_w_down': _jnp.float32, 'mix_norm': _jnp.float32, 'w_in': _jnp.float32, 'gate_bias': _jnp.float32, 'conv_w': _jnp.float32, 'conv_b': _jnp.float32, 'conv_ln_g': _jnp.float32, 'conv_ln_b': _jnp.float32, 'w_conv_out': _jnp.float32, 'cq_norm': _jnp.float32, 'ckv_norm': _jnp.float32, 'w_uq': _jnp.float32, 'w_ukv': _jnp.float32, 'q_norm': _jnp.float32, 'k_norm': _jnp.float32, 'w_mla_out': _jnp.float32, 'w_out': _jnp.float32, 'ffn2_norm': _jnp.float32, 'ffn2_w_gate': _jnp.float32, 'ffn2_w_up': _jnp.float32, 'ffn2_w_down': _jnp.float32}
MOMENT_SCALE = {'ffn1_norm': 6.177526e+00, 'ffn1_w_gate': 8.222684e-02, 'ffn1_w_up': 8.432580e-02, 'ffn1_w_down': 1.372764e-01, 'mix_norm': 3.540393e-01, 'w_in': 1.263997e-01, 'gate_bias': 9.014037e-01, 'conv_w': 3.272224e-01, 'conv_b': 5.267173e+00, 'conv_ln_g': 8.752453e+00, 'conv_ln_b': 6.259829e+00, 'w_conv_out': 7.453753e-01, 'cq_norm': 5.685648e-02, 'ckv_norm': 5.137794e-01, 'w_uq': 3.406346e-02, 'w_ukv': 1.719687e-01, 'q_norm': 3.889709e-01, 'k_norm': 3.963572e-01, 'w_mla_out': 1.813337e-01, 'w_out': 6.194587e-01, 'ffn2_norm': 6.061503e+00, 'ffn2_w_gate': 8.865124e-02, 'ffn2_w_up': 8.373110e-02, 'ffn2_w_down': 1.360868e-01}


def _to_microbatches(a, axis):
    t = _jnp.moveaxis(a, axis, 0)
    t = t.reshape((N_MICROBATCH, t.shape[0] // N_MICROBATCH) + t.shape[1:])
    return _jnp.moveaxis(t, 1, axis + 1)


def setup_inputs(seed: int = 0) -> dict:
    inp = _fwd_setup_inputs(seed)
    key = _jax.random.fold_in(_jax.random.key(seed), 7919)
    shape, _ = _output_shape()
    out = dict(inp)
    out["loss_target"] = _jax.random.normal(_jax.random.fold_in(key, 0), shape, _jnp.float32)
    for i, name in enumerate(TWIN_WEIGHTS):
        w = inp[name].astype(_jnp.float32)
        if MOMENT_SCALE is None:
            s = _jnp.sqrt(_jnp.mean(_jnp.square(w)) + 1e-30)
        else:
            s = MOMENT_SCALE[name]
        km, kv = _jax.random.split(_jax.random.fold_in(key, i + 1))
        out[name] = w
        out["m_" + name] = s * _jax.random.normal(km, w.shape, _jnp.float32)
        out["v_" + name] = (s * s) * _jax.random.uniform(kv, w.shape, _jnp.float32, 0.5, 1.5)
    if N_MICROBATCH > 1:
        for name, axis in PER_EXAMPLE_BATCH_AXIS.items():
            out[name] = _to_microbatches(out[name], axis)
    return {'x': out['x'], 'positions': out['positions'], 'ffn1_norm': out['ffn1_norm'], 'ffn1_w_gate': out['ffn1_w_gate'], 'ffn1_w_up': out['ffn1_w_up'], 'ffn1_w_down': out['ffn1_w_down'], 'mix_norm': out['mix_norm'], 'w_in': out['w_in'], 'gate_bias': out['gate_bias'], 'conv_w': out['conv_w'], 'conv_b': out['conv_b'], 'conv_ln_g': out['conv_ln_g'], 'conv_ln_b': out['conv_ln_b'], 'w_conv_out': out['w_conv_out'], 'cq_norm': out['cq_norm'], 'ckv_norm': out['ckv_norm'], 'w_uq': out['w_uq'], 'w_ukv': out['w_ukv'], 'q_norm': out['q_norm'], 'k_norm': out['k_norm'], 'w_mla_out': out['w_mla_out'], 'w_out': out['w_out'], 'ffn2_norm': out['ffn2_norm'], 'ffn2_w_gate': out['ffn2_w_gate'], 'ffn2_w_up': out['ffn2_w_up'], 'ffn2_w_down': out['ffn2_w_down'], 'loss_target': out['loss_target'], 'm_ffn1_norm': out['m_ffn1_norm'], 'm_ffn1_w_gate': out['m_ffn1_w_gate'], 'm_ffn1_w_up': out['m_ffn1_w_up'], 'm_ffn1_w_down': out['m_ffn1_w_down'], 'm_mix_norm': out['m_mix_norm'], 'm_w_in': out['m_w_in'], 'm_gate_bias': out['m_gate_bias'], 'm_conv_w': out['m_conv_w'], 'm_conv_b': out['m_conv_b'], 'm_conv_ln_g': out['m_conv_ln_g'], 'm_conv_ln_b': out['m_conv_ln_b'], 'm_w_conv_out': out['m_w_conv_out'], 'm_cq_norm': out['m_cq_norm'], 'm_ckv_norm': out['m_ckv_norm'], 'm_w_uq': out['m_w_uq'], 'm_w_ukv': out['m_w_ukv'], 'm_q_norm': out['m_q_norm'], 'm_k_norm': out['m_k_norm'], 'm_w_mla_out': out['m_w_mla_out'], 'm_w_out': out['m_w_out'], 'm_ffn2_norm': out['m_ffn2_norm'], 'm_ffn2_w_gate': out['m_ffn2_w_gate'], 'm_ffn2_w_up': out['m_ffn2_w_up'], 'm_ffn2_w_down': out['m_ffn2_w_down'], 'v_ffn1_norm': out['v_ffn1_norm'], 'v_ffn1_w_gate': out['v_ffn1_w_gate'], 'v_ffn1_w_up': out['v_ffn1_w_up'], 'v_ffn1_w_down': out['v_ffn1_w_down'], 'v_mix_norm': out['v_mix_norm'], 'v_w_in': out['v_w_in'], 'v_gate_bias': out['v_gate_bias'], 'v_conv_w': out['v_conv_w'], 'v_conv_b': out['v_conv_b'], 'v_conv_ln_g': out['v_conv_ln_g'], 'v_conv_ln_b': out['v_conv_ln_b'], 'v_w_conv_out': out['v_w_conv_out'], 'v_cq_norm': out['v_cq_norm'], 'v_ckv_norm': out['v_ckv_norm'], 'v_w_uq': out['v_w_uq'], 'v_w_ukv': out['v_w_ukv'], 'v_q_norm': out['v_q_norm'], 'v_k_norm': out['v_k_norm'], 'v_w_mla_out': out['v_w_mla_out'], 'v_w_out': out['v_w_out'], 'v_ffn2_norm': out['v_ffn2_norm'], 'v_ffn2_w_gate': out['v_ffn2_w_gate'], 'v_ffn2_w_up': out['v_ffn2_w_up'], 'v_ffn2_w_down': out['v_ffn2_w_down']}


def _loss(weights, diff, rest, loss_target):
    with _jax.named_scope("forward"):
        args = {**rest, TWIN_DIFF_INPUT: diff, **{k: w.astype(_WEIGHT_DTYPES[k]) for k, w in weights.items()}}
        y = _forward(args)
    with _jax.named_scope("loss_head"):
        err = _jnp.square(y.astype(_jnp.float32) - loss_target)
        return 0.5 * _jnp.sum(_jnp.mean(err, axis=-1)) if err.ndim else 0.5 * err


def _adamw(w, g, m, v):
    m = ADAM_B1 * m + (1.0 - ADAM_B1) * g
    v = ADAM_B2 * v + (1.0 - ADAM_B2) * _jnp.square(g)
    m_hat = m / (1.0 - ADAM_B1 ** ADAM_STEP)
    v_hat = v / (1.0 - ADAM_B2 ** ADAM_STEP)
    delta = -ADAM_LR * (m_hat / (_jnp.sqrt(v_hat) + ADAM_EPS) + ADAM_WD * w)
    return delta, m, v


def reference(x, positions, ffn1_norm, ffn1_w_gate, ffn1_w_up, ffn1_w_down, mix_norm, w_in, gate_bias, conv_w, conv_b, conv_ln_g, conv_ln_b, w_conv_out, cq_norm, ckv_norm, w_uq, w_ukv, q_norm, k_norm, w_mla_out, w_out, ffn2_norm, ffn2_w_gate, ffn2_w_up, ffn2_w_down, loss_target, m_ffn1_norm, m_ffn1_w_gate, m_ffn1_w_up, m_ffn1_w_down, m_mix_norm, m_w_in, m_gate_bias, m_conv_w, m_conv_b, m_conv_ln_g, m_conv_ln_b, m_w_conv_out, m_cq_norm, m_ckv_norm, m_w_uq, m_w_ukv, m_q_norm, m_k_norm, m_w_mla_out, m_w_out, m_ffn2_norm, m_ffn2_w_gate, m_ffn2_w_up, m_ffn2_w_down, v_ffn1_norm, v_ffn1_w_gate, v_ffn1_w_up, v_ffn1_w_down, v_mix_norm, v_w_in, v_gate_bias, v_conv_w, v_conv_b, v_conv_ln_g, v_conv_ln_b, v_w_conv_out, v_cq_norm, v_ckv_norm, v_w_uq, v_w_ukv, v_q_norm, v_k_norm, v_w_mla_out, v_w_out, v_ffn2_norm, v_ffn2_w_gate, v_ffn2_w_up, v_ffn2_w_down):
    given = dict(x=x, positions=positions, ffn1_norm=ffn1_norm, ffn1_w_gate=ffn1_w_gate, ffn1_w_up=ffn1_w_up, ffn1_w_down=ffn1_w_down, mix_norm=mix_norm, w_in=w_in, gate_bias=gate_bias, conv_w=conv_w, conv_b=conv_b, conv_ln_g=conv_ln_g, conv_ln_b=conv_ln_b, w_conv_out=w_conv_out, cq_norm=cq_norm, ckv_norm=ckv_norm, w_uq=w_uq, w_ukv=w_ukv, q_norm=q_norm, k_norm=k_norm, w_mla_out=w_mla_out, w_out=w_out, ffn2_norm=ffn2_norm, ffn2_w_gate=ffn2_w_gate, ffn2_w_up=ffn2_w_up, ffn2_w_down=ffn2_w_down, loss_target=loss_target, m_ffn1_norm=m_ffn1_norm, m_ffn1_w_gate=m_ffn1_w_gate, m_ffn1_w_up=m_ffn1_w_up, m_ffn1_w_down=m_ffn1_w_down, m_mix_norm=m_mix_norm, m_w_in=m_w_in, m_gate_bias=m_gate_bias, m_conv_w=m_conv_w, m_conv_b=m_conv_b, m_conv_ln_g=m_conv_ln_g, m_conv_ln_b=m_conv_ln_b, m_w_conv_out=m_w_conv_out, m_cq_norm=m_cq_norm, m_ckv_norm=m_ckv_norm, m_w_uq=m_w_uq, m_w_ukv=m_w_ukv, m_q_norm=m_q_norm, m_k_norm=m_k_norm, m_w_mla_out=m_w_mla_out, m_w_out=m_w_out, m_ffn2_norm=m_ffn2_norm, m_ffn2_w_gate=m_ffn2_w_gate, m_ffn2_w_up=m_ffn2_w_up, m_ffn2_w_down=m_ffn2_w_down, v_ffn1_norm=v_ffn1_norm, v_ffn1_w_gate=v_ffn1_w_gate, v_ffn1_w_up=v_ffn1_w_up, v_ffn1_w_down=v_ffn1_w_down, v_mix_norm=v_mix_norm, v_w_in=v_w_in, v_gate_bias=v_gate_bias, v_conv_w=v_conv_w, v_conv_b=v_conv_b, v_conv_ln_g=v_conv_ln_g, v_conv_ln_b=v_conv_ln_b, v_w_conv_out=v_w_conv_out, v_cq_norm=v_cq_norm, v_ckv_norm=v_ckv_norm, v_w_uq=v_w_uq, v_w_ukv=v_w_ukv, v_q_norm=v_q_norm, v_k_norm=v_k_norm, v_w_mla_out=v_w_mla_out, v_w_out=v_w_out, v_ffn2_norm=v_ffn2_norm, v_ffn2_w_gate=v_ffn2_w_gate, v_ffn2_w_up=v_ffn2_w_up, v_ffn2_w_down=v_ffn2_w_down)
    weights = {n: given[n] for n in TWIN_WEIGHTS}
    shared = {n: given[n] for n in SHARED_INPUTS}
    per_example = {n: given[n] for n in ['x', 'positions']}
    grad_fn = _jax.value_and_grad(_loss, argnums=(0, 1))

    def one_microbatch(ex, loss_target):
        ex = dict(ex)
        diff = ex.pop(TWIN_DIFF_INPUT)
        return grad_fn(weights, diff, {**shared, **ex}, loss_target)

    if N_MICROBATCH == 1:
        loss, (grad_w, grad_x) = one_microbatch(per_example, given["loss_target"])
    else:
        def body(carry, xs):
            loss_sum, grad_sum = carry
            l_k, (gw_k, gx_k) = one_microbatch(xs[0], xs[1])
            with _jax.named_scope("update"):
                return (loss_sum + l_k, _jax.tree.map(_jnp.add, grad_sum, gw_k)), gx_k

        init = (_jnp.zeros((), _jnp.float32), _jax.tree.map(_jnp.zeros_like, weights))
        (loss, grad_w), grad_x = _jax.lax.scan(body, init, (per_example, given["loss_target"]))
    with _jax.named_scope("update"):
        delta_w, new_m, new_v = {}, {}, {}
        for n in TWIN_WEIGHTS:
            delta_w[n], new_m[n], new_v[n] = _adamw(weights[n], grad_w[n], given["m_" + n], given["v_" + n])
    return (loss, grad_x, *[grad_w[n] for n in TWIN_WEIGHTS], *[delta_w[n] for n in TWIN_WEIGHTS],
            *[new_m[n] for n in TWIN_WEIGHTS], *[new_v[n] for n in TWIN_WEIGHTS])
```

```python
import functools

import jax
import jax.numpy as jnp
from jax import lax
from jax.experimental import pallas as pl
from jax.experimental.pallas import tpu as pltpu

F32, BF16 = jnp.float32, jnp.bfloat16
SDS = jax.ShapeDtypeStruct
MESH = pl.DeviceIdType.MESH
ANY = pl.BlockSpec(memory_space=pl.ANY)

NCHIP = 4
N_DEV = 8
N_HEADS, NOPE, ROPE, V_DIM = 8, 64, 32, 64
QK = NOPE + ROPE
HP = 128
CHUNK = 64
KW = 31
CPAD = 32
ROPE_THETA = 10000.0
EPS = 1e-6
LR, B1, B2, EPS_ADAM, WD, STEP = 0.001, 0.9, 0.999, 1e-08, 0.01, 10
VMEM_BIG = 48 << 20


def _cparams(sem=None, vmem=None):
    kw = {}
    if sem is not None:
        kw["dimension_semantics"] = sem
    if vmem is not None:
        kw["vmem_limit_bytes"] = vmem
    return pltpu.CompilerParams(**kw)


def _rt(t):
    return min(512, t // 2)


def _pick(n, cands):
    for c in cands:
        if c <= n and n % c == 0:
            return c
    return n


def _rows_tile(r, c, nblocks):
    t = r
    while t % 32 == 0 and 2 * nblocks * t * c * 4 > VMEM_BIG // 2:
        t //= 2
    return t


def _sig(v):
    return 1.0 / (1.0 + jnp.exp(-v))


def _rms_r(v):
    return lax.rsqrt(jnp.mean(v * v, axis=-1, keepdims=True) + EPS)


def _rms_bwd(xv, g, dy):
    r = _rms_r(xv)
    xh = xv * r
    dg = jnp.sum(dy * xh, axis=0, keepdims=True)
    dxh = dy * g
    dx = r * (dxh - xh * jnp.mean(dxh * xh, axis=-1, keepdims=True))
    return dx, dg


def _dot(a, b):
    return jnp.dot(a, b, preferred_element_type=F32)


def _dot_nt(a, b):
    return lax.dot_general(a, b, (((1,), (1,)), ((), ())), preferred_element_type=F32)


def _dot_tn(a, b):
    return lax.dot_general(a, b, (((0,), (0,)), ((), ())), preferred_element_type=F32)


def _mm(a, b, mode, *, name, out_dtype=F32, res=None, alpha=1.0, b_l=None, a_g=False, b_g=False,
        out_l=None, out_nl=None, out_prev=None):
    a2 = a.shape[1:] if a_g else a.shape
    b2 = b.shape[-2:]
    if mode == "nn":
        (m, k), (k2, n) = a2, b2
    elif mode == "nt":
        (m, k), (n, k2) = a2, b2
    else:
        (k, m), (k2, n) = a2, b2
    assert k == k2, (name, a.shape, b.shape)
    g = a.shape[0] if a_g else (b.shape[0] if b_g else 1)
    out_g = a_g or b_g
    tm = m if m <= 768 else _pick(m, (512, 256, 128))
    tn = n if n <= 1280 else _pick(n, (1280, 1024, 768, 512, 256, 128))
    tk = k if k <= 1280 else _pick(k, (1280, 1024, 768, 512, 256, 128))
    nk = k // tk
    dn = {"nn": (((1,), (0,)), ((), ())), "nt": (((1,), (1,)), ((), ())), "tn": (((0,), (0,)), ((), ()))}[mode]

    def body(*refs):
        a_ref, b_ref = refs[0], refs[1]
        pos = 2
        res_ref = None
        if res is not None:
            res_ref = refs[pos]
            pos += 1
        if out_prev is not None:
            pos += 1
        o_ref = refs[pos]
        p = lax.dot_general(a_ref[...].astype(BF16), b_ref[...].astype(BF16), dn, preferred_element_type=F32)

        def fin(v):
            if alpha != 1.0:
                v = v * alpha
            if res_ref is not None:
                v = v + res_ref[...]
            o_ref[...] = v.astype(out_dtype)

        if nk == 1:
            fin(p)
        else:
            acc_ref = refs[pos + 1]
            kk = pl.program_id(3)

            @pl.when(kk == 0)
            def _():
                acc_ref[...] = p

            @pl.when(kk > 0)
            def _():
                acc_ref[...] += p

            @pl.when(kk == nk - 1)
            def _():
                fin(acc_ref[...])

    def spec(block, idx, lead):
        if lead is None:
            return pl.BlockSpec(block, idx)
        if lead == "g":
            return pl.BlockSpec((None,) + block, lambda gg, i, j, kk: (gg,) + idx(gg, i, j, kk))
        return pl.BlockSpec((None,) + block, lambda gg, i, j, kk: (lead,) + idx(gg, i, j, kk))

    if mode == "tn":
        a_spec = spec((tk, tm), lambda gg, i, j, kk: (kk, i), "g" if a_g else None)
    else:
        a_spec = spec((tm, tk), lambda gg, i, j, kk: (i, kk), "g" if a_g else None)
    b_lead = "g" if b_g else b_l
    if mode == "nt":
        b_spec = spec((tn, tk), lambda gg, i, j, kk: (j, kk), b_lead)
    else:
        b_spec = spec((tk, tn), lambda gg, i, j, kk: (kk, j), b_lead)
    in_specs, args = [a_spec, b_spec], [a, b]
    if res is not None:
        in_specs.append(pl.BlockSpec((tm, tn), lambda gg, i, j, kk: (i, j)))
        args.append(res)
    aliases = {}
    if out_prev is not None:
        aliases = {len(args): 0}
        in_specs.append(ANY)
        args.append(out_prev)
    oblock, oshape = (tm, tn), (m, n)
    oidx = lambda gg, i, j, kk: (i, j)
    if out_g:
        oblock, oshape = (None,) + oblock, (g,) + oshape
        oidx = lambda gg, i, j, kk: (gg, i, j)
    if out_l is not None:
        oblock, oshape = (None,) + oblock, (out_nl,) + oshape
        inner = oidx
        oidx = lambda gg, i, j, kk: (out_l,) + inner(gg, i, j, kk)
    return pl.pallas_call(
        body, name=name, grid=(g, m // tm, n // tn, nk), in_specs=in_specs,
        out_specs=pl.BlockSpec(oblock, oidx), out_shape=SDS(oshape, out_dtype),
        scratch_shapes=[pltpu.VMEM((tm, tn), F32)] if nk > 1 else [],
        input_output_aliases=aliases,
        compiler_params=_cparams(("arbitrary",) * 4, VMEM_BIG))(*args)


def _ffn_fwd(x, gains, l, wg, wu, wd, name):
    t, d = x.shape
    f = wg.shape[-1]
    tt = _rt(t)

    def body(x_ref, g_ref, wg_ref, wu_ref, wd_ref, o_ref, gt_ref, up_ref, h_sc, acc_sc):
        j = pl.program_id(1)

        @pl.when(j == 0)
        def _():
            xv = x_ref[...]
            h_sc[...] = (xv * _rms_r(xv) * g_ref[l:l + 1, :]).astype(BF16)
            acc_sc[...] = jnp.zeros_like(acc_sc)

        h = h_sc[...]
        gt = _dot(h, wg_ref[...])
        up = _dot(h, wu_ref[...])
        gt_ref[...] = gt.astype(BF16)
        up_ref[...] = up.astype(BF16)
        act = (gt * _sig(gt) * up).astype(BF16)
        acc_sc[...] += _dot(act, wd_ref[...])

        @pl.when(j == NCHIP - 1)
        def _():
            o_ref[...] = x_ref[...] + 0.5 * acc_sc[...]

    def wspec(k, n):
        return pl.BlockSpec((None, None, k, n), lambda i, j: (l, j, 0, 0))

    row = pl.BlockSpec((tt, d), lambda i, j: (i, 0))
    sh = pl.BlockSpec((None, tt, f), lambda i, j: (j, i, 0))
    return pl.pallas_call(
        body, name=name, grid=(t // tt, NCHIP),
        in_specs=[row, pl.BlockSpec(gains.shape, lambda i, j: (0, 0)), wspec(d, f), wspec(d, f), wspec(f, d)],
        out_specs=[row, sh, sh],
        out_shape=[SDS((t, d), F32), SDS((NCHIP, t, f), BF16), SDS((NCHIP, t, f), BF16)],
        scratch_shapes=[pltpu.VMEM((tt, d), BF16), pltpu.VMEM((tt, d), F32)],
        compiler_params=_cparams(("arbitrary", "arbitrary"), VMEM_BIG))(x, gains, wg, wu, wd)


def _ffn_bwd(x, gains, l, wg, wu, wd, gts, ups, dout, name):
    t, d = x.shape
    f = wg.shape[-1]
    tt = _rt(t)

    def body(x_ref, g_ref, wg_ref, wu_ref, wd_ref, gt_ref, up_ref, do_ref,
             dx_ref, dg_ref, dgt_ref, dup_ref, act_ref, h_ref, dob_ref, dh_sc):
        i, j = pl.program_id(0), pl.program_id(1)

        @pl.when(j == 0)
        def _():
            xv = x_ref[...]
            h_ref[...] = (xv * _rms_r(xv) * g_ref[l:l + 1, :]).astype(BF16)
            dob_ref[...] = (0.5 * do_ref[...]).astype(BF16)
            dh_sc[...] = jnp.zeros_like(dh_sc)

        @pl.when(jnp.logical_and(i == 0, j == 0))
        def _():
            dg_ref[...] = jnp.zeros_like(dg_ref)

        dact = _dot_nt(dob_ref[...], wd_ref[...])
        gt = gt_ref[...].astype(F32)
        up = up_ref[...].astype(F32)
        s = _sig(gt)
        sl = gt * s
        dup = (dact * sl).astype(BF16)
        dgt = (dact * up * (s * (1.0 + gt * (1.0 - s)))).astype(BF16)
        dgt_ref[...] = dgt
        dup_ref[...] = dup
        act_ref[...] = (sl * up).astype(BF16)
        dh_sc[...] += _dot_nt(dgt, wg_ref[...]) + _dot_nt(dup, wu_ref[...])

        @pl.when(j == NCHIP - 1)
        def _():
            dxn, dg = _rms_bwd(x_ref[...], g_ref[l:l + 1, :], dh_sc[...])
            dx_ref[...] = do_ref[...] + dxn
            dg_ref[...] += dg

    def wspec(k, n):
        return pl.BlockSpec((None, None, k, n), lambda i, j: (l, j, 0, 0))

    row = pl.BlockSpec((tt, d), lambda i, j: (i, 0))
    sh = pl.BlockSpec((None, tt, f), lambda i, j: (j, i, 0))
    return pl.pallas_call(
        body, name=name, grid=(t // tt, NCHIP),
        in_specs=[row, pl.BlockSpec(gains.shape, lambda i, j: (0, 0)), wspec(d, f), wspec(d, f), wspec(f, d), sh, sh, row],
        out_specs=[row, pl.BlockSpec((1, d), lambda i, j: (0, 0)), sh, sh, sh, row, row],
        out_shape=[SDS((t, d), F32), SDS((1, d), F32)] + [SDS((NCHIP, t, f), BF16)] * 3 + [SDS((t, d), BF16)] * 2,
        scratch_shapes=[pltpu.VMEM((tt, d), F32)],
        compiler_params=_cparams(("arbitrary", "arbitrary"), VMEM_BIG))(x, gains, wg, wu, wd, gts, ups, dout)


def _rms_fwd(x, gains, l, name):
    t, d = x.shape
    tt = _rt(t)

    def body(x_ref, g_ref, o_ref):
        xv = x_ref[...]
        o_ref[...] = (xv * _rms_r(xv) * g_ref[l:l + 1, :]).astype(BF16)

    row = pl.BlockSpec((tt, d), lambda i: (i, 0))
    return pl.pallas_call(body, name=name, grid=(t // tt,), in_specs=[row, pl.BlockSpec(gains.shape, lambda i: (0, 0))],
                          out_specs=row, out_shape=SDS((t, d), BF16), compiler_params=_cparams(("arbitrary",)))(x, gains)


def _rms_back(x, gains, l, dh, dres, name):
    t, d = x.shape
    tt = _rt(t)

    def body(x_ref, g_ref, dh_ref, dr_ref, dx_ref, dg_ref):
        @pl.when(pl.program_id(0) == 0)
        def _():
            dg_ref[...] = jnp.zeros_like(dg_ref)

        dxn, dg = _rms_bwd(x_ref[...], g_ref[l:l + 1, :], dh_ref[...])
        dx_ref[...] = dr_ref[...] + dxn
        dg_ref[...] += dg

    row = pl.BlockSpec((tt, d), lambda i: (i, 0))
    return pl.pallas_call(body, name=name, grid=(t // tt,),
                          in_specs=[row, pl.BlockSpec(gains.shape, lambda i: (0, 0)), row, row],
                          out_specs=[row, pl.BlockSpec((1, d), lambda i: (0, 0))],
                          out_shape=[SDS((t, d), F32), SDS((1, d), F32)],
                          compiler_params=_cparams(("arbitrary",)))(x, gains, dh, dres)


def _dwconv_fwd(proj, convw, convb, l, o_a, cc, name):
    t = proj.shape[0]
    r = min(256, t)
    nb = cc // HP

    def body(a_ref, gate_ref, w_ref, b_ref, uc_ref, u_sc):
        u_sc[0:CPAD, :] = jnp.zeros((CPAD, HP), F32)

        def fill(ci, carry):
            r0 = pl.multiple_of(ci * r, r)
            u_sc[pl.ds(CPAD + r0, r), :] = a_ref[pl.ds(r0, r), :] * _sig(gate_ref[pl.ds(r0, r), :])
            return carry

        lax.fori_loop(0, t // r, fill, 0)
        w = w_ref[l]
        bias = b_ref[l:l + 1, :]

        def conv(ci, carry):
            r0 = pl.multiple_of(ci * r, r)
            win = u_sc[pl.ds(r0, r + CPAD), :]
            acc = jnp.zeros((r, HP), F32) + bias
            for k in range(KW):
                off = CPAD - (KW - 1) + k
                acc = acc + win[off:off + r, :] * w[k:k + 1, :]
            uc_ref[pl.ds(r0, r), :] = acc
            return carry

        lax.fori_loop(0, t // r, conv, 0)

    col = lambda base: pl.BlockSpec((t, HP), lambda c: (0, base // HP + c))
    return pl.pallas_call(
        body, name=name, grid=(nb,),
        in_specs=[col(o_a), col(o_a + cc), pl.BlockSpec((convw.shape[0], CPAD, HP), lambda c: (0, 0, c)),
                  pl.BlockSpec((convb.shape[0], HP), lambda c: (0, c))],
        out_specs=pl.BlockSpec((t, HP), lambda c: (0, c)), out_shape=SDS((t, cc), F32),
        scratch_shapes=[pltpu.VMEM((t + CPAD, HP), F32)],
        compiler_params=_cparams(("arbitrary",), VMEM_BIG))(proj, proj, convw, convb)


def _dwconv_bwd(duc, proj, convw, l, o_a, cc, name):
    t = proj.shape[0]
    r = min(256, t)
    nb = cc // HP

    def body(d_ref, a_ref, gate_ref, w_ref, du_ref, dw_ref, u_sc, d_sc, dw_sc):
        u_sc[0:CPAD, :] = jnp.zeros((CPAD, HP), F32)
        d_sc[t:t + CPAD, :] = jnp.zeros((CPAD, HP), F32)
        dw_sc[...] = jnp.zeros_like(dw_sc)

        def fill(ci, carry):
            r0 = pl.multiple_of(ci * r, r)
            u_sc[pl.ds(CPAD + r0, r), :] = a_ref[pl.ds(r0, r), :] * _sig(gate_ref[pl.ds(r0, r), :])
            d_sc[pl.ds(r0, r), :] = d_ref[pl.ds(r0, r), :]
            return carry

        lax.fori_loop(0, t // r, fill, 0)
        w = w_ref[l]

        def conv(ci, carry):
            r0 = pl.multiple_of(ci * r, r)
            dwin = d_sc[pl.ds(r0, r + CPAD), :]
            uwin = u_sc[pl.ds(r0, r + CPAD), :]
            dcur = dwin[0:r, :]
            acc = jnp.zeros((r, HP), F32)
            for k in range(KW):
                acc = acc + dwin[KW - 1 - k:KW - 1 - k + r, :] * w[k:k + 1, :]
                off = CPAD - (KW - 1) + k
                part = (dcur * uwin[off:off + r, :]).reshape(r // 8, 8, HP).sum(axis=0)
                dw_sc[8 * k:8 * k + 8, :] += part
            du_ref[pl.ds(r0, r), :] = acc
            return carry

        lax.fori_loop(0, t // r, conv, 0)
        dw_ref[...] = jnp.zeros_like(dw_ref)
        for k in range(KW):
            dw_ref[k:k + 1, :] = jnp.sum(dw_sc[8 * k:8 * k + 8, :], axis=0, keepdims=True)

    col = lambda base: pl.BlockSpec((t, HP), lambda c: (0, base // HP + c))
    return pl.pallas_call(
        body, name=name, grid=(nb,),
        in_specs=[pl.BlockSpec((t, HP), lambda c: (0, c)), col(o_a), col(o_a + cc),
                  pl.BlockSpec((convw.shape[0], CPAD, HP), lambda c: (0, 0, c))],
        out_specs=[pl.BlockSpec((t, HP), lambda c: (0, c)), pl.BlockSpec((CPAD, HP), lambda c: (0, c))],
        out_shape=[SDS((t, cc), F32), SDS((CPAD, cc), F32)],
        scratch_shapes=[pltpu.VMEM((t + CPAD, HP), F32), pltpu.VMEM((t + CPAD, HP), F32), pltpu.VMEM((8 * CPAD, HP), F32)],
        compiler_params=_cparams(("arbitrary",), VMEM_BIG))(duc, proj, proj, convw)


def _ln_parts(uc, g, b):
    mu = jnp.mean(uc, axis=-1, keepdims=True)
    xc = uc - mu
    r = lax.rsqrt(jnp.mean(xc * xc, axis=-1, keepdims=True) + EPS)
    xh = xc * r
    return r, xh, xh * g + b


def _ln_silu(uc, ln_g, ln_b, l, name):
    t, cc = uc.shape
    tt = _rt(t)

    def body(u_ref, g_ref, b_ref, s_ref):
        _, _, yv = _ln_parts(u_ref[...], g_ref[l:l + 1, :], b_ref[l:l + 1, :])
        s_ref[...] = (yv * _sig(yv)).astype(BF16)

    row = pl.BlockSpec((tt, cc), lambda i: (i, 0))
    full = pl.BlockSpec(ln_g.shape, lambda i: (0, 0))
    return pl.pallas_call(body, name=name, grid=(t // tt,), in_specs=[row, full, full], out_specs=row,
                          out_shape=SDS((t, cc), BF16), compiler_params=_cparams(("arbitrary",)))(uc, ln_g, ln_b)


def _ln_silu_bwd(uc, ln_g, ln_b, l, ds, name):
    t, cc = uc.shape
    tt = _rt(t)

    def body(u_ref, g_ref, b_ref, ds_ref, du_ref, dg_ref, db_ref, dcb_ref):
        @pl.when(pl.program_id(0) == 0)
        def _():
            dg_ref[...] = jnp.zeros_like(dg_ref)
            db_ref[...] = jnp.zeros_like(db_ref)
            dcb_ref[...] = jnp.zeros_like(dcb_ref)

        g = g_ref[l:l + 1, :]
        r, xh, yv = _ln_parts(u_ref[...], g, b_ref[l:l + 1, :])
        sy = _sig(yv)
        dy = ds_ref[...] * (sy * (1.0 + yv * (1.0 - sy)))
        dg_ref[...] += jnp.sum(dy * xh, axis=0, keepdims=True)
        db_ref[...] += jnp.sum(dy, axis=0, keepdims=True)
        dxh = dy * g
        du = r * (dxh - jnp.mean(dxh, axis=-1, keepdims=True) - xh * jnp.mean(dxh * xh, axis=-1, keepdims=True))
        du_ref[...] = du
        dcb_ref[...] += jnp.sum(du, axis=0, keepdims=True)

    row = pl.BlockSpec((tt, cc), lambda i: (i, 0))
    full = pl.BlockSpec(ln_g.shape, lambda i: (0, 0))
    vec = pl.BlockSpec((1, cc), lambda i: (0, 0))
    return pl.pallas_call(body, name=name, grid=(t // tt,), in_specs=[row, full, full, row],
                          out_specs=[row, vec, vec, vec], out_shape=[SDS((t, cc), F32)] + [SDS((1, cc), F32)] * 3,
                          compiler_params=_cparams(("arbitrary",)))(uc, ln_g, ln_b, ds)


def _glu_bwd(du, proj, dproj, o_a, cc, name):
    t = du.shape[0]
    tt = _rt(t)

    def body(du_ref, a_ref, gate_ref, prev_ref, o_ref):
        sg = _sig(gate_ref[...])
        dv = du_ref[...]
        o_ref[:, 0:cc] = (dv * sg).astype(BF16)
        o_ref[:, cc:2 * cc] = (dv * a_ref[...] * sg * (1.0 - sg)).astype(BF16)

    return pl.pallas_call(
        body, name=name, grid=(t // tt,),
        in_specs=[pl.BlockSpec((tt, cc), lambda i: (i, 0)), pl.BlockSpec((tt, cc), lambda i: (i, o_a // cc)),
                  pl.BlockSpec((tt, cc), lambda i: (i, o_a // cc + 1)), ANY],
        out_specs=pl.BlockSpec((tt, 2 * cc), lambda i: (i, o_a // (2 * cc))),
        out_shape=SDS(dproj.shape, dproj.dtype), input_output_aliases={3: 0},
        compiler_params=_cparams(("arbitrary",)))(du, proj, proj, dproj)


def _rope(v, cs, s1, s2):
    return v * cs + pltpu.roll(v, HP - ROPE // 2, 1) * s1 + pltpu.roll(v, ROPE // 2, 1) * s2


def _rope_t(dv, cs, s1, s2):
    return dv * cs + pltpu.roll(dv * s1, ROPE // 2, 1) + pltpu.roll(dv * s2, HP - ROPE // 2, 1)


def _head_norm(v, g):
    r = lax.rsqrt(jnp.sum(v * v, axis=-1, keepdims=True) * (1.0 / QK) + EPS)
    return v * r * g, r


def _head_norm_bwd(v, r, g, dy):
    xh = v * r
    dg = jnp.sum(dy * xh, axis=0, keepdims=True)
    dxh = dy * g
    dx = r * (dxh - xh * (jnp.sum(dxh * xh, axis=-1, keepdims=True) * (1.0 / QK)))
    return dx, dg


def _mla_specs(t, tt, lay, l, cq_norm, ckv_norm, qn, kn, wuq, wukv):
    ql, kvl = cq_norm.shape[1], ckv_norm.shape[1]
    hw = N_HEADS * HP
    full = lambda a: pl.BlockSpec(a.shape, lambda i: (0,) * a.ndim)
    tab = pl.BlockSpec((tt, HP), lambda i: (i, 0))
    return [pl.BlockSpec((tt, ql), lambda i: (i, lay["cq"] // ql)),
            pl.BlockSpec((tt, kvl), lambda i: (i, lay["ckv"] // kvl)),
            pl.BlockSpec((tt, HP), lambda i: (i, lay["kr"] // HP)),
            full(cq_norm), full(ckv_norm), full(qn), full(kn),
            pl.BlockSpec((None, ql, hw), lambda i: (l, 0, 0)), pl.BlockSpec((None, kvl, hw), lambda i: (l, 0, 0)),
            tab, tab, tab]


def _mla_pre_fwd(proj, lay, l, cq_norm, ckv_norm, qn, kn, wuq, wukv, tabs, name):
    t = proj.shape[0]
    tt = _rt(t)
    hw = N_HEADS * HP

    def body(cq_ref, ckv_ref, kr_ref, gq_ref, gkv_ref, qn_ref, kn_ref, wq_ref, wkv_ref, c_ref, s1_ref, s2_ref,
             q_ref, k_ref, v_ref):
        cq = cq_ref[...]
        cqn = (cq * _rms_r(cq) * gq_ref[l:l + 1, :]).astype(BF16)
        ckv = ckv_ref[...]
        ckvn = (ckv * _rms_r(ckv) * gkv_ref[l:l + 1, :]).astype(BF16)
        qraw = _dot(cqn, wq_ref[...])
        kv = _dot(ckvn, wkv_ref[...])
        v_ref[...] = kv.astype(BF16)
        lane = lax.broadcasted_iota(jnp.int32, (tt, HP), 1)
        krs = pltpu.roll(jnp.where(lane < ROPE, kr_ref[...], 0.0), NOPE, 1)
        cs, s1, s2 = c_ref[...], s1_ref[...], s2_ref[...]
        gq, gk = qn_ref[l:l + 1, :], kn_ref[l:l + 1, :]
        for h in range(N_HEADS):
            sl = slice(h * HP, (h + 1) * HP)
            qh, _ = _head_norm(qraw[:, sl], gq)
            q_ref[:, sl] = (_rope(qh, cs, s1, s2) * QK ** -0.5).astype(BF16)
            kh, _ = _head_norm(jnp.where(lane < NOPE, kv[:, sl], krs), gk)
            k_ref[:, sl] = _rope(kh, cs, s1, s2).astype(BF16)

    row = pl.BlockSpec((tt, hw), lambda i: (i, 0))
    return pl.pallas_call(
        body, name=name, grid=(t // tt,),
        in_specs=_mla_specs(t, tt, lay, l, cq_norm, ckv_norm, qn, kn, wuq, wukv),
        out_specs=[row, row, row], out_shape=[SDS((t, hw), BF16)] * 3,
        compiler_params=_cparams(("arbitrary",), VMEM_BIG))(
            proj, proj, proj, cq_norm, ckv_norm, qn, kn, wuq, wukv, *tabs)


def _mla_pre_bwd(proj, lay, l, cq_norm, ckv_norm, qn, kn, wuq, wukv, tabs, dq, dk, dv, dproj, name):
    t = proj.shape[0]
    tt = _rt(t)
    hw = N_HEADS * HP
    ql, kvl = cq_norm.shape[1], ckv_norm.shape[1]
    wm = lay["wm"]

    def body(cq_ref, ckv_ref, kr_ref, gq_ref, gkv_ref, qn_ref, kn_ref, wq_ref, wkv_ref, c_ref, s1_ref, s2_ref,
             dq_ref, dk_ref, dv_ref, prev_ref,
             o_ref, dqr_ref, dkv_ref, cqn_ref, ckvn_ref, dgq_ref, dgkv_ref, dqn_ref, dkn_ref):
        @pl.when(pl.program_id(0) == 0)
        def _():
            for ref in (dgq_ref, dgkv_ref, dqn_ref, dkn_ref):
                ref[...] = jnp.zeros_like(ref)

        cq = cq_ref[...]
        cqn = (cq * _rms_r(cq) * gq_ref[l:l + 1, :]).astype(BF16)
        ckv = ckv_ref[...]
        ckvn = (ckv * _rms_r(ckv) * gkv_ref[l:l + 1, :]).astype(BF16)
        cqn_ref[...] = cqn
        ckvn_ref[...] = ckvn
        qraw = _dot(cqn, wq_ref[...])
        kv = _dot(ckvn, wkv_ref[...])
        lane = lax.broadcasted_iota(jnp.int32, (tt, HP), 1)
        krs = pltpu.roll(jnp.where(lane < ROPE, kr_ref[...], 0.0), NOPE, 1)
        cs, s1, s2 = c_ref[...], s1_ref[...], s2_ref[...]
        gq, gk = qn_ref[l:l + 1, :], kn_ref[l:l + 1, :]
        dkr = jnp.zeros((tt, HP), F32)
        dgq = jnp.zeros((1, HP), F32)
        dgk = jnp.zeros((1, HP), F32)
        for h in range(N_HEADS):
            sl = slice(h * HP, (h + 1) * HP)
            qh = qraw[:, sl]
            _, rq = _head_norm(qh, gq)
            dqh, dg = _head_norm_bwd(qh, rq, gq, _rope_t(dq_ref[:, sl] * QK ** -0.5, cs, s1, s2))
            dgq = dgq + dg
            dqr_ref[:, sl] = dqh.astype(BF16)
            kp = jnp.where(lane < NOPE, kv[:, sl], krs)
            _, rk = _head_norm(kp, gk)
            dkp, dg = _head_norm_bwd(kp, rk, gk, _rope_t(dk_ref[:, sl], cs, s1, s2))
            dgk = dgk + dg
            dkv_ref[:, sl] = (jnp.where(lane < NOPE, dkp, 0.0) + dv_ref[:, sl]).astype(BF16)
            dkr = dkr + dkp
        dqn_ref[...] += dgq
        dkn_ref[...] += dgk
        dkr = jnp.where(lane < ROPE, pltpu.roll(dkr, HP - NOPE, 1), 0.0)
        dcq, dg = _rms_bwd(cq, gq_ref[l:l + 1, :], _dot_nt(dqr_ref[...], wq_ref[...]))
        dgq_ref[...] += dg
        dckv, dg = _rms_bwd(ckv, gkv_ref[l:l + 1, :], _dot_nt(dkv_ref[...], wkv_ref[...]))
        dgkv_ref[...] += dg
        o_ref[:, 0:ql] = dcq.astype(BF16)
        o_ref[:, ql:ql + kvl] = dckv.astype(BF16)
        o_ref[:, ql + kvl:ql + kvl + HP] = dkr.astype(BF16)
        o_ref[:, ql + kvl + HP:wm] = jnp.zeros((tt, wm - ql - kvl - HP), BF16)

    row = pl.BlockSpec((tt, hw), lambda i: (i, 0))
    vec = lambda n: pl.BlockSpec((1, n), lambda i: (0, 0))
    return pl.pallas_call(
        body, name=name, grid=(t // tt,),
        in_specs=_mla_specs(t, tt, lay, l, cq_norm, ckv_norm, qn, kn, wuq, wukv) + [row, row, row, ANY],
        out_specs=[pl.BlockSpec((tt, wm), lambda i: (i, lay["cq"] // wm)), row, row,
                   pl.BlockSpec((tt, ql), lambda i: (i, 0)), pl.BlockSpec((tt, kvl), lambda i: (i, 0)),
                   vec(ql), vec(kvl), vec(HP), vec(HP)],
        out_shape=[SDS(dproj.shape, dproj.dtype), SDS((t, hw), BF16), SDS((t, hw), BF16), SDS((t, ql), BF16),
                   SDS((t, kvl), BF16), SDS((1, ql), F32), SDS((1, kvl), F32), SDS((1, HP), F32), SDS((1, HP), F32)],
        input_output_aliases={15: 0},
        compiler_params=_cparams(("arbitrary",), VMEM_BIG))(
            proj, proj, proj, cq_norm, ckv_norm, qn, kn, wuq, wukv, *tabs, dq, dk, dv, dproj)


def _comb_fwd(proj, gate_bias, l, yc, ym, name):
    t, d = yc.shape
    tt = _rt(t)

    def body(p_ref, b_ref, yc_ref, ym_ref, y_ref):
        b = b_ref[l]
        g0 = _sig(p_ref[:, 0:d] + b[0:1, :])
        g1 = _sig(p_ref[:, d:2 * d] + b[1:2, :])
        y_ref[...] = (g0 * yc_ref[...] + g1 * ym_ref[...]).astype(BF16)

    row = pl.BlockSpec((tt, d), lambda i: (i, 0))
    return pl.pallas_call(
        body, name=name, grid=(t // tt,),
        in_specs=[pl.BlockSpec((tt, 2 * d), lambda i: (i, 0)), pl.BlockSpec(gate_bias.shape, lambda i: (0, 0, 0)), row, row],
        out_specs=row, out_shape=SDS((t, d), BF16), compiler_params=_cparams(("arbitrary",)))(proj, gate_bias, yc, ym)


def _comb_bwd(proj, gate_bias, l, yc, ym, dy, dp_cols, name):
    t, d = yc.shape
    tt = _rt(t)

    def body(p_ref, b_ref, yc_ref, ym_ref, dy_ref, dyc_ref, dym_ref, dp_ref, db_ref):
        @pl.when(pl.program_id(0) == 0)
        def _():
            db_ref[...] = jnp.zeros_like(db_ref)

        b = b_ref[l]
        dyv = dy_ref[...]
        g0 = _sig(p_ref[:, 0:d] + b[0:1, :])
        g1 = _sig(p_ref[:, d:2 * d] + b[1:2, :])
        dyc_ref[...] = (dyv * g0).astype(BF16)
        dym_ref[...] = (dyv * g1).astype(BF16)
        dg0 = dyv * yc_ref[...] * g0 * (1.0 - g0)
        dg1 = dyv * ym_ref[...] * g1 * (1.0 - g1)
        dp_ref[:, 0:d] = dg0.astype(BF16)
        dp_ref[:, d:2 * d] = dg1.astype(BF16)
        db_ref[0:1, :] += jnp.sum(dg0, axis=0, keepdims=True)
        db_ref[1:2, :] += jnp.sum(dg1, axis=0, keepdims=True)

    row = pl.BlockSpec((tt, d), lambda i: (i, 0))
    wide = pl.BlockSpec((tt, 2 * d), lambda i: (i, 0))
    return pl.pallas_call(
        body, name=name, grid=(t // tt,),
        in_specs=[wide, pl.BlockSpec(gate_bias.shape, lambda i: (0, 0, 0)), row, row, row],
        out_specs=[row, row, wide, pl.BlockSpec((2, d), lambda i: (0, 0))],
        out_shape=[SDS((t, d), BF16), SDS((t, d), BF16), SDS((t, dp_cols), BF16), SDS((2, d), F32)],
        compiler_params=_cparams(("arbitrary",)))(proj, gate_bias, yc, ym, dy)


def _loss_grad(y, target, name):
    t, d = y.shape
    tt = _rt(t)
    nt = t // tt

    def body(y_ref, t_ref, dy_ref, loss_ref, acc_sc):
        i = pl.program_id(0)

        @pl.when(i == 0)
        def _():
            acc_sc[...] = jnp.zeros_like(acc_sc)

        diff = y_ref[...] - t_ref[...]
        dy_ref[...] = diff * (1.0 / d)
        acc_sc[...] += jnp.sum(diff * diff, axis=0, keepdims=True)

        @pl.when(i == nt - 1)
        def _():
            tot = jnp.sum(acc_sc[...], axis=1, keepdims=True) * (0.5 / d)
            loss_ref[...] = jnp.broadcast_to(tot, (1, HP))

    row = pl.BlockSpec((tt, d), lambda i: (i, 0))
    return pl.pallas_call(body, name=name, grid=(nt,), in_specs=[row, row],
                          out_specs=[row, pl.BlockSpec((1, HP), lambda i: (0, 0))],
                          out_shape=[SDS((t, d), F32), SDS((1, HP), F32)],
                          scratch_shapes=[pltpu.VMEM((1, d), F32)],
                          compiler_params=_cparams(("arbitrary",)))(y, target)


def _chunk_mask(tq):
    rows = lax.broadcasted_iota(jnp.int32, (tq, tq), 0) // CHUNK
    cols = lax.broadcasted_iota(jnp.int32, (tq, tq), 1) // CHUNK
    return cols <= rows


NEG = -1e30


def _flash_fwd(q, k, v, name):
    t = q.shape[0]
    tq = _rt(t)
    nq = t // tq

    def body(q_ref, k_ref, v_ref, o_ref, lse_ref, m_sc, l_sc, acc_sc):
        qi, ki = pl.program_id(1), pl.program_id(2)

        @pl.when(ki == 0)
        def _():
            m_sc[...] = jnp.full_like(m_sc, NEG)
            l_sc[...] = jnp.zeros_like(l_sc)
            acc_sc[...] = jnp.zeros_like(acc_sc)

        def step(masked):
            s = _dot_nt(q_ref[...], k_ref[...])
            if masked:
                s = jnp.where(_chunk_mask(tq), s, NEG)
            m_prev = m_sc[...]
            m_new = jnp.maximum(m_prev, jnp.max(s, axis=-1, keepdims=True))
            a = jnp.exp(m_prev - m_new)
            p = jnp.exp(s - m_new)
            l_sc[...] = a * l_sc[...] + jnp.sum(p, axis=-1, keepdims=True)
            acc_sc[...] = a * acc_sc[...] + _dot(p.astype(BF16), v_ref[...])
            m_sc[...] = m_new

        @pl.when(ki < qi)
        def _():
            step(False)

        @pl.when(ki == qi)
        def _():
            step(True)
            lv = l_sc[...]
            o_ref[...] = (acc_sc[...] / lv).astype(BF16)
            lse_ref[...] = jnp.broadcast_to(m_sc[...] + jnp.log(lv), (tq, HP))

    qspec = pl.BlockSpec((tq, HP), lambda h, qi, ki: (qi, h))
    kspec = pl.BlockSpec((tq, HP), lambda h, qi, ki: (jnp.minimum(ki, qi), h))
    return pl.pallas_call(
        body, name=name, grid=(N_HEADS, nq, nq), in_specs=[qspec, kspec, kspec],
        out_specs=[qspec, pl.BlockSpec((None, tq, HP), lambda h, qi, ki: (h, qi, 0))],
        out_shape=[SDS(q.shape, BF16), SDS((N_HEADS, t, HP), F32)],
        scratch_shapes=[pltpu.VMEM((tq, 1), F32), pltpu.VMEM((tq, 1), F32), pltpu.VMEM((tq, HP), F32)],
        compiler_params=_cparams(("arbitrary",) * 3, VMEM_BIG))(q, k, v)


def _flash_probs(q_ref, k_ref, v_ref, do_ref, o_ref, lse_ref, masked, tq):
    s = _dot_nt(q_ref[...], k_ref[...])
    if masked:
        s = jnp.where(_chunk_mask(tq), s, NEG)
    p = jnp.exp(s - lse_ref[:, 0:1])
    dov = do_ref[...]
    delta = jnp.sum(dov.astype(F32) * o_ref[...].astype(F32), axis=-1, keepdims=True)
    ds = p * (_dot_nt(dov, v_ref[...]) - delta)
    return p, ds


def _flash_bwd_dq(q, k, v, do, o, lse, name):
    t = q.shape[0]
    tq = _rt(t)
    nq = t // tq

    def body(q_ref, k_ref, v_ref, do_ref, o_ref, lse_ref, dq_ref, acc_sc):
        qi, ki = pl.program_id(1), pl.program_id(2)

        @pl.when(ki == 0)
        def _():
            acc_sc[...] = jnp.zeros_like(acc_sc)

        def step(masked):
            _, ds = _flash_probs(q_ref, k_ref, v_ref, do_ref, o_ref, lse_ref, masked, tq)
            acc_sc[...] += _dot(ds.astype(BF16), k_ref[...])

        @pl.when(ki < qi)
        def _():
            step(False)

        @pl.when(ki == qi)
        def _():
            step(True)
            dq_ref[...] = acc_sc[...]

    qspec = pl.BlockSpec((tq, HP), lambda h, qi, ki: (qi, h))
    kspec = pl.BlockSpec((tq, HP), lambda h, qi, ki: (jnp.minimum(ki, qi), h))
    lspec = pl.BlockSpec((None, tq, HP), lambda h, qi, ki: (h, qi, 0))
    return pl.pallas_call(
        body, name=name, grid=(N_HEADS, nq, nq), in_specs=[qspec, kspec, kspec, qspec, qspec, lspec],
        out_specs=qspec, out_shape=SDS(q.shape, F32), scratch_shapes=[pltpu.VMEM((tq, HP), F32)],
        compiler_params=_cparams(("arbitrary",) * 3, VMEM_BIG))(q, k, v, do, o, lse)


def _flash_bwd_dkv(q, k, v, do, o, lse, name):
    t = q.shape[0]
    tq = _rt(t)
    nq = t // tq

    def body(q_ref, k_ref, v_ref, do_ref, o_ref, lse_ref, dk_ref, dv_ref, dk_sc, dv_sc):
        ki, qi = pl.program_id(1), pl.program_id(2)

        @pl.when(qi == 0)
        def _():
            dk_sc[...] = jnp.zeros_like(dk_sc)
            dv_sc[...] = jnp.zeros_like(dv_sc)

        def step(masked):
            p, ds = _flash_probs(q_ref, k_ref, v_ref, do_ref, o_ref, lse_ref, masked, tq)
            dv_sc[...] += _dot_tn(p.astype(BF16), do_ref[...])
            dk_sc[...] += _dot_tn(ds.astype(BF16), q_ref[...])

        @pl.when(qi == ki)
        def _():
            step(True)

        @pl.when(qi > ki)
        def _():
            step(False)

        @pl.when(qi == nq - 1)
        def _():
            dk_ref[...] = dk_sc[...]
            dv_ref[...] = dv_sc[...]

    qspec = pl.BlockSpec((tq, HP), lambda h, ki, qi: (jnp.maximum(qi, ki), h))
    kspec = pl.BlockSpec((tq, HP), lambda h, ki, qi: (ki, h))
    lspec = pl.BlockSpec((None, tq, HP), lambda h, ki, qi: (h, jnp.maximum(qi, ki), 0))
    return pl.pallas_call(
        body, name=name, grid=(N_HEADS, nq, nq), in_specs=[qspec, kspec, kspec, qspec, qspec, lspec],
        out_specs=[kspec, kspec], out_shape=[SDS(q.shape, F32), SDS(q.shape, F32)],
        scratch_shapes=[pltpu.VMEM((tq, HP), F32), pltpu.VMEM((tq, HP), F32)],
        compiler_params=_cparams(("arbitrary",) * 3, VMEM_BIG))(q, k, v, do, o, lse)


def _add_cast(gs, rs, cidx, name):
    n = len(gs)
    _, _, r, c = gs[0].shape
    tr = _rows_tile(r, c, 3 * n)

    def body(c_ref, *refs):
        for a in range(n):
            refs[2 * n + a][...] = (refs[a][...] + refs[n + a][...]).astype(BF16)

    gspec = pl.BlockSpec((None, None, tr, c), lambda j, i, c_ref: (c_ref[0], j, i, 0))
    rspec = pl.BlockSpec((None, tr, c), lambda j, i, c_ref: (j, i, 0))
    grid_spec = pltpu.PrefetchScalarGridSpec(num_scalar_prefetch=1, grid=(NCHIP, r // tr),
                                             in_specs=[gspec] * n + [rspec] * n, out_specs=[rspec] * n)
    return pl.pallas_call(body, name=name, grid_spec=grid_spec, out_shape=[SDS((NCHIP, r, c), BF16)] * n,
                          compiler_params=_cparams(("arbitrary", "arbitrary"), VMEM_BIG))(cidx, *gs, *rs)


def _sum_chips(ss, name):
    n = len(ss)
    _, r, c = ss[0].shape
    tr = _rows_tile(r, c, 3 * n)

    def body(*refs):
        for a in range(n):
            acc = refs[a][0].astype(F32)
            for kk in range(1, NCHIP):
                acc = acc + refs[a][kk].astype(F32)
            refs[n + a][...] = acc

    return pl.pallas_call(body, name=name, grid=(r // tr,),
                          in_specs=[pl.BlockSpec((NCHIP, tr, c), lambda i: (0, i, 0))] * n,
                          out_specs=[pl.BlockSpec((tr, c), lambda i: (i, 0))] * n, out_shape=[SDS((r, c), F32)] * n,
                          compiler_params=_cparams(("arbitrary",), VMEM_BIG))(*ss)


def _adamw(ws, gs, ms, vs, name):
    n = len(ws)
    r, c = ws[0].shape
    tr = _rows_tile(r, c, 7 * n)
    c1, c2 = 1.0 / (1.0 - B1 ** STEP), 1.0 / (1.0 - B2 ** STEP)

    def body(*refs):
        for a in range(n):
            w, g, m, v = (refs[kk * n + a][...] for kk in range(4))
            m2 = B1 * m + (1.0 - B1) * g
            v2 = B2 * v + (1.0 - B2) * (g * g)
            refs[4 * n + a][...] = -LR * ((m2 * c1) / (jnp.sqrt(v2 * c2) + EPS_ADAM) + WD * w)
            refs[5 * n + a][...] = m2
            refs[6 * n + a][...] = v2

    blk = pl.BlockSpec((tr, c), lambda i: (i, 0))
    outs = pl.pallas_call(body, name=name, grid=(r // tr,), in_specs=[blk] * (4 * n), out_specs=[blk] * (3 * n),
                          out_shape=[SDS((r, c), F32)] * (3 * n),
                          compiler_params=_cparams(("arbitrary",), VMEM_BIG))(*ws, *gs, *ms, *vs)
    return outs[:n], outs[n:2 * n], outs[2 * n:]


def _place():
    x, y, c = lax.axis_index("x"), lax.axis_index("y"), lax.axis_index("c")
    return x, y, c, [(1 - x, y), (x, 1 - y), (1 - x, 1 - y)]


def _rcopy(src, dst, ssem, rsem, k, dev):
    return pltpu.make_async_remote_copy(src_ref=src, dst_ref=dst, send_sem=ssem.at[k], recv_sem=rsem.at[k],
                                        device_id=dev, device_id_type=MESH)


def _all_gather(arrs, name):
    n = len(arrs)
    nl = arrs[0].shape[0]

    def body(*refs):
        ins, outs = refs[:n], refs[n:2 * n]
        ssem, rsem, lsem = refs[2 * n:]
        x, y, c, peers = _place()
        me = 2 * x + y
        sib = (x, y, 1 - c)
        local, sends = [], []
        for a in range(n):
            for l in range(nl):
                d = pltpu.make_async_copy(ins[a].at[l], outs[a].at[l, me], lsem.at[a * nl + l])
                d.start()
                local.append(d)
        for a in range(n):
            for kk, (px, py) in enumerate(peers):
                d = _rcopy(ins[a].at[c], outs[a].at[c, me], ssem, rsem, a * 6 + kk, (px, py, c))
                d.start()
                sends.append(d)
        for a in range(n):
            for kk, (px, py) in enumerate(peers):
                slot = outs[a].at[c, 2 * px + py]
                _rcopy(slot, slot, ssem, rsem, a * 6 + kk, (px, py, c)).wait_recv()
                d = _rcopy(slot, slot, ssem, rsem, a * 6 + 3 + kk, sib)
                d.start()
                sends.append(d)
        for a in range(n):
            for kk, (px, py) in enumerate(peers):
                slot = outs[a].at[1 - c, 2 * px + py]
                _rcopy(slot, slot, ssem, rsem, a * 6 + 3 + kk, sib).wait_recv()
        for d in sends:
            d.wait_send()
        for d in local:
            d.wait()

    return pl.pallas_call(
        body, name=name, in_specs=[ANY] * n, out_specs=[ANY] * n,
        out_shape=[SDS((nl, NCHIP) + a.shape[1:], a.dtype) for a in arrs],
        scratch_shapes=[pltpu.SemaphoreType.DMA((6 * n,)), pltpu.SemaphoreType.DMA((6 * n,)),
                        pltpu.SemaphoreType.DMA((n * nl,))])(*arrs)


def _rs_pair(gs, name):
    n = len(gs)

    def body(*refs):
        ins, outs = refs[:n], refs[n:2 * n]
        ssem, rsem = refs[2 * n:]
        x, y, c, _ = _place()
        sib = (x, y, 1 - c)
        sends = [_rcopy(ins[a].at[1 - c], outs[a], ssem, rsem, a, sib) for a in range(n)]
        for d in sends:
            d.start()
        for d in sends:
            d.wait_recv()
        for d in sends:
            d.wait_send()

    return pl.pallas_call(
        body, name=name, in_specs=[ANY] * n, out_specs=[ANY] * n,
        out_shape=[SDS(g.shape[1:], g.dtype) for g in gs],
        scratch_shapes=[pltpu.SemaphoreType.DMA((n,)), pltpu.SemaphoreType.DMA((n,))])(*gs)


def _rs_chips(qs, name):
    n = len(qs)

    def body(*refs):
        ins, outs = refs[:n], refs[n:2 * n]
        ssem, rsem, lsem = refs[2 * n:]
        x, y, c, peers = _place()
        me = 2 * x + y
        local, sends = [], []
        for a in range(n):
            d = pltpu.make_async_copy(ins[a].at[me], outs[a].at[me], lsem.at[a])
            d.start()
            local.append(d)
            for kk, (px, py) in enumerate(peers):
                d = _rcopy(ins[a].at[2 * px + py], outs[a].at[me], ssem, rsem, a * 3 + kk, (px, py, c))
                d.start()
                sends.append(d)
        for a in range(n):
            for kk, (px, py) in enumerate(peers):
                slot = outs[a].at[2 * px + py]
                _rcopy(slot, slot, ssem, rsem, a * 3 + kk, (px, py, c)).wait_recv()
        for d in sends:
            d.wait_send()
        for d in local:
            d.wait()

    return pl.pallas_call(
        body, name=name, in_specs=[ANY] * n, out_specs=[ANY] * n, out_shape=[SDS(q.shape, q.dtype) for q in qs],
        scratch_shapes=[pltpu.SemaphoreType.DMA((3 * n,)), pltpu.SemaphoreType.DMA((3 * n,)),
                        pltpu.SemaphoreType.DMA((n,))])(*qs)


def _rs_share(fs, nl, name):
    n = len(fs)

    def body(*refs):
        ins, outs = refs[:n], refs[n:2 * n]
        ssem, rsem, lsem = refs[2 * n:]
        x, y, c, _ = _place()
        sib = (x, y, 1 - c)
        local, sends = [], []
        for a in range(n):
            d = pltpu.make_async_copy(ins[a], outs[a].at[c], lsem.at[a])
            d.start()
            local.append(d)
            d = _rcopy(ins[a], outs[a].at[c], ssem, rsem, a, sib)
            d.start()
            sends.append(d)
        for a in range(n):
            slot = outs[a].at[1 - c]
            _rcopy(slot, slot, ssem, rsem, a, sib).wait_recv()
        for d in sends:
            d.wait_send()
        for d in local:
            d.wait()

    return pl.pallas_call(
        body, name=name, in_specs=[ANY] * n, out_specs=[ANY] * n,
        out_shape=[SDS((nl,) + f.shape, f.dtype) for f in fs],
        scratch_shapes=[pltpu.SemaphoreType.DMA((n,)), pltpu.SemaphoreType.DMA((n,)), pltpu.SemaphoreType.DMA((n,))])(*fs)


def _exchange(buf, reduce, name):
    r, w = buf.shape

    def body(in_ref, out_ref, recv_sc, ssem, rsem):
        x, y, c, _ = _place()
        me = 4 * x + 2 * y + c
        sends = []
        for rel in range(1, N_DEV):
            dev = (1 - x if rel & 4 else x, 1 - y if rel & 2 else y, 1 - c if rel & 1 else c)
            d = _rcopy(in_ref, recv_sc.at[me], ssem, rsem, rel - 1, dev)
            d.start()
            sends.append(d)
        recv_sc[me] = in_ref[...]
        for rel in range(1, N_DEV):
            px, py, pc = (1 - x if rel & 4 else x, 1 - y if rel & 2 else y, 1 - c if rel & 1 else c)
            slot = recv_sc.at[4 * px + 2 * py + pc]
            _rcopy(slot, slot, ssem, rsem, rel - 1, (px, py, pc)).wait_recv()
        for d in sends:
            d.wait_send()
        if reduce:
            acc = recv_sc[0]
            for dv in range(1, N_DEV):
                acc = acc + recv_sc[dv]
            out_ref[...] = acc
        else:
            out_ref[...] = recv_sc[...]

    vm = pl.BlockSpec(memory_space=pltpu.VMEM)
    return pl.pallas_call(
        body, name=name, in_specs=[vm], out_specs=vm,
        out_shape=SDS((r, w) if reduce else (N_DEV, r, w), F32),
        scratch_shapes=[pltpu.VMEM((N_DEV, r, w), F32), pltpu.SemaphoreType.DMA((N_DEV - 1,)),
                        pltpu.SemaphoreType.DMA((N_DEV - 1,))],
        compiler_params=_cparams(None, VMEM_BIG))(buf)


def _cols_full(g):
    nl, _, k, ns = g.shape
    return g.transpose(0, 2, 1, 3).reshape(nl, k, NCHIP * ns)


def _cols_shards(w):
    k, n = w.shape
    return w.reshape(k, NCHIP, n // NCHIP).transpose(1, 0, 2)


def _rows_pad(rows, width):
    out = jnp.concatenate([jnp.pad(a, ((0, 0), (0, width - a.shape[1]))) for a in rows], axis=0)
    return jnp.pad(out, ((0, -out.shape[0] % 8), (0, 0)))


def kernel(x, positions, ffn1_norm, ffn1_w_gate, ffn1_w_up, ffn1_w_down, mix_norm, w_in, gate_bias, conv_w, conv_b, conv_ln_g, conv_ln_b, w_conv_out, cq_norm, ckv_norm, w_uq, w_ukv, q_norm, k_norm, w_mla_out, w_out, ffn2_norm, ffn2_w_gate, ffn2_w_up, ffn2_w_down, loss_target, m_ffn1_norm, m_ffn1_w_gate, m_ffn1_w_up, m_ffn1_w_down, m_mix_norm, m_w_in, m_gate_bias, m_conv_w, m_conv_b, m_conv_ln_g, m_conv_ln_b, m_w_conv_out, m_cq_norm, m_ckv_norm, m_w_uq, m_w_ukv, m_q_norm, m_k_norm, m_w_mla_out, m_w_out, m_ffn2_norm, m_ffn2_w_gate, m_ffn2_w_up, m_ffn2_w_down, v_ffn1_norm, v_ffn1_w_gate, v_ffn1_w_up, v_ffn1_w_down, v_mix_norm, v_w_in, v_gate_bias, v_conv_w, v_conv_b, v_conv_ln_g, v_conv_ln_b, v_w_conv_out, v_cq_norm, v_ckv_norm, v_w_uq, v_w_ukv, v_q_norm, v_k_norm, v_w_mla_out, v_w_out, v_ffn2_norm, v_ffn2_w_gate, v_ffn2_w_up, v_ffn2_w_down):
    names = ["ffn1_norm", "ffn1_w_gate", "ffn1_w_up", "ffn1_w_down", "mix_norm", "w_in", "gate_bias", "conv_w",
             "conv_b", "conv_ln_g", "conv_ln_b", "w_conv_out", "cq_norm", "ckv_norm", "w_uq", "w_ukv", "q_norm",
             "k_norm", "w_mla_out", "w_out", "ffn2_norm", "ffn2_w_gate", "ffn2_w_up", "ffn2_w_down"]
    env = dict(locals())
    wts = {nm: env[nm] for nm in names}
    mom = {nm: env["m_" + nm] for nm in names}
    var = {nm: env["v_" + nm] for nm in names}

    t, d = x.shape[1], x.shape[2]
    nl = ffn1_norm.shape[0]
    cc = conv_b.shape[1]
    ql, kvl = cq_norm.shape[1], ckv_norm.shape[1]
    vw = N_HEADS * V_DIM
    hw = N_HEADS * HP
    lay = {"a": 2 * d, "cq": 2 * d + 2 * cc, "ckv": 2 * d + 2 * cc + ql, "kr": 2 * d + 2 * cc + ql + kvl,
           "wm": ql + kvl + 2 * HP}
    dp = lay["cq"] + lay["wm"]
    nat_g = 2 * cc + ql + kvl + ROPE
    assert lay["cq"] % lay["wm"] == 0 and lay["cq"] % ql == 0 and lay["ckv"] % kvl == 0 and lay["a"] % (2 * cc) == 0
    assert cc % HP == 0 and d % HP == 0 and t % (2 * CHUNK) == 0 and w_in.shape[2] * NCHIP == nat_g + 2 * d

    x0, target = x[0], loss_target[0]
    cidx = lax.axis_index("c").astype(jnp.int32).reshape(1)
    chip = 2 * lax.axis_index("x") + lax.axis_index("y")

    inv_freq = ROPE_THETA ** (-jnp.arange(0, ROPE, 2, dtype=F32) / ROPE)
    ang = positions[0].astype(F32)[:, None] * inv_freq
    cos, sin, z = jnp.cos(ang), jnp.sin(ang), jnp.zeros((t, ROPE // 2), F32)
    tabs = (jnp.concatenate([jnp.ones((t, NOPE), F32), cos, cos, jnp.zeros((t, HP - QK), F32)], axis=1),
            jnp.concatenate([jnp.zeros((t, NOPE), F32), -sin, z, jnp.zeros((t, HP - QK), F32)], axis=1),
            jnp.concatenate([jnp.zeros((t, NOPE), F32), z, sin, jnp.zeros((t, HP - QK), F32)], axis=1))

    big = ["ffn1_w_gate", "ffn1_w_up", "ffn1_w_down", "ffn2_w_gate", "ffn2_w_up", "ffn2_w_down",
           "w_in", "w_conv_out", "w_uq", "w_ukv", "w_mla_out", "w_out"]
    gathered = dict(zip(big, _all_gather([wts[nm].astype(BF16) for nm in big], "gather_weights")))
    w_in_nat = _cols_full(gathered["w_in"])
    w_in_k = jnp.concatenate([w_in_nat[..., nat_g:], w_in_nat[..., :nat_g],
                              jnp.zeros((nl, d, dp - nat_g - 2 * d), BF16)], axis=-1)
    w_co_k = _cols_full(gathered["w_conv_out"])
    w_uq_k = jnp.pad(_cols_full(gathered["w_uq"]).reshape(nl, ql, N_HEADS, QK),
                     ((0, 0), (0, 0), (0, 0), (0, HP - QK))).reshape(nl, ql, hw)
    w_ukv_k = _cols_full(gathered["w_ukv"])
    w_mo_k = jnp.pad(_cols_full(gathered["w_mla_out"]).reshape(nl, N_HEADS, V_DIM, d),
                     ((0, 0), (0, 0), (HP - V_DIM, 0), (0, 0))).reshape(nl, hw, d)
    w_out_k = gathered["w_out"].reshape(nl, d, d)
    qn_k = jnp.pad(q_norm, ((0, 0), (0, HP - QK)))
    kn_k = jnp.pad(k_norm, ((0, 0), (0, HP - QK)))
    small = _rows_pad([gate_bias.reshape(nl * 2, d // NCHIP), conv_w.reshape(nl * KW, cc // NCHIP)], d)
    everyone = _exchange(small, False, "gather_small")[0::2]
    gb_k = everyone[:, :nl * 2, :d // NCHIP].reshape(NCHIP, nl, 2, d // NCHIP).transpose(1, 2, 0, 3).reshape(nl, 2, d)
    cw = everyone[:, nl * 2:nl * 2 + nl * KW, :cc // NCHIP].reshape(NCHIP, nl, KW, cc // NCHIP)
    cw_k = jnp.pad(cw.transpose(1, 2, 0, 3).reshape(nl, KW, cc), ((0, 0), (0, CPAD - KW), (0, 0)))

    saved = []
    xc = x0
    for l in range(nl):
        x1, gt1, up1 = _ffn_fwd(xc, ffn1_norm, l, gathered["ffn1_w_gate"], gathered["ffn1_w_up"],
                                gathered["ffn1_w_down"], f"ffn1_fwd_{l}")
        hm = _rms_fwd(x1, mix_norm, l, f"mix_norm_{l}")
        proj = _mm(hm, w_in_k, "nn", name=f"proj_{l}", b_l=l)
        uc = _dwconv_fwd(proj, cw_k, conv_b, l, lay["a"], cc, f"dwconv_{l}")
        sc = _ln_silu(uc, conv_ln_g, conv_ln_b, l, f"conv_ln_{l}")
        yc = _mm(sc, w_co_k, "nn", name=f"conv_out_{l}", b_l=l)
        q, k, kv = _mla_pre_fwd(proj, lay, l, cq_norm, ckv_norm, qn_k, kn_k, w_uq_k, w_ukv_k, tabs, f"mla_pre_{l}")
        o, lse = _flash_fwd(q, k, kv, f"flash_{l}")
        ym = _mm(o, w_mo_k, "nn", name=f"mla_out_{l}", b_l=l)
        yv = _comb_fwd(proj, gb_k, l, yc, ym, f"combine_{l}")
        x2 = _mm(yv, w_out_k, "nn", name=f"mix_out_{l}", b_l=l, res=x1)
        x3, gt2, up2 = _ffn_fwd(x2, ffn2_norm, l, gathered["ffn2_w_gate"], gathered["ffn2_w_up"],
                                gathered["ffn2_w_down"], f"ffn2_fwd_{l}")
        saved.append(dict(x0=xc, x1=x1, gt1=gt1, up1=up1, hm=hm, proj=proj, uc=uc, sc=sc, yc=yc, q=q, k=k, kv=kv,
                          o=o, lse=lse, ym=ym, yv=yv, x2=x2, gt2=gt2, up2=up2))
        xc = x3
    dx, loss_row = _loss_grad(xc, target, "loss")

    gw = {nm: [None] * nl for nm in names}
    ffn_dw = {}
    for l in reversed(range(nl)):
        s = saved[l]

        def ffn_back(tag, xin, gains, gts, ups, dout):
            wg, wu, wd = (gathered[f"{tag}_w_{p}"] for p in ("gate", "up", "down"))
            dxi, dgain, dgt, dup, act, hb, dob = _ffn_bwd(xin, gains, l, wg, wu, wd, gts, ups, dout, f"{tag}_bwd_{l}")
            kw = dict(out_l=l, out_nl=nl)
            ffn_dw[f"{tag}_w_gate"] = _mm(hb, dgt, "tn", name=f"{tag}_dwg_{l}", b_g=True,
                                          out_prev=ffn_dw.get(f"{tag}_w_gate"), **kw)
            ffn_dw[f"{tag}_w_up"] = _mm(hb, dup, "tn", name=f"{tag}_dwu_{l}", b_g=True,
                                        out_prev=ffn_dw.get(f"{tag}_w_up"), **kw)
            ffn_dw[f"{tag}_w_down"] = _mm(act, dob, "tn", name=f"{tag}_dwd_{l}", a_g=True,
                                          out_prev=ffn_dw.get(f"{tag}_w_down"), **kw)
            gw[f"{tag}_norm"][l] = dgain
            return dxi

        dx2 = ffn_back("ffn2", s["x2"], ffn2_norm, s["gt2"], s["up2"], dx)
        dyv = _mm(dx2, w_out_k, "nt", name=f"mix_out_dy_{l}", b_l=l)
        gw["w_out"][l] = _mm(s["yv"], dx2, "tn", name=f"mix_out_dw_{l}")
        dyc, dym, dproj, dgb = _comb_bwd(s["proj"], gb_k, l, s["yc"], s["ym"], dyv, dp, f"combine_bwd_{l}")
        gw["gate_bias"][l] = dgb
        dsc = _mm(dyc, w_co_k, "nt", name=f"conv_out_ds_{l}", b_l=l)
        gw["w_conv_out"][l] = _mm(s["sc"], dyc, "tn", name=f"conv_out_dw_{l}")
        duc, gw["conv_ln_g"][l], gw["conv_ln_b"][l], gw["conv_b"][l] = _ln_silu_bwd(
            s["uc"], conv_ln_g, conv_ln_b, l, dsc, f"conv_ln_bwd_{l}")
        du, dcw = _dwconv_bwd(duc, s["proj"], cw_k, l, lay["a"], cc, f"dwconv_bwd_{l}")
        gw["conv_w"][l] = dcw[:KW]
        dproj = _glu_bwd(du, s["proj"], dproj, lay["a"], cc, f"glu_bwd_{l}")
        do = _mm(dym, w_mo_k, "nt", name=f"mla_out_do_{l}", b_l=l, out_dtype=BF16)
        gw["w_mla_out"][l] = _mm(s["o"], dym, "tn", name=f"mla_out_dw_{l}").reshape(N_HEADS, HP, d)[:, HP - V_DIM:].reshape(vw, d)
        dq = _flash_bwd_dq(s["q"], s["k"], s["kv"], do, s["o"], s["lse"], f"flash_dq_{l}")
        dk, dv = _flash_bwd_dkv(s["q"], s["k"], s["kv"], do, s["o"], s["lse"], f"flash_dkv_{l}")
        dproj, dqr, dkv, cqn, ckvn, gw["cq_norm"][l], gw["ckv_norm"][l], dqn, dkn = _mla_pre_bwd(
            s["proj"], lay, l, cq_norm, ckv_norm, qn_k, kn_k, w_uq_k, w_ukv_k, tabs, dq, dk, dv, dproj, f"mla_pre_bwd_{l}")
        gw["q_norm"][l], gw["k_norm"][l] = dqn[:, :QK], dkn[:, :QK]
        gw["w_uq"][l] = _mm(cqn, dqr, "tn", name=f"uq_dw_{l}").reshape(ql, N_HEADS, HP)[:, :, :QK].reshape(ql, N_HEADS * QK)
        gw["w_ukv"][l] = _mm(ckvn, dkv, "tn", name=f"ukv_dw_{l}")
        dhm = _mm(dproj, w_in_k, "nt", name=f"proj_dh_{l}", b_l=l)
        dwin = _mm(s["hm"], dproj, "tn", name=f"proj_dw_{l}")
        gw["w_in"][l] = jnp.concatenate([dwin[:, 2 * d:2 * d + nat_g], dwin[:, :2 * d]], axis=1)
        dx1, gw["mix_norm"][l] = _rms_back(s["x1"], mix_norm, l, dhm, dx2, f"mix_norm_bwd_{l}")
        dx = ffn_back("ffn1", s["x0"], ffn1_norm, s["gt1"], s["up1"], dx1)

    col_sh = ["w_in", "w_conv_out", "w_uq", "w_ukv", "w_mla_out"]
    stacks = {nm: ffn_dw[nm] for nm in big[:6]}
    for nm in col_sh:
        stacks[nm] = jnp.stack([_cols_shards(gw[nm][l]) for l in range(nl)])
    stacks["w_out"] = jnp.stack(gw["w_out"]).reshape(nl, NCHIP, d // NCHIP, d)
    same_shape = {}
    for nm in big:
        same_shape.setdefault(stacks[nm].shape, []).append(nm)
    groups = list(same_shape.values())
    sib_part = dict(zip(big, _rs_pair([stacks[nm] for nm in big], "rs_pair")))
    qb = {}
    for gi, grp in enumerate(groups):
        outs = _add_cast([stacks[nm] for nm in grp], [sib_part[nm] for nm in grp], cidx, f"rs_add_{gi}")
        qb.update(zip(grp, outs))
    parts = dict(zip(big, _rs_chips([qb[nm] for nm in big], "rs_chips")))
    red = {}
    for gi, grp in enumerate(groups):
        red.update(zip(grp, _sum_chips([parts[nm] for nm in grp], f"rs_sum_{gi}")))
    grads = dict(zip(big, _rs_share([red[nm] for nm in big], nl, "rs_share")))
    grads = {nm: g.reshape(wts[nm].shape) for nm, g in grads.items()}

    vec_names = ["ffn1_norm", "mix_norm", "ffn2_norm", "conv_b", "conv_ln_g", "conv_ln_b", "cq_norm", "ckv_norm",
                 "q_norm", "k_norm"]
    rows = [jnp.concatenate(gw[nm], axis=0) for nm in vec_names]
    rows += [jnp.concatenate(gw["gate_bias"], axis=0), jnp.concatenate(gw["conv_w"], axis=0), loss_row]
    total = _exchange(_rows_pad(rows, d), True, "allreduce_small")
    r0 = 0
    for nm in vec_names:
        grads[nm] = total[r0:r0 + nl, :wts[nm].shape[1]]
        r0 += nl
    gb_all = total[r0:r0 + 2 * nl, :d].reshape(nl, 2, NCHIP, d // NCHIP)
    r0 += 2 * nl
    cw_all = total[r0:r0 + KW * nl, :cc].reshape(nl, KW, NCHIP, cc // NCHIP)
    r0 += KW * nl
    grads["gate_bias"] = lax.dynamic_index_in_dim(gb_all, chip, axis=2, keepdims=False)
    grads["conv_w"] = lax.dynamic_index_in_dim(cw_all, chip, axis=2, keepdims=False)
    loss = total[r0, 0]

    delta, new_m, new_v = {}, {}, {}
    by_shape = {}
    for nm in names:
        shp = wts[nm].shape
        by_shape.setdefault((shp[0] * (shp[1] if len(shp) == 3 else 1), shp[-1]), []).append(nm)
    for gi, (shp2, grp) in enumerate(by_shape.items()):
        to2 = lambda a: a.reshape(shp2)
        ds_, ms_, vs_ = _adamw([to2(wts[nm]) for nm in grp], [to2(grads[nm]) for nm in grp],
                               [to2(mom[nm]) for nm in grp], [to2(var[nm]) for nm in grp], f"adamw_{gi}")
        for nm, dd, mm_, vv in zip(grp, ds_, ms_, vs_):
            delta[nm], new_m[nm], new_v[nm] = (a.reshape(wts[nm].shape) for a in (dd, mm_, vv))

    return (loss, dx[None], *[grads[nm] for nm in names], *[delta[nm] for nm in names],
            *[new_m[nm] for nm in names], *[new_v[nm] for nm in names])
```

```python
import functools

import jax
import jax.numpy as jnp
from jax import lax
from jax.experimental import pallas as pl
from jax.experimental.pallas import tpu as pltpu

F32, BF16 = jnp.float32, jnp.bfloat16
SDS = jax.ShapeDtypeStruct
MESH = pl.DeviceIdType.MESH
ANY = pl.BlockSpec(memory_space=pl.ANY)

NCHIP = 4
N_DEV = 8
N_HEADS, NOPE, ROPE, V_DIM = 8, 64, 32, 64
QK = NOPE + ROPE
HP = 128
CHUNK = 64
KW = 31
CPAD = 32
ROPE_THETA = 10000.0
EPS = 1e-6
LR, B1, B2, EPS_ADAM, WD, STEP = 0.001, 0.9, 0.999, 1e-08, 0.01, 10
VMEM_BIG = 48 << 20


def _cparams(sem=None, vmem=None):
    kw = {}
    if sem is not None:
        kw["dimension_semantics"] = sem
    if vmem is not None:
        kw["vmem_limit_bytes"] = vmem
    return pltpu.CompilerParams(**kw)


def _rt(t):
    return min(512, t // 2)


def _pick(n, cands):
    for c in cands:
        if c <= n and n % c == 0:
            return c
    return n


def _tile2(r, c, nblocks, row_mult):
    tr, tc = r, c
    while 2 * nblocks * tr * tc * 4 > VMEM_BIG // 2 and tr % (2 * row_mult) == 0:
        tr //= 2
    while 2 * nblocks * tr * tc * 4 > VMEM_BIG // 2 and tc % 256 == 0:
        tc //= 2
    return tr, tc


def _sig(v):
    return 1.0 / (1.0 + jnp.exp(-v))


def _rms_r(v):
    return lax.rsqrt(jnp.mean(v * v, axis=-1, keepdims=True) + EPS)


def _rms_bwd(xv, g, dy):
    r = _rms_r(xv)
    xh = xv * r
    dg = jnp.sum(dy * xh, axis=0, keepdims=True)
    dxh = dy * g
    dx = r * (dxh - xh * jnp.mean(dxh * xh, axis=-1, keepdims=True))
    return dx, dg


def _dot(a, b):
    return jnp.dot(a, b, preferred_element_type=F32)


def _dot_nt(a, b):
    return lax.dot_general(a, b, (((1,), (1,)), ((), ())), preferred_element_type=F32)


def _dot_tn(a, b):
    return lax.dot_general(a, b, (((0,), (0,)), ((), ())), preferred_element_type=F32)


def _mm(a, b, mode, *, name, out_dtype=F32, res=None, alpha=1.0, b_l=None, a_g=False, b_g=False,
        out_l=None, out_nl=None, out_prev=None):
    a2 = a.shape[1:] if a_g else a.shape
    b2 = b.shape[-2:]
    if mode == "nn":
        (m, k), (k2, n) = a2, b2
    elif mode == "nt":
        (m, k), (n, k2) = a2, b2
    else:
        (k, m), (k2, n) = a2, b2
    assert k == k2, (name, a.shape, b.shape)
    g = a.shape[0] if a_g else (b.shape[0] if b_g else 1)
    out_g = a_g or b_g
    tm = m if m <= 768 else _pick(m, (768, 512, 256, 128))
    tn = n if n <= 1280 else _pick(n, (1280, 1024, 768, 512, 256, 128))
    tk = k if k <= 1280 else _pick(k, (1280, 1024, 768, 512, 256, 128))
    nk = k // tk
    dn = {"nn": (((1,), (0,)), ((), ())), "nt": (((1,), (1,)), ((), ())), "tn": (((0,), (0,)), ((), ()))}[mode]

    def body(*refs):
        a_ref, b_ref = refs[0], refs[1]
        pos = 2
        res_ref = None
        if res is not None:
            res_ref = refs[pos]
            pos += 1
        if out_prev is not None:
            pos += 1
        o_ref = refs[pos]
        p = lax.dot_general(a_ref[...].astype(BF16), b_ref[...].astype(BF16), dn, preferred_element_type=F32)

        def fin(v):
            if alpha != 1.0:
                v = v * alpha
            if res_ref is not None:
                v = v + res_ref[...]
            o_ref[...] = v.astype(out_dtype)

        if nk == 1:
            fin(p)
        else:
            acc_ref = refs[pos + 1]
            kk = pl.program_id(3)

            @pl.when(kk == 0)
            def _():
                acc_ref[...] = p

            @pl.when(kk > 0)
            def _():
                acc_ref[...] += p

            @pl.when(kk == nk - 1)
            def _():
                fin(acc_ref[...])

    def spec(block, idx, lead):
        if lead is None:
            return pl.BlockSpec(block, idx)
        if lead == "g":
            return pl.BlockSpec((None,) + block, lambda gg, i, j, kk: (gg,) + idx(gg, i, j, kk))
        return pl.BlockSpec((None,) + block, lambda gg, i, j, kk: (lead,) + idx(gg, i, j, kk))

    if mode == "tn":
        a_spec = spec((tk, tm), lambda gg, i, j, kk: (kk, i), "g" if a_g else None)
    else:
        a_spec = spec((tm, tk), lambda gg, i, j, kk: (i, kk), "g" if a_g else None)
    b_lead = "g" if b_g else b_l
    if mode == "nt":
        b_spec = spec((tn, tk), lambda gg, i, j, kk: (j, kk), b_lead)
    else:
        b_spec = spec((tk, tn), lambda gg, i, j, kk: (kk, j), b_lead)
    in_specs, args = [a_spec, b_spec], [a, b]
    if res is not None:
        in_specs.append(pl.BlockSpec((tm, tn), lambda gg, i, j, kk: (i, j)))
        args.append(res)
    aliases = {}
    if out_prev is not None:
        aliases = {len(args): 0}
        in_specs.append(ANY)
        args.append(out_prev)
    oblock, oshape = (tm, tn), (m, n)
    oidx = lambda gg, i, j, kk: (i, j)
    if out_g:
        oblock, oshape = (None,) + oblock, (g,) + oshape
        oidx = lambda gg, i, j, kk: (gg, i, j)
    if out_l is not None:
        oblock, oshape = (None,) + oblock, (out_nl,) + oshape
        inner = oidx
        oidx = lambda gg, i, j, kk: (out_l,) + inner(gg, i, j, kk)
    return pl.pallas_call(
        body, name=name, grid=(g, m // tm, n // tn, nk), in_specs=in_specs,
        out_specs=pl.BlockSpec(oblock, oidx), out_shape=SDS(oshape, out_dtype),
        scratch_shapes=[pltpu.VMEM((tm, tn), F32)] if nk > 1 else [],
        input_output_aliases=aliases,
        compiler_params=_cparams(("arbitrary",) * 4, VMEM_BIG))(*args)


def _ffn_fwd(x, gains, l, wg, wu, wd, name):
    t, d = x.shape
    f = wg.shape[-2]
    tt = _rt(t)

    def body(x_ref, g_ref, wg_ref, wu_ref, wd_ref, o_ref, gt_ref, up_ref, h_sc, acc_sc):
        j = pl.program_id(1)

        @pl.when(j == 0)
        def _():
            xv = x_ref[...]
            h_sc[...] = (xv * _rms_r(xv) * g_ref[l:l + 1, :]).astype(BF16)
            acc_sc[...] = jnp.zeros_like(acc_sc)

        h = h_sc[...]
        gt = _dot_nt(h, wg_ref[...])
        up = _dot_nt(h, wu_ref[...])
        gt_ref[...] = gt.astype(BF16)
        up_ref[...] = up.astype(BF16)
        act = (gt * _sig(gt) * up).astype(BF16)
        acc_sc[...] += _dot(act, wd_ref[...])

        @pl.when(j == NCHIP - 1)
        def _():
            o_ref[...] = x_ref[...] + 0.5 * acc_sc[...]

    def wspec(k, n):
        return pl.BlockSpec((None, None, k, n), lambda i, j: (l, j, 0, 0))

    row = pl.BlockSpec((tt, d), lambda i, j: (i, 0))
    sh = pl.BlockSpec((None, tt, f), lambda i, j: (j, i, 0))
    return pl.pallas_call(
        body, name=name, grid=(t // tt, NCHIP),
        in_specs=[row, pl.BlockSpec(gains.shape, lambda i, j: (0, 0)), wspec(f, d), wspec(f, d), wspec(f, d)],
        out_specs=[row, sh, sh],
        out_shape=[SDS((t, d), F32), SDS((NCHIP, t, f), BF16), SDS((NCHIP, t, f), BF16)],
        scratch_shapes=[pltpu.VMEM((tt, d), BF16), pltpu.VMEM((tt, d), F32)],
        compiler_params=_cparams(("arbitrary", "arbitrary"), VMEM_BIG))(x, gains, wg, wu, wd)


def _ffn_bwd(x, gains, l, wg, wu, wd, gts, ups, dout, name):
    t, d = x.shape
    f = wg.shape[-2]
    tt = _rt(t)

    def body(x_ref, g_ref, wg_ref, wu_ref, wd_ref, gt_ref, up_ref, do_ref,
             dx_ref, dg_ref, dgt_ref, dup_ref, act_ref, h_ref, dob_ref, dh_sc):
        i, j = pl.program_id(0), pl.program_id(1)

        @pl.when(j == 0)
        def _():
            xv = x_ref[...]
            h_ref[...] = (xv * _rms_r(xv) * g_ref[l:l + 1, :]).astype(BF16)
            dob_ref[...] = (0.5 * do_ref[...]).astype(BF16)
            dh_sc[...] = jnp.zeros_like(dh_sc)

        @pl.when(jnp.logical_and(i == 0, j == 0))
        def _():
            dg_ref[...] = jnp.zeros_like(dg_ref)

        dact = _dot_nt(dob_ref[...], wd_ref[...])
        gt = gt_ref[...].astype(F32)
        up = up_ref[...].astype(F32)
        s = _sig(gt)
        sl = gt * s
        dup = (dact * sl).astype(BF16)
        dgt = (dact * up * (s * (1.0 + gt * (1.0 - s)))).astype(BF16)
        dgt_ref[...] = dgt
        dup_ref[...] = dup
        act_ref[...] = (sl * up).astype(BF16)
        dh_sc[...] += _dot(dgt, wg_ref[...]) + _dot(dup, wu_ref[...])

        @pl.when(j == NCHIP - 1)
        def _():
            dxn, dg = _rms_bwd(x_ref[...], g_ref[l:l + 1, :], dh_sc[...])
            dx_ref[...] = do_ref[...] + dxn
            dg_ref[...] += dg

    def wspec(k, n):
        return pl.BlockSpec((None, None, k, n), lambda i, j: (l, j, 0, 0))

    row = pl.BlockSpec((tt, d), lambda i, j: (i, 0))
    sh = pl.BlockSpec((None, tt, f), lambda i, j: (j, i, 0))
    return pl.pallas_call(
        body, name=name, grid=(t // tt, NCHIP),
        in_specs=[row, pl.BlockSpec(gains.shape, lambda i, j: (0, 0)), wspec(f, d), wspec(f, d), wspec(f, d), sh, sh, row],
        out_specs=[row, pl.BlockSpec((1, d), lambda i, j: (0, 0)), sh, sh, sh, row, row],
        out_shape=[SDS((t, d), F32), SDS((1, d), F32)] + [SDS((NCHIP, t, f), BF16)] * 3 + [SDS((t, d), BF16)] * 2,
        scratch_shapes=[pltpu.VMEM((tt, d), F32)],
        compiler_params=_cparams(("arbitrary", "arbitrary"), VMEM_BIG))(x, gains, wg, wu, wd, gts, ups, dout)


def _rms_fwd(x, gains, l, name):
    t, d = x.shape
    tt = _rt(t)

    def body(x_ref, g_ref, o_ref):
        xv = x_ref[...]
        o_ref[...] = (xv * _rms_r(xv) * g_ref[l:l + 1, :]).astype(BF16)

    row = pl.BlockSpec((tt, d), lambda i: (i, 0))
    return pl.pallas_call(body, name=name, grid=(t // tt,), in_specs=[row, pl.BlockSpec(gains.shape, lambda i: (0, 0))],
                          out_specs=row, out_shape=SDS((t, d), BF16), compiler_params=_cparams(("arbitrary",)))(x, gains)


def _rms_back(x, gains, l, dh, dres, name):
    t, d = x.shape
    tt = _rt(t)

    def body(x_ref, g_ref, dh_ref, dr_ref, dx_ref, dg_ref):
        @pl.when(pl.program_id(0) == 0)
        def _():
            dg_ref[...] = jnp.zeros_like(dg_ref)

        dxn, dg = _rms_bwd(x_ref[...], g_ref[l:l + 1, :], dh_ref[...])
        dx_ref[...] = dr_ref[...] + dxn
        dg_ref[...] += dg

    row = pl.BlockSpec((tt, d), lambda i: (i, 0))
    return pl.pallas_call(body, name=name, grid=(t // tt,),
                          in_specs=[row, pl.BlockSpec(gains.shape, lambda i: (0, 0)), row, row],
                          out_specs=[row, pl.BlockSpec((1, d), lambda i: (0, 0))],
                          out_shape=[SDS((t, d), F32), SDS((1, d), F32)],
                          compiler_params=_cparams(("arbitrary",)))(x, gains, dh, dres)


def _dwconv_fwd(proj, convw, convb, l, o_a, cc, name):
    t = proj.shape[0]
    r = min(256, t)
    nb = cc // HP

    def body(a_ref, gate_ref, w_ref, b_ref, uc_ref, u_sc):
        u_sc[0:CPAD, :] = jnp.zeros((CPAD, HP), F32)

        def fill(ci, carry):
            r0 = pl.multiple_of(ci * r, r)
            u_sc[pl.ds(CPAD + r0, r), :] = a_ref[pl.ds(r0, r), :] * _sig(gate_ref[pl.ds(r0, r), :])
            return carry

        lax.fori_loop(0, t // r, fill, 0)
        w = w_ref[l]
        bias = b_ref[l:l + 1, :]

        def conv(ci, carry):
            r0 = pl.multiple_of(ci * r, r)
            win = u_sc[pl.ds(r0, r + CPAD), :]
            acc = jnp.zeros((r, HP), F32) + bias
            for k in range(KW):
                off = CPAD - (KW - 1) + k
                acc = acc + win[off:off + r, :] * w[k:k + 1, :]
            uc_ref[pl.ds(r0, r), :] = acc
            return carry

        lax.fori_loop(0, t // r, conv, 0)

    col = lambda base: pl.BlockSpec((t, HP), lambda c: (0, base // HP + c))
    return pl.pallas_call(
        body, name=name, grid=(nb,),
        in_specs=[col(o_a), col(o_a + cc), pl.BlockSpec((convw.shape[0], CPAD, HP), lambda c: (0, 0, c)),
                  pl.BlockSpec((convb.shape[0], HP), lambda c: (0, c))],
        out_specs=pl.BlockSpec((t, HP), lambda c: (0, c)), out_shape=SDS((t, cc), F32),
        scratch_shapes=[pltpu.VMEM((t + CPAD, HP), F32)],
        compiler_params=_cparams(("arbitrary",), VMEM_BIG))(proj, proj, convw, convb)


def _dwconv_bwd(duc, proj, convw, l, o_a, cc, name):
    t = proj.shape[0]
    r = min(256, t)
    nb = cc // HP

    def body(d_ref, a_ref, gate_ref, w_ref, du_ref, dw_ref, u_sc, d_sc, dw_sc):
        u_sc[0:CPAD, :] = jnp.zeros((CPAD, HP), F32)
        d_sc[t:t + CPAD, :] = jnp.zeros((CPAD, HP), F32)
        dw_sc[...] = jnp.zeros_like(dw_sc)

        def fill(ci, carry):
            r0 = pl.multiple_of(ci * r, r)
            u_sc[pl.ds(CPAD + r0, r), :] = a_ref[pl.ds(r0, r), :] * _sig(gate_ref[pl.ds(r0, r), :])
            d_sc[pl.ds(r0, r), :] = d_ref[pl.ds(r0, r), :]
            return carry

        lax.fori_loop(0, t // r, fill, 0)
        w = w_ref[l]

        def conv(ci, carry):
            r0 = pl.multiple_of(ci * r, r)
            dwin = d_sc[pl.ds(r0, r + CPAD), :]
            uwin = u_sc[pl.ds(r0, r + CPAD), :]
            dcur = dwin[0:r, :]
            acc = jnp.zeros((r, HP), F32)
            for k in range(KW):
                acc = acc + dwin[KW - 1 - k:KW - 1 - k + r, :] * w[k:k + 1, :]
                off = CPAD - (KW - 1) + k
                part = (dcur * uwin[off:off + r, :]).reshape(r // 8, 8, HP).sum(axis=0)
                dw_sc[8 * k:8 * k + 8, :] += part
            du_ref[pl.ds(r0, r), :] = acc
            return carry

        lax.fori_loop(0, t // r, conv, 0)
        dw_ref[...] = jnp.zeros_like(dw_ref)
        for k in range(KW):
            dw_ref[k:k + 1, :] = jnp.sum(dw_sc[8 * k:8 * k + 8, :], axis=0, keepdims=True)

    col = lambda base: pl.BlockSpec((t, HP), lambda c: (0, base // HP + c))
    return pl.pallas_call(
        body, name=name, grid=(nb,),
        in_specs=[pl.BlockSpec((t, HP), lambda c: (0, c)), col(o_a), col(o_a + cc),
                  pl.BlockSpec((convw.shape[0], CPAD, HP), lambda c: (0, 0, c))],
        out_specs=[pl.BlockSpec((t, HP), lambda c: (0, c)), pl.BlockSpec((CPAD, HP), lambda c: (0, c))],
        out_shape=[SDS((t, cc), F32), SDS((CPAD, cc), F32)],
        scratch_shapes=[pltpu.VMEM((t + CPAD, HP), F32), pltpu.VMEM((t + CPAD, HP), F32), pltpu.VMEM((8 * CPAD, HP), F32)],
        compiler_params=_cparams(("arbitrary",), VMEM_BIG))(duc, proj, proj, convw)


def _ln_parts(uc, g, b):
    mu = jnp.mean(uc, axis=-1, keepdims=True)
    xc = uc - mu
    r = lax.rsqrt(jnp.mean(xc * xc, axis=-1, keepdims=True) + EPS)
    xh = xc * r
    return r, xh, xh * g + b


def _ln_silu(uc, ln_g, ln_b, l, name):
    t, cc = uc.shape
    tt = _rt(t)

    def body(u_ref, g_ref, b_ref, s_ref):
        _, _, yv = _ln_parts(u_ref[...], g_ref[l:l + 1, :], b_ref[l:l + 1, :])
        s_ref[...] = (yv * _sig(yv)).astype(BF16)

    row = pl.BlockSpec((tt, cc), lambda i: (i, 0))
    full = pl.BlockSpec(ln_g.shape, lambda i: (0, 0))
    return pl.pallas_call(body, name=name, grid=(t // tt,), in_specs=[row, full, full], out_specs=row,
                          out_shape=SDS((t, cc), BF16), compiler_params=_cparams(("arbitrary",)))(uc, ln_g, ln_b)


def _ln_silu_bwd(uc, ln_g, ln_b, l, ds, name):
    t, cc = uc.shape
    tt = _rt(t)

    def body(u_ref, g_ref, b_ref, ds_ref, du_ref, dg_ref, db_ref, dcb_ref):
        @pl.when(pl.program_id(0) == 0)
        def _():
            dg_ref[...] = jnp.zeros_like(dg_ref)
            db_ref[...] = jnp.zeros_like(db_ref)
            dcb_ref[...] = jnp.zeros_like(dcb_ref)

        g = g_ref[l:l + 1, :]
        r, xh, yv = _ln_parts(u_ref[...], g, b_ref[l:l + 1, :])
        sy = _sig(yv)
        dy = ds_ref[...] * (sy * (1.0 + yv * (1.0 - sy)))
        dg_ref[...] += jnp.sum(dy * xh, axis=0, keepdims=True)
        db_ref[...] += jnp.sum(dy, axis=0, keepdims=True)
        dxh = dy * g
        du = r * (dxh - jnp.mean(dxh, axis=-1, keepdims=True) - xh * jnp.mean(dxh * xh, axis=-1, keepdims=True))
        du_ref[...] = du
        dcb_ref[...] += jnp.sum(du, axis=0, keepdims=True)

    row = pl.BlockSpec((tt, cc), lambda i: (i, 0))
    full = pl.BlockSpec(ln_g.shape, lambda i: (0, 0))
    vec = pl.BlockSpec((1, cc), lambda i: (0, 0))
    return pl.pallas_call(body, name=name, grid=(t // tt,), in_specs=[row, full, full, row],
                          out_specs=[row, vec, vec, vec], out_shape=[SDS((t, cc), F32)] + [SDS((1, cc), F32)] * 3,
                          compiler_params=_cparams(("arbitrary",)))(uc, ln_g, ln_b, ds)


def _glu_bwd(du, proj, dproj, o_a, cc, name):
    t = du.shape[0]
    tt = _rt(t)

    def body(du_ref, a_ref, gate_ref, prev_ref, o_ref):
        sg = _sig(gate_ref[...])
        dv = du_ref[...]
        o_ref[:, 0:cc] = (dv * sg).astype(BF16)
        o_ref[:, cc:2 * cc] = (dv * a_ref[...] * sg * (1.0 - sg)).astype(BF16)

    return pl.pallas_call(
        body, name=name, grid=(t // tt,),
        in_specs=[pl.BlockSpec((tt, cc), lambda i: (i, 0)), pl.BlockSpec((tt, cc), lambda i: (i, o_a // cc)),
                  pl.BlockSpec((tt, cc), lambda i: (i, o_a // cc + 1)), ANY],
        out_specs=pl.BlockSpec((tt, 2 * cc), lambda i: (i, o_a // (2 * cc))),
        out_shape=SDS(dproj.shape, dproj.dtype), input_output_aliases={3: 0},
        compiler_params=_cparams(("arbitrary",)))(du, proj, proj, dproj)


def _rope(v, cs, s1, s2):
    return v * cs + pltpu.roll(v, HP - ROPE // 2, 1) * s1 + pltpu.roll(v, ROPE // 2, 1) * s2


def _rope_t(dv, cs, s1, s2):
    return dv * cs + pltpu.roll(dv * s1, ROPE // 2, 1) + pltpu.roll(dv * s2, HP - ROPE // 2, 1)


def _head_norm(v, g):
    r = lax.rsqrt(jnp.sum(v * v, axis=-1, keepdims=True) * (1.0 / QK) + EPS)
    return v * r * g, r


def _head_norm_bwd(v, r, g, dy):
    xh = v * r
    dg = jnp.sum(dy * xh, axis=0, keepdims=True)
    dxh = dy * g
    dx = r * (dxh - xh * (jnp.sum(dxh * xh, axis=-1, keepdims=True) * (1.0 / QK)))
    return dx, dg


def _mla_specs(t, tt, lay, l, cq_norm, ckv_norm, qn, kn, wuq, wukv):
    ql, kvl = cq_norm.shape[1], ckv_norm.shape[1]
    hw = N_HEADS * HP
    full = lambda a: pl.BlockSpec(a.shape, lambda i: (0,) * a.ndim)
    tab = pl.BlockSpec((tt, HP), lambda i: (i, 0))
    return [pl.BlockSpec((tt, ql), lambda i: (i, lay["cq"] // ql)),
            pl.BlockSpec((tt, kvl), lambda i: (i, lay["ckv"] // kvl)),
            pl.BlockSpec((tt, HP), lambda i: (i, lay["kr"] // HP)),
            full(cq_norm), full(ckv_norm), full(qn), full(kn),
            pl.BlockSpec((None, hw, ql), lambda i: (l, 0, 0)), pl.BlockSpec((None, kvl, hw), lambda i: (l, 0, 0)),
            tab, tab, tab]


def _mla_pre_fwd(proj, lay, l, cq_norm, ckv_norm, qn, kn, wuq, wukv, tabs, name):
    t = proj.shape[0]
    tt = _rt(t)
    hw = N_HEADS * HP

    def body(cq_ref, ckv_ref, kr_ref, gq_ref, gkv_ref, qn_ref, kn_ref, wq_ref, wkv_ref, c_ref, s1_ref, s2_ref,
             q_ref, k_ref, v_ref):
        cq = cq_ref[...]
        cqn = (cq * _rms_r(cq) * gq_ref[l:l + 1, :]).astype(BF16)
        ckv = ckv_ref[...]
        ckvn = (ckv * _rms_r(ckv) * gkv_ref[l:l + 1, :]).astype(BF16)
        qraw = _dot_nt(cqn, wq_ref[...])
        kv = _dot(ckvn, wkv_ref[...])
        v_ref[...] = kv.astype(BF16)
        lane = lax.broadcasted_iota(jnp.int32, (tt, HP), 1)
        krs = pltpu.roll(jnp.where(lane < ROPE, kr_ref[...], 0.0), NOPE, 1)
        cs, s1, s2 = c_ref[...], s1_ref[...], s2_ref[...]
        gq, gk = qn_ref[l:l + 1, :], kn_ref[l:l + 1, :]
        for h in range(N_HEADS):
            sl = slice(h * HP, (h + 1) * HP)
            qh, _ = _head_norm(qraw[:, sl], gq)
            q_ref[:, sl] = (_rope(qh, cs, s1, s2) * QK ** -0.5).astype(BF16)
            kh, _ = _head_norm(jnp.where(lane < NOPE, kv[:, sl], krs), gk)
            k_ref[:, sl] = _rope(kh, cs, s1, s2).astype(BF16)

    row = pl.BlockSpec((tt, hw), lambda i: (i, 0))
    return pl.pallas_call(
        body, name=name, grid=(t // tt,),
        in_specs=_mla_specs(t, tt, lay, l, cq_norm, ckv_norm, qn, kn, wuq, wukv),
        out_specs=[row, row, row], out_shape=[SDS((t, hw), BF16)] * 3,
        compiler_params=_cparams(("arbitrary",), VMEM_BIG))(
            proj, proj, proj, cq_norm, ckv_norm, qn, kn, wuq, wukv, *tabs)


def _mla_pre_bwd(proj, lay, l, cq_norm, ckv_norm, qn, kn, wuq, wukv, tabs, dq, dk, dv, dproj, name):
    t = proj.shape[0]
    tt = _rt(t)
    hw = N_HEADS * HP
    ql, kvl = cq_norm.shape[1], ckv_norm.shape[1]
    wm = lay["wm"]

    def body(cq_ref, ckv_ref, kr_ref, gq_ref, gkv_ref, qn_ref, kn_ref, wq_ref, wkv_ref, c_ref, s1_ref, s2_ref,
             dq_ref, dk_ref, dv_ref, prev_ref,
             o_ref, dqr_ref, dkv_ref, cqn_ref, ckvn_ref, dgq_ref, dgkv_ref, dqn_ref, dkn_ref):
        @pl.when(pl.program_id(0) == 0)
        def _():
            for ref in (dgq_ref, dgkv_ref, dqn_ref, dkn_ref):
                ref[...] = jnp.zeros_like(ref)

        cq = cq_ref[...]
        cqn = (cq * _rms_r(cq) * gq_ref[l:l + 1, :]).astype(BF16)
        ckv = ckv_ref[...]
        ckvn = (ckv * _rms_r(ckv) * gkv_ref[l:l + 1, :]).astype(BF16)
        cqn_ref[...] = cqn
        ckvn_ref[...] = ckvn
        qraw = _dot_nt(cqn, wq_ref[...])
        kv = _dot(ckvn, wkv_ref[...])
        lane = lax.broadcasted_iota(jnp.int32, (tt, HP), 1)
        krs = pltpu.roll(jnp.where(lane < ROPE, kr_ref[...], 0.0), NOPE, 1)
        cs, s1, s2 = c_ref[...], s1_ref[...], s2_ref[...]
        gq, gk = qn_ref[l:l + 1, :], kn_ref[l:l + 1, :]
        dkr = jnp.zeros((tt, HP), F32)
        dgq = jnp.zeros((1, HP), F32)
        dgk = jnp.zeros((1, HP), F32)
        for h in range(N_HEADS):
            sl = slice(h * HP, (h + 1) * HP)
            qh = qraw[:, sl]
            _, rq = _head_norm(qh, gq)
            dqh, dg = _head_norm_bwd(qh, rq, gq, _rope_t(dq_ref[:, sl] * QK ** -0.5, cs, s1, s2))
            dgq = dgq + dg
            dqr_ref[:, sl] = dqh.astype(BF16)
            kp = jnp.where(lane < NOPE, kv[:, sl], krs)
            _, rk = _head_norm(kp, gk)
            dkp, dg = _head_norm_bwd(kp, rk, gk, _rope_t(dk_ref[:, sl], cs, s1, s2))
            dgk = dgk + dg
            dkv_ref[:, sl] = (jnp.where(lane < NOPE, dkp, 0.0) + dv_ref[:, sl]).astype(BF16)
            dkr = dkr + dkp
        dqn_ref[...] += dgq
        dkn_ref[...] += dgk
        dkr = jnp.where(lane < ROPE, pltpu.roll(dkr, HP - NOPE, 1), 0.0)
        dcq, dg = _rms_bwd(cq, gq_ref[l:l + 1, :], _dot(dqr_ref[...], wq_ref[...]))
        dgq_ref[...] += dg
        dckv, dg = _rms_bwd(ckv, gkv_ref[l:l + 1, :], _dot_nt(dkv_ref[...], wkv_ref[...]))
        dgkv_ref[...] += dg
        o_ref[:, 0:ql] = dcq.astype(BF16)
        o_ref[:, ql:ql + kvl] = dckv.astype(BF16)
        o_ref[:, ql + kvl:ql + kvl + HP] = dkr.astype(BF16)
        o_ref[:, ql + kvl + HP:wm] = jnp.zeros((tt, wm - ql - kvl - HP), BF16)

    row = pl.BlockSpec((tt, hw), lambda i: (i, 0))
    vec = lambda n: pl.BlockSpec((1, n), lambda i: (0, 0))
    return pl.pallas_call(
        body, name=name, grid=(t // tt,),
        in_specs=_mla_specs(t, tt, lay, l, cq_norm, ckv_norm, qn, kn, wuq, wukv) + [row, row, row, ANY],
        out_specs=[pl.BlockSpec((tt, wm), lambda i: (i, lay["cq"] // wm)), row, row,
                   pl.BlockSpec((tt, ql), lambda i: (i, 0)), pl.BlockSpec((tt, kvl), lambda i: (i, 0)),
                   vec(ql), vec(kvl), vec(HP), vec(HP)],
        out_shape=[SDS(dproj.shape, dproj.dtype), SDS((t, hw), BF16), SDS((t, hw), BF16), SDS((t, ql), BF16),
                   SDS((t, kvl), BF16), SDS((1, ql), F32), SDS((1, kvl), F32), SDS((1, HP), F32), SDS((1, HP), F32)],
        input_output_aliases={15: 0},
        compiler_params=_cparams(("arbitrary",), VMEM_BIG))(
            proj, proj, proj, cq_norm, ckv_norm, qn, kn, wuq, wukv, *tabs, dq, dk, dv, dproj)


def _comb_fwd(proj, gate_bias, l, yc, ym, name):
    t, d = yc.shape
    tt = _rt(t)

    def body(p_ref, b_ref, yc_ref, ym_ref, y_ref):
        b = b_ref[l]
        g0 = _sig(p_ref[:, 0:d] + b[0:1, :])
        g1 = _sig(p_ref[:, d:2 * d] + b[1:2, :])
        y_ref[...] = (g0 * yc_ref[...] + g1 * ym_ref[...]).astype(BF16)

    row = pl.BlockSpec((tt, d), lambda i: (i, 0))
    return pl.pallas_call(
        body, name=name, grid=(t // tt,),
        in_specs=[pl.BlockSpec((tt, 2 * d), lambda i: (i, 0)), pl.BlockSpec(gate_bias.shape, lambda i: (0, 0, 0)), row, row],
        out_specs=row, out_shape=SDS((t, d), BF16), compiler_params=_cparams(("arbitrary",)))(proj, gate_bias, yc, ym)


def _comb_bwd(proj, gate_bias, l, yc, ym, dy, dp_cols, name):
    t, d = yc.shape
    tt = _rt(t)

    def body(p_ref, b_ref, yc_ref, ym_ref, dy_ref, dyc_ref, dym_ref, dp_ref, db_ref):
        @pl.when(pl.program_id(0) == 0)
        def _():
            db_ref[...] = jnp.zeros_like(db_ref)

        b = b_ref[l]
        dyv = dy_ref[...]
        g0 = _sig(p_ref[:, 0:d] + b[0:1, :])
        g1 = _sig(p_ref[:, d:2 * d] + b[1:2, :])
        dyc_ref[...] = (dyv * g0).astype(BF16)
        dym_ref[...] = (dyv * g1).astype(BF16)
        dg0 = dyv * yc_ref[...] * g0 * (1.0 - g0)
        dg1 = dyv * ym_ref[...] * g1 * (1.0 - g1)
        dp_ref[:, 0:d] = dg0.astype(BF16)
        dp_ref[:, d:2 * d] = dg1.astype(BF16)
        db_ref[0:1, :] += jnp.sum(dg0, axis=0, keepdims=True)
        db_ref[1:2, :] += jnp.sum(dg1, axis=0, keepdims=True)

    row = pl.BlockSpec((tt, d), lambda i: (i, 0))
    wide = pl.BlockSpec((tt, 2 * d), lambda i: (i, 0))
    return pl.pallas_call(
        body, name=name, grid=(t // tt,),
        in_specs=[wide, pl.BlockSpec(gate_bias.shape, lambda i: (0, 0, 0)), row, row, row],
        out_specs=[row, row, wide, pl.BlockSpec((2, d), lambda i: (0, 0))],
        out_shape=[SDS((t, d), BF16), SDS((t, d), BF16), SDS((t, dp_cols), BF16), SDS((2, d), F32)],
        compiler_params=_cparams(("arbitrary",)))(proj, gate_bias, yc, ym, dy)


def _loss_grad(y, target, name):
    t, d = y.shape
    tt = _rt(t)
    nt = t // tt

    def body(y_ref, t_ref, dy_ref, loss_ref, acc_sc):
        i = pl.program_id(0)

        @pl.when(i == 0)
        def _():
            acc_sc[...] = jnp.zeros_like(acc_sc)

        diff = y_ref[...] - t_ref[...]
        dy_ref[...] = diff * (1.0 / d)
        acc_sc[...] += jnp.sum(diff * diff, axis=0, keepdims=True)

        @pl.when(i == nt - 1)
        def _():
            tot = jnp.sum(acc_sc[...], axis=1, keepdims=True) * (0.5 / d)
            loss_ref[...] = jnp.broadcast_to(tot, (1, HP))

    row = pl.BlockSpec((tt, d), lambda i: (i, 0))
    return pl.pallas_call(body, name=name, grid=(nt,), in_specs=[row, row],
                          out_specs=[row, pl.BlockSpec((1, HP), lambda i: (0, 0))],
                          out_shape=[SDS((t, d), F32), SDS((1, HP), F32)],
                          scratch_shapes=[pltpu.VMEM((1, d), F32)],
                          compiler_params=_cparams(("arbitrary",)))(y, target)


def _chunk_mask(tq):
    rows = lax.broadcasted_iota(jnp.int32, (tq, tq), 0) // CHUNK
    cols = lax.broadcasted_iota(jnp.int32, (tq, tq), 1) // CHUNK
    return cols <= rows


NEG = -1e30


def _flash_fwd(q, k, v, name):
    t = q.shape[0]
    tq = _rt(t)
    nq = t // tq

    rep = tq // HP

    def body(q_ref, k_ref, v_ref, o_ref, lse_ref):
        qi = pl.program_id(1)
        qv = q_ref[...]

        def step(ki, carry, masked):
            m_prev, l_prev, acc = carry
            r0 = pl.multiple_of(ki * tq, tq)
            s = _dot_nt(qv, k_ref[pl.ds(r0, tq), :])
            if masked:
                s = jnp.where(_chunk_mask(tq), s, NEG)
            m_new = jnp.maximum(m_prev, jnp.max(s, axis=-1, keepdims=True))
            a = jnp.exp(m_prev - m_new)
            p = jnp.exp(s - jnp.tile(m_new, (1, rep)))
            l_new = a * l_prev + jnp.sum(p, axis=-1, keepdims=True)
            acc = a * acc + _dot(p.astype(BF16), v_ref[pl.ds(r0, tq), :])
            return m_new, l_new, acc

        init = (jnp.full((tq, HP), NEG, F32), jnp.zeros((tq, HP), F32), jnp.zeros((tq, HP), F32))
        carry = lax.fori_loop(0, qi, lambda ki, cr: step(ki, cr, False), init)
        m_fin, l_fin, acc = step(qi, carry, True)
        o_ref[...] = (acc / l_fin).astype(BF16)
        lse_ref[...] = m_fin + jnp.log(l_fin)

    qspec = pl.BlockSpec((tq, HP), lambda h, qi: (qi, h))
    head = pl.BlockSpec((t, HP), lambda h, qi: (0, h))
    return pl.pallas_call(
        body, name=name, grid=(N_HEADS, nq), in_specs=[qspec, head, head],
        out_specs=[qspec, pl.BlockSpec((None, tq, HP), lambda h, qi: (h, qi, 0))],
        out_shape=[SDS(q.shape, BF16), SDS((N_HEADS, t, HP), F32)],
        compiler_params=_cparams(("arbitrary",) * 2, VMEM_BIG))(q, k, v)


def _flash_bwd(q, k, v, do, o, lse, name):
    t = q.shape[0]
    tq = _rt(t)
    nq = t // tq
    rep = tq // HP

    def body(q_ref, k_ref, v_ref, do_ref, o_ref, lse_ref, dq_ref, dk_ref, dv_ref, delta_sc):
        def prep(qi, carry):
            r0 = pl.multiple_of(qi * tq, tq)
            rows = pl.ds(r0, tq)
            dlt = jnp.sum(do_ref[rows, :].astype(F32) * o_ref[rows, :].astype(F32), axis=-1, keepdims=True)
            delta_sc[rows, :] = jnp.broadcast_to(dlt, (tq, HP))
            dq_ref[rows, :] = jnp.zeros((tq, HP), F32)
            return carry

        lax.fori_loop(0, nq, prep, 0)

        def keys(ki, carry0):
            krows = pl.ds(pl.multiple_of(ki * tq, tq), tq)
            kt, vt = k_ref[krows, :], v_ref[krows, :]

            def step(qi, carry, masked):
                dk_acc, dv_acc = carry
                rows = pl.ds(pl.multiple_of(qi * tq, tq), tq)
                qt, dot_ = q_ref[rows, :], do_ref[rows, :]
                s = _dot_nt(qt, kt)
                if masked:
                    s = jnp.where(_chunk_mask(tq), s, NEG)
                p = jnp.exp(s - jnp.tile(lse_ref[rows, :], (1, rep)))
                ds = (p * (_dot_nt(dot_, vt) - jnp.tile(delta_sc[rows, :], (1, rep)))).astype(BF16)
                dv_acc = dv_acc + _dot_tn(p.astype(BF16), dot_)
                dk_acc = dk_acc + _dot_tn(ds, qt)
                dq_ref[rows, :] += _dot(ds, kt)
                return dk_acc, dv_acc

            zero = jnp.zeros((tq, HP), F32)
            carry = step(ki, (zero, zero), True)
            dk_acc, dv_acc = lax.fori_loop(ki + 1, nq, lambda qi, cr: step(qi, cr, False), carry)
            dk_ref[krows, :] = dk_acc
            dv_ref[krows, :] = dv_acc
            return carry0

        lax.fori_loop(0, nq, keys, 0)

    head = pl.BlockSpec((t, HP), lambda h: (0, h))
    return pl.pallas_call(
        body, name=name, grid=(N_HEADS,),
        in_specs=[head] * 5 + [pl.BlockSpec((None, t, HP), lambda h: (h, 0, 0))],
        out_specs=[head] * 3, out_shape=[SDS(q.shape, F32)] * 3,
        scratch_shapes=[pltpu.VMEM((t, HP), F32)],
        compiler_params=_cparams(("arbitrary",), VMEM_BIG))(q, k, v, do, o, lse)


def _add_cast(gs, rs, cidx, name):
    n = len(gs)
    _, _, r, c = gs[0].shape
    tr, tc = _tile2(r, c, 3 * n, 16)

    def body(c_ref, *refs):
        for a in range(n):
            refs[2 * n + a][...] = (refs[a][...] + refs[n + a][...]).astype(BF16)

    gspec = pl.BlockSpec((None, None, tr, tc), lambda j, i, k, c_ref: (c_ref[0], j, i, k))
    rspec = pl.BlockSpec((None, tr, tc), lambda j, i, k, c_ref: (j, i, k))
    grid_spec = pltpu.PrefetchScalarGridSpec(num_scalar_prefetch=1, grid=(NCHIP, r // tr, c // tc),
                                             in_specs=[gspec] * n + [rspec] * n, out_specs=[rspec] * n)
    return pl.pallas_call(body, name=name, grid_spec=grid_spec, out_shape=[SDS((NCHIP, r, c), BF16)] * n,
                          compiler_params=_cparams(("arbitrary",) * 3, VMEM_BIG))(cidx, *gs, *rs)


def _sum_chips(ss, qs, place, nl, name):
    n = len(ss)
    _, r, c = ss[0].shape
    tr, tc = _tile2(r, c, 3 * n, 16)

    def body(place_ref, *refs):
        for a in range(n):
            acc = refs[n + a][...].astype(F32)
            for kk in range(NCHIP - 1):
                acc = acc + refs[a][kk].astype(F32)
            refs[2 * n + a][...] = acc

    grid_spec = pltpu.PrefetchScalarGridSpec(
        num_scalar_prefetch=1, grid=(r // tr, c // tc),
        in_specs=[pl.BlockSpec((NCHIP - 1, tr, tc), lambda i, k, pr: (0, i, k))] * n
        + [pl.BlockSpec((None, tr, tc), lambda i, k, pr: (pr[1], i, k))] * n,
        out_specs=[pl.BlockSpec((None, tr, tc), lambda i, k, pr: (pr[0], i, k))] * n)
    return pl.pallas_call(body, name=name, grid_spec=grid_spec, out_shape=[SDS((nl, r, c), F32)] * n,
                          compiler_params=_cparams(("arbitrary",) * 2, VMEM_BIG))(place, *ss, *qs)


def _cast_place(ws, place, name):
    n = len(ws)
    nl, r, c = ws[0].shape
    tr, tc = _tile2(r, c, 2 * n, 16)

    def body(place_ref, *refs):
        for a in range(n):
            refs[n + a][...] = refs[a][...].astype(BF16)

    grid_spec = pltpu.PrefetchScalarGridSpec(
        num_scalar_prefetch=1, grid=(nl, r // tr, c // tc),
        in_specs=[pl.BlockSpec((None, tr, tc), lambda l, i, k, pr: (l, i, k))] * n,
        out_specs=[pl.BlockSpec((None, None, tr, tc), lambda l, i, k, pr: (l, pr[1], i, k))] * n)
    return pl.pallas_call(body, name=name, grid_spec=grid_spec, out_shape=[SDS((nl, NCHIP, r, c), BF16)] * n,
                          compiler_params=_cparams(("arbitrary",) * 3, VMEM_BIG))(place, *ws)


def _adamw(ws, gs, ms, vs, name):
    n = len(ws)
    r, c = ws[0].shape
    tr, tc = _tile2(r, c, 7 * n, 8)
    c1, c2 = 1.0 / (1.0 - B1 ** STEP), 1.0 / (1.0 - B2 ** STEP)

    def body(*refs):
        for a in range(n):
            w, g, m, v = (refs[kk * n + a][...] for kk in range(4))
            m2 = B1 * m + (1.0 - B1) * g
            v2 = B2 * v + (1.0 - B2) * (g * g)
            refs[4 * n + a][...] = -LR * ((m2 * c1) / (jnp.sqrt(v2 * c2) + EPS_ADAM) + WD * w)
            refs[5 * n + a][...] = m2
            refs[6 * n + a][...] = v2

    blk = pl.BlockSpec((tr, tc), lambda i, k: (i, k))
    outs = pl.pallas_call(body, name=name, grid=(r // tr, c // tc), in_specs=[blk] * (4 * n),
                          out_specs=[blk] * (3 * n), out_shape=[SDS((r, c), F32)] * (3 * n),
                          compiler_params=_cparams(("arbitrary",) * 2, VMEM_BIG))(*ws, *gs, *ms, *vs)
    return outs[:n], outs[n:2 * n], outs[2 * n:]


def _place():
    x, y, c = lax.axis_index("x"), lax.axis_index("y"), lax.axis_index("c")
    return x, y, c, [(1 - x, y), (x, 1 - y), (1 - x, 1 - y)]


def _rcopy(src, dst, ssem, rsem, k, dev):
    return pltpu.make_async_remote_copy(src_ref=src, dst_ref=dst, send_sem=ssem.at[k], recv_sem=rsem.at[k],
                                        device_id=dev, device_id_type=MESH)


def _all_gather(bufs, name):
    n = len(bufs)

    def body(*refs):
        outs = refs[n:2 * n]
        ssem, rsem = refs[2 * n:]
        x, y, c, peers = _place()
        me = 2 * x + y
        sib = (x, y, 1 - c)
        sends = []
        for a in range(n):
            for kk, (px, py) in enumerate(peers):
                mine = outs[a].at[c, me]
                d = _rcopy(mine, mine, ssem, rsem, a * 6 + kk, (px, py, c))
                d.start()
                sends.append(d)
        for a in range(n):
            for kk, (px, py) in enumerate(peers):
                slot = outs[a].at[c, 2 * px + py]
                _rcopy(slot, slot, ssem, rsem, a * 6 + kk, (px, py, c)).wait_recv()
                d = _rcopy(slot, slot, ssem, rsem, a * 6 + 3 + kk, sib)
                d.start()
                sends.append(d)
        for a in range(n):
            for kk, (px, py) in enumerate(peers):
                slot = outs[a].at[1 - c, 2 * px + py]
                _rcopy(slot, slot, ssem, rsem, a * 6 + 3 + kk, sib).wait_recv()
        for d in sends:
            d.wait_send()

    return pl.pallas_call(
        body, name=name, in_specs=[ANY] * n, out_specs=[ANY] * n,
        out_shape=[SDS(b.shape, b.dtype) for b in bufs], input_output_aliases={a: a for a in range(n)},
        scratch_shapes=[pltpu.SemaphoreType.DMA((6 * n,)), pltpu.SemaphoreType.DMA((6 * n,))])(*bufs)


def _rs_pair(gs, name):
    n = len(gs)

    def body(*refs):
        ins, outs = refs[:n], refs[n:2 * n]
        ssem, rsem = refs[2 * n:]
        x, y, c, _ = _place()
        sib = (x, y, 1 - c)
        sends = [_rcopy(ins[a].at[1 - c], outs[a], ssem, rsem, a, sib) for a in range(n)]
        for d in sends:
            d.start()
        for d in sends:
            d.wait_recv()
        for d in sends:
            d.wait_send()

    return pl.pallas_call(
        body, name=name, in_specs=[ANY] * n, out_specs=[ANY] * n,
        out_shape=[SDS(g.shape[1:], g.dtype) for g in gs],
        scratch_shapes=[pltpu.SemaphoreType.DMA((n,)), pltpu.SemaphoreType.DMA((n,))])(*gs)


def _rs_chips(qs, name):
    n = len(qs)

    def body(*refs):
        ins, outs = refs[:n], refs[n:2 * n]
        ssem, rsem = refs[2 * n:]
        x, y, c, peers = _place()
        sends = []
        for a in range(n):
            for kk, (px, py) in enumerate(peers):
                d = _rcopy(ins[a].at[2 * px + py], outs[a].at[kk], ssem, rsem, a * 3 + kk, (px, py, c))
                d.start()
                sends.append(d)
        for a in range(n):
            for kk, (px, py) in enumerate(peers):
                slot = outs[a].at[kk]
                _rcopy(slot, slot, ssem, rsem, a * 3 + kk, (px, py, c)).wait_recv()
        for d in sends:
            d.wait_send()

    return pl.pallas_call(
        body, name=name, in_specs=[ANY] * n, out_specs=[ANY] * n,
        out_shape=[SDS((NCHIP - 1,) + q.shape[1:], q.dtype) for q in qs],
        scratch_shapes=[pltpu.SemaphoreType.DMA((3 * n,)), pltpu.SemaphoreType.DMA((3 * n,))])(*qs)


def _rs_share(fs, name):
    n = len(fs)

    def body(*refs):
        outs = refs[n:2 * n]
        ssem, rsem = refs[2 * n:]
        x, y, c, _ = _place()
        sib = (x, y, 1 - c)
        sends = []
        for a in range(n):
            mine = outs[a].at[c]
            d = _rcopy(mine, mine, ssem, rsem, a, sib)
            d.start()
            sends.append(d)
        for a in range(n):
            slot = outs[a].at[1 - c]
            _rcopy(slot, slot, ssem, rsem, a, sib).wait_recv()
        for d in sends:
            d.wait_send()

    return pl.pallas_call(
        body, name=name, in_specs=[ANY] * n, out_specs=[ANY] * n,
        out_shape=[SDS(f.shape, f.dtype) for f in fs], input_output_aliases={a: a for a in range(n)},
        scratch_shapes=[pltpu.SemaphoreType.DMA((n,)), pltpu.SemaphoreType.DMA((n,))])(*fs)


def _exchange(buf, reduce, name):
    r, w = buf.shape

    def body(in_ref, out_ref, recv_sc, ssem, rsem):
        x, y, c, _ = _place()
        me = 4 * x + 2 * y + c
        sends = []
        for rel in range(1, N_DEV):
            dev = (1 - x if rel & 4 else x, 1 - y if rel & 2 else y, 1 - c if rel & 1 else c)
            d = _rcopy(in_ref, recv_sc.at[me], ssem, rsem, rel - 1, dev)
            d.start()
            sends.append(d)
        recv_sc[me] = in_ref[...]
        for rel in range(1, N_DEV):
            px, py, pc = (1 - x if rel & 4 else x, 1 - y if rel & 2 else y, 1 - c if rel & 1 else c)
            slot = recv_sc.at[4 * px + 2 * py + pc]
            _rcopy(slot, slot, ssem, rsem, rel - 1, (px, py, pc)).wait_recv()
        for d in sends:
            d.wait_send()
        if reduce:
            acc = recv_sc[0]
            for dv in range(1, N_DEV):
                acc = acc + recv_sc[dv]
            out_ref[...] = acc
        else:
            out_ref[...] = recv_sc[...]

    vm = pl.BlockSpec(memory_space=pltpu.VMEM)
    return pl.pallas_call(
        body, name=name, in_specs=[vm], out_specs=vm,
        out_shape=SDS((r, w) if reduce else (N_DEV, r, w), F32),
        scratch_shapes=[pltpu.VMEM((N_DEV, r, w), F32), pltpu.SemaphoreType.DMA((N_DEV - 1,)),
                        pltpu.SemaphoreType.DMA((N_DEV - 1,))],
        compiler_params=_cparams(None, VMEM_BIG))(buf)


def _cols_full(g):
    nl, _, k, ns = g.shape
    return g.transpose(0, 2, 1, 3).reshape(nl, k, NCHIP * ns)


def _cols_shards(w):
    k, n = w.shape
    return w.reshape(k, NCHIP, n // NCHIP).transpose(1, 0, 2)


def _rows_pad(rows, width):
    out = jnp.concatenate([jnp.pad(a, ((0, 0), (0, width - a.shape[1]))) for a in rows], axis=0)
    return jnp.pad(out, ((0, -out.shape[0] % 8), (0, 0)))


def kernel(x, positions, ffn1_norm, ffn1_w_gate, ffn1_w_up, ffn1_w_down, mix_norm, w_in, gate_bias, conv_w, conv_b, conv_ln_g, conv_ln_b, w_conv_out, cq_norm, ckv_norm, w_uq, w_ukv, q_norm, k_norm, w_mla_out, w_out, ffn2_norm, ffn2_w_gate, ffn2_w_up, ffn2_w_down, loss_target, m_ffn1_norm, m_ffn1_w_gate, m_ffn1_w_up, m_ffn1_w_down, m_mix_norm, m_w_in, m_gate_bias, m_conv_w, m_conv_b, m_conv_ln_g, m_conv_ln_b, m_w_conv_out, m_cq_norm, m_ckv_norm, m_w_uq, m_w_ukv, m_q_norm, m_k_norm, m_w_mla_out, m_w_out, m_ffn2_norm, m_ffn2_w_gate, m_ffn2_w_up, m_ffn2_w_down, v_ffn1_norm, v_ffn1_w_gate, v_ffn1_w_up, v_ffn1_w_down, v_mix_norm, v_w_in, v_gate_bias, v_conv_w, v_conv_b, v_conv_ln_g, v_conv_ln_b, v_w_conv_out, v_cq_norm, v_ckv_norm, v_w_uq, v_w_ukv, v_q_norm, v_k_norm, v_w_mla_out, v_w_out, v_ffn2_norm, v_ffn2_w_gate, v_ffn2_w_up, v_ffn2_w_down):
    names = ["ffn1_norm", "ffn1_w_gate", "ffn1_w_up", "ffn1_w_down", "mix_norm", "w_in", "gate_bias", "conv_w",
             "conv_b", "conv_ln_g", "conv_ln_b", "w_conv_out", "cq_norm", "ckv_norm", "w_uq", "w_ukv", "q_norm",
             "k_norm", "w_mla_out", "w_out", "ffn2_norm", "ffn2_w_gate", "ffn2_w_up", "ffn2_w_down"]
    env = dict(locals())
    turned = ("ffn1_w_gate", "ffn1_w_up", "ffn2_w_gate", "ffn2_w_up", "w_in", "w_uq")
    view = lambda nm, a: jnp.swapaxes(a, 1, 2) if nm in turned else a
    wts = {nm: view(nm, env[nm]) for nm in names}
    mom = {nm: view(nm, env["m_" + nm]) for nm in names}
    var = {nm: view(nm, env["v_" + nm]) for nm in names}

    t, d = x.shape[1], x.shape[2]
    nl = ffn1_norm.shape[0]
    cc = conv_b.shape[1]
    ql, kvl = cq_norm.shape[1], ckv_norm.shape[1]
    vw = N_HEADS * V_DIM
    hw = N_HEADS * HP
    lay = {"a": 2 * d, "cq": 2 * d + 2 * cc, "ckv": 2 * d + 2 * cc + ql, "kr": 2 * d + 2 * cc + ql + kvl,
           "wm": ql + kvl + 2 * HP}
    dp = lay["cq"] + lay["wm"]
    nat_g = 2 * cc + ql + kvl + ROPE
    assert lay["cq"] % lay["wm"] == 0 and lay["cq"] % ql == 0 and lay["ckv"] % kvl == 0 and lay["a"] % (2 * cc) == 0
    assert cc % HP == 0 and d % HP == 0 and t % (2 * CHUNK) == 0 and w_in.shape[2] * NCHIP == nat_g + 2 * d

    x0, target = x[0], loss_target[0]
    cidx = lax.axis_index("c").astype(jnp.int32).reshape(1)
    chip = 2 * lax.axis_index("x") + lax.axis_index("y")

    inv_freq = ROPE_THETA ** (-jnp.arange(0, ROPE, 2, dtype=F32) / ROPE)
    ang = positions[0].astype(F32)[:, None] * inv_freq
    cos, sin, z = jnp.cos(ang), jnp.sin(ang), jnp.zeros((t, ROPE // 2), F32)
    tabs = (jnp.concatenate([jnp.ones((t, NOPE), F32), cos, cos, jnp.zeros((t, HP - QK), F32)], axis=1),
            jnp.concatenate([jnp.zeros((t, NOPE), F32), -sin, z, jnp.zeros((t, HP - QK), F32)], axis=1),
            jnp.concatenate([jnp.zeros((t, NOPE), F32), z, sin, jnp.zeros((t, HP - QK), F32)], axis=1))

    big = ["ffn1_w_gate", "ffn1_w_up", "ffn1_w_down", "ffn2_w_gate", "ffn2_w_up", "ffn2_w_down",
           "w_in", "w_conv_out", "w_uq", "w_ukv", "w_mla_out", "w_out"]
    place = jnp.stack([lax.axis_index("c"), chip]).astype(jnp.int32)
    placed, like = {}, {}
    for nm in big:
        like.setdefault(wts[nm].shape, []).append(nm)
    for gi, grp in enumerate(like.values()):
        placed.update(zip(grp, _cast_place([wts[nm] for nm in grp], place, f"cast_place_{gi}")))
    gathered = dict(zip(big, _all_gather([placed[nm] for nm in big], "gather_weights")))
    w_in_nat = gathered["w_in"].reshape(nl, nat_g + 2 * d, d)
    w_in_k = jnp.concatenate([w_in_nat[:, nat_g:], w_in_nat[:, :nat_g],
                              jnp.zeros((nl, dp - nat_g - 2 * d, d), BF16)], axis=1)
    w_co_k = _cols_full(gathered["w_conv_out"])
    w_uq_k = jnp.pad(gathered["w_uq"].reshape(nl, N_HEADS, QK, ql),
                     ((0, 0), (0, 0), (0, HP - QK), (0, 0))).reshape(nl, hw, ql)
    w_ukv_k = _cols_full(gathered["w_ukv"])
    w_mo_k = jnp.pad(_cols_full(gathered["w_mla_out"]).reshape(nl, N_HEADS, V_DIM, d),
                     ((0, 0), (0, 0), (HP - V_DIM, 0), (0, 0))).reshape(nl, hw, d)
    w_out_k = gathered["w_out"].reshape(nl, d, d)
    qn_k = jnp.pad(q_norm, ((0, 0), (0, HP - QK)))
    kn_k = jnp.pad(k_norm, ((0, 0), (0, HP - QK)))
    small = _rows_pad([gate_bias.reshape(nl * 2, d // NCHIP), conv_w.reshape(nl * KW, cc // NCHIP)], d)
    everyone = _exchange(small, False, "gather_small")[0::2]
    gb_k = everyone[:, :nl * 2, :d // NCHIP].reshape(NCHIP, nl, 2, d // NCHIP).transpose(1, 2, 0, 3).reshape(nl, 2, d)
    cw = everyone[:, nl * 2:nl * 2 + nl * KW, :cc // NCHIP].reshape(NCHIP, nl, KW, cc // NCHIP)
    cw_k = jnp.pad(cw.transpose(1, 2, 0, 3).reshape(nl, KW, cc), ((0, 0), (0, CPAD - KW), (0, 0)))

    saved = []
    xc = x0
    for l in range(nl):
        x1, gt1, up1 = _ffn_fwd(xc, ffn1_norm, l, gathered["ffn1_w_gate"], gathered["ffn1_w_up"],
                                gathered["ffn1_w_down"], f"ffn1_fwd_{l}")
        hm = _rms_fwd(x1, mix_norm, l, f"mix_norm_{l}")
        proj = _mm(hm, w_in_k, "nt", name=f"proj_{l}", b_l=l)
        uc = _dwconv_fwd(proj, cw_k, conv_b, l, lay["a"], cc, f"dwconv_{l}")
        sc = _ln_silu(uc, conv_ln_g, conv_ln_b, l, f"conv_ln_{l}")
        yc = _mm(sc, w_co_k, "nn", name=f"conv_out_{l}", b_l=l)
        q, k, kv = _mla_pre_fwd(proj, lay, l, cq_norm, ckv_norm, qn_k, kn_k, w_uq_k, w_ukv_k, tabs, f"mla_pre_{l}")
        o, lse = _flash_fwd(q, k, kv, f"flash_{l}")
        ym = _mm(o, w_mo_k, "nn", name=f"mla_out_{l}", b_l=l)
        yv = _comb_fwd(proj, gb_k, l, yc, ym, f"combine_{l}")
        x2 = _mm(yv, w_out_k, "nn", name=f"mix_out_{l}", b_l=l, res=x1)
        x3, gt2, up2 = _ffn_fwd(x2, ffn2_norm, l, gathered["ffn2_w_gate"], gathered["ffn2_w_up"],
                                gathered["ffn2_w_down"], f"ffn2_fwd_{l}")
        saved.append(dict(x0=xc, x1=x1, gt1=gt1, up1=up1, hm=hm, proj=proj, uc=uc, sc=sc, yc=yc, q=q, k=k, kv=kv,
                          o=o, lse=lse, ym=ym, yv=yv, x2=x2, gt2=gt2, up2=up2))
        xc = x3
    dx, loss_row = _loss_grad(xc, target, "loss")

    gw = {nm: [None] * nl for nm in names}
    ffn_dw = {}
    for l in reversed(range(nl)):
        s = saved[l]

        def ffn_back(tag, xin, gains, gts, ups, dout):
            wg, wu, wd = (gathered[f"{tag}_w_{p}"] for p in ("gate", "up", "down"))
            dxi, dgain, dgt, dup, act, hb, dob = _ffn_bwd(xin, gains, l, wg, wu, wd, gts, ups, dout, f"{tag}_bwd_{l}")
            kw = dict(out_l=l, out_nl=nl)
            ffn_dw[f"{tag}_w_gate"] = _mm(dgt, hb, "tn", name=f"{tag}_dwg_{l}", a_g=True,
                                          out_prev=ffn_dw.get(f"{tag}_w_gate"), **kw)
            ffn_dw[f"{tag}_w_up"] = _mm(dup, hb, "tn", name=f"{tag}_dwu_{l}", a_g=True,
                                        out_prev=ffn_dw.get(f"{tag}_w_up"), **kw)
            ffn_dw[f"{tag}_w_down"] = _mm(act, dob, "tn", name=f"{tag}_dwd_{l}", a_g=True,
                                          out_prev=ffn_dw.get(f"{tag}_w_down"), **kw)
            gw[f"{tag}_norm"][l] = dgain
            return dxi

        dx2 = ffn_back("ffn2", s["x2"], ffn2_norm, s["gt2"], s["up2"], dx)
        dyv = _mm(dx2, w_out_k, "nt", name=f"mix_out_dy_{l}", b_l=l)
        gw["w_out"][l] = _mm(s["yv"], dx2, "tn", name=f"mix_out_dw_{l}")
        dyc, dym, dproj, dgb = _comb_bwd(s["proj"], gb_k, l, s["yc"], s["ym"], dyv, dp, f"combine_bwd_{l}")
        gw["gate_bias"][l] = dgb
        dsc = _mm(dyc, w_co_k, "nt", name=f"conv_out_ds_{l}", b_l=l)
        gw["w_conv_out"][l] = _mm(s["sc"], dyc, "tn", name=f"conv_out_dw_{l}")
        duc, gw["conv_ln_g"][l], gw["conv_ln_b"][l], gw["conv_b"][l] = _ln_silu_bwd(
            s["uc"], conv_ln_g, conv_ln_b, l, dsc, f"conv_ln_bwd_{l}")
        du, dcw = _dwconv_bwd(duc, s["proj"], cw_k, l, lay["a"], cc, f"dwconv_bwd_{l}")
        gw["conv_w"][l] = dcw[:KW]
        dproj = _glu_bwd(du, s["proj"], dproj, lay["a"], cc, f"glu_bwd_{l}")
        do = _mm(dym, w_mo_k, "nt", name=f"mla_out_do_{l}", b_l=l, out_dtype=BF16)
        gw["w_mla_out"][l] = _mm(s["o"], dym, "tn", name=f"mla_out_dw_{l}").reshape(N_HEADS, HP, d)[:, HP - V_DIM:].reshape(vw, d)
        dq, dk, dv = _flash_bwd(s["q"], s["k"], s["kv"], do, s["o"], s["lse"], f"flash_bwd_{l}")
        dproj, dqr, dkv, cqn, ckvn, gw["cq_norm"][l], gw["ckv_norm"][l], dqn, dkn = _mla_pre_bwd(
            s["proj"], lay, l, cq_norm, ckv_norm, qn_k, kn_k, w_uq_k, w_ukv_k, tabs, dq, dk, dv, dproj, f"mla_pre_bwd_{l}")
        gw["q_norm"][l], gw["k_norm"][l] = dqn[:, :QK], dkn[:, :QK]
        gw["w_uq"][l] = _mm(dqr, cqn, "tn", name=f"uq_dw_{l}").reshape(N_HEADS, HP, ql)[:, :QK].reshape(N_HEADS * QK, ql)
        gw["w_ukv"][l] = _mm(ckvn, dkv, "tn", name=f"ukv_dw_{l}")
        dhm = _mm(dproj, w_in_k, "nn", name=f"proj_dh_{l}", b_l=l)
        dwin = _mm(dproj, s["hm"], "tn", name=f"proj_dw_{l}")
        gw["w_in"][l] = jnp.concatenate([dwin[2 * d:2 * d + nat_g], dwin[:2 * d]], axis=0)
        dx1, gw["mix_norm"][l] = _rms_back(s["x1"], mix_norm, l, dhm, dx2, f"mix_norm_bwd_{l}")
        dx = ffn_back("ffn1", s["x0"], ffn1_norm, s["gt1"], s["up1"], dx1)

    stacks = {nm: ffn_dw[nm] for nm in big[:6]}
    for nm in ("w_conv_out", "w_ukv", "w_mla_out"):
        stacks[nm] = jnp.stack([_cols_shards(gw[nm][l]) for l in range(nl)])
    for nm in ("w_in", "w_uq", "w_out"):
        stacks[nm] = jnp.stack(gw[nm]).reshape((nl, NCHIP) + wts[nm].shape[1:])
    same_shape = {}
    for nm in big:
        same_shape.setdefault(stacks[nm].shape, []).append(nm)
    groups = list(same_shape.values())
    sib_part = dict(zip(big, _rs_pair([stacks[nm] for nm in big], "rs_pair")))
    qb = {}
    for gi, grp in enumerate(groups):
        outs = _add_cast([stacks[nm] for nm in grp], [sib_part[nm] for nm in grp], cidx, f"rs_add_{gi}")
        qb.update(zip(grp, outs))
    parts = dict(zip(big, _rs_chips([qb[nm] for nm in big], "rs_chips")))
    red = {}
    for gi, grp in enumerate(groups):
        red.update(zip(grp, _sum_chips([parts[nm] for nm in grp], [qb[nm] for nm in grp], place, nl, f"rs_sum_{gi}")))
    grads = dict(zip(big, _rs_share([red[nm] for nm in big], "rs_share")))
    grads = {nm: g.reshape(wts[nm].shape) for nm, g in grads.items()}

    vec_names = ["ffn1_norm", "mix_norm", "ffn2_norm", "conv_b", "conv_ln_g", "conv_ln_b", "cq_norm", "ckv_norm",
                 "q_norm", "k_norm"]
    rows = [jnp.concatenate(gw[nm], axis=0) for nm in vec_names]
    rows += [jnp.concatenate(gw["gate_bias"], axis=0), jnp.concatenate(gw["conv_w"], axis=0), loss_row]
    total = _exchange(_rows_pad(rows, d), True, "allreduce_small")
    r0 = 0
    for nm in vec_names:
        grads[nm] = total[r0:r0 + nl, :wts[nm].shape[1]]
        r0 += nl
    gb_all = total[r0:r0 + 2 * nl, :d].reshape(nl, 2, NCHIP, d // NCHIP)
    r0 += 2 * nl
    cw_all = total[r0:r0 + KW * nl, :cc].reshape(nl, KW, NCHIP, cc // NCHIP)
    r0 += KW * nl
    grads["gate_bias"] = lax.dynamic_index_in_dim(gb_all, chip, axis=2, keepdims=False)
    grads["conv_w"] = lax.dynamic_index_in_dim(cw_all, chip, axis=2, keepdims=False)
    loss = total[r0, 0]

    delta, new_m, new_v = {}, {}, {}
    by_shape = {}
    for nm in names:
        shp = wts[nm].shape
        by_shape.setdefault((shp[0] * (shp[1] if len(shp) == 3 else 1), shp[-1]), []).append(nm)
    for gi, (shp2, grp) in enumerate(by_shape.items()):
        to2 = lambda a: a.reshape(shp2)
        ds_, ms_, vs_ = _adamw([to2(wts[nm]) for nm in grp], [to2(grads[nm]) for nm in grp],
                               [to2(mom[nm]) for nm in grp], [to2(var[nm]) for nm in grp], f"adamw_{gi}")
        for nm, dd, mm_, vv in zip(grp, ds_, ms_, vs_):
            delta[nm], new_m[nm], new_v[nm] = (a.reshape(wts[nm].shape) for a in (dd, mm_, vv))

    return (loss, dx[None], *[view(nm, grads[nm]) for nm in names], *[view(nm, delta[nm]) for nm in names],
            *[view(nm, new_m[nm]) for nm in names], *[view(nm, new_v[nm]) for nm in names])
```

```python
import functools

import jax
import jax.numpy as jnp
from jax import lax
from jax.experimental import pallas as pl
from jax.experimental.pallas import tpu as pltpu

F32, BF16 = jnp.float32, jnp.bfloat16
SDS = jax.ShapeDtypeStruct
MESH = pl.DeviceIdType.MESH
ANY = pl.BlockSpec(memory_space=pl.ANY)

NCHIP = 4
N_DEV = 8
N_HEADS, NOPE, ROPE, V_DIM = 8, 64, 32, 64
QK = NOPE + ROPE
HP = 128
CHUNK = 64
KW = 31
CPAD = 32
ROPE_THETA = 10000.0
EPS = 1e-6
LR, B1, B2, EPS_ADAM, WD, STEP = 0.001, 0.9, 0.999, 1e-08, 0.01, 10
VMEM_BIG = 48 << 20


def _cparams(sem=None, vmem=None):
    kw = {}
    if sem is not None:
        kw["dimension_semantics"] = sem
    if vmem is not None:
        kw["vmem_limit_bytes"] = vmem
    return pltpu.CompilerParams(**kw)


def _rt(t):
    return min(512, t // 2)


def _pick(n, cands):
    for c in cands:
        if c <= n and n % c == 0:
            return c
    return n


def _tile2(r, c, nblocks, row_mult):
    tr, tc = r, c
    while 2 * nblocks * tr * tc * 4 > VMEM_BIG // 2 and tr % (2 * row_mult) == 0:
        tr //= 2
    while 2 * nblocks * tr * tc * 4 > VMEM_BIG // 2 and tc % 256 == 0:
        tc //= 2
    return tr, tc


def _sig(v):
    return 1.0 / (1.0 + jnp.exp(-v))


def _rms_r(v):
    return lax.rsqrt(jnp.mean(v * v, axis=-1, keepdims=True) + EPS)


def _rms_bwd(xv, g, dy):
    r = _rms_r(xv)
    xh = xv * r
    dg = jnp.sum(dy * xh, axis=0, keepdims=True)
    dxh = dy * g
    dx = r * (dxh - xh * jnp.mean(dxh * xh, axis=-1, keepdims=True))
    return dx, dg


def _dot(a, b):
    return jnp.dot(a, b, preferred_element_type=F32)


def _dot_nt(a, b):
    return lax.dot_general(a, b, (((1,), (1,)), ((), ())), preferred_element_type=F32)


def _dot_tn(a, b):
    return lax.dot_general(a, b, (((0,), (0,)), ((), ())), preferred_element_type=F32)


class _Comm:
    def __init__(self, ins, out_shapes, aliases, nsem, start, finish):
        self.ins, self.out_shapes, self.aliases, self.nsem = list(ins), list(out_shapes), dict(aliases), nsem
        self.start, self.finish = start, finish


def _call(body, comm, first, last, *, name, grid, in_specs, args, out_specs, out_shape, scratch, sem):
    in_specs, args, out_specs, out_shape, scratch = map(list, (in_specs, args, out_specs, out_shape, scratch))
    n_in, n_out, n_scr = len(args), len(out_shape), len(scratch)
    aliases = {}
    kern = body
    if comm is not None:
        nci, nco = len(comm.ins), len(comm.out_shapes)
        o0 = n_in + nci
        s0 = o0 + n_out + nco

        def kern(*refs):
            cins, couts = refs[n_in:o0], refs[o0 + n_out:s0]
            ssem, rsem = refs[s0 + n_scr:]
            pl.when(first())(lambda: comm.start(cins, couts, ssem, rsem))
            body(*refs[:n_in], *refs[o0:o0 + n_out], *refs[s0:s0 + n_scr])
            pl.when(last())(lambda: comm.finish(cins, couts, ssem, rsem))

        in_specs += [ANY] * nci
        args += comm.ins
        out_specs += [ANY] * nco
        out_shape += comm.out_shapes
        scratch += [pltpu.SemaphoreType.DMA((comm.nsem,))] * 2
        aliases = {n_in + i: n_out + o for i, o in comm.aliases.items()}
    outs = pl.pallas_call(kern, name=name, grid=grid, in_specs=in_specs, out_specs=out_specs, out_shape=out_shape,
                          scratch_shapes=scratch, input_output_aliases=aliases,
                          compiler_params=_cparams(sem, VMEM_BIG))(*args)
    return outs[:n_out], outs[n_out:]


def _comm_call(comm, name):
    once = lambda: pl.program_id(0) == 0
    return _call(lambda: None, comm, once, once, name=name, grid=(1,), in_specs=[], args=[], out_specs=[],
                 out_shape=[], scratch=[], sem=("arbitrary",))[1]


def _mm(a, b, mode, *, name, out_dtype=F32, res=None, a_g=False):
    a2 = a.shape[1:] if a_g else a.shape
    if mode == "nn":
        (m, k), (k2, n) = a2, b.shape
    elif mode == "nt":
        (m, k), (n, k2) = a2, b.shape
    else:
        (k, m), (k2, n) = a2, b.shape
    assert k == k2, (name, a.shape, b.shape)
    g = a.shape[0] if a_g else 1
    tm = m if m <= 768 else _pick(m, (768, 512, 256, 128))
    tn = n if n <= 1280 else _pick(n, (1280, 1024, 768, 512, 256, 128))
    tk = k if k <= 1280 else _pick(k, (1280, 1024, 768, 512, 256, 128))
    nk = k // tk
    dn = {"nn": (((1,), (0,)), ((), ())), "nt": (((1,), (1,)), ((), ())), "tn": (((0,), (0,)), ((), ()))}[mode]

    def body(*refs):
        a_ref, b_ref = refs[0], refs[1]
        res_ref = refs[2] if res is not None else None
        o_ref = refs[3] if res is not None else refs[2]
        p = lax.dot_general(a_ref[...].astype(BF16), b_ref[...].astype(BF16), dn, preferred_element_type=F32)

        def fin(v):
            if res_ref is not None:
                v = v + res_ref[...]
            o_ref[...] = v.astype(out_dtype)

        if nk == 1:
            fin(p)
        else:
            acc_ref = refs[-1]
            kk = pl.program_id(3)

            @pl.when(kk == 0)
            def _():
                acc_ref[...] = p

            @pl.when(kk > 0)
            def _():
                acc_ref[...] += p

            @pl.when(kk == nk - 1)
            def _():
                fin(acc_ref[...])

    a_block, a_idx = ((tk, tm), lambda gg, i, j, kk: (kk, i)) if mode == "tn" else ((tm, tk), lambda gg, i, j, kk: (i, kk))
    if a_g:
        a_spec = pl.BlockSpec((None,) + a_block, lambda gg, i, j, kk: (gg,) + a_idx(gg, i, j, kk))
    else:
        a_spec = pl.BlockSpec(a_block, a_idx)
    if mode == "nt":
        b_spec = pl.BlockSpec((tn, tk), lambda gg, i, j, kk: (j, kk))
    else:
        b_spec = pl.BlockSpec((tk, tn), lambda gg, i, j, kk: (kk, j))
    in_specs, args = [a_spec, b_spec], [a, b]
    if res is not None:
        in_specs.append(pl.BlockSpec((tm, tn), lambda gg, i, j, kk: (i, j)))
        args.append(res)
    if a_g:
        out_spec, oshape = pl.BlockSpec((None, tm, tn), lambda gg, i, j, kk: (gg, i, j)), (g, m, n)
    else:
        out_spec, oshape = pl.BlockSpec((tm, tn), lambda gg, i, j, kk: (i, j)), (m, n)
    return pl.pallas_call(
        body, name=name, grid=(g, m // tm, n // tn, nk), in_specs=in_specs, out_specs=out_spec,
        out_shape=SDS(oshape, out_dtype), scratch_shapes=[pltpu.VMEM((tm, tn), F32)] if nk > 1 else [],
        compiler_params=_cparams(("arbitrary",) * 4, VMEM_BIG))(*args)


def _ffn_fwd(x, gains, l, wg, wu, wd, name, comm=None):
    t, d = x.shape
    f = wg.shape[-2]
    tt = _rt(t)

    def body(x_ref, g_ref, wg_ref, wu_ref, wd_ref, o_ref, gt_ref, up_ref, h_sc, acc_sc):
        j = pl.program_id(1)

        @pl.when(j == 0)
        def _():
            xv = x_ref[...]
            h_sc[...] = (xv * _rms_r(xv) * g_ref[l:l + 1, :]).astype(BF16)
            acc_sc[...] = jnp.zeros_like(acc_sc)

        h = h_sc[...]
        gt = _dot_nt(h, wg_ref[...])
        up = _dot_nt(h, wu_ref[...])
        gt_ref[...] = gt.astype(BF16)
        up_ref[...] = up.astype(BF16)
        act = (gt * _sig(gt) * up).astype(BF16)
        acc_sc[...] += _dot(act, wd_ref[...])

        @pl.when(j == NCHIP - 1)
        def _():
            o_ref[...] = x_ref[...] + 0.5 * acc_sc[...]

    wspec = pl.BlockSpec((None, f, d), lambda i, j: (j, 0, 0))
    row = pl.BlockSpec((tt, d), lambda i, j: (i, 0))
    sh = pl.BlockSpec((None, tt, f), lambda i, j: (j, i, 0))
    ni = t // tt
    first = lambda: jnp.logical_and(pl.program_id(0) == 0, pl.program_id(1) == 0)
    last = lambda: jnp.logical_and(pl.program_id(0) == ni - 1, pl.program_id(1) == NCHIP - 1)
    outs, couts = _call(
        body, comm, first, last, name=name, grid=(ni, NCHIP),
        in_specs=[row, pl.BlockSpec(gains.shape, lambda i, j: (0, 0)), wspec, wspec, wspec], args=[x, gains, wg, wu, wd],
        out_specs=[row, sh, sh], out_shape=[SDS((t, d), F32), SDS((NCHIP, t, f), BF16), SDS((NCHIP, t, f), BF16)],
        scratch=[pltpu.VMEM((tt, d), BF16), pltpu.VMEM((tt, d), F32)], sem=("arbitrary", "arbitrary"))
    return (*outs, couts)


def _ffn_bwd(x, gains, l, wg, wu, wd, gts, ups, dout, name, comm=None):
    t, d = x.shape
    f = wg.shape[-2]
    tt = _rt(t)

    def body(x_ref, g_ref, wg_ref, wu_ref, wd_ref, gt_ref, up_ref, do_ref,
             dx_ref, dg_ref, dgt_ref, dup_ref, act_ref, h_ref, dob_ref, dh_sc):
        i, j = pl.program_id(0), pl.program_id(1)

        @pl.when(j == 0)
        def _():
            xv = x_ref[...]
            h_ref[...] = (xv * _rms_r(xv) * g_ref[l:l + 1, :]).astype(BF16)
            dob_ref[...] = (0.5 * do_ref[...]).astype(BF16)
            dh_sc[...] = jnp.zeros_like(dh_sc)

        @pl.when(jnp.logical_and(i == 0, j == 0))
        def _():
            dg_ref[...] = jnp.zeros_like(dg_ref)

        dact = _dot_nt(dob_ref[...], wd_ref[...])
        gt = gt_ref[...].astype(F32)
        up = up_ref[...].astype(F32)
        s = _sig(gt)
        sl = gt * s
        dup = (dact * sl).astype(BF16)
        dgt = (dact * up * (s * (1.0 + gt * (1.0 - s)))).astype(BF16)
        dgt_ref[...] = dgt
        dup_ref[...] = dup
        act_ref[...] = (sl * up).astype(BF16)
        dh_sc[...] += _dot(dgt, wg_ref[...]) + _dot(dup, wu_ref[...])

        @pl.when(j == NCHIP - 1)
        def _():
            dxn, dg = _rms_bwd(x_ref[...], g_ref[l:l + 1, :], dh_sc[...])
            dx_ref[...] = do_ref[...] + dxn
            dg_ref[...] += dg

    wspec = pl.BlockSpec((None, f, d), lambda i, j: (j, 0, 0))
    row = pl.BlockSpec((tt, d), lambda i, j: (i, 0))
    sh = pl.BlockSpec((None, tt, f), lambda i, j: (j, i, 0))
    ni = t // tt
    first = lambda: jnp.logical_and(pl.program_id(0) == 0, pl.program_id(1) == 0)
    last = lambda: jnp.logical_and(pl.program_id(0) == ni - 1, pl.program_id(1) == NCHIP - 1)
    outs, couts = _call(
        body, comm, first, last, name=name, grid=(ni, NCHIP),
        in_specs=[row, pl.BlockSpec(gains.shape, lambda i, j: (0, 0)), wspec, wspec, wspec, sh, sh, row],
        args=[x, gains, wg, wu, wd, gts, ups, dout],
        out_specs=[row, pl.BlockSpec((1, d), lambda i, j: (0, 0)), sh, sh, sh, row, row],
        out_shape=[SDS((t, d), F32), SDS((1, d), F32)] + [SDS((NCHIP, t, f), BF16)] * 3 + [SDS((t, d), BF16)] * 2,
        scratch=[pltpu.VMEM((tt, d), F32)], sem=("arbitrary", "arbitrary"))
    return (*outs, couts)


def _rms_fwd(x, gains, l, name):
    t, d = x.shape
    tt = _rt(t)

    def body(x_ref, g_ref, o_ref):
        xv = x_ref[...]
        o_ref[...] = (xv * _rms_r(xv) * g_ref[l:l + 1, :]).astype(BF16)

    row = pl.BlockSpec((tt, d), lambda i: (i, 0))
    return pl.pallas_call(body, name=name, grid=(t // tt,), in_specs=[row, pl.BlockSpec(gains.shape, lambda i: (0, 0))],
                          out_specs=row, out_shape=SDS((t, d), BF16), compiler_params=_cparams(("arbitrary",)))(x, gains)


def _rms_back(x, gains, l, dh, dres, name):
    t, d = x.shape
    tt = _rt(t)

    def body(x_ref, g_ref, dh_ref, dr_ref, dx_ref, dg_ref):
        @pl.when(pl.program_id(0) == 0)
        def _():
            dg_ref[...] = jnp.zeros_like(dg_ref)

        dxn, dg = _rms_bwd(x_ref[...], g_ref[l:l + 1, :], dh_ref[...])
        dx_ref[...] = dr_ref[...] + dxn
        dg_ref[...] += dg

    row = pl.BlockSpec((tt, d), lambda i: (i, 0))
    return pl.pallas_call(body, name=name, grid=(t // tt,),
                          in_specs=[row, pl.BlockSpec(gains.shape, lambda i: (0, 0)), row, row],
                          out_specs=[row, pl.BlockSpec((1, d), lambda i: (0, 0))],
                          out_shape=[SDS((t, d), F32), SDS((1, d), F32)],
                          compiler_params=_cparams(("arbitrary",)))(x, gains, dh, dres)


def _dwconv_fwd(proj, convw, convb, l, o_a, cc, name):
    t = proj.shape[0]
    r = min(256, t)
    nb = cc // HP

    def body(a_ref, gate_ref, w_ref, b_ref, uc_ref, u_sc):
        u_sc[0:CPAD, :] = jnp.zeros((CPAD, HP), F32)

        def fill(ci, carry):
            r0 = pl.multiple_of(ci * r, r)
            u_sc[pl.ds(CPAD + r0, r), :] = a_ref[pl.ds(r0, r), :] * _sig(gate_ref[pl.ds(r0, r), :])
            return carry

        lax.fori_loop(0, t // r, fill, 0)
        w = w_ref[l]
        bias = b_ref[l:l + 1, :]

        def conv(ci, carry):
            r0 = pl.multiple_of(ci * r, r)
            win = u_sc[pl.ds(r0, r + CPAD), :]
            acc = jnp.zeros((r, HP), F32) + bias
            for k in range(KW):
                off = CPAD - (KW - 1) + k
                acc = acc + win[off:off + r, :] * w[k:k + 1, :]
            uc_ref[pl.ds(r0, r), :] = acc
            return carry

        lax.fori_loop(0, t // r, conv, 0)

    col = lambda base: pl.BlockSpec((t, HP), lambda c: (0, base // HP + c))
    return pl.pallas_call(
        body, name=name, grid=(nb,),
        in_specs=[col(o_a), col(o_a + cc), pl.BlockSpec((convw.shape[0], CPAD, HP), lambda c: (0, 0, c)),
                  pl.BlockSpec((convb.shape[0], HP), lambda c: (0, c))],
        out_specs=pl.BlockSpec((t, HP), lambda c: (0, c)), out_shape=SDS((t, cc), F32),
        scratch_shapes=[pltpu.VMEM((t + CPAD, HP), F32)],
        compiler_params=_cparams(("arbitrary",), VMEM_BIG))(proj, proj, convw, convb)


def _dwconv_bwd(duc, proj, convw, l, o_a, cc, name):
    t = proj.shape[0]
    r = min(256, t)
    nb = cc // HP

    def body(d_ref, a_ref, gate_ref, w_ref, du_ref, dw_ref, u_sc, d_sc, dw_sc):
        u_sc[0:CPAD, :] = jnp.zeros((CPAD, HP), F32)
        d_sc[t:t + CPAD, :] = jnp.zeros((CPAD, HP), F32)
        dw_sc[...] = jnp.zeros_like(dw_sc)

        def fill(ci, carry):
            r0 = pl.multiple_of(ci * r, r)
            u_sc[pl.ds(CPAD + r0, r), :] = a_ref[pl.ds(r0, r), :] * _sig(gate_ref[pl.ds(r0, r), :])
            d_sc[pl.ds(r0, r), :] = d_ref[pl.ds(r0, r), :]
            return carry

        lax.fori_loop(0, t // r, fill, 0)
        w = w_ref[l]

        def conv(ci, carry):
            r0 = pl.multiple_of(ci * r, r)
            dwin = d_sc[pl.ds(r0, r + CPAD), :]
            uwin = u_sc[pl.ds(r0, r + CPAD), :]
            dcur = dwin[0:r, :]
            acc = jnp.zeros((r, HP), F32)
            for k in range(KW):
                acc = acc + dwin[KW - 1 - k:KW - 1 - k + r, :] * w[k:k + 1, :]
                off = CPAD - (KW - 1) + k
                part = (dcur * uwin[off:off + r, :]).reshape(r // 8, 8, HP).sum(axis=0)
                dw_sc[8 * k:8 * k + 8, :] += part
            du_ref[pl.ds(r0, r), :] = acc
            return carry

        lax.fori_loop(0, t // r, conv, 0)
        dw_ref[...] = jnp.zeros_like(dw_ref)
        for k in range(KW):
            dw_ref[k:k + 1, :] = jnp.sum(dw_sc[8 * k:8 * k + 8, :], axis=0, keepdims=True)

    col = lambda base: pl.BlockSpec((t, HP), lambda c: (0, base // HP + c))
    return pl.pallas_call(
        body, name=name, grid=(nb,),
        in_specs=[pl.BlockSpec((t, HP), lambda c: (0, c)), col(o_a), col(o_a + cc),
                  pl.BlockSpec((convw.shape[0], CPAD, HP), lambda c: (0, 0, c))],
        out_specs=[pl.BlockSpec((t, HP), lambda c: (0, c)), pl.BlockSpec((CPAD, HP), lambda c: (0, c))],
        out_shape=[SDS((t, cc), F32), SDS((CPAD, cc), F32)],
        scratch_shapes=[pltpu.VMEM((t + CPAD, HP), F32), pltpu.VMEM((t + CPAD, HP), F32), pltpu.VMEM((8 * CPAD, HP), F32)],
        compiler_params=_cparams(("arbitrary",), VMEM_BIG))(duc, proj, proj, convw)


def _ln_parts(uc, g, b):
    mu = jnp.mean(uc, axis=-1, keepdims=True)
    xc = uc - mu
    r = lax.rsqrt(jnp.mean(xc * xc, axis=-1, keepdims=True) + EPS)
    xh = xc * r
    return r, xh, xh * g + b


def _ln_silu(uc, ln_g, ln_b, l, name):
    t, cc = uc.shape
    tt = _rt(t)

    def body(u_ref, g_ref, b_ref, s_ref):
        _, _, yv = _ln_parts(u_ref[...], g_ref[l:l + 1, :], b_ref[l:l + 1, :])
        s_ref[...] = (yv * _sig(yv)).astype(BF16)

    row = pl.BlockSpec((tt, cc), lambda i: (i, 0))
    full = pl.BlockSpec(ln_g.shape, lambda i: (0, 0))
    return pl.pallas_call(body, name=name, grid=(t // tt,), in_specs=[row, full, full], out_specs=row,
                          out_shape=SDS((t, cc), BF16), compiler_params=_cparams(("arbitrary",)))(uc, ln_g, ln_b)


def _ln_silu_bwd(uc, ln_g, ln_b, l, ds, name):
    t, cc = uc.shape
    tt = _rt(t)

    def body(u_ref, g_ref, b_ref, ds_ref, du_ref, dg_ref, db_ref, dcb_ref):
        @pl.when(pl.program_id(0) == 0)
        def _():
            dg_ref[...] = jnp.zeros_like(dg_ref)
            db_ref[...] = jnp.zeros_like(db_ref)
            dcb_ref[...] = jnp.zeros_like(dcb_ref)

        g = g_ref[l:l + 1, :]
        r, xh, yv = _ln_parts(u_ref[...], g, b_ref[l:l + 1, :])
        sy = _sig(yv)
        dy = ds_ref[...] * (sy * (1.0 + yv * (1.0 - sy)))
        dg_ref[...] += jnp.sum(dy * xh, axis=0, keepdims=True)
        db_ref[...] += jnp.sum(dy, axis=0, keepdims=True)
        dxh = dy * g
        du = r * (dxh - jnp.mean(dxh, axis=-1, keepdims=True) - xh * jnp.mean(dxh * xh, axis=-1, keepdims=True))
        du_ref[...] = du
        dcb_ref[...] += jnp.sum(du, axis=0, keepdims=True)

    row = pl.BlockSpec((tt, cc), lambda i: (i, 0))
    full = pl.BlockSpec(ln_g.shape, lambda i: (0, 0))
    vec = pl.BlockSpec((1, cc), lambda i: (0, 0))
    return pl.pallas_call(body, name=name, grid=(t // tt,), in_specs=[row, full, full, row],
                          out_specs=[row, vec, vec, vec], out_shape=[SDS((t, cc), F32)] + [SDS((1, cc), F32)] * 3,
                          compiler_params=_cparams(("arbitrary",)))(uc, ln_g, ln_b, ds)


def _glu_bwd(du, proj, dproj, o_a, cc, name):
    t = du.shape[0]
    tt = _rt(t)

    def body(du_ref, a_ref, gate_ref, prev_ref, o_ref):
        sg = _sig(gate_ref[...])
        dv = du_ref[...]
        o_ref[:, 0:cc] = (dv * sg).astype(BF16)
        o_ref[:, cc:2 * cc] = (dv * a_ref[...] * sg * (1.0 - sg)).astype(BF16)

    return pl.pallas_call(
        body, name=name, grid=(t // tt,),
        in_specs=[pl.BlockSpec((tt, cc), lambda i: (i, 0)), pl.BlockSpec((tt, cc), lambda i: (i, o_a // cc)),
                  pl.BlockSpec((tt, cc), lambda i: (i, o_a // cc + 1)), ANY],
        out_specs=pl.BlockSpec((tt, 2 * cc), lambda i: (i, o_a // (2 * cc))),
        out_shape=SDS(dproj.shape, dproj.dtype), input_output_aliases={3: 0},
        compiler_params=_cparams(("arbitrary",)))(du, proj, proj, dproj)


def _rope(v, cs, s1, s2):
    return v * cs + pltpu.roll(v, HP - ROPE // 2, 1) * s1 + pltpu.roll(v, ROPE // 2, 1) * s2


def _rope_t(dv, cs, s1, s2):
    return dv * cs + pltpu.roll(dv * s1, ROPE // 2, 1) + pltpu.roll(dv * s2, HP - ROPE // 2, 1)


def _head_norm(v, g):
    r = lax.rsqrt(jnp.sum(v * v, axis=-1, keepdims=True) * (1.0 / QK) + EPS)
    return v * r * g, r


def _head_norm_bwd(v, r, g, dy):
    xh = v * r
    dg = jnp.sum(dy * xh, axis=0, keepdims=True)
    dxh = dy * g
    dx = r * (dxh - xh * (jnp.sum(dxh * xh, axis=-1, keepdims=True) * (1.0 / QK)))
    return dx, dg


def _mla_specs(tt, lay, cq_norm, ckv_norm, qn, kn, wuq, wukv):
    ql, kvl = cq_norm.shape[1], ckv_norm.shape[1]
    full = lambda a: pl.BlockSpec(a.shape, lambda i: (0,) * a.ndim)
    tab = pl.BlockSpec((tt, HP), lambda i: (i, 0))
    return [pl.BlockSpec((tt, ql), lambda i: (i, lay["cq"] // ql)),
            pl.BlockSpec((tt, kvl), lambda i: (i, lay["ckv"] // kvl)),
            pl.BlockSpec((tt, HP), lambda i: (i, lay["kr"] // HP)),
            full(cq_norm), full(ckv_norm), full(qn), full(kn), full(wuq), full(wukv), tab, tab, tab]


def _mla_pre_fwd(proj, lay, l, cq_norm, ckv_norm, qn, kn, wuq, wukv, tabs, name):
    t = proj.shape[0]
    tt = _rt(t)
    hw = N_HEADS * HP

    def body(cq_ref, ckv_ref, kr_ref, gq_ref, gkv_ref, qn_ref, kn_ref, wq_ref, wkv_ref, c_ref, s1_ref, s2_ref,
             q_ref, k_ref, v_ref):
        cq = cq_ref[...]
        cqn = (cq * _rms_r(cq) * gq_ref[l:l + 1, :]).astype(BF16)
        ckv = ckv_ref[...]
        ckvn = (ckv * _rms_r(ckv) * gkv_ref[l:l + 1, :]).astype(BF16)
        qraw = _dot_nt(cqn, wq_ref[...])
        kv = _dot(ckvn, wkv_ref[...])
        v_ref[...] = kv.astype(BF16)
        lane = lax.broadcasted_iota(jnp.int32, (tt, HP), 1)
        krs = pltpu.roll(jnp.where(lane < ROPE, kr_ref[...], 0.0), NOPE, 1)
        cs, s1, s2 = c_ref[...], s1_ref[...], s2_ref[...]
        gq, gk = qn_ref[l:l + 1, :], kn_ref[l:l + 1, :]
        for h in range(N_HEADS):
            sl = slice(h * HP, (h + 1) * HP)
            qh, _ = _head_norm(qraw[:, sl], gq)
            q_ref[:, sl] = (_rope(qh, cs, s1, s2) * QK ** -0.5).astype(BF16)
            kh, _ = _head_norm(jnp.where(lane < NOPE, kv[:, sl], krs), gk)
            k_ref[:, sl] = _rope(kh, cs, s1, s2).astype(BF16)

    row = pl.BlockSpec((tt, hw), lambda i: (i, 0))
    return pl.pallas_call(
        body, name=name, grid=(t // tt,),
        in_specs=_mla_specs(tt, lay, cq_norm, ckv_norm, qn, kn, wuq, wukv),
        out_specs=[row, row, row], out_shape=[SDS((t, hw), BF16)] * 3,
        compiler_params=_cparams(("arbitrary",), VMEM_BIG))(
            proj, proj, proj, cq_norm, ckv_norm, qn, kn, wuq, wukv, *tabs)


def _mla_pre_bwd(proj, lay, l, cq_norm, ckv_norm, qn, kn, wuq, wukv, tabs, dq, dk, dv, dproj, name):
    t = proj.shape[0]
    tt = _rt(t)
    hw = N_HEADS * HP
    ql, kvl = cq_norm.shape[1], ckv_norm.shape[1]
    wm = lay["wm"]

    def body(cq_ref, ckv_ref, kr_ref, gq_ref, gkv_ref, qn_ref, kn_ref, wq_ref, wkv_ref, c_ref, s1_ref, s2_ref,
             dq_ref, dk_ref, dv_ref, prev_ref,
             o_ref, dqr_ref, dkv_ref, cqn_ref, ckvn_ref, dgq_ref, dgkv_ref, dqn_ref, dkn_ref):
        @pl.when(pl.program_id(0) == 0)
        def _():
            for ref in (dgq_ref, dgkv_ref, dqn_ref, dkn_ref):
                ref[...] = jnp.zeros_like(ref)

        cq = cq_ref[...]
        cqn = (cq * _rms_r(cq) * gq_ref[l:l + 1, :]).astype(BF16)
        ckv = ckv_ref[...]
        ckvn = (ckv * _rms_r(ckv) * gkv_ref[l:l + 1, :]).astype(BF16)
        cqn_ref[...] = cqn
        ckvn_ref[...] = ckvn
        qraw = _dot_nt(cqn, wq_ref[...])
        kv = _dot(ckvn, wkv_ref[...])
        lane = lax.broadcasted_iota(jnp.int32, (tt, HP), 1)
        krs = pltpu.roll(jnp.where(lane < ROPE, kr_ref[...], 0.0), NOPE, 1)
        cs, s1, s2 = c_ref[...], s1_ref[...], s2_ref[...]
        gq, gk = qn_ref[l:l + 1, :], kn_ref[l:l + 1, :]
        dkr = jnp.zeros((tt, HP), F32)
        dgq = jnp.zeros((1, HP), F32)
        dgk = jnp.zeros((1, HP), F32)
        for h in range(N_HEADS):
            sl = slice(h * HP, (h + 1) * HP)
            qh = qraw[:, sl]
            _, rq = _head_norm(qh, gq)
            dqh, dg = _head_norm_bwd(qh, rq, gq, _rope_t(dq_ref[:, sl] * QK ** -0.5, cs, s1, s2))
            dgq = dgq + dg
            dqr_ref[:, sl] = dqh.astype(BF16)
            kp = jnp.where(lane < NOPE, kv[:, sl], krs)
            _, rk = _head_norm(kp, gk)
            dkp, dg = _head_norm_bwd(kp, rk, gk, _rope_t(dk_ref[:, sl], cs, s1, s2))
            dgk = dgk + dg
            dkv_ref[:, sl] = (jnp.where(lane < NOPE, dkp, 0.0) + dv_ref[:, sl]).astype(BF16)
            dkr = dkr + dkp
        dqn_ref[...] += dgq
        dkn_ref[...] += dgk
        dkr = jnp.where(lane < ROPE, pltpu.roll(dkr, HP - NOPE, 1), 0.0)
        dcq, dg = _rms_bwd(cq, gq_ref[l:l + 1, :], _dot(dqr_ref[...], wq_ref[...]))
        dgq_ref[...] += dg
        dckv, dg = _rms_bwd(ckv, gkv_ref[l:l + 1, :], _dot_nt(dkv_ref[...], wkv_ref[...]))
        dgkv_ref[...] += dg
        o_ref[:, 0:ql] = dcq.astype(BF16)
        o_ref[:, ql:ql + kvl] = dckv.astype(BF16)
        o_ref[:, ql + kvl:ql + kvl + HP] = dkr.astype(BF16)
        o_ref[:, ql + kvl + HP:wm] = jnp.zeros((tt, wm - ql - kvl - HP), BF16)

    row = pl.BlockSpec((tt, hw), lambda i: (i, 0))
    vec = lambda n: pl.BlockSpec((1, n), lambda i: (0, 0))
    return pl.pallas_call(
        body, name=name, grid=(t // tt,),
        in_specs=_mla_specs(tt, lay, cq_norm, ckv_norm, qn, kn, wuq, wukv) + [row, row, row, ANY],
        out_specs=[pl.BlockSpec((tt, wm), lambda i: (i, lay["cq"] // wm)), row, row,
                   pl.BlockSpec((tt, ql), lambda i: (i, 0)), pl.BlockSpec((tt, kvl), lambda i: (i, 0)),
                   vec(ql), vec(kvl), vec(HP), vec(HP)],
        out_shape=[SDS(dproj.shape, dproj.dtype), SDS((t, hw), BF16), SDS((t, hw), BF16), SDS((t, ql), BF16),
                   SDS((t, kvl), BF16), SDS((1, ql), F32), SDS((1, kvl), F32), SDS((1, HP), F32), SDS((1, HP), F32)],
        input_output_aliases={15: 0},
        compiler_params=_cparams(("arbitrary",), VMEM_BIG))(
            proj, proj, proj, cq_norm, ckv_norm, qn, kn, wuq, wukv, *tabs, dq, dk, dv, dproj)


def _comb_fwd(proj, gate_bias, l, yc, ym, name):
    t, d = yc.shape
    tt = _rt(t)

    def body(p_ref, b_ref, yc_ref, ym_ref, y_ref):
        b = b_ref[l]
        g0 = _sig(p_ref[:, 0:d] + b[0:1, :])
        g1 = _sig(p_ref[:, d:2 * d] + b[1:2, :])
        y_ref[...] = (g0 * yc_ref[...] + g1 * ym_ref[...]).astype(BF16)

    row = pl.BlockSpec((tt, d), lambda i: (i, 0))
    return pl.pallas_call(
        body, name=name, grid=(t // tt,),
        in_specs=[pl.BlockSpec((tt, 2 * d), lambda i: (i, 0)), pl.BlockSpec(gate_bias.shape, lambda i: (0, 0, 0)), row, row],
        out_specs=row, out_shape=SDS((t, d), BF16), compiler_params=_cparams(("arbitrary",)))(proj, gate_bias, yc, ym)


def _comb_bwd(proj, gate_bias, l, yc, ym, dy, dp_cols, name):
    t, d = yc.shape
    tt = _rt(t)

    def body(p_ref, b_ref, yc_ref, ym_ref, dy_ref, dyc_ref, dym_ref, dp_ref, db_ref):
        @pl.when(pl.program_id(0) == 0)
        def _():
            db_ref[...] = jnp.zeros_like(db_ref)

        b = b_ref[l]
        dyv = dy_ref[...]
        g0 = _sig(p_ref[:, 0:d] + b[0:1, :])
        g1 = _sig(p_ref[:, d:2 * d] + b[1:2, :])
        dyc_ref[...] = (dyv * g0).astype(BF16)
        dym_ref[...] = (dyv * g1).astype(BF16)
        dg0 = dyv * yc_ref[...] * g0 * (1.0 - g0)
        dg1 = dyv * ym_ref[...] * g1 * (1.0 - g1)
        dp_ref[:, 0:d] = dg0.astype(BF16)
        dp_ref[:, d:2 * d] = dg1.astype(BF16)
        db_ref[0:1, :] += jnp.sum(dg0, axis=0, keepdims=True)
        db_ref[1:2, :] += jnp.sum(dg1, axis=0, keepdims=True)

    row = pl.BlockSpec((tt, d), lambda i: (i, 0))
    wide = pl.BlockSpec((tt, 2 * d), lambda i: (i, 0))
    return pl.pallas_call(
        body, name=name, grid=(t // tt,),
        in_specs=[wide, pl.BlockSpec(gate_bias.shape, lambda i: (0, 0, 0)), row, row, row],
        out_specs=[row, row, wide, pl.BlockSpec((2, d), lambda i: (0, 0))],
        out_shape=[SDS((t, d), BF16), SDS((t, d), BF16), SDS((t, dp_cols), BF16), SDS((2, d), F32)],
        compiler_params=_cparams(("arbitrary",)))(proj, gate_bias, yc, ym, dy)


def _loss_grad(y, target, name):
    t, d = y.shape
    tt = _rt(t)
    nt = t // tt

    def body(y_ref, t_ref, dy_ref, loss_ref, acc_sc):
        i = pl.program_id(0)

        @pl.when(i == 0)
        def _():
            acc_sc[...] = jnp.zeros_like(acc_sc)

        diff = y_ref[...] - t_ref[...]
        dy_ref[...] = diff * (1.0 / d)
        acc_sc[...] += jnp.sum(diff * diff, axis=0, keepdims=True)

        @pl.when(i == nt - 1)
        def _():
            tot = jnp.sum(acc_sc[...], axis=1, keepdims=True) * (0.5 / d)
            loss_ref[...] = jnp.broadcast_to(tot, (1, HP))

    row = pl.BlockSpec((tt, d), lambda i: (i, 0))
    return pl.pallas_call(body, name=name, grid=(nt,), in_specs=[row, row],
                          out_specs=[row, pl.BlockSpec((1, HP), lambda i: (0, 0))],
                          out_shape=[SDS((t, d), F32), SDS((1, HP), F32)],
                          scratch_shapes=[pltpu.VMEM((1, d), F32)],
                          compiler_params=_cparams(("arbitrary",)))(y, target)


def _chunk_mask(tq):
    rows = lax.broadcasted_iota(jnp.int32, (tq, tq), 0) // CHUNK
    cols = lax.broadcasted_iota(jnp.int32, (tq, tq), 1) // CHUNK
    return cols <= rows


NEG = -1e30


def _flash_fwd(q, k, v, name, comm=None):
    t = q.shape[0]
    tq = _rt(t)
    nq = t // tq
    rep = tq // HP

    def body(q_ref, k_ref, v_ref, o_ref, lse_ref):
        qi = pl.program_id(1)
        qv = q_ref[...]

        def step(ki, carry, masked):
            m_prev, l_prev, acc = carry
            r0 = pl.multiple_of(ki * tq, tq)
            s = _dot_nt(qv, k_ref[pl.ds(r0, tq), :])
            if masked:
                s = jnp.where(_chunk_mask(tq), s, NEG)
            m_new = jnp.maximum(m_prev, jnp.max(s, axis=-1, keepdims=True))
            a = jnp.exp(m_prev - m_new)
            p = jnp.exp(s - jnp.tile(m_new, (1, rep)))
            l_new = a * l_prev + jnp.sum(p, axis=-1, keepdims=True)
            acc = a * acc + _dot(p.astype(BF16), v_ref[pl.ds(r0, tq), :])
            return m_new, l_new, acc

        init = (jnp.full((tq, HP), NEG, F32), jnp.zeros((tq, HP), F32), jnp.zeros((tq, HP), F32))
        carry = lax.fori_loop(0, qi, lambda ki, cr: step(ki, cr, False), init)
        m_fin, l_fin, acc = step(qi, carry, True)
        o_ref[...] = (acc / l_fin).astype(BF16)
        lse_ref[...] = m_fin + jnp.log(l_fin)

    qspec = pl.BlockSpec((tq, HP), lambda h, qi: (qi, h))
    head = pl.BlockSpec((t, HP), lambda h, qi: (0, h))
    first = lambda: jnp.logical_and(pl.program_id(0) == 0, pl.program_id(1) == 0)
    last = lambda: jnp.logical_and(pl.program_id(0) == N_HEADS - 1, pl.program_id(1) == nq - 1)
    outs, couts = _call(
        body, comm, first, last, name=name, grid=(N_HEADS, nq), in_specs=[qspec, head, head], args=[q, k, v],
        out_specs=[qspec, pl.BlockSpec((None, tq, HP), lambda h, qi: (h, qi, 0))],
        out_shape=[SDS(q.shape, BF16), SDS((N_HEADS, t, HP), F32)], scratch=[], sem=("arbitrary",) * 2)
    return (*outs, couts)


def _flash_bwd(q, k, v, do, o, lse, name, comm=None):
    t = q.shape[0]
    tq = _rt(t)
    nq = t // tq
    rep = tq // HP

    def body(q_ref, k_ref, v_ref, do_ref, o_ref, lse_ref, dq_ref, dk_ref, dv_ref, delta_sc):
        def prep(qi, carry):
            r0 = pl.multiple_of(qi * tq, tq)
            rows = pl.ds(r0, tq)
            dlt = jnp.sum(do_ref[rows, :].astype(F32) * o_ref[rows, :].astype(F32), axis=-1, keepdims=True)
            delta_sc[rows, :] = jnp.broadcast_to(dlt, (tq, HP))
            dq_ref[rows, :] = jnp.zeros((tq, HP), F32)
            return carry

        lax.fori_loop(0, nq, prep, 0)

        def keys(ki, carry0):
            krows = pl.ds(pl.multiple_of(ki * tq, tq), tq)
            kt, vt = k_ref[krows, :], v_ref[krows, :]

            def step(qi, carry, masked):
                dk_acc, dv_acc = carry
                rows = pl.ds(pl.multiple_of(qi * tq, tq), tq)
                qt, dot_ = q_ref[rows, :], do_ref[rows, :]
                s = _dot_nt(qt, kt)
                if masked:
                    s = jnp.where(_chunk_mask(tq), s, NEG)
                p = jnp.exp(s - jnp.tile(lse_ref[rows, :], (1, rep)))
                ds = (p * (_dot_nt(dot_, vt) - jnp.tile(delta_sc[rows, :], (1, rep)))).astype(BF16)
                dv_acc = dv_acc + _dot_tn(p.astype(BF16), dot_)
                dk_acc = dk_acc + _dot_tn(ds, qt)
                dq_ref[rows, :] += _dot(ds, kt)
                return dk_acc, dv_acc

            zero = jnp.zeros((tq, HP), F32)
            carry = step(ki, (zero, zero), True)
            dk_acc, dv_acc = lax.fori_loop(ki + 1, nq, lambda qi, cr: step(qi, cr, False), carry)
            dk_ref[krows, :] = dk_acc
            dv_ref[krows, :] = dv_acc
            return carry0

        lax.fori_loop(0, nq, keys, 0)

    head = pl.BlockSpec((t, HP), lambda h: (0, h))
    outs, couts = _call(
        body, comm, lambda: pl.program_id(0) == 0, lambda: pl.program_id(0) == N_HEADS - 1, name=name, grid=(N_HEADS,),
        in_specs=[head] * 5 + [pl.BlockSpec((None, t, HP), lambda h: (h, 0, 0))], args=[q, k, v, do, o, lse],
        out_specs=[head] * 3, out_shape=[SDS(q.shape, F32)] * 3, scratch=[pltpu.VMEM((t, HP), F32)], sem=("arbitrary",))
    return (*outs, couts)


def _add_cast(gs, rs, place, name):
    n = len(gs)
    _, r, c = gs[0].shape
    tr, tc = _tile2(r, c // 2, 3 * n, 16)
    nct = c // 2 // tc

    def body(place_ref, *refs):
        for a in range(n):
            refs[2 * n + a][...] = (refs[a][...] + refs[n + a][...]).astype(BF16)

    gspec = pl.BlockSpec((None, tr, tc), lambda j, i, k, pr: (j, i, pr[0] * nct + k))
    rspec = pl.BlockSpec((None, tr, tc), lambda j, i, k, pr: (j, i, k))
    grid_spec = pltpu.PrefetchScalarGridSpec(num_scalar_prefetch=1, grid=(NCHIP, r // tr, nct),
                                             in_specs=[gspec] * n + [rspec] * n, out_specs=[rspec] * n)
    return pl.pallas_call(body, name=name, grid_spec=grid_spec, out_shape=[SDS((NCHIP, r, c // 2), BF16)] * n,
                          compiler_params=_cparams(("arbitrary",) * 3, VMEM_BIG))(place, *gs, *rs)


def _sum_chips(ss, qs, place, l, nl, prevs, name):
    n = len(ss)
    _, r, h = ss[0].shape
    tr, tc = _tile2(r, h, 3 * n, 16)
    nct = h // tc

    def body(place_ref, *refs):
        for a in range(n):
            acc = refs[n + a][...].astype(F32)
            for kk in range(NCHIP - 1):
                acc = acc + refs[a][kk].astype(F32)
            refs[-n + a][...] = acc

    in_specs = ([pl.BlockSpec((NCHIP - 1, tr, tc), lambda i, k, pr: (0, i, k))] * n
                + [pl.BlockSpec((None, tr, tc), lambda i, k, pr: (pr[1], i, k))] * n)
    args = [*ss, *qs]
    aliases = {}
    if prevs is not None:
        aliases = {1 + len(args) + a: a for a in range(n)}
        in_specs += [ANY] * n
        args += list(prevs)
    grid_spec = pltpu.PrefetchScalarGridSpec(
        num_scalar_prefetch=1, grid=(r // tr, nct), in_specs=in_specs,
        out_specs=[pl.BlockSpec((None, tr, tc), lambda i, k, pr: (l, i, pr[0] * nct + k))] * n)
    return pl.pallas_call(body, name=name, grid_spec=grid_spec, out_shape=[SDS((nl, r, 2 * h), F32)] * n,
                          input_output_aliases=aliases,
                          compiler_params=_cparams(("arbitrary",) * 2, VMEM_BIG))(place, *args)


def _cast_place(ws, place, name):
    n = len(ws)
    nl, r, c = ws[0].shape
    tr, tc = _tile2(r, c, 2 * n * nl, 16)

    def body(place_ref, *refs):
        for a in range(n * nl):
            refs[n * nl + a][...] = refs[a][...].astype(BF16)

    in_specs = [pl.BlockSpec((None, tr, tc), functools.partial(lambda l, i, k, pr: (l, i, k), l))
                for _ in range(n) for l in range(nl)]
    grid_spec = pltpu.PrefetchScalarGridSpec(
        num_scalar_prefetch=1, grid=(r // tr, c // tc), in_specs=in_specs,
        out_specs=[pl.BlockSpec((None, tr, tc), lambda i, k, pr: (pr[1], i, k))] * (n * nl))
    outs = pl.pallas_call(body, name=name, grid_spec=grid_spec, out_shape=[SDS((NCHIP, r, c), BF16)] * (n * nl),
                          compiler_params=_cparams(("arbitrary",) * 2, VMEM_BIG))(
                              place, *[w for w in ws for _ in range(nl)])
    return [outs[a * nl:(a + 1) * nl] for a in range(n)]


def _adamw(ws, gs, ms, vs, name):
    n = len(ws)
    r, c = ws[0].shape
    tr, tc = _tile2(r, c, 7 * n, 8)
    c1, c2 = 1.0 / (1.0 - B1 ** STEP), 1.0 / (1.0 - B2 ** STEP)

    def body(*refs):
        for a in range(n):
            w, g, m, v = (refs[kk * n + a][...] for kk in range(4))
            m2 = B1 * m + (1.0 - B1) * g
            v2 = B2 * v + (1.0 - B2) * (g * g)
            refs[4 * n + a][...] = -LR * ((m2 * c1) / (jnp.sqrt(v2 * c2) + EPS_ADAM) + WD * w)
            refs[5 * n + a][...] = m2
            refs[6 * n + a][...] = v2

    blk = pl.BlockSpec((tr, tc), lambda i, k: (i, k))
    outs = pl.pallas_call(body, name=name, grid=(r // tr, c // tc), in_specs=[blk] * (4 * n),
                          out_specs=[blk] * (3 * n), out_shape=[SDS((r, c), F32)] * (3 * n),
                          compiler_params=_cparams(("arbitrary",) * 2, VMEM_BIG))(*ws, *gs, *ms, *vs)
    return outs[:n], outs[n:2 * n], outs[2 * n:]


def _place():
    x, y, c = lax.axis_index("x"), lax.axis_index("y"), lax.axis_index("c")
    return x, y, c, [(1 - x, y), (x, 1 - y), (1 - x, 1 - y)]


def _rcopy(src, dst, ssem, rsem, k, dev):
    return pltpu.make_async_remote_copy(src_ref=src, dst_ref=dst, send_sem=ssem.at[k], recv_sem=rsem.at[k],
                                        device_id=dev, device_id_type=MESH)


def _half(ref, lead, cc):
    h = ref.shape[-1] // 2
    return ref.at[(*lead, slice(None), pl.ds(cc * h, h))]


def _per_core(fn):
    c = lax.axis_index("c")
    for cc in (0, 1):
        pl.when(c == cc)(functools.partial(fn, cc))


def _gather_comm(bufs):
    n = len(bufs)

    def plan(couts, ssem, rsem, cc):
        x, y, _, peers = _place()
        me, sib = 2 * x + y, (x, y, 1 - cc)
        send, recv, fwd, recv2 = [], [], [], []
        for a in range(n):
            for kk, (px, py) in enumerate(peers):
                mine = _half(couts[a], (me,), cc)
                got = _half(couts[a], (2 * px + py,), cc)
                other = _half(couts[a], (2 * px + py,), 1 - cc)
                send.append(_rcopy(mine, mine, ssem, rsem, a * 6 + kk, (px, py, cc)))
                recv.append(_rcopy(got, got, ssem, rsem, a * 6 + kk, (px, py, cc)))
                fwd.append(_rcopy(got, got, ssem, rsem, a * 6 + 3 + kk, sib))
                recv2.append(_rcopy(other, other, ssem, rsem, a * 6 + 3 + kk, sib))
        return send, recv, fwd, recv2

    def start(cins, couts, ssem, rsem):
        def go(cc):
            for d in plan(couts, ssem, rsem, cc)[0]:
                d.start()

        _per_core(go)

    def finish(cins, couts, ssem, rsem):
        def go(cc):
            send, recv, fwd, recv2 = plan(couts, ssem, rsem, cc)
            for dr, df in zip(recv, fwd):
                dr.wait_recv()
                df.start()
            for d in recv2:
                d.wait_recv()
            for d in send + fwd:
                d.wait_send()

        _per_core(go)

    return _Comm(bufs, [SDS(b.shape, b.dtype) for b in bufs], {a: a for a in range(n)}, 6 * n, start, finish)


def _pair_comm(gs):
    n = len(gs)
    halves = [g.shape[-1] // 2 for g in gs]

    def plan(cins, couts, ssem, rsem, cc):
        x, y, _, _ = _place()
        return [_rcopy(cins[a].at[:, :, pl.ds((1 - cc) * halves[a], halves[a])], couts[a], ssem, rsem, a, (x, y, 1 - cc))
                for a in range(n)]

    def start(cins, couts, ssem, rsem):
        def go(cc):
            for d in plan(cins, couts, ssem, rsem, cc):
                d.start()

        _per_core(go)

    def finish(cins, couts, ssem, rsem):
        def go(cc):
            ds = plan(cins, couts, ssem, rsem, cc)
            for d in ds:
                d.wait_recv()
            for d in ds:
                d.wait_send()

        _per_core(go)

    return _Comm(gs, [SDS(g.shape[:-1] + (g.shape[-1] // 2,), g.dtype) for g in gs], {}, n, start, finish)


def _chips_comm(qs):
    n = len(qs)

    def plan(cins, couts, ssem, rsem):
        x, y, c, peers = _place()
        return [_rcopy(cins[a].at[2 * px + py], couts[a].at[kk], ssem, rsem, a * 3 + kk, (px, py, c))
                for a in range(n) for kk, (px, py) in enumerate(peers)]

    def start(cins, couts, ssem, rsem):
        for d in plan(cins, couts, ssem, rsem):
            d.start()

    def finish(cins, couts, ssem, rsem):
        ds = plan(cins, couts, ssem, rsem)
        for d in ds:
            d.wait_recv()
        for d in ds:
            d.wait_send()

    return _Comm(qs, [SDS((NCHIP - 1,) + q.shape[1:], q.dtype) for q in qs], {}, 3 * n, start, finish)


def _share_comm(fs, l):
    n = len(fs)

    def plan(couts, ssem, rsem, cc, which):
        x, y, _, _ = _place()
        out = []
        for a in range(n):
            piece = _half(couts[a], (l,), which)
            out.append(_rcopy(piece, piece, ssem, rsem, a, (x, y, 1 - cc)))
        return out

    def start(cins, couts, ssem, rsem):
        def go(cc):
            for d in plan(couts, ssem, rsem, cc, cc):
                d.start()

        _per_core(go)

    def finish(cins, couts, ssem, rsem):
        def go(cc):
            for d in plan(couts, ssem, rsem, cc, 1 - cc):
                d.wait_recv()
            for d in plan(couts, ssem, rsem, cc, cc):
                d.wait_send()

        _per_core(go)

    return _Comm(fs, [SDS(f.shape, f.dtype) for f in fs], {a: a for a in range(n)}, n, start, finish)


def _exchange(buf, reduce, name):
    r, w = buf.shape

    def body(in_ref, out_ref, recv_sc, ssem, rsem):
        x, y, c, _ = _place()
        me = 4 * x + 2 * y + c
        sends = []
        for rel in range(1, N_DEV):
            dev = (1 - x if rel & 4 else x, 1 - y if rel & 2 else y, 1 - c if rel & 1 else c)
            d = _rcopy(in_ref, recv_sc.at[me], ssem, rsem, rel - 1, dev)
            d.start()
            sends.append(d)
        recv_sc[me] = in_ref[...]
        for rel in range(1, N_DEV):
            px, py, pc = (1 - x if rel & 4 else x, 1 - y if rel & 2 else y, 1 - c if rel & 1 else c)
            slot = recv_sc.at[4 * px + 2 * py + pc]
            _rcopy(slot, slot, ssem, rsem, rel - 1, (px, py, pc)).wait_recv()
        for d in sends:
            d.wait_send()
        if reduce:
            acc = recv_sc[0]
            for dv in range(1, N_DEV):
                acc = acc + recv_sc[dv]
            out_ref[...] = acc
        else:
            out_ref[...] = recv_sc[...]

    vm = pl.BlockSpec(memory_space=pltpu.VMEM)
    return pl.pallas_call(
        body, name=name, in_specs=[vm], out_specs=vm,
        out_shape=SDS((r, w) if reduce else (N_DEV, r, w), F32),
        scratch_shapes=[pltpu.VMEM((N_DEV, r, w), F32), pltpu.SemaphoreType.DMA((N_DEV - 1,)),
                        pltpu.SemaphoreType.DMA((N_DEV - 1,))],
        compiler_params=_cparams(None, VMEM_BIG))(buf)


def _cols_full(g):
    _, k, ns = g.shape
    return g.transpose(1, 0, 2).reshape(k, NCHIP * ns)


def _cols_shards(w):
    k, n = w.shape
    return w.reshape(k, NCHIP, n // NCHIP).transpose(1, 0, 2)


def _rows_pad(rows, width):
    out = jnp.concatenate([jnp.pad(a, ((0, 0), (0, width - a.shape[1]))) for a in rows], axis=0)
    return jnp.pad(out, ((0, -out.shape[0] % 8), (0, 0)))


def kernel(x, positions, ffn1_norm, ffn1_w_gate, ffn1_w_up, ffn1_w_down, mix_norm, w_in, gate_bias, conv_w, conv_b, conv_ln_g, conv_ln_b, w_conv_out, cq_norm, ckv_norm, w_uq, w_ukv, q_norm, k_norm, w_mla_out, w_out, ffn2_norm, ffn2_w_gate, ffn2_w_up, ffn2_w_down, loss_target, m_ffn1_norm, m_ffn1_w_gate, m_ffn1_w_up, m_ffn1_w_down, m_mix_norm, m_w_in, m_gate_bias, m_conv_w, m_conv_b, m_conv_ln_g, m_conv_ln_b, m_w_conv_out, m_cq_norm, m_ckv_norm, m_w_uq, m_w_ukv, m_q_norm, m_k_norm, m_w_mla_out, m_w_out, m_ffn2_norm, m_ffn2_w_gate, m_ffn2_w_up, m_ffn2_w_down, v_ffn1_norm, v_ffn1_w_gate, v_ffn1_w_up, v_ffn1_w_down, v_mix_norm, v_w_in, v_gate_bias, v_conv_w, v_conv_b, v_conv_ln_g, v_conv_ln_b, v_w_conv_out, v_cq_norm, v_ckv_norm, v_w_uq, v_w_ukv, v_q_norm, v_k_norm, v_w_mla_out, v_w_out, v_ffn2_norm, v_ffn2_w_gate, v_ffn2_w_up, v_ffn2_w_down):
    names = ["ffn1_norm", "ffn1_w_gate", "ffn1_w_up", "ffn1_w_down", "mix_norm", "w_in", "gate_bias", "conv_w",
             "conv_b", "conv_ln_g", "conv_ln_b", "w_conv_out", "cq_norm", "ckv_norm", "w_uq", "w_ukv", "q_norm",
             "k_norm", "w_mla_out", "w_out", "ffn2_norm", "ffn2_w_gate", "ffn2_w_up", "ffn2_w_down"]
    env = dict(locals())
    turned = ("ffn1_w_gate", "ffn1_w_up", "ffn2_w_gate", "ffn2_w_up", "w_in", "w_uq")
    view = lambda nm, a: jnp.swapaxes(a, 1, 2) if nm in turned else a
    wts = {nm: view(nm, env[nm]) for nm in names}
    mom = {nm: view(nm, env["m_" + nm]) for nm in names}
    var = {nm: view(nm, env["v_" + nm]) for nm in names}

    t, d = x.shape[1], x.shape[2]
    nl = ffn1_norm.shape[0]
    cc = conv_b.shape[1]
    ql, kvl = cq_norm.shape[1], ckv_norm.shape[1]
    vw = N_HEADS * V_DIM
    hw = N_HEADS * HP
    lay = {"a": 2 * d, "cq": 2 * d + 2 * cc, "ckv": 2 * d + 2 * cc + ql, "kr": 2 * d + 2 * cc + ql + kvl,
           "wm": ql + kvl + 2 * HP}
    dp = lay["cq"] + lay["wm"]
    nat_g = 2 * cc + ql + kvl + ROPE
    assert lay["cq"] % lay["wm"] == 0 and lay["cq"] % ql == 0 and lay["ckv"] % kvl == 0 and lay["a"] % (2 * cc) == 0
    assert cc % HP == 0 and d % HP == 0 and t % (2 * CHUNK) == 0 and w_in.shape[2] * NCHIP == nat_g + 2 * d
    assert nl == 2

    x0, target = x[0], loss_target[0]
    chip = 2 * lax.axis_index("x") + lax.axis_index("y")
    place = jnp.stack([lax.axis_index("c"), chip]).astype(jnp.int32)

    inv_freq = ROPE_THETA ** (-jnp.arange(0, ROPE, 2, dtype=F32) / ROPE)
    ang = positions[0].astype(F32)[:, None] * inv_freq
    cos, sin, z = jnp.cos(ang), jnp.sin(ang), jnp.zeros((t, ROPE // 2), F32)
    tabs = (jnp.concatenate([jnp.ones((t, NOPE), F32), cos, cos, jnp.zeros((t, HP - QK), F32)], axis=1),
            jnp.concatenate([jnp.zeros((t, NOPE), F32), -sin, z, jnp.zeros((t, HP - QK), F32)], axis=1),
            jnp.concatenate([jnp.zeros((t, NOPE), F32), z, sin, jnp.zeros((t, HP - QK), F32)], axis=1))

    big = ["ffn1_w_gate", "ffn1_w_up", "ffn1_w_down", "ffn2_w_gate", "ffn2_w_up", "ffn2_w_down",
           "w_in", "w_conv_out", "w_uq", "w_ukv", "w_mla_out", "w_out"]
    ffn1_w, ffn2_w, mix_w = big[0:3], big[3:6], big[6:]
    bufs, like = {}, {}
    for nm in big:
        like.setdefault(wts[nm].shape, []).append(nm)
    for gi, grp in enumerate(like.values()):
        for nm, per_layer in zip(grp, _cast_place([wts[nm] for nm in grp], place, f"cast_place_{gi}")):
            for l in range(nl):
                bufs[nm, l] = per_layer[l]

    def gather(keys):
        return _gather_comm([bufs[key] for key in keys])

    def landed(keys, outs):
        for key, o in zip(keys, outs):
            bufs[key] = o

    chunk = lambda nms, l: [(nm, l) for nm in nms]
    first_keys = chunk(ffn1_w, 0)
    landed(first_keys, _comm_call(gather(first_keys), "gather_first"))
    ride = {("ffn1", 0): chunk(mix_w, 0), ("flash", 0): chunk(ffn2_w, 0) + chunk(ffn1_w, 1),
            ("ffn2", 0): chunk(mix_w, 1), ("ffn1", 1): chunk(ffn2_w, 1)}

    def riding(kind, l):
        keys = ride.get((kind, l))
        return keys or [], (None if keys is None else gather(keys))

    def mixer_weights(l):
        w_in_nat = bufs["w_in", l].reshape(nat_g + 2 * d, d)
        w_in_k = jnp.concatenate([w_in_nat[nat_g:], w_in_nat[:nat_g], jnp.zeros((dp - nat_g - 2 * d, d), BF16)], axis=0)
        w_uq_k = jnp.pad(bufs["w_uq", l].reshape(N_HEADS, QK, ql), ((0, 0), (0, HP - QK), (0, 0))).reshape(hw, ql)
        w_mo_k = jnp.pad(_cols_full(bufs["w_mla_out", l]).reshape(N_HEADS, V_DIM, d),
                         ((0, 0), (HP - V_DIM, 0), (0, 0))).reshape(hw, d)
        return dict(w_in=w_in_k, w_co=_cols_full(bufs["w_conv_out", l]), w_uq=w_uq_k,
                    w_ukv=_cols_full(bufs["w_ukv", l]), w_mo=w_mo_k, w_out=bufs["w_out", l].reshape(d, d))

    qn_k = jnp.pad(q_norm, ((0, 0), (0, HP - QK)))
    kn_k = jnp.pad(k_norm, ((0, 0), (0, HP - QK)))
    small = _rows_pad([gate_bias.reshape(nl * 2, d // NCHIP), conv_w.reshape(nl * KW, cc // NCHIP)], d)
    everyone = _exchange(small, False, "gather_small")[0::2]
    gb_k = everyone[:, :nl * 2, :d // NCHIP].reshape(NCHIP, nl, 2, d // NCHIP).transpose(1, 2, 0, 3).reshape(nl, 2, d)
    cw = everyone[:, nl * 2:nl * 2 + nl * KW, :cc // NCHIP].reshape(NCHIP, nl, KW, cc // NCHIP)
    cw_k = jnp.pad(cw.transpose(1, 2, 0, 3).reshape(nl, KW, cc), ((0, 0), (0, CPAD - KW), (0, 0)))

    saved, mixw = [], []
    xc = x0
    for l in range(nl):
        keys, comm = riding("ffn1", l)
        x1, gt1, up1, got = _ffn_fwd(xc, ffn1_norm, l, *[bufs[nm, l] for nm in ffn1_w], f"ffn1_fwd_{l}", comm)
        landed(keys, got)
        mw = mixer_weights(l)
        mixw.append(mw)
        hm = _rms_fwd(x1, mix_norm, l, f"mix_norm_{l}")
        proj = _mm(hm, mw["w_in"], "nt", name=f"proj_{l}")
        uc = _dwconv_fwd(proj, cw_k, conv_b, l, lay["a"], cc, f"dwconv_{l}")
        sc = _ln_silu(uc, conv_ln_g, conv_ln_b, l, f"conv_ln_{l}")
        yc = _mm(sc, mw["w_co"], "nn", name=f"conv_out_{l}")
        q, k, kv = _mla_pre_fwd(proj, lay, l, cq_norm, ckv_norm, qn_k, kn_k, mw["w_uq"], mw["w_ukv"], tabs, f"mla_pre_{l}")
        keys, comm = riding("flash", l)
        o, lse, got = _flash_fwd(q, k, kv, f"flash_{l}", comm)
        landed(keys, got)
        ym = _mm(o, mw["w_mo"], "nn", name=f"mla_out_{l}")
        yv = _comb_fwd(proj, gb_k, l, yc, ym, f"combine_{l}")
        x2 = _mm(yv, mw["w_out"], "nn", name=f"mix_out_{l}", res=x1)
        keys, comm = riding("ffn2", l)
        x3, gt2, up2, got = _ffn_fwd(x2, ffn2_norm, l, *[bufs[nm, l] for nm in ffn2_w], f"ffn2_fwd_{l}", comm)
        landed(keys, got)
        saved.append(dict(x0=xc, x1=x1, gt1=gt1, up1=up1, hm=hm, proj=proj, uc=uc, sc=sc, yc=yc, q=q, k=k, kv=kv,
                          o=o, lse=lse, ym=ym, yv=yv, x2=x2, gt2=gt2, up2=up2))
        xc = x3
    dx, loss_row = _loss_grad(xc, target, "loss")

    gw = {nm: [None] * nl for nm in names}
    red = None

    def shard_groups(l):
        same = {}
        for nm in big:
            same.setdefault(gw[nm][l].shape, []).append(nm)
        return list(same.values())

    def add_parts(l, sib_part):
        qb = {}
        for gi, grp in enumerate(shard_groups(l)):
            outs = _add_cast([gw[nm][l] for nm in grp], [sib_part[nm] for nm in grp], place, f"rs_add_{l}_{gi}")
            qb.update(zip(grp, outs))
        return qb

    def sum_parts(l, qb, parts, prev):
        out = {}
        for gi, grp in enumerate(shard_groups(l)):
            outs = _sum_chips([parts[nm] for nm in grp], [qb[nm] for nm in grp], place, l, nl,
                              None if prev is None else [prev[nm] for nm in grp], f"rs_sum_{l}_{gi}")
            out.update(zip(grp, outs))
        return out

    for l in reversed(range(nl)):
        s, mw = saved[l], mixw[l]
        early = l + 1 if l + 1 < nl else None

        def ffn_back(tag, xin, gains, gts, ups, dout, comm=None):
            wg, wu, wd = (bufs[f"{tag}_w_{p}", l] for p in ("gate", "up", "down"))
            dxi, dgain, dgt, dup, act, hb, dob, got = _ffn_bwd(xin, gains, l, wg, wu, wd, gts, ups, dout,
                                                              f"{tag}_bwd_{l}", comm)
            gw[f"{tag}_w_gate"][l] = _mm(dgt, hb, "tn", name=f"{tag}_dwg_{l}", a_g=True)
            gw[f"{tag}_w_up"][l] = _mm(dup, hb, "tn", name=f"{tag}_dwu_{l}", a_g=True)
            gw[f"{tag}_w_down"][l] = _mm(act, dob, "tn", name=f"{tag}_dwd_{l}", a_g=True)
            gw[f"{tag}_norm"][l] = dgain
            return dxi, got

        comm = None if early is None else _pair_comm([gw[nm][early] for nm in big])
        dx2, got = ffn_back("ffn2", s["x2"], ffn2_norm, s["gt2"], s["up2"], dx, comm)
        if early is not None:
            qb = add_parts(early, dict(zip(big, got)))
        dyv = _mm(dx2, mw["w_out"], "nt", name=f"mix_out_dy_{l}")
        gw["w_out"][l] = _mm(s["yv"], dx2, "tn", name=f"mix_out_dw_{l}").reshape(NCHIP, d // NCHIP, d)
        dyc, dym, dproj, dgb = _comb_bwd(s["proj"], gb_k, l, s["yc"], s["ym"], dyv, dp, f"combine_bwd_{l}")
        gw["gate_bias"][l] = dgb
        dsc = _mm(dyc, mw["w_co"], "nt", name=f"conv_out_ds_{l}")
        gw["w_conv_out"][l] = _cols_shards(_mm(s["sc"], dyc, "tn", name=f"conv_out_dw_{l}"))
        duc, gw["conv_ln_g"][l], gw["conv_ln_b"][l], gw["conv_b"][l] = _ln_silu_bwd(
            s["uc"], conv_ln_g, conv_ln_b, l, dsc, f"conv_ln_bwd_{l}")
        du, dcw = _dwconv_bwd(duc, s["proj"], cw_k, l, lay["a"], cc, f"dwconv_bwd_{l}")
        gw["conv_w"][l] = dcw[:KW]
        dproj = _glu_bwd(du, s["proj"], dproj, lay["a"], cc, f"glu_bwd_{l}")
        do = _mm(dym, mw["w_mo"], "nt", name=f"mla_out_do_{l}", out_dtype=BF16)
        dwmo = _mm(s["o"], dym, "tn", name=f"mla_out_dw_{l}").reshape(N_HEADS, HP, d)[:, HP - V_DIM:].reshape(vw, d)
        gw["w_mla_out"][l] = _cols_shards(dwmo)
        comm = None if early is None else _chips_comm([qb[nm] for nm in big])
        dq, dk, dv, got = _flash_bwd(s["q"], s["k"], s["kv"], do, s["o"], s["lse"], f"flash_bwd_{l}", comm)
        if early is not None:
            red = sum_parts(early, qb, dict(zip(big, got)), red)
        dproj, dqr, dkv, cqn, ckvn, gw["cq_norm"][l], gw["ckv_norm"][l], dqn, dkn = _mla_pre_bwd(
            s["proj"], lay, l, cq_norm, ckv_norm, qn_k, kn_k, mw["w_uq"], mw["w_ukv"], tabs, dq, dk, dv, dproj,
            f"mla_pre_bwd_{l}")
        gw["q_norm"][l], gw["k_norm"][l] = dqn[:, :QK], dkn[:, :QK]
        dwuq = _mm(dqr, cqn, "tn", name=f"uq_dw_{l}").reshape(N_HEADS, HP, ql)[:, :QK]
        gw["w_uq"][l] = dwuq.reshape(NCHIP, N_HEADS * QK // NCHIP, ql)
        gw["w_ukv"][l] = _cols_shards(_mm(ckvn, dkv, "tn", name=f"ukv_dw_{l}"))
        dhm = _mm(dproj, mw["w_in"], "nn", name=f"proj_dh_{l}")
        dwin = _mm(dproj, s["hm"], "tn", name=f"proj_dw_{l}")
        gw["w_in"][l] = jnp.concatenate([dwin[2 * d:2 * d + nat_g], dwin[:2 * d]], axis=0).reshape(
            NCHIP, (nat_g + 2 * d) // NCHIP, d)
        dx1, gw["mix_norm"][l] = _rms_back(s["x1"], mix_norm, l, dhm, dx2, f"mix_norm_bwd_{l}")
        comm = None if early is None else _share_comm([red[nm] for nm in big], early)
        dx, got = ffn_back("ffn1", s["x0"], ffn1_norm, s["gt1"], s["up1"], dx1, comm)
        if early is not None:
            red = dict(zip(big, got))

    sib_part = dict(zip(big, _comm_call(_pair_comm([gw[nm][0] for nm in big]), "rs_pair")))
    qb = add_parts(0, sib_part)
    parts = dict(zip(big, _comm_call(_chips_comm([qb[nm] for nm in big]), "rs_chips")))
    red = sum_parts(0, qb, parts, red)
    red = dict(zip(big, _comm_call(_share_comm([red[nm] for nm in big], 0), "rs_share")))
    grads = {nm: g.reshape(wts[nm].shape) for nm, g in red.items()}

    vec_names = ["ffn1_norm", "mix_norm", "ffn2_norm", "conv_b", "conv_ln_g", "conv_ln_b", "cq_norm", "ckv_norm",
                 "q_norm", "k_norm"]
    rows = [jnp.concatenate(gw[nm], axis=0) for nm in vec_names]
    rows += [jnp.concatenate(gw["gate_bias"], axis=0), jnp.concatenate(gw["conv_w"], axis=0), loss_row]
    total = _exchange(_rows_pad(rows, d), True, "allreduce_small")
    r0 = 0
    for nm in vec_names:
        grads[nm] = total[r0:r0 + nl, :wts[nm].shape[1]]
        r0 += nl
    gb_all = total[r0:r0 + 2 * nl, :d].reshape(nl, 2, NCHIP, d // NCHIP)
    r0 += 2 * nl
    cw_all = total[r0:r0 + KW * nl, :cc].reshape(nl, KW, NCHIP, cc // NCHIP)
    r0 += KW * nl
    grads["gate_bias"] = lax.dynamic_index_in_dim(gb_all, chip, axis=2, keepdims=False)
    grads["conv_w"] = lax.dynamic_index_in_dim(cw_all, chip, axis=2, keepdims=False)
    loss = total[r0, 0]

    delta, new_m, new_v = {}, {}, {}
    by_shape = {}
    for nm in names:
        shp = wts[nm].shape
        by_shape.setdefault((shp[0] * (shp[1] if len(shp) == 3 else 1), shp[-1]), []).append(nm)
    for gi, (shp2, grp) in enumerate(by_shape.items()):
        to2 = lambda a: a.reshape(shp2)
        ds_, ms_, vs_ = _adamw([to2(wts[nm]) for nm in grp], [to2(grads[nm]) for nm in grp],
                               [to2(mom[nm]) for nm in grp], [to2(var[nm]) for nm in grp], f"adamw_{gi}")
        for nm, dd, mm_, vv in zip(grp, ds_, ms_, vs_):
            delta[nm], new_m[nm], new_v[nm] = (a.reshape(wts[nm].shape) for a in (dd, mm_, vv))

    return (loss, dx[None], *[view(nm, grads[nm]) for nm in names], *[view(nm, delta[nm]) for nm in names],
            *[view(nm, new_m[nm]) for nm in names], *[view(nm, new_v[nm]) for nm in names])
```

```python
import functools

import jax
import jax.numpy as jnp
from jax import lax
from jax.experimental import pallas as pl
from jax.experimental.pallas import tpu as pltpu

F32, BF16 = jnp.float32, jnp.bfloat16
SDS = jax.ShapeDtypeStruct
MESH = pl.DeviceIdType.MESH
ANY = pl.BlockSpec(memory_space=pl.ANY)

NCHIP = 4
N_DEV = 8
N_HEADS, NOPE, ROPE, V_DIM = 8, 64, 32, 64
QK = NOPE + ROPE
HP = 128
CHUNK = 64
KW = 31
CPAD = 32
ROPE_THETA = 10000.0
EPS = 1e-6
LR, B1, B2, EPS_ADAM, WD, STEP = 0.001, 0.9, 0.999, 1e-08, 0.01, 10
VMEM_BIG = 48 << 20


def _cparams(sem=None, vmem=None):
    kw = {}
    if sem is not None:
        kw["dimension_semantics"] = sem
    if vmem is not None:
        kw["vmem_limit_bytes"] = vmem
    return pltpu.CompilerParams(**kw)


def _rt(t):
    return min(512, t // 2)


def _pick(n, cands):
    for c in cands:
        if c <= n and n % c == 0:
            return c
    return n


def _tile2(r, c, nblocks, row_mult):
    tr, tc = r, c
    while 2 * nblocks * tr * tc * 4 > VMEM_BIG // 2 and tr % (2 * row_mult) == 0:
        tr //= 2
    while 2 * nblocks * tr * tc * 4 > VMEM_BIG // 2 and tc % 256 == 0:
        tc //= 2
    return tr, tc


def _sig(v):
    return 1.0 / (1.0 + jnp.exp(-v))


def _rms_r(v):
    return lax.rsqrt(jnp.mean(v * v, axis=-1, keepdims=True) + EPS)


def _rms_bwd(xv, g, dy):
    r = _rms_r(xv)
    xh = xv * r
    dg = jnp.sum(dy * xh, axis=0, keepdims=True)
    dxh = dy * g
    dx = r * (dxh - xh * jnp.mean(dxh * xh, axis=-1, keepdims=True))
    return dx, dg


def _dot(a, b):
    return jnp.dot(a, b, preferred_element_type=F32)


def _dot_nt(a, b):
    return lax.dot_general(a, b, (((1,), (1,)), ((), ())), preferred_element_type=F32)


def _dot_tn(a, b):
    return lax.dot_general(a, b, (((0,), (0,)), ((), ())), preferred_element_type=F32)


class _Comm:
    def __init__(self, ins, out_shapes, aliases, nsem, start, finish):
        self.ins, self.out_shapes, self.aliases, self.nsem = list(ins), list(out_shapes), dict(aliases), nsem
        self.start, self.finish = start, finish


def _call(body, comm, first, last, *, name, grid, in_specs, args, out_specs, out_shape, scratch, sem):
    in_specs, args, out_specs, out_shape, scratch = map(list, (in_specs, args, out_specs, out_shape, scratch))
    n_in, n_out, n_scr = len(args), len(out_shape), len(scratch)
    aliases = {}
    kern = body
    if comm is not None:
        nci, nco = len(comm.ins), len(comm.out_shapes)
        o0 = n_in + nci
        s0 = o0 + n_out + nco

        def kern(*refs):
            cins, couts = refs[n_in:o0], refs[o0 + n_out:s0]
            ssem, rsem = refs[s0 + n_scr:]
            pl.when(first())(lambda: comm.start(cins, couts, ssem, rsem))
            body(*refs[:n_in], *refs[o0:o0 + n_out], *refs[s0:s0 + n_scr])
            pl.when(last())(lambda: comm.finish(cins, couts, ssem, rsem))

        in_specs += [ANY] * nci
        args += comm.ins
        out_specs += [ANY] * nco
        out_shape += comm.out_shapes
        scratch += [pltpu.SemaphoreType.DMA((comm.nsem,))] * 2
        aliases = {n_in + i: n_out + o for i, o in comm.aliases.items()}
    outs = pl.pallas_call(kern, name=name, grid=grid, in_specs=in_specs, out_specs=out_specs, out_shape=out_shape,
                          scratch_shapes=scratch, input_output_aliases=aliases,
                          compiler_params=_cparams(sem, VMEM_BIG))(*args)
    return outs[:n_out], outs[n_out:]


def _comm_call(comm, name):
    once = lambda: pl.program_id(0) == 0
    return _call(lambda: None, comm, once, once, name=name, grid=(1,), in_specs=[], args=[], out_specs=[],
                 out_shape=[], scratch=[], sem=("arbitrary",))[1]


def _mm(a, b, mode, *, name, out_dtype=F32, res=None, a_g=False):
    a2 = a.shape[1:] if a_g else a.shape
    if mode == "nn":
        (m, k), (k2, n) = a2, b.shape
    elif mode == "nt":
        (m, k), (n, k2) = a2, b.shape
    else:
        (k, m), (k2, n) = a2, b.shape
    assert k == k2, (name, a.shape, b.shape)
    g = a.shape[0] if a_g else 1
    tm = m if m <= 768 else _pick(m, (768, 512, 256, 128))
    tn = n if n <= 1280 else _pick(n, (1280, 1024, 768, 512, 256, 128))
    tk = k if k <= 1280 else _pick(k, (1280, 1024, 768, 512, 256, 128))
    if mode != "tn" and 2 * (2 * k * n + tm * k * a.dtype.itemsize + tm * n * 4) <= VMEM_BIG - (8 << 20):
        tn, tk = n, k
    nk = k // tk
    dn = {"nn": (((1,), (0,)), ((), ())), "nt": (((1,), (1,)), ((), ())), "tn": (((0,), (0,)), ((), ()))}[mode]

    def body(*refs):
        a_ref, b_ref = refs[0], refs[1]
        res_ref = refs[2] if res is not None else None
        o_ref = refs[3] if res is not None else refs[2]
        p = lax.dot_general(a_ref[...].astype(BF16), b_ref[...].astype(BF16), dn, preferred_element_type=F32)

        def fin(v):
            if res_ref is not None:
                v = v + res_ref[...]
            o_ref[...] = v.astype(out_dtype)

        if nk == 1:
            fin(p)
        else:
            acc_ref = refs[-1]
            kk = pl.program_id(3)

            @pl.when(kk == 0)
            def _():
                acc_ref[...] = p

            @pl.when(kk > 0)
            def _():
                acc_ref[...] += p

            @pl.when(kk == nk - 1)
            def _():
                fin(acc_ref[...])

    a_block, a_idx = ((tk, tm), lambda gg, i, j, kk: (kk, i)) if mode == "tn" else ((tm, tk), lambda gg, i, j, kk: (i, kk))
    if a_g:
        a_spec = pl.BlockSpec((None,) + a_block, lambda gg, i, j, kk: (gg,) + a_idx(gg, i, j, kk))
    else:
        a_spec = pl.BlockSpec(a_block, a_idx)
    if mode == "nt":
        b_spec = pl.BlockSpec((tn, tk), lambda gg, i, j, kk: (j, kk))
    else:
        b_spec = pl.BlockSpec((tk, tn), lambda gg, i, j, kk: (kk, j))
    in_specs, args = [a_spec, b_spec], [a, b]
    if res is not None:
        in_specs.append(pl.BlockSpec((tm, tn), lambda gg, i, j, kk: (i, j)))
        args.append(res)
    if a_g:
        out_spec, oshape = pl.BlockSpec((None, tm, tn), lambda gg, i, j, kk: (gg, i, j)), (g, m, n)
    else:
        out_spec, oshape = pl.BlockSpec((tm, tn), lambda gg, i, j, kk: (i, j)), (m, n)
    return pl.pallas_call(
        body, name=name, grid=(g, m // tm, n // tn, nk), in_specs=in_specs, out_specs=out_spec,
        out_shape=SDS(oshape, out_dtype), scratch_shapes=[pltpu.VMEM((tm, tn), F32)] if nk > 1 else [],
        compiler_params=_cparams(("arbitrary",) * 4, VMEM_BIG))(*args)


def _ffn_fwd(x, gains, l, wg, wu, wd, name, comm=None):
    t, d = x.shape
    f = wg.shape[-2]
    tt = _rt(t)

    def body(x_ref, g_ref, wg_ref, wu_ref, wd_ref, o_ref, gt_ref, up_ref, h_sc, acc_sc):
        j = pl.program_id(1)

        @pl.when(j == 0)
        def _():
            xv = x_ref[...]
            h_sc[...] = (xv * _rms_r(xv) * g_ref[l:l + 1, :]).astype(BF16)
            acc_sc[...] = jnp.zeros_like(acc_sc)

        h = h_sc[...]
        gt = _dot_nt(h, wg_ref[...]).astype(BF16)
        up = _dot_nt(h, wu_ref[...]).astype(BF16)
        gt_ref[...] = gt
        up_ref[...] = up
        acc_sc[...] += _dot(gt * _sig(gt) * up, wd_ref[...])

        @pl.when(j == NCHIP - 1)
        def _():
            o_ref[...] = x_ref[...] + 0.5 * acc_sc[...]

    wspec = pl.BlockSpec((None, f, d), lambda i, j: (j, 0, 0))
    row = pl.BlockSpec((tt, d), lambda i, j: (i, 0))
    sh = pl.BlockSpec((None, tt, f), lambda i, j: (j, i, 0))
    ni = t // tt
    first = lambda: jnp.logical_and(pl.program_id(0) == 0, pl.program_id(1) == 0)
    last = lambda: jnp.logical_and(pl.program_id(0) == ni - 1, pl.program_id(1) == NCHIP - 1)
    outs, couts = _call(
        body, comm, first, last, name=name, grid=(ni, NCHIP),
        in_specs=[row, pl.BlockSpec(gains.shape, lambda i, j: (0, 0)), wspec, wspec, wspec], args=[x, gains, wg, wu, wd],
        out_specs=[row, sh, sh], out_shape=[SDS((t, d), F32), SDS((NCHIP, t, f), BF16), SDS((NCHIP, t, f), BF16)],
        scratch=[pltpu.VMEM((tt, d), BF16), pltpu.VMEM((tt, d), F32)], sem=("arbitrary", "arbitrary"))
    return (*outs, couts)


def _ffn_bwd(x, gains, l, wg, wu, wd, gts, ups, dout, name, comm=None):
    t, d = x.shape
    f = wg.shape[-2]
    tt = _rt(t)

    def body(x_ref, g_ref, wg_ref, wu_ref, wd_ref, gt_ref, up_ref, do_ref,
             dx_ref, dg_ref, dgt_ref, dup_ref, act_ref, h_ref, dob_ref, dh_sc):
        i, j = pl.program_id(0), pl.program_id(1)

        @pl.when(j == 0)
        def _():
            xv = x_ref[...]
            h_ref[...] = (xv * _rms_r(xv) * g_ref[l:l + 1, :]).astype(BF16)
            dob_ref[...] = (0.5 * do_ref[...]).astype(BF16)
            dh_sc[...] = jnp.zeros_like(dh_sc)

        @pl.when(jnp.logical_and(i == 0, j == 0))
        def _():
            dg_ref[...] = jnp.zeros_like(dg_ref)

        dact = _dot_nt(dob_ref[...], wd_ref[...]).astype(BF16)
        gt = gt_ref[...]
        up = up_ref[...]
        s = _sig(gt)
        sl = gt * s
        dup = dact * sl
        dgt = dact * up * (s + sl * (1.0 - s))
        dgt_ref[...] = dgt
        dup_ref[...] = dup
        act_ref[...] = sl * up
        dh_sc[...] += _dot(dgt, wg_ref[...]) + _dot(dup, wu_ref[...])

        @pl.when(j == NCHIP - 1)
        def _():
            dxn, dg = _rms_bwd(x_ref[...], g_ref[l:l + 1, :], dh_sc[...])
            dx_ref[...] = do_ref[...] + dxn
            dg_ref[...] += dg

    wspec = pl.BlockSpec((None, f, d), lambda i, j: (j, 0, 0))
    row = pl.BlockSpec((tt, d), lambda i, j: (i, 0))
    sh = pl.BlockSpec((None, tt, f), lambda i, j: (j, i, 0))
    ni = t // tt
    first = lambda: jnp.logical_and(pl.program_id(0) == 0, pl.program_id(1) == 0)
    last = lambda: jnp.logical_and(pl.program_id(0) == ni - 1, pl.program_id(1) == NCHIP - 1)
    outs, couts = _call(
        body, comm, first, last, name=name, grid=(ni, NCHIP),
        in_specs=[row, pl.BlockSpec(gains.shape, lambda i, j: (0, 0)), wspec, wspec, wspec, sh, sh, row],
        args=[x, gains, wg, wu, wd, gts, ups, dout],
        out_specs=[row, pl.BlockSpec((1, d), lambda i, j: (0, 0)), sh, sh, sh, row, row],
        out_shape=[SDS((t, d), F32), SDS((1, d), F32)] + [SDS((NCHIP, t, f), BF16)] * 3 + [SDS((t, d), BF16)] * 2,
        scratch=[pltpu.VMEM((tt, d), F32)], sem=("arbitrary", "arbitrary"))
    return (*outs, couts)


def _rms_fwd(x, gains, l, name):
    t, d = x.shape
    tt = _rt(t)

    def body(x_ref, g_ref, o_ref):
        xv = x_ref[...]
        o_ref[...] = (xv * _rms_r(xv) * g_ref[l:l + 1, :]).astype(BF16)

    row = pl.BlockSpec((tt, d), lambda i: (i, 0))
    return pl.pallas_call(body, name=name, grid=(t // tt,), in_specs=[row, pl.BlockSpec(gains.shape, lambda i: (0, 0))],
                          out_specs=row, out_shape=SDS((t, d), BF16), compiler_params=_cparams(("arbitrary",)))(x, gains)


def _rms_back(x, gains, l, dh, dres, name):
    t, d = x.shape
    tt = _rt(t)

    def body(x_ref, g_ref, dh_ref, dr_ref, dx_ref, dg_ref):
        @pl.when(pl.program_id(0) == 0)
        def _():
            dg_ref[...] = jnp.zeros_like(dg_ref)

        dxn, dg = _rms_bwd(x_ref[...], g_ref[l:l + 1, :], dh_ref[...])
        dx_ref[...] = dr_ref[...] + dxn
        dg_ref[...] += dg

    row = pl.BlockSpec((tt, d), lambda i: (i, 0))
    return pl.pallas_call(body, name=name, grid=(t // tt,),
                          in_specs=[row, pl.BlockSpec(gains.shape, lambda i: (0, 0)), row, row],
                          out_specs=[row, pl.BlockSpec((1, d), lambda i: (0, 0))],
                          out_shape=[SDS((t, d), F32), SDS((1, d), F32)],
                          compiler_params=_cparams(("arbitrary",)))(x, gains, dh, dres)


def _dwconv_fwd(proj, convw, convb, l, o_a, cc, name):
    t = proj.shape[0]
    r = min(256, t)
    nb = cc // HP

    def body(a_ref, gate_ref, w_ref, b_ref, uc_ref, u_sc):
        u_sc[0:CPAD, :] = jnp.zeros((CPAD, HP), F32)

        def fill(ci, carry):
            r0 = pl.multiple_of(ci * r, r)
            u_sc[pl.ds(CPAD + r0, r), :] = a_ref[pl.ds(r0, r), :] * _sig(gate_ref[pl.ds(r0, r), :])
            return carry

        lax.fori_loop(0, t // r, fill, 0)
        w = w_ref[l]
        bias = b_ref[l:l + 1, :]

        def conv(ci, carry):
            r0 = pl.multiple_of(ci * r, r)
            win = u_sc[pl.ds(r0, r + CPAD), :]
            acc = jnp.zeros((r, HP), F32) + bias
            for k in range(KW):
                off = CPAD - (KW - 1) + k
                acc = acc + win[off:off + r, :] * w[k:k + 1, :]
            uc_ref[pl.ds(r0, r), :] = acc
            return carry

        lax.fori_loop(0, t // r, conv, 0)

    col = lambda base: pl.BlockSpec((t, HP), lambda c: (0, base // HP + c))
    return pl.pallas_call(
        body, name=name, grid=(nb,),
        in_specs=[col(o_a), col(o_a + cc), pl.BlockSpec((convw.shape[0], CPAD, HP), lambda c: (0, 0, c)),
                  pl.BlockSpec((convb.shape[0], HP), lambda c: (0, c))],
        out_specs=pl.BlockSpec((t, HP), lambda c: (0, c)), out_shape=SDS((t, cc), F32),
        scratch_shapes=[pltpu.VMEM((t + CPAD, HP), F32)],
        compiler_params=_cparams(("arbitrary",), VMEM_BIG))(proj, proj, convw, convb)


def _dwconv_bwd(duc, proj, convw, l, o_a, cc, name):
    t = proj.shape[0]
    r = min(256, t)
    nb = cc // HP

    def body(d_ref, a_ref, gate_ref, w_ref, du_ref, dw_ref, u_sc, d_sc, dw_sc):
        u_sc[0:CPAD, :] = jnp.zeros((CPAD, HP), F32)
        d_sc[t:t + CPAD, :] = jnp.zeros((CPAD, HP), F32)
        dw_sc[...] = jnp.zeros_like(dw_sc)

        def fill(ci, carry):
            r0 = pl.multiple_of(ci * r, r)
            u_sc[pl.ds(CPAD + r0, r), :] = a_ref[pl.ds(r0, r), :] * _sig(gate_ref[pl.ds(r0, r), :])
            d_sc[pl.ds(r0, r), :] = d_ref[pl.ds(r0, r), :]
            return carry

        lax.fori_loop(0, t // r, fill, 0)
        w = w_ref[l]

        def conv(ci, carry):
            r0 = pl.multiple_of(ci * r, r)
            dwin = d_sc[pl.ds(r0, r + CPAD), :]
            uwin = u_sc[pl.ds(r0, r + CPAD), :]
            dcur = dwin[0:r, :]
            acc = jnp.zeros((r, HP), F32)
            for k in range(KW):
                acc = acc + dwin[KW - 1 - k:KW - 1 - k + r, :] * w[k:k + 1, :]
                off = CPAD - (KW - 1) + k
                part = (dcur * uwin[off:off + r, :]).reshape(r // 8, 8, HP).sum(axis=0)
                dw_sc[8 * k:8 * k + 8, :] += part
            du_ref[pl.ds(r0, r), :] = acc
            return carry

        lax.fori_loop(0, t // r, conv, 0)
        dw_ref[...] = jnp.zeros_like(dw_ref)
        for k in range(KW):
            dw_ref[k:k + 1, :] = jnp.sum(dw_sc[8 * k:8 * k + 8, :], axis=0, keepdims=True)

    col = lambda base: pl.BlockSpec((t, HP), lambda c: (0, base // HP + c))
    return pl.pallas_call(
        body, name=name, grid=(nb,),
        in_specs=[pl.BlockSpec((t, HP), lambda c: (0, c)), col(o_a), col(o_a + cc),
                  pl.BlockSpec((convw.shape[0], CPAD, HP), lambda c: (0, 0, c))],
        out_specs=[pl.BlockSpec((t, HP), lambda c: (0, c)), pl.BlockSpec((CPAD, HP), lambda c: (0, c))],
        out_shape=[SDS((t, cc), F32), SDS((CPAD, cc), F32)],
        scratch_shapes=[pltpu.VMEM((t + CPAD, HP), F32), pltpu.VMEM((t + CPAD, HP), F32), pltpu.VMEM((8 * CPAD, HP), F32)],
        compiler_params=_cparams(("arbitrary",), VMEM_BIG))(duc, proj, proj, convw)


def _ln_parts(uc, g, b):
    mu = jnp.mean(uc, axis=-1, keepdims=True)
    xc = uc - mu
    r = lax.rsqrt(jnp.mean(xc * xc, axis=-1, keepdims=True) + EPS)
    xh = xc * r
    return r, xh, xh * g + b


def _ln_silu(uc, ln_g, ln_b, l, name):
    t, cc = uc.shape
    tt = _rt(t)

    def body(u_ref, g_ref, b_ref, s_ref):
        _, _, yv = _ln_parts(u_ref[...], g_ref[l:l + 1, :], b_ref[l:l + 1, :])
        s_ref[...] = (yv * _sig(yv)).astype(BF16)

    row = pl.BlockSpec((tt, cc), lambda i: (i, 0))
    full = pl.BlockSpec(ln_g.shape, lambda i: (0, 0))
    return pl.pallas_call(body, name=name, grid=(t // tt,), in_specs=[row, full, full], out_specs=row,
                          out_shape=SDS((t, cc), BF16), compiler_params=_cparams(("arbitrary",)))(uc, ln_g, ln_b)


def _ln_silu_bwd(uc, ln_g, ln_b, l, ds, name):
    t, cc = uc.shape
    tt = _rt(t)

    def body(u_ref, g_ref, b_ref, ds_ref, du_ref, dg_ref, db_ref, dcb_ref):
        @pl.when(pl.program_id(0) == 0)
        def _():
            dg_ref[...] = jnp.zeros_like(dg_ref)
            db_ref[...] = jnp.zeros_like(db_ref)
            dcb_ref[...] = jnp.zeros_like(dcb_ref)

        g = g_ref[l:l + 1, :]
        r, xh, yv = _ln_parts(u_ref[...], g, b_ref[l:l + 1, :])
        sy = _sig(yv)
        dy = ds_ref[...] * (sy * (1.0 + yv * (1.0 - sy)))
        dg_ref[...] += jnp.sum(dy * xh, axis=0, keepdims=True)
        db_ref[...] += jnp.sum(dy, axis=0, keepdims=True)
        dxh = dy * g
        du = r * (dxh - jnp.mean(dxh, axis=-1, keepdims=True) - xh * jnp.mean(dxh * xh, axis=-1, keepdims=True))
        du_ref[...] = du
        dcb_ref[...] += jnp.sum(du, axis=0, keepdims=True)

    row = pl.BlockSpec((tt, cc), lambda i: (i, 0))
    full = pl.BlockSpec(ln_g.shape, lambda i: (0, 0))
    vec = pl.BlockSpec((1, cc), lambda i: (0, 0))
    return pl.pallas_call(body, name=name, grid=(t // tt,), in_specs=[row, full, full, row],
                          out_specs=[row, vec, vec, vec], out_shape=[SDS((t, cc), F32)] + [SDS((1, cc), F32)] * 3,
                          compiler_params=_cparams(("arbitrary",)))(uc, ln_g, ln_b, ds)


def _glu_bwd(du, proj, dproj, o_a, cc, name):
    t = du.shape[0]
    tt = _rt(t)

    def body(du_ref, a_ref, gate_ref, prev_ref, o_ref):
        sg = _sig(gate_ref[...])
        dv = du_ref[...]
        o_ref[:, 0:cc] = (dv * sg).astype(BF16)
        o_ref[:, cc:2 * cc] = (dv * a_ref[...] * sg * (1.0 - sg)).astype(BF16)

    return pl.pallas_call(
        body, name=name, grid=(t // tt,),
        in_specs=[pl.BlockSpec((tt, cc), lambda i: (i, 0)), pl.BlockSpec((tt, cc), lambda i: (i, o_a // cc)),
                  pl.BlockSpec((tt, cc), lambda i: (i, o_a // cc + 1)), ANY],
        out_specs=pl.BlockSpec((tt, 2 * cc), lambda i: (i, o_a // (2 * cc))),
        out_shape=SDS(dproj.shape, dproj.dtype), input_output_aliases={3: 0},
        compiler_params=_cparams(("arbitrary",)))(du, proj, proj, dproj)


def _rope(v, cs, s1, s2):
    return v * cs + pltpu.roll(v, HP - ROPE // 2, 1) * s1 + pltpu.roll(v, ROPE // 2, 1) * s2


def _rope_t(dv, cs, s1, s2):
    return dv * cs + pltpu.roll(dv * s1, ROPE // 2, 1) + pltpu.roll(dv * s2, HP - ROPE // 2, 1)


def _head_norm(v, g):
    r = lax.rsqrt(jnp.sum(v * v, axis=-1, keepdims=True) * (1.0 / QK) + EPS)
    return v * r * g, r


def _head_norm_bwd(v, r, g, dy):
    xh = v * r
    dg = jnp.sum(dy * xh, axis=0, keepdims=True)
    dxh = dy * g
    dx = r * (dxh - xh * (jnp.sum(dxh * xh, axis=-1, keepdims=True) * (1.0 / QK)))
    return dx, dg


def _mla_specs(tt, lay, cq_norm, ckv_norm, qn, kn, wuq, wukv):
    ql, kvl = cq_norm.shape[1], ckv_norm.shape[1]
    full = lambda a: pl.BlockSpec(a.shape, lambda i: (0,) * a.ndim)
    tab = pl.BlockSpec((tt, HP), lambda i: (i, 0))
    return [pl.BlockSpec((tt, ql), lambda i: (i, lay["cq"] // ql)),
            pl.BlockSpec((tt, kvl), lambda i: (i, lay["ckv"] // kvl)),
            pl.BlockSpec((tt, HP), lambda i: (i, lay["kr"] // HP)),
            full(cq_norm), full(ckv_norm), full(qn), full(kn), full(wuq), full(wukv), tab, tab, tab]


def _mla_pre_fwd(proj, lay, l, cq_norm, ckv_norm, qn, kn, wuq, wukv, tabs, name):
    t = proj.shape[0]
    tt = _rt(t)
    hw = N_HEADS * HP

    def body(cq_ref, ckv_ref, kr_ref, gq_ref, gkv_ref, qn_ref, kn_ref, wq_ref, wkv_ref, c_ref, s1_ref, s2_ref,
             q_ref, k_ref, v_ref):
        cq = cq_ref[...]
        cqn = (cq * _rms_r(cq) * gq_ref[l:l + 1, :]).astype(BF16)
        ckv = ckv_ref[...]
        ckvn = (ckv * _rms_r(ckv) * gkv_ref[l:l + 1, :]).astype(BF16)
        qraw = _dot_nt(cqn, wq_ref[...])
        kv = _dot(ckvn, wkv_ref[...])
        v_ref[...] = kv.astype(BF16)
        lane = lax.broadcasted_iota(jnp.int32, (tt, HP), 1)
        krs = pltpu.roll(jnp.where(lane < ROPE, kr_ref[...], 0.0), NOPE, 1)
        cs, s1, s2 = c_ref[...], s1_ref[...], s2_ref[...]
        gq, gk = qn_ref[l:l + 1, :], kn_ref[l:l + 1, :]
        for h in range(N_HEADS):
            sl = slice(h * HP, (h + 1) * HP)
            qh, _ = _head_norm(qraw[:, sl], gq)
            q_ref[:, sl] = (_rope(qh, cs, s1, s2) * QK ** -0.5).astype(BF16)
            kh, _ = _head_norm(jnp.where(lane < NOPE, kv[:, sl], krs), gk)
            k_ref[:, sl] = _rope(kh, cs, s1, s2).astype(BF16)

    row = pl.BlockSpec((tt, hw), lambda i: (i, 0))
    return pl.pallas_call(
        body, name=name, grid=(t // tt,),
        in_specs=_mla_specs(tt, lay, cq_norm, ckv_norm, qn, kn, wuq, wukv),
        out_specs=[row, row, row], out_shape=[SDS((t, hw), BF16)] * 3,
        compiler_params=_cparams(("arbitrary",), VMEM_BIG))(
            proj, proj, proj, cq_norm, ckv_norm, qn, kn, wuq, wukv, *tabs)


def _mla_pre_bwd(proj, lay, l, cq_norm, ckv_norm, qn, kn, wuq, wukv, tabs, dq, dk, dv, dproj, name):
    t = proj.shape[0]
    tt = _rt(t)
    hw = N_HEADS * HP
    ql, kvl = cq_norm.shape[1], ckv_norm.shape[1]
    wm = lay["wm"]

    def body(cq_ref, ckv_ref, kr_ref, gq_ref, gkv_ref, qn_ref, kn_ref, wq_ref, wkv_ref, c_ref, s1_ref, s2_ref,
             dq_ref, dk_ref, dv_ref, prev_ref,
             o_ref, dqr_ref, dkv_ref, cqn_ref, ckvn_ref, dgq_ref, dgkv_ref, dqn_ref, dkn_ref):
        @pl.when(pl.program_id(0) == 0)
        def _():
            for ref in (dgq_ref, dgkv_ref, dqn_ref, dkn_ref):
                ref[...] = jnp.zeros_like(ref)

        cq = cq_ref[...]
        cqn = (cq * _rms_r(cq) * gq_ref[l:l + 1, :]).astype(BF16)
        ckv = ckv_ref[...]
        ckvn = (ckv * _rms_r(ckv) * gkv_ref[l:l + 1, :]).astype(BF16)
        cqn_ref[...] = cqn
        ckvn_ref[...] = ckvn
        qraw = _dot_nt(cqn, wq_ref[...])
        kv = _dot(ckvn, wkv_ref[...])
        lane = lax.broadcasted_iota(jnp.int32, (tt, HP), 1)
        krs = pltpu.roll(jnp.where(lane < ROPE, kr_ref[...], 0.0), NOPE, 1)
        cs, s1, s2 = c_ref[...], s1_ref[...], s2_ref[...]
        gq, gk = qn_ref[l:l + 1, :], kn_ref[l:l + 1, :]
        dkr = jnp.zeros((tt, HP), F32)
        dgq = jnp.zeros((1, HP), F32)
        dgk = jnp.zeros((1, HP), F32)
        for h in range(N_HEADS):
            sl = slice(h * HP, (h + 1) * HP)
            qh = qraw[:, sl]
            _, rq = _head_norm(qh, gq)
            dqh, dg = _head_norm_bwd(qh, rq, gq, _rope_t(dq_ref[:, sl] * QK ** -0.5, cs, s1, s2))
            dgq = dgq + dg
            dqr_ref[:, sl] = dqh.astype(BF16)
            kp = jnp.where(lane < NOPE, kv[:, sl], krs)
            _, rk = _head_norm(kp, gk)
            dkp, dg = _head_norm_bwd(kp, rk, gk, _rope_t(dk_ref[:, sl], cs, s1, s2))
            dgk = dgk + dg
            dkv_ref[:, sl] = (jnp.where(lane < NOPE, dkp, 0.0) + dv_ref[:, sl]).astype(BF16)
            dkr = dkr + dkp
        dqn_ref[...] += dgq
        dkn_ref[...] += dgk
        dkr = jnp.where(lane < ROPE, pltpu.roll(dkr, HP - NOPE, 1), 0.0)
        dcq, dg = _rms_bwd(cq, gq_ref[l:l + 1, :], _dot(dqr_ref[...], wq_ref[...]))
        dgq_ref[...] += dg
        dckv, dg = _rms_bwd(ckv, gkv_ref[l:l + 1, :], _dot_nt(dkv_ref[...], wkv_ref[...]))
        dgkv_ref[...] += dg
        o_ref[:, 0:ql] = dcq.astype(BF16)
        o_ref[:, ql:ql + kvl] = dckv.astype(BF16)
        o_ref[:, ql + kvl:ql + kvl + HP] = dkr.astype(BF16)
        o_ref[:, ql + kvl + HP:wm] = jnp.zeros((tt, wm - ql - kvl - HP), BF16)

    row = pl.BlockSpec((tt, hw), lambda i: (i, 0))
    vec = lambda n: pl.BlockSpec((1, n), lambda i: (0, 0))
    return pl.pallas_call(
        body, name=name, grid=(t // tt,),
        in_specs=_mla_specs(tt, lay, cq_norm, ckv_norm, qn, kn, wuq, wukv) + [row, row, row, ANY],
        out_specs=[pl.BlockSpec((tt, wm), lambda i: (i, lay["cq"] // wm)), row, row,
                   pl.BlockSpec((tt, ql), lambda i: (i, 0)), pl.BlockSpec((tt, kvl), lambda i: (i, 0)),
                   vec(ql), vec(kvl), vec(HP), vec(HP)],
        out_shape=[SDS(dproj.shape, dproj.dtype), SDS((t, hw), BF16), SDS((t, hw), BF16), SDS((t, ql), BF16),
                   SDS((t, kvl), BF16), SDS((1, ql), F32), SDS((1, kvl), F32), SDS((1, HP), F32), SDS((1, HP), F32)],
        input_output_aliases={15: 0},
        compiler_params=_cparams(("arbitrary",), VMEM_BIG))(
            proj, proj, proj, cq_norm, ckv_norm, qn, kn, wuq, wukv, *tabs, dq, dk, dv, dproj)


def _comb_fwd(proj, gate_bias, l, yc, ym, name):
    t, d = yc.shape
    tt = _rt(t)

    def body(p_ref, b_ref, yc_ref, ym_ref, y_ref):
        b = b_ref[l]
        g0 = _sig(p_ref[:, 0:d] + b[0:1, :])
        g1 = _sig(p_ref[:, d:2 * d] + b[1:2, :])
        y_ref[...] = (g0 * yc_ref[...] + g1 * ym_ref[...]).astype(BF16)

    row = pl.BlockSpec((tt, d), lambda i: (i, 0))
    return pl.pallas_call(
        body, name=name, grid=(t // tt,),
        in_specs=[pl.BlockSpec((tt, 2 * d), lambda i: (i, 0)), pl.BlockSpec(gate_bias.shape, lambda i: (0, 0, 0)), row, row],
        out_specs=row, out_shape=SDS((t, d), BF16), compiler_params=_cparams(("arbitrary",)))(proj, gate_bias, yc, ym)


def _comb_bwd(proj, gate_bias, l, yc, ym, dy, dp_cols, name):
    t, d = yc.shape
    tt = _rt(t)

    def body(p_ref, b_ref, yc_ref, ym_ref, dy_ref, dyc_ref, dym_ref, dp_ref, db_ref):
        @pl.when(pl.program_id(0) == 0)
        def _():
            db_ref[...] = jnp.zeros_like(db_ref)

        b = b_ref[l]
        dyv = dy_ref[...]
        g0 = _sig(p_ref[:, 0:d] + b[0:1, :])
        g1 = _sig(p_ref[:, d:2 * d] + b[1:2, :])
        dyc_ref[...] = (dyv * g0).astype(BF16)
        dym_ref[...] = (dyv * g1).astype(BF16)
        dg0 = dyv * yc_ref[...] * g0 * (1.0 - g0)
        dg1 = dyv * ym_ref[...] * g1 * (1.0 - g1)
        dp_ref[:, 0:d] = dg0.astype(BF16)
        dp_ref[:, d:2 * d] = dg1.astype(BF16)
        db_ref[0:1, :] += jnp.sum(dg0, axis=0, keepdims=True)
        db_ref[1:2, :] += jnp.sum(dg1, axis=0, keepdims=True)

    row = pl.BlockSpec((tt, d), lambda i: (i, 0))
    wide = pl.BlockSpec((tt, 2 * d), lambda i: (i, 0))
    return pl.pallas_call(
        body, name=name, grid=(t // tt,),
        in_specs=[wide, pl.BlockSpec(gate_bias.shape, lambda i: (0, 0, 0)), row, row, row],
        out_specs=[row, row, wide, pl.BlockSpec((2, d), lambda i: (0, 0))],
        out_shape=[SDS((t, d), BF16), SDS((t, d), BF16), SDS((t, dp_cols), BF16), SDS((2, d), F32)],
        compiler_params=_cparams(("arbitrary",)))(proj, gate_bias, yc, ym, dy)


def _loss_grad(y, target, name):
    t, d = y.shape
    tt = _rt(t)
    nt = t // tt

    def body(y_ref, t_ref, dy_ref, loss_ref, acc_sc):
        i = pl.program_id(0)

        @pl.when(i == 0)
        def _():
            acc_sc[...] = jnp.zeros_like(acc_sc)

        diff = y_ref[...] - t_ref[...]
        dy_ref[...] = diff * (1.0 / d)
        acc_sc[...] += jnp.sum(diff * diff, axis=0, keepdims=True)

        @pl.when(i == nt - 1)
        def _():
            tot = jnp.sum(acc_sc[...], axis=1, keepdims=True) * (0.5 / d)
            loss_ref[...] = jnp.broadcast_to(tot, (1, HP))

    row = pl.BlockSpec((tt, d), lambda i: (i, 0))
    return pl.pallas_call(body, name=name, grid=(nt,), in_specs=[row, row],
                          out_specs=[row, pl.BlockSpec((1, HP), lambda i: (0, 0))],
                          out_shape=[SDS((t, d), F32), SDS((1, HP), F32)],
                          scratch_shapes=[pltpu.VMEM((1, d), F32)],
                          compiler_params=_cparams(("arbitrary",)))(y, target)


def _chunk_mask(tq):
    rows = lax.broadcasted_iota(jnp.int32, (tq, tq), 0) // CHUNK
    cols = lax.broadcasted_iota(jnp.int32, (tq, tq), 1) // CHUNK
    return cols <= rows


NEG = -1e30


def _flash_fwd(q, k, v, name, comm=None):
    t = q.shape[0]
    tq = _rt(t)
    nq = t // tq
    rep = tq // HP

    def body(q_ref, k_ref, v_ref, o_ref, lse_ref):
        qi = pl.program_id(1)
        qv = q_ref[...]

        def step(ki, carry, masked):
            m_prev, l_prev, acc = carry
            r0 = pl.multiple_of(ki * tq, tq)
            s = _dot_nt(qv, k_ref[pl.ds(r0, tq), :])
            if masked:
                s = jnp.where(_chunk_mask(tq), s, NEG)
            m_new = jnp.maximum(m_prev, jnp.max(s, axis=-1, keepdims=True))
            a = jnp.exp(m_prev - m_new)
            p = jnp.exp(s - jnp.tile(m_new, (1, rep)))
            l_new = a * l_prev + jnp.sum(p, axis=-1, keepdims=True)
            acc = a * acc + _dot(p.astype(BF16), v_ref[pl.ds(r0, tq), :])
            return m_new, l_new, acc

        init = (jnp.full((tq, HP), NEG, F32), jnp.zeros((tq, HP), F32), jnp.zeros((tq, HP), F32))
        carry = lax.fori_loop(0, qi, lambda ki, cr: step(ki, cr, False), init)
        m_fin, l_fin, acc = step(qi, carry, True)
        o_ref[...] = (acc / l_fin).astype(BF16)
        lse_ref[...] = m_fin + jnp.log(l_fin)

    qspec = pl.BlockSpec((tq, HP), lambda h, qi: (qi, h))
    head = pl.BlockSpec((t, HP), lambda h, qi: (0, h))
    first = lambda: jnp.logical_and(pl.program_id(0) == 0, pl.program_id(1) == 0)
    last = lambda: jnp.logical_and(pl.program_id(0) == N_HEADS - 1, pl.program_id(1) == nq - 1)
    outs, couts = _call(
        body, comm, first, last, name=name, grid=(N_HEADS, nq), in_specs=[qspec, head, head], args=[q, k, v],
        out_specs=[qspec, pl.BlockSpec((None, tq, HP), lambda h, qi: (h, qi, 0))],
        out_shape=[SDS(q.shape, BF16), SDS((N_HEADS, t, HP), F32)], scratch=[], sem=("arbitrary",) * 2)
    return (*outs, couts)


def _flash_bwd(q, k, v, do, o, lse, name, comm=None):
    t = q.shape[0]
    tq = _rt(t)
    nq = t // tq
    rep = tq // HP

    def body(q_ref, k_ref, v_ref, do_ref, o_ref, lse_ref, dq_ref, dk_ref, dv_ref, delta_sc):
        def prep(qi, carry):
            r0 = pl.multiple_of(qi * tq, tq)
            rows = pl.ds(r0, tq)
            dlt = jnp.sum(do_ref[rows, :].astype(F32) * o_ref[rows, :].astype(F32), axis=-1, keepdims=True)
            delta_sc[rows, :] = jnp.broadcast_to(dlt, (tq, HP))
            dq_ref[rows, :] = jnp.zeros((tq, HP), F32)
            return carry

        lax.fori_loop(0, nq, prep, 0)

        def keys(ki, carry0):
            krows = pl.ds(pl.multiple_of(ki * tq, tq), tq)
            kt, vt = k_ref[krows, :], v_ref[krows, :]

            def step(qi, carry, masked):
                dk_acc, dv_acc = carry
                rows = pl.ds(pl.multiple_of(qi * tq, tq), tq)
                qt, dot_ = q_ref[rows, :], do_ref[rows, :]
                s = _dot_nt(qt, kt)
                if masked:
                    s = jnp.where(_chunk_mask(tq), s, NEG)
                p = jnp.exp(s - jnp.tile(lse_ref[rows, :], (1, rep)))
                ds = (p * (_dot_nt(dot_, vt) - jnp.tile(delta_sc[rows, :], (1, rep)))).astype(BF16)
                dv_acc = dv_acc + _dot_tn(p.astype(BF16), dot_)
                dk_acc = dk_acc + _dot_tn(ds, qt)
                dq_ref[rows, :] += _dot(ds, kt)
                return dk_acc, dv_acc

            zero = jnp.zeros((tq, HP), F32)
            carry = step(ki, (zero, zero), True)
            dk_acc, dv_acc = lax.fori_loop(ki + 1, nq, lambda qi, cr: step(qi, cr, False), carry)
            dk_ref[krows, :] = dk_acc
            dv_ref[krows, :] = dv_acc
            return carry0

        lax.fori_loop(0, nq, keys, 0)

    head = pl.BlockSpec((t, HP), lambda h: (0, h))
    outs, couts = _call(
        body, comm, lambda: pl.program_id(0) == 0, lambda: pl.program_id(0) == N_HEADS - 1, name=name, grid=(N_HEADS,),
        in_specs=[head] * 5 + [pl.BlockSpec((None, t, HP), lambda h: (h, 0, 0))], args=[q, k, v, do, o, lse],
        out_specs=[head] * 3, out_shape=[SDS(q.shape, F32)] * 3, scratch=[pltpu.VMEM((t, HP), F32)], sem=("arbitrary",))
    return (*outs, couts)


def _add_cast(gs, rs, place, name):
    n = len(gs)
    _, r, c = gs[0].shape
    tr, tc = _tile2(r, c // 2, 3 * n, 16)
    nct = c // 2 // tc

    def body(place_ref, *refs):
        for a in range(n):
            refs[2 * n + a][...] = (refs[a][...] + refs[n + a][...]).astype(BF16)

    gspec = pl.BlockSpec((None, tr, tc), lambda j, i, k, pr: (j, i, pr[0] * nct + k))
    rspec = pl.BlockSpec((None, tr, tc), lambda j, i, k, pr: (j, i, k))
    grid_spec = pltpu.PrefetchScalarGridSpec(num_scalar_prefetch=1, grid=(NCHIP, r // tr, nct),
                                             in_specs=[gspec] * n + [rspec] * n, out_specs=[rspec] * n)
    return pl.pallas_call(body, name=name, grid_spec=grid_spec, out_shape=[SDS((NCHIP, r, c // 2), BF16)] * n,
                          compiler_params=_cparams(("arbitrary",) * 3, VMEM_BIG))(place, *gs, *rs)


def _sum_chips(ss, qs, place, l, nl, prevs, name):
    n = len(ss)
    _, r, h = ss[0].shape
    tr, tc = _tile2(r, h, 3 * n, 16)
    nct = h // tc

    def body(place_ref, *refs):
        for a in range(n):
            acc = refs[n + a][...].astype(F32)
            for kk in range(NCHIP - 1):
                acc = acc + refs[a][kk].astype(F32)
            refs[-n + a][...] = acc

    in_specs = ([pl.BlockSpec((NCHIP - 1, tr, tc), lambda i, k, pr: (0, i, k))] * n
                + [pl.BlockSpec((None, tr, tc), lambda i, k, pr: (pr[1], i, k))] * n)
    args = [*ss, *qs]
    aliases = {}
    if prevs is not None:
        aliases = {1 + len(args) + a: a for a in range(n)}
        in_specs += [ANY] * n
        args += list(prevs)
    grid_spec = pltpu.PrefetchScalarGridSpec(
        num_scalar_prefetch=1, grid=(r // tr, nct), in_specs=in_specs,
        out_specs=[pl.BlockSpec((None, tr, tc), lambda i, k, pr: (l, i, pr[0] * nct + k))] * n)
    return pl.pallas_call(body, name=name, grid_spec=grid_spec, out_shape=[SDS((nl, r, 2 * h), F32)] * n,
                          input_output_aliases=aliases,
                          compiler_params=_cparams(("arbitrary",) * 2, VMEM_BIG))(place, *args)


def _cast_place(ws, place, name):
    n = len(ws)
    nl, r, c = ws[0].shape
    tr, tc = _tile2(r, c, 2 * n * nl, 16)

    def body(place_ref, *refs):
        for a in range(n * nl):
            refs[n * nl + a][...] = refs[a][...].astype(BF16)

    in_specs = [pl.BlockSpec((None, tr, tc), functools.partial(lambda l, i, k, pr: (l, i, k), l))
                for _ in range(n) for l in range(nl)]
    grid_spec = pltpu.PrefetchScalarGridSpec(
        num_scalar_prefetch=1, grid=(r // tr, c // tc), in_specs=in_specs,
        out_specs=[pl.BlockSpec((None, tr, tc), lambda i, k, pr: (pr[1], i, k))] * (n * nl))
    outs = pl.pallas_call(body, name=name, grid_spec=grid_spec, out_shape=[SDS((NCHIP, r, c), BF16)] * (n * nl),
                          compiler_params=_cparams(("arbitrary",) * 2, VMEM_BIG))(
                              place, *[w for w in ws for _ in range(nl)])
    return [outs[a * nl:(a + 1) * nl] for a in range(n)]


def _adamw(ws, gs, ms, vs, name):
    n = len(ws)
    r, c = ws[0].shape
    tr, tc = _tile2(r, c, 7 * n, 8)
    c1, c2 = 1.0 / (1.0 - B1 ** STEP), 1.0 / (1.0 - B2 ** STEP)

    def body(*refs):
        for a in range(n):
            w, g, m, v = (refs[kk * n + a][...] for kk in range(4))
            m2 = B1 * m + (1.0 - B1) * g
            v2 = B2 * v + (1.0 - B2) * (g * g)
            refs[4 * n + a][...] = -LR * ((m2 * c1) / (jnp.sqrt(v2 * c2) + EPS_ADAM) + WD * w)
            refs[5 * n + a][...] = m2
            refs[6 * n + a][...] = v2

    blk = pl.BlockSpec((tr, tc), lambda i, k: (i, k))
    outs = pl.pallas_call(body, name=name, grid=(r // tr, c // tc), in_specs=[blk] * (4 * n),
                          out_specs=[blk] * (3 * n), out_shape=[SDS((r, c), F32)] * (3 * n),
                          compiler_params=_cparams(("arbitrary",) * 2, VMEM_BIG))(*ws, *gs, *ms, *vs)
    return outs[:n], outs[n:2 * n], outs[2 * n:]


def _place():
    x, y, c = lax.axis_index("x"), lax.axis_index("y"), lax.axis_index("c")
    return x, y, c, [(1 - x, y), (x, 1 - y), (1 - x, 1 - y)]


def _rcopy(src, dst, ssem, rsem, k, dev):
    return pltpu.make_async_remote_copy(src_ref=src, dst_ref=dst, send_sem=ssem.at[k], recv_sem=rsem.at[k],
                                        device_id=dev, device_id_type=MESH)


def _half(ref, lead, cc):
    h = ref.shape[-1] // 2
    return ref.at[(*lead, slice(None), pl.ds(cc * h, h))]


def _per_core(fn):
    c = lax.axis_index("c")
    for cc in (0, 1):
        pl.when(c == cc)(functools.partial(fn, cc))


def _gather_comm(bufs):
    n = len(bufs)

    def plan(couts, ssem, rsem, cc):
        x, y, _, peers = _place()
        me, sib = 2 * x + y, (x, y, 1 - cc)
        send, recv, fwd, recv2 = [], [], [], []
        for a in range(n):
            for kk, (px, py) in enumerate(peers):
                mine = _half(couts[a], (me,), cc)
                got = _half(couts[a], (2 * px + py,), cc)
                other = _half(couts[a], (2 * px + py,), 1 - cc)
                send.append(_rcopy(mine, mine, ssem, rsem, a * 6 + kk, (px, py, cc)))
                recv.append(_rcopy(got, got, ssem, rsem, a * 6 + kk, (px, py, cc)))
                fwd.append(_rcopy(got, got, ssem, rsem, a * 6 + 3 + kk, sib))
                recv2.append(_rcopy(other, other, ssem, rsem, a * 6 + 3 + kk, sib))
        return send, recv, fwd, recv2

    def start(cins, couts, ssem, rsem):
        def go(cc):
            for d in plan(couts, ssem, rsem, cc)[0]:
                d.start()

        _per_core(go)

    def finish(cins, couts, ssem, rsem):
        def go(cc):
            send, recv, fwd, recv2 = plan(couts, ssem, rsem, cc)
            for dr, df in zip(recv, fwd):
                dr.wait_recv()
                df.start()
            for d in recv2:
                d.wait_recv()
            for d in send + fwd:
                d.wait_send()

        _per_core(go)

    return _Comm(bufs, [SDS(b.shape, b.dtype) for b in bufs], {a: a for a in range(n)}, 6 * n, start, finish)


def _pair_comm(gs):
    n = len(gs)
    halves = [g.shape[-1] // 2 for g in gs]

    def plan(cins, couts, ssem, rsem, cc):
        x, y, _, _ = _place()
        return [_rcopy(cins[a].at[:, :, pl.ds((1 - cc) * halves[a], halves[a])], couts[a], ssem, rsem, a, (x, y, 1 - cc))
                for a in range(n)]

    def start(cins, couts, ssem, rsem):
        def go(cc):
            for d in plan(cins, couts, ssem, rsem, cc):
                d.start()

        _per_core(go)

    def finish(cins, couts, ssem, rsem):
        def go(cc):
            ds = plan(cins, couts, ssem, rsem, cc)
            for d in ds:
                d.wait_recv()
            for d in ds:
                d.wait_send()

        _per_core(go)

    return _Comm(gs, [SDS(g.shape[:-1] + (g.shape[-1] // 2,), g.dtype) for g in gs], {}, n, start, finish)


def _chips_comm(qs):
    n = len(qs)

    def plan(cins, couts, ssem, rsem):
        x, y, c, peers = _place()
        return [_rcopy(cins[a].at[2 * px + py], couts[a].at[kk], ssem, rsem, a * 3 + kk, (px, py, c))
                for a in range(n) for kk, (px, py) in enumerate(peers)]

    def start(cins, couts, ssem, rsem):
        for d in plan(cins, couts, ssem, rsem):
            d.start()

    def finish(cins, couts, ssem, rsem):
        ds = plan(cins, couts, ssem, rsem)
        for d in ds:
            d.wait_recv()
        for d in ds:
            d.wait_send()

    return _Comm(qs, [SDS((NCHIP - 1,) + q.shape[1:], q.dtype) for q in qs], {}, 3 * n, start, finish)


def _share_comm(fs, l):
    n = len(fs)

    def plan(couts, ssem, rsem, cc, which):
        x, y, _, _ = _place()
        out = []
        for a in range(n):
            piece = _half(couts[a], (l,), which)
            out.append(_rcopy(piece, piece, ssem, rsem, a, (x, y, 1 - cc)))
        return out

    def start(cins, couts, ssem, rsem):
        def go(cc):
            for d in plan(couts, ssem, rsem, cc, cc):
                d.start()

        _per_core(go)

    def finish(cins, couts, ssem, rsem):
        def go(cc):
            for d in plan(couts, ssem, rsem, cc, 1 - cc):
                d.wait_recv()
            for d in plan(couts, ssem, rsem, cc, cc):
                d.wait_send()

        _per_core(go)

    return _Comm(fs, [SDS(f.shape, f.dtype) for f in fs], {a: a for a in range(n)}, n, start, finish)


def _exchange(buf, reduce, name):
    r, w = buf.shape

    def body(in_ref, out_ref, recv_sc, ssem, rsem):
        x, y, c, _ = _place()
        me = 4 * x + 2 * y + c
        sends = []
        for rel in range(1, N_DEV):
            dev = (1 - x if rel & 4 else x, 1 - y if rel & 2 else y, 1 - c if rel & 1 else c)
            d = _rcopy(in_ref, recv_sc.at[me], ssem, rsem, rel - 1, dev)
            d.start()
            sends.append(d)
        recv_sc[me] = in_ref[...]
        for rel in range(1, N_DEV):
            px, py, pc = (1 - x if rel & 4 else x, 1 - y if rel & 2 else y, 1 - c if rel & 1 else c)
            slot = recv_sc.at[4 * px + 2 * py + pc]
            _rcopy(slot, slot, ssem, rsem, rel - 1, (px, py, pc)).wait_recv()
        for d in sends:
            d.wait_send()
        if reduce:
            acc = recv_sc[0]
            for dv in range(1, N_DEV):
                acc = acc + recv_sc[dv]
            out_ref[...] = acc
        else:
            out_ref[...] = recv_sc[...]

    vm = pl.BlockSpec(memory_space=pltpu.VMEM)
    return pl.pallas_call(
        body, name=name, in_specs=[vm], out_specs=vm,
        out_shape=SDS((r, w) if reduce else (N_DEV, r, w), F32),
        scratch_shapes=[pltpu.VMEM((N_DEV, r, w), F32), pltpu.SemaphoreType.DMA((N_DEV - 1,)),
                        pltpu.SemaphoreType.DMA((N_DEV - 1,))],
        compiler_params=_cparams(None, VMEM_BIG))(buf)


def _cols_full(g):
    _, k, ns = g.shape
    return g.transpose(1, 0, 2).reshape(k, NCHIP * ns)


def _cols_shards(w):
    k, n = w.shape
    return w.reshape(k, NCHIP, n // NCHIP).transpose(1, 0, 2)


def _rows_pad(rows, width):
    out = jnp.concatenate([jnp.pad(a, ((0, 0), (0, width - a.shape[1]))) for a in rows], axis=0)
    return jnp.pad(out, ((0, -out.shape[0] % 8), (0, 0)))


def kernel(x, positions, ffn1_norm, ffn1_w_gate, ffn1_w_up, ffn1_w_down, mix_norm, w_in, gate_bias, conv_w, conv_b, conv_ln_g, conv_ln_b, w_conv_out, cq_norm, ckv_norm, w_uq, w_ukv, q_norm, k_norm, w_mla_out, w_out, ffn2_norm, ffn2_w_gate, ffn2_w_up, ffn2_w_down, loss_target, m_ffn1_norm, m_ffn1_w_gate, m_ffn1_w_up, m_ffn1_w_down, m_mix_norm, m_w_in, m_gate_bias, m_conv_w, m_conv_b, m_conv_ln_g, m_conv_ln_b, m_w_conv_out, m_cq_norm, m_ckv_norm, m_w_uq, m_w_ukv, m_q_norm, m_k_norm, m_w_mla_out, m_w_out, m_ffn2_norm, m_ffn2_w_gate, m_ffn2_w_up, m_ffn2_w_down, v_ffn1_norm, v_ffn1_w_gate, v_ffn1_w_up, v_ffn1_w_down, v_mix_norm, v_w_in, v_gate_bias, v_conv_w, v_conv_b, v_conv_ln_g, v_conv_ln_b, v_w_conv_out, v_cq_norm, v_ckv_norm, v_w_uq, v_w_ukv, v_q_norm, v_k_norm, v_w_mla_out, v_w_out, v_ffn2_norm, v_ffn2_w_gate, v_ffn2_w_up, v_ffn2_w_down):
    names = ["ffn1_norm", "ffn1_w_gate", "ffn1_w_up", "ffn1_w_down", "mix_norm", "w_in", "gate_bias", "conv_w",
             "conv_b", "conv_ln_g", "conv_ln_b", "w_conv_out", "cq_norm", "ckv_norm", "w_uq", "w_ukv", "q_norm",
             "k_norm", "w_mla_out", "w_out", "ffn2_norm", "ffn2_w_gate", "ffn2_w_up", "ffn2_w_down"]
    env = dict(locals())
    turned = ("ffn1_w_gate", "ffn1_w_up", "ffn2_w_gate", "ffn2_w_up", "w_in", "w_uq")
    view = lambda nm, a: jnp.swapaxes(a, 1, 2) if nm in turned else a
    wts = {nm: view(nm, env[nm]) for nm in names}
    mom = {nm: view(nm, env["m_" + nm]) for nm in names}
    var = {nm: view(nm, env["v_" + nm]) for nm in names}

    t, d = x.shape[1], x.shape[2]
    nl = ffn1_norm.shape[0]
    cc = conv_b.shape[1]
    ql, kvl = cq_norm.shape[1], ckv_norm.shape[1]
    vw = N_HEADS * V_DIM
    hw = N_HEADS * HP
    lay = {"a": 2 * d, "cq": 2 * d + 2 * cc, "ckv": 2 * d + 2 * cc + ql, "kr": 2 * d + 2 * cc + ql + kvl,
           "wm": ql + kvl + 2 * HP}
    dp = lay["cq"] + lay["wm"]
    nat_g = 2 * cc + ql + kvl + ROPE
    assert lay["cq"] % lay["wm"] == 0 and lay["cq"] % ql == 0 and lay["ckv"] % kvl == 0 and lay["a"] % (2 * cc) == 0
    assert cc % HP == 0 and d % HP == 0 and t % (2 * CHUNK) == 0 and w_in.shape[2] * NCHIP == nat_g + 2 * d
    assert nl == 2

    x0, target = x[0], loss_target[0]
    chip = 2 * lax.axis_index("x") + lax.axis_index("y")
    place = jnp.stack([lax.axis_index("c"), chip]).astype(jnp.int32)

    inv_freq = ROPE_THETA ** (-jnp.arange(0, ROPE, 2, dtype=F32) / ROPE)
    ang = positions[0].astype(F32)[:, None] * inv_freq
    cos, sin, z = jnp.cos(ang), jnp.sin(ang), jnp.zeros((t, ROPE // 2), F32)
    tabs = (jnp.concatenate([jnp.ones((t, NOPE), F32), cos, cos, jnp.zeros((t, HP - QK), F32)], axis=1),
            jnp.concatenate([jnp.zeros((t, NOPE), F32), -sin, z, jnp.zeros((t, HP - QK), F32)], axis=1),
            jnp.concatenate([jnp.zeros((t, NOPE), F32), z, sin, jnp.zeros((t, HP - QK), F32)], axis=1))

    big = ["ffn1_w_gate", "ffn1_w_up", "ffn1_w_down", "ffn2_w_gate", "ffn2_w_up", "ffn2_w_down",
           "w_in", "w_conv_out", "w_uq", "w_ukv", "w_mla_out", "w_out"]
    ffn1_w, ffn2_w, mix_w = big[0:3], big[3:6], big[6:]
    bufs, like = {}, {}
    for nm in big:
        like.setdefault(wts[nm].shape, []).append(nm)
    for gi, grp in enumerate(like.values()):
        for nm, per_layer in zip(grp, _cast_place([wts[nm] for nm in grp], place, f"cast_place_{gi}")):
            for l in range(nl):
                bufs[nm, l] = per_layer[l]

    def gather(keys):
        return _gather_comm([bufs[key] for key in keys])

    def landed(keys, outs):
        for key, o in zip(keys, outs):
            bufs[key] = o

    chunk = lambda nms, l: [(nm, l) for nm in nms]
    first_keys = chunk(ffn1_w, 0)
    landed(first_keys, _comm_call(gather(first_keys), "gather_first"))
    ride = {("ffn1", 0): chunk(mix_w, 0), ("flash", 0): chunk(ffn2_w, 0) + chunk(ffn1_w, 1),
            ("ffn2", 0): chunk(mix_w, 1), ("ffn1", 1): chunk(ffn2_w, 1)}

    def riding(kind, l):
        keys = ride.get((kind, l))
        return keys or [], (None if keys is None else gather(keys))

    def mixer_weights(l):
        w_in_nat = bufs["w_in", l].reshape(nat_g + 2 * d, d)
        w_in_k = jnp.concatenate([w_in_nat[nat_g:], w_in_nat[:nat_g], jnp.zeros((dp - nat_g - 2 * d, d), BF16)], axis=0)
        w_uq_k = jnp.pad(bufs["w_uq", l].reshape(N_HEADS, QK, ql), ((0, 0), (0, HP - QK), (0, 0))).reshape(hw, ql)
        w_mo_k = jnp.pad(_cols_full(bufs["w_mla_out", l]).reshape(N_HEADS, V_DIM, d),
                         ((0, 0), (HP - V_DIM, 0), (0, 0))).reshape(hw, d)
        return dict(w_in=w_in_k, w_co=_cols_full(bufs["w_conv_out", l]), w_uq=w_uq_k,
                    w_ukv=_cols_full(bufs["w_ukv", l]), w_mo=w_mo_k, w_out=bufs["w_out", l].reshape(d, d))

    qn_k = jnp.pad(q_norm, ((0, 0), (0, HP - QK)))
    kn_k = jnp.pad(k_norm, ((0, 0), (0, HP - QK)))
    small = _rows_pad([gate_bias.reshape(nl * 2, d // NCHIP), conv_w.reshape(nl * KW, cc // NCHIP)], d)
    everyone = _exchange(small, False, "gather_small")[0::2]
    gb_k = everyone[:, :nl * 2, :d // NCHIP].reshape(NCHIP, nl, 2, d // NCHIP).transpose(1, 2, 0, 3).reshape(nl, 2, d)
    cw = everyone[:, nl * 2:nl * 2 + nl * KW, :cc // NCHIP].reshape(NCHIP, nl, KW, cc // NCHIP)
    cw_k = jnp.pad(cw.transpose(1, 2, 0, 3).reshape(nl, KW, cc), ((0, 0), (0, CPAD - KW), (0, 0)))

    saved, mixw = [], []
    xc = x0
    for l in range(nl):
        keys, comm = riding("ffn1", l)
        x1, gt1, up1, got = _ffn_fwd(xc, ffn1_norm, l, *[bufs[nm, l] for nm in ffn1_w], f"ffn1_fwd_{l}", comm)
        landed(keys, got)
        mw = mixer_weights(l)
        mixw.append(mw)
        hm = _rms_fwd(x1, mix_norm, l, f"mix_norm_{l}")
        proj = _mm(hm, mw["w_in"], "nt", name=f"proj_{l}")
        uc = _dwconv_fwd(proj, cw_k, conv_b, l, lay["a"], cc, f"dwconv_{l}")
        sc = _ln_silu(uc, conv_ln_g, conv_ln_b, l, f"conv_ln_{l}")
        yc = _mm(sc, mw["w_co"], "nn", name=f"conv_out_{l}")
        q, k, kv = _mla_pre_fwd(proj, lay, l, cq_norm, ckv_norm, qn_k, kn_k, mw["w_uq"], mw["w_ukv"], tabs, f"mla_pre_{l}")
        keys, comm = riding("flash", l)
        o, lse, got = _flash_fwd(q, k, kv, f"flash_{l}", comm)
        landed(keys, got)
        ym = _mm(o, mw["w_mo"], "nn", name=f"mla_out_{l}")
        yv = _comb_fwd(proj, gb_k, l, yc, ym, f"combine_{l}")
        x2 = _mm(yv, mw["w_out"], "nn", name=f"mix_out_{l}", res=x1)
        keys, comm = riding("ffn2", l)
        x3, gt2, up2, got = _ffn_fwd(x2, ffn2_norm, l, *[bufs[nm, l] for nm in ffn2_w], f"ffn2_fwd_{l}", comm)
        landed(keys, got)
        saved.append(dict(x0=xc, x1=x1, gt1=gt1, up1=up1, hm=hm, proj=proj, uc=uc, sc=sc, yc=yc, q=q, k=k, kv=kv,
                          o=o, lse=lse, ym=ym, yv=yv, x2=x2, gt2=gt2, up2=up2))
        xc = x3
    dx, loss_row = _loss_grad(xc, target, "loss")

    gw = {nm: [None] * nl for nm in names}
    red = None

    def shard_groups(l):
        same = {}
        for nm in big:
            same.setdefault(gw[nm][l].shape, []).append(nm)
        return list(same.values())

    def add_parts(l, sib_part):
        qb = {}
        for gi, grp in enumerate(shard_groups(l)):
            outs = _add_cast([gw[nm][l] for nm in grp], [sib_part[nm] for nm in grp], place, f"rs_add_{l}_{gi}")
            qb.update(zip(grp, outs))
        return qb

    def sum_parts(l, qb, parts, prev):
        out = {}
        for gi, grp in enumerate(shard_groups(l)):
            outs = _sum_chips([parts[nm] for nm in grp], [qb[nm] for nm in grp], place, l, nl,
                              None if prev is None else [prev[nm] for nm in grp], f"rs_sum_{l}_{gi}")
            out.update(zip(grp, outs))
        return out

    for l in reversed(range(nl)):
        s, mw = saved[l], mixw[l]
        early = l + 1 if l + 1 < nl else None

        def ffn_back(tag, xin, gains, gts, ups, dout, comm=None):
            wg, wu, wd = (bufs[f"{tag}_w_{p}", l] for p in ("gate", "up", "down"))
            dxi, dgain, dgt, dup, act, hb, dob, got = _ffn_bwd(xin, gains, l, wg, wu, wd, gts, ups, dout,
                                                              f"{tag}_bwd_{l}", comm)
            gw[f"{tag}_w_gate"][l] = _mm(dgt, hb, "tn", name=f"{tag}_dwg_{l}", a_g=True)
            gw[f"{tag}_w_up"][l] = _mm(dup, hb, "tn", name=f"{tag}_dwu_{l}", a_g=True)
            gw[f"{tag}_w_down"][l] = _mm(act, dob, "tn", name=f"{tag}_dwd_{l}", a_g=True)
            gw[f"{tag}_norm"][l] = dgain
            return dxi, got

        comm = None if early is None else _pair_comm([gw[nm][early] for nm in big])
        dx2, got = ffn_back("ffn2", s["x2"], ffn2_norm, s["gt2"], s["up2"], dx, comm)
        if early is not None:
            qb = add_parts(early, dict(zip(big, got)))
        dyv = _mm(dx2, mw["w_out"], "nt", name=f"mix_out_dy_{l}")
        gw["w_out"][l] = _mm(s["yv"], dx2, "tn", name=f"mix_out_dw_{l}").reshape(NCHIP, d // NCHIP, d)
        dyc, dym, dproj, dgb = _comb_bwd(s["proj"], gb_k, l, s["yc"], s["ym"], dyv, dp, f"combine_bwd_{l}")
        gw["gate_bias"][l] = dgb
        dsc = _mm(dyc, mw["w_co"], "nt", name=f"conv_out_ds_{l}")
        gw["w_conv_out"][l] = _cols_shards(_mm(s["sc"], dyc, "tn", name=f"conv_out_dw_{l}"))
        duc, gw["conv_ln_g"][l], gw["conv_ln_b"][l], gw["conv_b"][l] = _ln_silu_bwd(
            s["uc"], conv_ln_g, conv_ln_b, l, dsc, f"conv_ln_bwd_{l}")
        du, dcw = _dwconv_bwd(duc, s["proj"], cw_k, l, lay["a"], cc, f"dwconv_bwd_{l}")
        gw["conv_w"][l] = dcw[:KW]
        dproj = _glu_bwd(du, s["proj"], dproj, lay["a"], cc, f"glu_bwd_{l}")
        do = _mm(dym, mw["w_mo"], "nt", name=f"mla_out_do_{l}", out_dtype=BF16)
        dwmo = _mm(s["o"], dym, "tn", name=f"mla_out_dw_{l}").reshape(N_HEADS, HP, d)[:, HP - V_DIM:].reshape(vw, d)
        gw["w_mla_out"][l] = _cols_shards(dwmo)
        comm = None if early is None else _chips_comm([qb[nm] for nm in big])
        dq, dk, dv, got = _flash_bwd(s["q"], s["k"], s["kv"], do, s["o"], s["lse"], f"flash_bwd_{l}", comm)
        if early is not None:
            red = sum_parts(early, qb, dict(zip(big, got)), red)
        dproj, dqr, dkv, cqn, ckvn, gw["cq_norm"][l], gw["ckv_norm"][l], dqn, dkn = _mla_pre_bwd(
            s["proj"], lay, l, cq_norm, ckv_norm, qn_k, kn_k, mw["w_uq"], mw["w_ukv"], tabs, dq, dk, dv, dproj,
            f"mla_pre_bwd_{l}")
        gw["q_norm"][l], gw["k_norm"][l] = dqn[:, :QK], dkn[:, :QK]
        dwuq = _mm(dqr, cqn, "tn", name=f"uq_dw_{l}").reshape(N_HEADS, HP, ql)[:, :QK]
        gw["w_uq"][l] = dwuq.reshape(NCHIP, N_HEADS * QK // NCHIP, ql)
        gw["w_ukv"][l] = _cols_shards(_mm(ckvn, dkv, "tn", name=f"ukv_dw_{l}"))
        dhm = _mm(dproj, mw["w_in"], "nn", name=f"proj_dh_{l}")
        dwin = _mm(dproj, s["hm"], "tn", name=f"proj_dw_{l}")
        gw["w_in"][l] = jnp.concatenate([dwin[2 * d:2 * d + nat_g], dwin[:2 * d]], axis=0).reshape(
            NCHIP, (nat_g + 2 * d) // NCHIP, d)
        dx1, gw["mix_norm"][l] = _rms_back(s["x1"], mix_norm, l, dhm, dx2, f"mix_norm_bwd_{l}")
        comm = None if early is None else _share_comm([red[nm] for nm in big], early)
        dx, got = ffn_back("ffn1", s["x0"], ffn1_norm, s["gt1"], s["up1"], dx1, comm)
        if early is not None:
            red = dict(zip(big, got))

    sib_part = dict(zip(big, _comm_call(_pair_comm([gw[nm][0] for nm in big]), "rs_pair")))
    qb = add_parts(0, sib_part)
    parts = dict(zip(big, _comm_call(_chips_comm([qb[nm] for nm in big]), "rs_chips")))
    red = sum_parts(0, qb, parts, red)
    red = dict(zip(big, _comm_call(_share_comm([red[nm] for nm in big], 0), "rs_share")))
    grads = {nm: g.reshape(wts[nm].shape) for nm, g in red.items()}

    vec_names = ["ffn1_norm", "mix_norm", "ffn2_norm", "conv_b", "conv_ln_g", "conv_ln_b", "cq_norm", "ckv_norm",
                 "q_norm", "k_norm"]
    rows = [jnp.concatenate(gw[nm], axis=0) for nm in vec_names]
    rows += [jnp.concatenate(gw["gate_bias"], axis=0), jnp.concatenate(gw["conv_w"], axis=0), loss_row]
    total = _exchange(_rows_pad(rows, d), True, "allreduce_small")
    r0 = 0
    for nm in vec_names:
        grads[nm] = total[r0:r0 + nl, :wts[nm].shape[1]]
        r0 += nl
    gb_all = total[r0:r0 + 2 * nl, :d].reshape(nl, 2, NCHIP, d // NCHIP)
    r0 += 2 * nl
    cw_all = total[r0:r0 + KW * nl, :cc].reshape(nl, KW, NCHIP, cc // NCHIP)
    r0 += KW * nl
    grads["gate_bias"] = lax.dynamic_index_in_dim(gb_all, chip, axis=2, keepdims=False)
    grads["conv_w"] = lax.dynamic_index_in_dim(cw_all, chip, axis=2, keepdims=False)
    loss = total[r0, 0]

    delta, new_m, new_v = {}, {}, {}
    by_shape = {}
    for nm in names:
        shp = wts[nm].shape
        by_shape.setdefault((shp[0] * (shp[1] if len(shp) == 3 else 1), shp[-1]), []).append(nm)
    for gi, (shp2, grp) in enumerate(by_shape.items()):
        to2 = lambda a: a.reshape(shp2)
        ds_, ms_, vs_ = _adamw([to2(wts[nm]) for nm in grp], [to2(grads[nm]) for nm in grp],
                               [to2(mom[nm]) for nm in grp], [to2(var[nm]) for nm in grp], f"adamw_{gi}")
        for nm, dd, mm_, vv in zip(grp, ds_, ms_, vs_):
            delta[nm], new_m[nm], new_v[nm] = (a.reshape(wts[nm].shape) for a in (dd, mm_, vv))

    return (loss, dx[None], *[view(nm, grads[nm]) for nm in names], *[view(nm, delta[nm]) for nm in names],
            *[view(nm, new_m[nm]) for nm in names], *[view(nm, new_v[nm]) for nm in names])
```

```python
import functools

import jax
import jax.numpy as jnp
from jax import lax
from jax.experimental import pallas as pl
from jax.experimental.pallas import tpu as pltpu

F32, BF16 = jnp.float32, jnp.bfloat16
SDS = jax.ShapeDtypeStruct
MESH = pl.DeviceIdType.MESH
ANY = pl.BlockSpec(memory_space=pl.ANY)

NCHIP = 4
N_DEV = 8
N_HEADS, NOPE, ROPE, V_DIM = 8, 64, 32, 64
QK = NOPE + ROPE
HP = 128
CHUNK = 64
KW = 31
CPAD = 32
ROPE_THETA = 10000.0
EPS = 1e-6
LR, B1, B2, EPS_ADAM, WD, STEP = 0.001, 0.9, 0.999, 1e-08, 0.01, 10
VMEM_BIG = 48 << 20
FFN_PARTS = 2


def _cparams(sem=None, vmem=None):
    kw = {}
    if sem is not None:
        kw["dimension_semantics"] = sem
    if vmem is not None:
        kw["vmem_limit_bytes"] = vmem
    return pltpu.CompilerParams(**kw)


def _rt(t):
    return min(512, t // 2)


def _pick(n, cands):
    for c in cands:
        if c <= n and n % c == 0:
            return c
    return n


def _tile2(r, c, nblocks, row_mult):
    tr, tc = r, c
    while 2 * nblocks * tr * tc * 4 > VMEM_BIG // 2 and tr % (2 * row_mult) == 0:
        tr //= 2
    while 2 * nblocks * tr * tc * 4 > VMEM_BIG // 2 and tc % 256 == 0:
        tc //= 2
    return tr, tc


def _sig(v):
    return 1.0 / (1.0 + jnp.exp(-v))


def _rms_r(v):
    return lax.rsqrt(jnp.mean(v * v, axis=-1, keepdims=True) + EPS)


def _rms_bwd(xv, g, dy):
    r = _rms_r(xv)
    xh = xv * r
    dg = jnp.sum(dy * xh, axis=0, keepdims=True)
    dxh = dy * g
    dx = r * (dxh - xh * jnp.mean(dxh * xh, axis=-1, keepdims=True))
    return dx, dg


def _dot(a, b):
    return jnp.dot(a, b, preferred_element_type=F32)


def _dot_nt(a, b):
    return lax.dot_general(a, b, (((1,), (1,)), ((), ())), preferred_element_type=F32)


def _dot_tn(a, b):
    return lax.dot_general(a, b, (((0,), (0,)), ((), ())), preferred_element_type=F32)


class _Comm:
    def __init__(self, ins, out_shapes, aliases, nsem, start, finish):
        self.ins, self.out_shapes, self.aliases, self.nsem = list(ins), list(out_shapes), dict(aliases), nsem
        self.start, self.finish = start, finish


def _call(body, comm, first, last, *, name, grid, in_specs, args, out_specs, out_shape, scratch, sem):
    in_specs, args, out_specs, out_shape, scratch = map(list, (in_specs, args, out_specs, out_shape, scratch))
    n_in, n_out, n_scr = len(args), len(out_shape), len(scratch)
    aliases = {}
    kern = body
    if comm is not None:
        nci, nco = len(comm.ins), len(comm.out_shapes)
        o0 = n_in + nci
        s0 = o0 + n_out + nco

        def kern(*refs):
            cins, couts = refs[n_in:o0], refs[o0 + n_out:s0]
            ssem, rsem = refs[s0 + n_scr:]
            pl.when(first())(lambda: comm.start(cins, couts, ssem, rsem))
            body(*refs[:n_in], *refs[o0:o0 + n_out], *refs[s0:s0 + n_scr])
            pl.when(last())(lambda: comm.finish(cins, couts, ssem, rsem))

        in_specs += [ANY] * nci
        args += comm.ins
        out_specs += [ANY] * nco
        out_shape += comm.out_shapes
        scratch += [pltpu.SemaphoreType.DMA((comm.nsem,))] * 2
        aliases = {n_in + i: n_out + o for i, o in comm.aliases.items()}
    outs = pl.pallas_call(kern, name=name, grid=grid, in_specs=in_specs, out_specs=out_specs, out_shape=out_shape,
                          scratch_shapes=scratch, input_output_aliases=aliases,
                          compiler_params=_cparams(sem, VMEM_BIG))(*args)
    return outs[:n_out], outs[n_out:]


def _comm_call(comm, name):
    once = lambda: pl.program_id(0) == 0
    return _call(lambda: None, comm, once, once, name=name, grid=(1,), in_specs=[], args=[], out_specs=[],
                 out_shape=[], scratch=[], sem=("arbitrary",))[1]


def _mm(a, b, mode, *, name, out_dtype=F32, res=None, a_g=False):
    a2 = a.shape[1:] if a_g else a.shape
    if mode == "nn":
        (m, k), (k2, n) = a2, b.shape
    elif mode == "nt":
        (m, k), (n, k2) = a2, b.shape
    else:
        (k, m), (k2, n) = a2, b.shape
    assert k == k2, (name, a.shape, b.shape)
    g = a.shape[0] if a_g else 1
    tm = m if m <= 768 else _pick(m, (768, 512, 256, 128))
    tn = n if n <= 1280 else _pick(n, (1280, 1024, 768, 512, 256, 128))
    tk = k if k <= 1280 else _pick(k, (1280, 1024, 768, 512, 256, 128))
    fits = lambda kk, nn: 2 * (kk * tm * a.dtype.itemsize + kk * nn * b.dtype.itemsize + tm * nn * 4) <= VMEM_BIG - (8 << 20)
    if fits(k, n):
        tn, tk = n, k
    elif fits(k, tn):
        tk = k
    nk = k // tk
    dn = {"nn": (((1,), (0,)), ((), ())), "nt": (((1,), (1,)), ((), ())), "tn": (((0,), (0,)), ((), ()))}[mode]

    def body(*refs):
        a_ref, b_ref = refs[0], refs[1]
        res_ref = refs[2] if res is not None else None
        o_ref = refs[3] if res is not None else refs[2]
        p = lax.dot_general(a_ref[...].astype(BF16), b_ref[...].astype(BF16), dn, preferred_element_type=F32)

        def fin(v):
            if res_ref is not None:
                v = v + res_ref[...]
            o_ref[...] = v.astype(out_dtype)

        if nk == 1:
            fin(p)
        else:
            acc_ref = refs[-1]
            kk = pl.program_id(3)

            @pl.when(kk == 0)
            def _():
                acc_ref[...] = p

            @pl.when(kk > 0)
            def _():
                acc_ref[...] += p

            @pl.when(kk == nk - 1)
            def _():
                fin(acc_ref[...])

    a_block, a_idx = ((tk, tm), lambda gg, i, j, kk: (kk, i)) if mode == "tn" else ((tm, tk), lambda gg, i, j, kk: (i, kk))
    if a_g:
        a_spec = pl.BlockSpec((None,) + a_block, lambda gg, i, j, kk: (gg,) + a_idx(gg, i, j, kk))
    else:
        a_spec = pl.BlockSpec(a_block, a_idx)
    if mode == "nt":
        b_spec = pl.BlockSpec((tn, tk), lambda gg, i, j, kk: (j, kk))
    else:
        b_spec = pl.BlockSpec((tk, tn), lambda gg, i, j, kk: (kk, j))
    in_specs, args = [a_spec, b_spec], [a, b]
    if res is not None:
        in_specs.append(pl.BlockSpec((tm, tn), lambda gg, i, j, kk: (i, j)))
        args.append(res)
    if a_g:
        out_spec, oshape = pl.BlockSpec((None, tm, tn), lambda gg, i, j, kk: (gg, i, j)), (g, m, n)
    else:
        out_spec, oshape = pl.BlockSpec((tm, tn), lambda gg, i, j, kk: (i, j)), (m, n)
    return pl.pallas_call(
        body, name=name, grid=(g, m // tm, n // tn, nk), in_specs=in_specs, out_specs=out_spec,
        out_shape=SDS(oshape, out_dtype), scratch_shapes=[pltpu.VMEM((tm, tn), F32)] if nk > 1 else [],
        compiler_params=_cparams(("arbitrary",) * 4, VMEM_BIG))(*args)


def _ffn_fwd(x, gains, l, wg, wu, wd, name, comm=None):
    t, d = x.shape
    f = wg.shape[-2]
    tt = _rt(t)

    def body(x_ref, g_ref, wg_ref, wu_ref, wd_ref, o_ref, gt_ref, up_ref, h_sc, acc_sc):
        j = pl.program_id(1)

        @pl.when(j == 0)
        def _():
            xv = x_ref[...]
            h_sc[...] = (xv * _rms_r(xv) * g_ref[l:l + 1, :]).astype(BF16)
            acc_sc[...] = jnp.zeros_like(acc_sc)

        for part in range(FFN_PARTS):
            rows = pl.ds(part * (tt // FFN_PARTS), tt // FFN_PARTS)
            h = h_sc[rows, :]
            gt = _dot_nt(h, wg_ref[...]).astype(BF16)
            up = _dot_nt(h, wu_ref[...]).astype(BF16)
            gt_ref[rows, :] = gt
            up_ref[rows, :] = up
            acc_sc[rows, :] += _dot(gt * _sig(gt) * up, wd_ref[...])

        @pl.when(j == NCHIP - 1)
        def _():
            o_ref[...] = x_ref[...] + 0.5 * acc_sc[...]

    wspec = pl.BlockSpec((None, f, d), lambda i, j: (j, 0, 0))
    row = pl.BlockSpec((tt, d), lambda i, j: (i, 0))
    sh = pl.BlockSpec((None, tt, f), lambda i, j: (j, i, 0))
    ni = t // tt
    first = lambda: jnp.logical_and(pl.program_id(0) == 0, pl.program_id(1) == 0)
    last = lambda: jnp.logical_and(pl.program_id(0) == ni - 1, pl.program_id(1) == NCHIP - 1)
    outs, couts = _call(
        body, comm, first, last, name=name, grid=(ni, NCHIP),
        in_specs=[row, pl.BlockSpec(gains.shape, lambda i, j: (0, 0)), wspec, wspec, wspec], args=[x, gains, wg, wu, wd],
        out_specs=[row, sh, sh], out_shape=[SDS((t, d), F32), SDS((NCHIP, t, f), BF16), SDS((NCHIP, t, f), BF16)],
        scratch=[pltpu.VMEM((tt, d), BF16), pltpu.VMEM((tt, d), F32)], sem=("arbitrary", "arbitrary"))
    return (*outs, couts)


def _ffn_bwd(x, gains, l, wg, wu, wd, gts, ups, dout, name, comm=None):
    t, d = x.shape
    f = wg.shape[-2]
    tt = _rt(t)

    def body(x_ref, g_ref, wg_ref, wu_ref, wd_ref, gt_ref, up_ref, do_ref,
             dx_ref, dg_ref, dgt_ref, dup_ref, act_ref, h_ref, dob_ref, dh_sc):
        i, j = pl.program_id(0), pl.program_id(1)

        @pl.when(j == 0)
        def _():
            xv = x_ref[...]
            h_ref[...] = (xv * _rms_r(xv) * g_ref[l:l + 1, :]).astype(BF16)
            dob_ref[...] = (0.5 * do_ref[...]).astype(BF16)
            dh_sc[...] = jnp.zeros_like(dh_sc)

        @pl.when(jnp.logical_and(i == 0, j == 0))
        def _():
            dg_ref[...] = jnp.zeros_like(dg_ref)

        for part in range(FFN_PARTS):
            rows = pl.ds(part * (tt // FFN_PARTS), tt // FFN_PARTS)
            dact = _dot_nt(dob_ref[rows, :], wd_ref[...]).astype(BF16)
            gt = gt_ref[rows, :]
            up = up_ref[rows, :]
            s = _sig(gt)
            sl = gt * s
            dup = dact * sl
            dgt = dact * up * (s + sl * (1.0 - s))
            dgt_ref[rows, :] = dgt
            dup_ref[rows, :] = dup
            act_ref[rows, :] = sl * up
            dh_sc[rows, :] += _dot(dgt, wg_ref[...]) + _dot(dup, wu_ref[...])

        @pl.when(j == NCHIP - 1)
        def _():
            dxn, dg = _rms_bwd(x_ref[...], g_ref[l:l + 1, :], dh_sc[...])
            dx_ref[...] = do_ref[...] + dxn
            dg_ref[...] += dg

    wspec = pl.BlockSpec((None, f, d), lambda i, j: (j, 0, 0))
    row = pl.BlockSpec((tt, d), lambda i, j: (i, 0))
    sh = pl.BlockSpec((None, tt, f), lambda i, j: (j, i, 0))
    ni = t // tt
    first = lambda: jnp.logical_and(pl.program_id(0) == 0, pl.program_id(1) == 0)
    last = lambda: jnp.logical_and(pl.program_id(0) == ni - 1, pl.program_id(1) == NCHIP - 1)
    outs, couts = _call(
        body, comm, first, last, name=name, grid=(ni, NCHIP),
        in_specs=[row, pl.BlockSpec(gains.shape, lambda i, j: (0, 0)), wspec, wspec, wspec, sh, sh, row],
        args=[x, gains, wg, wu, wd, gts, ups, dout],
        out_specs=[row, pl.BlockSpec((1, d), lambda i, j: (0, 0)), sh, sh, sh, row, row],
        out_shape=[SDS((t, d), F32), SDS((1, d), F32)] + [SDS((NCHIP, t, f), BF16)] * 3 + [SDS((t, d), BF16)] * 2,
        scratch=[pltpu.VMEM((tt, d), F32)], sem=("arbitrary", "arbitrary"))
    return (*outs, couts)


def _rms_fwd(x, gains, l, name):
    t, d = x.shape
    tt = _rt(t)

    def body(x_ref, g_ref, o_ref):
        xv = x_ref[...]
        o_ref[...] = (xv * _rms_r(xv) * g_ref[l:l + 1, :]).astype(BF16)

    row = pl.BlockSpec((tt, d), lambda i: (i, 0))
    return pl.pallas_call(body, name=name, grid=(t // tt,), in_specs=[row, pl.BlockSpec(gains.shape, lambda i: (0, 0))],
                          out_specs=row, out_shape=SDS((t, d), BF16), compiler_params=_cparams(("arbitrary",)))(x, gains)


def _rms_back(x, gains, l, dh, dres, name):
    t, d = x.shape
    tt = _rt(t)

    def body(x_ref, g_ref, dh_ref, dr_ref, dx_ref, dg_ref):
        @pl.when(pl.program_id(0) == 0)
        def _():
            dg_ref[...] = jnp.zeros_like(dg_ref)

        dxn, dg = _rms_bwd(x_ref[...], g_ref[l:l + 1, :], dh_ref[...])
        dx_ref[...] = dr_ref[...] + dxn
        dg_ref[...] += dg

    row = pl.BlockSpec((tt, d), lambda i: (i, 0))
    return pl.pallas_call(body, name=name, grid=(t // tt,),
                          in_specs=[row, pl.BlockSpec(gains.shape, lambda i: (0, 0)), row, row],
                          out_specs=[row, pl.BlockSpec((1, d), lambda i: (0, 0))],
                          out_shape=[SDS((t, d), F32), SDS((1, d), F32)],
                          compiler_params=_cparams(("arbitrary",)))(x, gains, dh, dres)


def _dwconv_fwd(proj, convw, convb, l, o_a, cc, name):
    t = proj.shape[0]
    r = min(256, t)
    nb = cc // HP

    def body(a_ref, gate_ref, w_ref, b_ref, uc_ref, u_sc):
        u_sc[0:CPAD, :] = jnp.zeros((CPAD, HP), F32)

        def fill(ci, carry):
            r0 = pl.multiple_of(ci * r, r)
            u_sc[pl.ds(CPAD + r0, r), :] = a_ref[pl.ds(r0, r), :] * _sig(gate_ref[pl.ds(r0, r), :])
            return carry

        lax.fori_loop(0, t // r, fill, 0)
        w = w_ref[l]
        bias = b_ref[l:l + 1, :]

        def conv(ci, carry):
            r0 = pl.multiple_of(ci * r, r)
            win = u_sc[pl.ds(r0, r + CPAD), :]
            acc = jnp.zeros((r, HP), F32) + bias
            for k in range(KW):
                off = CPAD - (KW - 1) + k
                acc = acc + win[off:off + r, :] * w[k:k + 1, :]
            uc_ref[pl.ds(r0, r), :] = acc
            return carry

        lax.fori_loop(0, t // r, conv, 0)

    col = lambda base: pl.BlockSpec((t, HP), lambda c: (0, base // HP + c))
    return pl.pallas_call(
        body, name=name, grid=(nb,),
        in_specs=[col(o_a), col(o_a + cc), pl.BlockSpec((convw.shape[0], CPAD, HP), lambda c: (0, 0, c)),
                  pl.BlockSpec((convb.shape[0], HP), lambda c: (0, c))],
        out_specs=pl.BlockSpec((t, HP), lambda c: (0, c)), out_shape=SDS((t, cc), F32),
        scratch_shapes=[pltpu.VMEM((t + CPAD, HP), F32)],
        compiler_params=_cparams(("arbitrary",), VMEM_BIG))(proj, proj, convw, convb)


def _dwconv_bwd(duc, proj, convw, l, o_a, cc, name):
    t = proj.shape[0]
    r = min(256, t)
    nb = cc // HP

    def body(d_ref, a_ref, gate_ref, w_ref, du_ref, dw_ref, u_sc, d_sc, dw_sc):
        u_sc[0:CPAD, :] = jnp.zeros((CPAD, HP), F32)
        d_sc[t:t + CPAD, :] = jnp.zeros((CPAD, HP), F32)
        dw_sc[...] = jnp.zeros_like(dw_sc)

        def fill(ci, carry):
            r0 = pl.multiple_of(ci * r, r)
            u_sc[pl.ds(CPAD + r0, r), :] = a_ref[pl.ds(r0, r), :] * _sig(gate_ref[pl.ds(r0, r), :])
            d_sc[pl.ds(r0, r), :] = d_ref[pl.ds(r0, r), :]
            return carry

        lax.fori_loop(0, t // r, fill, 0)
        w = w_ref[l]

        def conv(ci, carry):
            r0 = pl.multiple_of(ci * r, r)
            dwin = d_sc[pl.ds(r0, r + CPAD), :]
            uwin = u_sc[pl.ds(r0, r + CPAD), :]
            dcur = dwin[0:r, :]
            acc = jnp.zeros((r, HP), F32)
            for k in range(KW):
                acc = acc + dwin[KW - 1 - k:KW - 1 - k + r, :] * w[k:k + 1, :]
                off = CPAD - (KW - 1) + k
                part = (dcur * uwin[off:off + r, :]).reshape(r // 8, 8, HP).sum(axis=0)
                dw_sc[8 * k:8 * k + 8, :] += part
            du_ref[pl.ds(r0, r), :] = acc
            return carry

        lax.fori_loop(0, t // r, conv, 0)
        dw_ref[...] = jnp.zeros_like(dw_ref)
        for k in range(KW):
            dw_ref[k:k + 1, :] = jnp.sum(dw_sc[8 * k:8 * k + 8, :], axis=0, keepdims=True)

    col = lambda base: pl.BlockSpec((t, HP), lambda c: (0, base // HP + c))
    return pl.pallas_call(
        body, name=name, grid=(nb,),
        in_specs=[pl.BlockSpec((t, HP), lambda c: (0, c)), col(o_a), col(o_a + cc),
                  pl.BlockSpec((convw.shape[0], CPAD, HP), lambda c: (0, 0, c))],
        out_specs=[pl.BlockSpec((t, HP), lambda c: (0, c)), pl.BlockSpec((CPAD, HP), lambda c: (0, c))],
        out_shape=[SDS((t, cc), F32), SDS((CPAD, cc), F32)],
        scratch_shapes=[pltpu.VMEM((t + CPAD, HP), F32), pltpu.VMEM((t + CPAD, HP), F32), pltpu.VMEM((8 * CPAD, HP), F32)],
        compiler_params=_cparams(("arbitrary",), VMEM_BIG))(duc, proj, proj, convw)


def _ln_parts(uc, g, b):
    mu = jnp.mean(uc, axis=-1, keepdims=True)
    xc = uc - mu
    r = lax.rsqrt(jnp.mean(xc * xc, axis=-1, keepdims=True) + EPS)
    xh = xc * r
    return r, xh, xh * g + b


def _ln_silu(uc, ln_g, ln_b, l, name):
    t, cc = uc.shape
    tt = _rt(t)

    def body(u_ref, g_ref, b_ref, s_ref):
        _, _, yv = _ln_parts(u_ref[...], g_ref[l:l + 1, :], b_ref[l:l + 1, :])
        s_ref[...] = (yv * _sig(yv)).astype(BF16)

    row = pl.BlockSpec((tt, cc), lambda i: (i, 0))
    full = pl.BlockSpec(ln_g.shape, lambda i: (0, 0))
    return pl.pallas_call(body, name=name, grid=(t // tt,), in_specs=[row, full, full], out_specs=row,
                          out_shape=SDS((t, cc), BF16), compiler_params=_cparams(("arbitrary",)))(uc, ln_g, ln_b)


def _ln_silu_bwd(uc, ln_g, ln_b, l, ds, name):
    t, cc = uc.shape
    tt = _rt(t)

    def body(u_ref, g_ref, b_ref, ds_ref, du_ref, dg_ref, db_ref, dcb_ref):
        @pl.when(pl.program_id(0) == 0)
        def _():
            dg_ref[...] = jnp.zeros_like(dg_ref)
            db_ref[...] = jnp.zeros_like(db_ref)
            dcb_ref[...] = jnp.zeros_like(dcb_ref)

        g = g_ref[l:l + 1, :]
        r, xh, yv = _ln_parts(u_ref[...], g, b_ref[l:l + 1, :])
        sy = _sig(yv)
        dy = ds_ref[...] * (sy * (1.0 + yv * (1.0 - sy)))
        dg_ref[...] += jnp.sum(dy * xh, axis=0, keepdims=True)
        db_ref[...] += jnp.sum(dy, axis=0, keepdims=True)
        dxh = dy * g
        du = r * (dxh - jnp.mean(dxh, axis=-1, keepdims=True) - xh * jnp.mean(dxh * xh, axis=-1, keepdims=True))
        du_ref[...] = du
        dcb_ref[...] += jnp.sum(du, axis=0, keepdims=True)

    row = pl.BlockSpec((tt, cc), lambda i: (i, 0))
    full = pl.BlockSpec(ln_g.shape, lambda i: (0, 0))
    vec = pl.BlockSpec((1, cc), lambda i: (0, 0))
    return pl.pallas_call(body, name=name, grid=(t // tt,), in_specs=[row, full, full, row],
                          out_specs=[row, vec, vec, vec], out_shape=[SDS((t, cc), F32)] + [SDS((1, cc), F32)] * 3,
                          compiler_params=_cparams(("arbitrary",)))(uc, ln_g, ln_b, ds)


def _glu_bwd(du, proj, dproj, o_a, cc, name):
    t = du.shape[0]
    tt = _rt(t)

    def body(du_ref, a_ref, gate_ref, prev_ref, o_ref):
        sg = _sig(gate_ref[...])
        dv = du_ref[...]
        o_ref[:, 0:cc] = (dv * sg).astype(BF16)
        o_ref[:, cc:2 * cc] = (dv * a_ref[...] * sg * (1.0 - sg)).astype(BF16)

    return pl.pallas_call(
        body, name=name, grid=(t // tt,),
        in_specs=[pl.BlockSpec((tt, cc), lambda i: (i, 0)), pl.BlockSpec((tt, cc), lambda i: (i, o_a // cc)),
                  pl.BlockSpec((tt, cc), lambda i: (i, o_a // cc + 1)), ANY],
        out_specs=pl.BlockSpec((tt, 2 * cc), lambda i: (i, o_a // (2 * cc))),
        out_shape=SDS(dproj.shape, dproj.dtype), input_output_aliases={3: 0},
        compiler_params=_cparams(("arbitrary",)))(du, proj, proj, dproj)


def _rope(v, cs, s1, s2):
    return v * cs + pltpu.roll(v, HP - ROPE // 2, 1) * s1 + pltpu.roll(v, ROPE // 2, 1) * s2


def _rope_t(dv, cs, s1, s2):
    return dv * cs + pltpu.roll(dv * s1, ROPE // 2, 1) + pltpu.roll(dv * s2, HP - ROPE // 2, 1)


def _head_norm(v, g):
    r = lax.rsqrt(jnp.sum(v * v, axis=-1, keepdims=True) * (1.0 / QK) + EPS)
    return v * r * g, r


def _head_norm_bwd(v, r, g, dy):
    xh = v * r
    dg = jnp.sum(dy * xh, axis=0, keepdims=True)
    dxh = dy * g
    dx = r * (dxh - xh * (jnp.sum(dxh * xh, axis=-1, keepdims=True) * (1.0 / QK)))
    return dx, dg


def _mla_specs(tt, lay, cq_norm, ckv_norm, qn, kn, wuq, wukv):
    ql, kvl = cq_norm.shape[1], ckv_norm.shape[1]
    full = lambda a: pl.BlockSpec(a.shape, lambda i: (0,) * a.ndim)
    tab = pl.BlockSpec((tt, HP), lambda i: (i, 0))
    return [pl.BlockSpec((tt, ql), lambda i: (i, lay["cq"] // ql)),
            pl.BlockSpec((tt, kvl), lambda i: (i, lay["ckv"] // kvl)),
            pl.BlockSpec((tt, HP), lambda i: (i, lay["kr"] // HP)),
            full(cq_norm), full(ckv_norm), full(qn), full(kn), full(wuq), full(wukv), tab, tab, tab]


def _mla_pre_fwd(proj, lay, l, cq_norm, ckv_norm, qn, kn, wuq, wukv, tabs, name):
    t = proj.shape[0]
    tt = _rt(t)
    hw = N_HEADS * HP

    def body(cq_ref, ckv_ref, kr_ref, gq_ref, gkv_ref, qn_ref, kn_ref, wq_ref, wkv_ref, c_ref, s1_ref, s2_ref,
             q_ref, k_ref, v_ref):
        cq = cq_ref[...]
        cqn = (cq * _rms_r(cq) * gq_ref[l:l + 1, :]).astype(BF16)
        ckv = ckv_ref[...]
        ckvn = (ckv * _rms_r(ckv) * gkv_ref[l:l + 1, :]).astype(BF16)
        qraw = _dot_nt(cqn, wq_ref[...])
        kv = _dot(ckvn, wkv_ref[...])
        v_ref[...] = kv.astype(BF16)
        lane = lax.broadcasted_iota(jnp.int32, (tt, HP), 1)
        krs = pltpu.roll(jnp.where(lane < ROPE, kr_ref[...], 0.0), NOPE, 1)
        cs, s1, s2 = c_ref[...], s1_ref[...], s2_ref[...]
        gq, gk = qn_ref[l:l + 1, :], kn_ref[l:l + 1, :]
        for h in range(N_HEADS):
            sl = slice(h * HP, (h + 1) * HP)
            qh, _ = _head_norm(qraw[:, sl], gq)
            q_ref[:, sl] = (_rope(qh, cs, s1, s2) * QK ** -0.5).astype(BF16)
            kh, _ = _head_norm(jnp.where(lane < NOPE, kv[:, sl], krs), gk)
            k_ref[:, sl] = _rope(kh, cs, s1, s2).astype(BF16)

    row = pl.BlockSpec((tt, hw), lambda i: (i, 0))
    return pl.pallas_call(
        body, name=name, grid=(t // tt,),
        in_specs=_mla_specs(tt, lay, cq_norm, ckv_norm, qn, kn, wuq, wukv),
        out_specs=[row, row, row], out_shape=[SDS((t, hw), BF16)] * 3,
        compiler_params=_cparams(("arbitrary",), VMEM_BIG))(
            proj, proj, proj, cq_norm, ckv_norm, qn, kn, wuq, wukv, *tabs)


def _mla_pre_bwd(proj, lay, l, cq_norm, ckv_norm, qn, kn, wuq, wukv, tabs, dq, dk, dv, dproj, name):
    t = proj.shape[0]
    tt = _rt(t)
    hw = N_HEADS * HP
    ql, kvl = cq_norm.shape[1], ckv_norm.shape[1]
    wm = lay["wm"]

    def body(cq_ref, ckv_ref, kr_ref, gq_ref, gkv_ref, qn_ref, kn_ref, wq_ref, wkv_ref, c_ref, s1_ref, s2_ref,
             dq_ref, dk_ref, dv_ref, prev_ref,
             o_ref, dqr_ref, dkv_ref, cqn_ref, ckvn_ref, dgq_ref, dgkv_ref, dqn_ref, dkn_ref):
        @pl.when(pl.program_id(0) == 0)
        def _():
            for ref in (dgq_ref, dgkv_ref, dqn_ref, dkn_ref):
                ref[...] = jnp.zeros_like(ref)

        cq = cq_ref[...]
        cqn = (cq * _rms_r(cq) * gq_ref[l:l + 1, :]).astype(BF16)
        ckv = ckv_ref[...]
        ckvn = (ckv * _rms_r(ckv) * gkv_ref[l:l + 1, :]).astype(BF16)
        cqn_ref[...] = cqn
        ckvn_ref[...] = ckvn
        qraw = _dot_nt(cqn, wq_ref[...])
        kv = _dot(ckvn, wkv_ref[...])
        lane = lax.broadcasted_iota(jnp.int32, (tt, HP), 1)
        krs = pltpu.roll(jnp.where(lane < ROPE, kr_ref[...], 0.0), NOPE, 1)
        cs, s1, s2 = c_ref[...], s1_ref[...], s2_ref[...]
        gq, gk = qn_ref[l:l + 1, :], kn_ref[l:l + 1, :]
        dkr = jnp.zeros((tt, HP), F32)
        dgq = jnp.zeros((1, HP), F32)
        dgk = jnp.zeros((1, HP), F32)
        for h in range(N_HEADS):
            sl = slice(h * HP, (h + 1) * HP)
            qh = qraw[:, sl]
            _, rq = _head_norm(qh, gq)
            dqh, dg = _head_norm_bwd(qh, rq, gq, _rope_t(dq_ref[:, sl] * QK ** -0.5, cs, s1, s2))
            dgq = dgq + dg
            dqr_ref[:, sl] = dqh.astype(BF16)
            kp = jnp.where(lane < NOPE, kv[:, sl], krs)
            _, rk = _head_norm(kp, gk)
            dkp, dg = _head_norm_bwd(kp, rk, gk, _rope_t(dk_ref[:, sl], cs, s1, s2))
            dgk = dgk + dg
            dkv_ref[:, sl] = (jnp.where(lane < NOPE, dkp, 0.0) + dv_ref[:, sl]).astype(BF16)
            dkr = dkr + dkp
        dqn_ref[...] += dgq
        dkn_ref[...] += dgk
        dkr = jnp.where(lane < ROPE, pltpu.roll(dkr, HP - NOPE, 1), 0.0)
        dcq, dg = _rms_bwd(cq, gq_ref[l:l + 1, :], _dot(dqr_ref[...], wq_ref[...]))
        dgq_ref[...] += dg
        dckv, dg = _rms_bwd(ckv, gkv_ref[l:l + 1, :], _dot_nt(dkv_ref[...], wkv_ref[...]))
        dgkv_ref[...] += dg
        o_ref[:, 0:ql] = dcq.astype(BF16)
        o_ref[:, ql:ql + kvl] = dckv.astype(BF16)
        o_ref[:, ql + kvl:ql + kvl + HP] = dkr.astype(BF16)
        o_ref[:, ql + kvl + HP:wm] = jnp.zeros((tt, wm - ql - kvl - HP), BF16)

    row = pl.BlockSpec((tt, hw), lambda i: (i, 0))
    vec = lambda n: pl.BlockSpec((1, n), lambda i: (0, 0))
    return pl.pallas_call(
        body, name=name, grid=(t // tt,),
        in_specs=_mla_specs(tt, lay, cq_norm, ckv_norm, qn, kn, wuq, wukv) + [row, row, row, ANY],
        out_specs=[pl.BlockSpec((tt, wm), lambda i: (i, lay["cq"] // wm)), row, row,
                   pl.BlockSpec((tt, ql), lambda i: (i, 0)), pl.BlockSpec((tt, kvl), lambda i: (i, 0)),
                   vec(ql), vec(kvl), vec(HP), vec(HP)],
        out_shape=[SDS(dproj.shape, dproj.dtype), SDS((t, hw), BF16), SDS((t, hw), BF16), SDS((t, ql), BF16),
                   SDS((t, kvl), BF16), SDS((1, ql), F32), SDS((1, kvl), F32), SDS((1, HP), F32), SDS((1, HP), F32)],
        input_output_aliases={15: 0},
        compiler_params=_cparams(("arbitrary",), VMEM_BIG))(
            proj, proj, proj, cq_norm, ckv_norm, qn, kn, wuq, wukv, *tabs, dq, dk, dv, dproj)


def _comb_fwd(proj, gate_bias, l, yc, ym, name):
    t, d = yc.shape
    tt = _rt(t)

    def body(p_ref, b_ref, yc_ref, ym_ref, y_ref):
        b = b_ref[l]
        g0 = _sig(p_ref[:, 0:d] + b[0:1, :])
        g1 = _sig(p_ref[:, d:2 * d] + b[1:2, :])
        y_ref[...] = (g0 * yc_ref[...] + g1 * ym_ref[...]).astype(BF16)

    row = pl.BlockSpec((tt, d), lambda i: (i, 0))
    return pl.pallas_call(
        body, name=name, grid=(t // tt,),
        in_specs=[pl.BlockSpec((tt, 2 * d), lambda i: (i, 0)), pl.BlockSpec(gate_bias.shape, lambda i: (0, 0, 0)), row, row],
        out_specs=row, out_shape=SDS((t, d), BF16), compiler_params=_cparams(("arbitrary",)))(proj, gate_bias, yc, ym)


def _comb_bwd(proj, gate_bias, l, yc, ym, dy, dp_cols, name):
    t, d = yc.shape
    tt = _rt(t)

    def body(p_ref, b_ref, yc_ref, ym_ref, dy_ref, dyc_ref, dym_ref, dp_ref, db_ref):
        @pl.when(pl.program_id(0) == 0)
        def _():
            db_ref[...] = jnp.zeros_like(db_ref)

        b = b_ref[l]
        dyv = dy_ref[...]
        g0 = _sig(p_ref[:, 0:d] + b[0:1, :])
        g1 = _sig(p_ref[:, d:2 * d] + b[1:2, :])
        dyc_ref[...] = (dyv * g0).astype(BF16)
        dym_ref[...] = (dyv * g1).astype(BF16)
        dg0 = dyv * yc_ref[...] * g0 * (1.0 - g0)
        dg1 = dyv * ym_ref[...] * g1 * (1.0 - g1)
        dp_ref[:, 0:d] = dg0.astype(BF16)
        dp_ref[:, d:2 * d] = dg1.astype(BF16)
        db_ref[0:1, :] += jnp.sum(dg0, axis=0, keepdims=True)
        db_ref[1:2, :] += jnp.sum(dg1, axis=0, keepdims=True)

    row = pl.BlockSpec((tt, d), lambda i: (i, 0))
    wide = pl.BlockSpec((tt, 2 * d), lambda i: (i, 0))
    return pl.pallas_call(
        body, name=name, grid=(t // tt,),
        in_specs=[wide, pl.BlockSpec(gate_bias.shape, lambda i: (0, 0, 0)), row, row, row],
        out_specs=[row, row, wide, pl.BlockSpec((2, d), lambda i: (0, 0))],
        out_shape=[SDS((t, d), BF16), SDS((t, d), BF16), SDS((t, dp_cols), BF16), SDS((2, d), F32)],
        compiler_params=_cparams(("arbitrary",)))(proj, gate_bias, yc, ym, dy)


def _loss_grad(y, target, name):
    t, d = y.shape
    tt = _rt(t)
    nt = t // tt

    def body(y_ref, t_ref, dy_ref, loss_ref, acc_sc):
        i = pl.program_id(0)

        @pl.when(i == 0)
        def _():
            acc_sc[...] = jnp.zeros_like(acc_sc)

        diff = y_ref[...] - t_ref[...]
        dy_ref[...] = diff * (1.0 / d)
        acc_sc[...] += jnp.sum(diff * diff, axis=0, keepdims=True)

        @pl.when(i == nt - 1)
        def _():
            tot = jnp.sum(acc_sc[...], axis=1, keepdims=True) * (0.5 / d)
            loss_ref[...] = jnp.broadcast_to(tot, (1, HP))

    row = pl.BlockSpec((tt, d), lambda i: (i, 0))
    return pl.pallas_call(body, name=name, grid=(nt,), in_specs=[row, row],
                          out_specs=[row, pl.BlockSpec((1, HP), lambda i: (0, 0))],
                          out_shape=[SDS((t, d), F32), SDS((1, HP), F32)],
                          scratch_shapes=[pltpu.VMEM((1, d), F32)],
                          compiler_params=_cparams(("arbitrary",)))(y, target)


def _chunk_mask(tq):
    rows = lax.broadcasted_iota(jnp.int32, (tq, tq), 0) // CHUNK
    cols = lax.broadcasted_iota(jnp.int32, (tq, tq), 1) // CHUNK
    return cols <= rows


NEG = -1e30


def _flash_fwd(q, k, v, name, comm=None):
    t = q.shape[0]
    tq = _rt(t)
    nq = t // tq
    rep = tq // HP

    def body(q_ref, k_ref, v_ref, o_ref, lse_ref):
        qi = pl.program_id(1)
        qv = q_ref[...]

        def step(ki, carry, masked):
            m_prev, l_prev, acc = carry
            r0 = pl.multiple_of(ki * tq, tq)
            s = _dot_nt(qv, k_ref[pl.ds(r0, tq), :])
            if masked:
                s = jnp.where(_chunk_mask(tq), s, NEG)
            m_new = jnp.maximum(m_prev, jnp.max(s, axis=-1, keepdims=True))
            a = jnp.exp(m_prev - m_new)
            p = jnp.exp(s - jnp.tile(m_new, (1, rep)))
            l_new = a * l_prev + jnp.sum(p, axis=-1, keepdims=True)
            acc = a * acc + _dot(p.astype(BF16), v_ref[pl.ds(r0, tq), :])
            return m_new, l_new, acc

        init = (jnp.full((tq, HP), NEG, F32), jnp.zeros((tq, HP), F32), jnp.zeros((tq, HP), F32))
        carry = lax.fori_loop(0, qi, lambda ki, cr: step(ki, cr, False), init)
        m_fin, l_fin, acc = step(qi, carry, True)
        o_ref[...] = (acc / l_fin).astype(BF16)
        lse_ref[...] = m_fin + jnp.log(l_fin)

    qspec = pl.BlockSpec((tq, HP), lambda h, qi: (qi, h))
    head = pl.BlockSpec((t, HP), lambda h, qi: (0, h))
    first = lambda: jnp.logical_and(pl.program_id(0) == 0, pl.program_id(1) == 0)
    last = lambda: jnp.logical_and(pl.program_id(0) == N_HEADS - 1, pl.program_id(1) == nq - 1)
    outs, couts = _call(
        body, comm, first, last, name=name, grid=(N_HEADS, nq), in_specs=[qspec, head, head], args=[q, k, v],
        out_specs=[qspec, pl.BlockSpec((None, tq, HP), lambda h, qi: (h, qi, 0))],
        out_shape=[SDS(q.shape, BF16), SDS((N_HEADS, t, HP), F32)], scratch=[], sem=("arbitrary",) * 2)
    return (*outs, couts)


def _flash_bwd(q, k, v, do, o, lse, name, comm=None):
    t = q.shape[0]
    tq = _rt(t)
    nq = t // tq
    rep = tq // HP

    def body(q_ref, k_ref, v_ref, do_ref, o_ref, lse_ref, dq_ref, dk_ref, dv_ref, delta_sc):
        def prep(qi, carry):
            r0 = pl.multiple_of(qi * tq, tq)
            rows = pl.ds(r0, tq)
            dlt = jnp.sum(do_ref[rows, :].astype(F32) * o_ref[rows, :].astype(F32), axis=-1, keepdims=True)
            delta_sc[rows, :] = jnp.broadcast_to(dlt, (tq, HP))
            dq_ref[rows, :] = jnp.zeros((tq, HP), F32)
            return carry

        lax.fori_loop(0, nq, prep, 0)

        def keys(ki, carry0):
            krows = pl.ds(pl.multiple_of(ki * tq, tq), tq)
            kt, vt = k_ref[krows, :], v_ref[krows, :]

            def step(qi, carry, masked):
                dk_acc, dv_acc = carry
                rows = pl.ds(pl.multiple_of(qi * tq, tq), tq)
                qt, dot_ = q_ref[rows, :], do_ref[rows, :]
                s = _dot_nt(qt, kt)
                if masked:
                    s = jnp.where(_chunk_mask(tq), s, NEG)
                p = jnp.exp(s - jnp.tile(lse_ref[rows, :], (1, rep)))
                ds = (p * (_dot_nt(dot_, vt) - jnp.tile(delta_sc[rows, :], (1, rep)))).astype(BF16)
                dv_acc = dv_acc + _dot_tn(p.astype(BF16), dot_)
                dk_acc = dk_acc + _dot_tn(ds, qt)
                dq_ref[rows, :] += _dot(ds, kt)
                return dk_acc, dv_acc

            zero = jnp.zeros((tq, HP), F32)
            carry = step(ki, (zero, zero), True)
            dk_acc, dv_acc = lax.fori_loop(ki + 1, nq, lambda qi, cr: step(qi, cr, False), carry)
            dk_ref[krows, :] = dk_acc
            dv_ref[krows, :] = dv_acc
            return carry0

        lax.fori_loop(0, nq, keys, 0)

    head = pl.BlockSpec((t, HP), lambda h: (0, h))
    outs, couts = _call(
        body, comm, lambda: pl.program_id(0) == 0, lambda: pl.program_id(0) == N_HEADS - 1, name=name, grid=(N_HEADS,),
        in_specs=[head] * 5 + [pl.BlockSpec((None, t, HP), lambda h: (h, 0, 0))], args=[q, k, v, do, o, lse],
        out_specs=[head] * 3, out_shape=[SDS(q.shape, F32)] * 3, scratch=[pltpu.VMEM((t, HP), F32)], sem=("arbitrary",))
    return (*outs, couts)


def _add_cast(gs, rs, place, name):
    n = len(gs)
    _, r, c = gs[0].shape
    tr, tc = _tile2(r, c // 2, 3 * n, 16)
    nct = c // 2 // tc

    def body(place_ref, *refs):
        for a in range(n):
            refs[2 * n + a][...] = (refs[a][...] + refs[n + a][...]).astype(BF16)

    gspec = pl.BlockSpec((None, tr, tc), lambda j, i, k, pr: (j, i, pr[0] * nct + k))
    rspec = pl.BlockSpec((None, tr, tc), lambda j, i, k, pr: (j, i, k))
    grid_spec = pltpu.PrefetchScalarGridSpec(num_scalar_prefetch=1, grid=(NCHIP, r // tr, nct),
                                             in_specs=[gspec] * n + [rspec] * n, out_specs=[rspec] * n)
    return pl.pallas_call(body, name=name, grid_spec=grid_spec, out_shape=[SDS((NCHIP, r, c // 2), BF16)] * n,
                          compiler_params=_cparams(("arbitrary",) * 3, VMEM_BIG))(place, *gs, *rs)


def _sum_chips(ss, qs, place, l, nl, prevs, name):
    n = len(ss)
    _, r, h = ss[0].shape
    tr, tc = _tile2(r, h, 3 * n, 16)
    nct = h // tc

    def body(place_ref, *refs):
        for a in range(n):
            acc = refs[n + a][...].astype(F32)
            for kk in range(NCHIP - 1):
                acc = acc + refs[a][kk].astype(F32)
            refs[-n + a][...] = acc

    in_specs = ([pl.BlockSpec((NCHIP - 1, tr, tc), lambda i, k, pr: (0, i, k))] * n
                + [pl.BlockSpec((None, tr, tc), lambda i, k, pr: (pr[1], i, k))] * n)
    args = [*ss, *qs]
    aliases = {}
    if prevs is not None:
        aliases = {1 + len(args) + a: a for a in range(n)}
        in_specs += [ANY] * n
        args += list(prevs)
    grid_spec = pltpu.PrefetchScalarGridSpec(
        num_scalar_prefetch=1, grid=(r // tr, nct), in_specs=in_specs,
        out_specs=[pl.BlockSpec((None, tr, tc), lambda i, k, pr: (l, i, pr[0] * nct + k))] * n)
    return pl.pallas_call(body, name=name, grid_spec=grid_spec, out_shape=[SDS((nl, r, 2 * h), F32)] * n,
                          input_output_aliases=aliases,
                          compiler_params=_cparams(("arbitrary",) * 2, VMEM_BIG))(place, *args)


def _cast_place(ws, place, name):
    n = len(ws)
    nl, r, c = ws[0].shape
    tr, tc = _tile2(r, c, 2 * n * nl, 16)

    def body(place_ref, *refs):
        for a in range(n * nl):
            refs[n * nl + a][...] = refs[a][...].astype(BF16)

    in_specs = [pl.BlockSpec((None, tr, tc), functools.partial(lambda l, i, k, pr: (l, i, k), l))
                for _ in range(n) for l in range(nl)]
    grid_spec = pltpu.PrefetchScalarGridSpec(
        num_scalar_prefetch=1, grid=(r // tr, c // tc), in_specs=in_specs,
        out_specs=[pl.BlockSpec((None, tr, tc), lambda i, k, pr: (pr[1], i, k))] * (n * nl))
    outs = pl.pallas_call(body, name=name, grid_spec=grid_spec, out_shape=[SDS((NCHIP, r, c), BF16)] * (n * nl),
                          compiler_params=_cparams(("arbitrary",) * 2, VMEM_BIG))(
                              place, *[w for w in ws for _ in range(nl)])
    return [outs[a * nl:(a + 1) * nl] for a in range(n)]


def _adamw(ws, gs, ms, vs, name):
    n = len(ws)
    r, c = ws[0].shape
    tr, tc = _tile2(r, c, 7 * n, 8)
    c1, c2 = 1.0 / (1.0 - B1 ** STEP), 1.0 / (1.0 - B2 ** STEP)

    def body(*refs):
        for a in range(n):
            w, g, m, v = (refs[kk * n + a][...] for kk in range(4))
            m2 = B1 * m + (1.0 - B1) * g
            v2 = B2 * v + (1.0 - B2) * (g * g)
            refs[4 * n + a][...] = -LR * ((m2 * c1) / (jnp.sqrt(v2 * c2) + EPS_ADAM) + WD * w)
            refs[5 * n + a][...] = m2
            refs[6 * n + a][...] = v2

    blk = pl.BlockSpec((tr, tc), lambda i, k: (i, k))
    outs = pl.pallas_call(body, name=name, grid=(r // tr, c // tc), in_specs=[blk] * (4 * n),
                          out_specs=[blk] * (3 * n), out_shape=[SDS((r, c), F32)] * (3 * n),
                          compiler_params=_cparams(("arbitrary",) * 2, VMEM_BIG))(*ws, *gs, *ms, *vs)
    return outs[:n], outs[n:2 * n], outs[2 * n:]


def _place():
    x, y, c = lax.axis_index("x"), lax.axis_index("y"), lax.axis_index("c")
    return x, y, c, [(1 - x, y), (x, 1 - y), (1 - x, 1 - y)]


def _rcopy(src, dst, ssem, rsem, k, dev):
    return pltpu.make_async_remote_copy(src_ref=src, dst_ref=dst, send_sem=ssem.at[k], recv_sem=rsem.at[k],
                                        device_id=dev, device_id_type=MESH)


def _half(ref, lead, cc):
    h = ref.shape[-1] // 2
    return ref.at[(*lead, slice(None), pl.ds(cc * h, h))]


def _per_core(fn):
    c = lax.axis_index("c")
    for cc in (0, 1):
        pl.when(c == cc)(functools.partial(fn, cc))


def _gather_comm(bufs):
    n = len(bufs)

    def plan(couts, ssem, rsem, cc):
        x, y, _, peers = _place()
        me, sib = 2 * x + y, (x, y, 1 - cc)
        send, recv, fwd, recv2 = [], [], [], []
        for a in range(n):
            for kk, (px, py) in enumerate(peers):
                mine = _half(couts[a], (me,), cc)
                got = _half(couts[a], (2 * px + py,), cc)
                other = _half(couts[a], (2 * px + py,), 1 - cc)
                send.append(_rcopy(mine, mine, ssem, rsem, a * 6 + kk, (px, py, cc)))
                recv.append(_rcopy(got, got, ssem, rsem, a * 6 + kk, (px, py, cc)))
                fwd.append(_rcopy(got, got, ssem, rsem, a * 6 + 3 + kk, sib))
                recv2.append(_rcopy(other, other, ssem, rsem, a * 6 + 3 + kk, sib))
        return send, recv, fwd, recv2

    def start(cins, couts, ssem, rsem):
        def go(cc):
            for d in plan(couts, ssem, rsem, cc)[0]:
                d.start()

        _per_core(go)

    def finish(cins, couts, ssem, rsem):
        def go(cc):
            send, recv, fwd, recv2 = plan(couts, ssem, rsem, cc)
            for dr, df in zip(recv, fwd):
                dr.wait_recv()
                df.start()
            for d in recv2:
                d.wait_recv()
            for d in send + fwd:
                d.wait_send()

        _per_core(go)

    return _Comm(bufs, [SDS(b.shape, b.dtype) for b in bufs], {a: a for a in range(n)}, 6 * n, start, finish)


def _pair_comm(gs):
    n = len(gs)
    halves = [g.shape[-1] // 2 for g in gs]

    def plan(cins, couts, ssem, rsem, cc):
        x, y, _, _ = _place()
        return [_rcopy(cins[a].at[:, :, pl.ds((1 - cc) * halves[a], halves[a])], couts[a], ssem, rsem, a, (x, y, 1 - cc))
                for a in range(n)]

    def start(cins, couts, ssem, rsem):
        def go(cc):
            for d in plan(cins, couts, ssem, rsem, cc):
                d.start()

        _per_core(go)

    def finish(cins, couts, ssem, rsem):
        def go(cc):
            ds = plan(cins, couts, ssem, rsem, cc)
            for d in ds:
                d.wait_recv()
            for d in ds:
                d.wait_send()

        _per_core(go)

    return _Comm(gs, [SDS(g.shape[:-1] + (g.shape[-1] // 2,), g.dtype) for g in gs], {}, n, start, finish)


def _chips_comm(qs):
    n = len(qs)

    def plan(cins, couts, ssem, rsem):
        x, y, c, peers = _place()
        return [_rcopy(cins[a].at[2 * px + py], couts[a].at[kk], ssem, rsem, a * 3 + kk, (px, py, c))
                for a in range(n) for kk, (px, py) in enumerate(peers)]

    def start(cins, couts, ssem, rsem):
        for d in plan(cins, couts, ssem, rsem):
            d.start()

    def finish(cins, couts, ssem, rsem):
        ds = plan(cins, couts, ssem, rsem)
        for d in ds:
            d.wait_recv()
        for d in ds:
            d.wait_send()

    return _Comm(qs, [SDS((NCHIP - 1,) + q.shape[1:], q.dtype) for q in qs], {}, 3 * n, start, finish)


def _share_comm(fs, l):
    n = len(fs)

    def plan(couts, ssem, rsem, cc, which):
        x, y, _, _ = _place()
        out = []
        for a in range(n):
            piece = _half(couts[a], (l,), which)
            out.append(_rcopy(piece, piece, ssem, rsem, a, (x, y, 1 - cc)))
        return out

    def start(cins, couts, ssem, rsem):
        def go(cc):
            for d in plan(couts, ssem, rsem, cc, cc):
                d.start()

        _per_core(go)

    def finish(cins, couts, ssem, rsem):
        def go(cc):
            for d in plan(couts, ssem, rsem, cc, 1 - cc):
                d.wait_recv()
            for d in plan(couts, ssem, rsem, cc, cc):
                d.wait_send()

        _per_core(go)

    return _Comm(fs, [SDS(f.shape, f.dtype) for f in fs], {a: a for a in range(n)}, n, start, finish)


def _exchange(buf, reduce, name):
    r, w = buf.shape

    def body(in_ref, out_ref, recv_sc, ssem, rsem):
        x, y, c, _ = _place()
        me = 4 * x + 2 * y + c
        sends = []
        for rel in range(1, N_DEV):
            dev = (1 - x if rel & 4 else x, 1 - y if rel & 2 else y, 1 - c if rel & 1 else c)
            d = _rcopy(in_ref, recv_sc.at[me], ssem, rsem, rel - 1, dev)
            d.start()
            sends.append(d)
        recv_sc[me] = in_ref[...]
        for rel in range(1, N_DEV):
            px, py, pc = (1 - x if rel & 4 else x, 1 - y if rel & 2 else y, 1 - c if rel & 1 else c)
            slot = recv_sc.at[4 * px + 2 * py + pc]
            _rcopy(slot, slot, ssem, rsem, rel - 1, (px, py, pc)).wait_recv()
        for d in sends:
            d.wait_send()
        if reduce:
            acc = recv_sc[0]
            for dv in range(1, N_DEV):
                acc = acc + recv_sc[dv]
            out_ref[...] = acc
        else:
            out_ref[...] = recv_sc[...]

    vm = pl.BlockSpec(memory_space=pltpu.VMEM)
    return pl.pallas_call(
        body, name=name, in_specs=[vm], out_specs=vm,
        out_shape=SDS((r, w) if reduce else (N_DEV, r, w), F32),
        scratch_shapes=[pltpu.VMEM((N_DEV, r, w), F32), pltpu.SemaphoreType.DMA((N_DEV - 1,)),
                        pltpu.SemaphoreType.DMA((N_DEV - 1,))],
        compiler_params=_cparams(None, VMEM_BIG))(buf)


def _cols_full(g):
    _, k, ns = g.shape
    return g.transpose(1, 0, 2).reshape(k, NCHIP * ns)


def _cols_shards(w):
    k, n = w.shape
    return w.reshape(k, NCHIP, n // NCHIP).transpose(1, 0, 2)


def _rows_pad(rows, width):
    out = jnp.concatenate([jnp.pad(a, ((0, 0), (0, width - a.shape[1]))) for a in rows], axis=0)
    return jnp.pad(out, ((0, -out.shape[0] % 8), (0, 0)))


def kernel(x, positions, ffn1_norm, ffn1_w_gate, ffn1_w_up, ffn1_w_down, mix_norm, w_in, gate_bias, conv_w, conv_b, conv_ln_g, conv_ln_b, w_conv_out, cq_norm, ckv_norm, w_uq, w_ukv, q_norm, k_norm, w_mla_out, w_out, ffn2_norm, ffn2_w_gate, ffn2_w_up, ffn2_w_down, loss_target, m_ffn1_norm, m_ffn1_w_gate, m_ffn1_w_up, m_ffn1_w_down, m_mix_norm, m_w_in, m_gate_bias, m_conv_w, m_conv_b, m_conv_ln_g, m_conv_ln_b, m_w_conv_out, m_cq_norm, m_ckv_norm, m_w_uq, m_w_ukv, m_q_norm, m_k_norm, m_w_mla_out, m_w_out, m_ffn2_norm, m_ffn2_w_gate, m_ffn2_w_up, m_ffn2_w_down, v_ffn1_norm, v_ffn1_w_gate, v_ffn1_w_up, v_ffn1_w_down, v_mix_norm, v_w_in, v_gate_bias, v_conv_w, v_conv_b, v_conv_ln_g, v_conv_ln_b, v_w_conv_out, v_cq_norm, v_ckv_norm, v_w_uq, v_w_ukv, v_q_norm, v_k_norm, v_w_mla_out, v_w_out, v_ffn2_norm, v_ffn2_w_gate, v_ffn2_w_up, v_ffn2_w_down):
    names = ["ffn1_norm", "ffn1_w_gate", "ffn1_w_up", "ffn1_w_down", "mix_norm", "w_in", "gate_bias", "conv_w",
             "conv_b", "conv_ln_g", "conv_ln_b", "w_conv_out", "cq_norm", "ckv_norm", "w_uq", "w_ukv", "q_norm",
             "k_norm", "w_mla_out", "w_out", "ffn2_norm", "ffn2_w_gate", "ffn2_w_up", "ffn2_w_down"]
    env = dict(locals())
    turned = ("ffn1_w_gate", "ffn1_w_up", "ffn2_w_gate", "ffn2_w_up", "w_in", "w_uq")
    view = lambda nm, a: jnp.swapaxes(a, 1, 2) if nm in turned else a
    wts = {nm: view(nm, env[nm]) for nm in names}
    mom = {nm: view(nm, env["m_" + nm]) for nm in names}
    var = {nm: view(nm, env["v_" + nm]) for nm in names}

    t, d = x.shape[1], x.shape[2]
    nl = ffn1_norm.shape[0]
    cc = conv_b.shape[1]
    ql, kvl = cq_norm.shape[1], ckv_norm.shape[1]
    vw = N_HEADS * V_DIM
    hw = N_HEADS * HP
    lay = {"a": 2 * d, "cq": 2 * d + 2 * cc, "ckv": 2 * d + 2 * cc + ql, "kr": 2 * d + 2 * cc + ql + kvl,
           "wm": ql + kvl + 2 * HP}
    dp = lay["cq"] + lay["wm"]
    nat_g = 2 * cc + ql + kvl + ROPE
    assert lay["cq"] % lay["wm"] == 0 and lay["cq"] % ql == 0 and lay["ckv"] % kvl == 0 and lay["a"] % (2 * cc) == 0
    assert cc % HP == 0 and d % HP == 0 and t % (2 * CHUNK) == 0 and w_in.shape[2] * NCHIP == nat_g + 2 * d
    assert nl == 2

    x0, target = x[0], loss_target[0]
    chip = 2 * lax.axis_index("x") + lax.axis_index("y")
    place = jnp.stack([lax.axis_index("c"), chip]).astype(jnp.int32)

    inv_freq = ROPE_THETA ** (-jnp.arange(0, ROPE, 2, dtype=F32) / ROPE)
    ang = positions[0].astype(F32)[:, None] * inv_freq
    cos, sin, z = jnp.cos(ang), jnp.sin(ang), jnp.zeros((t, ROPE // 2), F32)
    tabs = (jnp.concatenate([jnp.ones((t, NOPE), F32), cos, cos, jnp.zeros((t, HP - QK), F32)], axis=1),
            jnp.concatenate([jnp.zeros((t, NOPE), F32), -sin, z, jnp.zeros((t, HP - QK), F32)], axis=1),
            jnp.concatenate([jnp.zeros((t, NOPE), F32), z, sin, jnp.zeros((t, HP - QK), F32)], axis=1))

    big = ["ffn1_w_gate", "ffn1_w_up", "ffn1_w_down", "ffn2_w_gate", "ffn2_w_up", "ffn2_w_down",
           "w_in", "w_conv_out", "w_uq", "w_ukv", "w_mla_out", "w_out"]
    ffn1_w, ffn2_w, mix_w = big[0:3], big[3:6], big[6:]
    bufs, like = {}, {}
    for nm in big:
        like.setdefault(wts[nm].shape, []).append(nm)
    for gi, grp in enumerate(like.values()):
        for nm, per_layer in zip(grp, _cast_place([wts[nm] for nm in grp], place, f"cast_place_{gi}")):
            for l in range(nl):
                bufs[nm, l] = per_layer[l]

    def gather(keys):
        return _gather_comm([bufs[key] for key in keys])

    def landed(keys, outs):
        for key, o in zip(keys, outs):
            bufs[key] = o

    chunk = lambda nms, l: [(nm, l) for nm in nms]
    first_keys = chunk(ffn1_w, 0)
    landed(first_keys, _comm_call(gather(first_keys), "gather_first"))
    ride = {("ffn1", 0): chunk(mix_w, 0), ("flash", 0): chunk(ffn2_w, 0) + chunk(ffn1_w, 1),
            ("ffn2", 0): chunk(mix_w, 1), ("ffn1", 1): chunk(ffn2_w, 1)}

    def riding(kind, l):
        keys = ride.get((kind, l))
        return keys or [], (None if keys is None else gather(keys))

    def mixer_weights(l):
        w_in_nat = bufs["w_in", l].reshape(nat_g + 2 * d, d)
        w_in_k = jnp.concatenate([w_in_nat[nat_g:], w_in_nat[:nat_g], jnp.zeros((dp - nat_g - 2 * d, d), BF16)], axis=0)
        w_uq_k = jnp.pad(bufs["w_uq", l].reshape(N_HEADS, QK, ql), ((0, 0), (0, HP - QK), (0, 0))).reshape(hw, ql)
        w_mo_k = jnp.pad(_cols_full(bufs["w_mla_out", l]).reshape(N_HEADS, V_DIM, d),
                         ((0, 0), (HP - V_DIM, 0), (0, 0))).reshape(hw, d)
        return dict(w_in=w_in_k, w_co=_cols_full(bufs["w_conv_out", l]), w_uq=w_uq_k,
                    w_ukv=_cols_full(bufs["w_ukv", l]), w_mo=w_mo_k, w_out=bufs["w_out", l].reshape(d, d))

    qn_k = jnp.pad(q_norm, ((0, 0), (0, HP - QK)))
    kn_k = jnp.pad(k_norm, ((0, 0), (0, HP - QK)))
    small = _rows_pad([gate_bias.reshape(nl * 2, d // NCHIP), conv_w.reshape(nl * KW, cc // NCHIP)], d)
    everyone = _exchange(small, False, "gather_small")[0::2]
    gb_k = everyone[:, :nl * 2, :d // NCHIP].reshape(NCHIP, nl, 2, d // NCHIP).transpose(1, 2, 0, 3).reshape(nl, 2, d)
    cw = everyone[:, nl * 2:nl * 2 + nl * KW, :cc // NCHIP].reshape(NCHIP, nl, KW, cc // NCHIP)
    cw_k = jnp.pad(cw.transpose(1, 2, 0, 3).reshape(nl, KW, cc), ((0, 0), (0, CPAD - KW), (0, 0)))

    saved, mixw = [], []
    xc = x0
    for l in range(nl):
        keys, comm = riding("ffn1", l)
        x1, gt1, up1, got = _ffn_fwd(xc, ffn1_norm, l, *[bufs[nm, l] for nm in ffn1_w], f"ffn1_fwd_{l}", comm)
        landed(keys, got)
        mw = mixer_weights(l)
        mixw.append(mw)
        hm = _rms_fwd(x1, mix_norm, l, f"mix_norm_{l}")
        proj = _mm(hm, mw["w_in"], "nt", name=f"proj_{l}")
        uc = _dwconv_fwd(proj, cw_k, conv_b, l, lay["a"], cc, f"dwconv_{l}")
        sc = _ln_silu(uc, conv_ln_g, conv_ln_b, l, f"conv_ln_{l}")
        yc = _mm(sc, mw["w_co"], "nn", name=f"conv_out_{l}")
        q, k, kv = _mla_pre_fwd(proj, lay, l, cq_norm, ckv_norm, qn_k, kn_k, mw["w_uq"], mw["w_ukv"], tabs, f"mla_pre_{l}")
        keys, comm = riding("flash", l)
        o, lse, got = _flash_fwd(q, k, kv, f"flash_{l}", comm)
        landed(keys, got)
        ym = _mm(o, mw["w_mo"], "nn", name=f"mla_out_{l}")
        yv = _comb_fwd(proj, gb_k, l, yc, ym, f"combine_{l}")
        x2 = _mm(yv, mw["w_out"], "nn", name=f"mix_out_{l}", res=x1)
        keys, comm = riding("ffn2", l)
        x3, gt2, up2, got = _ffn_fwd(x2, ffn2_norm, l, *[bufs[nm, l] for nm in ffn2_w], f"ffn2_fwd_{l}", comm)
        landed(keys, got)
        saved.append(dict(x0=xc, x1=x1, gt1=gt1, up1=up1, hm=hm, proj=proj, uc=uc, sc=sc, yc=yc, q=q, k=k, kv=kv,
                          o=o, lse=lse, ym=ym, yv=yv, x2=x2, gt2=gt2, up2=up2))
        xc = x3
    dx, loss_row = _loss_grad(xc, target, "loss")

    gw = {nm: [None] * nl for nm in names}
    red = None

    def shard_groups(l):
        same = {}
        for nm in big:
            same.setdefault(gw[nm][l].shape, []).append(nm)
        return list(same.values())

    def add_parts(l, sib_part):
        qb = {}
        for gi, grp in enumerate(shard_groups(l)):
            outs = _add_cast([gw[nm][l] for nm in grp], [sib_part[nm] for nm in grp], place, f"rs_add_{l}_{gi}")
            qb.update(zip(grp, outs))
        return qb

    def sum_parts(l, qb, parts, prev):
        out = {}
        for gi, grp in enumerate(shard_groups(l)):
            outs = _sum_chips([parts[nm] for nm in grp], [qb[nm] for nm in grp], place, l, nl,
                              None if prev is None else [prev[nm] for nm in grp], f"rs_sum_{l}_{gi}")
            out.update(zip(grp, outs))
        return out

    for l in reversed(range(nl)):
        s, mw = saved[l], mixw[l]
        early = l + 1 if l + 1 < nl else None

        def ffn_back(tag, xin, gains, gts, ups, dout, comm=None):
            wg, wu, wd = (bufs[f"{tag}_w_{p}", l] for p in ("gate", "up", "down"))
            dxi, dgain, dgt, dup, act, hb, dob, got = _ffn_bwd(xin, gains, l, wg, wu, wd, gts, ups, dout,
                                                              f"{tag}_bwd_{l}", comm)
            gw[f"{tag}_w_gate"][l] = _mm(dgt, hb, "tn", name=f"{tag}_dwg_{l}", a_g=True)
            gw[f"{tag}_w_up"][l] = _mm(dup, hb, "tn", name=f"{tag}_dwu_{l}", a_g=True)
            gw[f"{tag}_w_down"][l] = _mm(act, dob, "tn", name=f"{tag}_dwd_{l}", a_g=True)
            gw[f"{tag}_norm"][l] = dgain
            return dxi, got

        comm = None if early is None else _pair_comm([gw[nm][early] for nm in big])
        dx2, got = ffn_back("ffn2", s["x2"], ffn2_norm, s["gt2"], s["up2"], dx, comm)
        if early is not None:
            qb = add_parts(early, dict(zip(big, got)))
        dyv = _mm(dx2, mw["w_out"], "nt", name=f"mix_out_dy_{l}")
        gw["w_out"][l] = _mm(s["yv"], dx2, "tn", name=f"mix_out_dw_{l}").reshape(NCHIP, d // NCHIP, d)
        dyc, dym, dproj, dgb = _comb_bwd(s["proj"], gb_k, l, s["yc"], s["ym"], dyv, dp, f"combine_bwd_{l}")
        gw["gate_bias"][l] = dgb
        dsc = _mm(dyc, mw["w_co"], "nt", name=f"conv_out_ds_{l}")
        gw["w_conv_out"][l] = _cols_shards(_mm(s["sc"], dyc, "tn", name=f"conv_out_dw_{l}"))
        duc, gw["conv_ln_g"][l], gw["conv_ln_b"][l], gw["conv_b"][l] = _ln_silu_bwd(
            s["uc"], conv_ln_g, conv_ln_b, l, dsc, f"conv_ln_bwd_{l}")
        du, dcw = _dwconv_bwd(duc, s["proj"], cw_k, l, lay["a"], cc, f"dwconv_bwd_{l}")
        gw["conv_w"][l] = dcw[:KW]
        dproj = _glu_bwd(du, s["proj"], dproj, lay["a"], cc, f"glu_bwd_{l}")
        do = _mm(dym, mw["w_mo"], "nt", name=f"mla_out_do_{l}", out_dtype=BF16)
        dwmo = _mm(s["o"], dym, "tn", name=f"mla_out_dw_{l}").reshape(N_HEADS, HP, d)[:, HP - V_DIM:].reshape(vw, d)
        gw["w_mla_out"][l] = _cols_shards(dwmo)
        comm = None if early is None else _chips_comm([qb[nm] for nm in big])
        dq, dk, dv, got = _flash_bwd(s["q"], s["k"], s["kv"], do, s["o"], s["lse"], f"flash_bwd_{l}", comm)
        if early is not None:
            red = sum_parts(early, qb, dict(zip(big, got)), red)
        dproj, dqr, dkv, cqn, ckvn, gw["cq_norm"][l], gw["ckv_norm"][l], dqn, dkn = _mla_pre_bwd(
            s["proj"], lay, l, cq_norm, ckv_norm, qn_k, kn_k, mw["w_uq"], mw["w_ukv"], tabs, dq, dk, dv, dproj,
            f"mla_pre_bwd_{l}")
        gw["q_norm"][l], gw["k_norm"][l] = dqn[:, :QK], dkn[:, :QK]
        dwuq = _mm(dqr, cqn, "tn", name=f"uq_dw_{l}").reshape(N_HEADS, HP, ql)[:, :QK]
        gw["w_uq"][l] = dwuq.reshape(NCHIP, N_HEADS * QK // NCHIP, ql)
        gw["w_ukv"][l] = _cols_shards(_mm(ckvn, dkv, "tn", name=f"ukv_dw_{l}"))
        dhm = _mm(dproj, mw["w_in"], "nn", name=f"proj_dh_{l}")
        dwin = _mm(dproj, s["hm"], "tn", name=f"proj_dw_{l}")
        gw["w_in"][l] = jnp.concatenate([dwin[2 * d:2 * d + nat_g], dwin[:2 * d]], axis=0).reshape(
            NCHIP, (nat_g + 2 * d) // NCHIP, d)
        dx1, gw["mix_norm"][l] = _rms_back(s["x1"], mix_norm, l, dhm, dx2, f"mix_norm_bwd_{l}")
        comm = None if early is None else _share_comm([red[nm] for nm in big], early)
        dx, got = ffn_back("ffn1", s["x0"], ffn1_norm, s["gt1"], s["up1"], dx1, comm)
        if early is not None:
            red = dict(zip(big, got))

    sib_part = dict(zip(big, _comm_call(_pair_comm([gw[nm][0] for nm in big]), "rs_pair")))
    qb = add_parts(0, sib_part)
    parts = dict(zip(big, _comm_call(_chips_comm([qb[nm] for nm in big]), "rs_chips")))
    red = sum_parts(0, qb, parts, red)
    red = dict(zip(big, _comm_call(_share_comm([red[nm] for nm in big], 0), "rs_share")))
    grads = {nm: g.reshape(wts[nm].shape) for nm, g in red.items()}

    vec_names = ["ffn1_norm", "mix_norm", "ffn2_norm", "conv_b", "conv_ln_g", "conv_ln_b", "cq_norm", "ckv_norm",
                 "q_norm", "k_norm"]
    rows = [jnp.concatenate(gw[nm], axis=0) for nm in vec_names]
    rows += [jnp.concatenate(gw["gate_bias"], axis=0), jnp.concatenate(gw["conv_w"], axis=0), loss_row]
    total = _exchange(_rows_pad(rows, d), True, "allreduce_small")
    r0 = 0
    for nm in vec_names:
        grads[nm] = total[r0:r0 + nl, :wts[nm].shape[1]]
        r0 += nl
    gb_all = total[r0:r0 + 2 * nl, :d].reshape(nl, 2, NCHIP, d // NCHIP)
    r0 += 2 * nl
    cw_all = total[r0:r0 + KW * nl, :cc].reshape(nl, KW, NCHIP, cc // NCHIP)
    r0 += KW * nl
    grads["gate_bias"] = lax.dynamic_index_in_dim(gb_all, chip, axis=2, keepdims=False)
    grads["conv_w"] = lax.dynamic_index_in_dim(cw_all, chip, axis=2, keepdims=False)
    loss = total[r0, 0]

    delta, new_m, new_v = {}, {}, {}
    by_shape = {}
    for nm in names:
        shp = wts[nm].shape
        by_shape.setdefault((shp[0] * (shp[1] if len(shp) == 3 else 1), shp[-1]), []).append(nm)
    for gi, (shp2, grp) in enumerate(by_shape.items()):
        to2 = lambda a: a.reshape(shp2)
        ds_, ms_, vs_ = _adamw([to2(wts[nm]) for nm in grp], [to2(grads[nm]) for nm in grp],
                               [to2(mom[nm]) for nm in grp], [to2(var[nm]) for nm in grp], f"adamw_{gi}")
        for nm, dd, mm_, vv in zip(grp, ds_, ms_, vs_):
            delta[nm], new_m[nm], new_v[nm] = (a.reshape(wts[nm].shape) for a in (dd, mm_, vv))

    return (loss, dx[None], *[view(nm, grads[nm]) for nm in names], *[view(nm, delta[nm]) for nm in names],
            *[view(nm, new_m[nm]) for nm in names], *[view(nm, new_v[nm]) for nm in names])
```

```python
import functools

import jax
import jax.numpy as jnp
from jax import lax
from jax.experimental import pallas as pl
from jax.experimental.pallas import tpu as pltpu

F32, BF16 = jnp.float32, jnp.bfloat16
SDS = jax.ShapeDtypeStruct
MESH = pl.DeviceIdType.MESH
ANY = pl.BlockSpec(memory_space=pl.ANY)

NCHIP = 4
N_DEV = 8
N_HEADS, NOPE, ROPE, V_DIM = 8, 64, 32, 64
QK = NOPE + ROPE
HP = 128
CHUNK = 64
KW = 31
CPAD = 32
ROPE_THETA = 10000.0
EPS = 1e-6
LR, B1, B2, EPS_ADAM, WD, STEP = 0.001, 0.9, 0.999, 1e-08, 0.01, 10
VMEM_BIG = 48 << 20
FFN_PARTS = 2


def _cparams(sem=None, vmem=None):
    kw = {}
    if sem is not None:
        kw["dimension_semantics"] = sem
    if vmem is not None:
        kw["vmem_limit_bytes"] = vmem
    return pltpu.CompilerParams(**kw)


def _rt(t):
    return min(512, t // 2)


def _pick(n, cands):
    for c in cands:
        if c <= n and n % c == 0:
            return c
    return n


def _tile2(r, c, nblocks, row_mult):
    tr, tc = r, c
    while 2 * nblocks * tr * tc * 4 > VMEM_BIG // 2 and tr % (2 * row_mult) == 0:
        tr //= 2
    while 2 * nblocks * tr * tc * 4 > VMEM_BIG // 2 and tc % 256 == 0:
        tc //= 2
    return tr, tc


def _sig(v):
    return 1.0 / (1.0 + jnp.exp(-v))


def _rms_r(v):
    return lax.rsqrt(jnp.mean(v * v, axis=-1, keepdims=True) + EPS)


def _rms_bwd(xv, g, dy):
    r = _rms_r(xv)
    xh = xv * r
    dg = jnp.sum(dy * xh, axis=0, keepdims=True)
    dxh = dy * g
    dx = r * (dxh - xh * jnp.mean(dxh * xh, axis=-1, keepdims=True))
    return dx, dg


def _dot(a, b):
    return jnp.dot(a, b, preferred_element_type=F32)


def _dot_nt(a, b):
    return lax.dot_general(a, b, (((1,), (1,)), ((), ())), preferred_element_type=F32)


def _dot_tn(a, b):
    return lax.dot_general(a, b, (((0,), (0,)), ((), ())), preferred_element_type=F32)


class _Comm:
    def __init__(self, ins, out_shapes, aliases, nsem, start, finish):
        self.ins, self.out_shapes, self.aliases, self.nsem = list(ins), list(out_shapes), dict(aliases), nsem
        self.start, self.finish = start, finish


def _call(body, comm, first, last, *, name, grid, in_specs, args, out_specs, out_shape, scratch, sem):
    in_specs, args, out_specs, out_shape, scratch = map(list, (in_specs, args, out_specs, out_shape, scratch))
    n_in, n_out, n_scr = len(args), len(out_shape), len(scratch)
    aliases = {}
    kern = body
    if comm is not None:
        nci, nco = len(comm.ins), len(comm.out_shapes)
        o0 = n_in + nci
        s0 = o0 + n_out + nco

        def kern(*refs):
            cins, couts = refs[n_in:o0], refs[o0 + n_out:s0]
            ssem, rsem = refs[s0 + n_scr:]
            pl.when(first())(lambda: comm.start(cins, couts, ssem, rsem))
            body(*refs[:n_in], *refs[o0:o0 + n_out], *refs[s0:s0 + n_scr])
            pl.when(last())(lambda: comm.finish(cins, couts, ssem, rsem))

        in_specs += [ANY] * nci
        args += comm.ins
        out_specs += [ANY] * nco
        out_shape += comm.out_shapes
        scratch += [pltpu.SemaphoreType.DMA((comm.nsem,))] * 2
        aliases = {n_in + i: n_out + o for i, o in comm.aliases.items()}
    outs = pl.pallas_call(kern, name=name, grid=grid, in_specs=in_specs, out_specs=out_specs, out_shape=out_shape,
                          scratch_shapes=scratch, input_output_aliases=aliases,
                          compiler_params=_cparams(sem, VMEM_BIG))(*args)
    return outs[:n_out], outs[n_out:]


def _comm_call(comm, name):
    once = lambda: pl.program_id(0) == 0
    return _call(lambda: None, comm, once, once, name=name, grid=(1,), in_specs=[], args=[], out_specs=[],
                 out_shape=[], scratch=[], sem=("arbitrary",))[1]


def _mm(a, b, mode, *, name, out_dtype=F32, res=None, a_g=False):
    a2 = a.shape[1:] if a_g else a.shape
    if mode == "nn":
        (m, k), (k2, n) = a2, b.shape
    elif mode == "nt":
        (m, k), (n, k2) = a2, b.shape
    else:
        (k, m), (k2, n) = a2, b.shape
    assert k == k2, (name, a.shape, b.shape)
    g = a.shape[0] if a_g else 1
    tm = m if m <= 768 else _pick(m, (768, 512, 256, 128))
    tn = n if n <= 1280 else _pick(n, (1280, 1024, 768, 512, 256, 128))
    tk = k if k <= 1280 else _pick(k, (1280, 1024, 768, 512, 256, 128))
    fits = lambda kk, nn: 2 * (kk * tm * a.dtype.itemsize + kk * nn * b.dtype.itemsize + tm * nn * 4) <= VMEM_BIG - (8 << 20)
    if fits(k, n):
        tn, tk = n, k
    elif fits(k, tn):
        tk = k
    nk = k // tk
    dn = {"nn": (((1,), (0,)), ((), ())), "nt": (((1,), (1,)), ((), ())), "tn": (((0,), (0,)), ((), ()))}[mode]

    def body(*refs):
        a_ref, b_ref = refs[0], refs[1]
        res_ref = refs[2] if res is not None else None
        o_ref = refs[3] if res is not None else refs[2]
        p = lax.dot_general(a_ref[...].astype(BF16), b_ref[...].astype(BF16), dn, preferred_element_type=F32)

        def fin(v):
            if res_ref is not None:
                v = v + res_ref[...]
            o_ref[...] = v.astype(out_dtype)

        if nk == 1:
            fin(p)
        else:
            acc_ref = refs[-1]
            kk = pl.program_id(3)

            @pl.when(kk == 0)
            def _():
                acc_ref[...] = p

            @pl.when(kk > 0)
            def _():
                acc_ref[...] += p

            @pl.when(kk == nk - 1)
            def _():
                fin(acc_ref[...])

    a_block, a_idx = ((tk, tm), lambda gg, i, j, kk: (kk, i)) if mode == "tn" else ((tm, tk), lambda gg, i, j, kk: (i, kk))
    if a_g:
        a_spec = pl.BlockSpec((None,) + a_block, lambda gg, i, j, kk: (gg,) + a_idx(gg, i, j, kk))
    else:
        a_spec = pl.BlockSpec(a_block, a_idx)
    if mode == "nt":
        b_spec = pl.BlockSpec((tn, tk), lambda gg, i, j, kk: (j, kk))
    else:
        b_spec = pl.BlockSpec((tk, tn), lambda gg, i, j, kk: (kk, j))
    in_specs, args = [a_spec, b_spec], [a, b]
    if res is not None:
        in_specs.append(pl.BlockSpec((tm, tn), lambda gg, i, j, kk: (i, j)))
        args.append(res)
    if a_g:
        out_spec, oshape = pl.BlockSpec((None, tm, tn), lambda gg, i, j, kk: (gg, i, j)), (g, m, n)
    else:
        out_spec, oshape = pl.BlockSpec((tm, tn), lambda gg, i, j, kk: (i, j)), (m, n)
    return pl.pallas_call(
        body, name=name, grid=(g, m // tm, n // tn, nk), in_specs=in_specs, out_specs=out_spec,
        out_shape=SDS(oshape, out_dtype), scratch_shapes=[pltpu.VMEM((tm, tn), F32)] if nk > 1 else [],
        compiler_params=_cparams(("arbitrary",) * 4, VMEM_BIG))(*args)


def _ffn_fwd(x, gains, l, wg, wu, wd, name, comm=None):
    t, d = x.shape
    f = wg.shape[-2]
    tt = _rt(t)

    def body(x_ref, g_ref, wg_ref, wu_ref, wd_ref, o_ref, gt_ref, up_ref, h_sc, acc_sc):
        j = pl.program_id(1)

        @pl.when(j == 0)
        def _():
            xv = x_ref[...]
            h_sc[...] = (xv * _rms_r(xv) * g_ref[l:l + 1, :]).astype(BF16)
            acc_sc[...] = jnp.zeros_like(acc_sc)

        for part in range(FFN_PARTS):
            rows = pl.ds(part * (tt // FFN_PARTS), tt // FFN_PARTS)
            h = h_sc[rows, :]
            gt = _dot_nt(h, wg_ref[...]).astype(BF16)
            up = _dot_nt(h, wu_ref[...]).astype(BF16)
            gt_ref[rows, :] = gt
            up_ref[rows, :] = up
            acc_sc[rows, :] += _dot(gt * _sig(gt) * up, wd_ref[...])

        @pl.when(j == NCHIP - 1)
        def _():
            o_ref[...] = x_ref[...] + 0.5 * acc_sc[...]

    wspec = pl.BlockSpec((None, f, d), lambda i, j: (j, 0, 0))
    row = pl.BlockSpec((tt, d), lambda i, j: (i, 0))
    sh = pl.BlockSpec((None, tt, f), lambda i, j: (j, i, 0))
    ni = t // tt
    first = lambda: jnp.logical_and(pl.program_id(0) == 0, pl.program_id(1) == 0)
    last = lambda: jnp.logical_and(pl.program_id(0) == ni - 1, pl.program_id(1) == NCHIP - 1)
    outs, couts = _call(
        body, comm, first, last, name=name, grid=(ni, NCHIP),
        in_specs=[row, pl.BlockSpec(gains.shape, lambda i, j: (0, 0)), wspec, wspec, wspec], args=[x, gains, wg, wu, wd],
        out_specs=[row, sh, sh], out_shape=[SDS((t, d), F32), SDS((NCHIP, t, f), BF16), SDS((NCHIP, t, f), BF16)],
        scratch=[pltpu.VMEM((tt, d), BF16), pltpu.VMEM((tt, d), F32)], sem=("arbitrary", "arbitrary"))
    return (*outs, couts)


def _ffn_bwd(x, gains, l, wg, wu, wd, gts, ups, dout, name, comm=None):
    t, d = x.shape
    f = wg.shape[-2]
    tt = _rt(t)

    def body(x_ref, g_ref, wg_ref, wu_ref, wd_ref, gt_ref, up_ref, do_ref,
             dx_ref, dg_ref, dgt_ref, dup_ref, act_ref, h_ref, dob_ref, dh_sc):
        i, j = pl.program_id(0), pl.program_id(1)

        @pl.when(j == 0)
        def _():
            xv = x_ref[...]
            h_ref[...] = (xv * _rms_r(xv) * g_ref[l:l + 1, :]).astype(BF16)
            dob_ref[...] = (0.5 * do_ref[...]).astype(BF16)
            dh_sc[...] = jnp.zeros_like(dh_sc)

        @pl.when(jnp.logical_and(i == 0, j == 0))
        def _():
            dg_ref[...] = jnp.zeros_like(dg_ref)

        for part in range(FFN_PARTS):
            rows = pl.ds(part * (tt // FFN_PARTS), tt // FFN_PARTS)
            dact = _dot_nt(dob_ref[rows, :], wd_ref[...]).astype(BF16)
            gt = gt_ref[rows, :]
            up = up_ref[rows, :]
            s = _sig(gt)
            sl = gt * s
            dup = dact * sl
            dgt = dact * up * (s + sl * (1.0 - s))
            dgt_ref[rows, :] = dgt
            dup_ref[rows, :] = dup
            act_ref[rows, :] = sl * up
            dh_sc[rows, :] += _dot(dgt, wg_ref[...]) + _dot(dup, wu_ref[...])

        @pl.when(j == NCHIP - 1)
        def _():
            dxn, dg = _rms_bwd(x_ref[...], g_ref[l:l + 1, :], dh_sc[...])
            dx_ref[...] = do_ref[...] + dxn
            dg_ref[...] += dg

    wspec = pl.BlockSpec((None, f, d), lambda i, j: (j, 0, 0))
    row = pl.BlockSpec((tt, d), lambda i, j: (i, 0))
    sh = pl.BlockSpec((None, tt, f), lambda i, j: (j, i, 0))
    ni = t // tt
    first = lambda: jnp.logical_and(pl.program_id(0) == 0, pl.program_id(1) == 0)
    last = lambda: jnp.logical_and(pl.program_id(0) == ni - 1, pl.program_id(1) == NCHIP - 1)
    outs, couts = _call(
        body, comm, first, last, name=name, grid=(ni, NCHIP),
        in_specs=[row, pl.BlockSpec(gains.shape, lambda i, j: (0, 0)), wspec, wspec, wspec, sh, sh, row],
        args=[x, gains, wg, wu, wd, gts, ups, dout],
        out_specs=[row, pl.BlockSpec((1, d), lambda i, j: (0, 0)), sh, sh, sh, row, row],
        out_shape=[SDS((t, d), F32), SDS((1, d), F32)] + [SDS((NCHIP, t, f), BF16)] * 3 + [SDS((t, d), BF16)] * 2,
        scratch=[pltpu.VMEM((tt, d), F32)], sem=("arbitrary", "arbitrary"))
    return (*outs, couts)


def _rms_fwd(x, gains, l, name):
    t, d = x.shape
    tt = _rt(t)

    def body(x_ref, g_ref, o_ref):
        xv = x_ref[...]
        o_ref[...] = (xv * _rms_r(xv) * g_ref[l:l + 1, :]).astype(BF16)

    row = pl.BlockSpec((tt, d), lambda i: (i, 0))
    return pl.pallas_call(body, name=name, grid=(t // tt,), in_specs=[row, pl.BlockSpec(gains.shape, lambda i: (0, 0))],
                          out_specs=row, out_shape=SDS((t, d), BF16), compiler_params=_cparams(("arbitrary",)))(x, gains)


def _rms_back(x, gains, l, dh, dres, name):
    t, d = x.shape
    tt = _rt(t)

    def body(x_ref, g_ref, dh_ref, dr_ref, dx_ref, dg_ref):
        @pl.when(pl.program_id(0) == 0)
        def _():
            dg_ref[...] = jnp.zeros_like(dg_ref)

        dxn, dg = _rms_bwd(x_ref[...], g_ref[l:l + 1, :], dh_ref[...])
        dx_ref[...] = dr_ref[...] + dxn
        dg_ref[...] += dg

    row = pl.BlockSpec((tt, d), lambda i: (i, 0))
    return pl.pallas_call(body, name=name, grid=(t // tt,),
                          in_specs=[row, pl.BlockSpec(gains.shape, lambda i: (0, 0)), row, row],
                          out_specs=[row, pl.BlockSpec((1, d), lambda i: (0, 0))],
                          out_shape=[SDS((t, d), F32), SDS((1, d), F32)],
                          compiler_params=_cparams(("arbitrary",)))(x, gains, dh, dres)


def _dwconv_fwd(proj, convw, convb, l, o_a, cc, name):
    t = proj.shape[0]
    r = min(256, t)
    nb = cc // HP

    def body(a_ref, gate_ref, w_ref, b_ref, uc_ref, u_sc):
        u_sc[0:CPAD, :] = jnp.zeros((CPAD, HP), F32)

        def fill(ci, carry):
            r0 = pl.multiple_of(ci * r, r)
            u_sc[pl.ds(CPAD + r0, r), :] = a_ref[pl.ds(r0, r), :].astype(F32) * _sig(gate_ref[pl.ds(r0, r), :].astype(F32))
            return carry

        lax.fori_loop(0, t // r, fill, 0)
        w = w_ref[l]
        bias = b_ref[l:l + 1, :]

        def conv(ci, carry):
            r0 = pl.multiple_of(ci * r, r)
            win = u_sc[pl.ds(r0, r + CPAD), :]
            acc = jnp.zeros((r, HP), F32) + bias
            for k in range(KW):
                off = CPAD - (KW - 1) + k
                acc = acc + win[off:off + r, :] * w[k:k + 1, :]
            uc_ref[pl.ds(r0, r), :] = acc
            return carry

        lax.fori_loop(0, t // r, conv, 0)

    col = lambda base: pl.BlockSpec((t, HP), lambda c: (0, base // HP + c))
    return pl.pallas_call(
        body, name=name, grid=(nb,),
        in_specs=[col(o_a), col(o_a + cc), pl.BlockSpec((convw.shape[0], CPAD, HP), lambda c: (0, 0, c)),
                  pl.BlockSpec((convb.shape[0], HP), lambda c: (0, c))],
        out_specs=pl.BlockSpec((t, HP), lambda c: (0, c)), out_shape=SDS((t, cc), F32),
        scratch_shapes=[pltpu.VMEM((t + CPAD, HP), F32)],
        compiler_params=_cparams(("arbitrary",), VMEM_BIG))(proj, proj, convw, convb)


def _dwconv_bwd(duc, proj, convw, l, o_a, cc, name):
    t = proj.shape[0]
    r = min(256, t)
    nb = cc // HP

    def body(d_ref, a_ref, gate_ref, w_ref, du_ref, dw_ref, u_sc, d_sc, dw_sc):
        u_sc[0:CPAD, :] = jnp.zeros((CPAD, HP), F32)
        d_sc[t:t + CPAD, :] = jnp.zeros((CPAD, HP), F32)
        dw_sc[...] = jnp.zeros_like(dw_sc)

        def fill(ci, carry):
            r0 = pl.multiple_of(ci * r, r)
            u_sc[pl.ds(CPAD + r0, r), :] = a_ref[pl.ds(r0, r), :].astype(F32) * _sig(gate_ref[pl.ds(r0, r), :].astype(F32))
            d_sc[pl.ds(r0, r), :] = d_ref[pl.ds(r0, r), :]
            return carry

        lax.fori_loop(0, t // r, fill, 0)
        w = w_ref[l]

        def conv(ci, carry):
            r0 = pl.multiple_of(ci * r, r)
            dwin = d_sc[pl.ds(r0, r + CPAD), :]
            uwin = u_sc[pl.ds(r0, r + CPAD), :]
            dcur = dwin[0:r, :]
            acc = jnp.zeros((r, HP), F32)
            for k in range(KW):
                acc = acc + dwin[KW - 1 - k:KW - 1 - k + r, :] * w[k:k + 1, :]
                off = CPAD - (KW - 1) + k
                part = (dcur * uwin[off:off + r, :]).reshape(r // 8, 8, HP).sum(axis=0)
                dw_sc[8 * k:8 * k + 8, :] += part
            du_ref[pl.ds(r0, r), :] = acc
            return carry

        lax.fori_loop(0, t // r, conv, 0)
        dw_ref[...] = jnp.zeros_like(dw_ref)
        for k in range(KW):
            dw_ref[k:k + 1, :] = jnp.sum(dw_sc[8 * k:8 * k + 8, :], axis=0, keepdims=True)

    col = lambda base: pl.BlockSpec((t, HP), lambda c: (0, base // HP + c))
    return pl.pallas_call(
        body, name=name, grid=(nb,),
        in_specs=[pl.BlockSpec((t, HP), lambda c: (0, c)), col(o_a), col(o_a + cc),
                  pl.BlockSpec((convw.shape[0], CPAD, HP), lambda c: (0, 0, c))],
        out_specs=[pl.BlockSpec((t, HP), lambda c: (0, c)), pl.BlockSpec((CPAD, HP), lambda c: (0, c))],
        out_shape=[SDS((t, cc), F32), SDS((CPAD, cc), F32)],
        scratch_shapes=[pltpu.VMEM((t + CPAD, HP), F32), pltpu.VMEM((t + CPAD, HP), F32), pltpu.VMEM((8 * CPAD, HP), F32)],
        compiler_params=_cparams(("arbitrary",), VMEM_BIG))(duc, proj, proj, convw)


def _ln_parts(uc, g, b):
    mu = jnp.mean(uc, axis=-1, keepdims=True)
    xc = uc - mu
    r = lax.rsqrt(jnp.mean(xc * xc, axis=-1, keepdims=True) + EPS)
    xh = xc * r
    return r, xh, xh * g + b


def _ln_silu(uc, ln_g, ln_b, l, name):
    t, cc = uc.shape
    tt = _rt(t)

    def body(u_ref, g_ref, b_ref, s_ref):
        _, _, yv = _ln_parts(u_ref[...], g_ref[l:l + 1, :], b_ref[l:l + 1, :])
        s_ref[...] = (yv * _sig(yv)).astype(BF16)

    row = pl.BlockSpec((tt, cc), lambda i: (i, 0))
    full = pl.BlockSpec(ln_g.shape, lambda i: (0, 0))
    return pl.pallas_call(body, name=name, grid=(t // tt,), in_specs=[row, full, full], out_specs=row,
                          out_shape=SDS((t, cc), BF16), compiler_params=_cparams(("arbitrary",)))(uc, ln_g, ln_b)


def _ln_silu_bwd(uc, ln_g, ln_b, l, ds, name):
    t, cc = uc.shape
    tt = _rt(t)

    def body(u_ref, g_ref, b_ref, ds_ref, du_ref, dg_ref, db_ref, dcb_ref):
        @pl.when(pl.program_id(0) == 0)
        def _():
            dg_ref[...] = jnp.zeros_like(dg_ref)
            db_ref[...] = jnp.zeros_like(db_ref)
            dcb_ref[...] = jnp.zeros_like(dcb_ref)

        g = g_ref[l:l + 1, :]
        r, xh, yv = _ln_parts(u_ref[...], g, b_ref[l:l + 1, :])
        sy = _sig(yv)
        dy = ds_ref[...] * (sy * (1.0 + yv * (1.0 - sy)))
        dg_ref[...] += jnp.sum(dy * xh, axis=0, keepdims=True)
        db_ref[...] += jnp.sum(dy, axis=0, keepdims=True)
        dxh = dy * g
        du = r * (dxh - jnp.mean(dxh, axis=-1, keepdims=True) - xh * jnp.mean(dxh * xh, axis=-1, keepdims=True))
        du_ref[...] = du
        dcb_ref[...] += jnp.sum(du, axis=0, keepdims=True)

    row = pl.BlockSpec((tt, cc), lambda i: (i, 0))
    full = pl.BlockSpec(ln_g.shape, lambda i: (0, 0))
    vec = pl.BlockSpec((1, cc), lambda i: (0, 0))
    return pl.pallas_call(body, name=name, grid=(t // tt,), in_specs=[row, full, full, row],
                          out_specs=[row, vec, vec, vec], out_shape=[SDS((t, cc), F32)] + [SDS((1, cc), F32)] * 3,
                          compiler_params=_cparams(("arbitrary",)))(uc, ln_g, ln_b, ds)


def _glu_bwd(du, proj, dproj, o_a, cc, name):
    t = du.shape[0]
    tt = _rt(t)

    def body(du_ref, a_ref, gate_ref, prev_ref, o_ref):
        sg = _sig(gate_ref[...].astype(F32))
        dv = du_ref[...]
        o_ref[:, 0:cc] = (dv * sg).astype(BF16)
        o_ref[:, cc:2 * cc] = (dv * a_ref[...] * sg * (1.0 - sg)).astype(BF16)

    return pl.pallas_call(
        body, name=name, grid=(t // tt,),
        in_specs=[pl.BlockSpec((tt, cc), lambda i: (i, 0)), pl.BlockSpec((tt, cc), lambda i: (i, o_a // cc)),
                  pl.BlockSpec((tt, cc), lambda i: (i, o_a // cc + 1)), ANY],
        out_specs=pl.BlockSpec((tt, 2 * cc), lambda i: (i, o_a // (2 * cc))),
        out_shape=SDS(dproj.shape, dproj.dtype), input_output_aliases={3: 0},
        compiler_params=_cparams(("arbitrary",)))(du, proj, proj, dproj)


def _rope(v, cs, s1, s2):
    return v * cs + pltpu.roll(v, HP - ROPE // 2, 1) * s1 + pltpu.roll(v, ROPE // 2, 1) * s2


def _rope_t(dv, cs, s1, s2):
    return dv * cs + pltpu.roll(dv * s1, ROPE // 2, 1) + pltpu.roll(dv * s2, HP - ROPE // 2, 1)


def _head_norm(v, g):
    r = lax.rsqrt(jnp.sum(v * v, axis=-1, keepdims=True) * (1.0 / QK) + EPS)
    return v * r * g, r


def _head_norm_bwd(v, r, g, dy):
    xh = v * r
    dg = jnp.sum(dy * xh, axis=0, keepdims=True)
    dxh = dy * g
    dx = r * (dxh - xh * (jnp.sum(dxh * xh, axis=-1, keepdims=True) * (1.0 / QK)))
    return dx, dg


def _mla_specs(tt, lay, cq_norm, ckv_norm, qn, kn, wuq, wukv):
    ql, kvl = cq_norm.shape[1], ckv_norm.shape[1]
    full = lambda a: pl.BlockSpec(a.shape, lambda i: (0,) * a.ndim)
    tab = pl.BlockSpec((tt, HP), lambda i: (i, 0))
    return [pl.BlockSpec((tt, ql), lambda i: (i, lay["cq"] // ql)),
            pl.BlockSpec((tt, kvl), lambda i: (i, lay["ckv"] // kvl)),
            pl.BlockSpec((tt, HP), lambda i: (i, lay["kr"] // HP)),
            full(cq_norm), full(ckv_norm), full(qn), full(kn), full(wuq), full(wukv), tab, tab, tab]


def _mla_pre_fwd(proj, lay, l, cq_norm, ckv_norm, qn, kn, wuq, wukv, tabs, name):
    t = proj.shape[0]
    tt = _rt(t)
    hw = N_HEADS * HP

    def body(cq_ref, ckv_ref, kr_ref, gq_ref, gkv_ref, qn_ref, kn_ref, wq_ref, wkv_ref, c_ref, s1_ref, s2_ref,
             q_ref, k_ref, v_ref):
        cq = cq_ref[...].astype(F32)
        cqn = (cq * _rms_r(cq) * gq_ref[l:l + 1, :]).astype(BF16)
        ckv = ckv_ref[...].astype(F32)
        ckvn = (ckv * _rms_r(ckv) * gkv_ref[l:l + 1, :]).astype(BF16)
        qraw = _dot_nt(cqn, wq_ref[...])
        kv = _dot(ckvn, wkv_ref[...])
        v_ref[...] = kv.astype(BF16)
        lane = lax.broadcasted_iota(jnp.int32, (tt, HP), 1)
        krs = pltpu.roll(jnp.where(lane < ROPE, kr_ref[...].astype(F32), 0.0), NOPE, 1)
        cs, s1, s2 = c_ref[...], s1_ref[...], s2_ref[...]
        gq, gk = qn_ref[l:l + 1, :], kn_ref[l:l + 1, :]
        for h in range(N_HEADS):
            sl = slice(h * HP, (h + 1) * HP)
            qh, _ = _head_norm(qraw[:, sl], gq)
            q_ref[:, sl] = (_rope(qh, cs, s1, s2) * QK ** -0.5).astype(BF16)
            kh, _ = _head_norm(jnp.where(lane < NOPE, kv[:, sl], krs), gk)
            k_ref[:, sl] = _rope(kh, cs, s1, s2).astype(BF16)

    row = pl.BlockSpec((tt, hw), lambda i: (i, 0))
    return pl.pallas_call(
        body, name=name, grid=(t // tt,),
        in_specs=_mla_specs(tt, lay, cq_norm, ckv_norm, qn, kn, wuq, wukv),
        out_specs=[row, row, row], out_shape=[SDS((t, hw), BF16)] * 3,
        compiler_params=_cparams(("arbitrary",), VMEM_BIG))(
            proj, proj, proj, cq_norm, ckv_norm, qn, kn, wuq, wukv, *tabs)


def _mla_pre_bwd(proj, lay, l, cq_norm, ckv_norm, qn, kn, wuq, wukv, tabs, dq, dk, dv, dproj, name):
    t = proj.shape[0]
    tt = _rt(t)
    hw = N_HEADS * HP
    ql, kvl = cq_norm.shape[1], ckv_norm.shape[1]
    wm = lay["wm"]

    def body(cq_ref, ckv_ref, kr_ref, gq_ref, gkv_ref, qn_ref, kn_ref, wq_ref, wkv_ref, c_ref, s1_ref, s2_ref,
             dq_ref, dk_ref, dv_ref, prev_ref,
             o_ref, dqr_ref, dkv_ref, cqn_ref, ckvn_ref, dgq_ref, dgkv_ref, dqn_ref, dkn_ref):
        @pl.when(pl.program_id(0) == 0)
        def _():
            for ref in (dgq_ref, dgkv_ref, dqn_ref, dkn_ref):
                ref[...] = jnp.zeros_like(ref)

        cq = cq_ref[...].astype(F32)
        cqn = (cq * _rms_r(cq) * gq_ref[l:l + 1, :]).astype(BF16)
        ckv = ckv_ref[...].astype(F32)
        ckvn = (ckv * _rms_r(ckv) * gkv_ref[l:l + 1, :]).astype(BF16)
        cqn_ref[...] = cqn
        ckvn_ref[...] = ckvn
        qraw = _dot_nt(cqn, wq_ref[...])
        kv = _dot(ckvn, wkv_ref[...])
        lane = lax.broadcasted_iota(jnp.int32, (tt, HP), 1)
        krs = pltpu.roll(jnp.where(lane < ROPE, kr_ref[...].astype(F32), 0.0), NOPE, 1)
        cs, s1, s2 = c_ref[...], s1_ref[...], s2_ref[...]
        gq, gk = qn_ref[l:l + 1, :], kn_ref[l:l + 1, :]
        dkr = jnp.zeros((tt, HP), F32)
        dgq = jnp.zeros((1, HP), F32)
        dgk = jnp.zeros((1, HP), F32)
        for h in range(N_HEADS):
            sl = slice(h * HP, (h + 1) * HP)
            qh = qraw[:, sl]
            _, rq = _head_norm(qh, gq)
            dqh, dg = _head_norm_bwd(qh, rq, gq, _rope_t(dq_ref[:, sl] * QK ** -0.5, cs, s1, s2))
            dgq = dgq + dg
            dqr_ref[:, sl] = dqh.astype(BF16)
            kp = jnp.where(lane < NOPE, kv[:, sl], krs)
            _, rk = _head_norm(kp, gk)
            dkp, dg = _head_norm_bwd(kp, rk, gk, _rope_t(dk_ref[:, sl], cs, s1, s2))
            dgk = dgk + dg
            dkv_ref[:, sl] = (jnp.where(lane < NOPE, dkp, 0.0) + dv_ref[:, sl]).astype(BF16)
            dkr = dkr + dkp
        dqn_ref[...] += dgq
        dkn_ref[...] += dgk
        dkr = jnp.where(lane < ROPE, pltpu.roll(dkr, HP - NOPE, 1), 0.0)
        dcq, dg = _rms_bwd(cq, gq_ref[l:l + 1, :], _dot(dqr_ref[...], wq_ref[...]))
        dgq_ref[...] += dg
        dckv, dg = _rms_bwd(ckv, gkv_ref[l:l + 1, :], _dot_nt(dkv_ref[...], wkv_ref[...]))
        dgkv_ref[...] += dg
        o_ref[:, 0:ql] = dcq.astype(BF16)
        o_ref[:, ql:ql + kvl] = dckv.astype(BF16)
        o_ref[:, ql + kvl:ql + kvl + HP] = dkr.astype(BF16)
        o_ref[:, ql + kvl + HP:wm] = jnp.zeros((tt, wm - ql - kvl - HP), BF16)

    row = pl.BlockSpec((tt, hw), lambda i: (i, 0))
    vec = lambda n: pl.BlockSpec((1, n), lambda i: (0, 0))
    return pl.pallas_call(
        body, name=name, grid=(t // tt,),
        in_specs=_mla_specs(tt, lay, cq_norm, ckv_norm, qn, kn, wuq, wukv) + [row, row, row, ANY],
        out_specs=[pl.BlockSpec((tt, wm), lambda i: (i, lay["cq"] // wm)), row, row,
                   pl.BlockSpec((tt, ql), lambda i: (i, 0)), pl.BlockSpec((tt, kvl), lambda i: (i, 0)),
                   vec(ql), vec(kvl), vec(HP), vec(HP)],
        out_shape=[SDS(dproj.shape, dproj.dtype), SDS((t, hw), BF16), SDS((t, hw), BF16), SDS((t, ql), BF16),
                   SDS((t, kvl), BF16), SDS((1, ql), F32), SDS((1, kvl), F32), SDS((1, HP), F32), SDS((1, HP), F32)],
        input_output_aliases={15: 0},
        compiler_params=_cparams(("arbitrary",), VMEM_BIG))(
            proj, proj, proj, cq_norm, ckv_norm, qn, kn, wuq, wukv, *tabs, dq, dk, dv, dproj)


def _comb_fwd(proj, gate_bias, l, yc, ym, name):
    t, d = yc.shape
    tt = _rt(t)

    def body(p_ref, b_ref, yc_ref, ym_ref, y_ref):
        b = b_ref[l]
        g0 = _sig(p_ref[:, 0:d] + b[0:1, :])
        g1 = _sig(p_ref[:, d:2 * d] + b[1:2, :])
        y_ref[...] = (g0 * yc_ref[...] + g1 * ym_ref[...]).astype(BF16)

    row = pl.BlockSpec((tt, d), lambda i: (i, 0))
    return pl.pallas_call(
        body, name=name, grid=(t // tt,),
        in_specs=[pl.BlockSpec((tt, 2 * d), lambda i: (i, 0)), pl.BlockSpec(gate_bias.shape, lambda i: (0, 0, 0)), row, row],
        out_specs=row, out_shape=SDS((t, d), BF16), compiler_params=_cparams(("arbitrary",)))(proj, gate_bias, yc, ym)


def _comb_bwd(proj, gate_bias, l, yc, ym, dy, dp_cols, name):
    t, d = yc.shape
    tt = _rt(t)

    def body(p_ref, b_ref, yc_ref, ym_ref, dy_ref, dyc_ref, dym_ref, dp_ref, db_ref):
        @pl.when(pl.program_id(0) == 0)
        def _():
            db_ref[...] = jnp.zeros_like(db_ref)

        b = b_ref[l]
        dyv = dy_ref[...].astype(F32)
        g0 = _sig(p_ref[:, 0:d] + b[0:1, :])
        g1 = _sig(p_ref[:, d:2 * d] + b[1:2, :])
        dyc_ref[...] = (dyv * g0).astype(BF16)
        dym_ref[...] = (dyv * g1).astype(BF16)
        dg0 = dyv * yc_ref[...] * g0 * (1.0 - g0)
        dg1 = dyv * ym_ref[...] * g1 * (1.0 - g1)
        dp_ref[:, 0:d] = dg0.astype(BF16)
        dp_ref[:, d:2 * d] = dg1.astype(BF16)
        db_ref[0:1, :] += jnp.sum(dg0, axis=0, keepdims=True)
        db_ref[1:2, :] += jnp.sum(dg1, axis=0, keepdims=True)

    row = pl.BlockSpec((tt, d), lambda i: (i, 0))
    wide = pl.BlockSpec((tt, 2 * d), lambda i: (i, 0))
    return pl.pallas_call(
        body, name=name, grid=(t // tt,),
        in_specs=[wide, pl.BlockSpec(gate_bias.shape, lambda i: (0, 0, 0)), row, row, row],
        out_specs=[row, row, wide, pl.BlockSpec((2, d), lambda i: (0, 0))],
        out_shape=[SDS((t, d), BF16), SDS((t, d), BF16), SDS((t, dp_cols), BF16), SDS((2, d), F32)],
        compiler_params=_cparams(("arbitrary",)))(proj, gate_bias, yc, ym, dy)


def _loss_grad(y, target, name):
    t, d = y.shape
    tt = _rt(t)
    nt = t // tt

    def body(y_ref, t_ref, dy_ref, loss_ref, acc_sc):
        i = pl.program_id(0)

        @pl.when(i == 0)
        def _():
            acc_sc[...] = jnp.zeros_like(acc_sc)

        diff = y_ref[...] - t_ref[...]
        dy_ref[...] = diff * (1.0 / d)
        acc_sc[...] += jnp.sum(diff * diff, axis=0, keepdims=True)

        @pl.when(i == nt - 1)
        def _():
            tot = jnp.sum(acc_sc[...], axis=1, keepdims=True) * (0.5 / d)
            loss_ref[...] = jnp.broadcast_to(tot, (1, HP))

    row = pl.BlockSpec((tt, d), lambda i: (i, 0))
    return pl.pallas_call(body, name=name, grid=(nt,), in_specs=[row, row],
                          out_specs=[row, pl.BlockSpec((1, HP), lambda i: (0, 0))],
                          out_shape=[SDS((t, d), F32), SDS((1, HP), F32)],
                          scratch_shapes=[pltpu.VMEM((1, d), F32)],
                          compiler_params=_cparams(("arbitrary",)))(y, target)


def _chunk_mask(tq):
    rows = lax.broadcasted_iota(jnp.int32, (tq, tq), 0) // CHUNK
    cols = lax.broadcasted_iota(jnp.int32, (tq, tq), 1) // CHUNK
    return cols <= rows


NEG = -1e30


def _flash_fwd(q, k, v, name, comm=None):
    t = q.shape[0]
    tq = _rt(t)
    nq = t // tq
    rep = tq // HP

    def body(q_ref, k_ref, v_ref, o_ref, lse_ref):
        qi = pl.program_id(1)
        qv = q_ref[...]

        def step(ki, carry, masked):
            m_prev, l_prev, acc = carry
            r0 = pl.multiple_of(ki * tq, tq)
            s = _dot_nt(qv, k_ref[pl.ds(r0, tq), :])
            if masked:
                s = jnp.where(_chunk_mask(tq), s, NEG)
            m_new = jnp.maximum(m_prev, jnp.max(s, axis=-1, keepdims=True))
            a = jnp.exp(m_prev - m_new)
            p = jnp.exp(s - jnp.tile(m_new, (1, rep)))
            l_new = a * l_prev + jnp.sum(p, axis=-1, keepdims=True)
            acc = a * acc + _dot(p.astype(BF16), v_ref[pl.ds(r0, tq), :])
            return m_new, l_new, acc

        init = (jnp.full((tq, HP), NEG, F32), jnp.zeros((tq, HP), F32), jnp.zeros((tq, HP), F32))
        carry = lax.fori_loop(0, qi, lambda ki, cr: step(ki, cr, False), init)
        m_fin, l_fin, acc = step(qi, carry, True)
        o_ref[...] = (acc / l_fin).astype(BF16)
        lse_ref[...] = m_fin + jnp.log(l_fin)

    qspec = pl.BlockSpec((tq, HP), lambda h, qi: (qi, h))
    head = pl.BlockSpec((t, HP), lambda h, qi: (0, h))
    first = lambda: jnp.logical_and(pl.program_id(0) == 0, pl.program_id(1) == 0)
    last = lambda: jnp.logical_and(pl.program_id(0) == N_HEADS - 1, pl.program_id(1) == nq - 1)
    outs, couts = _call(
        body, comm, first, last, name=name, grid=(N_HEADS, nq), in_specs=[qspec, head, head], args=[q, k, v],
        out_specs=[qspec, pl.BlockSpec((None, tq, HP), lambda h, qi: (h, qi, 0))],
        out_shape=[SDS(q.shape, BF16), SDS((N_HEADS, t, HP), F32)], scratch=[], sem=("arbitrary",) * 2)
    return (*outs, couts)


def _flash_bwd(q, k, v, do, o, lse, name, comm=None):
    t = q.shape[0]
    tq = _rt(t)
    nq = t // tq
    rep = tq // HP

    def body(q_ref, k_ref, v_ref, do_ref, o_ref, lse_ref, dq_ref, dk_ref, dv_ref, delta_sc):
        def prep(qi, carry):
            r0 = pl.multiple_of(qi * tq, tq)
            rows = pl.ds(r0, tq)
            dlt = jnp.sum(do_ref[rows, :].astype(F32) * o_ref[rows, :].astype(F32), axis=-1, keepdims=True)
            delta_sc[rows, :] = jnp.broadcast_to(dlt, (tq, HP))
            dq_ref[rows, :] = jnp.zeros((tq, HP), F32)
            return carry

        lax.fori_loop(0, nq, prep, 0)

        def keys(ki, carry0):
            krows = pl.ds(pl.multiple_of(ki * tq, tq), tq)
            kt, vt = k_ref[krows, :], v_ref[krows, :]

            def step(qi, carry, masked):
                dk_acc, dv_acc = carry
                rows = pl.ds(pl.multiple_of(qi * tq, tq), tq)
                qt, dot_ = q_ref[rows, :], do_ref[rows, :]
                s = _dot_nt(qt, kt)
                if masked:
                    s = jnp.where(_chunk_mask(tq), s, NEG)
                p = jnp.exp(s - jnp.tile(lse_ref[rows, :], (1, rep)))
                ds = (p * (_dot_nt(dot_, vt) - jnp.tile(delta_sc[rows, :], (1, rep)))).astype(BF16)
                dv_acc = dv_acc + _dot_tn(p.astype(BF16), dot_)
                dk_acc = dk_acc + _dot_tn(ds, qt)
                dq_ref[rows, :] += _dot(ds, kt)
                return dk_acc, dv_acc

            zero = jnp.zeros((tq, HP), F32)
            carry = step(ki, (zero, zero), True)
            dk_acc, dv_acc = lax.fori_loop(ki + 1, nq, lambda qi, cr: step(qi, cr, False), carry)
            dk_ref[krows, :] = dk_acc
            dv_ref[krows, :] = dv_acc
            return carry0

        lax.fori_loop(0, nq, keys, 0)

    head = pl.BlockSpec((t, HP), lambda h: (0, h))
    outs, couts = _call(
        body, comm, lambda: pl.program_id(0) == 0, lambda: pl.program_id(0) == N_HEADS - 1, name=name, grid=(N_HEADS,),
        in_specs=[head] * 5 + [pl.BlockSpec((None, t, HP), lambda h: (h, 0, 0))], args=[q, k, v, do, o, lse],
        out_specs=[head] * 3, out_shape=[SDS(q.shape, F32)] * 3, scratch=[pltpu.VMEM((t, HP), F32)], sem=("arbitrary",))
    return (*outs, couts)


def _add_cast(gs, rs, place, name):
    n = len(gs)
    _, r, c = gs[0].shape
    tr, tc = _tile2(r, c // 2, 3 * n, 16)
    nct = c // 2 // tc

    def body(place_ref, *refs):
        for a in range(n):
            refs[2 * n + a][...] = (refs[a][...] + refs[n + a][...]).astype(BF16)

    gspec = pl.BlockSpec((None, tr, tc), lambda j, i, k, pr: (j, i, pr[0] * nct + k))
    rspec = pl.BlockSpec((None, tr, tc), lambda j, i, k, pr: (j, i, k))
    grid_spec = pltpu.PrefetchScalarGridSpec(num_scalar_prefetch=1, grid=(NCHIP, r // tr, nct),
                                             in_specs=[gspec] * n + [rspec] * n, out_specs=[rspec] * n)
    return pl.pallas_call(body, name=name, grid_spec=grid_spec, out_shape=[SDS((NCHIP, r, c // 2), BF16)] * n,
                          compiler_params=_cparams(("arbitrary",) * 3, VMEM_BIG))(place, *gs, *rs)


def _sum_chips(ss, qs, place, l, nl, prevs, name):
    n = len(ss)
    _, r, h = ss[0].shape
    tr, tc = _tile2(r, h, 3 * n, 16)
    nct = h // tc

    def body(place_ref, *refs):
        for a in range(n):
            acc = refs[n + a][...].astype(F32)
            for kk in range(NCHIP - 1):
                acc = acc + refs[a][kk].astype(F32)
            refs[-n + a][...] = acc

    in_specs = ([pl.BlockSpec((NCHIP - 1, tr, tc), lambda i, k, pr: (0, i, k))] * n
                + [pl.BlockSpec((None, tr, tc), lambda i, k, pr: (pr[1], i, k))] * n)
    args = [*ss, *qs]
    aliases = {}
    if prevs is not None:
        aliases = {1 + len(args) + a: a for a in range(n)}
        in_specs += [ANY] * n
        args += list(prevs)
    grid_spec = pltpu.PrefetchScalarGridSpec(
        num_scalar_prefetch=1, grid=(r // tr, nct), in_specs=in_specs,
        out_specs=[pl.BlockSpec((None, tr, tc), lambda i, k, pr: (l, i, pr[0] * nct + k))] * n)
    return pl.pallas_call(body, name=name, grid_spec=grid_spec, out_shape=[SDS((nl, r, 2 * h), F32)] * n,
                          input_output_aliases=aliases,
                          compiler_params=_cparams(("arbitrary",) * 2, VMEM_BIG))(place, *args)


def _cast_place(ws, place, name):
    n = len(ws)
    nl, r, c = ws[0].shape
    tr, tc = _tile2(r, c, 2 * n * nl, 16)

    def body(place_ref, *refs):
        for a in range(n * nl):
            refs[n * nl + a][...] = refs[a][...].astype(BF16)

    in_specs = [pl.BlockSpec((None, tr, tc), functools.partial(lambda l, i, k, pr: (l, i, k), l))
                for _ in range(n) for l in range(nl)]
    grid_spec = pltpu.PrefetchScalarGridSpec(
        num_scalar_prefetch=1, grid=(r // tr, c // tc), in_specs=in_specs,
        out_specs=[pl.BlockSpec((None, tr, tc), lambda i, k, pr: (pr[1], i, k))] * (n * nl))
    outs = pl.pallas_call(body, name=name, grid_spec=grid_spec, out_shape=[SDS((NCHIP, r, c), BF16)] * (n * nl),
                          compiler_params=_cparams(("arbitrary",) * 2, VMEM_BIG))(
                              place, *[w for w in ws for _ in range(nl)])
    return [outs[a * nl:(a + 1) * nl] for a in range(n)]


def _adamw(ws, gs, ms, vs, name):
    n = len(ws)
    r, c = ws[0].shape
    tr, tc = _tile2(r, c, 7 * n, 8)
    c1, c2 = 1.0 / (1.0 - B1 ** STEP), 1.0 / (1.0 - B2 ** STEP)

    def body(*refs):
        for a in range(n):
            w, g, m, v = (refs[kk * n + a][...] for kk in range(4))
            m2 = B1 * m + (1.0 - B1) * g
            v2 = B2 * v + (1.0 - B2) * (g * g)
            refs[4 * n + a][...] = -LR * ((m2 * c1) / (jnp.sqrt(v2 * c2) + EPS_ADAM) + WD * w)
            refs[5 * n + a][...] = m2
            refs[6 * n + a][...] = v2

    blk = pl.BlockSpec((tr, tc), lambda i, k: (i, k))
    outs = pl.pallas_call(body, name=name, grid=(r // tr, c // tc), in_specs=[blk] * (4 * n),
                          out_specs=[blk] * (3 * n), out_shape=[SDS((r, c), F32)] * (3 * n),
                          compiler_params=_cparams(("arbitrary",) * 2, VMEM_BIG))(*ws, *gs, *ms, *vs)
    return outs[:n], outs[n:2 * n], outs[2 * n:]


def _place():
    x, y, c = lax.axis_index("x"), lax.axis_index("y"), lax.axis_index("c")
    return x, y, c, [(1 - x, y), (x, 1 - y), (1 - x, 1 - y)]


def _rcopy(src, dst, ssem, rsem, k, dev):
    return pltpu.make_async_remote_copy(src_ref=src, dst_ref=dst, send_sem=ssem.at[k], recv_sem=rsem.at[k],
                                        device_id=dev, device_id_type=MESH)


def _half(ref, lead, cc):
    h = ref.shape[-1] // 2
    return ref.at[(*lead, slice(None), pl.ds(cc * h, h))]


def _per_core(fn):
    c = lax.axis_index("c")
    for cc in (0, 1):
        pl.when(c == cc)(functools.partial(fn, cc))


def _gather_comm(bufs):
    n = len(bufs)

    def plan(couts, ssem, rsem, cc):
        x, y, _, peers = _place()
        me, sib = 2 * x + y, (x, y, 1 - cc)
        send, recv, fwd, recv2 = [], [], [], []
        for a in range(n):
            for kk, (px, py) in enumerate(peers):
                mine = _half(couts[a], (me,), cc)
                got = _half(couts[a], (2 * px + py,), cc)
                other = _half(couts[a], (2 * px + py,), 1 - cc)
                send.append(_rcopy(mine, mine, ssem, rsem, a * 6 + kk, (px, py, cc)))
                recv.append(_rcopy(got, got, ssem, rsem, a * 6 + kk, (px, py, cc)))
                fwd.append(_rcopy(got, got, ssem, rsem, a * 6 + 3 + kk, sib))
                recv2.append(_rcopy(other, other, ssem, rsem, a * 6 + 3 + kk, sib))
        return send, recv, fwd, recv2

    def start(cins, couts, ssem, rsem):
        def go(cc):
            for d in plan(couts, ssem, rsem, cc)[0]:
                d.start()

        _per_core(go)

    def finish(cins, couts, ssem, rsem):
        def go(cc):
            send, recv, fwd, recv2 = plan(couts, ssem, rsem, cc)
            for dr, df in zip(recv, fwd):
                dr.wait_recv()
                df.start()
            for d in recv2:
                d.wait_recv()
            for d in send + fwd:
                d.wait_send()

        _per_core(go)

    return _Comm(bufs, [SDS(b.shape, b.dtype) for b in bufs], {a: a for a in range(n)}, 6 * n, start, finish)


def _pair_comm(gs):
    n = len(gs)
    halves = [g.shape[-1] // 2 for g in gs]

    def plan(cins, couts, ssem, rsem, cc):
        x, y, _, _ = _place()
        return [_rcopy(cins[a].at[:, :, pl.ds((1 - cc) * halves[a], halves[a])], couts[a], ssem, rsem, a, (x, y, 1 - cc))
                for a in range(n)]

    def start(cins, couts, ssem, rsem):
        def go(cc):
            for d in plan(cins, couts, ssem, rsem, cc):
                d.start()

        _per_core(go)

    def finish(cins, couts, ssem, rsem):
        def go(cc):
            ds = plan(cins, couts, ssem, rsem, cc)
            for d in ds:
                d.wait_recv()
            for d in ds:
                d.wait_send()

        _per_core(go)

    return _Comm(gs, [SDS(g.shape[:-1] + (g.shape[-1] // 2,), g.dtype) for g in gs], {}, n, start, finish)


def _chips_comm(qs):
    n = len(qs)

    def plan(cins, couts, ssem, rsem):
        x, y, c, peers = _place()
        return [_rcopy(cins[a].at[2 * px + py], couts[a].at[kk], ssem, rsem, a * 3 + kk, (px, py, c))
                for a in range(n) for kk, (px, py) in enumerate(peers)]

    def start(cins, couts, ssem, rsem):
        for d in plan(cins, couts, ssem, rsem):
            d.start()

    def finish(cins, couts, ssem, rsem):
        ds = plan(cins, couts, ssem, rsem)
        for d in ds:
            d.wait_recv()
        for d in ds:
            d.wait_send()

    return _Comm(qs, [SDS((NCHIP - 1,) + q.shape[1:], q.dtype) for q in qs], {}, 3 * n, start, finish)


def _share_comm(fs, l):
    n = len(fs)

    def plan(couts, ssem, rsem, cc, which):
        x, y, _, _ = _place()
        out = []
        for a in range(n):
            piece = _half(couts[a], (l,), which)
            out.append(_rcopy(piece, piece, ssem, rsem, a, (x, y, 1 - cc)))
        return out

    def start(cins, couts, ssem, rsem):
        def go(cc):
            for d in plan(couts, ssem, rsem, cc, cc):
                d.start()

        _per_core(go)

    def finish(cins, couts, ssem, rsem):
        def go(cc):
            for d in plan(couts, ssem, rsem, cc, 1 - cc):
                d.wait_recv()
            for d in plan(couts, ssem, rsem, cc, cc):
                d.wait_send()

        _per_core(go)

    return _Comm(fs, [SDS(f.shape, f.dtype) for f in fs], {a: a for a in range(n)}, n, start, finish)


def _exchange(buf, reduce, name):
    r, w = buf.shape

    def body(in_ref, out_ref, recv_sc, ssem, rsem):
        x, y, c, _ = _place()
        me = 4 * x + 2 * y + c
        sends = []
        for rel in range(1, N_DEV):
            dev = (1 - x if rel & 4 else x, 1 - y if rel & 2 else y, 1 - c if rel & 1 else c)
            d = _rcopy(in_ref, recv_sc.at[me], ssem, rsem, rel - 1, dev)
            d.start()
            sends.append(d)
        recv_sc[me] = in_ref[...]
        for rel in range(1, N_DEV):
            px, py, pc = (1 - x if rel & 4 else x, 1 - y if rel & 2 else y, 1 - c if rel & 1 else c)
            slot = recv_sc.at[4 * px + 2 * py + pc]
            _rcopy(slot, slot, ssem, rsem, rel - 1, (px, py, pc)).wait_recv()
        for d in sends:
            d.wait_send()
        if reduce:
            acc = recv_sc[0]
            for dv in range(1, N_DEV):
                acc = acc + recv_sc[dv]
            out_ref[...] = acc
        else:
            out_ref[...] = recv_sc[...]

    vm = pl.BlockSpec(memory_space=pltpu.VMEM)
    return pl.pallas_call(
        body, name=name, in_specs=[vm], out_specs=vm,
        out_shape=SDS((r, w) if reduce else (N_DEV, r, w), F32),
        scratch_shapes=[pltpu.VMEM((N_DEV, r, w), F32), pltpu.SemaphoreType.DMA((N_DEV - 1,)),
                        pltpu.SemaphoreType.DMA((N_DEV - 1,))],
        compiler_params=_cparams(None, VMEM_BIG))(buf)


def _cols_full(g):
    _, k, ns = g.shape
    return g.transpose(1, 0, 2).reshape(k, NCHIP * ns)


def _cols_shards(w):
    k, n = w.shape
    return w.reshape(k, NCHIP, n // NCHIP).transpose(1, 0, 2)


def _rows_pad(rows, width):
    out = jnp.concatenate([jnp.pad(a, ((0, 0), (0, width - a.shape[1]))) for a in rows], axis=0)
    return jnp.pad(out, ((0, -out.shape[0] % 8), (0, 0)))


def kernel(x, positions, ffn1_norm, ffn1_w_gate, ffn1_w_up, ffn1_w_down, mix_norm, w_in, gate_bias, conv_w, conv_b, conv_ln_g, conv_ln_b, w_conv_out, cq_norm, ckv_norm, w_uq, w_ukv, q_norm, k_norm, w_mla_out, w_out, ffn2_norm, ffn2_w_gate, ffn2_w_up, ffn2_w_down, loss_target, m_ffn1_norm, m_ffn1_w_gate, m_ffn1_w_up, m_ffn1_w_down, m_mix_norm, m_w_in, m_gate_bias, m_conv_w, m_conv_b, m_conv_ln_g, m_conv_ln_b, m_w_conv_out, m_cq_norm, m_ckv_norm, m_w_uq, m_w_ukv, m_q_norm, m_k_norm, m_w_mla_out, m_w_out, m_ffn2_norm, m_ffn2_w_gate, m_ffn2_w_up, m_ffn2_w_down, v_ffn1_norm, v_ffn1_w_gate, v_ffn1_w_up, v_ffn1_w_down, v_mix_norm, v_w_in, v_gate_bias, v_conv_w, v_conv_b, v_conv_ln_g, v_conv_ln_b, v_w_conv_out, v_cq_norm, v_ckv_norm, v_w_uq, v_w_ukv, v_q_norm, v_k_norm, v_w_mla_out, v_w_out, v_ffn2_norm, v_ffn2_w_gate, v_ffn2_w_up, v_ffn2_w_down):
    names = ["ffn1_norm", "ffn1_w_gate", "ffn1_w_up", "ffn1_w_down", "mix_norm", "w_in", "gate_bias", "conv_w",
             "conv_b", "conv_ln_g", "conv_ln_b", "w_conv_out", "cq_norm", "ckv_norm", "w_uq", "w_ukv", "q_norm",
             "k_norm", "w_mla_out", "w_out", "ffn2_norm", "ffn2_w_gate", "ffn2_w_up", "ffn2_w_down"]
    env = dict(locals())
    turned = ("ffn1_w_gate", "ffn1_w_up", "ffn2_w_gate", "ffn2_w_up", "w_in", "w_uq")
    view = lambda nm, a: jnp.swapaxes(a, 1, 2) if nm in turned else a
    wts = {nm: view(nm, env[nm]) for nm in names}
    mom = {nm: view(nm, env["m_" + nm]) for nm in names}
    var = {nm: view(nm, env["v_" + nm]) for nm in names}

    t, d = x.shape[1], x.shape[2]
    nl = ffn1_norm.shape[0]
    cc = conv_b.shape[1]
    ql, kvl = cq_norm.shape[1], ckv_norm.shape[1]
    vw = N_HEADS * V_DIM
    hw = N_HEADS * HP
    lay = {"a": 2 * d, "cq": 2 * d + 2 * cc, "ckv": 2 * d + 2 * cc + ql, "kr": 2 * d + 2 * cc + ql + kvl,
           "wm": ql + kvl + 2 * HP}
    dp = lay["cq"] + lay["wm"]
    nat_g = 2 * cc + ql + kvl + ROPE
    assert lay["cq"] % lay["wm"] == 0 and lay["cq"] % ql == 0 and lay["ckv"] % kvl == 0 and lay["a"] % (2 * cc) == 0
    assert cc % HP == 0 and d % HP == 0 and t % (2 * CHUNK) == 0 and w_in.shape[2] * NCHIP == nat_g + 2 * d
    assert nl == 2

    x0, target = x[0], loss_target[0]
    chip = 2 * lax.axis_index("x") + lax.axis_index("y")
    place = jnp.stack([lax.axis_index("c"), chip]).astype(jnp.int32)

    inv_freq = ROPE_THETA ** (-jnp.arange(0, ROPE, 2, dtype=F32) / ROPE)
    ang = positions[0].astype(F32)[:, None] * inv_freq
    cos, sin, z = jnp.cos(ang), jnp.sin(ang), jnp.zeros((t, ROPE // 2), F32)
    tabs = (jnp.concatenate([jnp.ones((t, NOPE), F32), cos, cos, jnp.zeros((t, HP - QK), F32)], axis=1),
            jnp.concatenate([jnp.zeros((t, NOPE), F32), -sin, z, jnp.zeros((t, HP - QK), F32)], axis=1),
            jnp.concatenate([jnp.zeros((t, NOPE), F32), z, sin, jnp.zeros((t, HP - QK), F32)], axis=1))

    big = ["ffn1_w_gate", "ffn1_w_up", "ffn1_w_down", "ffn2_w_gate", "ffn2_w_up", "ffn2_w_down",
           "w_in", "w_conv_out", "w_uq", "w_ukv", "w_mla_out", "w_out"]
    ffn1_w, ffn2_w, mix_w = big[0:3], big[3:6], big[6:]
    bufs, like = {}, {}
    for nm in big:
        like.setdefault(wts[nm].shape, []).append(nm)
    for gi, grp in enumerate(like.values()):
        for nm, per_layer in zip(grp, _cast_place([wts[nm] for nm in grp], place, f"cast_place_{gi}")):
            for l in range(nl):
                bufs[nm, l] = per_layer[l]

    def gather(keys):
        return _gather_comm([bufs[key] for key in keys])

    def landed(keys, outs):
        for key, o in zip(keys, outs):
            bufs[key] = o

    chunk = lambda nms, l: [(nm, l) for nm in nms]
    first_keys = chunk(ffn1_w, 0)
    landed(first_keys, _comm_call(gather(first_keys), "gather_first"))
    ride = {("ffn1", 0): chunk(mix_w, 0), ("flash", 0): chunk(ffn2_w, 0) + chunk(ffn1_w, 1),
            ("ffn2", 0): chunk(mix_w, 1), ("ffn1", 1): chunk(ffn2_w, 1)}

    def riding(kind, l):
        keys = ride.get((kind, l))
        return keys or [], (None if keys is None else gather(keys))

    def mixer_weights(l):
        w_in_nat = bufs["w_in", l].reshape(nat_g + 2 * d, d)
        w_in_k = jnp.concatenate([w_in_nat[nat_g:], w_in_nat[:nat_g], jnp.zeros((dp - nat_g - 2 * d, d), BF16)], axis=0)
        w_uq_k = jnp.pad(bufs["w_uq", l].reshape(N_HEADS, QK, ql), ((0, 0), (0, HP - QK), (0, 0))).reshape(hw, ql)
        w_mo_k = jnp.pad(_cols_full(bufs["w_mla_out", l]).reshape(N_HEADS, V_DIM, d),
                         ((0, 0), (HP - V_DIM, 0), (0, 0))).reshape(hw, d)
        return dict(w_in=w_in_k, w_co=_cols_full(bufs["w_conv_out", l]), w_uq=w_uq_k,
                    w_ukv=_cols_full(bufs["w_ukv", l]), w_mo=w_mo_k, w_out=bufs["w_out", l].reshape(d, d))

    qn_k = jnp.pad(q_norm, ((0, 0), (0, HP - QK)))
    kn_k = jnp.pad(k_norm, ((0, 0), (0, HP - QK)))
    small = _rows_pad([gate_bias.reshape(nl * 2, d // NCHIP), conv_w.reshape(nl * KW, cc // NCHIP)], d)
    everyone = _exchange(small, False, "gather_small")[0::2]
    gb_k = everyone[:, :nl * 2, :d // NCHIP].reshape(NCHIP, nl, 2, d // NCHIP).transpose(1, 2, 0, 3).reshape(nl, 2, d)
    cw = everyone[:, nl * 2:nl * 2 + nl * KW, :cc // NCHIP].reshape(NCHIP, nl, KW, cc // NCHIP)
    cw_k = jnp.pad(cw.transpose(1, 2, 0, 3).reshape(nl, KW, cc), ((0, 0), (0, CPAD - KW), (0, 0)))

    saved, mixw = [], []
    xc = x0
    for l in range(nl):
        keys, comm = riding("ffn1", l)
        x1, gt1, up1, got = _ffn_fwd(xc, ffn1_norm, l, *[bufs[nm, l] for nm in ffn1_w], f"ffn1_fwd_{l}", comm)
        landed(keys, got)
        mw = mixer_weights(l)
        mixw.append(mw)
        hm = _rms_fwd(x1, mix_norm, l, f"mix_norm_{l}")
        proj = _mm(hm, mw["w_in"], "nt", name=f"proj_{l}", out_dtype=BF16)
        uc = _dwconv_fwd(proj, cw_k, conv_b, l, lay["a"], cc, f"dwconv_{l}")
        sc = _ln_silu(uc, conv_ln_g, conv_ln_b, l, f"conv_ln_{l}")
        yc = _mm(sc, mw["w_co"], "nn", name=f"conv_out_{l}", out_dtype=BF16)
        q, k, kv = _mla_pre_fwd(proj, lay, l, cq_norm, ckv_norm, qn_k, kn_k, mw["w_uq"], mw["w_ukv"], tabs, f"mla_pre_{l}")
        keys, comm = riding("flash", l)
        o, lse, got = _flash_fwd(q, k, kv, f"flash_{l}", comm)
        landed(keys, got)
        ym = _mm(o, mw["w_mo"], "nn", name=f"mla_out_{l}", out_dtype=BF16)
        yv = _comb_fwd(proj, gb_k, l, yc, ym, f"combine_{l}")
        x2 = _mm(yv, mw["w_out"], "nn", name=f"mix_out_{l}", res=x1)
        keys, comm = riding("ffn2", l)
        x3, gt2, up2, got = _ffn_fwd(x2, ffn2_norm, l, *[bufs[nm, l] for nm in ffn2_w], f"ffn2_fwd_{l}", comm)
        landed(keys, got)
        saved.append(dict(x0=xc, x1=x1, gt1=gt1, up1=up1, hm=hm, proj=proj, uc=uc, sc=sc, yc=yc, q=q, k=k, kv=kv,
                          o=o, lse=lse, ym=ym, yv=yv, x2=x2, gt2=gt2, up2=up2))
        xc = x3
    dx, loss_row = _loss_grad(xc, target, "loss")

    gw = {nm: [None] * nl for nm in names}
    red = None

    def shard_groups(l):
        same = {}
        for nm in big:
            same.setdefault(gw[nm][l].shape, []).append(nm)
        return list(same.values())

    def add_parts(l, sib_part):
        qb = {}
        for gi, grp in enumerate(shard_groups(l)):
            outs = _add_cast([gw[nm][l] for nm in grp], [sib_part[nm] for nm in grp], place, f"rs_add_{l}_{gi}")
            qb.update(zip(grp, outs))
        return qb

    def sum_parts(l, qb, parts, prev):
        out = {}
        for gi, grp in enumerate(shard_groups(l)):
            outs = _sum_chips([parts[nm] for nm in grp], [qb[nm] for nm in grp], place, l, nl,
                              None if prev is None else [prev[nm] for nm in grp], f"rs_sum_{l}_{gi}")
            out.update(zip(grp, outs))
        return out

    for l in reversed(range(nl)):
        s, mw = saved[l], mixw[l]
        early = l + 1 if l + 1 < nl else None

        def ffn_back(tag, xin, gains, gts, ups, dout, comm=None):
            wg, wu, wd = (bufs[f"{tag}_w_{p}", l] for p in ("gate", "up", "down"))
            dxi, dgain, dgt, dup, act, hb, dob, got = _ffn_bwd(xin, gains, l, wg, wu, wd, gts, ups, dout,
                                                              f"{tag}_bwd_{l}", comm)
            gw[f"{tag}_w_gate"][l] = _mm(dgt, hb, "tn", name=f"{tag}_dwg_{l}", a_g=True)
            gw[f"{tag}_w_up"][l] = _mm(dup, hb, "tn", name=f"{tag}_dwu_{l}", a_g=True)
            gw[f"{tag}_w_down"][l] = _mm(act, dob, "tn", name=f"{tag}_dwd_{l}", a_g=True)
            gw[f"{tag}_norm"][l] = dgain
            return dxi, got

        comm = None if early is None else _pair_comm([gw[nm][early] for nm in big])
        dx2, got = ffn_back("ffn2", s["x2"], ffn2_norm, s["gt2"], s["up2"], dx, comm)
        if early is not None:
            qb = add_parts(early, dict(zip(big, got)))
        dyv = _mm(dx2, mw["w_out"], "nt", name=f"mix_out_dy_{l}", out_dtype=BF16)
        gw["w_out"][l] = _mm(s["yv"], dx2, "tn", name=f"mix_out_dw_{l}").reshape(NCHIP, d // NCHIP, d)
        dyc, dym, dproj, dgb = _comb_bwd(s["proj"], gb_k, l, s["yc"], s["ym"], dyv, dp, f"combine_bwd_{l}")
        gw["gate_bias"][l] = dgb
        dsc = _mm(dyc, mw["w_co"], "nt", name=f"conv_out_ds_{l}")
        gw["w_conv_out"][l] = _cols_shards(_mm(s["sc"], dyc, "tn", name=f"conv_out_dw_{l}"))
        duc, gw["conv_ln_g"][l], gw["conv_ln_b"][l], gw["conv_b"][l] = _ln_silu_bwd(
            s["uc"], conv_ln_g, conv_ln_b, l, dsc, f"conv_ln_bwd_{l}")
        du, dcw = _dwconv_bwd(duc, s["proj"], cw_k, l, lay["a"], cc, f"dwconv_bwd_{l}")
        gw["conv_w"][l] = dcw[:KW]
        dproj = _glu_bwd(du, s["proj"], dproj, lay["a"], cc, f"glu_bwd_{l}")
        do = _mm(dym, mw["w_mo"], "nt", name=f"mla_out_do_{l}", out_dtype=BF16)
        dwmo = _mm(s["o"], dym, "tn", name=f"mla_out_dw_{l}").reshape(N_HEADS, HP, d)[:, HP - V_DIM:].reshape(vw, d)
        gw["w_mla_out"][l] = _cols_shards(dwmo)
        comm = None if early is None else _chips_comm([qb[nm] for nm in big])
        dq, dk, dv, got = _flash_bwd(s["q"], s["k"], s["kv"], do, s["o"], s["lse"], f"flash_bwd_{l}", comm)
        if early is not None:
            red = sum_parts(early, qb, dict(zip(big, got)), red)
        dproj, dqr, dkv, cqn, ckvn, gw["cq_norm"][l], gw["ckv_norm"][l], dqn, dkn = _mla_pre_bwd(
            s["proj"], lay, l, cq_norm, ckv_norm, qn_k, kn_k, mw["w_uq"], mw["w_ukv"], tabs, dq, dk, dv, dproj,
            f"mla_pre_bwd_{l}")
        gw["q_norm"][l], gw["k_norm"][l] = dqn[:, :QK], dkn[:, :QK]
        dwuq = _mm(dqr, cqn, "tn", name=f"uq_dw_{l}").reshape(N_HEADS, HP, ql)[:, :QK]
        gw["w_uq"][l] = dwuq.reshape(NCHIP, N_HEADS * QK // NCHIP, ql)
        gw["w_ukv"][l] = _cols_shards(_mm(ckvn, dkv, "tn", name=f"ukv_dw_{l}"))
        dhm = _mm(dproj, mw["w_in"], "nn", name=f"proj_dh_{l}")
        dwin = _mm(dproj, s["hm"], "tn", name=f"proj_dw_{l}")
        gw["w_in"][l] = jnp.concatenate([dwin[2 * d:2 * d + nat_g], dwin[:2 * d]], axis=0).reshape(
            NCHIP, (nat_g + 2 * d) // NCHIP, d)
        dx1, gw["mix_norm"][l] = _rms_back(s["x1"], mix_norm, l, dhm, dx2, f"mix_norm_bwd_{l}")
        comm = None if early is None else _share_comm([red[nm] for nm in big], early)
        dx, got = ffn_back("ffn1", s["x0"], ffn1_norm, s["gt1"], s["up1"], dx1, comm)
        if early is not None:
            red = dict(zip(big, got))

    sib_part = dict(zip(big, _comm_call(_pair_comm([gw[nm][0] for nm in big]), "rs_pair")))
    qb = add_parts(0, sib_part)
    parts = dict(zip(big, _comm_call(_chips_comm([qb[nm] for nm in big]), "rs_chips")))
    red = sum_parts(0, qb, parts, red)
    red = dict(zip(big, _comm_call(_share_comm([red[nm] for nm in big], 0), "rs_share")))
    grads = {nm: g.reshape(wts[nm].shape) for nm, g in red.items()}

    vec_names = ["ffn1_norm", "mix_norm", "ffn2_norm", "conv_b", "conv_ln_g", "conv_ln_b", "cq_norm", "ckv_norm",
                 "q_norm", "k_norm"]
    rows = [jnp.concatenate(gw[nm], axis=0) for nm in vec_names]
    rows += [jnp.concatenate(gw["gate_bias"], axis=0), jnp.concatenate(gw["conv_w"], axis=0), loss_row]
    total = _exchange(_rows_pad(rows, d), True, "allreduce_small")
    r0 = 0
    for nm in vec_names:
        grads[nm] = total[r0:r0 + nl, :wts[nm].shape[1]]
        r0 += nl
    gb_all = total[r0:r0 + 2 * nl, :d].reshape(nl, 2, NCHIP, d // NCHIP)
    r0 += 2 * nl
    cw_all = total[r0:r0 + KW * nl, :cc].reshape(nl, KW, NCHIP, cc // NCHIP)
    r0 += KW * nl
    grads["gate_bias"] = lax.dynamic_index_in_dim(gb_all, chip, axis=2, keepdims=False)
    grads["conv_w"] = lax.dynamic_index_in_dim(cw_all, chip, axis=2, keepdims=False)
    loss = total[r0, 0]

    delta, new_m, new_v = {}, {}, {}
    by_shape = {}
    for nm in names:
        shp = wts[nm].shape
        by_shape.setdefault((shp[0] * (shp[1] if len(shp) == 3 else 1), shp[-1]), []).append(nm)
    for gi, (shp2, grp) in enumerate(by_shape.items()):
        to2 = lambda a: a.reshape(shp2)
        ds_, ms_, vs_ = _adamw([to2(wts[nm]) for nm in grp], [to2(grads[nm]) for nm in grp],
                               [to2(mom[nm]) for nm in grp], [to2(var[nm]) for nm in grp], f"adamw_{gi}")
        for nm, dd, mm_, vv in zip(grp, ds_, ms_, vs_):
            delta[nm], new_m[nm], new_v[nm] = (a.reshape(wts[nm].shape) for a in (dd, mm_, vv))

    return (loss, dx[None], *[view(nm, grads[nm]) for nm in names], *[view(nm, delta[nm]) for nm in names],
            *[view(nm, new_m[nm]) for nm in names], *[view(nm, new_v[nm]) for nm in names])
```

```python
import functools

import jax
import jax.numpy as jnp
from jax import lax
from jax.experimental import pallas as pl
from jax.experimental.pallas import tpu as pltpu

F32, BF16 = jnp.float32, jnp.bfloat16
SDS = jax.ShapeDtypeStruct
MESH = pl.DeviceIdType.MESH
ANY = pl.BlockSpec(memory_space=pl.ANY)

NCHIP = 4
N_DEV = 8
N_HEADS, NOPE, ROPE, V_DIM = 8, 64, 32, 64
QK = NOPE + ROPE
HP = 128
CHUNK = 64
KW = 31
CPAD = 32
ROPE_THETA = 10000.0
EPS = 1e-6
LR, B1, B2, EPS_ADAM, WD, STEP = 0.001, 0.9, 0.999, 1e-08, 0.01, 10
VMEM_BIG = 48 << 20
FFN_PARTS = 2


def _cparams(sem=None, vmem=None):
    kw = {}
    if sem is not None:
        kw["dimension_semantics"] = sem
    if vmem is not None:
        kw["vmem_limit_bytes"] = vmem
    return pltpu.CompilerParams(**kw)


def _rt(t):
    return min(512, t // 2)


def _pick(n, cands):
    for c in cands:
        if c <= n and n % c == 0:
            return c
    return n


def _tile2(r, c, nblocks, row_mult):
    tr, tc = r, c
    while 2 * nblocks * tr * tc * 4 > VMEM_BIG // 2 and tr % (2 * row_mult) == 0:
        tr //= 2
    while 2 * nblocks * tr * tc * 4 > VMEM_BIG // 2 and tc % 256 == 0:
        tc //= 2
    return tr, tc


def _sig(v):
    return 1.0 / (1.0 + jnp.exp(-v))


def _rms_r(v):
    return lax.rsqrt(jnp.mean(v * v, axis=-1, keepdims=True) + EPS)


def _rms_bwd(xv, g, dy):
    r = _rms_r(xv)
    xh = xv * r
    dg = jnp.sum(dy * xh, axis=0, keepdims=True)
    dxh = dy * g
    dx = r * (dxh - xh * jnp.mean(dxh * xh, axis=-1, keepdims=True))
    return dx, dg


def _dot(a, b):
    return jnp.dot(a, b, preferred_element_type=F32)


def _dot_nt(a, b):
    return lax.dot_general(a, b, (((1,), (1,)), ((), ())), preferred_element_type=F32)


def _dot_tn(a, b):
    return lax.dot_general(a, b, (((0,), (0,)), ((), ())), preferred_element_type=F32)


class _Comm:
    def __init__(self, ins, out_shapes, aliases, nsem, start, finish):
        self.ins, self.out_shapes, self.aliases, self.nsem = list(ins), list(out_shapes), dict(aliases), nsem
        self.start, self.finish = start, finish


def _call(body, comm, first, last, *, name, grid, in_specs, args, out_specs, out_shape, scratch, sem):
    in_specs, args, out_specs, out_shape, scratch = map(list, (in_specs, args, out_specs, out_shape, scratch))
    n_in, n_out, n_scr = len(args), len(out_shape), len(scratch)
    aliases = {}
    kern = body
    if comm is not None:
        nci, nco = len(comm.ins), len(comm.out_shapes)
        o0 = n_in + nci
        s0 = o0 + n_out + nco

        def kern(*refs):
            cins, couts = refs[n_in:o0], refs[o0 + n_out:s0]
            ssem, rsem = refs[s0 + n_scr:]
            pl.when(first())(lambda: comm.start(cins, couts, ssem, rsem))
            body(*refs[:n_in], *refs[o0:o0 + n_out], *refs[s0:s0 + n_scr])
            pl.when(last())(lambda: comm.finish(cins, couts, ssem, rsem))

        in_specs += [ANY] * nci
        args += comm.ins
        out_specs += [ANY] * nco
        out_shape += comm.out_shapes
        scratch += [pltpu.SemaphoreType.DMA((comm.nsem,))] * 2
        aliases = {n_in + i: n_out + o for i, o in comm.aliases.items()}
    outs = pl.pallas_call(kern, name=name, grid=grid, in_specs=in_specs, out_specs=out_specs, out_shape=out_shape,
                          scratch_shapes=scratch, input_output_aliases=aliases,
                          compiler_params=_cparams(sem, VMEM_BIG))(*args)
    return outs[:n_out], outs[n_out:]


class _SemView:
    def __init__(self, sems, base):
        self.sems, self.base = sems, base

    @property
    def at(self):
        return self

    def __getitem__(self, k):
        return self.sems.at[self.base + k]


def _join(*comms):
    ins, outs, aliases, spans, nsem = [], [], {}, [], 0
    for cm in comms:
        aliases.update({len(ins) + i: len(outs) + o for i, o in cm.aliases.items()})
        spans.append((len(ins), len(ins) + len(cm.ins), len(outs), len(outs) + len(cm.out_shapes), nsem))
        ins += cm.ins
        outs += cm.out_shapes
        nsem += cm.nsem

    def run(which, cins, couts, ssem, rsem):
        for cm, (i0, i1, o0, o1, base) in zip(comms, spans):
            getattr(cm, which)(cins[i0:i1], couts[o0:o1], _SemView(ssem, base), _SemView(rsem, base))

    joined = _Comm(ins, outs, aliases, nsem, functools.partial(run, "start"), functools.partial(run, "finish"))
    joined.spans = [(o0, o1) for _, _, o0, o1, _ in spans]
    return joined


def _comm_call(comm, name):
    once = lambda: pl.program_id(0) == 0
    return _call(lambda: None, comm, once, once, name=name, grid=(1,), in_specs=[], args=[], out_specs=[],
                 out_shape=[], scratch=[], sem=("arbitrary",))[1]


def _mm(a, b, mode, *, name, out_dtype=F32, res=None, a_g=False, comm=None):
    a2 = a.shape[1:] if a_g else a.shape
    if mode == "nn":
        (m, k), (k2, n) = a2, b.shape
    elif mode == "nt":
        (m, k), (n, k2) = a2, b.shape
    else:
        (k, m), (k2, n) = a2, b.shape
    assert k == k2, (name, a.shape, b.shape)
    g = a.shape[0] if a_g else 1
    tm = m if m <= 768 else _pick(m, (768, 512, 256, 128))
    tn = n if n <= 1280 else _pick(n, (1280, 1024, 768, 512, 256, 128))
    tk = k if k <= 1280 else _pick(k, (1280, 1024, 768, 512, 256, 128))
    fits = lambda kk, nn: 2 * (kk * tm * a.dtype.itemsize + kk * nn * b.dtype.itemsize + tm * nn * 4) <= VMEM_BIG - (8 << 20)
    if fits(k, n):
        tn, tk = n, k
    elif fits(k, tn):
        tk = k
    nk = k // tk
    dn = {"nn": (((1,), (0,)), ((), ())), "nt": (((1,), (1,)), ((), ())), "tn": (((0,), (0,)), ((), ()))}[mode]

    def body(*refs):
        a_ref, b_ref = refs[0], refs[1]
        res_ref = refs[2] if res is not None else None
        o_ref = refs[3] if res is not None else refs[2]
        p = lax.dot_general(a_ref[...].astype(BF16), b_ref[...].astype(BF16), dn, preferred_element_type=F32)

        def fin(v):
            if res_ref is not None:
                v = v + res_ref[...]
            o_ref[...] = v.astype(out_dtype)

        if nk == 1:
            fin(p)
        else:
            acc_ref = refs[-1]
            kk = pl.program_id(3)

            @pl.when(kk == 0)
            def _():
                acc_ref[...] = p

            @pl.when(kk > 0)
            def _():
                acc_ref[...] += p

            @pl.when(kk == nk - 1)
            def _():
                fin(acc_ref[...])

    a_block, a_idx = ((tk, tm), lambda gg, i, j, kk: (kk, i)) if mode == "tn" else ((tm, tk), lambda gg, i, j, kk: (i, kk))
    if a_g:
        a_spec = pl.BlockSpec((None,) + a_block, lambda gg, i, j, kk: (gg,) + a_idx(gg, i, j, kk))
    else:
        a_spec = pl.BlockSpec(a_block, a_idx)
    if mode == "nt":
        b_spec = pl.BlockSpec((tn, tk), lambda gg, i, j, kk: (j, kk))
    else:
        b_spec = pl.BlockSpec((tk, tn), lambda gg, i, j, kk: (kk, j))
    in_specs, args = [a_spec, b_spec], [a, b]
    if res is not None:
        in_specs.append(pl.BlockSpec((tm, tn), lambda gg, i, j, kk: (i, j)))
        args.append(res)
    if a_g:
        out_spec, oshape = pl.BlockSpec((None, tm, tn), lambda gg, i, j, kk: (gg, i, j)), (g, m, n)
    else:
        out_spec, oshape = pl.BlockSpec((tm, tn), lambda gg, i, j, kk: (i, j)), (m, n)
    grid = (g, m // tm, n // tn, nk)

    def at(corner):
        hit = pl.program_id(0) == corner[0]
        for ax in range(1, 4):
            hit = jnp.logical_and(hit, pl.program_id(ax) == corner[ax])
        return hit

    outs, couts = _call(body, comm, lambda: at((0, 0, 0, 0)), lambda: at(tuple(dim - 1 for dim in grid)), name=name,
                        grid=grid, in_specs=in_specs, args=args, out_specs=[out_spec], out_shape=[SDS(oshape, out_dtype)],
                        scratch=[pltpu.VMEM((tm, tn), F32)] if nk > 1 else [], sem=("arbitrary",) * 4)
    return outs[0] if comm is None else (outs[0], couts)


def _ffn_fwd(x, gains, l, wg, wu, wd, name, comm=None):
    t, d = x.shape
    f = wg.shape[-2]
    tt = _rt(t)

    def body(x_ref, g_ref, wg_ref, wu_ref, wd_ref, o_ref, gt_ref, up_ref, h_sc, acc_sc):
        j = pl.program_id(1)

        @pl.when(j == 0)
        def _():
            xv = x_ref[...]
            h_sc[...] = (xv * _rms_r(xv) * g_ref[l:l + 1, :]).astype(BF16)
            acc_sc[...] = jnp.zeros_like(acc_sc)

        for part in range(FFN_PARTS):
            rows = pl.ds(part * (tt // FFN_PARTS), tt // FFN_PARTS)
            h = h_sc[rows, :]
            gt = _dot_nt(h, wg_ref[...]).astype(BF16)
            up = _dot_nt(h, wu_ref[...]).astype(BF16)
            gt_ref[rows, :] = gt
            up_ref[rows, :] = up
            acc_sc[rows, :] += _dot(gt * _sig(gt) * up, wd_ref[...])

        @pl.when(j == NCHIP - 1)
        def _():
            o_ref[...] = x_ref[...] + 0.5 * acc_sc[...]

    wspec = pl.BlockSpec((None, f, d), lambda i, j: (j, 0, 0))
    row = pl.BlockSpec((tt, d), lambda i, j: (i, 0))
    sh = pl.BlockSpec((None, tt, f), lambda i, j: (j, i, 0))
    ni = t // tt
    first = lambda: jnp.logical_and(pl.program_id(0) == 0, pl.program_id(1) == 0)
    last = lambda: jnp.logical_and(pl.program_id(0) == ni - 1, pl.program_id(1) == NCHIP - 1)
    outs, couts = _call(
        body, comm, first, last, name=name, grid=(ni, NCHIP),
        in_specs=[row, pl.BlockSpec(gains.shape, lambda i, j: (0, 0)), wspec, wspec, wspec], args=[x, gains, wg, wu, wd],
        out_specs=[row, sh, sh], out_shape=[SDS((t, d), F32), SDS((NCHIP, t, f), BF16), SDS((NCHIP, t, f), BF16)],
        scratch=[pltpu.VMEM((tt, d), BF16), pltpu.VMEM((tt, d), F32)], sem=("arbitrary", "arbitrary"))
    return (*outs, couts)


def _ffn_bwd(x, gains, l, wg, wu, wd, gts, ups, dout, name, comm=None):
    t, d = x.shape
    f = wg.shape[-2]
    tt = _rt(t)

    def body(x_ref, g_ref, wg_ref, wu_ref, wd_ref, gt_ref, up_ref, do_ref,
             dx_ref, dg_ref, dgt_ref, dup_ref, act_ref, h_ref, dob_ref, dh_sc):
        i, j = pl.program_id(0), pl.program_id(1)

        @pl.when(j == 0)
        def _():
            xv = x_ref[...]
            h_ref[...] = (xv * _rms_r(xv) * g_ref[l:l + 1, :]).astype(BF16)
            dob_ref[...] = (0.5 * do_ref[...]).astype(BF16)
            dh_sc[...] = jnp.zeros_like(dh_sc)

        @pl.when(jnp.logical_and(i == 0, j == 0))
        def _():
            dg_ref[...] = jnp.zeros_like(dg_ref)

        for part in range(FFN_PARTS):
            rows = pl.ds(part * (tt // FFN_PARTS), tt // FFN_PARTS)
            dact = _dot_nt(dob_ref[rows, :], wd_ref[...]).astype(BF16)
            gt = gt_ref[rows, :]
            up = up_ref[rows, :]
            s = _sig(gt)
            sl = gt * s
            dup = dact * sl
            dgt = dact * up * (s + sl * (1.0 - s))
            dgt_ref[rows, :] = dgt
            dup_ref[rows, :] = dup
            act_ref[rows, :] = sl * up
            dh_sc[rows, :] += _dot(dgt, wg_ref[...]) + _dot(dup, wu_ref[...])

        @pl.when(j == NCHIP - 1)
        def _():
            dxn, dg = _rms_bwd(x_ref[...], g_ref[l:l + 1, :], dh_sc[...])
            dx_ref[...] = do_ref[...] + dxn
            dg_ref[...] += dg

    wspec = pl.BlockSpec((None, f, d), lambda i, j: (j, 0, 0))
    row = pl.BlockSpec((tt, d), lambda i, j: (i, 0))
    sh = pl.BlockSpec((None, tt, f), lambda i, j: (j, i, 0))
    ni = t // tt
    first = lambda: jnp.logical_and(pl.program_id(0) == 0, pl.program_id(1) == 0)
    last = lambda: jnp.logical_and(pl.program_id(0) == ni - 1, pl.program_id(1) == NCHIP - 1)
    outs, couts = _call(
        body, comm, first, last, name=name, grid=(ni, NCHIP),
        in_specs=[row, pl.BlockSpec(gains.shape, lambda i, j: (0, 0)), wspec, wspec, wspec, sh, sh, row],
        args=[x, gains, wg, wu, wd, gts, ups, dout],
        out_specs=[row, pl.BlockSpec((1, d), lambda i, j: (0, 0)), sh, sh, sh, row, row],
        out_shape=[SDS((t, d), F32), SDS((1, d), F32)] + [SDS((NCHIP, t, f), BF16)] * 3 + [SDS((t, d), BF16)] * 2,
        scratch=[pltpu.VMEM((tt, d), F32)], sem=("arbitrary", "arbitrary"))
    return (*outs, couts)


def _rms_fwd(x, gains, l, name):
    t, d = x.shape
    tt = _rt(t)

    def body(x_ref, g_ref, o_ref):
        xv = x_ref[...]
        o_ref[...] = (xv * _rms_r(xv) * g_ref[l:l + 1, :]).astype(BF16)

    row = pl.BlockSpec((tt, d), lambda i: (i, 0))
    return pl.pallas_call(body, name=name, grid=(t // tt,), in_specs=[row, pl.BlockSpec(gains.shape, lambda i: (0, 0))],
                          out_specs=row, out_shape=SDS((t, d), BF16), compiler_params=_cparams(("arbitrary",)))(x, gains)


def _rms_back(x, gains, l, dh, dres, name):
    t, d = x.shape
    tt = _rt(t)

    def body(x_ref, g_ref, dh_ref, dr_ref, dx_ref, dg_ref):
        @pl.when(pl.program_id(0) == 0)
        def _():
            dg_ref[...] = jnp.zeros_like(dg_ref)

        dxn, dg = _rms_bwd(x_ref[...], g_ref[l:l + 1, :], dh_ref[...])
        dx_ref[...] = dr_ref[...] + dxn
        dg_ref[...] += dg

    row = pl.BlockSpec((tt, d), lambda i: (i, 0))
    return pl.pallas_call(body, name=name, grid=(t // tt,),
                          in_specs=[row, pl.BlockSpec(gains.shape, lambda i: (0, 0)), row, row],
                          out_specs=[row, pl.BlockSpec((1, d), lambda i: (0, 0))],
                          out_shape=[SDS((t, d), F32), SDS((1, d), F32)],
                          compiler_params=_cparams(("arbitrary",)))(x, gains, dh, dres)


def _dwconv_fwd(proj, convw, convb, l, o_a, cc, name):
    t = proj.shape[0]
    r = min(256, t)
    nb = cc // HP

    def body(a_ref, gate_ref, w_ref, b_ref, uc_ref, u_sc):
        u_sc[0:CPAD, :] = jnp.zeros((CPAD, HP), F32)

        def fill(ci, carry):
            r0 = pl.multiple_of(ci * r, r)
            u_sc[pl.ds(CPAD + r0, r), :] = a_ref[pl.ds(r0, r), :].astype(F32) * _sig(gate_ref[pl.ds(r0, r), :].astype(F32))
            return carry

        lax.fori_loop(0, t // r, fill, 0)
        w = w_ref[l]
        bias = b_ref[l:l + 1, :]

        def conv(ci, carry):
            r0 = pl.multiple_of(ci * r, r)
            win = u_sc[pl.ds(r0, r + CPAD), :]
            acc = jnp.zeros((r, HP), F32) + bias
            for k in range(KW):
                off = CPAD - (KW - 1) + k
                acc = acc + win[off:off + r, :] * w[k:k + 1, :]
            uc_ref[pl.ds(r0, r), :] = acc
            return carry

        lax.fori_loop(0, t // r, conv, 0)

    col = lambda base: pl.BlockSpec((t, HP), lambda c: (0, base // HP + c))
    return pl.pallas_call(
        body, name=name, grid=(nb,),
        in_specs=[col(o_a), col(o_a + cc), pl.BlockSpec((convw.shape[0], CPAD, HP), lambda c: (0, 0, c)),
                  pl.BlockSpec((convb.shape[0], HP), lambda c: (0, c))],
        out_specs=pl.BlockSpec((t, HP), lambda c: (0, c)), out_shape=SDS((t, cc), F32),
        scratch_shapes=[pltpu.VMEM((t + CPAD, HP), F32)],
        compiler_params=_cparams(("arbitrary",), VMEM_BIG))(proj, proj, convw, convb)


def _dwconv_bwd(duc, proj, convw, l, o_a, cc, name):
    t = proj.shape[0]
    r = min(256, t)
    nb = cc // HP

    def body(d_ref, a_ref, gate_ref, w_ref, du_ref, dw_ref, u_sc, d_sc, dw_sc):
        u_sc[0:CPAD, :] = jnp.zeros((CPAD, HP), F32)
        d_sc[t:t + CPAD, :] = jnp.zeros((CPAD, HP), F32)
        dw_sc[...] = jnp.zeros_like(dw_sc)

        def fill(ci, carry):
            r0 = pl.multiple_of(ci * r, r)
            u_sc[pl.ds(CPAD + r0, r), :] = a_ref[pl.ds(r0, r), :].astype(F32) * _sig(gate_ref[pl.ds(r0, r), :].astype(F32))
            d_sc[pl.ds(r0, r), :] = d_ref[pl.ds(r0, r), :]
            return carry

        lax.fori_loop(0, t // r, fill, 0)
        w = w_ref[l]

        def conv(ci, carry):
            r0 = pl.multiple_of(ci * r, r)
            dwin = d_sc[pl.ds(r0, r + CPAD), :]
            uwin = u_sc[pl.ds(r0, r + CPAD), :]
            dcur = dwin[0:r, :]
            acc = jnp.zeros((r, HP), F32)
            for k in range(KW):
                acc = acc + dwin[KW - 1 - k:KW - 1 - k + r, :] * w[k:k + 1, :]
                off = CPAD - (KW - 1) + k
                part = (dcur * uwin[off:off + r, :]).reshape(r // 8, 8, HP).sum(axis=0)
                dw_sc[8 * k:8 * k + 8, :] += part
            du_ref[pl.ds(r0, r), :] = acc
            return carry

        lax.fori_loop(0, t // r, conv, 0)
        dw_ref[...] = jnp.zeros_like(dw_ref)
        for k in range(KW):
            dw_ref[k:k + 1, :] = jnp.sum(dw_sc[8 * k:8 * k + 8, :], axis=0, keepdims=True)

    col = lambda base: pl.BlockSpec((t, HP), lambda c: (0, base // HP + c))
    return pl.pallas_call(
        body, name=name, grid=(nb,),
        in_specs=[pl.BlockSpec((t, HP), lambda c: (0, c)), col(o_a), col(o_a + cc),
                  pl.BlockSpec((convw.shape[0], CPAD, HP), lambda c: (0, 0, c))],
        out_specs=[pl.BlockSpec((t, HP), lambda c: (0, c)), pl.BlockSpec((CPAD, HP), lambda c: (0, c))],
        out_shape=[SDS((t, cc), F32), SDS((CPAD, cc), F32)],
        scratch_shapes=[pltpu.VMEM((t + CPAD, HP), F32), pltpu.VMEM((t + CPAD, HP), F32), pltpu.VMEM((8 * CPAD, HP), F32)],
        compiler_params=_cparams(("arbitrary",), VMEM_BIG))(duc, proj, proj, convw)


def _ln_parts(uc, g, b):
    mu = jnp.mean(uc, axis=-1, keepdims=True)
    xc = uc - mu
    r = lax.rsqrt(jnp.mean(xc * xc, axis=-1, keepdims=True) + EPS)
    xh = xc * r
    return r, xh, xh * g + b


def _ln_silu(uc, ln_g, ln_b, l, name):
    t, cc = uc.shape
    tt = _rt(t)

    def body(u_ref, g_ref, b_ref, s_ref):
        _, _, yv = _ln_parts(u_ref[...], g_ref[l:l + 1, :], b_ref[l:l + 1, :])
        s_ref[...] = (yv * _sig(yv)).astype(BF16)

    row = pl.BlockSpec((tt, cc), lambda i: (i, 0))
    full = pl.BlockSpec(ln_g.shape, lambda i: (0, 0))
    return pl.pallas_call(body, name=name, grid=(t // tt,), in_specs=[row, full, full], out_specs=row,
                          out_shape=SDS((t, cc), BF16), compiler_params=_cparams(("arbitrary",)))(uc, ln_g, ln_b)


def _ln_silu_bwd(uc, ln_g, ln_b, l, ds, name):
    t, cc = uc.shape
    tt = _rt(t)

    def body(u_ref, g_ref, b_ref, ds_ref, du_ref, dg_ref, db_ref, dcb_ref):
        @pl.when(pl.program_id(0) == 0)
        def _():
            dg_ref[...] = jnp.zeros_like(dg_ref)
            db_ref[...] = jnp.zeros_like(db_ref)
            dcb_ref[...] = jnp.zeros_like(dcb_ref)

        g = g_ref[l:l + 1, :]
        r, xh, yv = _ln_parts(u_ref[...], g, b_ref[l:l + 1, :])
        sy = _sig(yv)
        dy = ds_ref[...] * (sy * (1.0 + yv * (1.0 - sy)))
        dg_ref[...] += jnp.sum(dy * xh, axis=0, keepdims=True)
        db_ref[...] += jnp.sum(dy, axis=0, keepdims=True)
        dxh = dy * g
        du = r * (dxh - jnp.mean(dxh, axis=-1, keepdims=True) - xh * jnp.mean(dxh * xh, axis=-1, keepdims=True))
        du_ref[...] = du
        dcb_ref[...] += jnp.sum(du, axis=0, keepdims=True)

    row = pl.BlockSpec((tt, cc), lambda i: (i, 0))
    full = pl.BlockSpec(ln_g.shape, lambda i: (0, 0))
    vec = pl.BlockSpec((1, cc), lambda i: (0, 0))
    return pl.pallas_call(body, name=name, grid=(t // tt,), in_specs=[row, full, full, row],
                          out_specs=[row, vec, vec, vec], out_shape=[SDS((t, cc), F32)] + [SDS((1, cc), F32)] * 3,
                          compiler_params=_cparams(("arbitrary",)))(uc, ln_g, ln_b, ds)


def _glu_bwd(du, proj, dproj, o_a, cc, name):
    t = du.shape[0]
    tt = _rt(t)

    def body(du_ref, a_ref, gate_ref, prev_ref, o_ref):
        sg = _sig(gate_ref[...].astype(F32))
        dv = du_ref[...]
        o_ref[:, 0:cc] = (dv * sg).astype(BF16)
        o_ref[:, cc:2 * cc] = (dv * a_ref[...] * sg * (1.0 - sg)).astype(BF16)

    return pl.pallas_call(
        body, name=name, grid=(t // tt,),
        in_specs=[pl.BlockSpec((tt, cc), lambda i: (i, 0)), pl.BlockSpec((tt, cc), lambda i: (i, o_a // cc)),
                  pl.BlockSpec((tt, cc), lambda i: (i, o_a // cc + 1)), ANY],
        out_specs=pl.BlockSpec((tt, 2 * cc), lambda i: (i, o_a // (2 * cc))),
        out_shape=SDS(dproj.shape, dproj.dtype), input_output_aliases={3: 0},
        compiler_params=_cparams(("arbitrary",)))(du, proj, proj, dproj)


def _rope(v, cs, s1, s2):
    return v * cs + pltpu.roll(v, HP - ROPE // 2, 1) * s1 + pltpu.roll(v, ROPE // 2, 1) * s2


def _rope_t(dv, cs, s1, s2):
    return dv * cs + pltpu.roll(dv * s1, ROPE // 2, 1) + pltpu.roll(dv * s2, HP - ROPE // 2, 1)


def _head_norm(v, g):
    r = lax.rsqrt(jnp.sum(v * v, axis=-1, keepdims=True) * (1.0 / QK) + EPS)
    return v * r * g, r


def _head_norm_bwd(v, r, g, dy):
    xh = v * r
    dg = jnp.sum(dy * xh, axis=0, keepdims=True)
    dxh = dy * g
    dx = r * (dxh - xh * (jnp.sum(dxh * xh, axis=-1, keepdims=True) * (1.0 / QK)))
    return dx, dg


def _mla_specs(tt, lay, cq_norm, ckv_norm, qn, kn, wuq, wukv):
    ql, kvl = cq_norm.shape[1], ckv_norm.shape[1]
    full = lambda a: pl.BlockSpec(a.shape, lambda i: (0,) * a.ndim)
    tab = pl.BlockSpec((tt, HP), lambda i: (i, 0))
    return [pl.BlockSpec((tt, ql), lambda i: (i, lay["cq"] // ql)),
            pl.BlockSpec((tt, kvl), lambda i: (i, lay["ckv"] // kvl)),
            pl.BlockSpec((tt, HP), lambda i: (i, lay["kr"] // HP)),
            full(cq_norm), full(ckv_norm), full(qn), full(kn), full(wuq), full(wukv), tab, tab, tab]


def _mla_pre_fwd(proj, lay, l, cq_norm, ckv_norm, qn, kn, wuq, wukv, tabs, name):
    t = proj.shape[0]
    tt = _rt(t)
    hw = N_HEADS * HP

    def body(cq_ref, ckv_ref, kr_ref, gq_ref, gkv_ref, qn_ref, kn_ref, wq_ref, wkv_ref, c_ref, s1_ref, s2_ref,
             q_ref, k_ref, v_ref):
        cq = cq_ref[...].astype(F32)
        cqn = (cq * _rms_r(cq) * gq_ref[l:l + 1, :]).astype(BF16)
        ckv = ckv_ref[...].astype(F32)
        ckvn = (ckv * _rms_r(ckv) * gkv_ref[l:l + 1, :]).astype(BF16)
        qraw = _dot_nt(cqn, wq_ref[...])
        kv = _dot(ckvn, wkv_ref[...])
        v_ref[...] = kv.astype(BF16)
        lane = lax.broadcasted_iota(jnp.int32, (tt, HP), 1)
        krs = pltpu.roll(jnp.where(lane < ROPE, kr_ref[...].astype(F32), 0.0), NOPE, 1)
        cs, s1, s2 = c_ref[...], s1_ref[...], s2_ref[...]
        gq, gk = qn_ref[l:l + 1, :], kn_ref[l:l + 1, :]
        for h in range(N_HEADS):
            sl = slice(h * HP, (h + 1) * HP)
            qh, _ = _head_norm(qraw[:, sl], gq)
            q_ref[:, sl] = (_rope(qh, cs, s1, s2) * QK ** -0.5).astype(BF16)
            kh, _ = _head_norm(jnp.where(lane < NOPE, kv[:, sl], krs), gk)
            k_ref[:, sl] = _rope(kh, cs, s1, s2).astype(BF16)

    row = pl.BlockSpec((tt, hw), lambda i: (i, 0))
    return pl.pallas_call(
        body, name=name, grid=(t // tt,),
        in_specs=_mla_specs(tt, lay, cq_norm, ckv_norm, qn, kn, wuq, wukv),
        out_specs=[row, row, row], out_shape=[SDS((t, hw), BF16)] * 3,
        compiler_params=_cparams(("arbitrary",), VMEM_BIG))(
            proj, proj, proj, cq_norm, ckv_norm, qn, kn, wuq, wukv, *tabs)


def _mla_pre_bwd(proj, lay, l, cq_norm, ckv_norm, qn, kn, wuq, wukv, tabs, dq, dk, dv, dproj, name):
    t = proj.shape[0]
    tt = _rt(t)
    hw = N_HEADS * HP
    ql, kvl = cq_norm.shape[1], ckv_norm.shape[1]
    wm = lay["wm"]

    def body(cq_ref, ckv_ref, kr_ref, gq_ref, gkv_ref, qn_ref, kn_ref, wq_ref, wkv_ref, c_ref, s1_ref, s2_ref,
             dq_ref, dk_ref, dv_ref, prev_ref,
             o_ref, dqr_ref, dkv_ref, cqn_ref, ckvn_ref, dgq_ref, dgkv_ref, dqn_ref, dkn_ref):
        @pl.when(pl.program_id(0) == 0)
        def _():
            for ref in (dgq_ref, dgkv_ref, dqn_ref, dkn_ref):
                ref[...] = jnp.zeros_like(ref)

        cq = cq_ref[...].astype(F32)
        cqn = (cq * _rms_r(cq) * gq_ref[l:l + 1, :]).astype(BF16)
        ckv = ckv_ref[...].astype(F32)
        ckvn = (ckv * _rms_r(ckv) * gkv_ref[l:l + 1, :]).astype(BF16)
        cqn_ref[...] = cqn
        ckvn_ref[...] = ckvn
        qraw = _dot_nt(cqn, wq_ref[...])
        kv = _dot(ckvn, wkv_ref[...])
        lane = lax.broadcasted_iota(jnp.int32, (tt, HP), 1)
        krs = pltpu.roll(jnp.where(lane < ROPE, kr_ref[...].astype(F32), 0.0), NOPE, 1)
        cs, s1, s2 = c_ref[...], s1_ref[...], s2_ref[...]
        gq, gk = qn_ref[l:l + 1, :], kn_ref[l:l + 1, :]
        dkr = jnp.zeros((tt, HP), F32)
        dgq = jnp.zeros((1, HP), F32)
        dgk = jnp.zeros((1, HP), F32)
        for h in range(N_HEADS):
            sl = slice(h * HP, (h + 1) * HP)
            qh = qraw[:, sl]
            _, rq = _head_norm(qh, gq)
            dqh, dg = _head_norm_bwd(qh, rq, gq, _rope_t(dq_ref[:, sl] * QK ** -0.5, cs, s1, s2))
            dgq = dgq + dg
            dqr_ref[:, sl] = dqh.astype(BF16)
            kp = jnp.where(lane < NOPE, kv[:, sl], krs)
            _, rk = _head_norm(kp, gk)
            dkp, dg = _head_norm_bwd(kp, rk, gk, _rope_t(dk_ref[:, sl], cs, s1, s2))
            dgk = dgk + dg
            dkv_ref[:, sl] = (jnp.where(lane < NOPE, dkp, 0.0) + dv_ref[:, sl]).astype(BF16)
            dkr = dkr + dkp
        dqn_ref[...] += dgq
        dkn_ref[...] += dgk
        dkr = jnp.where(lane < ROPE, pltpu.roll(dkr, HP - NOPE, 1), 0.0)
        dcq, dg = _rms_bwd(cq, gq_ref[l:l + 1, :], _dot(dqr_ref[...], wq_ref[...]))
        dgq_ref[...] += dg
        dckv, dg = _rms_bwd(ckv, gkv_ref[l:l + 1, :], _dot_nt(dkv_ref[...], wkv_ref[...]))
        dgkv_ref[...] += dg
        o_ref[:, 0:ql] = dcq.astype(BF16)
        o_ref[:, ql:ql + kvl] = dckv.astype(BF16)
        o_ref[:, ql + kvl:ql + kvl + HP] = dkr.astype(BF16)
        o_ref[:, ql + kvl + HP:wm] = jnp.zeros((tt, wm - ql - kvl - HP), BF16)

    row = pl.BlockSpec((tt, hw), lambda i: (i, 0))
    vec = lambda n: pl.BlockSpec((1, n), lambda i: (0, 0))
    return pl.pallas_call(
        body, name=name, grid=(t // tt,),
        in_specs=_mla_specs(tt, lay, cq_norm, ckv_norm, qn, kn, wuq, wukv) + [row, row, row, ANY],
        out_specs=[pl.BlockSpec((tt, wm), lambda i: (i, lay["cq"] // wm)), row, row,
                   pl.BlockSpec((tt, ql), lambda i: (i, 0)), pl.BlockSpec((tt, kvl), lambda i: (i, 0)),
                   vec(ql), vec(kvl), vec(HP), vec(HP)],
        out_shape=[SDS(dproj.shape, dproj.dtype), SDS((t, hw), BF16), SDS((t, hw), BF16), SDS((t, ql), BF16),
                   SDS((t, kvl), BF16), SDS((1, ql), F32), SDS((1, kvl), F32), SDS((1, HP), F32), SDS((1, HP), F32)],
        input_output_aliases={15: 0},
        compiler_params=_cparams(("arbitrary",), VMEM_BIG))(
            proj, proj, proj, cq_norm, ckv_norm, qn, kn, wuq, wukv, *tabs, dq, dk, dv, dproj)


def _comb_fwd(proj, gate_bias, l, yc, ym, name):
    t, d = yc.shape
    tt = _rt(t)

    def body(p_ref, b_ref, yc_ref, ym_ref, y_ref):
        b = b_ref[l]
        g0 = _sig(p_ref[:, 0:d] + b[0:1, :])
        g1 = _sig(p_ref[:, d:2 * d] + b[1:2, :])
        y_ref[...] = (g0 * yc_ref[...] + g1 * ym_ref[...]).astype(BF16)

    row = pl.BlockSpec((tt, d), lambda i: (i, 0))
    return pl.pallas_call(
        body, name=name, grid=(t // tt,),
        in_specs=[pl.BlockSpec((tt, 2 * d), lambda i: (i, 0)), pl.BlockSpec(gate_bias.shape, lambda i: (0, 0, 0)), row, row],
        out_specs=row, out_shape=SDS((t, d), BF16), compiler_params=_cparams(("arbitrary",)))(proj, gate_bias, yc, ym)


def _comb_bwd(proj, gate_bias, l, yc, ym, dy, dp_cols, name):
    t, d = yc.shape
    tt = _rt(t)

    def body(p_ref, b_ref, yc_ref, ym_ref, dy_ref, dyc_ref, dym_ref, dp_ref, db_ref):
        @pl.when(pl.program_id(0) == 0)
        def _():
            db_ref[...] = jnp.zeros_like(db_ref)

        b = b_ref[l]
        dyv = dy_ref[...].astype(F32)
        g0 = _sig(p_ref[:, 0:d] + b[0:1, :])
        g1 = _sig(p_ref[:, d:2 * d] + b[1:2, :])
        dyc_ref[...] = (dyv * g0).astype(BF16)
        dym_ref[...] = (dyv * g1).astype(BF16)
        dg0 = dyv * yc_ref[...] * g0 * (1.0 - g0)
        dg1 = dyv * ym_ref[...] * g1 * (1.0 - g1)
        dp_ref[:, 0:d] = dg0.astype(BF16)
        dp_ref[:, d:2 * d] = dg1.astype(BF16)
        db_ref[0:1, :] += jnp.sum(dg0, axis=0, keepdims=True)
        db_ref[1:2, :] += jnp.sum(dg1, axis=0, keepdims=True)

    row = pl.BlockSpec((tt, d), lambda i: (i, 0))
    wide = pl.BlockSpec((tt, 2 * d), lambda i: (i, 0))
    return pl.pallas_call(
        body, name=name, grid=(t // tt,),
        in_specs=[wide, pl.BlockSpec(gate_bias.shape, lambda i: (0, 0, 0)), row, row, row],
        out_specs=[row, row, wide, pl.BlockSpec((2, d), lambda i: (0, 0))],
        out_shape=[SDS((t, d), BF16), SDS((t, d), BF16), SDS((t, dp_cols), BF16), SDS((2, d), F32)],
        compiler_params=_cparams(("arbitrary",)))(proj, gate_bias, yc, ym, dy)


def _loss_grad(y, target, name):
    t, d = y.shape
    tt = _rt(t)
    nt = t // tt

    def body(y_ref, t_ref, dy_ref, loss_ref, acc_sc):
        i = pl.program_id(0)

        @pl.when(i == 0)
        def _():
            acc_sc[...] = jnp.zeros_like(acc_sc)

        diff = y_ref[...] - t_ref[...]
        dy_ref[...] = diff * (1.0 / d)
        acc_sc[...] += jnp.sum(diff * diff, axis=0, keepdims=True)

        @pl.when(i == nt - 1)
        def _():
            tot = jnp.sum(acc_sc[...], axis=1, keepdims=True) * (0.5 / d)
            loss_ref[...] = jnp.broadcast_to(tot, (1, HP))

    row = pl.BlockSpec((tt, d), lambda i: (i, 0))
    return pl.pallas_call(body, name=name, grid=(nt,), in_specs=[row, row],
                          out_specs=[row, pl.BlockSpec((1, HP), lambda i: (0, 0))],
                          out_shape=[SDS((t, d), F32), SDS((1, HP), F32)],
                          scratch_shapes=[pltpu.VMEM((1, d), F32)],
                          compiler_params=_cparams(("arbitrary",)))(y, target)


def _chunk_mask(tq):
    rows = lax.broadcasted_iota(jnp.int32, (tq, tq), 0) // CHUNK
    cols = lax.broadcasted_iota(jnp.int32, (tq, tq), 1) // CHUNK
    return cols <= rows


NEG = -1e30


def _flash_fwd(q, k, v, name, comm=None):
    t = q.shape[0]
    tq = _rt(t)
    nq = t // tq
    rep = tq // HP

    def body(q_ref, k_ref, v_ref, o_ref, lse_ref):
        qi = pl.program_id(1)
        qv = q_ref[...]

        def step(ki, carry, masked):
            m_prev, l_prev, acc = carry
            r0 = pl.multiple_of(ki * tq, tq)
            s = _dot_nt(qv, k_ref[pl.ds(r0, tq), :])
            if masked:
                s = jnp.where(_chunk_mask(tq), s, NEG)
            m_new = jnp.maximum(m_prev, jnp.max(s, axis=-1, keepdims=True))
            a = jnp.exp(m_prev - m_new)
            p = jnp.exp(s - jnp.tile(m_new, (1, rep)))
            l_new = a * l_prev + jnp.sum(p, axis=-1, keepdims=True)
            acc = a * acc + _dot(p.astype(BF16), v_ref[pl.ds(r0, tq), :])
            return m_new, l_new, acc

        init = (jnp.full((tq, HP), NEG, F32), jnp.zeros((tq, HP), F32), jnp.zeros((tq, HP), F32))
        carry = lax.fori_loop(0, qi, lambda ki, cr: step(ki, cr, False), init)
        m_fin, l_fin, acc = step(qi, carry, True)
        o_ref[...] = (acc / l_fin).astype(BF16)
        lse_ref[...] = m_fin + jnp.log(l_fin)

    qspec = pl.BlockSpec((tq, HP), lambda h, qi: (qi, h))
    head = pl.BlockSpec((t, HP), lambda h, qi: (0, h))
    first = lambda: jnp.logical_and(pl.program_id(0) == 0, pl.program_id(1) == 0)
    last = lambda: jnp.logical_and(pl.program_id(0) == N_HEADS - 1, pl.program_id(1) == nq - 1)
    outs, couts = _call(
        body, comm, first, last, name=name, grid=(N_HEADS, nq), in_specs=[qspec, head, head], args=[q, k, v],
        out_specs=[qspec, pl.BlockSpec((None, tq, HP), lambda h, qi: (h, qi, 0))],
        out_shape=[SDS(q.shape, BF16), SDS((N_HEADS, t, HP), F32)], scratch=[], sem=("arbitrary",) * 2)
    return (*outs, couts)


def _flash_bwd(q, k, v, do, o, lse, name, comm=None):
    t = q.shape[0]
    tq = _rt(t)
    nq = t // tq
    rep = tq // HP

    def body(q_ref, k_ref, v_ref, do_ref, o_ref, lse_ref, dq_ref, dk_ref, dv_ref, delta_sc):
        def prep(qi, carry):
            r0 = pl.multiple_of(qi * tq, tq)
            rows = pl.ds(r0, tq)
            dlt = jnp.sum(do_ref[rows, :].astype(F32) * o_ref[rows, :].astype(F32), axis=-1, keepdims=True)
            delta_sc[rows, :] = jnp.broadcast_to(dlt, (tq, HP))
            dq_ref[rows, :] = jnp.zeros((tq, HP), F32)
            return carry

        lax.fori_loop(0, nq, prep, 0)

        def keys(ki, carry0):
            krows = pl.ds(pl.multiple_of(ki * tq, tq), tq)
            kt, vt = k_ref[krows, :], v_ref[krows, :]

            def step(qi, carry, masked):
                dk_acc, dv_acc = carry
                rows = pl.ds(pl.multiple_of(qi * tq, tq), tq)
                qt, dot_ = q_ref[rows, :], do_ref[rows, :]
                s = _dot_nt(qt, kt)
                if masked:
                    s = jnp.where(_chunk_mask(tq), s, NEG)
                p = jnp.exp(s - jnp.tile(lse_ref[rows, :], (1, rep)))
                ds = (p * (_dot_nt(dot_, vt) - jnp.tile(delta_sc[rows, :], (1, rep)))).astype(BF16)
                dv_acc = dv_acc + _dot_tn(p.astype(BF16), dot_)
                dk_acc = dk_acc + _dot_tn(ds, qt)
                dq_ref[rows, :] += _dot(ds, kt)
                return dk_acc, dv_acc

            zero = jnp.zeros((tq, HP), F32)
            carry = step(ki, (zero, zero), True)
            dk_acc, dv_acc = lax.fori_loop(ki + 1, nq, lambda qi, cr: step(qi, cr, False), carry)
            dk_ref[krows, :] = dk_acc
            dv_ref[krows, :] = dv_acc
            return carry0

        lax.fori_loop(0, nq, keys, 0)

    head = pl.BlockSpec((t, HP), lambda h: (0, h))
    outs, couts = _call(
        body, comm, lambda: pl.program_id(0) == 0, lambda: pl.program_id(0) == N_HEADS - 1, name=name, grid=(N_HEADS,),
        in_specs=[head] * 5 + [pl.BlockSpec((None, t, HP), lambda h: (h, 0, 0))], args=[q, k, v, do, o, lse],
        out_specs=[head] * 3, out_shape=[SDS(q.shape, F32)] * 3, scratch=[pltpu.VMEM((t, HP), F32)], sem=("arbitrary",))
    return (*outs, couts)


def _add_cast(gs, rs, place, name):
    n = len(gs)
    _, r, c = gs[0].shape
    tr, tc = _tile2(r, c // 2, 3 * n, 16)
    nct = c // 2 // tc

    def body(place_ref, *refs):
        for a in range(n):
            refs[2 * n + a][...] = (refs[a][...] + refs[n + a][...]).astype(BF16)

    gspec = pl.BlockSpec((None, tr, tc), lambda j, i, k, pr: (j, i, pr[0] * nct + k))
    rspec = pl.BlockSpec((None, tr, tc), lambda j, i, k, pr: (j, i, k))
    grid_spec = pltpu.PrefetchScalarGridSpec(num_scalar_prefetch=1, grid=(NCHIP, r // tr, nct),
                                             in_specs=[gspec] * n + [rspec] * n, out_specs=[rspec] * n)
    return pl.pallas_call(body, name=name, grid_spec=grid_spec, out_shape=[SDS((NCHIP, r, c // 2), BF16)] * n,
                          compiler_params=_cparams(("arbitrary",) * 3, VMEM_BIG))(place, *gs, *rs)


def _sum_chips(ss, qs, place, l, nl, prevs, name):
    n = len(ss)
    _, r, h = ss[0].shape
    tr, tc = _tile2(r, h, 3 * n, 16)
    nct = h // tc

    def body(place_ref, *refs):
        for a in range(n):
            acc = refs[n + a][...].astype(F32)
            for kk in range(NCHIP - 1):
                acc = acc + refs[a][kk].astype(F32)
            refs[-n + a][...] = acc

    in_specs = ([pl.BlockSpec((NCHIP - 1, tr, tc), lambda i, k, pr: (0, i, k))] * n
                + [pl.BlockSpec((None, tr, tc), lambda i, k, pr: (pr[1], i, k))] * n)
    args = [*ss, *qs]
    aliases = {}
    if prevs is not None:
        aliases = {1 + len(args) + a: a for a in range(n)}
        in_specs += [ANY] * n
        args += list(prevs)
    grid_spec = pltpu.PrefetchScalarGridSpec(
        num_scalar_prefetch=1, grid=(r // tr, nct), in_specs=in_specs,
        out_specs=[pl.BlockSpec((None, tr, tc), lambda i, k, pr: (l, i, pr[0] * nct + k))] * n)
    return pl.pallas_call(body, name=name, grid_spec=grid_spec, out_shape=[SDS((nl, r, 2 * h), F32)] * n,
                          input_output_aliases=aliases,
                          compiler_params=_cparams(("arbitrary",) * 2, VMEM_BIG))(place, *args)


def _cast_place(ws, place, name):
    n = len(ws)
    nl, r, c = ws[0].shape
    tr, tc = _tile2(r, c, 2 * n * nl, 16)

    def body(place_ref, *refs):
        for a in range(n * nl):
            refs[n * nl + a][...] = refs[a][...].astype(BF16)

    in_specs = [pl.BlockSpec((None, tr, tc), functools.partial(lambda l, i, k, pr: (l, i, k), l))
                for _ in range(n) for l in range(nl)]
    grid_spec = pltpu.PrefetchScalarGridSpec(
        num_scalar_prefetch=1, grid=(r // tr, c // tc), in_specs=in_specs,
        out_specs=[pl.BlockSpec((None, tr, tc), lambda i, k, pr: (pr[1], i, k))] * (n * nl))
    outs = pl.pallas_call(body, name=name, grid_spec=grid_spec, out_shape=[SDS((NCHIP, r, c), BF16)] * (n * nl),
                          compiler_params=_cparams(("arbitrary",) * 2, VMEM_BIG))(
                              place, *[w for w in ws for _ in range(nl)])
    return [outs[a * nl:(a + 1) * nl] for a in range(n)]


def _adamw(ws, gs, ms, vs, name):
    n = len(ws)
    r, c = ws[0].shape
    tr, tc = _tile2(r, c, 7 * n, 8)
    c1, c2 = 1.0 / (1.0 - B1 ** STEP), 1.0 / (1.0 - B2 ** STEP)

    def body(*refs):
        for a in range(n):
            w, g, m, v = (refs[kk * n + a][...] for kk in range(4))
            m2 = B1 * m + (1.0 - B1) * g
            v2 = B2 * v + (1.0 - B2) * (g * g)
            refs[4 * n + a][...] = -LR * ((m2 * c1) / (jnp.sqrt(v2 * c2) + EPS_ADAM) + WD * w)
            refs[5 * n + a][...] = m2
            refs[6 * n + a][...] = v2

    blk = pl.BlockSpec((tr, tc), lambda i, k: (i, k))
    outs = pl.pallas_call(body, name=name, grid=(r // tr, c // tc), in_specs=[blk] * (4 * n),
                          out_specs=[blk] * (3 * n), out_shape=[SDS((r, c), F32)] * (3 * n),
                          compiler_params=_cparams(("arbitrary",) * 2, VMEM_BIG))(*ws, *gs, *ms, *vs)
    return outs[:n], outs[n:2 * n], outs[2 * n:]


def _place():
    x, y, c = lax.axis_index("x"), lax.axis_index("y"), lax.axis_index("c")
    return x, y, c, [(1 - x, y), (x, 1 - y), (1 - x, 1 - y)]


def _rcopy(src, dst, ssem, rsem, k, dev):
    return pltpu.make_async_remote_copy(src_ref=src, dst_ref=dst, send_sem=ssem.at[k], recv_sem=rsem.at[k],
                                        device_id=dev, device_id_type=MESH)


def _half(ref, lead, cc):
    h = ref.shape[-1] // 2
    return ref.at[(*lead, slice(None), pl.ds(cc * h, h))]


def _per_core(fn):
    c = lax.axis_index("c")
    for cc in (0, 1):
        pl.when(c == cc)(functools.partial(fn, cc))


def _gather_comm(bufs):
    n = len(bufs)

    def plan(couts, ssem, rsem, cc):
        x, y, _, peers = _place()
        me, sib = 2 * x + y, (x, y, 1 - cc)
        send, recv, fwd, recv2 = [], [], [], []
        for a in range(n):
            for kk, (px, py) in enumerate(peers):
                mine = _half(couts[a], (me,), cc)
                got = _half(couts[a], (2 * px + py,), cc)
                other = _half(couts[a], (2 * px + py,), 1 - cc)
                send.append(_rcopy(mine, mine, ssem, rsem, a * 6 + kk, (px, py, cc)))
                recv.append(_rcopy(got, got, ssem, rsem, a * 6 + kk, (px, py, cc)))
                fwd.append(_rcopy(got, got, ssem, rsem, a * 6 + 3 + kk, sib))
                recv2.append(_rcopy(other, other, ssem, rsem, a * 6 + 3 + kk, sib))
        return send, recv, fwd, recv2

    def start(cins, couts, ssem, rsem):
        def go(cc):
            for d in plan(couts, ssem, rsem, cc)[0]:
                d.start()

        _per_core(go)

    def finish(cins, couts, ssem, rsem):
        def go(cc):
            send, recv, fwd, recv2 = plan(couts, ssem, rsem, cc)
            for dr, df in zip(recv, fwd):
                dr.wait_recv()
                df.start()
            for d in recv2:
                d.wait_recv()
            for d in send + fwd:
                d.wait_send()

        _per_core(go)

    return _Comm(bufs, [SDS(b.shape, b.dtype) for b in bufs], {a: a for a in range(n)}, 6 * n, start, finish)


def _pair_comm(gs):
    n = len(gs)
    halves = [g.shape[-1] // 2 for g in gs]

    def plan(cins, couts, ssem, rsem, cc):
        x, y, _, _ = _place()
        return [_rcopy(cins[a].at[:, :, pl.ds((1 - cc) * halves[a], halves[a])], couts[a], ssem, rsem, a, (x, y, 1 - cc))
                for a in range(n)]

    def start(cins, couts, ssem, rsem):
        def go(cc):
            for d in plan(cins, couts, ssem, rsem, cc):
                d.start()

        _per_core(go)

    def finish(cins, couts, ssem, rsem):
        def go(cc):
            ds = plan(cins, couts, ssem, rsem, cc)
            for d in ds:
                d.wait_recv()
            for d in ds:
                d.wait_send()

        _per_core(go)

    return _Comm(gs, [SDS(g.shape[:-1] + (g.shape[-1] // 2,), g.dtype) for g in gs], {}, n, start, finish)


def _chips_comm(qs):
    n = len(qs)

    def plan(cins, couts, ssem, rsem):
        x, y, c, peers = _place()
        return [_rcopy(cins[a].at[2 * px + py], couts[a].at[kk], ssem, rsem, a * 3 + kk, (px, py, c))
                for a in range(n) for kk, (px, py) in enumerate(peers)]

    def start(cins, couts, ssem, rsem):
        for d in plan(cins, couts, ssem, rsem):
            d.start()

    def finish(cins, couts, ssem, rsem):
        ds = plan(cins, couts, ssem, rsem)
        for d in ds:
            d.wait_recv()
        for d in ds:
            d.wait_send()

    return _Comm(qs, [SDS((NCHIP - 1,) + q.shape[1:], q.dtype) for q in qs], {}, 3 * n, start, finish)


def _share_comm(fs, l):
    n = len(fs)

    def plan(couts, ssem, rsem, cc, which):
        x, y, _, _ = _place()
        out = []
        for a in range(n):
            piece = _half(couts[a], (l,), which)
            out.append(_rcopy(piece, piece, ssem, rsem, a, (x, y, 1 - cc)))
        return out

    def start(cins, couts, ssem, rsem):
        def go(cc):
            for d in plan(couts, ssem, rsem, cc, cc):
                d.start()

        _per_core(go)

    def finish(cins, couts, ssem, rsem):
        def go(cc):
            for d in plan(couts, ssem, rsem, cc, 1 - cc):
                d.wait_recv()
            for d in plan(couts, ssem, rsem, cc, cc):
                d.wait_send()

        _per_core(go)

    return _Comm(fs, [SDS(f.shape, f.dtype) for f in fs], {a: a for a in range(n)}, n, start, finish)


def _exchange(buf, reduce, name):
    r, w = buf.shape

    def body(in_ref, out_ref, recv_sc, ssem, rsem):
        x, y, c, _ = _place()
        me = 4 * x + 2 * y + c
        sends = []
        for rel in range(1, N_DEV):
            dev = (1 - x if rel & 4 else x, 1 - y if rel & 2 else y, 1 - c if rel & 1 else c)
            d = _rcopy(in_ref, recv_sc.at[me], ssem, rsem, rel - 1, dev)
            d.start()
            sends.append(d)
        recv_sc[me] = in_ref[...]
        for rel in range(1, N_DEV):
            px, py, pc = (1 - x if rel & 4 else x, 1 - y if rel & 2 else y, 1 - c if rel & 1 else c)
            slot = recv_sc.at[4 * px + 2 * py + pc]
            _rcopy(slot, slot, ssem, rsem, rel - 1, (px, py, pc)).wait_recv()
        for d in sends:
            d.wait_send()
        if reduce:
            acc = recv_sc[0]
            for dv in range(1, N_DEV):
                acc = acc + recv_sc[dv]
            out_ref[...] = acc
        else:
            out_ref[...] = recv_sc[...]

    vm = pl.BlockSpec(memory_space=pltpu.VMEM)
    return pl.pallas_call(
        body, name=name, in_specs=[vm], out_specs=vm,
        out_shape=SDS((r, w) if reduce else (N_DEV, r, w), F32),
        scratch_shapes=[pltpu.VMEM((N_DEV, r, w), F32), pltpu.SemaphoreType.DMA((N_DEV - 1,)),
                        pltpu.SemaphoreType.DMA((N_DEV - 1,))],
        compiler_params=_cparams(None, VMEM_BIG))(buf)


def _cols_full(g):
    _, k, ns = g.shape
    return g.transpose(1, 0, 2).reshape(k, NCHIP * ns)


def _cols_shards(w):
    k, n = w.shape
    return w.reshape(k, NCHIP, n // NCHIP).transpose(1, 0, 2)


def _rows_pad(rows, width):
    out = jnp.concatenate([jnp.pad(a, ((0, 0), (0, width - a.shape[1]))) for a in rows], axis=0)
    return jnp.pad(out, ((0, -out.shape[0] % 8), (0, 0)))


def kernel(x, positions, ffn1_norm, ffn1_w_gate, ffn1_w_up, ffn1_w_down, mix_norm, w_in, gate_bias, conv_w, conv_b, conv_ln_g, conv_ln_b, w_conv_out, cq_norm, ckv_norm, w_uq, w_ukv, q_norm, k_norm, w_mla_out, w_out, ffn2_norm, ffn2_w_gate, ffn2_w_up, ffn2_w_down, loss_target, m_ffn1_norm, m_ffn1_w_gate, m_ffn1_w_up, m_ffn1_w_down, m_mix_norm, m_w_in, m_gate_bias, m_conv_w, m_conv_b, m_conv_ln_g, m_conv_ln_b, m_w_conv_out, m_cq_norm, m_ckv_norm, m_w_uq, m_w_ukv, m_q_norm, m_k_norm, m_w_mla_out, m_w_out, m_ffn2_norm, m_ffn2_w_gate, m_ffn2_w_up, m_ffn2_w_down, v_ffn1_norm, v_ffn1_w_gate, v_ffn1_w_up, v_ffn1_w_down, v_mix_norm, v_w_in, v_gate_bias, v_conv_w, v_conv_b, v_conv_ln_g, v_conv_ln_b, v_w_conv_out, v_cq_norm, v_ckv_norm, v_w_uq, v_w_ukv, v_q_norm, v_k_norm, v_w_mla_out, v_w_out, v_ffn2_norm, v_ffn2_w_gate, v_ffn2_w_up, v_ffn2_w_down):
    names = ["ffn1_norm", "ffn1_w_gate", "ffn1_w_up", "ffn1_w_down", "mix_norm", "w_in", "gate_bias", "conv_w",
             "conv_b", "conv_ln_g", "conv_ln_b", "w_conv_out", "cq_norm", "ckv_norm", "w_uq", "w_ukv", "q_norm",
             "k_norm", "w_mla_out", "w_out", "ffn2_norm", "ffn2_w_gate", "ffn2_w_up", "ffn2_w_down"]
    env = dict(locals())
    turned = ("ffn1_w_gate", "ffn1_w_up", "ffn2_w_gate", "ffn2_w_up", "w_in", "w_uq")
    view = lambda nm, a: jnp.swapaxes(a, 1, 2) if nm in turned else a
    wts = {nm: view(nm, env[nm]) for nm in names}
    mom = {nm: view(nm, env["m_" + nm]) for nm in names}
    var = {nm: view(nm, env["v_" + nm]) for nm in names}

    t, d = x.shape[1], x.shape[2]
    nl = ffn1_norm.shape[0]
    cc = conv_b.shape[1]
    ql, kvl = cq_norm.shape[1], ckv_norm.shape[1]
    vw = N_HEADS * V_DIM
    hw = N_HEADS * HP
    lay = {"a": 2 * d, "cq": 2 * d + 2 * cc, "ckv": 2 * d + 2 * cc + ql, "kr": 2 * d + 2 * cc + ql + kvl,
           "wm": ql + kvl + 2 * HP}
    dp = lay["cq"] + lay["wm"]
    nat_g = 2 * cc + ql + kvl + ROPE
    assert lay["cq"] % lay["wm"] == 0 and lay["cq"] % ql == 0 and lay["ckv"] % kvl == 0 and lay["a"] % (2 * cc) == 0
    assert cc % HP == 0 and d % HP == 0 and t % (2 * CHUNK) == 0 and w_in.shape[2] * NCHIP == nat_g + 2 * d
    assert nl == 2

    x0, target = x[0], loss_target[0]
    chip = 2 * lax.axis_index("x") + lax.axis_index("y")
    place = jnp.stack([lax.axis_index("c"), chip]).astype(jnp.int32)

    inv_freq = ROPE_THETA ** (-jnp.arange(0, ROPE, 2, dtype=F32) / ROPE)
    ang = positions[0].astype(F32)[:, None] * inv_freq
    cos, sin, z = jnp.cos(ang), jnp.sin(ang), jnp.zeros((t, ROPE // 2), F32)
    tabs = (jnp.concatenate([jnp.ones((t, NOPE), F32), cos, cos, jnp.zeros((t, HP - QK), F32)], axis=1),
            jnp.concatenate([jnp.zeros((t, NOPE), F32), -sin, z, jnp.zeros((t, HP - QK), F32)], axis=1),
            jnp.concatenate([jnp.zeros((t, NOPE), F32), z, sin, jnp.zeros((t, HP - QK), F32)], axis=1))

    big = ["ffn1_w_gate", "ffn1_w_up", "ffn1_w_down", "ffn2_w_gate", "ffn2_w_up", "ffn2_w_down",
           "w_in", "w_conv_out", "w_uq", "w_ukv", "w_mla_out", "w_out"]
    ffn1_w, ffn2_w, mix_w = big[0:3], big[3:6], big[6:]
    bufs, like = {}, {}
    for nm in big:
        like.setdefault(wts[nm].shape, []).append(nm)
    for gi, grp in enumerate(like.values()):
        for nm, per_layer in zip(grp, _cast_place([wts[nm] for nm in grp], place, f"cast_place_{gi}")):
            for l in range(nl):
                bufs[nm, l] = per_layer[l]

    def gather(keys):
        return _gather_comm([bufs[key] for key in keys])

    def landed(keys, outs):
        for key, o in zip(keys, outs):
            bufs[key] = o

    chunk = lambda nms, l: [(nm, l) for nm in nms]
    first_keys = chunk(ffn1_w, 0)
    landed(first_keys, _comm_call(gather(first_keys), "gather_first"))
    ride = {("ffn1", 0): chunk(mix_w, 0), ("flash", 0): chunk(ffn2_w, 0) + chunk(ffn1_w, 1),
            ("ffn2", 0): chunk(mix_w, 1), ("ffn1", 1): chunk(ffn2_w, 1)}

    def riding(kind, l):
        keys = ride.get((kind, l))
        return keys or [], (None if keys is None else gather(keys))

    def mixer_weights(l):
        w_in_nat = bufs["w_in", l].reshape(nat_g + 2 * d, d)
        w_in_k = jnp.concatenate([w_in_nat[nat_g:], w_in_nat[:nat_g], jnp.zeros((dp - nat_g - 2 * d, d), BF16)], axis=0)
        w_uq_k = jnp.pad(bufs["w_uq", l].reshape(N_HEADS, QK, ql), ((0, 0), (0, HP - QK), (0, 0))).reshape(hw, ql)
        w_mo_k = jnp.pad(_cols_full(bufs["w_mla_out", l]).reshape(N_HEADS, V_DIM, d),
                         ((0, 0), (HP - V_DIM, 0), (0, 0))).reshape(hw, d)
        return dict(w_in=w_in_k, w_co=_cols_full(bufs["w_conv_out", l]), w_uq=w_uq_k,
                    w_ukv=_cols_full(bufs["w_ukv", l]), w_mo=w_mo_k, w_out=bufs["w_out", l].reshape(d, d))

    qn_k = jnp.pad(q_norm, ((0, 0), (0, HP - QK)))
    kn_k = jnp.pad(k_norm, ((0, 0), (0, HP - QK)))
    small = _rows_pad([gate_bias.reshape(nl * 2, d // NCHIP), conv_w.reshape(nl * KW, cc // NCHIP)], d)
    everyone = _exchange(small, False, "gather_small")[0::2]
    gb_k = everyone[:, :nl * 2, :d // NCHIP].reshape(NCHIP, nl, 2, d // NCHIP).transpose(1, 2, 0, 3).reshape(nl, 2, d)
    cw = everyone[:, nl * 2:nl * 2 + nl * KW, :cc // NCHIP].reshape(NCHIP, nl, KW, cc // NCHIP)
    cw_k = jnp.pad(cw.transpose(1, 2, 0, 3).reshape(nl, KW, cc), ((0, 0), (0, CPAD - KW), (0, 0)))

    saved, mixw = [], []
    xc = x0
    for l in range(nl):
        keys, comm = riding("ffn1", l)
        x1, gt1, up1, got = _ffn_fwd(xc, ffn1_norm, l, *[bufs[nm, l] for nm in ffn1_w], f"ffn1_fwd_{l}", comm)
        landed(keys, got)
        mw = mixer_weights(l)
        mixw.append(mw)
        hm = _rms_fwd(x1, mix_norm, l, f"mix_norm_{l}")
        proj = _mm(hm, mw["w_in"], "nt", name=f"proj_{l}", out_dtype=BF16)
        uc = _dwconv_fwd(proj, cw_k, conv_b, l, lay["a"], cc, f"dwconv_{l}")
        sc = _ln_silu(uc, conv_ln_g, conv_ln_b, l, f"conv_ln_{l}")
        yc = _mm(sc, mw["w_co"], "nn", name=f"conv_out_{l}", out_dtype=BF16)
        q, k, kv = _mla_pre_fwd(proj, lay, l, cq_norm, ckv_norm, qn_k, kn_k, mw["w_uq"], mw["w_ukv"], tabs, f"mla_pre_{l}")
        keys, comm = riding("flash", l)
        o, lse, got = _flash_fwd(q, k, kv, f"flash_{l}", comm)
        landed(keys, got)
        ym = _mm(o, mw["w_mo"], "nn", name=f"mla_out_{l}", out_dtype=BF16)
        yv = _comb_fwd(proj, gb_k, l, yc, ym, f"combine_{l}")
        x2 = _mm(yv, mw["w_out"], "nn", name=f"mix_out_{l}", res=x1)
        keys, comm = riding("ffn2", l)
        x3, gt2, up2, got = _ffn_fwd(x2, ffn2_norm, l, *[bufs[nm, l] for nm in ffn2_w], f"ffn2_fwd_{l}", comm)
        landed(keys, got)
        saved.append(dict(x0=xc, x1=x1, gt1=gt1, up1=up1, hm=hm, proj=proj, uc=uc, sc=sc, yc=yc, q=q, k=k, kv=kv,
                          o=o, lse=lse, ym=ym, yv=yv, x2=x2, gt2=gt2, up2=up2))
        xc = x3
    dx, loss_row = _loss_grad(xc, target, "loss")

    gw = {nm: [None] * nl for nm in names}
    red = {}

    def shard_groups(nms, l):
        same = {}
        for nm in nms:
            same.setdefault(gw[nm][l].shape, []).append(nm)
        return list(same.values())

    def add_parts(nms, l, sib_part, tag):
        qb = {}
        for gi, grp in enumerate(shard_groups(nms, l)):
            outs = _add_cast([gw[nm][l] for nm in grp], [sib_part[nm] for nm in grp], place, f"rs_add_{tag}_{gi}")
            qb.update(zip(grp, outs))
        return qb

    def sum_parts(nms, l, qb, parts, tag):
        for gi, grp in enumerate(shard_groups(nms, l)):
            prevs = [red[nm] for nm in grp] if grp[0] in red else None
            outs = _sum_chips([parts[nm] for nm in grp], [qb[nm] for nm in grp], place, l, nl, prevs, f"rs_sum_{tag}_{gi}")
            red.update(zip(grp, outs))

    pair = lambda nms, l: _pair_comm([gw[nm][l] for nm in nms])
    chips = lambda nms, qb: _chips_comm([qb[nm] for nm in nms])
    share = lambda nms, l: _share_comm([red[nm] for nm in nms], l)

    def split(joined, got):
        return [got[o0:o1] for o0, o1 in joined.spans]

    mix_parts = [["w_in"], ["w_out", "w_conv_out"], ["w_mla_out", "w_ukv", "w_uq"]]

    for l in reversed(range(nl)):
        s, mw = saved[l], mixw[l]
        riding_rs = l == 0

        def ffn_back(tag, xin, gains, gts, ups, dout, comm=None):
            wg, wu, wd = (bufs[f"{tag}_w_{p}", l] for p in ("gate", "up", "down"))
            dxi, dgain, dgt, dup, act, hb, dob, got = _ffn_bwd(xin, gains, l, wg, wu, wd, gts, ups, dout,
                                                              f"{tag}_bwd_{l}", comm)
            for p, lhs, rhs in zip(("gate", "up", "down"), (dgt, dup, act), (hb, hb, dob)):
                gw[f"{tag}_w_{p}"][l] = _mm(lhs, rhs, "tn", name=f"{tag}_dw_{p}_{l}", a_g=True)
            gw[f"{tag}_norm"][l] = dgain
            return dxi, got

        dx2, got = ffn_back("ffn2", s["x2"], ffn2_norm, s["gt2"], s["up2"], dx, pair(big, 1) if riding_rs else None)
        if riding_rs:
            qb1 = add_parts(big, 1, dict(zip(big, got)), "l1")
        dyv = _mm(dx2, mw["w_out"], "nt", name=f"mix_out_dy_{l}", out_dtype=BF16)
        gw["w_out"][l] = _mm(s["yv"], dx2, "tn", name=f"mix_out_dw_{l}").reshape(NCHIP, d // NCHIP, d)
        dyc, dym, dproj, dgb = _comb_bwd(s["proj"], gb_k, l, s["yc"], s["ym"], dyv, dp, f"combine_bwd_{l}")
        gw["gate_bias"][l] = dgb
        dsc = _mm(dyc, mw["w_co"], "nt", name=f"conv_out_ds_{l}")
        gw["w_conv_out"][l] = _cols_shards(_mm(s["sc"], dyc, "tn", name=f"conv_out_dw_{l}"))
        duc, gw["conv_ln_g"][l], gw["conv_ln_b"][l], gw["conv_b"][l] = _ln_silu_bwd(
            s["uc"], conv_ln_g, conv_ln_b, l, dsc, f"conv_ln_bwd_{l}")
        du, dcw = _dwconv_bwd(duc, s["proj"], cw_k, l, lay["a"], cc, f"dwconv_bwd_{l}")
        gw["conv_w"][l] = dcw[:KW]
        dproj = _glu_bwd(du, s["proj"], dproj, lay["a"], cc, f"glu_bwd_{l}")
        do = _mm(dym, mw["w_mo"], "nt", name=f"mla_out_do_{l}", out_dtype=BF16)
        dwmo = _mm(s["o"], dym, "tn", name=f"mla_out_dw_{l}").reshape(N_HEADS, HP, d)[:, HP - V_DIM:].reshape(vw, d)
        gw["w_mla_out"][l] = _cols_shards(dwmo)
        comm = _join(chips(big, qb1), pair(ffn2_w, 0)) if riding_rs else None
        dq, dk, dv, got = _flash_bwd(s["q"], s["k"], s["kv"], do, s["o"], s["lse"], f"flash_bwd_{l}", comm)
        if riding_rs:
            parts1, sib_f2 = split(comm, got)
            sum_parts(big, 1, qb1, dict(zip(big, parts1)), "l1")
            qb_f2 = add_parts(ffn2_w, 0, dict(zip(ffn2_w, sib_f2)), "f2")
        dproj, dqr, dkv, cqn, ckvn, gw["cq_norm"][l], gw["ckv_norm"][l], dqn, dkn = _mla_pre_bwd(
            s["proj"], lay, l, cq_norm, ckv_norm, qn_k, kn_k, mw["w_uq"], mw["w_ukv"], tabs, dq, dk, dv, dproj,
            f"mla_pre_bwd_{l}")
        gw["q_norm"][l], gw["k_norm"][l] = dqn[:, :QK], dkn[:, :QK]
        dwuq = _mm(dqr, cqn, "tn", name=f"uq_dw_{l}").reshape(N_HEADS, HP, ql)[:, :QK]
        gw["w_uq"][l] = dwuq.reshape(NCHIP, N_HEADS * QK // NCHIP, ql)
        gw["w_ukv"][l] = _cols_shards(_mm(ckvn, dkv, "tn", name=f"ukv_dw_{l}"))
        dhm = _mm(dproj, mw["w_in"], "nn", name=f"proj_dh_{l}")
        dwin = _mm(dproj, s["hm"], "tn", name=f"proj_dw_{l}")
        gw["w_in"][l] = jnp.concatenate([dwin[2 * d:2 * d + nat_g], dwin[:2 * d]], axis=0).reshape(
            NCHIP, (nat_g + 2 * d) // NCHIP, d)
        dx1, gw["mix_norm"][l] = _rms_back(s["x1"], mix_norm, l, dhm, dx2, f"mix_norm_bwd_{l}")
        if not riding_rs:
            dx, _ = ffn_back("ffn1", s["x0"], ffn1_norm, s["gt1"], s["up1"], dx1)
            continue
        comm = _join(share(big, 1), chips(ffn2_w, qb_f2), pair(mix_w, 0))
        wg, wu, wd = (bufs[f"ffn1_w_{p}", l] for p in ("gate", "up", "down"))
        dx, dgain, dgt, dup, act, hb, dob, got = _ffn_bwd(s["x0"], ffn1_norm, l, wg, wu, wd, s["gt1"], s["up1"], dx1,
                                                         f"ffn1_bwd_{l}", comm)
        gw["ffn1_norm"][l] = dgain
        shared, parts_f2, sib_mx = split(comm, got)
        red.update(zip(big, shared))
        sum_parts(ffn2_w, 0, qb_f2, dict(zip(ffn2_w, parts_f2)), "f2")
        qb_mx = add_parts(mix_w, 0, dict(zip(mix_w, sib_mx)), "mx")
        parts_mx = {}
        for part, (p, lhs, rhs) in enumerate(zip(("gate", "up", "down"), (dgt, dup, act), (hb, hb, dob))):
            gw[f"ffn1_w_{p}"][l], got = _mm(lhs, rhs, "tn", name=f"ffn1_dw_{p}_{l}", a_g=True,
                                            comm=chips(mix_parts[part], qb_mx))
            parts_mx.update(zip(mix_parts[part], got))
        sum_parts(mix_w, 0, qb_mx, parts_mx, "mx")

    sib_f1 = dict(zip(ffn1_w, _comm_call(pair(ffn1_w, 0), "rs_pair")))
    qb_f1 = add_parts(ffn1_w, 0, sib_f1, "f1")
    parts_f1 = dict(zip(ffn1_w, _comm_call(chips(ffn1_w, qb_f1), "rs_chips")))
    sum_parts(ffn1_w, 0, qb_f1, parts_f1, "f1")
    red = dict(zip(big, _comm_call(share(big, 0), "rs_share")))
    grads = {nm: g.reshape(wts[nm].shape) for nm, g in red.items()}

    vec_names = ["ffn1_norm", "mix_norm", "ffn2_norm", "conv_b", "conv_ln_g", "conv_ln_b", "cq_norm", "ckv_norm",
                 "q_norm", "k_norm"]
    rows = [jnp.concatenate(gw[nm], axis=0) for nm in vec_names]
    rows += [jnp.concatenate(gw["gate_bias"], axis=0), jnp.concatenate(gw["conv_w"], axis=0), loss_row]
    total = _exchange(_rows_pad(rows, d), True, "allreduce_small")
    r0 = 0
    for nm in vec_names:
        grads[nm] = total[r0:r0 + nl, :wts[nm].shape[1]]
        r0 += nl
    gb_all = total[r0:r0 + 2 * nl, :d].reshape(nl, 2, NCHIP, d // NCHIP)
    r0 += 2 * nl
    cw_all = total[r0:r0 + KW * nl, :cc].reshape(nl, KW, NCHIP, cc // NCHIP)
    r0 += KW * nl
    grads["gate_bias"] = lax.dynamic_index_in_dim(gb_all, chip, axis=2, keepdims=False)
    grads["conv_w"] = lax.dynamic_index_in_dim(cw_all, chip, axis=2, keepdims=False)
    loss = total[r0, 0]

    delta, new_m, new_v = {}, {}, {}
    by_shape = {}
    for nm in names:
        shp = wts[nm].shape
        by_shape.setdefault((shp[0] * (shp[1] if len(shp) == 3 else 1), shp[-1]), []).append(nm)
    for gi, (shp2, grp) in enumerate(by_shape.items()):
        to2 = lambda a: a.reshape(shp2)
        ds_, ms_, vs_ = _adamw([to2(wts[nm]) for nm in grp], [to2(grads[nm]) for nm in grp],
                               [to2(mom[nm]) for nm in grp], [to2(var[nm]) for nm in grp], f"adamw_{gi}")
        for nm, dd, mm_, vv in zip(grp, ds_, ms_, vs_):
            delta[nm], new_m[nm], new_v[nm] = (a.reshape(wts[nm].shape) for a in (dd, mm_, vv))

    return (loss, dx[None], *[view(nm, grads[nm]) for nm in names], *[view(nm, delta[nm]) for nm in names],
            *[view(nm, new_m[nm]) for nm in names], *[view(nm, new_v[nm]) for nm in names])
```

```python
import functools

import jax
import jax.numpy as jnp
from jax import lax
from jax.experimental import pallas as pl
from jax.experimental.pallas import tpu as pltpu

F32, BF16 = jnp.float32, jnp.bfloat16
SDS = jax.ShapeDtypeStruct
MESH = pl.DeviceIdType.MESH
ANY = pl.BlockSpec(memory_space=pl.ANY)

NCHIP = 4
N_DEV = 8
N_HEADS, NOPE, ROPE, V_DIM = 8, 64, 32, 64
QK = NOPE + ROPE
HP = 128
CHUNK = 64
KW = 31
CPAD = 32
ROPE_THETA = 10000.0
EPS = 1e-6
LR, B1, B2, EPS_ADAM, WD, STEP = 0.001, 0.9, 0.999, 1e-08, 0.01, 10
VMEM_BIG = 48 << 20
FFN_PARTS = 2
FLASH_HEADS = 2


def _cparams(sem=None, vmem=None):
    kw = {}
    if sem is not None:
        kw["dimension_semantics"] = sem
    if vmem is not None:
        kw["vmem_limit_bytes"] = vmem
    return pltpu.CompilerParams(**kw)


def _rt(t):
    return min(512, t // 2)


def _pick(n, cands):
    for c in cands:
        if c <= n and n % c == 0:
            return c
    return n


def _tile2(r, c, nblocks, row_mult):
    tr, tc = r, c
    while 2 * nblocks * tr * tc * 4 > VMEM_BIG // 2 and tr % (2 * row_mult) == 0:
        tr //= 2
    while 2 * nblocks * tr * tc * 4 > VMEM_BIG // 2 and tc % 256 == 0:
        tc //= 2
    return tr, tc


def _sig(v):
    return 1.0 / (1.0 + jnp.exp(-v))


def _rms_r(v):
    return lax.rsqrt(jnp.mean(v * v, axis=-1, keepdims=True) + EPS)


def _rms_bwd(xv, g, dy):
    r = _rms_r(xv)
    xh = xv * r
    dg = jnp.sum(dy * xh, axis=0, keepdims=True)
    dxh = dy * g
    dx = r * (dxh - xh * jnp.mean(dxh * xh, axis=-1, keepdims=True))
    return dx, dg


def _dot(a, b):
    return jnp.dot(a, b, preferred_element_type=F32)


def _dot_nt(a, b):
    return lax.dot_general(a, b, (((1,), (1,)), ((), ())), preferred_element_type=F32)


def _dot_tn(a, b):
    return lax.dot_general(a, b, (((0,), (0,)), ((), ())), preferred_element_type=F32)


class _Comm:
    def __init__(self, ins, out_shapes, aliases, nsem, start, finish, mid=None):
        self.ins, self.out_shapes, self.aliases, self.nsem = list(ins), list(out_shapes), dict(aliases), nsem
        self.start, self.finish, self.mid = start, finish, mid


def _call(body, comm, first, last, *, name, grid, in_specs, args, out_specs, out_shape, scratch, sem, mid_at=None):
    in_specs, args, out_specs, out_shape, scratch = map(list, (in_specs, args, out_specs, out_shape, scratch))
    n_in, n_out, n_scr = len(args), len(out_shape), len(scratch)
    aliases = {}
    kern = body
    if comm is not None:
        nci, nco = len(comm.ins), len(comm.out_shapes)
        o0 = n_in + nci
        s0 = o0 + n_out + nco

        def kern(*refs):
            cins, couts = refs[n_in:o0], refs[o0 + n_out:s0]
            ssem, rsem = refs[s0 + n_scr:]
            pl.when(first())(lambda: comm.start(cins, couts, ssem, rsem))
            if comm.mid is not None and mid_at is not None:
                pl.when(mid_at())(lambda: comm.mid(cins, couts, ssem, rsem))
            body(*refs[:n_in], *refs[o0:o0 + n_out], *refs[s0:s0 + n_scr])

            def end():
                if comm.mid is not None and mid_at is None:
                    comm.mid(cins, couts, ssem, rsem)
                comm.finish(cins, couts, ssem, rsem)

            pl.when(last())(end)

        in_specs += [ANY] * nci
        args += comm.ins
        out_specs += [ANY] * nco
        out_shape += comm.out_shapes
        scratch += [pltpu.SemaphoreType.DMA((comm.nsem,))] * 2
        aliases = {n_in + i: n_out + o for i, o in comm.aliases.items()}
    outs = pl.pallas_call(kern, name=name, grid=grid, in_specs=in_specs, out_specs=out_specs, out_shape=out_shape,
                          scratch_shapes=scratch, input_output_aliases=aliases,
                          compiler_params=_cparams(sem, VMEM_BIG))(*args)
    return outs[:n_out], outs[n_out:]


class _SemView:
    def __init__(self, sems, base):
        self.sems, self.base = sems, base

    @property
    def at(self):
        return self

    def __getitem__(self, k):
        return self.sems.at[self.base + k]


def _join(*comms):
    ins, outs, aliases, spans, nsem = [], [], {}, [], 0
    for cm in comms:
        aliases.update({len(ins) + i: len(outs) + o for i, o in cm.aliases.items()})
        spans.append((len(ins), len(ins) + len(cm.ins), len(outs), len(outs) + len(cm.out_shapes), nsem))
        ins += cm.ins
        outs += cm.out_shapes
        nsem += cm.nsem

    def run(which, cins, couts, ssem, rsem):
        for cm, (i0, i1, o0, o1, base) in zip(comms, spans):
            if getattr(cm, which) is not None:
                getattr(cm, which)(cins[i0:i1], couts[o0:o1], _SemView(ssem, base), _SemView(rsem, base))

    joined = _Comm(ins, outs, aliases, nsem, functools.partial(run, "start"), functools.partial(run, "finish"),
                   functools.partial(run, "mid") if any(cm.mid is not None for cm in comms) else None)
    joined.spans = [(o0, o1) for _, _, o0, o1, _ in spans]
    return joined


def _comm_call(comm, name):
    once = lambda: pl.program_id(0) == 0
    return _call(lambda: None, comm, once, once, name=name, grid=(1,), in_specs=[], args=[], out_specs=[],
                 out_shape=[], scratch=[], sem=("arbitrary",))[1]


def _mm(a, b, mode, *, name, out_dtype=F32, res=None, a_g=False, comm=None):
    a2 = a.shape[1:] if a_g else a.shape
    if mode == "nn":
        (m, k), (k2, n) = a2, b.shape
    elif mode == "nt":
        (m, k), (n, k2) = a2, b.shape
    else:
        (k, m), (k2, n) = a2, b.shape
    assert k == k2, (name, a.shape, b.shape)
    g = a.shape[0] if a_g else 1
    tm = m if m <= 768 else _pick(m, (768, 512, 256, 128))
    tn = n if n <= 1280 else _pick(n, (1280, 1024, 768, 512, 256, 128))
    tk = k if k <= 1280 else _pick(k, (1280, 1024, 768, 512, 256, 128))
    fits = lambda kk, nn: 2 * (kk * tm * a.dtype.itemsize + kk * nn * b.dtype.itemsize + tm * nn * 4) <= VMEM_BIG - (8 << 20)
    if fits(k, n):
        tn, tk = n, k
    elif fits(k, tn):
        tk = k
    nk = k // tk
    dn = {"nn": (((1,), (0,)), ((), ())), "nt": (((1,), (1,)), ((), ())), "tn": (((0,), (0,)), ((), ()))}[mode]

    def body(*refs):
        a_ref, b_ref = refs[0], refs[1]
        res_ref = refs[2] if res is not None else None
        o_ref = refs[3] if res is not None else refs[2]
        p = lax.dot_general(a_ref[...].astype(BF16), b_ref[...].astype(BF16), dn, preferred_element_type=F32)

        def fin(v):
            if res_ref is not None:
                v = v + res_ref[...]
            o_ref[...] = v.astype(out_dtype)

        if nk == 1:
            fin(p)
        else:
            acc_ref = refs[-1]
            kk = pl.program_id(3)

            @pl.when(kk == 0)
            def _():
                acc_ref[...] = p

            @pl.when(kk > 0)
            def _():
                acc_ref[...] += p

            @pl.when(kk == nk - 1)
            def _():
                fin(acc_ref[...])

    a_block, a_idx = ((tk, tm), lambda gg, i, j, kk: (kk, i)) if mode == "tn" else ((tm, tk), lambda gg, i, j, kk: (i, kk))
    if a_g:
        a_spec = pl.BlockSpec((None,) + a_block, lambda gg, i, j, kk: (gg,) + a_idx(gg, i, j, kk))
    else:
        a_spec = pl.BlockSpec(a_block, a_idx)
    if mode == "nt":
        b_spec = pl.BlockSpec((tn, tk), lambda gg, i, j, kk: (j, kk))
    else:
        b_spec = pl.BlockSpec((tk, tn), lambda gg, i, j, kk: (kk, j))
    in_specs, args = [a_spec, b_spec], [a, b]
    if res is not None:
        in_specs.append(pl.BlockSpec((tm, tn), lambda gg, i, j, kk: (i, j)))
        args.append(res)
    if a_g:
        out_spec, oshape = pl.BlockSpec((None, tm, tn), lambda gg, i, j, kk: (gg, i, j)), (g, m, n)
    else:
        out_spec, oshape = pl.BlockSpec((tm, tn), lambda gg, i, j, kk: (i, j)), (m, n)
    grid = (g, m // tm, n // tn, nk)

    def at(corner):
        hit = pl.program_id(0) == corner[0]
        for ax in range(1, 4):
            hit = jnp.logical_and(hit, pl.program_id(ax) == corner[ax])
        return hit

    outs, couts = _call(body, comm, lambda: at((0, 0, 0, 0)), lambda: at(tuple(dim - 1 for dim in grid)), name=name,
                        grid=grid, in_specs=in_specs, args=args, out_specs=[out_spec], out_shape=[SDS(oshape, out_dtype)],
                        scratch=[pltpu.VMEM((tm, tn), F32)] if nk > 1 else [], sem=("arbitrary",) * 4)
    return outs[0] if comm is None else (outs[0], couts)


def _ffn_fwd(x, gains, l, wg, wu, wd, name, comm=None):
    t, d = x.shape
    f = wg.shape[-2]
    tt = _rt(t)

    def body(x_ref, g_ref, wg_ref, wu_ref, wd_ref, o_ref, gt_ref, up_ref, h_sc, acc_sc):
        j = pl.program_id(1)

        @pl.when(j == 0)
        def _():
            xv = x_ref[...]
            h_sc[...] = (xv * _rms_r(xv) * g_ref[l:l + 1, :]).astype(BF16)
            acc_sc[...] = jnp.zeros_like(acc_sc)

        for part in range(FFN_PARTS):
            rows = pl.ds(part * (tt // FFN_PARTS), tt // FFN_PARTS)
            h = h_sc[rows, :]
            gt = _dot_nt(h, wg_ref[...]).astype(BF16)
            up = _dot_nt(h, wu_ref[...]).astype(BF16)
            gt_ref[rows, :] = gt
            up_ref[rows, :] = up
            acc_sc[rows, :] += _dot(gt * _sig(gt) * up, wd_ref[...])

        @pl.when(j == NCHIP - 1)
        def _():
            o_ref[...] = x_ref[...] + 0.5 * acc_sc[...]

    wspec = pl.BlockSpec((None, f, d), lambda i, j: (j, 0, 0))
    row = pl.BlockSpec((tt, d), lambda i, j: (i, 0))
    sh = pl.BlockSpec((None, tt, f), lambda i, j: (j, i, 0))
    ni = t // tt
    first = lambda: jnp.logical_and(pl.program_id(0) == 0, pl.program_id(1) == 0)
    last = lambda: jnp.logical_and(pl.program_id(0) == ni - 1, pl.program_id(1) == NCHIP - 1)
    mid_at = lambda: pl.program_id(0) * NCHIP + pl.program_id(1) == (3 * ni * NCHIP) // 4
    outs, couts = _call(
        body, comm, first, last, name=name, grid=(ni, NCHIP), mid_at=mid_at,
        in_specs=[row, pl.BlockSpec(gains.shape, lambda i, j: (0, 0)), wspec, wspec, wspec], args=[x, gains, wg, wu, wd],
        out_specs=[row, sh, sh], out_shape=[SDS((t, d), F32), SDS((NCHIP, t, f), BF16), SDS((NCHIP, t, f), BF16)],
        scratch=[pltpu.VMEM((tt, d), BF16), pltpu.VMEM((tt, d), F32)], sem=("arbitrary", "arbitrary"))
    return (*outs, couts)


def _ffn_bwd(x, gains, l, wg, wu, wd, gts, ups, dout, name, comm=None):
    t, d = x.shape
    f = wg.shape[-2]
    tt = _rt(t)

    def body(x_ref, g_ref, wg_ref, wu_ref, wd_ref, gt_ref, up_ref, do_ref,
             dx_ref, dg_ref, dgt_ref, dup_ref, act_ref, h_ref, dob_ref, dh_sc):
        i, j = pl.program_id(0), pl.program_id(1)

        @pl.when(j == 0)
        def _():
            xv = x_ref[...]
            h_ref[...] = (xv * _rms_r(xv) * g_ref[l:l + 1, :]).astype(BF16)
            dob_ref[...] = (0.5 * do_ref[...]).astype(BF16)
            dh_sc[...] = jnp.zeros_like(dh_sc)

        @pl.when(jnp.logical_and(i == 0, j == 0))
        def _():
            dg_ref[...] = jnp.zeros_like(dg_ref)

        for part in range(FFN_PARTS):
            rows = pl.ds(part * (tt // FFN_PARTS), tt // FFN_PARTS)
            dact = _dot_nt(dob_ref[rows, :], wd_ref[...]).astype(BF16)
            gt = gt_ref[rows, :]
            up = up_ref[rows, :]
            s = _sig(gt)
            sl = gt * s
            dup = dact * sl
            dgt = dact * up * (s + sl * (1.0 - s))
            dgt_ref[rows, :] = dgt
            dup_ref[rows, :] = dup
            act_ref[rows, :] = sl * up
            dh_sc[rows, :] += _dot(dgt, wg_ref[...]) + _dot(dup, wu_ref[...])

        @pl.when(j == NCHIP - 1)
        def _():
            dxn, dg = _rms_bwd(x_ref[...], g_ref[l:l + 1, :], dh_sc[...])
            dx_ref[...] = do_ref[...] + dxn
            dg_ref[...] += dg

    wspec = pl.BlockSpec((None, f, d), lambda i, j: (j, 0, 0))
    row = pl.BlockSpec((tt, d), lambda i, j: (i, 0))
    sh = pl.BlockSpec((None, tt, f), lambda i, j: (j, i, 0))
    ni = t // tt
    first = lambda: jnp.logical_and(pl.program_id(0) == 0, pl.program_id(1) == 0)
    last = lambda: jnp.logical_and(pl.program_id(0) == ni - 1, pl.program_id(1) == NCHIP - 1)
    outs, couts = _call(
        body, comm, first, last, name=name, grid=(ni, NCHIP),
        in_specs=[row, pl.BlockSpec(gains.shape, lambda i, j: (0, 0)), wspec, wspec, wspec, sh, sh, row],
        args=[x, gains, wg, wu, wd, gts, ups, dout],
        out_specs=[row, pl.BlockSpec((1, d), lambda i, j: (0, 0)), sh, sh, sh, row, row],
        out_shape=[SDS((t, d), F32), SDS((1, d), F32)] + [SDS((NCHIP, t, f), BF16)] * 3 + [SDS((t, d), BF16)] * 2,
        scratch=[pltpu.VMEM((tt, d), F32)], sem=("arbitrary", "arbitrary"))
    return (*outs, couts)


def _rms_fwd(x, gains, l, name):
    t, d = x.shape
    tt = _rt(t)

    def body(x_ref, g_ref, o_ref):
        xv = x_ref[...]
        o_ref[...] = (xv * _rms_r(xv) * g_ref[l:l + 1, :]).astype(BF16)

    row = pl.BlockSpec((tt, d), lambda i: (i, 0))
    return pl.pallas_call(body, name=name, grid=(t // tt,), in_specs=[row, pl.BlockSpec(gains.shape, lambda i: (0, 0))],
                          out_specs=row, out_shape=SDS((t, d), BF16), compiler_params=_cparams(("arbitrary",)))(x, gains)


def _rms_back(x, gains, l, dh, dres, name):
    t, d = x.shape
    tt = _rt(t)

    def body(x_ref, g_ref, dh_ref, dr_ref, dx_ref, dg_ref):
        @pl.when(pl.program_id(0) == 0)
        def _():
            dg_ref[...] = jnp.zeros_like(dg_ref)

        dxn, dg = _rms_bwd(x_ref[...], g_ref[l:l + 1, :], dh_ref[...])
        dx_ref[...] = dr_ref[...] + dxn
        dg_ref[...] += dg

    row = pl.BlockSpec((tt, d), lambda i: (i, 0))
    return pl.pallas_call(body, name=name, grid=(t // tt,),
                          in_specs=[row, pl.BlockSpec(gains.shape, lambda i: (0, 0)), row, row],
                          out_specs=[row, pl.BlockSpec((1, d), lambda i: (0, 0))],
                          out_shape=[SDS((t, d), F32), SDS((1, d), F32)],
                          compiler_params=_cparams(("arbitrary",)))(x, gains, dh, dres)


def _dwconv_fwd(proj, convw, convb, l, o_a, cc, name):
    t = proj.shape[0]
    r = min(256, t)
    nb = cc // HP

    def body(a_ref, gate_ref, w_ref, b_ref, uc_ref, u_sc):
        u_sc[0:CPAD, :] = jnp.zeros((CPAD, HP), F32)

        def fill(ci, carry):
            r0 = pl.multiple_of(ci * r, r)
            u_sc[pl.ds(CPAD + r0, r), :] = a_ref[pl.ds(r0, r), :].astype(F32) * _sig(gate_ref[pl.ds(r0, r), :].astype(F32))
            return carry

        lax.fori_loop(0, t // r, fill, 0)
        w = w_ref[l]
        bias = b_ref[l:l + 1, :]

        def conv(ci, carry):
            r0 = pl.multiple_of(ci * r, r)
            win = u_sc[pl.ds(r0, r + CPAD), :]
            acc = jnp.zeros((r, HP), F32) + bias
            for k in range(KW):
                off = CPAD - (KW - 1) + k
                acc = acc + win[off:off + r, :] * w[k:k + 1, :]
            uc_ref[pl.ds(r0, r), :] = acc
            return carry

        lax.fori_loop(0, t // r, conv, 0)

    col = lambda base: pl.BlockSpec((t, HP), lambda c: (0, base // HP + c))
    return pl.pallas_call(
        body, name=name, grid=(nb,),
        in_specs=[col(o_a), col(o_a + cc), pl.BlockSpec((convw.shape[0], CPAD, HP), lambda c: (0, 0, c)),
                  pl.BlockSpec((convb.shape[0], HP), lambda c: (0, c))],
        out_specs=pl.BlockSpec((t, HP), lambda c: (0, c)), out_shape=SDS((t, cc), F32),
        scratch_shapes=[pltpu.VMEM((t + CPAD, HP), F32)],
        compiler_params=_cparams(("arbitrary",), VMEM_BIG))(proj, proj, convw, convb)


def _dwconv_bwd(duc, proj, convw, l, o_a, cc, name):
    t = proj.shape[0]
    r = min(256, t)
    nb = cc // HP

    def body(d_ref, a_ref, gate_ref, w_ref, du_ref, dw_ref, u_sc, d_sc, dw_sc):
        u_sc[0:CPAD, :] = jnp.zeros((CPAD, HP), F32)
        d_sc[t:t + CPAD, :] = jnp.zeros((CPAD, HP), F32)
        dw_sc[...] = jnp.zeros_like(dw_sc)

        def fill(ci, carry):
            r0 = pl.multiple_of(ci * r, r)
            u_sc[pl.ds(CPAD + r0, r), :] = a_ref[pl.ds(r0, r), :].astype(F32) * _sig(gate_ref[pl.ds(r0, r), :].astype(F32))
            d_sc[pl.ds(r0, r), :] = d_ref[pl.ds(r0, r), :]
            return carry

        lax.fori_loop(0, t // r, fill, 0)
        w = w_ref[l]

        def conv(ci, carry):
            r0 = pl.multiple_of(ci * r, r)
            dwin = d_sc[pl.ds(r0, r + CPAD), :]
            uwin = u_sc[pl.ds(r0, r + CPAD), :]
            dcur = dwin[0:r, :]
            acc = jnp.zeros((r, HP), F32)
            for k in range(KW):
                acc = acc + dwin[KW - 1 - k:KW - 1 - k + r, :] * w[k:k + 1, :]
                off = CPAD - (KW - 1) + k
                part = (dcur * uwin[off:off + r, :]).reshape(r // 8, 8, HP).sum(axis=0)
                dw_sc[8 * k:8 * k + 8, :] += part
            du_ref[pl.ds(r0, r), :] = acc
            return carry

        lax.fori_loop(0, t // r, conv, 0)
        dw_ref[...] = jnp.zeros_like(dw_ref)
        for k in range(KW):
            dw_ref[k:k + 1, :] = jnp.sum(dw_sc[8 * k:8 * k + 8, :], axis=0, keepdims=True)

    col = lambda base: pl.BlockSpec((t, HP), lambda c: (0, base // HP + c))
    return pl.pallas_call(
        body, name=name, grid=(nb,),
        in_specs=[pl.BlockSpec((t, HP), lambda c: (0, c)), col(o_a), col(o_a + cc),
                  pl.BlockSpec((convw.shape[0], CPAD, HP), lambda c: (0, 0, c))],
        out_specs=[pl.BlockSpec((t, HP), lambda c: (0, c)), pl.BlockSpec((CPAD, HP), lambda c: (0, c))],
        out_shape=[SDS((t, cc), F32), SDS((CPAD, cc), F32)],
        scratch_shapes=[pltpu.VMEM((t + CPAD, HP), F32), pltpu.VMEM((t + CPAD, HP), F32), pltpu.VMEM((8 * CPAD, HP), F32)],
        compiler_params=_cparams(("arbitrary",), VMEM_BIG))(duc, proj, proj, convw)


def _ln_parts(uc, g, b):
    mu = jnp.mean(uc, axis=-1, keepdims=True)
    xc = uc - mu
    r = lax.rsqrt(jnp.mean(xc * xc, axis=-1, keepdims=True) + EPS)
    xh = xc * r
    return r, xh, xh * g + b


def _ln_silu(uc, ln_g, ln_b, l, name):
    t, cc = uc.shape
    tt = _rt(t)

    def body(u_ref, g_ref, b_ref, s_ref):
        _, _, yv = _ln_parts(u_ref[...], g_ref[l:l + 1, :], b_ref[l:l + 1, :])
        s_ref[...] = (yv * _sig(yv)).astype(BF16)

    row = pl.BlockSpec((tt, cc), lambda i: (i, 0))
    full = pl.BlockSpec(ln_g.shape, lambda i: (0, 0))
    return pl.pallas_call(body, name=name, grid=(t // tt,), in_specs=[row, full, full], out_specs=row,
                          out_shape=SDS((t, cc), BF16), compiler_params=_cparams(("arbitrary",)))(uc, ln_g, ln_b)


def _ln_silu_bwd(uc, ln_g, ln_b, l, ds, name):
    t, cc = uc.shape
    tt = _rt(t)

    def body(u_ref, g_ref, b_ref, ds_ref, du_ref, dg_ref, db_ref, dcb_ref):
        @pl.when(pl.program_id(0) == 0)
        def _():
            dg_ref[...] = jnp.zeros_like(dg_ref)
            db_ref[...] = jnp.zeros_like(db_ref)
            dcb_ref[...] = jnp.zeros_like(dcb_ref)

        g = g_ref[l:l + 1, :]
        r, xh, yv = _ln_parts(u_ref[...], g, b_ref[l:l + 1, :])
        sy = _sig(yv)
        dy = ds_ref[...] * (sy * (1.0 + yv * (1.0 - sy)))
        dg_ref[...] += jnp.sum(dy * xh, axis=0, keepdims=True)
        db_ref[...] += jnp.sum(dy, axis=0, keepdims=True)
        dxh = dy * g
        du = r * (dxh - jnp.mean(dxh, axis=-1, keepdims=True) - xh * jnp.mean(dxh * xh, axis=-1, keepdims=True))
        du_ref[...] = du
        dcb_ref[...] += jnp.sum(du, axis=0, keepdims=True)

    row = pl.BlockSpec((tt, cc), lambda i: (i, 0))
    full = pl.BlockSpec(ln_g.shape, lambda i: (0, 0))
    vec = pl.BlockSpec((1, cc), lambda i: (0, 0))
    return pl.pallas_call(body, name=name, grid=(t // tt,), in_specs=[row, full, full, row],
                          out_specs=[row, vec, vec, vec], out_shape=[SDS((t, cc), F32)] + [SDS((1, cc), F32)] * 3,
                          compiler_params=_cparams(("arbitrary",)))(uc, ln_g, ln_b, ds)


def _glu_bwd(du, proj, dproj, o_a, cc, name):
    t = du.shape[0]
    tt = _rt(t)

    def body(du_ref, a_ref, gate_ref, prev_ref, o_ref):
        sg = _sig(gate_ref[...].astype(F32))
        dv = du_ref[...]
        o_ref[:, 0:cc] = (dv * sg).astype(BF16)
        o_ref[:, cc:2 * cc] = (dv * a_ref[...] * sg * (1.0 - sg)).astype(BF16)

    return pl.pallas_call(
        body, name=name, grid=(t // tt,),
        in_specs=[pl.BlockSpec((tt, cc), lambda i: (i, 0)), pl.BlockSpec((tt, cc), lambda i: (i, o_a // cc)),
                  pl.BlockSpec((tt, cc), lambda i: (i, o_a // cc + 1)), ANY],
        out_specs=pl.BlockSpec((tt, 2 * cc), lambda i: (i, o_a // (2 * cc))),
        out_shape=SDS(dproj.shape, dproj.dtype), input_output_aliases={3: 0},
        compiler_params=_cparams(("arbitrary",)))(du, proj, proj, dproj)


def _rope(v, cs, s1, s2):
    return v * cs + pltpu.roll(v, HP - ROPE // 2, 1) * s1 + pltpu.roll(v, ROPE // 2, 1) * s2


def _rope_t(dv, cs, s1, s2):
    return dv * cs + pltpu.roll(dv * s1, ROPE // 2, 1) + pltpu.roll(dv * s2, HP - ROPE // 2, 1)


def _head_norm(v, g):
    r = lax.rsqrt(jnp.sum(v * v, axis=-1, keepdims=True) * (1.0 / QK) + EPS)
    return v * r * g, r


def _head_norm_bwd(v, r, g, dy):
    xh = v * r
    dg = jnp.sum(dy * xh, axis=0, keepdims=True)
    dxh = dy * g
    dx = r * (dxh - xh * (jnp.sum(dxh * xh, axis=-1, keepdims=True) * (1.0 / QK)))
    return dx, dg


def _mla_specs(tt, lay, cq_norm, ckv_norm, qn, kn, wuq, wukv):
    ql, kvl = cq_norm.shape[1], ckv_norm.shape[1]
    full = lambda a: pl.BlockSpec(a.shape, lambda i: (0,) * a.ndim)
    tab = pl.BlockSpec((tt, HP), lambda i: (i, 0))
    return [pl.BlockSpec((tt, ql), lambda i: (i, lay["cq"] // ql)),
            pl.BlockSpec((tt, kvl), lambda i: (i, lay["ckv"] // kvl)),
            pl.BlockSpec((tt, HP), lambda i: (i, lay["kr"] // HP)),
            full(cq_norm), full(ckv_norm), full(qn), full(kn), full(wuq), full(wukv), tab, tab, tab]


def _mla_pre_fwd(proj, lay, l, cq_norm, ckv_norm, qn, kn, wuq, wukv, tabs, name):
    t = proj.shape[0]
    tt = _rt(t)
    hw = N_HEADS * HP

    def body(cq_ref, ckv_ref, kr_ref, gq_ref, gkv_ref, qn_ref, kn_ref, wq_ref, wkv_ref, c_ref, s1_ref, s2_ref,
             q_ref, k_ref, v_ref):
        cq = cq_ref[...].astype(F32)
        cqn = (cq * _rms_r(cq) * gq_ref[l:l + 1, :]).astype(BF16)
        ckv = ckv_ref[...].astype(F32)
        ckvn = (ckv * _rms_r(ckv) * gkv_ref[l:l + 1, :]).astype(BF16)
        qraw = _dot_nt(cqn, wq_ref[...])
        kv = _dot(ckvn, wkv_ref[...])
        v_ref[...] = kv.astype(BF16)
        lane = lax.broadcasted_iota(jnp.int32, (tt, HP), 1)
        krs = pltpu.roll(jnp.where(lane < ROPE, kr_ref[...].astype(F32), 0.0), NOPE, 1)
        cs, s1, s2 = c_ref[...], s1_ref[...], s2_ref[...]
        gq, gk = qn_ref[l:l + 1, :], kn_ref[l:l + 1, :]
        for h in range(N_HEADS):
            sl = slice(h * HP, (h + 1) * HP)
            qh, _ = _head_norm(qraw[:, sl], gq)
            q_ref[:, sl] = (_rope(qh, cs, s1, s2) * QK ** -0.5).astype(BF16)
            kh, _ = _head_norm(jnp.where(lane < NOPE, kv[:, sl], krs), gk)
            k_ref[:, sl] = _rope(kh, cs, s1, s2).astype(BF16)

    row = pl.BlockSpec((tt, hw), lambda i: (i, 0))
    return pl.pallas_call(
        body, name=name, grid=(t // tt,),
        in_specs=_mla_specs(tt, lay, cq_norm, ckv_norm, qn, kn, wuq, wukv),
        out_specs=[row, row, row], out_shape=[SDS((t, hw), BF16)] * 3,
        compiler_params=_cparams(("arbitrary",), VMEM_BIG))(
            proj, proj, proj, cq_norm, ckv_norm, qn, kn, wuq, wukv, *tabs)


def _mla_pre_bwd(proj, lay, l, cq_norm, ckv_norm, qn, kn, wuq, wukv, tabs, dq, dk, dv, dproj, name):
    t = proj.shape[0]
    tt = _rt(t)
    hw = N_HEADS * HP
    ql, kvl = cq_norm.shape[1], ckv_norm.shape[1]
    wm = lay["wm"]

    def body(cq_ref, ckv_ref, kr_ref, gq_ref, gkv_ref, qn_ref, kn_ref, wq_ref, wkv_ref, c_ref, s1_ref, s2_ref,
             dq_ref, dk_ref, dv_ref, prev_ref,
             o_ref, dqr_ref, dkv_ref, cqn_ref, ckvn_ref, dgq_ref, dgkv_ref, dqn_ref, dkn_ref):
        @pl.when(pl.program_id(0) == 0)
        def _():
            for ref in (dgq_ref, dgkv_ref, dqn_ref, dkn_ref):
                ref[...] = jnp.zeros_like(ref)

        cq = cq_ref[...].astype(F32)
        cqn = (cq * _rms_r(cq) * gq_ref[l:l + 1, :]).astype(BF16)
        ckv = ckv_ref[...].astype(F32)
        ckvn = (ckv * _rms_r(ckv) * gkv_ref[l:l + 1, :]).astype(BF16)
        cqn_ref[...] = cqn
        ckvn_ref[...] = ckvn
        qraw = _dot_nt(cqn, wq_ref[...])
        kv = _dot(ckvn, wkv_ref[...])
        lane = lax.broadcasted_iota(jnp.int32, (tt, HP), 1)
        krs = pltpu.roll(jnp.where(lane < ROPE, kr_ref[...].astype(F32), 0.0), NOPE, 1)
        cs, s1, s2 = c_ref[...], s1_ref[...], s2_ref[...]
        gq, gk = qn_ref[l:l + 1, :], kn_ref[l:l + 1, :]
        dkr = jnp.zeros((tt, HP), F32)
        dgq = jnp.zeros((1, HP), F32)
        dgk = jnp.zeros((1, HP), F32)
        for h in range(N_HEADS):
            sl = slice(h * HP, (h + 1) * HP)
            qh = qraw[:, sl]
            _, rq = _head_norm(qh, gq)
            dqh, dg = _head_norm_bwd(qh, rq, gq, _rope_t(dq_ref[:, sl] * QK ** -0.5, cs, s1, s2))
            dgq = dgq + dg
            dqr_ref[:, sl] = dqh.astype(BF16)
            kp = jnp.where(lane < NOPE, kv[:, sl], krs)
            _, rk = _head_norm(kp, gk)
            dkp, dg = _head_norm_bwd(kp, rk, gk, _rope_t(dk_ref[:, sl], cs, s1, s2))
            dgk = dgk + dg
            dkv_ref[:, sl] = (jnp.where(lane < NOPE, dkp, 0.0) + dv_ref[:, sl]).astype(BF16)
            dkr = dkr + dkp
        dqn_ref[...] += dgq
        dkn_ref[...] += dgk
        dkr = jnp.where(lane < ROPE, pltpu.roll(dkr, HP - NOPE, 1), 0.0)
        dcq, dg = _rms_bwd(cq, gq_ref[l:l + 1, :], _dot(dqr_ref[...], wq_ref[...]))
        dgq_ref[...] += dg
        dckv, dg = _rms_bwd(ckv, gkv_ref[l:l + 1, :], _dot_nt(dkv_ref[...], wkv_ref[...]))
        dgkv_ref[...] += dg
        o_ref[:, 0:ql] = dcq.astype(BF16)
        o_ref[:, ql:ql + kvl] = dckv.astype(BF16)
        o_ref[:, ql + kvl:ql + kvl + HP] = dkr.astype(BF16)
        o_ref[:, ql + kvl + HP:wm] = jnp.zeros((tt, wm - ql - kvl - HP), BF16)

    row = pl.BlockSpec((tt, hw), lambda i: (i, 0))
    vec = lambda n: pl.BlockSpec((1, n), lambda i: (0, 0))
    return pl.pallas_call(
        body, name=name, grid=(t // tt,),
        in_specs=_mla_specs(tt, lay, cq_norm, ckv_norm, qn, kn, wuq, wukv) + [row, row, row, ANY],
        out_specs=[pl.BlockSpec((tt, wm), lambda i: (i, lay["cq"] // wm)), row, row,
                   pl.BlockSpec((tt, ql), lambda i: (i, 0)), pl.BlockSpec((tt, kvl), lambda i: (i, 0)),
                   vec(ql), vec(kvl), vec(HP), vec(HP)],
        out_shape=[SDS(dproj.shape, dproj.dtype), SDS((t, hw), BF16), SDS((t, hw), BF16), SDS((t, ql), BF16),
                   SDS((t, kvl), BF16), SDS((1, ql), F32), SDS((1, kvl), F32), SDS((1, HP), F32), SDS((1, HP), F32)],
        input_output_aliases={15: 0},
        compiler_params=_cparams(("arbitrary",), VMEM_BIG))(
            proj, proj, proj, cq_norm, ckv_norm, qn, kn, wuq, wukv, *tabs, dq, dk, dv, dproj)


def _comb_fwd(proj, gate_bias, l, yc, ym, name):
    t, d = yc.shape
    tt = _rt(t)

    def body(p_ref, b_ref, yc_ref, ym_ref, y_ref):
        b = b_ref[l]
        g0 = _sig(p_ref[:, 0:d] + b[0:1, :])
        g1 = _sig(p_ref[:, d:2 * d] + b[1:2, :])
        y_ref[...] = (g0 * yc_ref[...] + g1 * ym_ref[...]).astype(BF16)

    row = pl.BlockSpec((tt, d), lambda i: (i, 0))
    return pl.pallas_call(
        body, name=name, grid=(t // tt,),
        in_specs=[pl.BlockSpec((tt, 2 * d), lambda i: (i, 0)), pl.BlockSpec(gate_bias.shape, lambda i: (0, 0, 0)), row, row],
        out_specs=row, out_shape=SDS((t, d), BF16), compiler_params=_cparams(("arbitrary",)))(proj, gate_bias, yc, ym)


def _comb_bwd(proj, gate_bias, l, yc, ym, dy, dp_cols, name):
    t, d = yc.shape
    tt = _rt(t)

    def body(p_ref, b_ref, yc_ref, ym_ref, dy_ref, dyc_ref, dym_ref, dp_ref, db_ref):
        @pl.when(pl.program_id(0) == 0)
        def _():
            db_ref[...] = jnp.zeros_like(db_ref)

        b = b_ref[l]
        dyv = dy_ref[...].astype(F32)
        g0 = _sig(p_ref[:, 0:d] + b[0:1, :])
        g1 = _sig(p_ref[:, d:2 * d] + b[1:2, :])
        dyc_ref[...] = (dyv * g0).astype(BF16)
        dym_ref[...] = (dyv * g1).astype(BF16)
        dg0 = dyv * yc_ref[...] * g0 * (1.0 - g0)
        dg1 = dyv * ym_ref[...] * g1 * (1.0 - g1)
        dp_ref[:, 0:d] = dg0.astype(BF16)
        dp_ref[:, d:2 * d] = dg1.astype(BF16)
        db_ref[0:1, :] += jnp.sum(dg0, axis=0, keepdims=True)
        db_ref[1:2, :] += jnp.sum(dg1, axis=0, keepdims=True)

    row = pl.BlockSpec((tt, d), lambda i: (i, 0))
    wide = pl.BlockSpec((tt, 2 * d), lambda i: (i, 0))
    return pl.pallas_call(
        body, name=name, grid=(t // tt,),
        in_specs=[wide, pl.BlockSpec(gate_bias.shape, lambda i: (0, 0, 0)), row, row, row],
        out_specs=[row, row, wide, pl.BlockSpec((2, d), lambda i: (0, 0))],
        out_shape=[SDS((t, d), BF16), SDS((t, d), BF16), SDS((t, dp_cols), BF16), SDS((2, d), F32)],
        compiler_params=_cparams(("arbitrary",)))(proj, gate_bias, yc, ym, dy)


def _loss_grad(y, target, name):
    t, d = y.shape
    tt = _rt(t)
    nt = t // tt

    def body(y_ref, t_ref, dy_ref, loss_ref, acc_sc):
        i = pl.program_id(0)

        @pl.when(i == 0)
        def _():
            acc_sc[...] = jnp.zeros_like(acc_sc)

        diff = y_ref[...] - t_ref[...]
        dy_ref[...] = diff * (1.0 / d)
        acc_sc[...] += jnp.sum(diff * diff, axis=0, keepdims=True)

        @pl.when(i == nt - 1)
        def _():
            tot = jnp.sum(acc_sc[...], axis=1, keepdims=True) * (0.5 / d)
            loss_ref[...] = jnp.broadcast_to(tot, (1, HP))

    row = pl.BlockSpec((tt, d), lambda i: (i, 0))
    return pl.pallas_call(body, name=name, grid=(nt,), in_specs=[row, row],
                          out_specs=[row, pl.BlockSpec((1, HP), lambda i: (0, 0))],
                          out_shape=[SDS((t, d), F32), SDS((1, HP), F32)],
                          scratch_shapes=[pltpu.VMEM((1, d), F32)],
                          compiler_params=_cparams(("arbitrary",)))(y, target)


def _chunk_mask(tq):
    rows = lax.broadcasted_iota(jnp.int32, (tq, tq), 0) // CHUNK
    cols = lax.broadcasted_iota(jnp.int32, (tq, tq), 1) // CHUNK
    return cols <= rows


NEG = -1e30


def _flash_fwd(q, k, v, name, comm=None):
    t = q.shape[0]
    tq = _rt(t)
    nq = t // tq
    rep = tq // HP

    def body(q_ref, k_ref, v_ref, o_ref, lse_ref):
        qi = pl.program_id(1)

        def one(hh, ki, carry, masked):
            m_prev, l_prev, acc = carry
            lanes = slice(hh * HP, (hh + 1) * HP)
            r0 = pl.multiple_of(ki * tq, tq)
            s = _dot_nt(q_ref[:, lanes], k_ref[pl.ds(r0, tq), lanes])
            if masked:
                s = jnp.where(_chunk_mask(tq), s, NEG)
            m_new = jnp.maximum(m_prev, jnp.max(s, axis=-1, keepdims=True))
            a = jnp.exp(m_prev - m_new)
            p = jnp.exp(s - jnp.tile(m_new, (1, rep)))
            l_new = a * l_prev + jnp.sum(p, axis=-1, keepdims=True)
            acc = a * acc + _dot(p.astype(BF16), v_ref[pl.ds(r0, tq), lanes])
            return m_new, l_new, acc

        def step(ki, carries, masked):
            return tuple(one(hh, ki, carries[hh], masked) for hh in range(FLASH_HEADS))

        init = (jnp.full((tq, HP), NEG, F32), jnp.zeros((tq, HP), F32), jnp.zeros((tq, HP), F32))
        carries = lax.fori_loop(0, qi, lambda ki, cr: step(ki, cr, False), (init,) * FLASH_HEADS)
        for hh, (m_fin, l_fin, acc) in enumerate(step(qi, carries, True)):
            o_ref[:, hh * HP:(hh + 1) * HP] = (acc / l_fin).astype(BF16)
            lse_ref[hh] = m_fin + jnp.log(l_fin)

    wide = FLASH_HEADS * HP
    ng = N_HEADS // FLASH_HEADS
    qspec = pl.BlockSpec((tq, wide), lambda h, qi: (qi, h))
    head = pl.BlockSpec((t, wide), lambda h, qi: (0, h))
    first = lambda: jnp.logical_and(pl.program_id(0) == 0, pl.program_id(1) == 0)
    last = lambda: jnp.logical_and(pl.program_id(0) == ng - 1, pl.program_id(1) == nq - 1)
    mid_at = lambda: pl.program_id(0) * nq + pl.program_id(1) == (7 * ng * nq) // 8
    outs, couts = _call(
        body, comm, first, last, name=name, grid=(ng, nq), mid_at=mid_at, in_specs=[qspec, head, head], args=[q, k, v],
        out_specs=[qspec, pl.BlockSpec((FLASH_HEADS, tq, HP), lambda h, qi: (h, qi, 0))],
        out_shape=[SDS(q.shape, BF16), SDS((N_HEADS, t, HP), F32)], scratch=[], sem=("arbitrary",) * 2)
    return (*outs, couts)


def _flash_bwd(q, k, v, do, o, lse, name, comm=None):
    t = q.shape[0]
    tq = _rt(t)
    nq = t // tq
    rep = tq // HP

    def body(q_ref, k_ref, v_ref, do_ref, o_ref, lse_ref, dq_ref, dk_ref, dv_ref, delta_sc):
        def prep(qi, carry):
            r0 = pl.multiple_of(qi * tq, tq)
            rows = pl.ds(r0, tq)
            dlt = jnp.sum(do_ref[rows, :].astype(F32) * o_ref[rows, :].astype(F32), axis=-1, keepdims=True)
            delta_sc[rows, :] = jnp.broadcast_to(dlt, (tq, HP))
            dq_ref[rows, :] = jnp.zeros((tq, HP), F32)
            return carry

        lax.fori_loop(0, nq, prep, 0)

        def keys(ki, carry0):
            krows = pl.ds(pl.multiple_of(ki * tq, tq), tq)
            kt, vt = k_ref[krows, :], v_ref[krows, :]

            def step(qi, carry, masked):
                dk_acc, dv_acc = carry
                rows = pl.ds(pl.multiple_of(qi * tq, tq), tq)
                qt, dot_ = q_ref[rows, :], do_ref[rows, :]
                s = _dot_nt(qt, kt)
                if masked:
                    s = jnp.where(_chunk_mask(tq), s, NEG)
                p = jnp.exp(s - jnp.tile(lse_ref[rows, :], (1, rep)))
                ds = (p * (_dot_nt(dot_, vt) - jnp.tile(delta_sc[rows, :], (1, rep)))).astype(BF16)
                dv_acc = dv_acc + _dot_tn(p.astype(BF16), dot_)
                dk_acc = dk_acc + _dot_tn(ds, qt)
                dq_ref[rows, :] += _dot(ds, kt)
                return dk_acc, dv_acc

            zero = jnp.zeros((tq, HP), F32)
            carry = step(ki, (zero, zero), True)
            dk_acc, dv_acc = lax.fori_loop(ki + 1, nq, lambda qi, cr: step(qi, cr, False), carry)
            dk_ref[krows, :] = dk_acc
            dv_ref[krows, :] = dv_acc
            return carry0

        lax.fori_loop(0, nq, keys, 0)

    head = pl.BlockSpec((t, HP), lambda h: (0, h))
    outs, couts = _call(
        body, comm, lambda: pl.program_id(0) == 0, lambda: pl.program_id(0) == N_HEADS - 1, name=name, grid=(N_HEADS,),
        in_specs=[head] * 5 + [pl.BlockSpec((None, t, HP), lambda h: (h, 0, 0))], args=[q, k, v, do, o, lse],
        out_specs=[head] * 3, out_shape=[SDS(q.shape, F32)] * 3, scratch=[pltpu.VMEM((t, HP), F32)], sem=("arbitrary",))
    return (*outs, couts)


def _add_cast(gs, rs, place, name):
    n = len(gs)
    _, r, c = gs[0].shape
    tr, tc = _tile2(r, c // 2, 3 * n, 16)
    nct = c // 2 // tc

    def body(place_ref, *refs):
        for a in range(n):
            refs[2 * n + a][...] = (refs[a][...] + refs[n + a][...]).astype(BF16)

    gspec = pl.BlockSpec((None, tr, tc), lambda j, i, k, pr: (j, i, pr[0] * nct + k))
    rspec = pl.BlockSpec((None, tr, tc), lambda j, i, k, pr: (j, i, k))
    grid_spec = pltpu.PrefetchScalarGridSpec(num_scalar_prefetch=1, grid=(NCHIP, r // tr, nct),
                                             in_specs=[gspec] * n + [rspec] * n, out_specs=[rspec] * n)
    return pl.pallas_call(body, name=name, grid_spec=grid_spec, out_shape=[SDS((NCHIP, r, c // 2), BF16)] * n,
                          compiler_params=_cparams(("arbitrary",) * 3, VMEM_BIG))(place, *gs, *rs)


def _sum_chips(ss, qs, place, l, nl, prevs, name):
    n = len(ss)
    _, r, h = ss[0].shape
    tr, tc = _tile2(r, h, 3 * n, 16)
    nct = h // tc

    def body(place_ref, *refs):
        for a in range(n):
            acc = refs[n + a][...].astype(F32)
            for kk in range(NCHIP - 1):
                acc = acc + refs[a][kk].astype(F32)
            refs[-n + a][...] = acc

    in_specs = ([pl.BlockSpec((NCHIP - 1, tr, tc), lambda i, k, pr: (0, i, k))] * n
                + [pl.BlockSpec((None, tr, tc), lambda i, k, pr: (pr[1], i, k))] * n)
    args = [*ss, *qs]
    aliases = {}
    if prevs is not None:
        aliases = {1 + len(args) + a: a for a in range(n)}
        in_specs += [ANY] * n
        args += list(prevs)
    grid_spec = pltpu.PrefetchScalarGridSpec(
        num_scalar_prefetch=1, grid=(r // tr, nct), in_specs=in_specs,
        out_specs=[pl.BlockSpec((None, tr, tc), lambda i, k, pr: (l, i, pr[0] * nct + k))] * n)
    return pl.pallas_call(body, name=name, grid_spec=grid_spec, out_shape=[SDS((nl, r, 2 * h), F32)] * n,
                          input_output_aliases=aliases,
                          compiler_params=_cparams(("arbitrary",) * 2, VMEM_BIG))(place, *args)


def _cast_place(ws, place, name):
    n = len(ws)
    nl, r, c = ws[0].shape
    tr, tc = _tile2(r, c, 2 * n * nl, 16)

    def body(place_ref, *refs):
        for a in range(n * nl):
            refs[n * nl + a][...] = refs[a][...].astype(BF16)

    in_specs = [pl.BlockSpec((None, tr, tc), functools.partial(lambda l, i, k, pr: (l, i, k), l))
                for _ in range(n) for l in range(nl)]
    grid_spec = pltpu.PrefetchScalarGridSpec(
        num_scalar_prefetch=1, grid=(r // tr, c // tc), in_specs=in_specs,
        out_specs=[pl.BlockSpec((None, tr, tc), lambda i, k, pr: (pr[1], i, k))] * (n * nl))
    outs = pl.pallas_call(body, name=name, grid_spec=grid_spec, out_shape=[SDS((NCHIP, r, c), BF16)] * (n * nl),
                          compiler_params=_cparams(("arbitrary",) * 2, VMEM_BIG))(
                              place, *[w for w in ws for _ in range(nl)])
    return [outs[a * nl:(a + 1) * nl] for a in range(n)]


def _adamw(ws, gs, ms, vs, name):
    n = len(ws)
    r, c = ws[0].shape
    tr, tc = _tile2(r, c, 7 * n, 8)
    c1, c2 = 1.0 / (1.0 - B1 ** STEP), 1.0 / (1.0 - B2 ** STEP)

    def body(*refs):
        for a in range(n):
            w, g, m, v = (refs[kk * n + a][...] for kk in range(4))
            m2 = B1 * m + (1.0 - B1) * g
            v2 = B2 * v + (1.0 - B2) * (g * g)
            refs[4 * n + a][...] = -LR * ((m2 * c1) / (jnp.sqrt(v2 * c2) + EPS_ADAM) + WD * w)
            refs[5 * n + a][...] = m2
            refs[6 * n + a][...] = v2

    blk = pl.BlockSpec((tr, tc), lambda i, k: (i, k))
    outs = pl.pallas_call(body, name=name, grid=(r // tr, c // tc), in_specs=[blk] * (4 * n),
                          out_specs=[blk] * (3 * n), out_shape=[SDS((r, c), F32)] * (3 * n),
                          compiler_params=_cparams(("arbitrary",) * 2, VMEM_BIG))(*ws, *gs, *ms, *vs)
    return outs[:n], outs[n:2 * n], outs[2 * n:]


def _place():
    x, y, c = lax.axis_index("x"), lax.axis_index("y"), lax.axis_index("c")
    return x, y, c, [(1 - x, y), (x, 1 - y), (1 - x, 1 - y)]


def _rcopy(src, dst, ssem, rsem, k, dev):
    return pltpu.make_async_remote_copy(src_ref=src, dst_ref=dst, send_sem=ssem.at[k], recv_sem=rsem.at[k],
                                        device_id=dev, device_id_type=MESH)


def _half(ref, lead, cc):
    h = ref.shape[-1] // 2
    return ref.at[(*lead, slice(None), pl.ds(cc * h, h))]


def _per_core(fn):
    c = lax.axis_index("c")
    for cc in (0, 1):
        pl.when(c == cc)(functools.partial(fn, cc))


def _gather_comm(bufs):
    n = len(bufs)

    def plan(couts, ssem, rsem, cc):
        x, y, _, peers = _place()
        me, sib = 2 * x + y, (x, y, 1 - cc)
        send, recv, fwd, recv2 = [], [], [], []
        for a in range(n):
            for kk, (px, py) in enumerate(peers):
                mine = _half(couts[a], (me,), cc)
                got = _half(couts[a], (2 * px + py,), cc)
                other = _half(couts[a], (2 * px + py,), 1 - cc)
                send.append(_rcopy(mine, mine, ssem, rsem, a * 6 + kk, (px, py, cc)))
                recv.append(_rcopy(got, got, ssem, rsem, a * 6 + kk, (px, py, cc)))
                fwd.append(_rcopy(got, got, ssem, rsem, a * 6 + 3 + kk, sib))
                recv2.append(_rcopy(other, other, ssem, rsem, a * 6 + 3 + kk, sib))
        return send, recv, fwd, recv2

    def start(cins, couts, ssem, rsem):
        def go(cc):
            for d in plan(couts, ssem, rsem, cc)[0]:
                d.start()

        _per_core(go)

    def mid(cins, couts, ssem, rsem):
        def go(cc):
            _, recv, fwd, _ = plan(couts, ssem, rsem, cc)
            for dr, df in zip(recv, fwd):
                dr.wait_recv()
                df.start()

        _per_core(go)

    def finish(cins, couts, ssem, rsem):
        def go(cc):
            send, _, fwd, recv2 = plan(couts, ssem, rsem, cc)
            for d in recv2:
                d.wait_recv()
            for d in send + fwd:
                d.wait_send()

        _per_core(go)

    return _Comm(bufs, [SDS(b.shape, b.dtype) for b in bufs], {a: a for a in range(n)}, 6 * n, start, finish, mid)


def _pair_comm(gs):
    n = len(gs)
    halves = [g.shape[-1] // 2 for g in gs]

    def plan(cins, couts, ssem, rsem, cc):
        x, y, _, _ = _place()
        return [_rcopy(cins[a].at[:, :, pl.ds((1 - cc) * halves[a], halves[a])], couts[a], ssem, rsem, a, (x, y, 1 - cc))
                for a in range(n)]

    def start(cins, couts, ssem, rsem):
        def go(cc):
            for d in plan(cins, couts, ssem, rsem, cc):
                d.start()

        _per_core(go)

    def finish(cins, couts, ssem, rsem):
        def go(cc):
            ds = plan(cins, couts, ssem, rsem, cc)
            for d in ds:
                d.wait_recv()
            for d in ds:
                d.wait_send()

        _per_core(go)

    return _Comm(gs, [SDS(g.shape[:-1] + (g.shape[-1] // 2,), g.dtype) for g in gs], {}, n, start, finish)


def _chips_comm(qs):
    n = len(qs)

    def plan(cins, couts, ssem, rsem):
        x, y, c, peers = _place()
        return [_rcopy(cins[a].at[2 * px + py], couts[a].at[kk], ssem, rsem, a * 3 + kk, (px, py, c))
                for a in range(n) for kk, (px, py) in enumerate(peers)]

    def start(cins, couts, ssem, rsem):
        for d in plan(cins, couts, ssem, rsem):
            d.start()

    def finish(cins, couts, ssem, rsem):
        ds = plan(cins, couts, ssem, rsem)
        for d in ds:
            d.wait_recv()
        for d in ds:
            d.wait_send()

    return _Comm(qs, [SDS((NCHIP - 1,) + q.shape[1:], q.dtype) for q in qs], {}, 3 * n, start, finish)


def _share_comm(fs, l):
    n = len(fs)

    def plan(couts, ssem, rsem, cc, which):
        x, y, _, _ = _place()
        out = []
        for a in range(n):
            piece = _half(couts[a], (l,), which)
            out.append(_rcopy(piece, piece, ssem, rsem, a, (x, y, 1 - cc)))
        return out

    def start(cins, couts, ssem, rsem):
        def go(cc):
            for d in plan(couts, ssem, rsem, cc, cc):
                d.start()

        _per_core(go)

    def finish(cins, couts, ssem, rsem):
        def go(cc):
            for d in plan(couts, ssem, rsem, cc, 1 - cc):
                d.wait_recv()
            for d in plan(couts, ssem, rsem, cc, cc):
                d.wait_send()

        _per_core(go)

    return _Comm(fs, [SDS(f.shape, f.dtype) for f in fs], {a: a for a in range(n)}, n, start, finish)


def _exchange(buf, reduce, name):
    r, w = buf.shape

    def body(in_ref, out_ref, recv_sc, ssem, rsem):
        x, y, c, _ = _place()
        me = 4 * x + 2 * y + c
        sends = []
        for rel in range(1, N_DEV):
            dev = (1 - x if rel & 4 else x, 1 - y if rel & 2 else y, 1 - c if rel & 1 else c)
            d = _rcopy(in_ref, recv_sc.at[me], ssem, rsem, rel - 1, dev)
            d.start()
            sends.append(d)
        recv_sc[me] = in_ref[...]
        for rel in range(1, N_DEV):
            px, py, pc = (1 - x if rel & 4 else x, 1 - y if rel & 2 else y, 1 - c if rel & 1 else c)
            slot = recv_sc.at[4 * px + 2 * py + pc]
            _rcopy(slot, slot, ssem, rsem, rel - 1, (px, py, pc)).wait_recv()
        for d in sends:
            d.wait_send()
        if reduce:
            acc = recv_sc[0]
            for dv in range(1, N_DEV):
                acc = acc + recv_sc[dv]
            out_ref[...] = acc
        else:
            out_ref[...] = recv_sc[...]

    vm = pl.BlockSpec(memory_space=pltpu.VMEM)
    return pl.pallas_call(
        body, name=name, in_specs=[vm], out_specs=vm,
        out_shape=SDS((r, w) if reduce else (N_DEV, r, w), F32),
        scratch_shapes=[pltpu.VMEM((N_DEV, r, w), F32), pltpu.SemaphoreType.DMA((N_DEV - 1,)),
                        pltpu.SemaphoreType.DMA((N_DEV - 1,))],
        compiler_params=_cparams(None, VMEM_BIG))(buf)


def _cols_full(g):
    _, k, ns = g.shape
    return g.transpose(1, 0, 2).reshape(k, NCHIP * ns)


def _cols_shards(w):
    k, n = w.shape
    return w.reshape(k, NCHIP, n // NCHIP).transpose(1, 0, 2)


def _rows_pad(rows, width):
    out = jnp.concatenate([jnp.pad(a, ((0, 0), (0, width - a.shape[1]))) for a in rows], axis=0)
    return jnp.pad(out, ((0, -out.shape[0] % 8), (0, 0)))


def kernel(x, positions, ffn1_norm, ffn1_w_gate, ffn1_w_up, ffn1_w_down, mix_norm, w_in, gate_bias, conv_w, conv_b, conv_ln_g, conv_ln_b, w_conv_out, cq_norm, ckv_norm, w_uq, w_ukv, q_norm, k_norm, w_mla_out, w_out, ffn2_norm, ffn2_w_gate, ffn2_w_up, ffn2_w_down, loss_target, m_ffn1_norm, m_ffn1_w_gate, m_ffn1_w_up, m_ffn1_w_down, m_mix_norm, m_w_in, m_gate_bias, m_conv_w, m_conv_b, m_conv_ln_g, m_conv_ln_b, m_w_conv_out, m_cq_norm, m_ckv_norm, m_w_uq, m_w_ukv, m_q_norm, m_k_norm, m_w_mla_out, m_w_out, m_ffn2_norm, m_ffn2_w_gate, m_ffn2_w_up, m_ffn2_w_down, v_ffn1_norm, v_ffn1_w_gate, v_ffn1_w_up, v_ffn1_w_down, v_mix_norm, v_w_in, v_gate_bias, v_conv_w, v_conv_b, v_conv_ln_g, v_conv_ln_b, v_w_conv_out, v_cq_norm, v_ckv_norm, v_w_uq, v_w_ukv, v_q_norm, v_k_norm, v_w_mla_out, v_w_out, v_ffn2_norm, v_ffn2_w_gate, v_ffn2_w_up, v_ffn2_w_down):
    names = ["ffn1_norm", "ffn1_w_gate", "ffn1_w_up", "ffn1_w_down", "mix_norm", "w_in", "gate_bias", "conv_w",
             "conv_b", "conv_ln_g", "conv_ln_b", "w_conv_out", "cq_norm", "ckv_norm", "w_uq", "w_ukv", "q_norm",
             "k_norm", "w_mla_out", "w_out", "ffn2_norm", "ffn2_w_gate", "ffn2_w_up", "ffn2_w_down"]
    env = dict(locals())
    turned = ("ffn1_w_gate", "ffn1_w_up", "ffn2_w_gate", "ffn2_w_up", "w_in", "w_uq")
    view = lambda nm, a: jnp.swapaxes(a, 1, 2) if nm in turned else a
    wts = {nm: view(nm, env[nm]) for nm in names}
    mom = {nm: view(nm, env["m_" + nm]) for nm in names}
    var = {nm: view(nm, env["v_" + nm]) for nm in names}

    t, d = x.shape[1], x.shape[2]
    nl = ffn1_norm.shape[0]
    cc = conv_b.shape[1]
    ql, kvl = cq_norm.shape[1], ckv_norm.shape[1]
    vw = N_HEADS * V_DIM
    hw = N_HEADS * HP
    lay = {"a": 2 * d, "cq": 2 * d + 2 * cc, "ckv": 2 * d + 2 * cc + ql, "kr": 2 * d + 2 * cc + ql + kvl,
           "wm": ql + kvl + 2 * HP}
    dp = lay["cq"] + lay["wm"]
    nat_g = 2 * cc + ql + kvl + ROPE
    assert lay["cq"] % lay["wm"] == 0 and lay["cq"] % ql == 0 and lay["ckv"] % kvl == 0 and lay["a"] % (2 * cc) == 0
    assert cc % HP == 0 and d % HP == 0 and t % (2 * CHUNK) == 0 and w_in.shape[2] * NCHIP == nat_g + 2 * d
    assert nl == 2

    x0, target = x[0], loss_target[0]
    chip = 2 * lax.axis_index("x") + lax.axis_index("y")
    place = jnp.stack([lax.axis_index("c"), chip]).astype(jnp.int32)

    inv_freq = ROPE_THETA ** (-jnp.arange(0, ROPE, 2, dtype=F32) / ROPE)
    ang = positions[0].astype(F32)[:, None] * inv_freq
    cos, sin, z = jnp.cos(ang), jnp.sin(ang), jnp.zeros((t, ROPE // 2), F32)
    tabs = (jnp.concatenate([jnp.ones((t, NOPE), F32), cos, cos, jnp.zeros((t, HP - QK), F32)], axis=1),
            jnp.concatenate([jnp.zeros((t, NOPE), F32), -sin, z, jnp.zeros((t, HP - QK), F32)], axis=1),
            jnp.concatenate([jnp.zeros((t, NOPE), F32), z, sin, jnp.zeros((t, HP - QK), F32)], axis=1))

    big = ["ffn1_w_gate", "ffn1_w_up", "ffn1_w_down", "ffn2_w_gate", "ffn2_w_up", "ffn2_w_down",
           "w_in", "w_conv_out", "w_uq", "w_ukv", "w_mla_out", "w_out"]
    ffn1_w, ffn2_w, mix_w = big[0:3], big[3:6], big[6:]
    bufs, like = {}, {}
    for nm in big:
        like.setdefault(wts[nm].shape, []).append(nm)
    for gi, grp in enumerate(like.values()):
        for nm, per_layer in zip(grp, _cast_place([wts[nm] for nm in grp], place, f"cast_place_{gi}")):
            for l in range(nl):
                bufs[nm, l] = per_layer[l]

    def gather(keys):
        return _gather_comm([bufs[key] for key in keys])

    def landed(keys, outs):
        for key, o in zip(keys, outs):
            bufs[key] = o

    chunk = lambda nms, l: [(nm, l) for nm in nms]
    first_keys = chunk(ffn1_w, 0)
    landed(first_keys, _comm_call(gather(first_keys), "gather_first"))
    ride = {("ffn1", 0): chunk(mix_w, 0), ("flash", 0): chunk(ffn2_w, 0) + chunk(ffn1_w, 1),
            ("ffn2", 0): chunk(mix_w, 1), ("ffn1", 1): chunk(ffn2_w, 1)}

    def riding(kind, l):
        keys = ride.get((kind, l))
        return keys or [], (None if keys is None else gather(keys))

    def mixer_weights(l):
        w_in_nat = bufs["w_in", l].reshape(nat_g + 2 * d, d)
        w_in_k = jnp.concatenate([w_in_nat[nat_g:], w_in_nat[:nat_g], jnp.zeros((dp - nat_g - 2 * d, d), BF16)], axis=0)
        w_uq_k = jnp.pad(bufs["w_uq", l].reshape(N_HEADS, QK, ql), ((0, 0), (0, HP - QK), (0, 0))).reshape(hw, ql)
        w_mo_k = jnp.pad(_cols_full(bufs["w_mla_out", l]).reshape(N_HEADS, V_DIM, d),
                         ((0, 0), (HP - V_DIM, 0), (0, 0))).reshape(hw, d)
        return dict(w_in=w_in_k, w_co=_cols_full(bufs["w_conv_out", l]), w_uq=w_uq_k,
                    w_ukv=_cols_full(bufs["w_ukv", l]), w_mo=w_mo_k, w_out=bufs["w_out", l].reshape(d, d))

    qn_k = jnp.pad(q_norm, ((0, 0), (0, HP - QK)))
    kn_k = jnp.pad(k_norm, ((0, 0), (0, HP - QK)))
    small = _rows_pad([gate_bias.reshape(nl * 2, d // NCHIP), conv_w.reshape(nl * KW, cc // NCHIP)], d)
    everyone = _exchange(small, False, "gather_small")[0::2]
    gb_k = everyone[:, :nl * 2, :d // NCHIP].reshape(NCHIP, nl, 2, d // NCHIP).transpose(1, 2, 0, 3).reshape(nl, 2, d)
    cw = everyone[:, nl * 2:nl * 2 + nl * KW, :cc // NCHIP].reshape(NCHIP, nl, KW, cc // NCHIP)
    cw_k = jnp.pad(cw.transpose(1, 2, 0, 3).reshape(nl, KW, cc), ((0, 0), (0, CPAD - KW), (0, 0)))

    saved, mixw = [], []
    xc = x0
    for l in range(nl):
        keys, comm = riding("ffn1", l)
        x1, gt1, up1, got = _ffn_fwd(xc, ffn1_norm, l, *[bufs[nm, l] for nm in ffn1_w], f"ffn1_fwd_{l}", comm)
        landed(keys, got)
        mw = mixer_weights(l)
        mixw.append(mw)
        hm = _rms_fwd(x1, mix_norm, l, f"mix_norm_{l}")
        proj = _mm(hm, mw["w_in"], "nt", name=f"proj_{l}", out_dtype=BF16)
        uc = _dwconv_fwd(proj, cw_k, conv_b, l, lay["a"], cc, f"dwconv_{l}")
        sc = _ln_silu(uc, conv_ln_g, conv_ln_b, l, f"conv_ln_{l}")
        yc = _mm(sc, mw["w_co"], "nn", name=f"conv_out_{l}", out_dtype=BF16)
        q, k, kv = _mla_pre_fwd(proj, lay, l, cq_norm, ckv_norm, qn_k, kn_k, mw["w_uq"], mw["w_ukv"], tabs, f"mla_pre_{l}")
        keys, comm = riding("flash", l)
        o, lse, got = _flash_fwd(q, k, kv, f"flash_{l}", comm)
        landed(keys, got)
        ym = _mm(o, mw["w_mo"], "nn", name=f"mla_out_{l}", out_dtype=BF16)
        yv = _comb_fwd(proj, gb_k, l, yc, ym, f"combine_{l}")
        x2 = _mm(yv, mw["w_out"], "nn", name=f"mix_out_{l}", res=x1)
        keys, comm = riding("ffn2", l)
        x3, gt2, up2, got = _ffn_fwd(x2, ffn2_norm, l, *[bufs[nm, l] for nm in ffn2_w], f"ffn2_fwd_{l}", comm)
        landed(keys, got)
        saved.append(dict(x0=xc, x1=x1, gt1=gt1, up1=up1, hm=hm, proj=proj, uc=uc, sc=sc, yc=yc, q=q, k=k, kv=kv,
                          o=o, lse=lse, ym=ym, yv=yv, x2=x2, gt2=gt2, up2=up2))
        xc = x3
    dx, loss_row = _loss_grad(xc, target, "loss")

    gw = {nm: [None] * nl for nm in names}
    red = {}

    def shard_groups(nms, l):
        same = {}
        for nm in nms:
            same.setdefault(gw[nm][l].shape, []).append(nm)
        return list(same.values())

    def add_parts(nms, l, sib_part, tag):
        qb = {}
        for gi, grp in enumerate(shard_groups(nms, l)):
            outs = _add_cast([gw[nm][l] for nm in grp], [sib_part[nm] for nm in grp], place, f"rs_add_{tag}_{gi}")
            qb.update(zip(grp, outs))
        return qb

    def sum_parts(nms, l, qb, parts, tag):
        for gi, grp in enumerate(shard_groups(nms, l)):
            prevs = [red[nm] for nm in grp] if grp[0] in red else None
            outs = _sum_chips([parts[nm] for nm in grp], [qb[nm] for nm in grp], place, l, nl, prevs, f"rs_sum_{tag}_{gi}")
            red.update(zip(grp, outs))

    pair = lambda nms, l: _pair_comm([gw[nm][l] for nm in nms])
    chips = lambda nms, qb: _chips_comm([qb[nm] for nm in nms])
    share = lambda nms, l: _share_comm([red[nm] for nm in nms], l)

    def split(joined, got):
        return [got[o0:o1] for o0, o1 in joined.spans]

    mix_parts = [["w_in"], ["w_out", "w_conv_out"], ["w_mla_out", "w_ukv", "w_uq"]]

    for l in reversed(range(nl)):
        s, mw = saved[l], mixw[l]
        riding_rs = l == 0

        def ffn_back(tag, xin, gains, gts, ups, dout, comm=None):
            wg, wu, wd = (bufs[f"{tag}_w_{p}", l] for p in ("gate", "up", "down"))
            dxi, dgain, dgt, dup, act, hb, dob, got = _ffn_bwd(xin, gains, l, wg, wu, wd, gts, ups, dout,
                                                              f"{tag}_bwd_{l}", comm)
            for p, lhs, rhs in zip(("gate", "up", "down"), (dgt, dup, act), (hb, hb, dob)):
                gw[f"{tag}_w_{p}"][l] = _mm(lhs, rhs, "tn", name=f"{tag}_dw_{p}_{l}", a_g=True)
            gw[f"{tag}_norm"][l] = dgain
            return dxi, got

        dx2, got = ffn_back("ffn2", s["x2"], ffn2_norm, s["gt2"], s["up2"], dx, pair(big, 1) if riding_rs else None)
        if riding_rs:
            qb1 = add_parts(big, 1, dict(zip(big, got)), "l1")
        dyv = _mm(dx2, mw["w_out"], "nt", name=f"mix_out_dy_{l}", out_dtype=BF16)
        gw["w_out"][l] = _mm(s["yv"], dx2, "tn", name=f"mix_out_dw_{l}").reshape(NCHIP, d // NCHIP, d)
        dyc, dym, dproj, dgb = _comb_bwd(s["proj"], gb_k, l, s["yc"], s["ym"], dyv, dp, f"combine_bwd_{l}")
        gw["gate_bias"][l] = dgb
        dsc = _mm(dyc, mw["w_co"], "nt", name=f"conv_out_ds_{l}")
        gw["w_conv_out"][l] = _cols_shards(_mm(s["sc"], dyc, "tn", name=f"conv_out_dw_{l}"))
        duc, gw["conv_ln_g"][l], gw["conv_ln_b"][l], gw["conv_b"][l] = _ln_silu_bwd(
            s["uc"], conv_ln_g, conv_ln_b, l, dsc, f"conv_ln_bwd_{l}")
        du, dcw = _dwconv_bwd(duc, s["proj"], cw_k, l, lay["a"], cc, f"dwconv_bwd_{l}")
        gw["conv_w"][l] = dcw[:KW]
        dproj = _glu_bwd(du, s["proj"], dproj, lay["a"], cc, f"glu_bwd_{l}")
        do = _mm(dym, mw["w_mo"], "nt", name=f"mla_out_do_{l}", out_dtype=BF16)
        dwmo = _mm(s["o"], dym, "tn", name=f"mla_out_dw_{l}").reshape(N_HEADS, HP, d)[:, HP - V_DIM:].reshape(vw, d)
        gw["w_mla_out"][l] = _cols_shards(dwmo)
        comm = _join(chips(big, qb1), pair(ffn2_w, 0)) if riding_rs else None
        dq, dk, dv, got = _flash_bwd(s["q"], s["k"], s["kv"], do, s["o"], s["lse"], f"flash_bwd_{l}", comm)
        if riding_rs:
            parts1, sib_f2 = split(comm, got)
            sum_parts(big, 1, qb1, dict(zip(big, parts1)), "l1")
            qb_f2 = add_parts(ffn2_w, 0, dict(zip(ffn2_w, sib_f2)), "f2")
        dproj, dqr, dkv, cqn, ckvn, gw["cq_norm"][l], gw["ckv_norm"][l], dqn, dkn = _mla_pre_bwd(
            s["proj"], lay, l, cq_norm, ckv_norm, qn_k, kn_k, mw["w_uq"], mw["w_ukv"], tabs, dq, dk, dv, dproj,
            f"mla_pre_bwd_{l}")
        gw["q_norm"][l], gw["k_norm"][l] = dqn[:, :QK], dkn[:, :QK]
        dwuq = _mm(dqr, cqn, "tn", name=f"uq_dw_{l}").reshape(N_HEADS, HP, ql)[:, :QK]
        gw["w_uq"][l] = dwuq.reshape(NCHIP, N_HEADS * QK // NCHIP, ql)
        gw["w_ukv"][l] = _cols_shards(_mm(ckvn, dkv, "tn", name=f"ukv_dw_{l}"))
        dhm = _mm(dproj, mw["w_in"], "nn", name=f"proj_dh_{l}")
        dwin = _mm(dproj, s["hm"], "tn", name=f"proj_dw_{l}")
        gw["w_in"][l] = jnp.concatenate([dwin[2 * d:2 * d + nat_g], dwin[:2 * d]], axis=0).reshape(
            NCHIP, (nat_g + 2 * d) // NCHIP, d)
        dx1, gw["mix_norm"][l] = _rms_back(s["x1"], mix_norm, l, dhm, dx2, f"mix_norm_bwd_{l}")
        if not riding_rs:
            dx, _ = ffn_back("ffn1", s["x0"], ffn1_norm, s["gt1"], s["up1"], dx1)
            continue
        comm = _join(share(big, 1), chips(ffn2_w, qb_f2), pair(mix_w, 0))
        wg, wu, wd = (bufs[f"ffn1_w_{p}", l] for p in ("gate", "up", "down"))
        dx, dgain, dgt, dup, act, hb, dob, got = _ffn_bwd(s["x0"], ffn1_norm, l, wg, wu, wd, s["gt1"], s["up1"], dx1,
                                                         f"ffn1_bwd_{l}", comm)
        gw["ffn1_norm"][l] = dgain
        shared, parts_f2, sib_mx = split(comm, got)
        red.update(zip(big, shared))
        sum_parts(ffn2_w, 0, qb_f2, dict(zip(ffn2_w, parts_f2)), "f2")
        qb_mx = add_parts(mix_w, 0, dict(zip(mix_w, sib_mx)), "mx")
        parts_mx = {}
        for part, (p, lhs, rhs) in enumerate(zip(("gate", "up", "down"), (dgt, dup, act), (hb, hb, dob))):
            gw[f"ffn1_w_{p}"][l], got = _mm(lhs, rhs, "tn", name=f"ffn1_dw_{p}_{l}", a_g=True,
                                            comm=chips(mix_parts[part], qb_mx))
            parts_mx.update(zip(mix_parts[part], got))
        sum_parts(mix_w, 0, qb_mx, parts_mx, "mx")

    sib_f1 = dict(zip(ffn1_w, _comm_call(pair(ffn1_w, 0), "rs_pair")))
    qb_f1 = add_parts(ffn1_w, 0, sib_f1, "f1")
    parts_f1 = dict(zip(ffn1_w, _comm_call(chips(ffn1_w, qb_f1), "rs_chips")))
    sum_parts(ffn1_w, 0, qb_f1, parts_f1, "f1")
    red = dict(zip(big, _comm_call(share(big, 0), "rs_share")))
    grads = {nm: g.reshape(wts[nm].shape) for nm, g in red.items()}

    vec_names = ["ffn1_norm", "mix_norm", "ffn2_norm", "conv_b", "conv_ln_g", "conv_ln_b", "cq_norm", "ckv_norm",
                 "q_norm", "k_norm"]
    rows = [jnp.concatenate(gw[nm], axis=0) for nm in vec_names]
    rows += [jnp.concatenate(gw["gate_bias"], axis=0), jnp.concatenate(gw["conv_w"], axis=0), loss_row]
    total = _exchange(_rows_pad(rows, d), True, "allreduce_small")
    r0 = 0
    for nm in vec_names:
        grads[nm] = total[r0:r0 + nl, :wts[nm].shape[1]]
        r0 += nl
    gb_all = total[r0:r0 + 2 * nl, :d].reshape(nl, 2, NCHIP, d // NCHIP)
    r0 += 2 * nl
    cw_all = total[r0:r0 + KW * nl, :cc].reshape(nl, KW, NCHIP, cc // NCHIP)
    r0 += KW * nl
    grads["gate_bias"] = lax.dynamic_index_in_dim(gb_all, chip, axis=2, keepdims=False)
    grads["conv_w"] = lax.dynamic_index_in_dim(cw_all, chip, axis=2, keepdims=False)
    loss = total[r0, 0]

    delta, new_m, new_v = {}, {}, {}
    by_shape = {}
    for nm in names:
        shp = wts[nm].shape
        by_shape.setdefault((shp[0] * (shp[1] if len(shp) == 3 else 1), shp[-1]), []).append(nm)
    for gi, (shp2, grp) in enumerate(by_shape.items()):
        to2 = lambda a: a.reshape(shp2)
        ds_, ms_, vs_ = _adamw([to2(wts[nm]) for nm in grp], [to2(grads[nm]) for nm in grp],
                               [to2(mom[nm]) for nm in grp], [to2(var[nm]) for nm in grp], f"adamw_{gi}")
        for nm, dd, mm_, vv in zip(grp, ds_, ms_, vs_):
            delta[nm], new_m[nm], new_v[nm] = (a.reshape(wts[nm].shape) for a in (dd, mm_, vv))

    return (loss, dx[None], *[view(nm, grads[nm]) for nm in names], *[view(nm, delta[nm]) for nm in names],
            *[view(nm, new_m[nm]) for nm in names], *[view(nm, new_v[nm]) for nm in names])
```

```python
import functools

import jax
import jax.numpy as jnp
from jax import lax
from jax.experimental import pallas as pl
from jax.experimental.pallas import tpu as pltpu

F32, BF16 = jnp.float32, jnp.bfloat16
SDS = jax.ShapeDtypeStruct
MESH = pl.DeviceIdType.MESH
ANY = pl.BlockSpec(memory_space=pl.ANY)

NCHIP = 4
N_DEV = 8
N_HEADS, NOPE, ROPE, V_DIM = 8, 64, 32, 64
QK = NOPE + ROPE
HP = 128
CHUNK = 64
KW = 31
CPAD = 32
ROPE_THETA = 10000.0
EPS = 1e-6
LR, B1, B2, EPS_ADAM, WD, STEP = 0.001, 0.9, 0.999, 1e-08, 0.01, 10
VMEM_BIG = 48 << 20
FFN_PARTS = 2
FLASH_HEADS = 2


def _cparams(sem=None, vmem=None):
    kw = {}
    if sem is not None:
        kw["dimension_semantics"] = sem
    if vmem is not None:
        kw["vmem_limit_bytes"] = vmem
    return pltpu.CompilerParams(**kw)


def _rt(t):
    return min(512, t // 2)


def _pick(n, cands):
    for c in cands:
        if c <= n and n % c == 0:
            return c
    return n


def _tile2(r, c, nblocks, row_mult):
    tr, tc = r, c
    while 2 * nblocks * tr * tc * 4 > VMEM_BIG // 2 and tr % (2 * row_mult) == 0:
        tr //= 2
    while 2 * nblocks * tr * tc * 4 > VMEM_BIG // 2 and tc % 256 == 0:
        tc //= 2
    return tr, tc


def _sig(v):
    return 1.0 / (1.0 + jnp.exp(-v))


def _rms_r(v):
    return lax.rsqrt(jnp.mean(v * v, axis=-1, keepdims=True) + EPS)


def _rms_bwd(xv, g, dy):
    r = _rms_r(xv)
    xh = xv * r
    dg = jnp.sum(dy * xh, axis=0, keepdims=True)
    dxh = dy * g
    dx = r * (dxh - xh * jnp.mean(dxh * xh, axis=-1, keepdims=True))
    return dx, dg


def _dot(a, b):
    return jnp.dot(a, b, preferred_element_type=F32)


def _dot_nt(a, b):
    return lax.dot_general(a, b, (((1,), (1,)), ((), ())), preferred_element_type=F32)


def _dot_tn(a, b):
    return lax.dot_general(a, b, (((0,), (0,)), ((), ())), preferred_element_type=F32)


class _Comm:
    def __init__(self, ins, out_shapes, aliases, nsem, start, finish, mid=None):
        self.ins, self.out_shapes, self.aliases, self.nsem = list(ins), list(out_shapes), dict(aliases), nsem
        self.start, self.finish, self.mid = start, finish, mid


def _call(body, comm, first, last, *, name, grid, in_specs, args, out_specs, out_shape, scratch, sem, mid_at=None):
    in_specs, args, out_specs, out_shape, scratch = map(list, (in_specs, args, out_specs, out_shape, scratch))
    n_in, n_out, n_scr = len(args), len(out_shape), len(scratch)
    aliases = {}
    kern = body
    if comm is not None:
        nci, nco = len(comm.ins), len(comm.out_shapes)
        o0 = n_in + nci
        s0 = o0 + n_out + nco

        def kern(*refs):
            cins, couts = refs[n_in:o0], refs[o0 + n_out:s0]
            ssem, rsem = refs[s0 + n_scr:]
            pl.when(first())(lambda: comm.start(cins, couts, ssem, rsem))
            if comm.mid is not None and mid_at is not None:
                pl.when(mid_at())(lambda: comm.mid(cins, couts, ssem, rsem))
            body(*refs[:n_in], *refs[o0:o0 + n_out], *refs[s0:s0 + n_scr])

            def end():
                if comm.mid is not None and mid_at is None:
                    comm.mid(cins, couts, ssem, rsem)
                comm.finish(cins, couts, ssem, rsem)

            pl.when(last())(end)

        in_specs += [ANY] * nci
        args += comm.ins
        out_specs += [ANY] * nco
        out_shape += comm.out_shapes
        scratch += [pltpu.SemaphoreType.DMA((comm.nsem,))] * 2
        aliases = {n_in + i: n_out + o for i, o in comm.aliases.items()}
    outs = pl.pallas_call(kern, name=name, grid=grid, in_specs=in_specs, out_specs=out_specs, out_shape=out_shape,
                          scratch_shapes=scratch, input_output_aliases=aliases,
                          compiler_params=_cparams(sem, VMEM_BIG))(*args)
    return outs[:n_out], outs[n_out:]


class _SemView:
    def __init__(self, sems, base):
        self.sems, self.base = sems, base

    @property
    def at(self):
        return self

    def __getitem__(self, k):
        return self.sems.at[self.base + k]


def _join(*comms):
    ins, outs, aliases, spans, nsem = [], [], {}, [], 0
    for cm in comms:
        aliases.update({len(ins) + i: len(outs) + o for i, o in cm.aliases.items()})
        spans.append((len(ins), len(ins) + len(cm.ins), len(outs), len(outs) + len(cm.out_shapes), nsem))
        ins += cm.ins
        outs += cm.out_shapes
        nsem += cm.nsem

    def run(which, cins, couts, ssem, rsem):
        for cm, (i0, i1, o0, o1, base) in zip(comms, spans):
            if getattr(cm, which) is not None:
                getattr(cm, which)(cins[i0:i1], couts[o0:o1], _SemView(ssem, base), _SemView(rsem, base))

    joined = _Comm(ins, outs, aliases, nsem, functools.partial(run, "start"), functools.partial(run, "finish"),
                   functools.partial(run, "mid") if any(cm.mid is not None for cm in comms) else None)
    joined.spans = [(o0, o1) for _, _, o0, o1, _ in spans]
    return joined


def _comm_call(comm, name):
    once = lambda: pl.program_id(0) == 0
    return _call(lambda: None, comm, once, once, name=name, grid=(1,), in_specs=[], args=[], out_specs=[],
                 out_shape=[], scratch=[], sem=("arbitrary",))[1]


def _mm(a, b, mode, *, name, out_dtype=F32, res=None, a_g=False, comm=None):
    a2 = a.shape[1:] if a_g else a.shape
    if mode == "nn":
        (m, k), (k2, n) = a2, b.shape
    elif mode == "nt":
        (m, k), (n, k2) = a2, b.shape
    else:
        (k, m), (k2, n) = a2, b.shape
    assert k == k2, (name, a.shape, b.shape)
    g = a.shape[0] if a_g else 1
    tm = m if m <= 768 else _pick(m, (768, 512, 256, 128))
    tn = n if n <= 1280 else _pick(n, (1280, 1024, 768, 512, 256, 128))
    tk = k if k <= 1280 else _pick(k, (1280, 1024, 768, 512, 256, 128))
    fits = lambda kk, nn: 2 * (kk * tm * a.dtype.itemsize + kk * nn * b.dtype.itemsize + tm * nn * 4) <= VMEM_BIG - (8 << 20)
    if fits(k, n):
        tn, tk = n, k
    elif fits(k, tn):
        tk = k
    nk = k // tk
    dn = {"nn": (((1,), (0,)), ((), ())), "nt": (((1,), (1,)), ((), ())), "tn": (((0,), (0,)), ((), ()))}[mode]

    def body(*refs):
        a_ref, b_ref = refs[0], refs[1]
        res_ref = refs[2] if res is not None else None
        o_ref = refs[3] if res is not None else refs[2]
        p = lax.dot_general(a_ref[...].astype(BF16), b_ref[...].astype(BF16), dn, preferred_element_type=F32)

        def fin(v):
            if res_ref is not None:
                v = v + res_ref[...]
            o_ref[...] = v.astype(out_dtype)

        if nk == 1:
            fin(p)
        else:
            acc_ref = refs[-1]
            kk = pl.program_id(3)

            @pl.when(kk == 0)
            def _():
                acc_ref[...] = p

            @pl.when(kk > 0)
            def _():
                acc_ref[...] += p

            @pl.when(kk == nk - 1)
            def _():
                fin(acc_ref[...])

    a_block, a_idx = ((tk, tm), lambda gg, i, j, kk: (kk, i)) if mode == "tn" else ((tm, tk), lambda gg, i, j, kk: (i, kk))
    if a_g:
        a_spec = pl.BlockSpec((None,) + a_block, lambda gg, i, j, kk: (gg,) + a_idx(gg, i, j, kk))
    else:
        a_spec = pl.BlockSpec(a_block, a_idx)
    if mode == "nt":
        b_spec = pl.BlockSpec((tn, tk), lambda gg, i, j, kk: (j, kk))
    else:
        b_spec = pl.BlockSpec((tk, tn), lambda gg, i, j, kk: (kk, j))
    in_specs, args = [a_spec, b_spec], [a, b]
    if res is not None:
        in_specs.append(pl.BlockSpec((tm, tn), lambda gg, i, j, kk: (i, j)))
        args.append(res)
    if a_g:
        out_spec, oshape = pl.BlockSpec((None, tm, tn), lambda gg, i, j, kk: (gg, i, j)), (g, m, n)
    else:
        out_spec, oshape = pl.BlockSpec((tm, tn), lambda gg, i, j, kk: (i, j)), (m, n)
    grid = (g, m // tm, n // tn, nk)

    def at(corner):
        hit = pl.program_id(0) == corner[0]
        for ax in range(1, 4):
            hit = jnp.logical_and(hit, pl.program_id(ax) == corner[ax])
        return hit

    outs, couts = _call(body, comm, lambda: at((0, 0, 0, 0)), lambda: at(tuple(dim - 1 for dim in grid)), name=name,
                        grid=grid, in_specs=in_specs, args=args, out_specs=[out_spec], out_shape=[SDS(oshape, out_dtype)],
                        scratch=[pltpu.VMEM((tm, tn), F32)] if nk > 1 else [], sem=("arbitrary",) * 4)
    return outs[0] if comm is None else (outs[0], couts)


def _ffn_fwd(x, gains, l, wg, wu, wd, name, comm=None, mid_frac=0.75):
    t, d = x.shape
    f = wg.shape[-2]
    tt = _rt(t)

    def body(x_ref, g_ref, wg_ref, wu_ref, wd_ref, o_ref, gt_ref, up_ref, h_sc, acc_sc):
        j = pl.program_id(1)

        @pl.when(j == 0)
        def _():
            xv = x_ref[...]
            h_sc[...] = (xv * _rms_r(xv) * g_ref[l:l + 1, :]).astype(BF16)
            acc_sc[...] = jnp.zeros_like(acc_sc)

        for part in range(FFN_PARTS):
            rows = pl.ds(part * (tt // FFN_PARTS), tt // FFN_PARTS)
            h = h_sc[rows, :]
            gt = _dot_nt(h, wg_ref[...]).astype(BF16)
            up = _dot_nt(h, wu_ref[...]).astype(BF16)
            gt_ref[rows, :] = gt
            up_ref[rows, :] = up
            acc_sc[rows, :] += _dot(gt * _sig(gt) * up, wd_ref[...])

        @pl.when(j == NCHIP - 1)
        def _():
            o_ref[...] = x_ref[...] + 0.5 * acc_sc[...]

    wspec = pl.BlockSpec((None, f, d), lambda i, j: (j, 0, 0))
    row = pl.BlockSpec((tt, d), lambda i, j: (i, 0))
    sh = pl.BlockSpec((None, tt, f), lambda i, j: (j, i, 0))
    ni = t // tt
    first = lambda: jnp.logical_and(pl.program_id(0) == 0, pl.program_id(1) == 0)
    last = lambda: jnp.logical_and(pl.program_id(0) == ni - 1, pl.program_id(1) == NCHIP - 1)
    mid_at = lambda: pl.program_id(0) * NCHIP + pl.program_id(1) == int(mid_frac * ni * NCHIP)
    outs, couts = _call(
        body, comm, first, last, name=name, grid=(ni, NCHIP), mid_at=mid_at,
        in_specs=[row, pl.BlockSpec(gains.shape, lambda i, j: (0, 0)), wspec, wspec, wspec], args=[x, gains, wg, wu, wd],
        out_specs=[row, sh, sh], out_shape=[SDS((t, d), F32), SDS((NCHIP, t, f), BF16), SDS((NCHIP, t, f), BF16)],
        scratch=[pltpu.VMEM((tt, d), BF16), pltpu.VMEM((tt, d), F32)], sem=("arbitrary", "arbitrary"))
    return (*outs, couts)


def _ffn_bwd(x, gains, l, wg, wu, wd, gts, ups, dout, name, comm=None):
    t, d = x.shape
    f = wg.shape[-2]
    tt = _rt(t)

    def body(x_ref, g_ref, wg_ref, wu_ref, wd_ref, gt_ref, up_ref, do_ref,
             dx_ref, dg_ref, dgt_ref, dup_ref, act_ref, h_ref, dob_ref, dh_sc):
        i, j = pl.program_id(0), pl.program_id(1)

        @pl.when(j == 0)
        def _():
            xv = x_ref[...]
            h_ref[...] = (xv * _rms_r(xv) * g_ref[l:l + 1, :]).astype(BF16)
            dob_ref[...] = (0.5 * do_ref[...]).astype(BF16)
            dh_sc[...] = jnp.zeros_like(dh_sc)

        @pl.when(jnp.logical_and(i == 0, j == 0))
        def _():
            dg_ref[...] = jnp.zeros_like(dg_ref)

        for part in range(FFN_PARTS):
            rows = pl.ds(part * (tt // FFN_PARTS), tt // FFN_PARTS)
            dact = _dot_nt(dob_ref[rows, :], wd_ref[...]).astype(BF16)
            gt = gt_ref[rows, :]
            up = up_ref[rows, :]
            s = _sig(gt)
            sl = gt * s
            dup = dact * sl
            dgt = dact * up * (s + sl * (1.0 - s))
            dgt_ref[rows, :] = dgt
            dup_ref[rows, :] = dup
            act_ref[rows, :] = sl * up
            dh_sc[rows, :] += _dot(dgt, wg_ref[...]) + _dot(dup, wu_ref[...])

        @pl.when(j == NCHIP - 1)
        def _():
            dxn, dg = _rms_bwd(x_ref[...], g_ref[l:l + 1, :], dh_sc[...])
            dx_ref[...] = do_ref[...] + dxn
            dg_ref[...] += dg

    wspec = pl.BlockSpec((None, f, d), lambda i, j: (j, 0, 0))
    row = pl.BlockSpec((tt, d), lambda i, j: (i, 0))
    sh = pl.BlockSpec((None, tt, f), lambda i, j: (j, i, 0))
    ni = t // tt
    first = lambda: jnp.logical_and(pl.program_id(0) == 0, pl.program_id(1) == 0)
    last = lambda: jnp.logical_and(pl.program_id(0) == ni - 1, pl.program_id(1) == NCHIP - 1)
    outs, couts = _call(
        body, comm, first, last, name=name, grid=(ni, NCHIP),
        in_specs=[row, pl.BlockSpec(gains.shape, lambda i, j: (0, 0)), wspec, wspec, wspec, sh, sh, row],
        args=[x, gains, wg, wu, wd, gts, ups, dout],
        out_specs=[row, pl.BlockSpec((1, d), lambda i, j: (0, 0)), sh, sh, sh, row, row],
        out_shape=[SDS((t, d), F32), SDS((1, d), F32)] + [SDS((NCHIP, t, f), BF16)] * 3 + [SDS((t, d), BF16)] * 2,
        scratch=[pltpu.VMEM((tt, d), F32)], sem=("arbitrary", "arbitrary"))
    return (*outs, couts)


def _rms_fwd(x, gains, l, name):
    t, d = x.shape
    tt = _rt(t)

    def body(x_ref, g_ref, o_ref):
        xv = x_ref[...]
        o_ref[...] = (xv * _rms_r(xv) * g_ref[l:l + 1, :]).astype(BF16)

    row = pl.BlockSpec((tt, d), lambda i: (i, 0))
    return pl.pallas_call(body, name=name, grid=(t // tt,), in_specs=[row, pl.BlockSpec(gains.shape, lambda i: (0, 0))],
                          out_specs=row, out_shape=SDS((t, d), BF16), compiler_params=_cparams(("arbitrary",)))(x, gains)


def _rms_back(x, gains, l, dh, dres, name):
    t, d = x.shape
    tt = _rt(t)

    def body(x_ref, g_ref, dh_ref, dr_ref, dx_ref, dg_ref):
        @pl.when(pl.program_id(0) == 0)
        def _():
            dg_ref[...] = jnp.zeros_like(dg_ref)

        dxn, dg = _rms_bwd(x_ref[...], g_ref[l:l + 1, :], dh_ref[...])
        dx_ref[...] = dr_ref[...] + dxn
        dg_ref[...] += dg

    row = pl.BlockSpec((tt, d), lambda i: (i, 0))
    return pl.pallas_call(body, name=name, grid=(t // tt,),
                          in_specs=[row, pl.BlockSpec(gains.shape, lambda i: (0, 0)), row, row],
                          out_specs=[row, pl.BlockSpec((1, d), lambda i: (0, 0))],
                          out_shape=[SDS((t, d), F32), SDS((1, d), F32)],
                          compiler_params=_cparams(("arbitrary",)))(x, gains, dh, dres)


def _dwconv_fwd(proj, convw, convb, l, o_a, cc, name):
    t = proj.shape[0]
    r = min(256, t)
    nb = cc // HP

    def body(a_ref, gate_ref, w_ref, b_ref, uc_ref, u_sc):
        u_sc[0:CPAD, :] = jnp.zeros((CPAD, HP), F32)

        def fill(ci, carry):
            r0 = pl.multiple_of(ci * r, r)
            u_sc[pl.ds(CPAD + r0, r), :] = a_ref[pl.ds(r0, r), :].astype(F32) * _sig(gate_ref[pl.ds(r0, r), :].astype(F32))
            return carry

        lax.fori_loop(0, t // r, fill, 0)
        w = w_ref[l]
        bias = b_ref[l:l + 1, :]

        def conv(ci, carry):
            r0 = pl.multiple_of(ci * r, r)
            win = u_sc[pl.ds(r0, r + CPAD), :]
            acc = jnp.zeros((r, HP), F32) + bias
            for k in range(KW):
                off = CPAD - (KW - 1) + k
                acc = acc + win[off:off + r, :] * w[k:k + 1, :]
            uc_ref[pl.ds(r0, r), :] = acc
            return carry

        lax.fori_loop(0, t // r, conv, 0)

    col = lambda base: pl.BlockSpec((t, HP), lambda c: (0, base // HP + c))
    return pl.pallas_call(
        body, name=name, grid=(nb,),
        in_specs=[col(o_a), col(o_a + cc), pl.BlockSpec((convw.shape[0], CPAD, HP), lambda c: (0, 0, c)),
                  pl.BlockSpec((convb.shape[0], HP), lambda c: (0, c))],
        out_specs=pl.BlockSpec((t, HP), lambda c: (0, c)), out_shape=SDS((t, cc), F32),
        scratch_shapes=[pltpu.VMEM((t + CPAD, HP), F32)],
        compiler_params=_cparams(("arbitrary",), VMEM_BIG))(proj, proj, convw, convb)


def _dwconv_bwd(duc, proj, convw, l, o_a, cc, name):
    t = proj.shape[0]
    r = min(256, t)
    nb = cc // HP

    def body(d_ref, a_ref, gate_ref, w_ref, du_ref, dw_ref, u_sc, d_sc, dw_sc):
        u_sc[0:CPAD, :] = jnp.zeros((CPAD, HP), F32)
        d_sc[t:t + CPAD, :] = jnp.zeros((CPAD, HP), F32)
        dw_sc[...] = jnp.zeros_like(dw_sc)

        def fill(ci, carry):
            r0 = pl.multiple_of(ci * r, r)
            u_sc[pl.ds(CPAD + r0, r), :] = a_ref[pl.ds(r0, r), :].astype(F32) * _sig(gate_ref[pl.ds(r0, r), :].astype(F32))
            d_sc[pl.ds(r0, r), :] = d_ref[pl.ds(r0, r), :]
            return carry

        lax.fori_loop(0, t // r, fill, 0)
        w = w_ref[l]

        def conv(ci, carry):
            r0 = pl.multiple_of(ci * r, r)
            dwin = d_sc[pl.ds(r0, r + CPAD), :]
            uwin = u_sc[pl.ds(r0, r + CPAD), :]
            dcur = dwin[0:r, :]
            acc = jnp.zeros((r, HP), F32)
            for k in range(KW):
                acc = acc + dwin[KW - 1 - k:KW - 1 - k + r, :] * w[k:k + 1, :]
                off = CPAD - (KW - 1) + k
                part = (dcur * uwin[off:off + r, :]).reshape(r // 8, 8, HP).sum(axis=0)
                dw_sc[8 * k:8 * k + 8, :] += part
            du_ref[pl.ds(r0, r), :] = acc
            return carry

        lax.fori_loop(0, t // r, conv, 0)
        dw_ref[...] = jnp.zeros_like(dw_ref)
        for k in range(KW):
            dw_ref[k:k + 1, :] = jnp.sum(dw_sc[8 * k:8 * k + 8, :], axis=0, keepdims=True)

    col = lambda base: pl.BlockSpec((t, HP), lambda c: (0, base // HP + c))
    return pl.pallas_call(
        body, name=name, grid=(nb,),
        in_specs=[pl.BlockSpec((t, HP), lambda c: (0, c)), col(o_a), col(o_a + cc),
                  pl.BlockSpec((convw.shape[0], CPAD, HP), lambda c: (0, 0, c))],
        out_specs=[pl.BlockSpec((t, HP), lambda c: (0, c)), pl.BlockSpec((CPAD, HP), lambda c: (0, c))],
        out_shape=[SDS((t, cc), F32), SDS((CPAD, cc), F32)],
        scratch_shapes=[pltpu.VMEM((t + CPAD, HP), F32), pltpu.VMEM((t + CPAD, HP), F32), pltpu.VMEM((8 * CPAD, HP), F32)],
        compiler_params=_cparams(("arbitrary",), VMEM_BIG))(duc, proj, proj, convw)


def _ln_parts(uc, g, b):
    mu = jnp.mean(uc, axis=-1, keepdims=True)
    xc = uc - mu
    r = lax.rsqrt(jnp.mean(xc * xc, axis=-1, keepdims=True) + EPS)
    xh = xc * r
    return r, xh, xh * g + b


def _ln_silu(uc, ln_g, ln_b, l, name):
    t, cc = uc.shape
    tt = _rt(t)

    def body(u_ref, g_ref, b_ref, s_ref):
        _, _, yv = _ln_parts(u_ref[...], g_ref[l:l + 1, :], b_ref[l:l + 1, :])
        s_ref[...] = (yv * _sig(yv)).astype(BF16)

    row = pl.BlockSpec((tt, cc), lambda i: (i, 0))
    full = pl.BlockSpec(ln_g.shape, lambda i: (0, 0))
    return pl.pallas_call(body, name=name, grid=(t // tt,), in_specs=[row, full, full], out_specs=row,
                          out_shape=SDS((t, cc), BF16), compiler_params=_cparams(("arbitrary",)))(uc, ln_g, ln_b)


def _ln_silu_bwd(uc, ln_g, ln_b, l, ds, name):
    t, cc = uc.shape
    tt = _rt(t)

    def body(u_ref, g_ref, b_ref, ds_ref, du_ref, dg_ref, db_ref, dcb_ref):
        @pl.when(pl.program_id(0) == 0)
        def _():
            dg_ref[...] = jnp.zeros_like(dg_ref)
            db_ref[...] = jnp.zeros_like(db_ref)
            dcb_ref[...] = jnp.zeros_like(dcb_ref)

        g = g_ref[l:l + 1, :]
        r, xh, yv = _ln_parts(u_ref[...], g, b_ref[l:l + 1, :])
        sy = _sig(yv)
        dy = ds_ref[...] * (sy * (1.0 + yv * (1.0 - sy)))
        dg_ref[...] += jnp.sum(dy * xh, axis=0, keepdims=True)
        db_ref[...] += jnp.sum(dy, axis=0, keepdims=True)
        dxh = dy * g
        du = r * (dxh - jnp.mean(dxh, axis=-1, keepdims=True) - xh * jnp.mean(dxh * xh, axis=-1, keepdims=True))
        du_ref[...] = du
        dcb_ref[...] += jnp.sum(du, axis=0, keepdims=True)

    row = pl.BlockSpec((tt, cc), lambda i: (i, 0))
    full = pl.BlockSpec(ln_g.shape, lambda i: (0, 0))
    vec = pl.BlockSpec((1, cc), lambda i: (0, 0))
    return pl.pallas_call(body, name=name, grid=(t // tt,), in_specs=[row, full, full, row],
                          out_specs=[row, vec, vec, vec], out_shape=[SDS((t, cc), F32)] + [SDS((1, cc), F32)] * 3,
                          compiler_params=_cparams(("arbitrary",)))(uc, ln_g, ln_b, ds)


def _glu_bwd(du, proj, dproj, o_a, cc, name):
    t = du.shape[0]
    tt = _rt(t)

    def body(du_ref, a_ref, gate_ref, prev_ref, o_ref):
        sg = _sig(gate_ref[...].astype(F32))
        dv = du_ref[...]
        o_ref[:, 0:cc] = (dv * sg).astype(BF16)
        o_ref[:, cc:2 * cc] = (dv * a_ref[...] * sg * (1.0 - sg)).astype(BF16)

    return pl.pallas_call(
        body, name=name, grid=(t // tt,),
        in_specs=[pl.BlockSpec((tt, cc), lambda i: (i, 0)), pl.BlockSpec((tt, cc), lambda i: (i, o_a // cc)),
                  pl.BlockSpec((tt, cc), lambda i: (i, o_a // cc + 1)), ANY],
        out_specs=pl.BlockSpec((tt, 2 * cc), lambda i: (i, o_a // (2 * cc))),
        out_shape=SDS(dproj.shape, dproj.dtype), input_output_aliases={3: 0},
        compiler_params=_cparams(("arbitrary",)))(du, proj, proj, dproj)


def _rope(v, cs, s1, s2):
    return v * cs + pltpu.roll(v, HP - ROPE // 2, 1) * s1 + pltpu.roll(v, ROPE // 2, 1) * s2


def _rope_t(dv, cs, s1, s2):
    return dv * cs + pltpu.roll(dv * s1, ROPE // 2, 1) + pltpu.roll(dv * s2, HP - ROPE // 2, 1)


def _head_norm(v, g):
    r = lax.rsqrt(jnp.sum(v * v, axis=-1, keepdims=True) * (1.0 / QK) + EPS)
    return v * r * g, r


def _head_norm_bwd(v, r, g, dy):
    xh = v * r
    dg = jnp.sum(dy * xh, axis=0, keepdims=True)
    dxh = dy * g
    dx = r * (dxh - xh * (jnp.sum(dxh * xh, axis=-1, keepdims=True) * (1.0 / QK)))
    return dx, dg


def _mla_specs(tt, lay, cq_norm, ckv_norm, qn, kn, wuq, wukv):
    ql, kvl = cq_norm.shape[1], ckv_norm.shape[1]
    full = lambda a: pl.BlockSpec(a.shape, lambda i: (0,) * a.ndim)
    tab = pl.BlockSpec((tt, HP), lambda i: (i, 0))
    return [pl.BlockSpec((tt, ql), lambda i: (i, lay["cq"] // ql)),
            pl.BlockSpec((tt, kvl), lambda i: (i, lay["ckv"] // kvl)),
            pl.BlockSpec((tt, HP), lambda i: (i, lay["kr"] // HP)),
            full(cq_norm), full(ckv_norm), full(qn), full(kn), full(wuq), full(wukv), tab, tab, tab]


def _mla_pre_fwd(proj, lay, l, cq_norm, ckv_norm, qn, kn, wuq, wukv, tabs, name):
    t = proj.shape[0]
    tt = _rt(t)
    hw = N_HEADS * HP

    def body(cq_ref, ckv_ref, kr_ref, gq_ref, gkv_ref, qn_ref, kn_ref, wq_ref, wkv_ref, c_ref, s1_ref, s2_ref,
             q_ref, k_ref, v_ref):
        cq = cq_ref[...].astype(F32)
        cqn = (cq * _rms_r(cq) * gq_ref[l:l + 1, :]).astype(BF16)
        ckv = ckv_ref[...].astype(F32)
        ckvn = (ckv * _rms_r(ckv) * gkv_ref[l:l + 1, :]).astype(BF16)
        qraw = _dot_nt(cqn, wq_ref[...])
        kv = _dot(ckvn, wkv_ref[...])
        v_ref[...] = kv.astype(BF16)
        lane = lax.broadcasted_iota(jnp.int32, (tt, HP), 1)
        krs = pltpu.roll(jnp.where(lane < ROPE, kr_ref[...].astype(F32), 0.0), NOPE, 1)
        cs, s1, s2 = c_ref[...], s1_ref[...], s2_ref[...]
        gq, gk = qn_ref[l:l + 1, :], kn_ref[l:l + 1, :]
        for h in range(N_HEADS):
            sl = slice(h * HP, (h + 1) * HP)
            qh, _ = _head_norm(qraw[:, sl], gq)
            q_ref[:, sl] = (_rope(qh, cs, s1, s2) * QK ** -0.5).astype(BF16)
            kh, _ = _head_norm(jnp.where(lane < NOPE, kv[:, sl], krs), gk)
            k_ref[:, sl] = _rope(kh, cs, s1, s2).astype(BF16)

    row = pl.BlockSpec((tt, hw), lambda i: (i, 0))
    return pl.pallas_call(
        body, name=name, grid=(t // tt,),
        in_specs=_mla_specs(tt, lay, cq_norm, ckv_norm, qn, kn, wuq, wukv),
        out_specs=[row, row, row], out_shape=[SDS((t, hw), BF16)] * 3,
        compiler_params=_cparams(("arbitrary",), VMEM_BIG))(
            proj, proj, proj, cq_norm, ckv_norm, qn, kn, wuq, wukv, *tabs)


def _mla_pre_bwd(proj, lay, l, cq_norm, ckv_norm, qn, kn, wuq, wukv, tabs, dq, dk, dv, dproj, name):
    t = proj.shape[0]
    tt = _rt(t)
    hw = N_HEADS * HP
    ql, kvl = cq_norm.shape[1], ckv_norm.shape[1]
    wm = lay["wm"]

    def body(cq_ref, ckv_ref, kr_ref, gq_ref, gkv_ref, qn_ref, kn_ref, wq_ref, wkv_ref, c_ref, s1_ref, s2_ref,
             dq_ref, dk_ref, dv_ref, prev_ref,
             o_ref, dqr_ref, dkv_ref, cqn_ref, ckvn_ref, dgq_ref, dgkv_ref, dqn_ref, dkn_ref):
        @pl.when(pl.program_id(0) == 0)
        def _():
            for ref in (dgq_ref, dgkv_ref, dqn_ref, dkn_ref):
                ref[...] = jnp.zeros_like(ref)

        cq = cq_ref[...].astype(F32)
        cqn = (cq * _rms_r(cq) * gq_ref[l:l + 1, :]).astype(BF16)
        ckv = ckv_ref[...].astype(F32)
        ckvn = (ckv * _rms_r(ckv) * gkv_ref[l:l + 1, :]).astype(BF16)
        cqn_ref[...] = cqn
        ckvn_ref[...] = ckvn
        qraw = _dot_nt(cqn, wq_ref[...])
        kv = _dot(ckvn, wkv_ref[...])
        lane = lax.broadcasted_iota(jnp.int32, (tt, HP), 1)
        krs = pltpu.roll(jnp.where(lane < ROPE, kr_ref[...].astype(F32), 0.0), NOPE, 1)
        cs, s1, s2 = c_ref[...], s1_ref[...], s2_ref[...]
        gq, gk = qn_ref[l:l + 1, :], kn_ref[l:l + 1, :]
        dkr = jnp.zeros((tt, HP), F32)
        dgq = jnp.zeros((1, HP), F32)
        dgk = jnp.zeros((1, HP), F32)
        for h in range(N_HEADS):
            sl = slice(h * HP, (h + 1) * HP)
            qh = qraw[:, sl]
            _, rq = _head_norm(qh, gq)
            dqh, dg = _head_norm_bwd(qh, rq, gq, _rope_t(dq_ref[:, sl] * QK ** -0.5, cs, s1, s2))
            dgq = dgq + dg
            dqr_ref[:, sl] = dqh.astype(BF16)
            kp = jnp.where(lane < NOPE, kv[:, sl], krs)
            _, rk = _head_norm(kp, gk)
            dkp, dg = _head_norm_bwd(kp, rk, gk, _rope_t(dk_ref[:, sl], cs, s1, s2))
            dgk = dgk + dg
            dkv_ref[:, sl] = (jnp.where(lane < NOPE, dkp, 0.0) + dv_ref[:, sl]).astype(BF16)
            dkr = dkr + dkp
        dqn_ref[...] += dgq
        dkn_ref[...] += dgk
        dkr = jnp.where(lane < ROPE, pltpu.roll(dkr, HP - NOPE, 1), 0.0)
        dcq, dg = _rms_bwd(cq, gq_ref[l:l + 1, :], _dot(dqr_ref[...], wq_ref[...]))
        dgq_ref[...] += dg
        dckv, dg = _rms_bwd(ckv, gkv_ref[l:l + 1, :], _dot_nt(dkv_ref[...], wkv_ref[...]))
        dgkv_ref[...] += dg
        o_ref[:, 0:ql] = dcq.astype(BF16)
        o_ref[:, ql:ql + kvl] = dckv.astype(BF16)
        o_ref[:, ql + kvl:ql + kvl + HP] = dkr.astype(BF16)
        o_ref[:, ql + kvl + HP:wm] = jnp.zeros((tt, wm - ql - kvl - HP), BF16)

    row = pl.BlockSpec((tt, hw), lambda i: (i, 0))
    vec = lambda n: pl.BlockSpec((1, n), lambda i: (0, 0))
    return pl.pallas_call(
        body, name=name, grid=(t // tt,),
        in_specs=_mla_specs(tt, lay, cq_norm, ckv_norm, qn, kn, wuq, wukv) + [row, row, row, ANY],
        out_specs=[pl.BlockSpec((tt, wm), lambda i: (i, lay["cq"] // wm)), row, row,
                   pl.BlockSpec((tt, ql), lambda i: (i, 0)), pl.BlockSpec((tt, kvl), lambda i: (i, 0)),
                   vec(ql), vec(kvl), vec(HP), vec(HP)],
        out_shape=[SDS(dproj.shape, dproj.dtype), SDS((t, hw), BF16), SDS((t, hw), BF16), SDS((t, ql), BF16),
                   SDS((t, kvl), BF16), SDS((1, ql), F32), SDS((1, kvl), F32), SDS((1, HP), F32), SDS((1, HP), F32)],
        input_output_aliases={15: 0},
        compiler_params=_cparams(("arbitrary",), VMEM_BIG))(
            proj, proj, proj, cq_norm, ckv_norm, qn, kn, wuq, wukv, *tabs, dq, dk, dv, dproj)


def _comb_fwd(proj, gate_bias, l, yc, ym, name):
    t, d = yc.shape
    tt = _rt(t)

    def body(p_ref, b_ref, yc_ref, ym_ref, y_ref):
        b = b_ref[l]
        g0 = _sig(p_ref[:, 0:d] + b[0:1, :])
        g1 = _sig(p_ref[:, d:2 * d] + b[1:2, :])
        y_ref[...] = (g0 * yc_ref[...] + g1 * ym_ref[...]).astype(BF16)

    row = pl.BlockSpec((tt, d), lambda i: (i, 0))
    return pl.pallas_call(
        body, name=name, grid=(t // tt,),
        in_specs=[pl.BlockSpec((tt, 2 * d), lambda i: (i, 0)), pl.BlockSpec(gate_bias.shape, lambda i: (0, 0, 0)), row, row],
        out_specs=row, out_shape=SDS((t, d), BF16), compiler_params=_cparams(("arbitrary",)))(proj, gate_bias, yc, ym)


def _comb_bwd(proj, gate_bias, l, yc, ym, dy, dp_cols, name):
    t, d = yc.shape
    tt = _rt(t)

    def body(p_ref, b_ref, yc_ref, ym_ref, dy_ref, dyc_ref, dym_ref, dp_ref, db_ref):
        @pl.when(pl.program_id(0) == 0)
        def _():
            db_ref[...] = jnp.zeros_like(db_ref)

        b = b_ref[l]
        dyv = dy_ref[...].astype(F32)
        g0 = _sig(p_ref[:, 0:d] + b[0:1, :])
        g1 = _sig(p_ref[:, d:2 * d] + b[1:2, :])
        dyc_ref[...] = (dyv * g0).astype(BF16)
        dym_ref[...] = (dyv * g1).astype(BF16)
        dg0 = dyv * yc_ref[...] * g0 * (1.0 - g0)
        dg1 = dyv * ym_ref[...] * g1 * (1.0 - g1)
        dp_ref[:, 0:d] = dg0.astype(BF16)
        dp_ref[:, d:2 * d] = dg1.astype(BF16)
        db_ref[0:1, :] += jnp.sum(dg0, axis=0, keepdims=True)
        db_ref[1:2, :] += jnp.sum(dg1, axis=0, keepdims=True)

    row = pl.BlockSpec((tt, d), lambda i: (i, 0))
    wide = pl.BlockSpec((tt, 2 * d), lambda i: (i, 0))
    return pl.pallas_call(
        body, name=name, grid=(t // tt,),
        in_specs=[wide, pl.BlockSpec(gate_bias.shape, lambda i: (0, 0, 0)), row, row, row],
        out_specs=[row, row, wide, pl.BlockSpec((2, d), lambda i: (0, 0))],
        out_shape=[SDS((t, d), BF16), SDS((t, d), BF16), SDS((t, dp_cols), BF16), SDS((2, d), F32)],
        compiler_params=_cparams(("arbitrary",)))(proj, gate_bias, yc, ym, dy)


def _loss_grad(y, target, name):
    t, d = y.shape
    tt = _rt(t)
    nt = t // tt

    def body(y_ref, t_ref, dy_ref, loss_ref, acc_sc):
        i = pl.program_id(0)

        @pl.when(i == 0)
        def _():
            acc_sc[...] = jnp.zeros_like(acc_sc)

        diff = y_ref[...] - t_ref[...]
        dy_ref[...] = diff * (1.0 / d)
        acc_sc[...] += jnp.sum(diff * diff, axis=0, keepdims=True)

        @pl.when(i == nt - 1)
        def _():
            tot = jnp.sum(acc_sc[...], axis=1, keepdims=True) * (0.5 / d)
            loss_ref[...] = jnp.broadcast_to(tot, (1, HP))

    row = pl.BlockSpec((tt, d), lambda i: (i, 0))
    return pl.pallas_call(body, name=name, grid=(nt,), in_specs=[row, row],
                          out_specs=[row, pl.BlockSpec((1, HP), lambda i: (0, 0))],
                          out_shape=[SDS((t, d), F32), SDS((1, HP), F32)],
                          scratch_shapes=[pltpu.VMEM((1, d), F32)],
                          compiler_params=_cparams(("arbitrary",)))(y, target)


def _chunk_mask(tq):
    rows = lax.broadcasted_iota(jnp.int32, (tq, tq), 0) // CHUNK
    cols = lax.broadcasted_iota(jnp.int32, (tq, tq), 1) // CHUNK
    return cols <= rows


NEG = -1e30


def _flash_fwd(q, k, v, name, comm=None):
    t = q.shape[0]
    tq = _rt(t)
    nq = t // tq
    rep = tq // HP

    def body(q_ref, k_ref, v_ref, o_ref, lse_ref):
        qi = pl.program_id(1)

        def one(hh, ki, carry, masked):
            m_prev, l_prev, acc = carry
            lanes = slice(hh * HP, (hh + 1) * HP)
            r0 = pl.multiple_of(ki * tq, tq)
            s = _dot_nt(q_ref[:, lanes], k_ref[pl.ds(r0, tq), lanes])
            if masked:
                s = jnp.where(_chunk_mask(tq), s, NEG)
            m_new = jnp.maximum(m_prev, jnp.max(s, axis=-1, keepdims=True))
            a = jnp.exp(m_prev - m_new)
            p = jnp.exp(s - jnp.tile(m_new, (1, rep)))
            l_new = a * l_prev + jnp.sum(p, axis=-1, keepdims=True)
            acc = a * acc + _dot(p.astype(BF16), v_ref[pl.ds(r0, tq), lanes])
            return m_new, l_new, acc

        def step(ki, carries, masked):
            return tuple(one(hh, ki, carries[hh], masked) for hh in range(FLASH_HEADS))

        init = (jnp.full((tq, HP), NEG, F32), jnp.zeros((tq, HP), F32), jnp.zeros((tq, HP), F32))
        carries = lax.fori_loop(0, qi, lambda ki, cr: step(ki, cr, False), (init,) * FLASH_HEADS)
        for hh, (m_fin, l_fin, acc) in enumerate(step(qi, carries, True)):
            o_ref[:, hh * HP:(hh + 1) * HP] = (acc / l_fin).astype(BF16)
            lse_ref[hh] = m_fin + jnp.log(l_fin)

    wide = FLASH_HEADS * HP
    ng = N_HEADS // FLASH_HEADS
    qspec = pl.BlockSpec((tq, wide), lambda h, qi: (qi, h))
    head = pl.BlockSpec((t, wide), lambda h, qi: (0, h))
    first = lambda: jnp.logical_and(pl.program_id(0) == 0, pl.program_id(1) == 0)
    last = lambda: jnp.logical_and(pl.program_id(0) == ng - 1, pl.program_id(1) == nq - 1)
    mid_at = lambda: pl.program_id(0) * nq + pl.program_id(1) == (7 * ng * nq) // 8
    outs, couts = _call(
        body, comm, first, last, name=name, grid=(ng, nq), mid_at=mid_at, in_specs=[qspec, head, head], args=[q, k, v],
        out_specs=[qspec, pl.BlockSpec((FLASH_HEADS, tq, HP), lambda h, qi: (h, qi, 0))],
        out_shape=[SDS(q.shape, BF16), SDS((N_HEADS, t, HP), F32)], scratch=[], sem=("arbitrary",) * 2)
    return (*outs, couts)


def _flash_bwd(q, k, v, do, o, lse, name, comm=None):
    t = q.shape[0]
    tq = _rt(t)
    nq = t // tq
    rep = tq // HP

    def body(q_ref, k_ref, v_ref, do_ref, o_ref, lse_ref, dq_ref, dk_ref, dv_ref, delta_sc):
        def prep(qi, carry):
            r0 = pl.multiple_of(qi * tq, tq)
            rows = pl.ds(r0, tq)
            dlt = jnp.sum(do_ref[rows, :].astype(F32) * o_ref[rows, :].astype(F32), axis=-1, keepdims=True)
            delta_sc[rows, :] = jnp.broadcast_to(dlt, (tq, HP))
            dq_ref[rows, :] = jnp.zeros((tq, HP), F32)
            return carry

        lax.fori_loop(0, nq, prep, 0)

        def keys(ki, carry0):
            krows = pl.ds(pl.multiple_of(ki * tq, tq), tq)
            kt, vt = k_ref[krows, :], v_ref[krows, :]

            def step(qi, carry, masked):
                dk_acc, dv_acc = carry
                rows = pl.ds(pl.multiple_of(qi * tq, tq), tq)
                qt, dot_ = q_ref[rows, :], do_ref[rows, :]
                s = _dot_nt(qt, kt)
                if masked:
                    s = jnp.where(_chunk_mask(tq), s, NEG)
                p = jnp.exp(s - jnp.tile(lse_ref[rows, :], (1, rep)))
                ds = (p * (_dot_nt(dot_, vt) - jnp.tile(delta_sc[rows, :], (1, rep)))).astype(BF16)
                dv_acc = dv_acc + _dot_tn(p.astype(BF16), dot_)
                dk_acc = dk_acc + _dot_tn(ds, qt)
                dq_ref[rows, :] += _dot(ds, kt)
                return dk_acc, dv_acc

            zero = jnp.zeros((tq, HP), F32)
            carry = step(ki, (zero, zero), True)
            dk_acc, dv_acc = lax.fori_loop(ki + 1, nq, lambda qi, cr: step(qi, cr, False), carry)
            dk_ref[krows, :] = dk_acc
            dv_ref[krows, :] = dv_acc
            return carry0

        lax.fori_loop(0, nq, keys, 0)

    head = pl.BlockSpec((t, HP), lambda h: (0, h))
    outs, couts = _call(
        body, comm, lambda: pl.program_id(0) == 0, lambda: pl.program_id(0) == N_HEADS - 1, name=name, grid=(N_HEADS,),
        in_specs=[head] * 5 + [pl.BlockSpec((None, t, HP), lambda h: (h, 0, 0))], args=[q, k, v, do, o, lse],
        out_specs=[head] * 3, out_shape=[SDS(q.shape, F32)] * 3, scratch=[pltpu.VMEM((t, HP), F32)], sem=("arbitrary",))
    return (*outs, couts)


def _add_cast(gs, rs, place, name):
    n = len(gs)
    _, r, c = gs[0].shape
    tr, tc = _tile2(r, c // 2, 3 * n, 16)
    nct = c // 2 // tc

    def body(place_ref, *refs):
        for a in range(n):
            refs[2 * n + a][...] = (refs[a][...] + refs[n + a][...]).astype(BF16)

    gspec = pl.BlockSpec((None, tr, tc), lambda j, i, k, pr: (j, i, pr[0] * nct + k))
    rspec = pl.BlockSpec((None, tr, tc), lambda j, i, k, pr: (j, i, k))
    grid_spec = pltpu.PrefetchScalarGridSpec(num_scalar_prefetch=1, grid=(NCHIP, r // tr, nct),
                                             in_specs=[gspec] * n + [rspec] * n, out_specs=[rspec] * n)
    return pl.pallas_call(body, name=name, grid_spec=grid_spec, out_shape=[SDS((NCHIP, r, c // 2), BF16)] * n,
                          compiler_params=_cparams(("arbitrary",) * 3, VMEM_BIG))(place, *gs, *rs)


def _sum_chips(ss, qs, place, l, nl, prevs, name):
    n = len(ss)
    _, r, h = ss[0].shape
    tr, tc = _tile2(r, h, 3 * n, 16)
    nct = h // tc

    def body(place_ref, *refs):
        for a in range(n):
            acc = refs[n + a][...].astype(F32)
            for kk in range(NCHIP - 1):
                acc = acc + refs[a][kk].astype(F32)
            refs[-n + a][...] = acc

    in_specs = ([pl.BlockSpec((NCHIP - 1, tr, tc), lambda i, k, pr: (0, i, k))] * n
                + [pl.BlockSpec((None, tr, tc), lambda i, k, pr: (pr[1], i, k))] * n)
    args = [*ss, *qs]
    aliases = {}
    if prevs is not None:
        aliases = {1 + len(args) + a: a for a in range(n)}
        in_specs += [ANY] * n
        args += list(prevs)
    grid_spec = pltpu.PrefetchScalarGridSpec(
        num_scalar_prefetch=1, grid=(r // tr, nct), in_specs=in_specs,
        out_specs=[pl.BlockSpec((None, tr, tc), lambda i, k, pr: (l, i, pr[0] * nct + k))] * n)
    return pl.pallas_call(body, name=name, grid_spec=grid_spec, out_shape=[SDS((nl, r, 2 * h), F32)] * n,
                          input_output_aliases=aliases,
                          compiler_params=_cparams(("arbitrary",) * 2, VMEM_BIG))(place, *args)


def _cast_place(ws, place, name):
    n = len(ws)
    nl, r, c = ws[0].shape
    tr, tc = _tile2(r, c, 2 * n * nl, 16)

    def body(place_ref, *refs):
        for a in range(n * nl):
            refs[n * nl + a][...] = refs[a][...].astype(BF16)

    in_specs = [pl.BlockSpec((None, tr, tc), functools.partial(lambda l, i, k, pr: (l, i, k), l))
                for _ in range(n) for l in range(nl)]
    grid_spec = pltpu.PrefetchScalarGridSpec(
        num_scalar_prefetch=1, grid=(r // tr, c // tc), in_specs=in_specs,
        out_specs=[pl.BlockSpec((None, tr, tc), lambda i, k, pr: (pr[1], i, k))] * (n * nl))
    outs = pl.pallas_call(body, name=name, grid_spec=grid_spec, out_shape=[SDS((NCHIP, r, c), BF16)] * (n * nl),
                          compiler_params=_cparams(("arbitrary",) * 2, VMEM_BIG))(
                              place, *[w for w in ws for _ in range(nl)])
    return [outs[a * nl:(a + 1) * nl] for a in range(n)]


def _adamw(ws, gs, ms, vs, name):
    n = len(ws)
    r, c = ws[0].shape
    tr, tc = _tile2(r, c, 7 * n, 8)
    c1, c2 = 1.0 / (1.0 - B1 ** STEP), 1.0 / (1.0 - B2 ** STEP)

    def body(*refs):
        for a in range(n):
            w, g, m, v = (refs[kk * n + a][...] for kk in range(4))
            m2 = B1 * m + (1.0 - B1) * g
            v2 = B2 * v + (1.0 - B2) * (g * g)
            refs[4 * n + a][...] = -LR * ((m2 * c1) / (jnp.sqrt(v2 * c2) + EPS_ADAM) + WD * w)
            refs[5 * n + a][...] = m2
            refs[6 * n + a][...] = v2

    blk = pl.BlockSpec((tr, tc), lambda i, k: (i, k))
    outs = pl.pallas_call(body, name=name, grid=(r // tr, c // tc), in_specs=[blk] * (4 * n),
                          out_specs=[blk] * (3 * n), out_shape=[SDS((r, c), F32)] * (3 * n),
                          compiler_params=_cparams(("arbitrary",) * 2, VMEM_BIG))(*ws, *gs, *ms, *vs)
    return outs[:n], outs[n:2 * n], outs[2 * n:]


def _place():
    x, y, c = lax.axis_index("x"), lax.axis_index("y"), lax.axis_index("c")
    return x, y, c, [(1 - x, y), (x, 1 - y), (1 - x, 1 - y)]


def _rcopy(src, dst, ssem, rsem, k, dev):
    return pltpu.make_async_remote_copy(src_ref=src, dst_ref=dst, send_sem=ssem.at[k], recv_sem=rsem.at[k],
                                        device_id=dev, device_id_type=MESH)


def _half(ref, lead, cc):
    h = ref.shape[-1] // 2
    return ref.at[(*lead, slice(None), pl.ds(cc * h, h))]


def _per_core(fn):
    c = lax.axis_index("c")
    for cc in (0, 1):
        pl.when(c == cc)(functools.partial(fn, cc))


def _gather_comm(bufs):
    n = len(bufs)

    def plan(couts, ssem, rsem, cc):
        x, y, _, peers = _place()
        me, sib = 2 * x + y, (x, y, 1 - cc)
        send, recv, fwd, recv2 = [], [], [], []
        for a in range(n):
            for kk, (px, py) in enumerate(peers):
                mine = _half(couts[a], (me,), cc)
                got = _half(couts[a], (2 * px + py,), cc)
                other = _half(couts[a], (2 * px + py,), 1 - cc)
                send.append(_rcopy(mine, mine, ssem, rsem, a * 6 + kk, (px, py, cc)))
                recv.append(_rcopy(got, got, ssem, rsem, a * 6 + kk, (px, py, cc)))
                fwd.append(_rcopy(got, got, ssem, rsem, a * 6 + 3 + kk, sib))
                recv2.append(_rcopy(other, other, ssem, rsem, a * 6 + 3 + kk, sib))
        return send, recv, fwd, recv2

    def start(cins, couts, ssem, rsem):
        def go(cc):
            for d in plan(couts, ssem, rsem, cc)[0]:
                d.start()

        _per_core(go)

    def mid(cins, couts, ssem, rsem):
        def go(cc):
            _, recv, fwd, _ = plan(couts, ssem, rsem, cc)
            for dr, df in zip(recv, fwd):
                dr.wait_recv()
                df.start()

        _per_core(go)

    def finish(cins, couts, ssem, rsem):
        def go(cc):
            send, _, fwd, recv2 = plan(couts, ssem, rsem, cc)
            for d in recv2:
                d.wait_recv()
            for d in send + fwd:
                d.wait_send()

        _per_core(go)

    return _Comm(bufs, [SDS(b.shape, b.dtype) for b in bufs], {a: a for a in range(n)}, 6 * n, start, finish, mid)


def _pair_comm(gs):
    n = len(gs)
    halves = [g.shape[-1] // 2 for g in gs]

    def plan(cins, couts, ssem, rsem, cc):
        x, y, _, _ = _place()
        return [_rcopy(cins[a].at[:, :, pl.ds((1 - cc) * halves[a], halves[a])], couts[a], ssem, rsem, a, (x, y, 1 - cc))
                for a in range(n)]

    def start(cins, couts, ssem, rsem):
        def go(cc):
            for d in plan(cins, couts, ssem, rsem, cc):
                d.start()

        _per_core(go)

    def finish(cins, couts, ssem, rsem):
        def go(cc):
            ds = plan(cins, couts, ssem, rsem, cc)
            for d in ds:
                d.wait_recv()
            for d in ds:
                d.wait_send()

        _per_core(go)

    return _Comm(gs, [SDS(g.shape[:-1] + (g.shape[-1] // 2,), g.dtype) for g in gs], {}, n, start, finish)


def _chips_comm(qs):
    n = len(qs)

    def plan(cins, couts, ssem, rsem):
        x, y, c, peers = _place()
        return [_rcopy(cins[a].at[2 * px + py], couts[a].at[kk], ssem, rsem, a * 3 + kk, (px, py, c))
                for a in range(n) for kk, (px, py) in enumerate(peers)]

    def start(cins, couts, ssem, rsem):
        for d in plan(cins, couts, ssem, rsem):
            d.start()

    def finish(cins, couts, ssem, rsem):
        ds = plan(cins, couts, ssem, rsem)
        for d in ds:
            d.wait_recv()
        for d in ds:
            d.wait_send()

    return _Comm(qs, [SDS((NCHIP - 1,) + q.shape[1:], q.dtype) for q in qs], {}, 3 * n, start, finish)


def _share_comm(fs, l):
    n = len(fs)

    def plan(couts, ssem, rsem, cc, which):
        x, y, _, _ = _place()
        out = []
        for a in range(n):
            piece = _half(couts[a], (l,), which)
            out.append(_rcopy(piece, piece, ssem, rsem, a, (x, y, 1 - cc)))
        return out

    def start(cins, couts, ssem, rsem):
        def go(cc):
            for d in plan(couts, ssem, rsem, cc, cc):
                d.start()

        _per_core(go)

    def finish(cins, couts, ssem, rsem):
        def go(cc):
            for d in plan(couts, ssem, rsem, cc, 1 - cc):
                d.wait_recv()
            for d in plan(couts, ssem, rsem, cc, cc):
                d.wait_send()

        _per_core(go)

    return _Comm(fs, [SDS(f.shape, f.dtype) for f in fs], {a: a for a in range(n)}, n, start, finish)


def _spread_comm(buf):
    def plan(cins, couts, ssem, rsem):
        x, y, c, _ = _place()
        me = 4 * x + 2 * y + c
        send, recv = [], []
        for rel in range(1, N_DEV):
            px, py, pc = (1 - x if rel & 4 else x, 1 - y if rel & 2 else y, 1 - c if rel & 1 else c)
            slot = couts[0].at[4 * px + 2 * py + pc]
            send.append(_rcopy(cins[0], couts[0].at[me], ssem, rsem, rel - 1, (px, py, pc)))
            recv.append(_rcopy(slot, slot, ssem, rsem, rel - 1, (px, py, pc)))
        own = pltpu.make_async_copy(cins[0], couts[0].at[me], ssem.at[N_DEV - 1])
        return send, recv, own

    def start(cins, couts, ssem, rsem):
        send, _, own = plan(cins, couts, ssem, rsem)
        for dsc in send + [own]:
            dsc.start()

    def finish(cins, couts, ssem, rsem):
        send, recv, own = plan(cins, couts, ssem, rsem)
        for dsc in recv:
            dsc.wait_recv()
        for dsc in send:
            dsc.wait_send()
        own.wait()

    return _Comm([buf], [SDS((N_DEV,) + buf.shape, buf.dtype)], {}, N_DEV, start, finish)


def _sum_devices(blocks, name):
    _, r, w = blocks.shape

    def body(in_ref, out_ref):
        acc = in_ref[0]
        for dv in range(1, N_DEV):
            acc = acc + in_ref[dv]
        out_ref[...] = acc

    vm = pl.BlockSpec(memory_space=pltpu.VMEM)
    return pl.pallas_call(body, name=name, in_specs=[vm], out_specs=vm, out_shape=SDS((r, w), F32),
                          compiler_params=_cparams(None, VMEM_BIG))(blocks)


def _cols_full(g):
    _, k, ns = g.shape
    return g.transpose(1, 0, 2).reshape(k, NCHIP * ns)


def _cols_shards(w):
    k, n = w.shape
    return w.reshape(k, NCHIP, n // NCHIP).transpose(1, 0, 2)


def _rows_pad(rows, width):
    out = jnp.concatenate([jnp.pad(a, ((0, 0), (0, width - a.shape[1]))) for a in rows], axis=0)
    return jnp.pad(out, ((0, -out.shape[0] % 8), (0, 0)))


def kernel(x, positions, ffn1_norm, ffn1_w_gate, ffn1_w_up, ffn1_w_down, mix_norm, w_in, gate_bias, conv_w, conv_b, conv_ln_g, conv_ln_b, w_conv_out, cq_norm, ckv_norm, w_uq, w_ukv, q_norm, k_norm, w_mla_out, w_out, ffn2_norm, ffn2_w_gate, ffn2_w_up, ffn2_w_down, loss_target, m_ffn1_norm, m_ffn1_w_gate, m_ffn1_w_up, m_ffn1_w_down, m_mix_norm, m_w_in, m_gate_bias, m_conv_w, m_conv_b, m_conv_ln_g, m_conv_ln_b, m_w_conv_out, m_cq_norm, m_ckv_norm, m_w_uq, m_w_ukv, m_q_norm, m_k_norm, m_w_mla_out, m_w_out, m_ffn2_norm, m_ffn2_w_gate, m_ffn2_w_up, m_ffn2_w_down, v_ffn1_norm, v_ffn1_w_gate, v_ffn1_w_up, v_ffn1_w_down, v_mix_norm, v_w_in, v_gate_bias, v_conv_w, v_conv_b, v_conv_ln_g, v_conv_ln_b, v_w_conv_out, v_cq_norm, v_ckv_norm, v_w_uq, v_w_ukv, v_q_norm, v_k_norm, v_w_mla_out, v_w_out, v_ffn2_norm, v_ffn2_w_gate, v_ffn2_w_up, v_ffn2_w_down):
    names = ["ffn1_norm", "ffn1_w_gate", "ffn1_w_up", "ffn1_w_down", "mix_norm", "w_in", "gate_bias", "conv_w",
             "conv_b", "conv_ln_g", "conv_ln_b", "w_conv_out", "cq_norm", "ckv_norm", "w_uq", "w_ukv", "q_norm",
             "k_norm", "w_mla_out", "w_out", "ffn2_norm", "ffn2_w_gate", "ffn2_w_up", "ffn2_w_down"]
    env = dict(locals())
    turned = ("ffn1_w_gate", "ffn1_w_up", "ffn2_w_gate", "ffn2_w_up", "w_in", "w_uq")
    view = lambda nm, a: jnp.swapaxes(a, 1, 2) if nm in turned else a
    wts = {nm: view(nm, env[nm]) for nm in names}
    mom = {nm: view(nm, env["m_" + nm]) for nm in names}
    var = {nm: view(nm, env["v_" + nm]) for nm in names}

    t, d = x.shape[1], x.shape[2]
    nl = ffn1_norm.shape[0]
    cc = conv_b.shape[1]
    ql, kvl = cq_norm.shape[1], ckv_norm.shape[1]
    vw = N_HEADS * V_DIM
    hw = N_HEADS * HP
    lay = {"a": 2 * d, "cq": 2 * d + 2 * cc, "ckv": 2 * d + 2 * cc + ql, "kr": 2 * d + 2 * cc + ql + kvl,
           "wm": ql + kvl + 2 * HP}
    dp = lay["cq"] + lay["wm"]
    nat_g = 2 * cc + ql + kvl + ROPE
    assert lay["cq"] % lay["wm"] == 0 and lay["cq"] % ql == 0 and lay["ckv"] % kvl == 0 and lay["a"] % (2 * cc) == 0
    assert cc % HP == 0 and d % HP == 0 and t % (2 * CHUNK) == 0 and w_in.shape[2] * NCHIP == nat_g + 2 * d
    assert nl == 2

    x0, target = x[0], loss_target[0]
    chip = 2 * lax.axis_index("x") + lax.axis_index("y")
    place = jnp.stack([lax.axis_index("c"), chip]).astype(jnp.int32)

    inv_freq = ROPE_THETA ** (-jnp.arange(0, ROPE, 2, dtype=F32) / ROPE)
    ang = positions[0].astype(F32)[:, None] * inv_freq
    cos, sin, z = jnp.cos(ang), jnp.sin(ang), jnp.zeros((t, ROPE // 2), F32)
    tabs = (jnp.concatenate([jnp.ones((t, NOPE), F32), cos, cos, jnp.zeros((t, HP - QK), F32)], axis=1),
            jnp.concatenate([jnp.zeros((t, NOPE), F32), -sin, z, jnp.zeros((t, HP - QK), F32)], axis=1),
            jnp.concatenate([jnp.zeros((t, NOPE), F32), z, sin, jnp.zeros((t, HP - QK), F32)], axis=1))

    big = ["ffn1_w_gate", "ffn1_w_up", "ffn1_w_down", "ffn2_w_gate", "ffn2_w_up", "ffn2_w_down",
           "w_in", "w_conv_out", "w_uq", "w_ukv", "w_mla_out", "w_out"]
    ffn1_w, ffn2_w, mix_w = big[0:3], big[3:6], big[6:]
    bufs, like = {}, {}
    for nm in big:
        like.setdefault(wts[nm].shape, []).append(nm)
    for gi, grp in enumerate(like.values()):
        for nm, per_layer in zip(grp, _cast_place([wts[nm] for nm in grp], place, f"cast_place_{gi}")):
            for l in range(nl):
                bufs[nm, l] = per_layer[l]

    def gather(keys):
        return _gather_comm([bufs[key] for key in keys])

    def landed(keys, outs):
        for key, o in zip(keys, outs):
            bufs[key] = o

    chunk = lambda nms, l: [(nm, l) for nm in nms]
    first_keys = chunk(ffn1_w, 0)
    landed(first_keys, _comm_call(gather(first_keys), "gather_first"))
    ride = {("ffn1", 0): chunk(mix_w, 0), ("flash", 0): chunk(ffn2_w, 0) + chunk(ffn1_w, 1),
            ("ffn2", 0): chunk(mix_w, 1), ("ffn1", 1): chunk(ffn2_w, 1)}

    def riding(kind, l):
        keys = ride.get((kind, l))
        return keys or [], (None if keys is None else gather(keys))

    def mixer_weights(l):
        w_in_nat = bufs["w_in", l].reshape(nat_g + 2 * d, d)
        w_in_k = jnp.concatenate([w_in_nat[nat_g:], w_in_nat[:nat_g], jnp.zeros((dp - nat_g - 2 * d, d), BF16)], axis=0)
        w_uq_k = jnp.pad(bufs["w_uq", l].reshape(N_HEADS, QK, ql), ((0, 0), (0, HP - QK), (0, 0))).reshape(hw, ql)
        w_mo_k = jnp.pad(_cols_full(bufs["w_mla_out", l]).reshape(N_HEADS, V_DIM, d),
                         ((0, 0), (HP - V_DIM, 0), (0, 0))).reshape(hw, d)
        return dict(w_in=w_in_k, w_co=_cols_full(bufs["w_conv_out", l]), w_uq=w_uq_k,
                    w_ukv=_cols_full(bufs["w_ukv", l]), w_mo=w_mo_k, w_out=bufs["w_out", l].reshape(d, d))

    qn_k = jnp.pad(q_norm, ((0, 0), (0, HP - QK)))
    kn_k = jnp.pad(k_norm, ((0, 0), (0, HP - QK)))
    small = _rows_pad([gate_bias.reshape(nl * 2, d // NCHIP), conv_w.reshape(nl * KW, cc // NCHIP)], d)

    saved, mixw = [], []
    xc = x0
    for l in range(nl):
        keys, comm = riding("ffn1", l)
        if l == 0:
            comm = _join(comm, _spread_comm(small))
        x1, gt1, up1, got = _ffn_fwd(xc, ffn1_norm, l, *[bufs[nm, l] for nm in ffn1_w], f"ffn1_fwd_{l}", comm,
                                     0.75 if l == 0 else 0.9)
        landed(keys, got)
        if l == 0:
            everyone = got[-1][0::2]
            gb_k = everyone[:, :nl * 2, :d // NCHIP].reshape(NCHIP, nl, 2, d // NCHIP).transpose(1, 2, 0, 3).reshape(nl, 2, d)
            cw = everyone[:, nl * 2:nl * 2 + nl * KW, :cc // NCHIP].reshape(NCHIP, nl, KW, cc // NCHIP)
            cw_k = jnp.pad(cw.transpose(1, 2, 0, 3).reshape(nl, KW, cc), ((0, 0), (0, CPAD - KW), (0, 0)))
        mw = mixer_weights(l)
        mixw.append(mw)
        hm = _rms_fwd(x1, mix_norm, l, f"mix_norm_{l}")
        proj = _mm(hm, mw["w_in"], "nt", name=f"proj_{l}", out_dtype=BF16)
        uc = _dwconv_fwd(proj, cw_k, conv_b, l, lay["a"], cc, f"dwconv_{l}")
        sc = _ln_silu(uc, conv_ln_g, conv_ln_b, l, f"conv_ln_{l}")
        yc = _mm(sc, mw["w_co"], "nn", name=f"conv_out_{l}", out_dtype=BF16)
        q, k, kv = _mla_pre_fwd(proj, lay, l, cq_norm, ckv_norm, qn_k, kn_k, mw["w_uq"], mw["w_ukv"], tabs, f"mla_pre_{l}")
        keys, comm = riding("flash", l)
        o, lse, got = _flash_fwd(q, k, kv, f"flash_{l}", comm)
        landed(keys, got)
        ym = _mm(o, mw["w_mo"], "nn", name=f"mla_out_{l}", out_dtype=BF16)
        yv = _comb_fwd(proj, gb_k, l, yc, ym, f"combine_{l}")
        x2 = _mm(yv, mw["w_out"], "nn", name=f"mix_out_{l}", res=x1)
        keys, comm = riding("ffn2", l)
        x3, gt2, up2, got = _ffn_fwd(x2, ffn2_norm, l, *[bufs[nm, l] for nm in ffn2_w], f"ffn2_fwd_{l}", comm)
        landed(keys, got)
        saved.append(dict(x0=xc, x1=x1, gt1=gt1, up1=up1, hm=hm, proj=proj, uc=uc, sc=sc, yc=yc, q=q, k=k, kv=kv,
                          o=o, lse=lse, ym=ym, yv=yv, x2=x2, gt2=gt2, up2=up2))
        xc = x3
    dx, loss_row = _loss_grad(xc, target, "loss")

    gw = {nm: [None] * nl for nm in names}
    red = {}

    def shard_groups(nms, l):
        same = {}
        for nm in nms:
            same.setdefault(gw[nm][l].shape, []).append(nm)
        return list(same.values())

    def add_parts(nms, l, sib_part, tag):
        qb = {}
        for gi, grp in enumerate(shard_groups(nms, l)):
            outs = _add_cast([gw[nm][l] for nm in grp], [sib_part[nm] for nm in grp], place, f"rs_add_{tag}_{gi}")
            qb.update(zip(grp, outs))
        return qb

    def sum_parts(nms, l, qb, parts, tag):
        for gi, grp in enumerate(shard_groups(nms, l)):
            prevs = [red[nm] for nm in grp] if grp[0] in red else None
            outs = _sum_chips([parts[nm] for nm in grp], [qb[nm] for nm in grp], place, l, nl, prevs, f"rs_sum_{tag}_{gi}")
            red.update(zip(grp, outs))

    pair = lambda nms, l: _pair_comm([gw[nm][l] for nm in nms])
    chips = lambda nms, qb: _chips_comm([qb[nm] for nm in nms])
    share = lambda nms, l: _share_comm([red[nm] for nm in nms], l)

    def split(joined, got):
        return [got[o0:o1] for o0, o1 in joined.spans]

    mix_parts = [["w_in"], ["w_out", "w_conv_out"], ["w_mla_out", "w_ukv", "w_uq"]]

    for l in reversed(range(nl)):
        s, mw = saved[l], mixw[l]
        riding_rs = l == 0

        def ffn_back(tag, xin, gains, gts, ups, dout, comm=None):
            wg, wu, wd = (bufs[f"{tag}_w_{p}", l] for p in ("gate", "up", "down"))
            dxi, dgain, dgt, dup, act, hb, dob, got = _ffn_bwd(xin, gains, l, wg, wu, wd, gts, ups, dout,
                                                              f"{tag}_bwd_{l}", comm)
            for p, lhs, rhs in zip(("gate", "up", "down"), (dgt, dup, act), (hb, hb, dob)):
                gw[f"{tag}_w_{p}"][l] = _mm(lhs, rhs, "tn", name=f"{tag}_dw_{p}_{l}", a_g=True)
            gw[f"{tag}_norm"][l] = dgain
            return dxi, got

        dx2, got = ffn_back("ffn2", s["x2"], ffn2_norm, s["gt2"], s["up2"], dx, pair(big, 1) if riding_rs else None)
        if riding_rs:
            qb1 = add_parts(big, 1, dict(zip(big, got)), "l1")
        dyv = _mm(dx2, mw["w_out"], "nt", name=f"mix_out_dy_{l}", out_dtype=BF16)
        gw["w_out"][l] = _mm(s["yv"], dx2, "tn", name=f"mix_out_dw_{l}").reshape(NCHIP, d // NCHIP, d)
        dyc, dym, dproj, dgb = _comb_bwd(s["proj"], gb_k, l, s["yc"], s["ym"], dyv, dp, f"combine_bwd_{l}")
        gw["gate_bias"][l] = dgb
        dsc = _mm(dyc, mw["w_co"], "nt", name=f"conv_out_ds_{l}")
        gw["w_conv_out"][l] = _cols_shards(_mm(s["sc"], dyc, "tn", name=f"conv_out_dw_{l}"))
        duc, gw["conv_ln_g"][l], gw["conv_ln_b"][l], gw["conv_b"][l] = _ln_silu_bwd(
            s["uc"], conv_ln_g, conv_ln_b, l, dsc, f"conv_ln_bwd_{l}")
        du, dcw = _dwconv_bwd(duc, s["proj"], cw_k, l, lay["a"], cc, f"dwconv_bwd_{l}")
        gw["conv_w"][l] = dcw[:KW]
        dproj = _glu_bwd(du, s["proj"], dproj, lay["a"], cc, f"glu_bwd_{l}")
        do = _mm(dym, mw["w_mo"], "nt", name=f"mla_out_do_{l}", out_dtype=BF16)
        dwmo = _mm(s["o"], dym, "tn", name=f"mla_out_dw_{l}").reshape(N_HEADS, HP, d)[:, HP - V_DIM:].reshape(vw, d)
        gw["w_mla_out"][l] = _cols_shards(dwmo)
        comm = _join(chips(big, qb1), pair(ffn2_w, 0)) if riding_rs else None
        dq, dk, dv, got = _flash_bwd(s["q"], s["k"], s["kv"], do, s["o"], s["lse"], f"flash_bwd_{l}", comm)
        if riding_rs:
            parts1, sib_f2 = split(comm, got)
            sum_parts(big, 1, qb1, dict(zip(big, parts1)), "l1")
            qb_f2 = add_parts(ffn2_w, 0, dict(zip(ffn2_w, sib_f2)), "f2")
        dproj, dqr, dkv, cqn, ckvn, gw["cq_norm"][l], gw["ckv_norm"][l], dqn, dkn = _mla_pre_bwd(
            s["proj"], lay, l, cq_norm, ckv_norm, qn_k, kn_k, mw["w_uq"], mw["w_ukv"], tabs, dq, dk, dv, dproj,
            f"mla_pre_bwd_{l}")
        gw["q_norm"][l], gw["k_norm"][l] = dqn[:, :QK], dkn[:, :QK]
        dwuq = _mm(dqr, cqn, "tn", name=f"uq_dw_{l}").reshape(N_HEADS, HP, ql)[:, :QK]
        gw["w_uq"][l] = dwuq.reshape(NCHIP, N_HEADS * QK // NCHIP, ql)
        gw["w_ukv"][l] = _cols_shards(_mm(ckvn, dkv, "tn", name=f"ukv_dw_{l}"))
        dhm = _mm(dproj, mw["w_in"], "nn", name=f"proj_dh_{l}")
        dwin = _mm(dproj, s["hm"], "tn", name=f"proj_dw_{l}")
        gw["w_in"][l] = jnp.concatenate([dwin[2 * d:2 * d + nat_g], dwin[:2 * d]], axis=0).reshape(
            NCHIP, (nat_g + 2 * d) // NCHIP, d)
        dx1, gw["mix_norm"][l] = _rms_back(s["x1"], mix_norm, l, dhm, dx2, f"mix_norm_bwd_{l}")
        if not riding_rs:
            dx, _ = ffn_back("ffn1", s["x0"], ffn1_norm, s["gt1"], s["up1"], dx1)
            continue
        comm = _join(share(big, 1), chips(ffn2_w, qb_f2), pair(mix_w, 0))
        wg, wu, wd = (bufs[f"ffn1_w_{p}", l] for p in ("gate", "up", "down"))
        dx, dgain, dgt, dup, act, hb, dob, got = _ffn_bwd(s["x0"], ffn1_norm, l, wg, wu, wd, s["gt1"], s["up1"], dx1,
                                                         f"ffn1_bwd_{l}", comm)
        gw["ffn1_norm"][l] = dgain
        shared, parts_f2, sib_mx = split(comm, got)
        red.update(zip(big, shared))
        sum_parts(ffn2_w, 0, qb_f2, dict(zip(ffn2_w, parts_f2)), "f2")
        qb_mx = add_parts(mix_w, 0, dict(zip(mix_w, sib_mx)), "mx")
        vec_names = ["ffn1_norm", "mix_norm", "ffn2_norm", "conv_b", "conv_ln_g", "conv_ln_b", "cq_norm", "ckv_norm",
                     "q_norm", "k_norm"]
        rows = [jnp.concatenate(gw[nm], axis=0) for nm in vec_names]
        rows += [jnp.concatenate(gw["gate_bias"], axis=0), jnp.concatenate(gw["conv_w"], axis=0), loss_row]
        parts_mx, f1 = {}, [f"ffn1_w_{p}" for p in ("gate", "up", "down")]
        qb_f1 = {}
        for part, (lhs, rhs) in enumerate(zip((dgt, dup, act), (hb, hb, dob))):
            cms = [chips(mix_parts[part], qb_mx)]
            if part == 0:
                cms.append(_spread_comm(_rows_pad(rows, d)))
            if part >= 1:
                cms.append(pair(f1[part - 1:part], 0))
            if part == 2:
                cms.append(chips(f1[0:1], qb_f1))
            comm = _join(*cms)
            gw[f1[part]][l], got = _mm(lhs, rhs, "tn", name=f"{f1[part]}_dw_{l}", a_g=True, comm=comm)
            pieces = split(comm, got)
            parts_mx.update(zip(mix_parts[part], pieces[0]))
            if part == 0:
                small_blocks = pieces[1][0]
            if part >= 1:
                qb_f1.update(add_parts(f1[part - 1:part], 0, dict(zip(f1[part - 1:part], pieces[1])), f"f1{part - 1}"))
            if part == 2:
                sum_parts(f1[0:1], 0, qb_f1, dict(zip(f1[0:1], pieces[2])), "f10")
        sum_parts(mix_w, 0, qb_mx, parts_mx, "mx")

    sib = dict(zip(f1[2:3], _comm_call(pair(f1[2:3], 0), "rs_pair")))
    qb_f1.update(add_parts(f1[2:3], 0, sib, "f12"))
    parts_f1 = dict(zip(f1[1:3], _comm_call(chips(f1[1:3], qb_f1), "rs_chips")))
    sum_parts(f1[1:3], 0, qb_f1, parts_f1, "f1")
    red = dict(zip(big, _comm_call(share(big, 0), "rs_share")))
    grads = {nm: g.reshape(wts[nm].shape) for nm, g in red.items()}

    total = _sum_devices(small_blocks, "allreduce_small")
    r0 = 0
    for nm in vec_names:
        grads[nm] = total[r0:r0 + nl, :wts[nm].shape[1]]
        r0 += nl
    gb_all = total[r0:r0 + 2 * nl, :d].reshape(nl, 2, NCHIP, d // NCHIP)
    r0 += 2 * nl
    cw_all = total[r0:r0 + KW * nl, :cc].reshape(nl, KW, NCHIP, cc // NCHIP)
    r0 += KW * nl
    grads["gate_bias"] = lax.dynamic_index_in_dim(gb_all, chip, axis=2, keepdims=False)
    grads["conv_w"] = lax.dynamic_index_in_dim(cw_all, chip, axis=2, keepdims=False)
    loss = total[r0, 0]

    delta, new_m, new_v = {}, {}, {}
    by_shape = {}
    for nm in names:
        shp = wts[nm].shape
        by_shape.setdefault((shp[0] * (shp[1] if len(shp) == 3 else 1), shp[-1]), []).append(nm)
    for gi, (shp2, grp) in enumerate(by_shape.items()):
        to2 = lambda a: a.reshape(shp2)
        ds_, ms_, vs_ = _adamw([to2(wts[nm]) for nm in grp], [to2(grads[nm]) for nm in grp],
                               [to2(mom[nm]) for nm in grp], [to2(var[nm]) for nm in grp], f"adamw_{gi}")
        for nm, dd, mm_, vv in zip(grp, ds_, ms_, vs_):
            delta[nm], new_m[nm], new_v[nm] = (a.reshape(wts[nm].shape) for a in (dd, mm_, vv))

    return (loss, dx[None], *[view(nm, grads[nm]) for nm in names], *[view(nm, delta[nm]) for nm in names],
            *[view(nm, new_m[nm]) for nm in names], *[view(nm, new_v[nm]) for nm in names])
```

```python
import functools

import jax
import jax.numpy as jnp
from jax import lax
from jax.experimental import pallas as pl
from jax.experimental.pallas import tpu as pltpu

F32, BF16 = jnp.float32, jnp.bfloat16
SDS = jax.ShapeDtypeStruct
MESH = pl.DeviceIdType.MESH
ANY = pl.BlockSpec(memory_space=pl.ANY)

NCHIP = 4
N_DEV = 8
N_HEADS, NOPE, ROPE, V_DIM = 8, 64, 32, 64
QK = NOPE + ROPE
HP = 128
CHUNK = 64
KW = 31
CPAD = 32
ROPE_THETA = 10000.0
EPS = 1e-6
LR, B1, B2, EPS_ADAM, WD, STEP = 0.001, 0.9, 0.999, 1e-08, 0.01, 10
VMEM_BIG = 48 << 20
FFN_PARTS = 2
FLASH_HEADS = 2


def _cparams(sem=None, vmem=None):
    kw = {}
    if sem is not None:
        kw["dimension_semantics"] = sem
    if vmem is not None:
        kw["vmem_limit_bytes"] = vmem
    return pltpu.CompilerParams(**kw)


def _rt(t):
    return min(512, t // 2)


def _pick(n, cands):
    for c in cands:
        if c <= n and n % c == 0:
            return c
    return n


def _tile2(r, c, nblocks, row_mult):
    tr, tc = r, c
    while 2 * nblocks * tr * tc * 4 > VMEM_BIG // 2 and tr % (2 * row_mult) == 0:
        tr //= 2
    while 2 * nblocks * tr * tc * 4 > VMEM_BIG // 2 and tc % 256 == 0:
        tc //= 2
    return tr, tc


def _sig(v):
    return 1.0 / (1.0 + jnp.exp(-v))


def _rms_r(v):
    return lax.rsqrt(jnp.mean(v * v, axis=-1, keepdims=True) + EPS)


def _rms_bwd(xv, g, dy):
    r = _rms_r(xv)
    xh = xv * r
    dg = jnp.sum(dy * xh, axis=0, keepdims=True)
    dxh = dy * g
    dx = r * (dxh - xh * jnp.mean(dxh * xh, axis=-1, keepdims=True))
    return dx, dg


def _dot(a, b):
    return jnp.dot(a, b, preferred_element_type=F32)


def _dot_nt(a, b):
    return lax.dot_general(a, b, (((1,), (1,)), ((), ())), preferred_element_type=F32)


def _dot_tn(a, b):
    return lax.dot_general(a, b, (((0,), (0,)), ((), ())), preferred_element_type=F32)


class _Comm:
    def __init__(self, ins, out_shapes, aliases, nsem, start, finish, mid=None):
        self.ins, self.out_shapes, self.aliases, self.nsem = list(ins), list(out_shapes), dict(aliases), nsem
        self.start, self.finish, self.mid = start, finish, mid


def _call(body, comm, first, last, *, name, grid, in_specs, args, out_specs, out_shape, scratch, sem, mid_at=None):
    in_specs, args, out_specs, out_shape, scratch = map(list, (in_specs, args, out_specs, out_shape, scratch))
    n_in, n_out, n_scr = len(args), len(out_shape), len(scratch)
    aliases = {}
    kern = body
    if comm is not None:
        nci, nco = len(comm.ins), len(comm.out_shapes)
        o0 = n_in + nci
        s0 = o0 + n_out + nco

        def kern(*refs):
            cins, couts = refs[n_in:o0], refs[o0 + n_out:s0]
            ssem, rsem = refs[s0 + n_scr:]
            pl.when(first())(lambda: comm.start(cins, couts, ssem, rsem))
            if comm.mid is not None and mid_at is not None:
                pl.when(mid_at())(lambda: comm.mid(cins, couts, ssem, rsem))
            body(*refs[:n_in], *refs[o0:o0 + n_out], *refs[s0:s0 + n_scr])

            def end():
                if comm.mid is not None and mid_at is None:
                    comm.mid(cins, couts, ssem, rsem)
                comm.finish(cins, couts, ssem, rsem)

            pl.when(last())(end)

        in_specs += [ANY] * nci
        args += comm.ins
        out_specs += [ANY] * nco
        out_shape += comm.out_shapes
        scratch += [pltpu.SemaphoreType.DMA((comm.nsem,))] * 2
        aliases = {n_in + i: n_out + o for i, o in comm.aliases.items()}
    outs = pl.pallas_call(kern, name=name, grid=grid, in_specs=in_specs, out_specs=out_specs, out_shape=out_shape,
                          scratch_shapes=scratch, input_output_aliases=aliases,
                          compiler_params=_cparams(sem, VMEM_BIG))(*args)
    return outs[:n_out], outs[n_out:]


class _SemView:
    def __init__(self, sems, base):
        self.sems, self.base = sems, base

    @property
    def at(self):
        return self

    def __getitem__(self, k):
        return self.sems.at[self.base + k]


def _join(*comms):
    ins, outs, aliases, spans, nsem = [], [], {}, [], 0
    for cm in comms:
        aliases.update({len(ins) + i: len(outs) + o for i, o in cm.aliases.items()})
        spans.append((len(ins), len(ins) + len(cm.ins), len(outs), len(outs) + len(cm.out_shapes), nsem))
        ins += cm.ins
        outs += cm.out_shapes
        nsem += cm.nsem

    def run(which, cins, couts, ssem, rsem):
        for cm, (i0, i1, o0, o1, base) in zip(comms, spans):
            if getattr(cm, which) is not None:
                getattr(cm, which)(cins[i0:i1], couts[o0:o1], _SemView(ssem, base), _SemView(rsem, base))

    joined = _Comm(ins, outs, aliases, nsem, functools.partial(run, "start"), functools.partial(run, "finish"),
                   functools.partial(run, "mid") if any(cm.mid is not None for cm in comms) else None)
    joined.spans = [(o0, o1) for _, _, o0, o1, _ in spans]
    return joined


def _comm_call(comm, name):
    once = lambda: pl.program_id(0) == 0
    return _call(lambda: None, comm, once, once, name=name, grid=(1,), in_specs=[], args=[], out_specs=[],
                 out_shape=[], scratch=[], sem=("arbitrary",))[1]


def _mm(a, b, mode, *, name, out_dtype=F32, res=None, a_g=False, comm=None):
    a2 = a.shape[1:] if a_g else a.shape
    if mode == "nn":
        (m, k), (k2, n) = a2, b.shape
    elif mode == "nt":
        (m, k), (n, k2) = a2, b.shape
    else:
        (k, m), (k2, n) = a2, b.shape
    assert k == k2, (name, a.shape, b.shape)
    g = a.shape[0] if a_g else 1
    tm = m if m <= 768 else _pick(m, (768, 512, 256, 128))
    tn = n if n <= 1280 else _pick(n, (1280, 1024, 768, 512, 256, 128))
    tk = k if k <= 1280 else _pick(k, (1280, 1024, 768, 512, 256, 128))
    fits = lambda kk, nn: 2 * (kk * tm * a.dtype.itemsize + kk * nn * b.dtype.itemsize + tm * nn * 4) <= VMEM_BIG - (8 << 20)
    if fits(k, n):
        tn, tk = n, k
    elif fits(k, tn):
        tk = k
    nk = k // tk
    dn = {"nn": (((1,), (0,)), ((), ())), "nt": (((1,), (1,)), ((), ())), "tn": (((0,), (0,)), ((), ()))}[mode]

    def body(*refs):
        a_ref, b_ref = refs[0], refs[1]
        res_ref = refs[2] if res is not None else None
        o_ref = refs[3] if res is not None else refs[2]
        p = lax.dot_general(a_ref[...].astype(BF16), b_ref[...].astype(BF16), dn, preferred_element_type=F32)

        def fin(v):
            if res_ref is not None:
                v = v + res_ref[...]
            o_ref[...] = v.astype(out_dtype)

        if nk == 1:
            fin(p)
        else:
            acc_ref = refs[-1]
            kk = pl.program_id(3)

            @pl.when(kk == 0)
            def _():
                acc_ref[...] = p

            @pl.when(kk > 0)
            def _():
                acc_ref[...] += p

            @pl.when(kk == nk - 1)
            def _():
                fin(acc_ref[...])

    a_block, a_idx = ((tk, tm), lambda gg, i, j, kk: (kk, i)) if mode == "tn" else ((tm, tk), lambda gg, i, j, kk: (i, kk))
    if a_g:
        a_spec = pl.BlockSpec((None,) + a_block, lambda gg, i, j, kk: (gg,) + a_idx(gg, i, j, kk))
    else:
        a_spec = pl.BlockSpec(a_block, a_idx)
    if mode == "nt":
        b_spec = pl.BlockSpec((tn, tk), lambda gg, i, j, kk: (j, kk))
    else:
        b_spec = pl.BlockSpec((tk, tn), lambda gg, i, j, kk: (kk, j))
    in_specs, args = [a_spec, b_spec], [a, b]
    if res is not None:
        in_specs.append(pl.BlockSpec((tm, tn), lambda gg, i, j, kk: (i, j)))
        args.append(res)
    if a_g:
        out_spec, oshape = pl.BlockSpec((None, tm, tn), lambda gg, i, j, kk: (gg, i, j)), (g, m, n)
    else:
        out_spec, oshape = pl.BlockSpec((tm, tn), lambda gg, i, j, kk: (i, j)), (m, n)
    grid = (g, m // tm, n // tn, nk)

    def at(corner):
        hit = pl.program_id(0) == corner[0]
        for ax in range(1, 4):
            hit = jnp.logical_and(hit, pl.program_id(ax) == corner[ax])
        return hit

    outs, couts = _call(body, comm, lambda: at((0, 0, 0, 0)), lambda: at(tuple(dim - 1 for dim in grid)), name=name,
                        grid=grid, in_specs=in_specs, args=args, out_specs=[out_spec], out_shape=[SDS(oshape, out_dtype)],
                        scratch=[pltpu.VMEM((tm, tn), F32)] if nk > 1 else [], sem=("arbitrary",) * 4)
    return outs[0] if comm is None else (outs[0], couts)


def _ffn_fwd(x, gains, l, wg, wu, wd, name, comm=None, mid_frac=0.75):
    t, d = x.shape
    f = wg.shape[-2]
    tt = _rt(t)

    def body(x_ref, g_ref, wg_ref, wu_ref, wd_ref, o_ref, gt_ref, up_ref, h_sc, acc_sc):
        j = pl.program_id(1)

        @pl.when(j == 0)
        def _():
            xv = x_ref[...]
            h_sc[...] = (xv * _rms_r(xv) * g_ref[l:l + 1, :]).astype(BF16)
            acc_sc[...] = jnp.zeros_like(acc_sc)

        for part in range(FFN_PARTS):
            rows = pl.ds(part * (tt // FFN_PARTS), tt // FFN_PARTS)
            h = h_sc[rows, :]
            gt = _dot_nt(h, wg_ref[...]).astype(BF16)
            up = _dot_nt(h, wu_ref[...]).astype(BF16)
            gt_ref[rows, :] = gt
            up_ref[rows, :] = up
            acc_sc[rows, :] += _dot(gt * _sig(gt) * up, wd_ref[...])

        @pl.when(j == NCHIP - 1)
        def _():
            o_ref[...] = x_ref[...] + 0.5 * acc_sc[...]

    wspec = pl.BlockSpec((None, f, d), lambda i, j: (j, 0, 0))
    row = pl.BlockSpec((tt, d), lambda i, j: (i, 0))
    sh = pl.BlockSpec((None, tt, f), lambda i, j: (j, i, 0))
    ni = t // tt
    first = lambda: jnp.logical_and(pl.program_id(0) == 0, pl.program_id(1) == 0)
    last = lambda: jnp.logical_and(pl.program_id(0) == ni - 1, pl.program_id(1) == NCHIP - 1)
    mid_at = lambda: pl.program_id(0) * NCHIP + pl.program_id(1) == int(mid_frac * ni * NCHIP)
    outs, couts = _call(
        body, comm, first, last, name=name, grid=(ni, NCHIP), mid_at=mid_at,
        in_specs=[row, pl.BlockSpec(gains.shape, lambda i, j: (0, 0)), wspec, wspec, wspec], args=[x, gains, wg, wu, wd],
        out_specs=[row, sh, sh], out_shape=[SDS((t, d), F32), SDS((NCHIP, t, f), BF16), SDS((NCHIP, t, f), BF16)],
        scratch=[pltpu.VMEM((tt, d), BF16), pltpu.VMEM((tt, d), F32)], sem=("arbitrary", "arbitrary"))
    return (*outs, couts)


def _ffn_bwd(x, gains, l, wg, wu, wd, gts, ups, dout, name, comm=None):
    t, d = x.shape
    f = wg.shape[-2]
    tt = _rt(t)

    def body(x_ref, g_ref, wg_ref, wu_ref, wd_ref, gt_ref, up_ref, do_ref,
             dx_ref, dg_ref, dgt_ref, dup_ref, act_ref, h_ref, dob_ref, dh_sc):
        i, j = pl.program_id(0), pl.program_id(1)

        @pl.when(j == 0)
        def _():
            xv = x_ref[...]
            h_ref[...] = (xv * _rms_r(xv) * g_ref[l:l + 1, :]).astype(BF16)
            dob_ref[...] = (0.5 * do_ref[...]).astype(BF16)
            dh_sc[...] = jnp.zeros_like(dh_sc)

        @pl.when(jnp.logical_and(i == 0, j == 0))
        def _():
            dg_ref[...] = jnp.zeros_like(dg_ref)

        for part in range(FFN_PARTS):
            rows = pl.ds(part * (tt // FFN_PARTS), tt // FFN_PARTS)
            dact = _dot_nt(dob_ref[rows, :], wd_ref[...]).astype(BF16)
            gt = gt_ref[rows, :]
            up = up_ref[rows, :]
            s = _sig(gt)
            sl = gt * s
            dup = dact * sl
            dgt = dact * up * (s + sl * (1.0 - s))
            dgt_ref[rows, :] = dgt
            dup_ref[rows, :] = dup
            act_ref[rows, :] = sl * up
            dh_sc[rows, :] += _dot(dgt, wg_ref[...]) + _dot(dup, wu_ref[...])

        @pl.when(j == NCHIP - 1)
        def _():
            dxn, dg = _rms_bwd(x_ref[...], g_ref[l:l + 1, :], dh_sc[...])
            dx_ref[...] = do_ref[...] + dxn
            dg_ref[...] += dg

    wspec = pl.BlockSpec((None, f, d), lambda i, j: (j, 0, 0))
    row = pl.BlockSpec((tt, d), lambda i, j: (i, 0))
    sh = pl.BlockSpec((None, tt, f), lambda i, j: (j, i, 0))
    ni = t // tt
    first = lambda: jnp.logical_and(pl.program_id(0) == 0, pl.program_id(1) == 0)
    last = lambda: jnp.logical_and(pl.program_id(0) == ni - 1, pl.program_id(1) == NCHIP - 1)
    outs, couts = _call(
        body, comm, first, last, name=name, grid=(ni, NCHIP),
        in_specs=[row, pl.BlockSpec(gains.shape, lambda i, j: (0, 0)), wspec, wspec, wspec, sh, sh, row],
        args=[x, gains, wg, wu, wd, gts, ups, dout],
        out_specs=[row, pl.BlockSpec((1, d), lambda i, j: (0, 0)), sh, sh, sh, row, row],
        out_shape=[SDS((t, d), F32), SDS((1, d), F32)] + [SDS((NCHIP, t, f), BF16)] * 3 + [SDS((t, d), BF16)] * 2,
        scratch=[pltpu.VMEM((tt, d), F32)], sem=("arbitrary", "arbitrary"))
    return (*outs, couts)


def _rms_fwd(x, gains, l, name):
    t, d = x.shape
    tt = _rt(t)

    def body(x_ref, g_ref, o_ref):
        xv = x_ref[...]
        o_ref[...] = (xv * _rms_r(xv) * g_ref[l:l + 1, :]).astype(BF16)

    row = pl.BlockSpec((tt, d), lambda i: (i, 0))
    return pl.pallas_call(body, name=name, grid=(t // tt,), in_specs=[row, pl.BlockSpec(gains.shape, lambda i: (0, 0))],
                          out_specs=row, out_shape=SDS((t, d), BF16), compiler_params=_cparams(("arbitrary",)))(x, gains)


def _rms_back(x, gains, l, dh, dres, name, comm=None):
    t, d = x.shape
    tt = _rt(t)

    def body(x_ref, g_ref, dh_ref, dr_ref, dx_ref, dg_ref):
        @pl.when(pl.program_id(0) == 0)
        def _():
            dg_ref[...] = jnp.zeros_like(dg_ref)

        dxn, dg = _rms_bwd(x_ref[...], g_ref[l:l + 1, :], dh_ref[...])
        dx_ref[...] = dr_ref[...] + dxn
        dg_ref[...] += dg

    row = pl.BlockSpec((tt, d), lambda i: (i, 0))
    ni = t // tt
    outs, couts = _call(body, comm, lambda: pl.program_id(0) == 0, lambda: pl.program_id(0) == ni - 1, name=name,
                        grid=(ni,), in_specs=[row, pl.BlockSpec(gains.shape, lambda i: (0, 0)), row, row],
                        args=[x, gains, dh, dres], out_specs=[row, pl.BlockSpec((1, d), lambda i: (0, 0))],
                        out_shape=[SDS((t, d), F32), SDS((1, d), F32)], scratch=[], sem=("arbitrary",))
    return (*outs, couts)


def _dwconv_fwd(proj, convw, convb, l, o_a, cc, name):
    t = proj.shape[0]
    r = min(256, t)
    nb = cc // HP

    def body(a_ref, gate_ref, w_ref, b_ref, uc_ref, u_sc):
        u_sc[0:CPAD, :] = jnp.zeros((CPAD, HP), F32)

        def fill(ci, carry):
            r0 = pl.multiple_of(ci * r, r)
            u_sc[pl.ds(CPAD + r0, r), :] = a_ref[pl.ds(r0, r), :].astype(F32) * _sig(gate_ref[pl.ds(r0, r), :].astype(F32))
            return carry

        lax.fori_loop(0, t // r, fill, 0)
        w = w_ref[l]
        bias = b_ref[l:l + 1, :]

        def conv(ci, carry):
            r0 = pl.multiple_of(ci * r, r)
            win = u_sc[pl.ds(r0, r + CPAD), :]
            acc = jnp.zeros((r, HP), F32) + bias
            for k in range(KW):
                off = CPAD - (KW - 1) + k
                acc = acc + win[off:off + r, :] * w[k:k + 1, :]
            uc_ref[pl.ds(r0, r), :] = acc
            return carry

        lax.fori_loop(0, t // r, conv, 0)

    col = lambda base: pl.BlockSpec((t, HP), lambda c: (0, base // HP + c))
    return pl.pallas_call(
        body, name=name, grid=(nb,),
        in_specs=[col(o_a), col(o_a + cc), pl.BlockSpec((convw.shape[0], CPAD, HP), lambda c: (0, 0, c)),
                  pl.BlockSpec((convb.shape[0], HP), lambda c: (0, c))],
        out_specs=pl.BlockSpec((t, HP), lambda c: (0, c)), out_shape=SDS((t, cc), F32),
        scratch_shapes=[pltpu.VMEM((t + CPAD, HP), F32)],
        compiler_params=_cparams(("arbitrary",), VMEM_BIG))(proj, proj, convw, convb)


def _dwconv_bwd(duc, proj, convw, l, o_a, cc, name):
    t = proj.shape[0]
    r = min(256, t)
    nb = cc // HP

    def body(d_ref, a_ref, gate_ref, w_ref, du_ref, dw_ref, u_sc, d_sc, dw_sc):
        u_sc[0:CPAD, :] = jnp.zeros((CPAD, HP), F32)
        d_sc[t:t + CPAD, :] = jnp.zeros((CPAD, HP), F32)
        dw_sc[...] = jnp.zeros_like(dw_sc)

        def fill(ci, carry):
            r0 = pl.multiple_of(ci * r, r)
            u_sc[pl.ds(CPAD + r0, r), :] = a_ref[pl.ds(r0, r), :].astype(F32) * _sig(gate_ref[pl.ds(r0, r), :].astype(F32))
            d_sc[pl.ds(r0, r), :] = d_ref[pl.ds(r0, r), :]
            return carry

        lax.fori_loop(0, t // r, fill, 0)
        w = w_ref[l]

        def conv(ci, carry):
            r0 = pl.multiple_of(ci * r, r)
            dwin = d_sc[pl.ds(r0, r + CPAD), :]
            uwin = u_sc[pl.ds(r0, r + CPAD), :]
            dcur = dwin[0:r, :]
            acc = jnp.zeros((r, HP), F32)
            for k in range(KW):
                acc = acc + dwin[KW - 1 - k:KW - 1 - k + r, :] * w[k:k + 1, :]
                off = CPAD - (KW - 1) + k
                part = (dcur * uwin[off:off + r, :]).reshape(r // 8, 8, HP).sum(axis=0)
                dw_sc[8 * k:8 * k + 8, :] += part
            du_ref[pl.ds(r0, r), :] = acc
            return carry

        lax.fori_loop(0, t // r, conv, 0)
        dw_ref[...] = jnp.zeros_like(dw_ref)
        for k in range(KW):
            dw_ref[k:k + 1, :] = jnp.sum(dw_sc[8 * k:8 * k + 8, :], axis=0, keepdims=True)

    col = lambda base: pl.BlockSpec((t, HP), lambda c: (0, base // HP + c))
    return pl.pallas_call(
        body, name=name, grid=(nb,),
        in_specs=[pl.BlockSpec((t, HP), lambda c: (0, c)), col(o_a), col(o_a + cc),
                  pl.BlockSpec((convw.shape[0], CPAD, HP), lambda c: (0, 0, c))],
        out_specs=[pl.BlockSpec((t, HP), lambda c: (0, c)), pl.BlockSpec((CPAD, HP), lambda c: (0, c))],
        out_shape=[SDS((t, cc), F32), SDS((CPAD, cc), F32)],
        scratch_shapes=[pltpu.VMEM((t + CPAD, HP), F32), pltpu.VMEM((t + CPAD, HP), F32), pltpu.VMEM((8 * CPAD, HP), F32)],
        compiler_params=_cparams(("arbitrary",), VMEM_BIG))(duc, proj, proj, convw)


def _ln_parts(uc, g, b):
    mu = jnp.mean(uc, axis=-1, keepdims=True)
    xc = uc - mu
    r = lax.rsqrt(jnp.mean(xc * xc, axis=-1, keepdims=True) + EPS)
    xh = xc * r
    return r, xh, xh * g + b


def _ln_silu(uc, ln_g, ln_b, l, name):
    t, cc = uc.shape
    tt = _rt(t)

    def body(u_ref, g_ref, b_ref, s_ref):
        _, _, yv = _ln_parts(u_ref[...], g_ref[l:l + 1, :], b_ref[l:l + 1, :])
        s_ref[...] = (yv * _sig(yv)).astype(BF16)

    row = pl.BlockSpec((tt, cc), lambda i: (i, 0))
    full = pl.BlockSpec(ln_g.shape, lambda i: (0, 0))
    return pl.pallas_call(body, name=name, grid=(t // tt,), in_specs=[row, full, full], out_specs=row,
                          out_shape=SDS((t, cc), BF16), compiler_params=_cparams(("arbitrary",)))(uc, ln_g, ln_b)


def _ln_silu_bwd(uc, ln_g, ln_b, l, ds, name):
    t, cc = uc.shape
    tt = _rt(t)

    def body(u_ref, g_ref, b_ref, ds_ref, du_ref, dg_ref, db_ref, dcb_ref):
        @pl.when(pl.program_id(0) == 0)
        def _():
            dg_ref[...] = jnp.zeros_like(dg_ref)
            db_ref[...] = jnp.zeros_like(db_ref)
            dcb_ref[...] = jnp.zeros_like(dcb_ref)

        g = g_ref[l:l + 1, :]
        r, xh, yv = _ln_parts(u_ref[...], g, b_ref[l:l + 1, :])
        sy = _sig(yv)
        dy = ds_ref[...] * (sy * (1.0 + yv * (1.0 - sy)))
        dg_ref[...] += jnp.sum(dy * xh, axis=0, keepdims=True)
        db_ref[...] += jnp.sum(dy, axis=0, keepdims=True)
        dxh = dy * g
        du = r * (dxh - jnp.mean(dxh, axis=-1, keepdims=True) - xh * jnp.mean(dxh * xh, axis=-1, keepdims=True))
        du_ref[...] = du
        dcb_ref[...] += jnp.sum(du, axis=0, keepdims=True)

    row = pl.BlockSpec((tt, cc), lambda i: (i, 0))
    full = pl.BlockSpec(ln_g.shape, lambda i: (0, 0))
    vec = pl.BlockSpec((1, cc), lambda i: (0, 0))
    return pl.pallas_call(body, name=name, grid=(t // tt,), in_specs=[row, full, full, row],
                          out_specs=[row, vec, vec, vec], out_shape=[SDS((t, cc), F32)] + [SDS((1, cc), F32)] * 3,
                          compiler_params=_cparams(("arbitrary",)))(uc, ln_g, ln_b, ds)


def _glu_bwd(du, proj, dproj, o_a, cc, name):
    t = du.shape[0]
    tt = _rt(t)

    def body(du_ref, a_ref, gate_ref, prev_ref, o_ref):
        sg = _sig(gate_ref[...].astype(F32))
        dv = du_ref[...]
        o_ref[:, 0:cc] = (dv * sg).astype(BF16)
        o_ref[:, cc:2 * cc] = (dv * a_ref[...] * sg * (1.0 - sg)).astype(BF16)

    return pl.pallas_call(
        body, name=name, grid=(t // tt,),
        in_specs=[pl.BlockSpec((tt, cc), lambda i: (i, 0)), pl.BlockSpec((tt, cc), lambda i: (i, o_a // cc)),
                  pl.BlockSpec((tt, cc), lambda i: (i, o_a // cc + 1)), ANY],
        out_specs=pl.BlockSpec((tt, 2 * cc), lambda i: (i, o_a // (2 * cc))),
        out_shape=SDS(dproj.shape, dproj.dtype), input_output_aliases={3: 0},
        compiler_params=_cparams(("arbitrary",)))(du, proj, proj, dproj)


def _rope(v, cs, s1, s2):
    return v * cs + pltpu.roll(v, HP - ROPE // 2, 1) * s1 + pltpu.roll(v, ROPE // 2, 1) * s2


def _rope_t(dv, cs, s1, s2):
    return dv * cs + pltpu.roll(dv * s1, ROPE // 2, 1) + pltpu.roll(dv * s2, HP - ROPE // 2, 1)


def _head_norm(v, g):
    r = lax.rsqrt(jnp.sum(v * v, axis=-1, keepdims=True) * (1.0 / QK) + EPS)
    return v * r * g, r


def _head_norm_bwd(v, r, g, dy):
    xh = v * r
    dg = jnp.sum(dy * xh, axis=0, keepdims=True)
    dxh = dy * g
    dx = r * (dxh - xh * (jnp.sum(dxh * xh, axis=-1, keepdims=True) * (1.0 / QK)))
    return dx, dg


def _mla_specs(tt, lay, cq_norm, ckv_norm, qn, kn, wuq, wukv):
    ql, kvl = cq_norm.shape[1], ckv_norm.shape[1]
    full = lambda a: pl.BlockSpec(a.shape, lambda i: (0,) * a.ndim)
    tab = pl.BlockSpec((tt, HP), lambda i: (i, 0))
    return [pl.BlockSpec((tt, ql), lambda i: (i, lay["cq"] // ql)),
            pl.BlockSpec((tt, kvl), lambda i: (i, lay["ckv"] // kvl)),
            pl.BlockSpec((tt, HP), lambda i: (i, lay["kr"] // HP)),
            full(cq_norm), full(ckv_norm), full(qn), full(kn), full(wuq), full(wukv), tab, tab, tab]


def _mla_pre_fwd(proj, lay, l, cq_norm, ckv_norm, qn, kn, wuq, wukv, tabs, name):
    t = proj.shape[0]
    tt = _rt(t)
    hw = N_HEADS * HP

    def body(cq_ref, ckv_ref, kr_ref, gq_ref, gkv_ref, qn_ref, kn_ref, wq_ref, wkv_ref, c_ref, s1_ref, s2_ref,
             q_ref, k_ref, v_ref):
        cq = cq_ref[...].astype(F32)
        cqn = (cq * _rms_r(cq) * gq_ref[l:l + 1, :]).astype(BF16)
        ckv = ckv_ref[...].astype(F32)
        ckvn = (ckv * _rms_r(ckv) * gkv_ref[l:l + 1, :]).astype(BF16)
        qraw = _dot_nt(cqn, wq_ref[...])
        kv = _dot(ckvn, wkv_ref[...])
        v_ref[...] = kv.astype(BF16)
        lane = lax.broadcasted_iota(jnp.int32, (tt, HP), 1)
        krs = pltpu.roll(jnp.where(lane < ROPE, kr_ref[...].astype(F32), 0.0), NOPE, 1)
        cs, s1, s2 = c_ref[...], s1_ref[...], s2_ref[...]
        gq, gk = qn_ref[l:l + 1, :], kn_ref[l:l + 1, :]
        for h in range(N_HEADS):
            sl = slice(h * HP, (h + 1) * HP)
            qh, _ = _head_norm(qraw[:, sl], gq)
            q_ref[:, sl] = (_rope(qh, cs, s1, s2) * QK ** -0.5).astype(BF16)
            kh, _ = _head_norm(jnp.where(lane < NOPE, kv[:, sl], krs), gk)
            k_ref[:, sl] = _rope(kh, cs, s1, s2).astype(BF16)

    row = pl.BlockSpec((tt, hw), lambda i: (i, 0))
    return pl.pallas_call(
        body, name=name, grid=(t // tt,),
        in_specs=_mla_specs(tt, lay, cq_norm, ckv_norm, qn, kn, wuq, wukv),
        out_specs=[row, row, row], out_shape=[SDS((t, hw), BF16)] * 3,
        compiler_params=_cparams(("arbitrary",), VMEM_BIG))(
            proj, proj, proj, cq_norm, ckv_norm, qn, kn, wuq, wukv, *tabs)


def _mla_pre_bwd(proj, lay, l, cq_norm, ckv_norm, qn, kn, wuq, wukv, tabs, dq, dk, dv, dproj, name):
    t = proj.shape[0]
    tt = _rt(t)
    hw = N_HEADS * HP
    ql, kvl = cq_norm.shape[1], ckv_norm.shape[1]
    wm = lay["wm"]

    def body(cq_ref, ckv_ref, kr_ref, gq_ref, gkv_ref, qn_ref, kn_ref, wq_ref, wkv_ref, c_ref, s1_ref, s2_ref,
             dq_ref, dk_ref, dv_ref, prev_ref,
             o_ref, dqr_ref, dkv_ref, cqn_ref, ckvn_ref, dgq_ref, dgkv_ref, dqn_ref, dkn_ref):
        @pl.when(pl.program_id(0) == 0)
        def _():
            for ref in (dgq_ref, dgkv_ref, dqn_ref, dkn_ref):
                ref[...] = jnp.zeros_like(ref)

        cq = cq_ref[...].astype(F32)
        cqn = (cq * _rms_r(cq) * gq_ref[l:l + 1, :]).astype(BF16)
        ckv = ckv_ref[...].astype(F32)
        ckvn = (ckv * _rms_r(ckv) * gkv_ref[l:l + 1, :]).astype(BF16)
        cqn_ref[...] = cqn
        ckvn_ref[...] = ckvn
        qraw = _dot_nt(cqn, wq_ref[...])
        kv = _dot(ckvn, wkv_ref[...])
        lane = lax.broadcasted_iota(jnp.int32, (tt, HP), 1)
        krs = pltpu.roll(jnp.where(lane < ROPE, kr_ref[...].astype(F32), 0.0), NOPE, 1)
        cs, s1, s2 = c_ref[...], s1_ref[...], s2_ref[...]
        gq, gk = qn_ref[l:l + 1, :], kn_ref[l:l + 1, :]
        dkr = jnp.zeros((tt, HP), F32)
        dgq = jnp.zeros((1, HP), F32)
        dgk = jnp.zeros((1, HP), F32)
        for h in range(N_HEADS):
            sl = slice(h * HP, (h + 1) * HP)
            qh = qraw[:, sl]
            _, rq = _head_norm(qh, gq)
            dqh, dg = _head_norm_bwd(qh, rq, gq, _rope_t(dq_ref[:, sl] * QK ** -0.5, cs, s1, s2))
            dgq = dgq + dg
            dqr_ref[:, sl] = dqh.astype(BF16)
            kp = jnp.where(lane < NOPE, kv[:, sl], krs)
            _, rk = _head_norm(kp, gk)
            dkp, dg = _head_norm_bwd(kp, rk, gk, _rope_t(dk_ref[:, sl], cs, s1, s2))
            dgk = dgk + dg
            dkv_ref[:, sl] = (jnp.where(lane < NOPE, dkp, 0.0) + dv_ref[:, sl]).astype(BF16)
            dkr = dkr + dkp
        dqn_ref[...] += dgq
        dkn_ref[...] += dgk
        dkr = jnp.where(lane < ROPE, pltpu.roll(dkr, HP - NOPE, 1), 0.0)
        dcq, dg = _rms_bwd(cq, gq_ref[l:l + 1, :], _dot(dqr_ref[...], wq_ref[...]))
        dgq_ref[...] += dg
        dckv, dg = _rms_bwd(ckv, gkv_ref[l:l + 1, :], _dot_nt(dkv_ref[...], wkv_ref[...]))
        dgkv_ref[...] += dg
        o_ref[:, 0:ql] = dcq.astype(BF16)
        o_ref[:, ql:ql + kvl] = dckv.astype(BF16)
        o_ref[:, ql + kvl:ql + kvl + HP] = dkr.astype(BF16)
        o_ref[:, ql + kvl + HP:wm] = jnp.zeros((tt, wm - ql - kvl - HP), BF16)

    row = pl.BlockSpec((tt, hw), lambda i: (i, 0))
    vec = lambda n: pl.BlockSpec((1, n), lambda i: (0, 0))
    return pl.pallas_call(
        body, name=name, grid=(t // tt,),
        in_specs=_mla_specs(tt, lay, cq_norm, ckv_norm, qn, kn, wuq, wukv) + [row, row, row, ANY],
        out_specs=[pl.BlockSpec((tt, wm), lambda i: (i, lay["cq"] // wm)), row, row,
                   pl.BlockSpec((tt, ql), lambda i: (i, 0)), pl.BlockSpec((tt, kvl), lambda i: (i, 0)),
                   vec(ql), vec(kvl), vec(HP), vec(HP)],
        out_shape=[SDS(dproj.shape, dproj.dtype), SDS((t, hw), BF16), SDS((t, hw), BF16), SDS((t, ql), BF16),
                   SDS((t, kvl), BF16), SDS((1, ql), F32), SDS((1, kvl), F32), SDS((1, HP), F32), SDS((1, HP), F32)],
        input_output_aliases={15: 0},
        compiler_params=_cparams(("arbitrary",), VMEM_BIG))(
            proj, proj, proj, cq_norm, ckv_norm, qn, kn, wuq, wukv, *tabs, dq, dk, dv, dproj)


def _comb_fwd(proj, gate_bias, l, yc, ym, name):
    t, d = yc.shape
    tt = _rt(t)

    def body(p_ref, b_ref, yc_ref, ym_ref, y_ref):
        b = b_ref[l]
        g0 = _sig(p_ref[:, 0:d] + b[0:1, :])
        g1 = _sig(p_ref[:, d:2 * d] + b[1:2, :])
        y_ref[...] = (g0 * yc_ref[...] + g1 * ym_ref[...]).astype(BF16)

    row = pl.BlockSpec((tt, d), lambda i: (i, 0))
    return pl.pallas_call(
        body, name=name, grid=(t // tt,),
        in_specs=[pl.BlockSpec((tt, 2 * d), lambda i: (i, 0)), pl.BlockSpec(gate_bias.shape, lambda i: (0, 0, 0)), row, row],
        out_specs=row, out_shape=SDS((t, d), BF16), compiler_params=_cparams(("arbitrary",)))(proj, gate_bias, yc, ym)


def _comb_bwd(proj, gate_bias, l, yc, ym, dy, dp_cols, name):
    t, d = yc.shape
    tt = _rt(t)

    def body(p_ref, b_ref, yc_ref, ym_ref, dy_ref, dyc_ref, dym_ref, dp_ref, db_ref):
        @pl.when(pl.program_id(0) == 0)
        def _():
            db_ref[...] = jnp.zeros_like(db_ref)

        b = b_ref[l]
        dyv = dy_ref[...].astype(F32)
        g0 = _sig(p_ref[:, 0:d] + b[0:1, :])
        g1 = _sig(p_ref[:, d:2 * d] + b[1:2, :])
        dyc_ref[...] = (dyv * g0).astype(BF16)
        dym_ref[...] = (dyv * g1).astype(BF16)
        dg0 = dyv * yc_ref[...] * g0 * (1.0 - g0)
        dg1 = dyv * ym_ref[...] * g1 * (1.0 - g1)
        dp_ref[:, 0:d] = dg0.astype(BF16)
        dp_ref[:, d:2 * d] = dg1.astype(BF16)
        db_ref[0:1, :] += jnp.sum(dg0, axis=0, keepdims=True)
        db_ref[1:2, :] += jnp.sum(dg1, axis=0, keepdims=True)

    row = pl.BlockSpec((tt, d), lambda i: (i, 0))
    wide = pl.BlockSpec((tt, 2 * d), lambda i: (i, 0))
    return pl.pallas_call(
        body, name=name, grid=(t // tt,),
        in_specs=[wide, pl.BlockSpec(gate_bias.shape, lambda i: (0, 0, 0)), row, row, row],
        out_specs=[row, row, wide, pl.BlockSpec((2, d), lambda i: (0, 0))],
        out_shape=[SDS((t, d), BF16), SDS((t, d), BF16), SDS((t, dp_cols), BF16), SDS((2, d), F32)],
        compiler_params=_cparams(("arbitrary",)))(proj, gate_bias, yc, ym, dy)


def _loss_grad(y, target, name):
    t, d = y.shape
    tt = _rt(t)
    nt = t // tt

    def body(y_ref, t_ref, dy_ref, loss_ref, acc_sc):
        i = pl.program_id(0)

        @pl.when(i == 0)
        def _():
            acc_sc[...] = jnp.zeros_like(acc_sc)

        diff = y_ref[...] - t_ref[...]
        dy_ref[...] = diff * (1.0 / d)
        acc_sc[...] += jnp.sum(diff * diff, axis=0, keepdims=True)

        @pl.when(i == nt - 1)
        def _():
            tot = jnp.sum(acc_sc[...], axis=1, keepdims=True) * (0.5 / d)
            loss_ref[...] = jnp.broadcast_to(tot, (1, HP))

    row = pl.BlockSpec((tt, d), lambda i: (i, 0))
    return pl.pallas_call(body, name=name, grid=(nt,), in_specs=[row, row],
                          out_specs=[row, pl.BlockSpec((1, HP), lambda i: (0, 0))],
                          out_shape=[SDS((t, d), F32), SDS((1, HP), F32)],
                          scratch_shapes=[pltpu.VMEM((1, d), F32)],
                          compiler_params=_cparams(("arbitrary",)))(y, target)


def _chunk_mask(tq):
    rows = lax.broadcasted_iota(jnp.int32, (tq, tq), 0) // CHUNK
    cols = lax.broadcasted_iota(jnp.int32, (tq, tq), 1) // CHUNK
    return cols <= rows


NEG = -1e30


def _flash_fwd(q, k, v, name, comm=None):
    t = q.shape[0]
    tq = _rt(t)
    nq = t // tq
    rep = tq // HP

    def body(q_ref, k_ref, v_ref, o_ref, lse_ref):
        qi = pl.program_id(1)

        def one(hh, ki, carry, masked):
            m_prev, l_prev, acc = carry
            lanes = slice(hh * HP, (hh + 1) * HP)
            r0 = pl.multiple_of(ki * tq, tq)
            s = _dot_nt(q_ref[:, lanes], k_ref[pl.ds(r0, tq), lanes])
            if masked:
                s = jnp.where(_chunk_mask(tq), s, NEG)
            m_new = jnp.maximum(m_prev, jnp.max(s, axis=-1, keepdims=True))
            a = jnp.exp(m_prev - m_new)
            p = jnp.exp(s - jnp.tile(m_new, (1, rep)))
            l_new = a * l_prev + jnp.sum(p, axis=-1, keepdims=True)
            acc = a * acc + _dot(p.astype(BF16), v_ref[pl.ds(r0, tq), lanes])
            return m_new, l_new, acc

        def step(ki, carries, masked):
            return tuple(one(hh, ki, carries[hh], masked) for hh in range(FLASH_HEADS))

        init = (jnp.full((tq, HP), NEG, F32), jnp.zeros((tq, HP), F32), jnp.zeros((tq, HP), F32))
        carries = lax.fori_loop(0, qi, lambda ki, cr: step(ki, cr, False), (init,) * FLASH_HEADS)
        for hh, (m_fin, l_fin, acc) in enumerate(step(qi, carries, True)):
            o_ref[:, hh * HP:(hh + 1) * HP] = (acc / l_fin).astype(BF16)
            lse_ref[hh] = m_fin + jnp.log(l_fin)

    wide = FLASH_HEADS * HP
    ng = N_HEADS // FLASH_HEADS
    qspec = pl.BlockSpec((tq, wide), lambda h, qi: (qi, h))
    head = pl.BlockSpec((t, wide), lambda h, qi: (0, h))
    first = lambda: jnp.logical_and(pl.program_id(0) == 0, pl.program_id(1) == 0)
    last = lambda: jnp.logical_and(pl.program_id(0) == ng - 1, pl.program_id(1) == nq - 1)
    mid_at = lambda: pl.program_id(0) * nq + pl.program_id(1) == (7 * ng * nq) // 8
    outs, couts = _call(
        body, comm, first, last, name=name, grid=(ng, nq), mid_at=mid_at, in_specs=[qspec, head, head], args=[q, k, v],
        out_specs=[qspec, pl.BlockSpec((FLASH_HEADS, tq, HP), lambda h, qi: (h, qi, 0))],
        out_shape=[SDS(q.shape, BF16), SDS((N_HEADS, t, HP), F32)], scratch=[], sem=("arbitrary",) * 2)
    return (*outs, couts)


def _flash_bwd(q, k, v, do, o, lse, name, comm=None):
    t = q.shape[0]
    tq = _rt(t)
    nq = t // tq
    rep = tq // HP

    def body(q_ref, k_ref, v_ref, do_ref, o_ref, lse_ref, dq_ref, dk_ref, dv_ref, delta_sc):
        def prep(qi, carry):
            r0 = pl.multiple_of(qi * tq, tq)
            rows = pl.ds(r0, tq)
            dlt = jnp.sum(do_ref[rows, :].astype(F32) * o_ref[rows, :].astype(F32), axis=-1, keepdims=True)
            delta_sc[rows, :] = jnp.broadcast_to(dlt, (tq, HP))
            dq_ref[rows, :] = jnp.zeros((tq, HP), F32)
            return carry

        lax.fori_loop(0, nq, prep, 0)

        def keys(ki, carry0):
            krows = pl.ds(pl.multiple_of(ki * tq, tq), tq)
            kt, vt = k_ref[krows, :], v_ref[krows, :]

            def step(qi, carry, masked):
                dk_acc, dv_acc = carry
                rows = pl.ds(pl.multiple_of(qi * tq, tq), tq)
                qt, dot_ = q_ref[rows, :], do_ref[rows, :]
                s = _dot_nt(qt, kt)
                if masked:
                    s = jnp.where(_chunk_mask(tq), s, NEG)
                p = jnp.exp(s - jnp.tile(lse_ref[rows, :], (1, rep)))
                ds = (p * (_dot_nt(dot_, vt) - jnp.tile(delta_sc[rows, :], (1, rep)))).astype(BF16)
                dv_acc = dv_acc + _dot_tn(p.astype(BF16), dot_)
                dk_acc = dk_acc + _dot_tn(ds, qt)
                dq_ref[rows, :] += _dot(ds, kt)
                return dk_acc, dv_acc

            zero = jnp.zeros((tq, HP), F32)
            carry = step(ki, (zero, zero), True)
            dk_acc, dv_acc = lax.fori_loop(ki + 1, nq, lambda qi, cr: step(qi, cr, False), carry)
            dk_ref[krows, :] = dk_acc
            dv_ref[krows, :] = dv_acc
            return carry0

        lax.fori_loop(0, nq, keys, 0)

    head = pl.BlockSpec((t, HP), lambda h: (0, h))
    outs, couts = _call(
        body, comm, lambda: pl.program_id(0) == 0, lambda: pl.program_id(0) == N_HEADS - 1, name=name, grid=(N_HEADS,),
        in_specs=[head] * 5 + [pl.BlockSpec((None, t, HP), lambda h: (h, 0, 0))], args=[q, k, v, do, o, lse],
        out_specs=[head] * 3, out_shape=[SDS(q.shape, F32)] * 3, scratch=[pltpu.VMEM((t, HP), F32)], sem=("arbitrary",))
    return (*outs, couts)


def _add_cast(gs, rs, place, name):
    n = len(gs)
    _, r, c = gs[0].shape
    tr, tc = _tile2(r, c // 2, 3 * n, 16)
    nct = c // 2 // tc

    def body(place_ref, *refs):
        for a in range(n):
            refs[2 * n + a][...] = (refs[a][...] + refs[n + a][...]).astype(BF16)

    gspec = pl.BlockSpec((None, tr, tc), lambda j, i, k, pr: (j, i, pr[0] * nct + k))
    rspec = pl.BlockSpec((None, tr, tc), lambda j, i, k, pr: (j, i, k))
    grid_spec = pltpu.PrefetchScalarGridSpec(num_scalar_prefetch=1, grid=(NCHIP, r // tr, nct),
                                             in_specs=[gspec] * n + [rspec] * n, out_specs=[rspec] * n)
    return pl.pallas_call(body, name=name, grid_spec=grid_spec, out_shape=[SDS((NCHIP, r, c // 2), BF16)] * n,
                          compiler_params=_cparams(("arbitrary",) * 3, VMEM_BIG))(place, *gs, *rs)


def _sum_chips(ss, qs, place, l, nl, prevs, name):
    n = len(ss)
    _, r, h = ss[0].shape
    tr, tc = _tile2(r, h, 3 * n, 16)
    nct = h // tc

    def body(place_ref, *refs):
        for a in range(n):
            acc = refs[n + a][...].astype(F32)
            for kk in range(NCHIP - 1):
                acc = acc + refs[a][kk].astype(F32)
            refs[-n + a][...] = acc

    in_specs = ([pl.BlockSpec((NCHIP - 1, tr, tc), lambda i, k, pr: (0, i, k))] * n
                + [pl.BlockSpec((None, tr, tc), lambda i, k, pr: (pr[1], i, k))] * n)
    args = [*ss, *qs]
    aliases = {}
    if prevs is not None:
        aliases = {1 + len(args) + a: a for a in range(n)}
        in_specs += [ANY] * n
        args += list(prevs)
    grid_spec = pltpu.PrefetchScalarGridSpec(
        num_scalar_prefetch=1, grid=(r // tr, nct), in_specs=in_specs,
        out_specs=[pl.BlockSpec((None, tr, tc), lambda i, k, pr: (l, i, pr[0] * nct + k))] * n)
    return pl.pallas_call(body, name=name, grid_spec=grid_spec, out_shape=[SDS((nl, r, 2 * h), F32)] * n,
                          input_output_aliases=aliases,
                          compiler_params=_cparams(("arbitrary",) * 2, VMEM_BIG))(place, *args)


def _cast_place(ws, place, name):
    n = len(ws)
    nl, r, c = ws[0].shape
    tr, tc = _tile2(r, c, 2 * n * nl, 16)

    def body(place_ref, *refs):
        for a in range(n * nl):
            refs[n * nl + a][...] = refs[a][...].astype(BF16)

    in_specs = [pl.BlockSpec((None, tr, tc), functools.partial(lambda l, i, k, pr: (l, i, k), l))
                for _ in range(n) for l in range(nl)]
    grid_spec = pltpu.PrefetchScalarGridSpec(
        num_scalar_prefetch=1, grid=(r // tr, c // tc), in_specs=in_specs,
        out_specs=[pl.BlockSpec((None, tr, tc), lambda i, k, pr: (pr[1], i, k))] * (n * nl))
    outs = pl.pallas_call(body, name=name, grid_spec=grid_spec, out_shape=[SDS((NCHIP, r, c), BF16)] * (n * nl),
                          compiler_params=_cparams(("arbitrary",) * 2, VMEM_BIG))(
                              place, *[w for w in ws for _ in range(nl)])
    return [outs[a * nl:(a + 1) * nl] for a in range(n)]


def _adamw(ws, gs, ms, vs, name):
    n = len(ws)
    r, c = ws[0].shape
    tr, tc = _tile2(r, c, 7 * n, 8)
    c1, c2 = 1.0 / (1.0 - B1 ** STEP), 1.0 / (1.0 - B2 ** STEP)

    def body(*refs):
        for a in range(n):
            w, g, m, v = (refs[kk * n + a][...] for kk in range(4))
            m2 = B1 * m + (1.0 - B1) * g
            v2 = B2 * v + (1.0 - B2) * (g * g)
            refs[4 * n + a][...] = -LR * ((m2 * c1) / (jnp.sqrt(v2 * c2) + EPS_ADAM) + WD * w)
            refs[5 * n + a][...] = m2
            refs[6 * n + a][...] = v2

    blk = pl.BlockSpec((tr, tc), lambda i, k: (i, k))
    outs = pl.pallas_call(body, name=name, grid=(r // tr, c // tc), in_specs=[blk] * (4 * n),
                          out_specs=[blk] * (3 * n), out_shape=[SDS((r, c), F32)] * (3 * n),
                          compiler_params=_cparams(("arbitrary",) * 2, VMEM_BIG))(*ws, *gs, *ms, *vs)
    return outs[:n], outs[n:2 * n], outs[2 * n:]


def _place():
    x, y, c = lax.axis_index("x"), lax.axis_index("y"), lax.axis_index("c")
    return x, y, c, [(1 - x, y), (x, 1 - y), (1 - x, 1 - y)]


def _rcopy(src, dst, ssem, rsem, k, dev):
    return pltpu.make_async_remote_copy(src_ref=src, dst_ref=dst, send_sem=ssem.at[k], recv_sem=rsem.at[k],
                                        device_id=dev, device_id_type=MESH)


def _half(ref, lead, cc):
    h = ref.shape[-1] // 2
    return ref.at[(*lead, slice(None), pl.ds(cc * h, h))]


def _per_core(fn):
    c = lax.axis_index("c")
    for cc in (0, 1):
        pl.when(c == cc)(functools.partial(fn, cc))


def _gather_comm(bufs):
    n = len(bufs)

    def plan(couts, ssem, rsem, cc):
        x, y, _, peers = _place()
        me, sib = 2 * x + y, (x, y, 1 - cc)
        send, recv, fwd, recv2 = [], [], [], []
        for a in range(n):
            for kk, (px, py) in enumerate(peers):
                mine = _half(couts[a], (me,), cc)
                got = _half(couts[a], (2 * px + py,), cc)
                other = _half(couts[a], (2 * px + py,), 1 - cc)
                send.append(_rcopy(mine, mine, ssem, rsem, a * 6 + kk, (px, py, cc)))
                recv.append(_rcopy(got, got, ssem, rsem, a * 6 + kk, (px, py, cc)))
                fwd.append(_rcopy(got, got, ssem, rsem, a * 6 + 3 + kk, sib))
                recv2.append(_rcopy(other, other, ssem, rsem, a * 6 + 3 + kk, sib))
        return send, recv, fwd, recv2

    def start(cins, couts, ssem, rsem):
        def go(cc):
            for d in plan(couts, ssem, rsem, cc)[0]:
                d.start()

        _per_core(go)

    def mid(cins, couts, ssem, rsem):
        def go(cc):
            _, recv, fwd, _ = plan(couts, ssem, rsem, cc)
            for dr, df in zip(recv, fwd):
                dr.wait_recv()
                df.start()

        _per_core(go)

    def finish(cins, couts, ssem, rsem):
        def go(cc):
            send, _, fwd, recv2 = plan(couts, ssem, rsem, cc)
            for d in recv2:
                d.wait_recv()
            for d in send + fwd:
                d.wait_send()

        _per_core(go)

    return _Comm(bufs, [SDS(b.shape, b.dtype) for b in bufs], {a: a for a in range(n)}, 6 * n, start, finish, mid)


def _pair_comm(gs):
    n = len(gs)
    halves = [g.shape[-1] // 2 for g in gs]

    def plan(cins, couts, ssem, rsem, cc):
        x, y, _, _ = _place()
        return [_rcopy(cins[a].at[:, :, pl.ds((1 - cc) * halves[a], halves[a])], couts[a], ssem, rsem, a, (x, y, 1 - cc))
                for a in range(n)]

    def start(cins, couts, ssem, rsem):
        def go(cc):
            for d in plan(cins, couts, ssem, rsem, cc):
                d.start()

        _per_core(go)

    def finish(cins, couts, ssem, rsem):
        def go(cc):
            ds = plan(cins, couts, ssem, rsem, cc)
            for d in ds:
                d.wait_recv()
            for d in ds:
                d.wait_send()

        _per_core(go)

    return _Comm(gs, [SDS(g.shape[:-1] + (g.shape[-1] // 2,), g.dtype) for g in gs], {}, n, start, finish)


def _chips_comm(qs):
    n = len(qs)

    def plan(cins, couts, ssem, rsem):
        x, y, c, peers = _place()
        return [_rcopy(cins[a].at[2 * px + py], couts[a].at[kk], ssem, rsem, a * 3 + kk, (px, py, c))
                for a in range(n) for kk, (px, py) in enumerate(peers)]

    def start(cins, couts, ssem, rsem):
        for d in plan(cins, couts, ssem, rsem):
            d.start()

    def finish(cins, couts, ssem, rsem):
        ds = plan(cins, couts, ssem, rsem)
        for d in ds:
            d.wait_recv()
        for d in ds:
            d.wait_send()

    return _Comm(qs, [SDS((NCHIP - 1,) + q.shape[1:], q.dtype) for q in qs], {}, 3 * n, start, finish)


def _share_comm(fs, l):
    n = len(fs)

    def plan(couts, ssem, rsem, cc, which):
        x, y, _, _ = _place()
        out = []
        for a in range(n):
            piece = _half(couts[a], (l,), which)
            out.append(_rcopy(piece, piece, ssem, rsem, a, (x, y, 1 - cc)))
        return out

    def start(cins, couts, ssem, rsem):
        def go(cc):
            for d in plan(couts, ssem, rsem, cc, cc):
                d.start()

        _per_core(go)

    def finish(cins, couts, ssem, rsem):
        def go(cc):
            for d in plan(couts, ssem, rsem, cc, 1 - cc):
                d.wait_recv()
            for d in plan(couts, ssem, rsem, cc, cc):
                d.wait_send()

        _per_core(go)

    return _Comm(fs, [SDS(f.shape, f.dtype) for f in fs], {a: a for a in range(n)}, n, start, finish)


def _spread_comm(buf):
    def plan(cins, couts, ssem, rsem):
        x, y, c, _ = _place()
        me = 4 * x + 2 * y + c
        send, recv = [], []
        for rel in range(1, N_DEV):
            px, py, pc = (1 - x if rel & 4 else x, 1 - y if rel & 2 else y, 1 - c if rel & 1 else c)
            slot = couts[0].at[4 * px + 2 * py + pc]
            send.append(_rcopy(cins[0], couts[0].at[me], ssem, rsem, rel - 1, (px, py, pc)))
            recv.append(_rcopy(slot, slot, ssem, rsem, rel - 1, (px, py, pc)))
        own = pltpu.make_async_copy(cins[0], couts[0].at[me], ssem.at[N_DEV - 1])
        return send, recv, own

    def start(cins, couts, ssem, rsem):
        send, _, own = plan(cins, couts, ssem, rsem)
        for dsc in send + [own]:
            dsc.start()

    def finish(cins, couts, ssem, rsem):
        send, recv, own = plan(cins, couts, ssem, rsem)
        for dsc in recv:
            dsc.wait_recv()
        for dsc in send:
            dsc.wait_send()
        own.wait()

    return _Comm([buf], [SDS((N_DEV,) + buf.shape, buf.dtype)], {}, N_DEV, start, finish)


def _sum_devices(blocks, name):
    _, r, w = blocks.shape

    def body(in_ref, out_ref):
        acc = in_ref[0]
        for dv in range(1, N_DEV):
            acc = acc + in_ref[dv]
        out_ref[...] = acc

    vm = pl.BlockSpec(memory_space=pltpu.VMEM)
    return pl.pallas_call(body, name=name, in_specs=[vm], out_specs=vm, out_shape=SDS((r, w), F32),
                          compiler_params=_cparams(None, VMEM_BIG))(blocks)


def _cols_full(g):
    _, k, ns = g.shape
    return g.transpose(1, 0, 2).reshape(k, NCHIP * ns)


def _cols_shards(w):
    k, n = w.shape
    return w.reshape(k, NCHIP, n // NCHIP).transpose(1, 0, 2)


def _rows_pad(rows, width):
    out = jnp.concatenate([jnp.pad(a, ((0, 0), (0, width - a.shape[1]))) for a in rows], axis=0)
    return jnp.pad(out, ((0, -out.shape[0] % 8), (0, 0)))


def kernel(x, positions, ffn1_norm, ffn1_w_gate, ffn1_w_up, ffn1_w_down, mix_norm, w_in, gate_bias, conv_w, conv_b, conv_ln_g, conv_ln_b, w_conv_out, cq_norm, ckv_norm, w_uq, w_ukv, q_norm, k_norm, w_mla_out, w_out, ffn2_norm, ffn2_w_gate, ffn2_w_up, ffn2_w_down, loss_target, m_ffn1_norm, m_ffn1_w_gate, m_ffn1_w_up, m_ffn1_w_down, m_mix_norm, m_w_in, m_gate_bias, m_conv_w, m_conv_b, m_conv_ln_g, m_conv_ln_b, m_w_conv_out, m_cq_norm, m_ckv_norm, m_w_uq, m_w_ukv, m_q_norm, m_k_norm, m_w_mla_out, m_w_out, m_ffn2_norm, m_ffn2_w_gate, m_ffn2_w_up, m_ffn2_w_down, v_ffn1_norm, v_ffn1_w_gate, v_ffn1_w_up, v_ffn1_w_down, v_mix_norm, v_w_in, v_gate_bias, v_conv_w, v_conv_b, v_conv_ln_g, v_conv_ln_b, v_w_conv_out, v_cq_norm, v_ckv_norm, v_w_uq, v_w_ukv, v_q_norm, v_k_norm, v_w_mla_out, v_w_out, v_ffn2_norm, v_ffn2_w_gate, v_ffn2_w_up, v_ffn2_w_down):
    names = ["ffn1_norm", "ffn1_w_gate", "ffn1_w_up", "ffn1_w_down", "mix_norm", "w_in", "gate_bias", "conv_w",
             "conv_b", "conv_ln_g", "conv_ln_b", "w_conv_out", "cq_norm", "ckv_norm", "w_uq", "w_ukv", "q_norm",
             "k_norm", "w_mla_out", "w_out", "ffn2_norm", "ffn2_w_gate", "ffn2_w_up", "ffn2_w_down"]
    env = dict(locals())
    turned = ("ffn1_w_gate", "ffn1_w_up", "ffn2_w_gate", "ffn2_w_up", "w_in", "w_uq")
    view = lambda nm, a: jnp.swapaxes(a, 1, 2) if nm in turned else a
    wts = {nm: view(nm, env[nm]) for nm in names}
    mom = {nm: view(nm, env["m_" + nm]) for nm in names}
    var = {nm: view(nm, env["v_" + nm]) for nm in names}

    t, d = x.shape[1], x.shape[2]
    nl = ffn1_norm.shape[0]
    cc = conv_b.shape[1]
    ql, kvl = cq_norm.shape[1], ckv_norm.shape[1]
    vw = N_HEADS * V_DIM
    hw = N_HEADS * HP
    lay = {"a": 2 * d, "cq": 2 * d + 2 * cc, "ckv": 2 * d + 2 * cc + ql, "kr": 2 * d + 2 * cc + ql + kvl,
           "wm": ql + kvl + 2 * HP}
    dp = lay["cq"] + lay["wm"]
    nat_g = 2 * cc + ql + kvl + ROPE
    assert lay["cq"] % lay["wm"] == 0 and lay["cq"] % ql == 0 and lay["ckv"] % kvl == 0 and lay["a"] % (2 * cc) == 0
    assert cc % HP == 0 and d % HP == 0 and t % (2 * CHUNK) == 0 and w_in.shape[2] * NCHIP == nat_g + 2 * d
    assert nl == 2

    x0, target = x[0], loss_target[0]
    chip = 2 * lax.axis_index("x") + lax.axis_index("y")
    place = jnp.stack([lax.axis_index("c"), chip]).astype(jnp.int32)

    inv_freq = ROPE_THETA ** (-jnp.arange(0, ROPE, 2, dtype=F32) / ROPE)
    ang = positions[0].astype(F32)[:, None] * inv_freq
    cos, sin, z = jnp.cos(ang), jnp.sin(ang), jnp.zeros((t, ROPE // 2), F32)
    tabs = (jnp.concatenate([jnp.ones((t, NOPE), F32), cos, cos, jnp.zeros((t, HP - QK), F32)], axis=1),
            jnp.concatenate([jnp.zeros((t, NOPE), F32), -sin, z, jnp.zeros((t, HP - QK), F32)], axis=1),
            jnp.concatenate([jnp.zeros((t, NOPE), F32), z, sin, jnp.zeros((t, HP - QK), F32)], axis=1))

    big = ["ffn1_w_gate", "ffn1_w_up", "ffn1_w_down", "ffn2_w_gate", "ffn2_w_up", "ffn2_w_down",
           "w_in", "w_conv_out", "w_uq", "w_ukv", "w_mla_out", "w_out"]
    ffn1_w, ffn2_w, mix_w = big[0:3], big[3:6], big[6:]
    bufs, like = {}, {}
    for nm in big:
        like.setdefault(wts[nm].shape, []).append(nm)
    for gi, grp in enumerate(like.values()):
        for nm, per_layer in zip(grp, _cast_place([wts[nm] for nm in grp], place, f"cast_place_{gi}")):
            for l in range(nl):
                bufs[nm, l] = per_layer[l]

    def gather(keys):
        return _gather_comm([bufs[key] for key in keys])

    def landed(keys, outs):
        for key, o in zip(keys, outs):
            bufs[key] = o

    chunk = lambda nms, l: [(nm, l) for nm in nms]
    first_keys = chunk(ffn1_w, 0)
    landed(first_keys, _comm_call(gather(first_keys), "gather_first"))
    ride = {("ffn1", 0): chunk(mix_w, 0), ("flash", 0): chunk(ffn2_w, 0) + chunk(ffn1_w, 1),
            ("ffn2", 0): chunk(mix_w, 1), ("ffn1", 1): chunk(ffn2_w, 1)}

    def riding(kind, l):
        keys = ride.get((kind, l))
        return keys or [], (None if keys is None else gather(keys))

    def mixer_weights(l):
        w_in_nat = bufs["w_in", l].reshape(nat_g + 2 * d, d)
        w_in_k = jnp.concatenate([w_in_nat[nat_g:], w_in_nat[:nat_g], jnp.zeros((dp - nat_g - 2 * d, d), BF16)], axis=0)
        w_uq_k = jnp.pad(bufs["w_uq", l].reshape(N_HEADS, QK, ql), ((0, 0), (0, HP - QK), (0, 0))).reshape(hw, ql)
        w_mo_k = jnp.pad(_cols_full(bufs["w_mla_out", l]).reshape(N_HEADS, V_DIM, d),
                         ((0, 0), (HP - V_DIM, 0), (0, 0))).reshape(hw, d)
        return dict(w_in=w_in_k, w_co=_cols_full(bufs["w_conv_out", l]), w_uq=w_uq_k,
                    w_ukv=_cols_full(bufs["w_ukv", l]), w_mo=w_mo_k, w_out=bufs["w_out", l].reshape(d, d))

    qn_k = jnp.pad(q_norm, ((0, 0), (0, HP - QK)))
    kn_k = jnp.pad(k_norm, ((0, 0), (0, HP - QK)))
    small = _rows_pad([gate_bias.reshape(nl * 2, d // NCHIP), conv_w.reshape(nl * KW, cc // NCHIP)], d)

    saved, mixw = [], []
    xc = x0
    for l in range(nl):
        keys, comm = riding("ffn1", l)
        if l == 0:
            comm = _join(comm, _spread_comm(small))
        x1, gt1, up1, got = _ffn_fwd(xc, ffn1_norm, l, *[bufs[nm, l] for nm in ffn1_w], f"ffn1_fwd_{l}", comm,
                                     0.75 if l == 0 else 0.9)
        landed(keys, got)
        if l == 0:
            everyone = got[-1][0::2]
            gb_k = everyone[:, :nl * 2, :d // NCHIP].reshape(NCHIP, nl, 2, d // NCHIP).transpose(1, 2, 0, 3).reshape(nl, 2, d)
            cw = everyone[:, nl * 2:nl * 2 + nl * KW, :cc // NCHIP].reshape(NCHIP, nl, KW, cc // NCHIP)
            cw_k = jnp.pad(cw.transpose(1, 2, 0, 3).reshape(nl, KW, cc), ((0, 0), (0, CPAD - KW), (0, 0)))
        mw = mixer_weights(l)
        mixw.append(mw)
        hm = _rms_fwd(x1, mix_norm, l, f"mix_norm_{l}")
        proj = _mm(hm, mw["w_in"], "nt", name=f"proj_{l}", out_dtype=BF16)
        uc = _dwconv_fwd(proj, cw_k, conv_b, l, lay["a"], cc, f"dwconv_{l}")
        sc = _ln_silu(uc, conv_ln_g, conv_ln_b, l, f"conv_ln_{l}")
        yc = _mm(sc, mw["w_co"], "nn", name=f"conv_out_{l}", out_dtype=BF16)
        q, k, kv = _mla_pre_fwd(proj, lay, l, cq_norm, ckv_norm, qn_k, kn_k, mw["w_uq"], mw["w_ukv"], tabs, f"mla_pre_{l}")
        keys, comm = riding("flash", l)
        o, lse, got = _flash_fwd(q, k, kv, f"flash_{l}", comm)
        landed(keys, got)
        ym = _mm(o, mw["w_mo"], "nn", name=f"mla_out_{l}", out_dtype=BF16)
        yv = _comb_fwd(proj, gb_k, l, yc, ym, f"combine_{l}")
        x2 = _mm(yv, mw["w_out"], "nn", name=f"mix_out_{l}", res=x1)
        keys, comm = riding("ffn2", l)
        x3, gt2, up2, got = _ffn_fwd(x2, ffn2_norm, l, *[bufs[nm, l] for nm in ffn2_w], f"ffn2_fwd_{l}", comm)
        landed(keys, got)
        saved.append(dict(x0=xc, x1=x1, gt1=gt1, up1=up1, hm=hm, proj=proj, uc=uc, sc=sc, yc=yc, q=q, k=k, kv=kv,
                          o=o, lse=lse, ym=ym, yv=yv, x2=x2, gt2=gt2, up2=up2))
        xc = x3
    dx, loss_row = _loss_grad(xc, target, "loss")

    gw = {nm: [None] * nl for nm in names}
    red = {}

    def shard_groups(nms, l):
        same = {}
        for nm in nms:
            same.setdefault(gw[nm][l].shape, []).append(nm)
        return list(same.values())

    def add_parts(nms, l, sib_part, tag):
        qb = {}
        for gi, grp in enumerate(shard_groups(nms, l)):
            outs = _add_cast([gw[nm][l] for nm in grp], [sib_part[nm] for nm in grp], place, f"rs_add_{tag}_{gi}")
            qb.update(zip(grp, outs))
        return qb

    def sum_parts(nms, l, qb, parts, tag):
        for gi, grp in enumerate(shard_groups(nms, l)):
            prevs = [red[nm] for nm in grp] if grp[0] in red else None
            outs = _sum_chips([parts[nm] for nm in grp], [qb[nm] for nm in grp], place, l, nl, prevs, f"rs_sum_{tag}_{gi}")
            red.update(zip(grp, outs))

    pair = lambda nms, l: _pair_comm([gw[nm][l] for nm in nms])
    chips = lambda nms, qb: _chips_comm([qb[nm] for nm in nms])
    share = lambda nms, l: _share_comm([red[nm] for nm in nms], l)

    def split(joined, got):
        return [got[o0:o1] for o0, o1 in joined.spans]

    mix_rest = [nm for nm in mix_w if nm != "w_in"]
    mix_parts = [["w_out", "w_conv_out"], ["w_mla_out", "w_ukv", "w_uq"], []]

    for l in reversed(range(nl)):
        s, mw = saved[l], mixw[l]
        riding_rs = l == 0

        def ffn_back(tag, xin, gains, gts, ups, dout, comm=None):
            wg, wu, wd = (bufs[f"{tag}_w_{p}", l] for p in ("gate", "up", "down"))
            dxi, dgain, dgt, dup, act, hb, dob, got = _ffn_bwd(xin, gains, l, wg, wu, wd, gts, ups, dout,
                                                              f"{tag}_bwd_{l}", comm)
            for p, lhs, rhs in zip(("gate", "up", "down"), (dgt, dup, act), (hb, hb, dob)):
                gw[f"{tag}_w_{p}"][l] = _mm(lhs, rhs, "tn", name=f"{tag}_dw_{p}_{l}", a_g=True)
            gw[f"{tag}_norm"][l] = dgain
            return dxi, got

        dx2, got = ffn_back("ffn2", s["x2"], ffn2_norm, s["gt2"], s["up2"], dx, pair(big, 1) if riding_rs else None)
        if riding_rs:
            qb1 = add_parts(big, 1, dict(zip(big, got)), "l1")
        dyv = _mm(dx2, mw["w_out"], "nt", name=f"mix_out_dy_{l}", out_dtype=BF16)
        gw["w_out"][l] = _mm(s["yv"], dx2, "tn", name=f"mix_out_dw_{l}").reshape(NCHIP, d // NCHIP, d)
        dyc, dym, dproj, dgb = _comb_bwd(s["proj"], gb_k, l, s["yc"], s["ym"], dyv, dp, f"combine_bwd_{l}")
        gw["gate_bias"][l] = dgb
        dsc = _mm(dyc, mw["w_co"], "nt", name=f"conv_out_ds_{l}")
        gw["w_conv_out"][l] = _cols_shards(_mm(s["sc"], dyc, "tn", name=f"conv_out_dw_{l}"))
        duc, gw["conv_ln_g"][l], gw["conv_ln_b"][l], gw["conv_b"][l] = _ln_silu_bwd(
            s["uc"], conv_ln_g, conv_ln_b, l, dsc, f"conv_ln_bwd_{l}")
        du, dcw = _dwconv_bwd(duc, s["proj"], cw_k, l, lay["a"], cc, f"dwconv_bwd_{l}")
        gw["conv_w"][l] = dcw[:KW]
        dproj = _glu_bwd(du, s["proj"], dproj, lay["a"], cc, f"glu_bwd_{l}")
        do = _mm(dym, mw["w_mo"], "nt", name=f"mla_out_do_{l}", out_dtype=BF16)
        dwmo = _mm(s["o"], dym, "tn", name=f"mla_out_dw_{l}").reshape(N_HEADS, HP, d)[:, HP - V_DIM:].reshape(vw, d)
        gw["w_mla_out"][l] = _cols_shards(dwmo)
        comm = _join(chips(big, qb1), pair(ffn2_w, 0)) if riding_rs else None
        dq, dk, dv, got = _flash_bwd(s["q"], s["k"], s["kv"], do, s["o"], s["lse"], f"flash_bwd_{l}", comm)
        if riding_rs:
            parts1, sib_f2 = split(comm, got)
            sum_parts(big, 1, qb1, dict(zip(big, parts1)), "l1")
            qb_f2 = add_parts(ffn2_w, 0, dict(zip(ffn2_w, sib_f2)), "f2")
        dproj, dqr, dkv, cqn, ckvn, gw["cq_norm"][l], gw["ckv_norm"][l], dqn, dkn = _mla_pre_bwd(
            s["proj"], lay, l, cq_norm, ckv_norm, qn_k, kn_k, mw["w_uq"], mw["w_ukv"], tabs, dq, dk, dv, dproj,
            f"mla_pre_bwd_{l}")
        gw["q_norm"][l], gw["k_norm"][l] = dqn[:, :QK], dkn[:, :QK]
        dwuq = _mm(dqr, cqn, "tn", name=f"uq_dw_{l}").reshape(N_HEADS, HP, ql)[:, :QK]
        gw["w_uq"][l] = dwuq.reshape(NCHIP, N_HEADS * QK // NCHIP, ql)
        gw["w_ukv"][l] = _cols_shards(_mm(ckvn, dkv, "tn", name=f"ukv_dw_{l}"))
        dhm = _mm(dproj, mw["w_in"], "nn", name=f"proj_dh_{l}")
        dwin = _mm(dproj, s["hm"], "tn", name=f"proj_dw_{l}")
        gw["w_in"][l] = jnp.concatenate([dwin[2 * d:2 * d + nat_g], dwin[:2 * d]], axis=0).reshape(
            NCHIP, (nat_g + 2 * d) // NCHIP, d)
        dx1, gw["mix_norm"][l], got = _rms_back(s["x1"], mix_norm, l, dhm, dx2, f"mix_norm_bwd_{l}",
                                                pair(["w_in"], 0) if riding_rs else None)
        if not riding_rs:
            dx, _ = ffn_back("ffn1", s["x0"], ffn1_norm, s["gt1"], s["up1"], dx1)
            continue
        qb_win = add_parts(["w_in"], 0, {"w_in": got[0]}, "win")
        comm = _join(share(big, 1), chips(ffn2_w, qb_f2), chips(["w_in"], qb_win), pair(mix_rest, 0))
        wg, wu, wd = (bufs[f"ffn1_w_{p}", l] for p in ("gate", "up", "down"))
        dx, dgain, dgt, dup, act, hb, dob, got = _ffn_bwd(s["x0"], ffn1_norm, l, wg, wu, wd, s["gt1"], s["up1"], dx1,
                                                         f"ffn1_bwd_{l}", comm)
        gw["ffn1_norm"][l] = dgain
        shared, parts_f2, parts_win, sib_mx = split(comm, got)
        red.update(zip(big, shared))
        sum_parts(ffn2_w, 0, qb_f2, dict(zip(ffn2_w, parts_f2)), "f2")
        sum_parts(["w_in"], 0, qb_win, {"w_in": parts_win[0]}, "win")
        qb_mx = add_parts(mix_rest, 0, dict(zip(mix_rest, sib_mx)), "mx")
        vec_names = ["ffn1_norm", "mix_norm", "ffn2_norm", "conv_b", "conv_ln_g", "conv_ln_b", "cq_norm", "ckv_norm",
                     "q_norm", "k_norm"]
        rows = [jnp.concatenate(gw[nm], axis=0) for nm in vec_names]
        rows += [jnp.concatenate(gw["gate_bias"], axis=0), jnp.concatenate(gw["conv_w"], axis=0), loss_row]
        parts_mx, f1 = {}, [f"ffn1_w_{p}" for p in ("gate", "up", "down")]
        qb_f1 = {}
        for part, (lhs, rhs) in enumerate(zip((dgt, dup, act), (hb, hb, dob))):
            cms = [_spread_comm(_rows_pad(rows, d))] if part == 0 else [pair(f1[part - 1:part], 0)]
            if mix_parts[part]:
                cms.append(chips(mix_parts[part], qb_mx))
            if part == 2:
                cms.append(chips(f1[0:1], qb_f1))
            comm = _join(*cms)
            gw[f1[part]][l], got = _mm(lhs, rhs, "tn", name=f"{f1[part]}_dw_{l}", a_g=True, comm=comm)
            pieces = split(comm, got)
            if part == 0:
                small_blocks = pieces[0][0]
            else:
                qb_f1.update(add_parts(f1[part - 1:part], 0, dict(zip(f1[part - 1:part], pieces[0])), f"f1{part - 1}"))
            if mix_parts[part]:
                parts_mx.update(zip(mix_parts[part], pieces[1]))
            if part == 2:
                sum_parts(f1[0:1], 0, qb_f1, dict(zip(f1[0:1], pieces[-1])), "f10")
        sum_parts(mix_rest, 0, qb_mx, parts_mx, "mx")

    sib = dict(zip(f1[2:3], _comm_call(pair(f1[2:3], 0), "rs_pair")))
    qb_f1.update(add_parts(f1[2:3], 0, sib, "f12"))
    parts_f1 = dict(zip(f1[1:3], _comm_call(chips(f1[1:3], qb_f1), "rs_chips")))
    sum_parts(f1[1:3], 0, qb_f1, parts_f1, "f1")
    red = dict(zip(big, _comm_call(share(big, 0), "rs_share")))
    grads = {nm: g.reshape(wts[nm].shape) for nm, g in red.items()}

    total = _sum_devices(small_blocks, "allreduce_small")
    r0 = 0
    for nm in vec_names:
        grads[nm] = total[r0:r0 + nl, :wts[nm].shape[1]]
        r0 += nl
    gb_all = total[r0:r0 + 2 * nl, :d].reshape(nl, 2, NCHIP, d // NCHIP)
    r0 += 2 * nl
    cw_all = total[r0:r0 + KW * nl, :cc].reshape(nl, KW, NCHIP, cc // NCHIP)
    r0 += KW * nl
    grads["gate_bias"] = lax.dynamic_index_in_dim(gb_all, chip, axis=2, keepdims=False)
    grads["conv_w"] = lax.dynamic_index_in_dim(cw_all, chip, axis=2, keepdims=False)
    loss = total[r0, 0]

    delta, new_m, new_v = {}, {}, {}
    by_shape = {}
    for nm in names:
        shp = wts[nm].shape
        by_shape.setdefault((shp[0] * (shp[1] if len(shp) == 3 else 1), shp[-1]), []).append(nm)
    for gi, (shp2, grp) in enumerate(by_shape.items()):
        to2 = lambda a: a.reshape(shp2)
        ds_, ms_, vs_ = _adamw([to2(wts[nm]) for nm in grp], [to2(grads[nm]) for nm in grp],
                               [to2(mom[nm]) for nm in grp], [to2(var[nm]) for nm in grp], f"adamw_{gi}")
        for nm, dd, mm_, vv in zip(grp, ds_, ms_, vs_):
            delta[nm], new_m[nm], new_v[nm] = (a.reshape(wts[nm].shape) for a in (dd, mm_, vv))

    return (loss, dx[None], *[view(nm, grads[nm]) for nm in names], *[view(nm, delta[nm]) for nm in names],
            *[view(nm, new_m[nm]) for nm in names], *[view(nm, new_v[nm]) for nm in names])
```

```python
import functools

import jax
import jax.numpy as jnp
from jax import lax
from jax.experimental import pallas as pl
from jax.experimental.pallas import tpu as pltpu

F32, BF16 = jnp.float32, jnp.bfloat16
SDS = jax.ShapeDtypeStruct
MESH = pl.DeviceIdType.MESH
ANY = pl.BlockSpec(memory_space=pl.ANY)

NCHIP = 4
N_DEV = 8
N_HEADS, NOPE, ROPE, V_DIM = 8, 64, 32, 64
QK = NOPE + ROPE
HP = 128
CHUNK = 64
KW = 31
CPAD = 32
ROPE_THETA = 10000.0
EPS = 1e-6
LR, B1, B2, EPS_ADAM, WD, STEP = 0.001, 0.9, 0.999, 1e-08, 0.01, 10
VMEM_BIG = 48 << 20
FFN_PARTS = 2
FLASH_HEADS = 4


def _cparams(sem=None, vmem=None):
    kw = {}
    if sem is not None:
        kw["dimension_semantics"] = sem
    if vmem is not None:
        kw["vmem_limit_bytes"] = vmem
    return pltpu.CompilerParams(**kw)


def _rt(t):
    return min(512, t // 2)


def _pick(n, cands):
    for c in cands:
        if c <= n and n % c == 0:
            return c
    return n


def _tile2(r, c, nblocks, row_mult):
    tr, tc = r, c
    while 2 * nblocks * tr * tc * 4 > VMEM_BIG // 2 and tr % (2 * row_mult) == 0:
        tr //= 2
    while 2 * nblocks * tr * tc * 4 > VMEM_BIG // 2 and tc % 256 == 0:
        tc //= 2
    return tr, tc


def _sig(v):
    return 1.0 / (1.0 + jnp.exp(-v))


def _rms_r(v):
    return lax.rsqrt(jnp.mean(v * v, axis=-1, keepdims=True) + EPS)


def _rms_bwd(xv, g, dy):
    r = _rms_r(xv)
    xh = xv * r
    dg = jnp.sum(dy * xh, axis=0, keepdims=True)
    dxh = dy * g
    dx = r * (dxh - xh * jnp.mean(dxh * xh, axis=-1, keepdims=True))
    return dx, dg


def _dot(a, b):
    return jnp.dot(a, b, preferred_element_type=F32)


def _dot_nt(a, b):
    return lax.dot_general(a, b, (((1,), (1,)), ((), ())), preferred_element_type=F32)


def _dot_tn(a, b):
    return lax.dot_general(a, b, (((0,), (0,)), ((), ())), preferred_element_type=F32)


class _Comm:
    def __init__(self, ins, out_shapes, aliases, nsem, start, finish, mid=None):
        self.ins, self.out_shapes, self.aliases, self.nsem = list(ins), list(out_shapes), dict(aliases), nsem
        self.start, self.finish, self.mid = start, finish, mid


def _call(body, comm, first, last, *, name, grid, in_specs, args, out_specs, out_shape, scratch, sem, mid_at=None):
    in_specs, args, out_specs, out_shape, scratch = map(list, (in_specs, args, out_specs, out_shape, scratch))
    n_in, n_out, n_scr = len(args), len(out_shape), len(scratch)
    aliases = {}
    kern = body
    if comm is not None:
        nci, nco = len(comm.ins), len(comm.out_shapes)
        o0 = n_in + nci
        s0 = o0 + n_out + nco

        def kern(*refs):
            cins, couts = refs[n_in:o0], refs[o0 + n_out:s0]
            ssem, rsem = refs[s0 + n_scr:]
            pl.when(first())(lambda: comm.start(cins, couts, ssem, rsem))
            if comm.mid is not None and mid_at is not None:
                pl.when(mid_at())(lambda: comm.mid(cins, couts, ssem, rsem))
            body(*refs[:n_in], *refs[o0:o0 + n_out], *refs[s0:s0 + n_scr])

            def end():
                if comm.mid is not None and mid_at is None:
                    comm.mid(cins, couts, ssem, rsem)
                comm.finish(cins, couts, ssem, rsem)

            pl.when(last())(end)

        in_specs += [ANY] * nci
        args += comm.ins
        out_specs += [ANY] * nco
        out_shape += comm.out_shapes
        scratch += [pltpu.SemaphoreType.DMA((comm.nsem,))] * 2
        aliases = {n_in + i: n_out + o for i, o in comm.aliases.items()}
    outs = pl.pallas_call(kern, name=name, grid=grid, in_specs=in_specs, out_specs=out_specs, out_shape=out_shape,
                          scratch_shapes=scratch, input_output_aliases=aliases,
                          compiler_params=_cparams(sem, VMEM_BIG))(*args)
    return outs[:n_out], outs[n_out:]


class _SemView:
    def __init__(self, sems, base):
        self.sems, self.base = sems, base

    @property
    def at(self):
        return self

    def __getitem__(self, k):
        return self.sems.at[self.base + k]


def _join(*comms):
    ins, outs, aliases, spans, nsem = [], [], {}, [], 0
    for cm in comms:
        aliases.update({len(ins) + i: len(outs) + o for i, o in cm.aliases.items()})
        spans.append((len(ins), len(ins) + len(cm.ins), len(outs), len(outs) + len(cm.out_shapes), nsem))
        ins += cm.ins
        outs += cm.out_shapes
        nsem += cm.nsem

    def run(which, cins, couts, ssem, rsem):
        for cm, (i0, i1, o0, o1, base) in zip(comms, spans):
            if getattr(cm, which) is not None:
                getattr(cm, which)(cins[i0:i1], couts[o0:o1], _SemView(ssem, base), _SemView(rsem, base))

    joined = _Comm(ins, outs, aliases, nsem, functools.partial(run, "start"), functools.partial(run, "finish"),
                   functools.partial(run, "mid") if any(cm.mid is not None for cm in comms) else None)
    joined.spans = [(o0, o1) for _, _, o0, o1, _ in spans]
    return joined


def _comm_call(comm, name):
    once = lambda: pl.program_id(0) == 0
    return _call(lambda: None, comm, once, once, name=name, grid=(1,), in_specs=[], args=[], out_specs=[],
                 out_shape=[], scratch=[], sem=("arbitrary",))[1]


def _mm(a, b, mode, *, name, out_dtype=F32, res=None, a_g=False, comm=None):
    a2 = a.shape[1:] if a_g else a.shape
    if mode == "nn":
        (m, k), (k2, n) = a2, b.shape
    elif mode == "nt":
        (m, k), (n, k2) = a2, b.shape
    else:
        (k, m), (k2, n) = a2, b.shape
    assert k == k2, (name, a.shape, b.shape)
    g = a.shape[0] if a_g else 1
    tm = m if m <= 768 else _pick(m, (768, 512, 256, 128))
    tn = n if n <= 1280 else _pick(n, (1280, 1024, 768, 512, 256, 128))
    tk = k if k <= 1280 else _pick(k, (1280, 1024, 768, 512, 256, 128))
    fits = lambda kk, nn: 2 * (kk * tm * a.dtype.itemsize + kk * nn * b.dtype.itemsize + tm * nn * 4) <= VMEM_BIG - (8 << 20)
    if fits(k, n):
        tn, tk = n, k
    elif fits(k, tn):
        tk = k
    nk = k // tk
    dn = {"nn": (((1,), (0,)), ((), ())), "nt": (((1,), (1,)), ((), ())), "tn": (((0,), (0,)), ((), ()))}[mode]

    def body(*refs):
        a_ref, b_ref = refs[0], refs[1]
        res_ref = refs[2] if res is not None else None
        o_ref = refs[3] if res is not None else refs[2]
        p = lax.dot_general(a_ref[...].astype(BF16), b_ref[...].astype(BF16), dn, preferred_element_type=F32)

        def fin(v):
            if res_ref is not None:
                v = v + res_ref[...]
            o_ref[...] = v.astype(out_dtype)

        if nk == 1:
            fin(p)
        else:
            acc_ref = refs[-1]
            kk = pl.program_id(3)

            @pl.when(kk == 0)
            def _():
                acc_ref[...] = p

            @pl.when(kk > 0)
            def _():
                acc_ref[...] += p

            @pl.when(kk == nk - 1)
            def _():
                fin(acc_ref[...])

    a_block, a_idx = ((tk, tm), lambda gg, i, j, kk: (kk, i)) if mode == "tn" else ((tm, tk), lambda gg, i, j, kk: (i, kk))
    if a_g:
        a_spec = pl.BlockSpec((None,) + a_block, lambda gg, i, j, kk: (gg,) + a_idx(gg, i, j, kk))
    else:
        a_spec = pl.BlockSpec(a_block, a_idx)
    if mode == "nt":
        b_spec = pl.BlockSpec((tn, tk), lambda gg, i, j, kk: (j, kk))
    else:
        b_spec = pl.BlockSpec((tk, tn), lambda gg, i, j, kk: (kk, j))
    in_specs, args = [a_spec, b_spec], [a, b]
    if res is not None:
        in_specs.append(pl.BlockSpec((tm, tn), lambda gg, i, j, kk: (i, j)))
        args.append(res)
    if a_g:
        out_spec, oshape = pl.BlockSpec((None, tm, tn), lambda gg, i, j, kk: (gg, i, j)), (g, m, n)
    else:
        out_spec, oshape = pl.BlockSpec((tm, tn), lambda gg, i, j, kk: (i, j)), (m, n)
    grid = (g, m // tm, n // tn, nk)

    def at(corner):
        hit = pl.program_id(0) == corner[0]
        for ax in range(1, 4):
            hit = jnp.logical_and(hit, pl.program_id(ax) == corner[ax])
        return hit

    outs, couts = _call(body, comm, lambda: at((0, 0, 0, 0)), lambda: at(tuple(dim - 1 for dim in grid)), name=name,
                        grid=grid, in_specs=in_specs, args=args, out_specs=[out_spec], out_shape=[SDS(oshape, out_dtype)],
                        scratch=[pltpu.VMEM((tm, tn), F32)] if nk > 1 else [], sem=("arbitrary",) * 4)
    return outs[0] if comm is None else (outs[0], couts)


def _ffn_fwd(x, gains, l, wg, wu, wd, name, comm=None, mid_frac=0.75):
    t, d = x.shape
    f = wg.shape[-2]
    tt = _rt(t)

    def body(x_ref, g_ref, wg_ref, wu_ref, wd_ref, o_ref, gt_ref, up_ref, h_sc, acc_sc):
        j = pl.program_id(1)

        @pl.when(j == 0)
        def _():
            xv = x_ref[...]
            h_sc[...] = (xv * _rms_r(xv) * g_ref[l:l + 1, :]).astype(BF16)
            acc_sc[...] = jnp.zeros_like(acc_sc)

        for part in range(FFN_PARTS):
            rows = pl.ds(part * (tt // FFN_PARTS), tt // FFN_PARTS)
            h = h_sc[rows, :]
            gt = _dot_nt(h, wg_ref[...]).astype(BF16)
            up = _dot_nt(h, wu_ref[...]).astype(BF16)
            gt_ref[rows, :] = gt
            up_ref[rows, :] = up
            acc_sc[rows, :] += _dot(gt * _sig(gt) * up, wd_ref[...])

        @pl.when(j == NCHIP - 1)
        def _():
            o_ref[...] = x_ref[...] + 0.5 * acc_sc[...]

    wspec = pl.BlockSpec((None, f, d), lambda i, j: (j, 0, 0))
    row = pl.BlockSpec((tt, d), lambda i, j: (i, 0))
    sh = pl.BlockSpec((None, tt, f), lambda i, j: (j, i, 0))
    ni = t // tt
    first = lambda: jnp.logical_and(pl.program_id(0) == 0, pl.program_id(1) == 0)
    last = lambda: jnp.logical_and(pl.program_id(0) == ni - 1, pl.program_id(1) == NCHIP - 1)
    mid_at = lambda: pl.program_id(0) * NCHIP + pl.program_id(1) == int(mid_frac * ni * NCHIP)
    outs, couts = _call(
        body, comm, first, last, name=name, grid=(ni, NCHIP), mid_at=mid_at,
        in_specs=[row, pl.BlockSpec(gains.shape, lambda i, j: (0, 0)), wspec, wspec, wspec], args=[x, gains, wg, wu, wd],
        out_specs=[row, sh, sh], out_shape=[SDS((t, d), F32), SDS((NCHIP, t, f), BF16), SDS((NCHIP, t, f), BF16)],
        scratch=[pltpu.VMEM((tt, d), BF16), pltpu.VMEM((tt, d), F32)], sem=("arbitrary", "arbitrary"))
    return (*outs, couts)


def _ffn_bwd(x, gains, l, wg, wu, wd, gts, ups, dout, name, comm=None):
    t, d = x.shape
    f = wg.shape[-2]
    tt = _rt(t)

    def body(x_ref, g_ref, wg_ref, wu_ref, wd_ref, gt_ref, up_ref, do_ref,
             dx_ref, dg_ref, dgt_ref, dup_ref, act_ref, h_ref, dob_ref, dh_sc):
        i, j = pl.program_id(0), pl.program_id(1)

        @pl.when(j == 0)
        def _():
            xv = x_ref[...]
            h_ref[...] = (xv * _rms_r(xv) * g_ref[l:l + 1, :]).astype(BF16)
            dob_ref[...] = (0.5 * do_ref[...]).astype(BF16)
            dh_sc[...] = jnp.zeros_like(dh_sc)

        @pl.when(jnp.logical_and(i == 0, j == 0))
        def _():
            dg_ref[...] = jnp.zeros_like(dg_ref)

        for part in range(FFN_PARTS):
            rows = pl.ds(part * (tt // FFN_PARTS), tt // FFN_PARTS)
            dact = _dot_nt(dob_ref[rows, :], wd_ref[...]).astype(BF16)
            gt = gt_ref[rows, :]
            up = up_ref[rows, :]
            s = _sig(gt)
            sl = gt * s
            dup = dact * sl
            dgt = dact * up * (s + sl * (1.0 - s))
            dgt_ref[rows, :] = dgt
            dup_ref[rows, :] = dup
            act_ref[rows, :] = sl * up
            dh_sc[rows, :] += _dot(dgt, wg_ref[...]) + _dot(dup, wu_ref[...])

        @pl.when(j == NCHIP - 1)
        def _():
            dxn, dg = _rms_bwd(x_ref[...], g_ref[l:l + 1, :], dh_sc[...])
            dx_ref[...] = do_ref[...] + dxn
            dg_ref[...] += dg

    wspec = pl.BlockSpec((None, f, d), lambda i, j: (j, 0, 0))
    row = pl.BlockSpec((tt, d), lambda i, j: (i, 0))
    sh = pl.BlockSpec((None, tt, f), lambda i, j: (j, i, 0))
    ni = t // tt
    first = lambda: jnp.logical_and(pl.program_id(0) == 0, pl.program_id(1) == 0)
    last = lambda: jnp.logical_and(pl.program_id(0) == ni - 1, pl.program_id(1) == NCHIP - 1)
    outs, couts = _call(
        body, comm, first, last, name=name, grid=(ni, NCHIP),
        in_specs=[row, pl.BlockSpec(gains.shape, lambda i, j: (0, 0)), wspec, wspec, wspec, sh, sh, row],
        args=[x, gains, wg, wu, wd, gts, ups, dout],
        out_specs=[row, pl.BlockSpec((1, d), lambda i, j: (0, 0)), sh, sh, sh, row, row],
        out_shape=[SDS((t, d), F32), SDS((1, d), F32)] + [SDS((NCHIP, t, f), BF16)] * 3 + [SDS((t, d), BF16)] * 2,
        scratch=[pltpu.VMEM((tt, d), F32)], sem=("arbitrary", "arbitrary"))
    return (*outs, couts)


def _rms_fwd(x, gains, l, name):
    t, d = x.shape
    tt = _rt(t)

    def body(x_ref, g_ref, o_ref):
        xv = x_ref[...]
        o_ref[...] = (xv * _rms_r(xv) * g_ref[l:l + 1, :]).astype(BF16)

    row = pl.BlockSpec((tt, d), lambda i: (i, 0))
    return pl.pallas_call(body, name=name, grid=(t // tt,), in_specs=[row, pl.BlockSpec(gains.shape, lambda i: (0, 0))],
                          out_specs=row, out_shape=SDS((t, d), BF16), compiler_params=_cparams(("arbitrary",)))(x, gains)


def _rms_back(x, gains, l, dh, dres, name, comm=None):
    t, d = x.shape
    tt = _rt(t)

    def body(x_ref, g_ref, dh_ref, dr_ref, dx_ref, dg_ref):
        @pl.when(pl.program_id(0) == 0)
        def _():
            dg_ref[...] = jnp.zeros_like(dg_ref)

        dxn, dg = _rms_bwd(x_ref[...], g_ref[l:l + 1, :], dh_ref[...])
        dx_ref[...] = dr_ref[...] + dxn
        dg_ref[...] += dg

    row = pl.BlockSpec((tt, d), lambda i: (i, 0))
    ni = t // tt
    outs, couts = _call(body, comm, lambda: pl.program_id(0) == 0, lambda: pl.program_id(0) == ni - 1, name=name,
                        grid=(ni,), in_specs=[row, pl.BlockSpec(gains.shape, lambda i: (0, 0)), row, row],
                        args=[x, gains, dh, dres], out_specs=[row, pl.BlockSpec((1, d), lambda i: (0, 0))],
                        out_shape=[SDS((t, d), F32), SDS((1, d), F32)], scratch=[], sem=("arbitrary",))
    return (*outs, couts)


def _dwconv_fwd(proj, convw, convb, l, o_a, cc, name):
    t = proj.shape[0]
    r = min(256, t)
    nb = cc // HP

    def body(a_ref, gate_ref, w_ref, b_ref, uc_ref, u_sc):
        u_sc[0:CPAD, :] = jnp.zeros((CPAD, HP), F32)

        def fill(ci, carry):
            r0 = pl.multiple_of(ci * r, r)
            u_sc[pl.ds(CPAD + r0, r), :] = a_ref[pl.ds(r0, r), :].astype(F32) * _sig(gate_ref[pl.ds(r0, r), :].astype(F32))
            return carry

        lax.fori_loop(0, t // r, fill, 0)
        w = w_ref[l]
        bias = b_ref[l:l + 1, :]

        def conv(ci, carry):
            r0 = pl.multiple_of(ci * r, r)
            win = u_sc[pl.ds(r0, r + CPAD), :]
            acc = jnp.zeros((r, HP), F32) + bias
            for k in range(KW):
                off = CPAD - (KW - 1) + k
                acc = acc + win[off:off + r, :] * w[k:k + 1, :]
            uc_ref[pl.ds(r0, r), :] = acc
            return carry

        lax.fori_loop(0, t // r, conv, 0)

    col = lambda base: pl.BlockSpec((t, HP), lambda c: (0, base // HP + c))
    return pl.pallas_call(
        body, name=name, grid=(nb,),
        in_specs=[col(o_a), col(o_a + cc), pl.BlockSpec((convw.shape[0], CPAD, HP), lambda c: (0, 0, c)),
                  pl.BlockSpec((convb.shape[0], HP), lambda c: (0, c))],
        out_specs=pl.BlockSpec((t, HP), lambda c: (0, c)), out_shape=SDS((t, cc), F32),
        scratch_shapes=[pltpu.VMEM((t + CPAD, HP), F32)],
        compiler_params=_cparams(("arbitrary",), VMEM_BIG))(proj, proj, convw, convb)


def _dwconv_bwd(duc, proj, convw, l, o_a, cc, name):
    t = proj.shape[0]
    r = min(256, t)
    nb = cc // HP

    def body(d_ref, a_ref, gate_ref, w_ref, du_ref, dw_ref, u_sc, d_sc, dw_sc):
        u_sc[0:CPAD, :] = jnp.zeros((CPAD, HP), F32)
        d_sc[t:t + CPAD, :] = jnp.zeros((CPAD, HP), F32)
        dw_sc[...] = jnp.zeros_like(dw_sc)

        def fill(ci, carry):
            r0 = pl.multiple_of(ci * r, r)
            u_sc[pl.ds(CPAD + r0, r), :] = a_ref[pl.ds(r0, r), :].astype(F32) * _sig(gate_ref[pl.ds(r0, r), :].astype(F32))
            d_sc[pl.ds(r0, r), :] = d_ref[pl.ds(r0, r), :]
            return carry

        lax.fori_loop(0, t // r, fill, 0)
        w = w_ref[l]

        def conv(ci, carry):
            r0 = pl.multiple_of(ci * r, r)
            dwin = d_sc[pl.ds(r0, r + CPAD), :]
            uwin = u_sc[pl.ds(r0, r + CPAD), :]
            dcur = dwin[0:r, :]
            acc = jnp.zeros((r, HP), F32)
            for k in range(KW):
                acc = acc + dwin[KW - 1 - k:KW - 1 - k + r, :] * w[k:k + 1, :]
                off = CPAD - (KW - 1) + k
                part = (dcur * uwin[off:off + r, :]).reshape(r // 8, 8, HP).sum(axis=0)
                dw_sc[8 * k:8 * k + 8, :] += part
            du_ref[pl.ds(r0, r), :] = acc
            return carry

        lax.fori_loop(0, t // r, conv, 0)
        dw_ref[...] = jnp.zeros_like(dw_ref)
        for k in range(KW):
            dw_ref[k:k + 1, :] = jnp.sum(dw_sc[8 * k:8 * k + 8, :], axis=0, keepdims=True)

    col = lambda base: pl.BlockSpec((t, HP), lambda c: (0, base // HP + c))
    return pl.pallas_call(
        body, name=name, grid=(nb,),
        in_specs=[pl.BlockSpec((t, HP), lambda c: (0, c)), col(o_a), col(o_a + cc),
                  pl.BlockSpec((convw.shape[0], CPAD, HP), lambda c: (0, 0, c))],
        out_specs=[pl.BlockSpec((t, HP), lambda c: (0, c)), pl.BlockSpec((CPAD, HP), lambda c: (0, c))],
        out_shape=[SDS((t, cc), F32), SDS((CPAD, cc), F32)],
        scratch_shapes=[pltpu.VMEM((t + CPAD, HP), F32), pltpu.VMEM((t + CPAD, HP), F32), pltpu.VMEM((8 * CPAD, HP), F32)],
        compiler_params=_cparams(("arbitrary",), VMEM_BIG))(duc, proj, proj, convw)


def _ln_parts(uc, g, b):
    mu = jnp.mean(uc, axis=-1, keepdims=True)
    xc = uc - mu
    r = lax.rsqrt(jnp.mean(xc * xc, axis=-1, keepdims=True) + EPS)
    xh = xc * r
    return r, xh, xh * g + b


def _ln_silu(uc, ln_g, ln_b, l, name):
    t, cc = uc.shape
    tt = _rt(t)

    def body(u_ref, g_ref, b_ref, s_ref):
        _, _, yv = _ln_parts(u_ref[...], g_ref[l:l + 1, :], b_ref[l:l + 1, :])
        s_ref[...] = (yv * _sig(yv)).astype(BF16)

    row = pl.BlockSpec((tt, cc), lambda i: (i, 0))
    full = pl.BlockSpec(ln_g.shape, lambda i: (0, 0))
    return pl.pallas_call(body, name=name, grid=(t // tt,), in_specs=[row, full, full], out_specs=row,
                          out_shape=SDS((t, cc), BF16), compiler_params=_cparams(("arbitrary",)))(uc, ln_g, ln_b)


def _ln_silu_bwd(uc, ln_g, ln_b, l, ds, name):
    t, cc = uc.shape
    tt = _rt(t)

    def body(u_ref, g_ref, b_ref, ds_ref, du_ref, dg_ref, db_ref, dcb_ref):
        @pl.when(pl.program_id(0) == 0)
        def _():
            dg_ref[...] = jnp.zeros_like(dg_ref)
            db_ref[...] = jnp.zeros_like(db_ref)
            dcb_ref[...] = jnp.zeros_like(dcb_ref)

        g = g_ref[l:l + 1, :]
        r, xh, yv = _ln_parts(u_ref[...], g, b_ref[l:l + 1, :])
        sy = _sig(yv)
        dy = ds_ref[...] * (sy * (1.0 + yv * (1.0 - sy)))
        dg_ref[...] += jnp.sum(dy * xh, axis=0, keepdims=True)
        db_ref[...] += jnp.sum(dy, axis=0, keepdims=True)
        dxh = dy * g
        du = r * (dxh - jnp.mean(dxh, axis=-1, keepdims=True) - xh * jnp.mean(dxh * xh, axis=-1, keepdims=True))
        du_ref[...] = du
        dcb_ref[...] += jnp.sum(du, axis=0, keepdims=True)

    row = pl.BlockSpec((tt, cc), lambda i: (i, 0))
    full = pl.BlockSpec(ln_g.shape, lambda i: (0, 0))
    vec = pl.BlockSpec((1, cc), lambda i: (0, 0))
    return pl.pallas_call(body, name=name, grid=(t // tt,), in_specs=[row, full, full, row],
                          out_specs=[row, vec, vec, vec], out_shape=[SDS((t, cc), F32)] + [SDS((1, cc), F32)] * 3,
                          compiler_params=_cparams(("arbitrary",)))(uc, ln_g, ln_b, ds)


def _glu_bwd(du, proj, dproj, o_a, cc, name):
    t = du.shape[0]
    tt = _rt(t)

    def body(du_ref, a_ref, gate_ref, prev_ref, o_ref):
        sg = _sig(gate_ref[...].astype(F32))
        dv = du_ref[...]
        o_ref[:, 0:cc] = (dv * sg).astype(BF16)
        o_ref[:, cc:2 * cc] = (dv * a_ref[...] * sg * (1.0 - sg)).astype(BF16)

    return pl.pallas_call(
        body, name=name, grid=(t // tt,),
        in_specs=[pl.BlockSpec((tt, cc), lambda i: (i, 0)), pl.BlockSpec((tt, cc), lambda i: (i, o_a // cc)),
                  pl.BlockSpec((tt, cc), lambda i: (i, o_a // cc + 1)), ANY],
        out_specs=pl.BlockSpec((tt, 2 * cc), lambda i: (i, o_a // (2 * cc))),
        out_shape=SDS(dproj.shape, dproj.dtype), input_output_aliases={3: 0},
        compiler_params=_cparams(("arbitrary",)))(du, proj, proj, dproj)


def _rope(v, cs, s1, s2):
    return v * cs + pltpu.roll(v, HP - ROPE // 2, 1) * s1 + pltpu.roll(v, ROPE // 2, 1) * s2


def _rope_t(dv, cs, s1, s2):
    return dv * cs + pltpu.roll(dv * s1, ROPE // 2, 1) + pltpu.roll(dv * s2, HP - ROPE // 2, 1)


def _head_norm(v, g):
    r = lax.rsqrt(jnp.sum(v * v, axis=-1, keepdims=True) * (1.0 / QK) + EPS)
    return v * r * g, r


def _head_norm_bwd(v, r, g, dy):
    xh = v * r
    dg = jnp.sum(dy * xh, axis=0, keepdims=True)
    dxh = dy * g
    dx = r * (dxh - xh * (jnp.sum(dxh * xh, axis=-1, keepdims=True) * (1.0 / QK)))
    return dx, dg


def _mla_specs(tt, lay, cq_norm, ckv_norm, qn, kn, wuq, wukv):
    ql, kvl = cq_norm.shape[1], ckv_norm.shape[1]
    full = lambda a: pl.BlockSpec(a.shape, lambda i: (0,) * a.ndim)
    tab = pl.BlockSpec((tt, HP), lambda i: (i, 0))
    return [pl.BlockSpec((tt, ql), lambda i: (i, lay["cq"] // ql)),
            pl.BlockSpec((tt, kvl), lambda i: (i, lay["ckv"] // kvl)),
            pl.BlockSpec((tt, HP), lambda i: (i, lay["kr"] // HP)),
            full(cq_norm), full(ckv_norm), full(qn), full(kn), full(wuq), full(wukv), tab, tab, tab]


def _mla_pre_fwd(proj, lay, l, cq_norm, ckv_norm, qn, kn, wuq, wukv, tabs, name):
    t = proj.shape[0]
    tt = _rt(t)
    hw = N_HEADS * HP

    def body(cq_ref, ckv_ref, kr_ref, gq_ref, gkv_ref, qn_ref, kn_ref, wq_ref, wkv_ref, c_ref, s1_ref, s2_ref,
             q_ref, k_ref, v_ref):
        cq = cq_ref[...].astype(F32)
        cqn = (cq * _rms_r(cq) * gq_ref[l:l + 1, :]).astype(BF16)
        ckv = ckv_ref[...].astype(F32)
        ckvn = (ckv * _rms_r(ckv) * gkv_ref[l:l + 1, :]).astype(BF16)
        qraw = _dot_nt(cqn, wq_ref[...])
        kv = _dot(ckvn, wkv_ref[...])
        v_ref[...] = kv.astype(BF16)
        lane = lax.broadcasted_iota(jnp.int32, (tt, HP), 1)
        krs = pltpu.roll(jnp.where(lane < ROPE, kr_ref[...].astype(F32), 0.0), NOPE, 1)
        cs, s1, s2 = c_ref[...], s1_ref[...], s2_ref[...]
        gq, gk = qn_ref[l:l + 1, :], kn_ref[l:l + 1, :]
        for h in range(N_HEADS):
            sl = slice(h * HP, (h + 1) * HP)
            qh, _ = _head_norm(qraw[:, sl], gq)
            q_ref[:, sl] = (_rope(qh, cs, s1, s2) * QK ** -0.5).astype(BF16)
            kh, _ = _head_norm(jnp.where(lane < NOPE, kv[:, sl], krs), gk)
            k_ref[:, sl] = _rope(kh, cs, s1, s2).astype(BF16)

    row = pl.BlockSpec((tt, hw), lambda i: (i, 0))
    return pl.pallas_call(
        body, name=name, grid=(t // tt,),
        in_specs=_mla_specs(tt, lay, cq_norm, ckv_norm, qn, kn, wuq, wukv),
        out_specs=[row, row, row], out_shape=[SDS((t, hw), BF16)] * 3,
        compiler_params=_cparams(("arbitrary",), VMEM_BIG))(
            proj, proj, proj, cq_norm, ckv_norm, qn, kn, wuq, wukv, *tabs)


def _mla_pre_bwd(proj, lay, l, cq_norm, ckv_norm, qn, kn, wuq, wukv, tabs, dq, dk, dv, dproj, name):
    t = proj.shape[0]
    tt = _rt(t)
    hw = N_HEADS * HP
    ql, kvl = cq_norm.shape[1], ckv_norm.shape[1]
    wm = lay["wm"]

    def body(cq_ref, ckv_ref, kr_ref, gq_ref, gkv_ref, qn_ref, kn_ref, wq_ref, wkv_ref, c_ref, s1_ref, s2_ref,
             dq_ref, dk_ref, dv_ref, prev_ref,
             o_ref, dqr_ref, dkv_ref, cqn_ref, ckvn_ref, dgq_ref, dgkv_ref, dqn_ref, dkn_ref):
        @pl.when(pl.program_id(0) == 0)
        def _():
            for ref in (dgq_ref, dgkv_ref, dqn_ref, dkn_ref):
                ref[...] = jnp.zeros_like(ref)

        cq = cq_ref[...].astype(F32)
        cqn = (cq * _rms_r(cq) * gq_ref[l:l + 1, :]).astype(BF16)
        ckv = ckv_ref[...].astype(F32)
        ckvn = (ckv * _rms_r(ckv) * gkv_ref[l:l + 1, :]).astype(BF16)
        cqn_ref[...] = cqn
        ckvn_ref[...] = ckvn
        qraw = _dot_nt(cqn, wq_ref[...])
        kv = _dot(ckvn, wkv_ref[...])
        lane = lax.broadcasted_iota(jnp.int32, (tt, HP), 1)
        krs = pltpu.roll(jnp.where(lane < ROPE, kr_ref[...].astype(F32), 0.0), NOPE, 1)
        cs, s1, s2 = c_ref[...], s1_ref[...], s2_ref[...]
        gq, gk = qn_ref[l:l + 1, :], kn_ref[l:l + 1, :]
        dkr = jnp.zeros((tt, HP), F32)
        dgq = jnp.zeros((1, HP), F32)
        dgk = jnp.zeros((1, HP), F32)
        for h in range(N_HEADS):
            sl = slice(h * HP, (h + 1) * HP)
            qh = qraw[:, sl]
            _, rq = _head_norm(qh, gq)
            dqh, dg = _head_norm_bwd(qh, rq, gq, _rope_t(dq_ref[:, sl] * QK ** -0.5, cs, s1, s2))
            dgq = dgq + dg
            dqr_ref[:, sl] = dqh.astype(BF16)
            kp = jnp.where(lane < NOPE, kv[:, sl], krs)
            _, rk = _head_norm(kp, gk)
            dkp, dg = _head_norm_bwd(kp, rk, gk, _rope_t(dk_ref[:, sl], cs, s1, s2))
            dgk = dgk + dg
            dkv_ref[:, sl] = (jnp.where(lane < NOPE, dkp, 0.0) + dv_ref[:, sl]).astype(BF16)
            dkr = dkr + dkp
        dqn_ref[...] += dgq
        dkn_ref[...] += dgk
        dkr = jnp.where(lane < ROPE, pltpu.roll(dkr, HP - NOPE, 1), 0.0)
        dcq, dg = _rms_bwd(cq, gq_ref[l:l + 1, :], _dot(dqr_ref[...], wq_ref[...]))
        dgq_ref[...] += dg
        dckv, dg = _rms_bwd(ckv, gkv_ref[l:l + 1, :], _dot_nt(dkv_ref[...], wkv_ref[...]))
        dgkv_ref[...] += dg
        o_ref[:, 0:ql] = dcq.astype(BF16)
        o_ref[:, ql:ql + kvl] = dckv.astype(BF16)
        o_ref[:, ql + kvl:ql + kvl + HP] = dkr.astype(BF16)
        o_ref[:, ql + kvl + HP:wm] = jnp.zeros((tt, wm - ql - kvl - HP), BF16)

    row = pl.BlockSpec((tt, hw), lambda i: (i, 0))
    vec = lambda n: pl.BlockSpec((1, n), lambda i: (0, 0))
    return pl.pallas_call(
        body, name=name, grid=(t // tt,),
        in_specs=_mla_specs(tt, lay, cq_norm, ckv_norm, qn, kn, wuq, wukv) + [row, row, row, ANY],
        out_specs=[pl.BlockSpec((tt, wm), lambda i: (i, lay["cq"] // wm)), row, row,
                   pl.BlockSpec((tt, ql), lambda i: (i, 0)), pl.BlockSpec((tt, kvl), lambda i: (i, 0)),
                   vec(ql), vec(kvl), vec(HP), vec(HP)],
        out_shape=[SDS(dproj.shape, dproj.dtype), SDS((t, hw), BF16), SDS((t, hw), BF16), SDS((t, ql), BF16),
                   SDS((t, kvl), BF16), SDS((1, ql), F32), SDS((1, kvl), F32), SDS((1, HP), F32), SDS((1, HP), F32)],
        input_output_aliases={15: 0},
        compiler_params=_cparams(("arbitrary",), VMEM_BIG))(
            proj, proj, proj, cq_norm, ckv_norm, qn, kn, wuq, wukv, *tabs, dq, dk, dv, dproj)


def _comb_fwd(proj, gate_bias, l, yc, ym, name):
    t, d = yc.shape
    tt = _rt(t)

    def body(p_ref, b_ref, yc_ref, ym_ref, y_ref):
        b = b_ref[l]
        g0 = _sig(p_ref[:, 0:d] + b[0:1, :])
        g1 = _sig(p_ref[:, d:2 * d] + b[1:2, :])
        y_ref[...] = (g0 * yc_ref[...] + g1 * ym_ref[...]).astype(BF16)

    row = pl.BlockSpec((tt, d), lambda i: (i, 0))
    return pl.pallas_call(
        body, name=name, grid=(t // tt,),
        in_specs=[pl.BlockSpec((tt, 2 * d), lambda i: (i, 0)), pl.BlockSpec(gate_bias.shape, lambda i: (0, 0, 0)), row, row],
        out_specs=row, out_shape=SDS((t, d), BF16), compiler_params=_cparams(("arbitrary",)))(proj, gate_bias, yc, ym)


def _comb_bwd(proj, gate_bias, l, yc, ym, dy, dp_cols, name):
    t, d = yc.shape
    tt = _rt(t)

    def body(p_ref, b_ref, yc_ref, ym_ref, dy_ref, dyc_ref, dym_ref, dp_ref, db_ref):
        @pl.when(pl.program_id(0) == 0)
        def _():
            db_ref[...] = jnp.zeros_like(db_ref)

        b = b_ref[l]
        dyv = dy_ref[...].astype(F32)
        g0 = _sig(p_ref[:, 0:d] + b[0:1, :])
        g1 = _sig(p_ref[:, d:2 * d] + b[1:2, :])
        dyc_ref[...] = (dyv * g0).astype(BF16)
        dym_ref[...] = (dyv * g1).astype(BF16)
        dg0 = dyv * yc_ref[...] * g0 * (1.0 - g0)
        dg1 = dyv * ym_ref[...] * g1 * (1.0 - g1)
        dp_ref[:, 0:d] = dg0.astype(BF16)
        dp_ref[:, d:2 * d] = dg1.astype(BF16)
        db_ref[0:1, :] += jnp.sum(dg0, axis=0, keepdims=True)
        db_ref[1:2, :] += jnp.sum(dg1, axis=0, keepdims=True)

    row = pl.BlockSpec((tt, d), lambda i: (i, 0))
    wide = pl.BlockSpec((tt, 2 * d), lambda i: (i, 0))
    return pl.pallas_call(
        body, name=name, grid=(t // tt,),
        in_specs=[wide, pl.BlockSpec(gate_bias.shape, lambda i: (0, 0, 0)), row, row, row],
        out_specs=[row, row, wide, pl.BlockSpec((2, d), lambda i: (0, 0))],
        out_shape=[SDS((t, d), BF16), SDS((t, d), BF16), SDS((t, dp_cols), BF16), SDS((2, d), F32)],
        compiler_params=_cparams(("arbitrary",)))(proj, gate_bias, yc, ym, dy)


def _loss_grad(y, target, name):
    t, d = y.shape
    tt = _rt(t)
    nt = t // tt

    def body(y_ref, t_ref, dy_ref, loss_ref, acc_sc):
        i = pl.program_id(0)

        @pl.when(i == 0)
        def _():
            acc_sc[...] = jnp.zeros_like(acc_sc)

        diff = y_ref[...] - t_ref[...]
        dy_ref[...] = diff * (1.0 / d)
        acc_sc[...] += jnp.sum(diff * diff, axis=0, keepdims=True)

        @pl.when(i == nt - 1)
        def _():
            tot = jnp.sum(acc_sc[...], axis=1, keepdims=True) * (0.5 / d)
            loss_ref[...] = jnp.broadcast_to(tot, (1, HP))

    row = pl.BlockSpec((tt, d), lambda i: (i, 0))
    return pl.pallas_call(body, name=name, grid=(nt,), in_specs=[row, row],
                          out_specs=[row, pl.BlockSpec((1, HP), lambda i: (0, 0))],
                          out_shape=[SDS((t, d), F32), SDS((1, HP), F32)],
                          scratch_shapes=[pltpu.VMEM((1, d), F32)],
                          compiler_params=_cparams(("arbitrary",)))(y, target)


def _chunk_mask(tq):
    rows = lax.broadcasted_iota(jnp.int32, (tq, tq), 0) // CHUNK
    cols = lax.broadcasted_iota(jnp.int32, (tq, tq), 1) // CHUNK
    return cols <= rows


NEG = -1e30


def _flash_fwd(q, k, v, name, comm=None):
    t = q.shape[0]
    tq = _rt(t)
    nq = t // tq
    rep = tq // HP

    def body(q_ref, k_ref, v_ref, o_ref, lse_ref):
        qi = pl.program_id(1)

        def one(hh, ki, carry, masked):
            m_prev, l_prev, acc = carry
            lanes = slice(hh * HP, (hh + 1) * HP)
            r0 = pl.multiple_of(ki * tq, tq)
            s = _dot_nt(q_ref[:, lanes], k_ref[pl.ds(r0, tq), lanes])
            if masked:
                s = jnp.where(_chunk_mask(tq), s, NEG)
            m_new = jnp.maximum(m_prev, jnp.max(s, axis=-1, keepdims=True))
            a = jnp.exp(m_prev - m_new)
            p = jnp.exp(s - jnp.tile(m_new, (1, rep)))
            l_new = a * l_prev + jnp.sum(p, axis=-1, keepdims=True)
            acc = a * acc + _dot(p.astype(BF16), v_ref[pl.ds(r0, tq), lanes])
            return m_new, l_new, acc

        def step(ki, carries, masked):
            return tuple(one(hh, ki, carries[hh], masked) for hh in range(FLASH_HEADS))

        init = (jnp.full((tq, HP), NEG, F32), jnp.zeros((tq, HP), F32), jnp.zeros((tq, HP), F32))
        carries = lax.fori_loop(0, qi, lambda ki, cr: step(ki, cr, False), (init,) * FLASH_HEADS)
        for hh, (m_fin, l_fin, acc) in enumerate(step(qi, carries, True)):
            o_ref[:, hh * HP:(hh + 1) * HP] = (acc / l_fin).astype(BF16)
            lse_ref[hh] = m_fin + jnp.log(l_fin)

    wide = FLASH_HEADS * HP
    ng = N_HEADS // FLASH_HEADS
    qspec = pl.BlockSpec((tq, wide), lambda h, qi: (qi, h))
    head = pl.BlockSpec((t, wide), lambda h, qi: (0, h))
    first = lambda: jnp.logical_and(pl.program_id(0) == 0, pl.program_id(1) == 0)
    last = lambda: jnp.logical_and(pl.program_id(0) == ng - 1, pl.program_id(1) == nq - 1)
    mid_at = lambda: pl.program_id(0) * nq + pl.program_id(1) == (7 * ng * nq) // 8
    outs, couts = _call(
        body, comm, first, last, name=name, grid=(ng, nq), mid_at=mid_at, in_specs=[qspec, head, head], args=[q, k, v],
        out_specs=[qspec, pl.BlockSpec((FLASH_HEADS, tq, HP), lambda h, qi: (h, qi, 0))],
        out_shape=[SDS(q.shape, BF16), SDS((N_HEADS, t, HP), F32)], scratch=[], sem=("arbitrary",) * 2)
    return (*outs, couts)


def _flash_bwd(q, k, v, do, o, lse, name, comm=None):
    t = q.shape[0]
    tq = _rt(t)
    nq = t // tq
    rep = tq // HP

    def body(q_ref, k_ref, v_ref, do_ref, o_ref, lse_ref, dq_ref, dk_ref, dv_ref, delta_sc):
        def prep(qi, carry):
            r0 = pl.multiple_of(qi * tq, tq)
            rows = pl.ds(r0, tq)
            dlt = jnp.sum(do_ref[rows, :].astype(F32) * o_ref[rows, :].astype(F32), axis=-1, keepdims=True)
            delta_sc[rows, :] = jnp.broadcast_to(dlt, (tq, HP))
            dq_ref[rows, :] = jnp.zeros((tq, HP), F32)
            return carry

        lax.fori_loop(0, nq, prep, 0)

        def keys(ki, carry0):
            krows = pl.ds(pl.multiple_of(ki * tq, tq), tq)
            kt, vt = k_ref[krows, :], v_ref[krows, :]

            def step(qi, carry, masked):
                dk_acc, dv_acc = carry
                rows = pl.ds(pl.multiple_of(qi * tq, tq), tq)
                qt, dot_ = q_ref[rows, :], do_ref[rows, :]
                s = _dot_nt(qt, kt)
                if masked:
                    s = jnp.where(_chunk_mask(tq), s, NEG)
                p = jnp.exp(s - jnp.tile(lse_ref[rows, :], (1, rep)))
                ds = (p * (_dot_nt(dot_, vt) - jnp.tile(delta_sc[rows, :], (1, rep)))).astype(BF16)
                dv_acc = dv_acc + _dot_tn(p.astype(BF16), dot_)
                dk_acc = dk_acc + _dot_tn(ds, qt)
                dq_ref[rows, :] += _dot(ds, kt)
                return dk_acc, dv_acc

            zero = jnp.zeros((tq, HP), F32)
            carry = step(ki, (zero, zero), True)
            dk_acc, dv_acc = lax.fori_loop(ki + 1, nq, lambda qi, cr: step(qi, cr, False), carry)
            dk_ref[krows, :] = dk_acc
            dv_ref[krows, :] = dv_acc
            return carry0

        lax.fori_loop(0, nq, keys, 0)

    head = pl.BlockSpec((t, HP), lambda h: (0, h))
    outs, couts = _call(
        body, comm, lambda: pl.program_id(0) == 0, lambda: pl.program_id(0) == N_HEADS - 1, name=name, grid=(N_HEADS,),
        in_specs=[head] * 5 + [pl.BlockSpec((None, t, HP), lambda h: (h, 0, 0))], args=[q, k, v, do, o, lse],
        out_specs=[head] * 3, out_shape=[SDS(q.shape, F32)] * 3, scratch=[pltpu.VMEM((t, HP), F32)], sem=("arbitrary",))
    return (*outs, couts)


def _add_cast(gs, rs, place, name):
    n = len(gs)
    _, r, c = gs[0].shape
    tr, tc = _tile2(r, c // 2, 3 * n, 16)
    nct = c // 2 // tc

    def body(place_ref, *refs):
        for a in range(n):
            refs[2 * n + a][...] = (refs[a][...] + refs[n + a][...]).astype(BF16)

    gspec = pl.BlockSpec((None, tr, tc), lambda j, i, k, pr: (j, i, pr[0] * nct + k))
    rspec = pl.BlockSpec((None, tr, tc), lambda j, i, k, pr: (j, i, k))
    grid_spec = pltpu.PrefetchScalarGridSpec(num_scalar_prefetch=1, grid=(NCHIP, r // tr, nct),
                                             in_specs=[gspec] * n + [rspec] * n, out_specs=[rspec] * n)
    return pl.pallas_call(body, name=name, grid_spec=grid_spec, out_shape=[SDS((NCHIP, r, c // 2), BF16)] * n,
                          compiler_params=_cparams(("arbitrary",) * 3, VMEM_BIG))(place, *gs, *rs)


def _sum_chips(ss, qs, place, l, nl, prevs, name):
    n = len(ss)
    _, r, h = ss[0].shape
    tr, tc = _tile2(r, h, 3 * n, 16)
    nct = h // tc

    def body(place_ref, *refs):
        for a in range(n):
            acc = refs[n + a][...].astype(F32)
            for kk in range(NCHIP - 1):
                acc = acc + refs[a][kk].astype(F32)
            refs[-n + a][...] = acc

    in_specs = ([pl.BlockSpec((NCHIP - 1, tr, tc), lambda i, k, pr: (0, i, k))] * n
                + [pl.BlockSpec((None, tr, tc), lambda i, k, pr: (pr[1], i, k))] * n)
    args = [*ss, *qs]
    aliases = {}
    if prevs is not None:
        aliases = {1 + len(args) + a: a for a in range(n)}
        in_specs += [ANY] * n
        args += list(prevs)
    grid_spec = pltpu.PrefetchScalarGridSpec(
        num_scalar_prefetch=1, grid=(r // tr, nct), in_specs=in_specs,
        out_specs=[pl.BlockSpec((None, tr, tc), lambda i, k, pr: (l, i, pr[0] * nct + k))] * n)
    return pl.pallas_call(body, name=name, grid_spec=grid_spec, out_shape=[SDS((nl, r, 2 * h), F32)] * n,
                          input_output_aliases=aliases,
                          compiler_params=_cparams(("arbitrary",) * 2, VMEM_BIG))(place, *args)


def _cast_place(ws, place, name):
    n = len(ws)
    nl, r, c = ws[0].shape
    tr, tc = _tile2(r, c, 2 * n * nl, 16)

    def body(place_ref, *refs):
        for a in range(n * nl):
            refs[n * nl + a][...] = refs[a][...].astype(BF16)

    in_specs = [pl.BlockSpec((None, tr, tc), functools.partial(lambda l, i, k, pr: (l, i, k), l))
                for _ in range(n) for l in range(nl)]
    grid_spec = pltpu.PrefetchScalarGridSpec(
        num_scalar_prefetch=1, grid=(r // tr, c // tc), in_specs=in_specs,
        out_specs=[pl.BlockSpec((None, tr, tc), lambda i, k, pr: (pr[1], i, k))] * (n * nl))
    outs = pl.pallas_call(body, name=name, grid_spec=grid_spec, out_shape=[SDS((NCHIP, r, c), BF16)] * (n * nl),
                          compiler_params=_cparams(("arbitrary",) * 2, VMEM_BIG))(
                              place, *[w for w in ws for _ in range(nl)])
    return [outs[a * nl:(a + 1) * nl] for a in range(n)]


def _adamw(ws, gs, ms, vs, name):
    n = len(ws)
    r, c = ws[0].shape
    tr, tc = _tile2(r, c, 7 * n, 8)
    c1, c2 = 1.0 / (1.0 - B1 ** STEP), 1.0 / (1.0 - B2 ** STEP)

    def body(*refs):
        for a in range(n):
            w, g, m, v = (refs[kk * n + a][...] for kk in range(4))
            m2 = B1 * m + (1.0 - B1) * g
            v2 = B2 * v + (1.0 - B2) * (g * g)
            refs[4 * n + a][...] = -LR * ((m2 * c1) / (jnp.sqrt(v2 * c2) + EPS_ADAM) + WD * w)
            refs[5 * n + a][...] = m2
            refs[6 * n + a][...] = v2

    blk = pl.BlockSpec((tr, tc), lambda i, k: (i, k))
    outs = pl.pallas_call(body, name=name, grid=(r // tr, c // tc), in_specs=[blk] * (4 * n),
                          out_specs=[blk] * (3 * n), out_shape=[SDS((r, c), F32)] * (3 * n),
                          compiler_params=_cparams(("arbitrary",) * 2, VMEM_BIG))(*ws, *gs, *ms, *vs)
    return outs[:n], outs[n:2 * n], outs[2 * n:]


def _place():
    x, y, c = lax.axis_index("x"), lax.axis_index("y"), lax.axis_index("c")
    return x, y, c, [(1 - x, y), (x, 1 - y), (1 - x, 1 - y)]


def _rcopy(src, dst, ssem, rsem, k, dev):
    return pltpu.make_async_remote_copy(src_ref=src, dst_ref=dst, send_sem=ssem.at[k], recv_sem=rsem.at[k],
                                        device_id=dev, device_id_type=MESH)


def _half(ref, lead, cc):
    h = ref.shape[-1] // 2
    return ref.at[(*lead, slice(None), pl.ds(cc * h, h))]


def _per_core(fn):
    c = lax.axis_index("c")
    for cc in (0, 1):
        pl.when(c == cc)(functools.partial(fn, cc))


def _gather_comm(bufs):
    n = len(bufs)

    def plan(couts, ssem, rsem, cc):
        x, y, _, peers = _place()
        me, sib = 2 * x + y, (x, y, 1 - cc)
        send, recv, fwd, recv2 = [], [], [], []
        for a in range(n):
            for kk, (px, py) in enumerate(peers):
                mine = _half(couts[a], (me,), cc)
                got = _half(couts[a], (2 * px + py,), cc)
                other = _half(couts[a], (2 * px + py,), 1 - cc)
                send.append(_rcopy(mine, mine, ssem, rsem, a * 6 + kk, (px, py, cc)))
                recv.append(_rcopy(got, got, ssem, rsem, a * 6 + kk, (px, py, cc)))
                fwd.append(_rcopy(got, got, ssem, rsem, a * 6 + 3 + kk, sib))
                recv2.append(_rcopy(other, other, ssem, rsem, a * 6 + 3 + kk, sib))
        return send, recv, fwd, recv2

    def start(cins, couts, ssem, rsem):
        def go(cc):
            for d in plan(couts, ssem, rsem, cc)[0]:
                d.start()

        _per_core(go)

    def mid(cins, couts, ssem, rsem):
        def go(cc):
            _, recv, fwd, _ = plan(couts, ssem, rsem, cc)
            for dr, df in zip(recv, fwd):
                dr.wait_recv()
                df.start()

        _per_core(go)

    def finish(cins, couts, ssem, rsem):
        def go(cc):
            send, _, fwd, recv2 = plan(couts, ssem, rsem, cc)
            for d in recv2:
                d.wait_recv()
            for d in send + fwd:
                d.wait_send()

        _per_core(go)

    return _Comm(bufs, [SDS(b.shape, b.dtype) for b in bufs], {a: a for a in range(n)}, 6 * n, start, finish, mid)


def _pair_comm(gs):
    n = len(gs)
    halves = [g.shape[-1] // 2 for g in gs]

    def plan(cins, couts, ssem, rsem, cc):
        x, y, _, _ = _place()
        return [_rcopy(cins[a].at[:, :, pl.ds((1 - cc) * halves[a], halves[a])], couts[a], ssem, rsem, a, (x, y, 1 - cc))
                for a in range(n)]

    def start(cins, couts, ssem, rsem):
        def go(cc):
            for d in plan(cins, couts, ssem, rsem, cc):
                d.start()

        _per_core(go)

    def finish(cins, couts, ssem, rsem):
        def go(cc):
            ds = plan(cins, couts, ssem, rsem, cc)
            for d in ds:
                d.wait_recv()
            for d in ds:
                d.wait_send()

        _per_core(go)

    return _Comm(gs, [SDS(g.shape[:-1] + (g.shape[-1] // 2,), g.dtype) for g in gs], {}, n, start, finish)


def _chips_comm(qs):
    n = len(qs)

    def plan(cins, couts, ssem, rsem):
        x, y, c, peers = _place()
        return [_rcopy(cins[a].at[2 * px + py], couts[a].at[kk], ssem, rsem, a * 3 + kk, (px, py, c))
                for a in range(n) for kk, (px, py) in enumerate(peers)]

    def start(cins, couts, ssem, rsem):
        for d in plan(cins, couts, ssem, rsem):
            d.start()

    def finish(cins, couts, ssem, rsem):
        ds = plan(cins, couts, ssem, rsem)
        for d in ds:
            d.wait_recv()
        for d in ds:
            d.wait_send()

    return _Comm(qs, [SDS((NCHIP - 1,) + q.shape[1:], q.dtype) for q in qs], {}, 3 * n, start, finish)


def _share_comm(fs, l):
    n = len(fs)

    def plan(couts, ssem, rsem, cc, which):
        x, y, _, _ = _place()
        out = []
        for a in range(n):
            piece = _half(couts[a], (l,), which)
            out.append(_rcopy(piece, piece, ssem, rsem, a, (x, y, 1 - cc)))
        return out

    def start(cins, couts, ssem, rsem):
        def go(cc):
            for d in plan(couts, ssem, rsem, cc, cc):
                d.start()

        _per_core(go)

    def finish(cins, couts, ssem, rsem):
        def go(cc):
            for d in plan(couts, ssem, rsem, cc, 1 - cc):
                d.wait_recv()
            for d in plan(couts, ssem, rsem, cc, cc):
                d.wait_send()

        _per_core(go)

    return _Comm(fs, [SDS(f.shape, f.dtype) for f in fs], {a: a for a in range(n)}, n, start, finish)


def _spread_comm(buf):
    def plan(cins, couts, ssem, rsem):
        x, y, c, _ = _place()
        me = 4 * x + 2 * y + c
        send, recv = [], []
        for rel in range(1, N_DEV):
            px, py, pc = (1 - x if rel & 4 else x, 1 - y if rel & 2 else y, 1 - c if rel & 1 else c)
            slot = couts[0].at[4 * px + 2 * py + pc]
            send.append(_rcopy(cins[0], couts[0].at[me], ssem, rsem, rel - 1, (px, py, pc)))
            recv.append(_rcopy(slot, slot, ssem, rsem, rel - 1, (px, py, pc)))
        own = pltpu.make_async_copy(cins[0], couts[0].at[me], ssem.at[N_DEV - 1])
        return send, recv, own

    def start(cins, couts, ssem, rsem):
        send, _, own = plan(cins, couts, ssem, rsem)
        for dsc in send + [own]:
            dsc.start()

    def finish(cins, couts, ssem, rsem):
        send, recv, own = plan(cins, couts, ssem, rsem)
        for dsc in recv:
            dsc.wait_recv()
        for dsc in send:
            dsc.wait_send()
        own.wait()

    return _Comm([buf], [SDS((N_DEV,) + buf.shape, buf.dtype)], {}, N_DEV, start, finish)


def _sum_devices(blocks, name):
    _, r, w = blocks.shape

    def body(in_ref, out_ref):
        acc = in_ref[0]
        for dv in range(1, N_DEV):
            acc = acc + in_ref[dv]
        out_ref[...] = acc

    vm = pl.BlockSpec(memory_space=pltpu.VMEM)
    return pl.pallas_call(body, name=name, in_specs=[vm], out_specs=vm, out_shape=SDS((r, w), F32),
                          compiler_params=_cparams(None, VMEM_BIG))(blocks)


def _cols_full(g):
    _, k, ns = g.shape
    return g.transpose(1, 0, 2).reshape(k, NCHIP * ns)


def _cols_shards(w):
    k, n = w.shape
    return w.reshape(k, NCHIP, n // NCHIP).transpose(1, 0, 2)


def _rows_pad(rows, width):
    out = jnp.concatenate([jnp.pad(a, ((0, 0), (0, width - a.shape[1]))) for a in rows], axis=0)
    return jnp.pad(out, ((0, -out.shape[0] % 8), (0, 0)))


def kernel(x, positions, ffn1_norm, ffn1_w_gate, ffn1_w_up, ffn1_w_down, mix_norm, w_in, gate_bias, conv_w, conv_b, conv_ln_g, conv_ln_b, w_conv_out, cq_norm, ckv_norm, w_uq, w_ukv, q_norm, k_norm, w_mla_out, w_out, ffn2_norm, ffn2_w_gate, ffn2_w_up, ffn2_w_down, loss_target, m_ffn1_norm, m_ffn1_w_gate, m_ffn1_w_up, m_ffn1_w_down, m_mix_norm, m_w_in, m_gate_bias, m_conv_w, m_conv_b, m_conv_ln_g, m_conv_ln_b, m_w_conv_out, m_cq_norm, m_ckv_norm, m_w_uq, m_w_ukv, m_q_norm, m_k_norm, m_w_mla_out, m_w_out, m_ffn2_norm, m_ffn2_w_gate, m_ffn2_w_up, m_ffn2_w_down, v_ffn1_norm, v_ffn1_w_gate, v_ffn1_w_up, v_ffn1_w_down, v_mix_norm, v_w_in, v_gate_bias, v_conv_w, v_conv_b, v_conv_ln_g, v_conv_ln_b, v_w_conv_out, v_cq_norm, v_ckv_norm, v_w_uq, v_w_ukv, v_q_norm, v_k_norm, v_w_mla_out, v_w_out, v_ffn2_norm, v_ffn2_w_gate, v_ffn2_w_up, v_ffn2_w_down):
    names = ["ffn1_norm", "ffn1_w_gate", "ffn1_w_up", "ffn1_w_down", "mix_norm", "w_in", "gate_bias", "conv_w",
             "conv_b", "conv_ln_g", "conv_ln_b", "w_conv_out", "cq_norm", "ckv_norm", "w_uq", "w_ukv", "q_norm",
             "k_norm", "w_mla_out", "w_out", "ffn2_norm", "ffn2_w_gate", "ffn2_w_up", "ffn2_w_down"]
    env = dict(locals())
    turned = ("ffn1_w_gate", "ffn1_w_up", "ffn2_w_gate", "ffn2_w_up", "w_in", "w_uq")
    view = lambda nm, a: jnp.swapaxes(a, 1, 2) if nm in turned else a
    wts = {nm: view(nm, env[nm]) for nm in names}
    mom = {nm: view(nm, env["m_" + nm]) for nm in names}
    var = {nm: view(nm, env["v_" + nm]) for nm in names}

    t, d = x.shape[1], x.shape[2]
    nl = ffn1_norm.shape[0]
    cc = conv_b.shape[1]
    ql, kvl = cq_norm.shape[1], ckv_norm.shape[1]
    vw = N_HEADS * V_DIM
    hw = N_HEADS * HP
    lay = {"a": 2 * d, "cq": 2 * d + 2 * cc, "ckv": 2 * d + 2 * cc + ql, "kr": 2 * d + 2 * cc + ql + kvl,
           "wm": ql + kvl + 2 * HP}
    dp = lay["cq"] + lay["wm"]
    nat_g = 2 * cc + ql + kvl + ROPE
    assert lay["cq"] % lay["wm"] == 0 and lay["cq"] % ql == 0 and lay["ckv"] % kvl == 0 and lay["a"] % (2 * cc) == 0
    assert cc % HP == 0 and d % HP == 0 and t % (2 * CHUNK) == 0 and w_in.shape[2] * NCHIP == nat_g + 2 * d
    assert nl == 2

    x0, target = x[0], loss_target[0]
    chip = 2 * lax.axis_index("x") + lax.axis_index("y")
    place = jnp.stack([lax.axis_index("c"), chip]).astype(jnp.int32)

    inv_freq = ROPE_THETA ** (-jnp.arange(0, ROPE, 2, dtype=F32) / ROPE)
    ang = positions[0].astype(F32)[:, None] * inv_freq
    cos, sin, z = jnp.cos(ang), jnp.sin(ang), jnp.zeros((t, ROPE // 2), F32)
    tabs = (jnp.concatenate([jnp.ones((t, NOPE), F32), cos, cos, jnp.zeros((t, HP - QK), F32)], axis=1),
            jnp.concatenate([jnp.zeros((t, NOPE), F32), -sin, z, jnp.zeros((t, HP - QK), F32)], axis=1),
            jnp.concatenate([jnp.zeros((t, NOPE), F32), z, sin, jnp.zeros((t, HP - QK), F32)], axis=1))

    big = ["ffn1_w_gate", "ffn1_w_up", "ffn1_w_down", "ffn2_w_gate", "ffn2_w_up", "ffn2_w_down",
           "w_in", "w_conv_out", "w_uq", "w_ukv", "w_mla_out", "w_out"]
    ffn1_w, ffn2_w, mix_w = big[0:3], big[3:6], big[6:]
    bufs, like = {}, {}
    for nm in big:
        like.setdefault(wts[nm].shape, []).append(nm)
    for gi, grp in enumerate(like.values()):
        for nm, per_layer in zip(grp, _cast_place([wts[nm] for nm in grp], place, f"cast_place_{gi}")):
            for l in range(nl):
                bufs[nm, l] = per_layer[l]

    def gather(keys):
        return _gather_comm([bufs[key] for key in keys])

    def landed(keys, outs):
        for key, o in zip(keys, outs):
            bufs[key] = o

    chunk = lambda nms, l: [(nm, l) for nm in nms]
    first_keys = chunk(ffn1_w, 0)
    landed(first_keys, _comm_call(gather(first_keys), "gather_first"))
    ride = {("ffn1", 0): chunk(mix_w, 0), ("flash", 0): chunk(ffn2_w, 0) + chunk(ffn1_w, 1),
            ("ffn2", 0): chunk(mix_w, 1), ("ffn1", 1): chunk(ffn2_w, 1)}

    def riding(kind, l):
        keys = ride.get((kind, l))
        return keys or [], (None if keys is None else gather(keys))

    def mixer_weights(l):
        w_in_nat = bufs["w_in", l].reshape(nat_g + 2 * d, d)
        w_in_k = jnp.concatenate([w_in_nat[nat_g:], w_in_nat[:nat_g], jnp.zeros((dp - nat_g - 2 * d, d), BF16)], axis=0)
        w_uq_k = jnp.pad(bufs["w_uq", l].reshape(N_HEADS, QK, ql), ((0, 0), (0, HP - QK), (0, 0))).reshape(hw, ql)
        w_mo_k = jnp.pad(_cols_full(bufs["w_mla_out", l]).reshape(N_HEADS, V_DIM, d),
                         ((0, 0), (HP - V_DIM, 0), (0, 0))).reshape(hw, d)
        return dict(w_in=w_in_k, w_co=_cols_full(bufs["w_conv_out", l]), w_uq=w_uq_k,
                    w_ukv=_cols_full(bufs["w_ukv", l]), w_mo=w_mo_k, w_out=bufs["w_out", l].reshape(d, d))

    qn_k = jnp.pad(q_norm, ((0, 0), (0, HP - QK)))
    kn_k = jnp.pad(k_norm, ((0, 0), (0, HP - QK)))
    small = _rows_pad([gate_bias.reshape(nl * 2, d // NCHIP), conv_w.reshape(nl * KW, cc // NCHIP)], d)

    saved, mixw = [], []
    xc = x0
    for l in range(nl):
        keys, comm = riding("ffn1", l)
        if l == 0:
            comm = _join(comm, _spread_comm(small))
        x1, gt1, up1, got = _ffn_fwd(xc, ffn1_norm, l, *[bufs[nm, l] for nm in ffn1_w], f"ffn1_fwd_{l}", comm,
                                     0.75 if l == 0 else 0.9)
        landed(keys, got)
        if l == 0:
            everyone = got[-1][0::2]
            gb_k = everyone[:, :nl * 2, :d // NCHIP].reshape(NCHIP, nl, 2, d // NCHIP).transpose(1, 2, 0, 3).reshape(nl, 2, d)
            cw = everyone[:, nl * 2:nl * 2 + nl * KW, :cc // NCHIP].reshape(NCHIP, nl, KW, cc // NCHIP)
            cw_k = jnp.pad(cw.transpose(1, 2, 0, 3).reshape(nl, KW, cc), ((0, 0), (0, CPAD - KW), (0, 0)))
        mw = mixer_weights(l)
        mixw.append(mw)
        hm = _rms_fwd(x1, mix_norm, l, f"mix_norm_{l}")
        proj = _mm(hm, mw["w_in"], "nt", name=f"proj_{l}", out_dtype=BF16)
        uc = _dwconv_fwd(proj, cw_k, conv_b, l, lay["a"], cc, f"dwconv_{l}")
        sc = _ln_silu(uc, conv_ln_g, conv_ln_b, l, f"conv_ln_{l}")
        yc = _mm(sc, mw["w_co"], "nn", name=f"conv_out_{l}", out_dtype=BF16)
        q, k, kv = _mla_pre_fwd(proj, lay, l, cq_norm, ckv_norm, qn_k, kn_k, mw["w_uq"], mw["w_ukv"], tabs, f"mla_pre_{l}")
        keys, comm = riding("flash", l)
        o, lse, got = _flash_fwd(q, k, kv, f"flash_{l}", comm)
        landed(keys, got)
        ym = _mm(o, mw["w_mo"], "nn", name=f"mla_out_{l}", out_dtype=BF16)
        yv = _comb_fwd(proj, gb_k, l, yc, ym, f"combine_{l}")
        x2 = _mm(yv, mw["w_out"], "nn", name=f"mix_out_{l}", res=x1)
        keys, comm = riding("ffn2", l)
        x3, gt2, up2, got = _ffn_fwd(x2, ffn2_norm, l, *[bufs[nm, l] for nm in ffn2_w], f"ffn2_fwd_{l}", comm)
        landed(keys, got)
        saved.append(dict(x0=xc, x1=x1, gt1=gt1, up1=up1, hm=hm, proj=proj, uc=uc, sc=sc, yc=yc, q=q, k=k, kv=kv,
                          o=o, lse=lse, ym=ym, yv=yv, x2=x2, gt2=gt2, up2=up2))
        xc = x3
    dx, loss_row = _loss_grad(xc, target, "loss")

    gw = {nm: [None] * nl for nm in names}
    red = {}

    def shard_groups(nms, l):
        same = {}
        for nm in nms:
            same.setdefault(gw[nm][l].shape, []).append(nm)
        return list(same.values())

    def add_parts(nms, l, sib_part, tag):
        qb = {}
        for gi, grp in enumerate(shard_groups(nms, l)):
            outs = _add_cast([gw[nm][l] for nm in grp], [sib_part[nm] for nm in grp], place, f"rs_add_{tag}_{gi}")
            qb.update(zip(grp, outs))
        return qb

    def sum_parts(nms, l, qb, parts, tag):
        for gi, grp in enumerate(shard_groups(nms, l)):
            prevs = [red[nm] for nm in grp] if grp[0] in red else None
            outs = _sum_chips([parts[nm] for nm in grp], [qb[nm] for nm in grp], place, l, nl, prevs, f"rs_sum_{tag}_{gi}")
            red.update(zip(grp, outs))

    pair = lambda nms, l: _pair_comm([gw[nm][l] for nm in nms])
    chips = lambda nms, qb: _chips_comm([qb[nm] for nm in nms])
    share = lambda nms, l: _share_comm([red[nm] for nm in nms], l)

    def split(joined, got):
        return [got[o0:o1] for o0, o1 in joined.spans]

    mix_rest = [nm for nm in mix_w if nm != "w_in"]
    mix_parts = [["w_out", "w_conv_out"], ["w_mla_out", "w_ukv", "w_uq"], []]

    for l in reversed(range(nl)):
        s, mw = saved[l], mixw[l]
        riding_rs = l == 0

        def ffn_back(tag, xin, gains, gts, ups, dout, comm=None):
            wg, wu, wd = (bufs[f"{tag}_w_{p}", l] for p in ("gate", "up", "down"))
            dxi, dgain, dgt, dup, act, hb, dob, got = _ffn_bwd(xin, gains, l, wg, wu, wd, gts, ups, dout,
                                                              f"{tag}_bwd_{l}", comm)
            for p, lhs, rhs in zip(("gate", "up", "down"), (dgt, dup, act), (hb, hb, dob)):
                gw[f"{tag}_w_{p}"][l] = _mm(lhs, rhs, "tn", name=f"{tag}_dw_{p}_{l}", a_g=True)
            gw[f"{tag}_norm"][l] = dgain
            return dxi, got

        dx2, got = ffn_back("ffn2", s["x2"], ffn2_norm, s["gt2"], s["up2"], dx, pair(big, 1) if riding_rs else None)
        if riding_rs:
            qb1 = add_parts(big, 1, dict(zip(big, got)), "l1")
        dyv = _mm(dx2, mw["w_out"], "nt", name=f"mix_out_dy_{l}", out_dtype=BF16)
        gw["w_out"][l] = _mm(s["yv"], dx2, "tn", name=f"mix_out_dw_{l}").reshape(NCHIP, d // NCHIP, d)
        dyc, dym, dproj, dgb = _comb_bwd(s["proj"], gb_k, l, s["yc"], s["ym"], dyv, dp, f"combine_bwd_{l}")
        gw["gate_bias"][l] = dgb
        dsc = _mm(dyc, mw["w_co"], "nt", name=f"conv_out_ds_{l}")
        gw["w_conv_out"][l] = _cols_shards(_mm(s["sc"], dyc, "tn", name=f"conv_out_dw_{l}"))
        duc, gw["conv_ln_g"][l], gw["conv_ln_b"][l], gw["conv_b"][l] = _ln_silu_bwd(
            s["uc"], conv_ln_g, conv_ln_b, l, dsc, f"conv_ln_bwd_{l}")
        du, dcw = _dwconv_bwd(duc, s["proj"], cw_k, l, lay["a"], cc, f"dwconv_bwd_{l}")
        gw["conv_w"][l] = dcw[:KW]
        dproj = _glu_bwd(du, s["proj"], dproj, lay["a"], cc, f"glu_bwd_{l}")
        do = _mm(dym, mw["w_mo"], "nt", name=f"mla_out_do_{l}", out_dtype=BF16)
        dwmo = _mm(s["o"], dym, "tn", name=f"mla_out_dw_{l}").reshape(N_HEADS, HP, d)[:, HP - V_DIM:].reshape(vw, d)
        gw["w_mla_out"][l] = _cols_shards(dwmo)
        comm = _join(chips(big, qb1), pair(ffn2_w, 0)) if riding_rs else None
        dq, dk, dv, got = _flash_bwd(s["q"], s["k"], s["kv"], do, s["o"], s["lse"], f"flash_bwd_{l}", comm)
        if riding_rs:
            parts1, sib_f2 = split(comm, got)
            sum_parts(big, 1, qb1, dict(zip(big, parts1)), "l1")
            qb_f2 = add_parts(ffn2_w, 0, dict(zip(ffn2_w, sib_f2)), "f2")
        dproj, dqr, dkv, cqn, ckvn, gw["cq_norm"][l], gw["ckv_norm"][l], dqn, dkn = _mla_pre_bwd(
            s["proj"], lay, l, cq_norm, ckv_norm, qn_k, kn_k, mw["w_uq"], mw["w_ukv"], tabs, dq, dk, dv, dproj,
            f"mla_pre_bwd_{l}")
        gw["q_norm"][l], gw["k_norm"][l] = dqn[:, :QK], dkn[:, :QK]
        dwuq = _mm(dqr, cqn, "tn", name=f"uq_dw_{l}").reshape(N_HEADS, HP, ql)[:, :QK]
        gw["w_uq"][l] = dwuq.reshape(NCHIP, N_HEADS * QK // NCHIP, ql)
        gw["w_ukv"][l] = _cols_shards(_mm(ckvn, dkv, "tn", name=f"ukv_dw_{l}"))
        dhm = _mm(dproj, mw["w_in"], "nn", name=f"proj_dh_{l}")
        dwin = _mm(dproj, s["hm"], "tn", name=f"proj_dw_{l}")
        gw["w_in"][l] = jnp.concatenate([dwin[2 * d:2 * d + nat_g], dwin[:2 * d]], axis=0).reshape(
            NCHIP, (nat_g + 2 * d) // NCHIP, d)
        dx1, gw["mix_norm"][l], got = _rms_back(s["x1"], mix_norm, l, dhm, dx2, f"mix_norm_bwd_{l}",
                                                pair(["w_in"], 0) if riding_rs else None)
        if not riding_rs:
            dx, _ = ffn_back("ffn1", s["x0"], ffn1_norm, s["gt1"], s["up1"], dx1)
            continue
        qb_win = add_parts(["w_in"], 0, {"w_in": got[0]}, "win")
        comm = _join(share(big, 1), chips(ffn2_w, qb_f2), chips(["w_in"], qb_win), pair(mix_rest, 0))
        wg, wu, wd = (bufs[f"ffn1_w_{p}", l] for p in ("gate", "up", "down"))
        dx, dgain, dgt, dup, act, hb, dob, got = _ffn_bwd(s["x0"], ffn1_norm, l, wg, wu, wd, s["gt1"], s["up1"], dx1,
                                                         f"ffn1_bwd_{l}", comm)
        gw["ffn1_norm"][l] = dgain
        shared, parts_f2, parts_win, sib_mx = split(comm, got)
        red.update(zip(big, shared))
        sum_parts(ffn2_w, 0, qb_f2, dict(zip(ffn2_w, parts_f2)), "f2")
        sum_parts(["w_in"], 0, qb_win, {"w_in": parts_win[0]}, "win")
        qb_mx = add_parts(mix_rest, 0, dict(zip(mix_rest, sib_mx)), "mx")
        vec_names = ["ffn1_norm", "mix_norm", "ffn2_norm", "conv_b", "conv_ln_g", "conv_ln_b", "cq_norm", "ckv_norm",
                     "q_norm", "k_norm"]
        rows = [jnp.concatenate(gw[nm], axis=0) for nm in vec_names]
        rows += [jnp.concatenate(gw["gate_bias"], axis=0), jnp.concatenate(gw["conv_w"], axis=0), loss_row]
        parts_mx, f1 = {}, [f"ffn1_w_{p}" for p in ("gate", "up", "down")]
        qb_f1 = {}
        for part, (lhs, rhs) in enumerate(zip((dgt, dup, act), (hb, hb, dob))):
            cms = [_spread_comm(_rows_pad(rows, d))] if part == 0 else [pair(f1[part - 1:part], 0)]
            if mix_parts[part]:
                cms.append(chips(mix_parts[part], qb_mx))
            if part == 2:
                cms.append(chips(f1[0:1], qb_f1))
            comm = _join(*cms)
            gw[f1[part]][l], got = _mm(lhs, rhs, "tn", name=f"{f1[part]}_dw_{l}", a_g=True, comm=comm)
            pieces = split(comm, got)
            if part == 0:
                small_blocks = pieces[0][0]
            else:
                qb_f1.update(add_parts(f1[part - 1:part], 0, dict(zip(f1[part - 1:part], pieces[0])), f"f1{part - 1}"))
            if mix_parts[part]:
                parts_mx.update(zip(mix_parts[part], pieces[1]))
            if part == 2:
                sum_parts(f1[0:1], 0, qb_f1, dict(zip(f1[0:1], pieces[-1])), "f10")
        sum_parts(mix_rest, 0, qb_mx, parts_mx, "mx")

    sib = dict(zip(f1[2:3], _comm_call(pair(f1[2:3], 0), "rs_pair")))
    qb_f1.update(add_parts(f1[2:3], 0, sib, "f12"))
    parts_f1 = dict(zip(f1[1:3], _comm_call(chips(f1[1:3], qb_f1), "rs_chips")))
    sum_parts(f1[1:3], 0, qb_f1, parts_f1, "f1")
    red = dict(zip(big, _comm_call(share(big, 0), "rs_share")))
    grads = {nm: g.reshape(wts[nm].shape) for nm, g in red.items()}

    total = _sum_devices(small_blocks, "allreduce_small")
    r0 = 0
    for nm in vec_names:
        grads[nm] = total[r0:r0 + nl, :wts[nm].shape[1]]
        r0 += nl
    gb_all = total[r0:r0 + 2 * nl, :d].reshape(nl, 2, NCHIP, d // NCHIP)
    r0 += 2 * nl
    cw_all = total[r0:r0 + KW * nl, :cc].reshape(nl, KW, NCHIP, cc // NCHIP)
    r0 += KW * nl
    grads["gate_bias"] = lax.dynamic_index_in_dim(gb_all, chip, axis=2, keepdims=False)
    grads["conv_w"] = lax.dynamic_index_in_dim(cw_all, chip, axis=2, keepdims=False)
    loss = total[r0, 0]

    delta, new_m, new_v = {}, {}, {}
    by_shape = {}
    for nm in names:
        shp = wts[nm].shape
        by_shape.setdefault((shp[0] * (shp[1] if len(shp) == 3 else 1), shp[-1]), []).append(nm)
    for gi, (shp2, grp) in enumerate(by_shape.items()):
        to2 = lambda a: a.reshape(shp2)
        ds_, ms_, vs_ = _adamw([to2(wts[nm]) for nm in grp], [to2(grads[nm]) for nm in grp],
                               [to2(mom[nm]) for nm in grp], [to2(var[nm]) for nm in grp], f"adamw_{gi}")
        for nm, dd, mm_, vv in zip(grp, ds_, ms_, vs_):
            delta[nm], new_m[nm], new_v[nm] = (a.reshape(wts[nm].shape) for a in (dd, mm_, vv))

    return (loss, dx[None], *[view(nm, grads[nm]) for nm in names], *[view(nm, delta[nm]) for nm in names],
            *[view(nm, new_m[nm]) for nm in names], *[view(nm, new_v[nm]) for nm in names])
```

```python
import functools

import jax
import jax.numpy as jnp
from jax import lax
from jax.experimental import pallas as pl
from jax.experimental.pallas import tpu as pltpu

F32, BF16 = jnp.float32, jnp.bfloat16
SDS = jax.ShapeDtypeStruct
MESH = pl.DeviceIdType.MESH
ANY = pl.BlockSpec(memory_space=pl.ANY)

NCHIP = 4
N_DEV = 8
N_HEADS, NOPE, ROPE, V_DIM = 8, 64, 32, 64
QK = NOPE + ROPE
HP = 128
CHUNK = 64
KW = 31
CPAD = 32
ROPE_THETA = 10000.0
EPS = 1e-6
LR, B1, B2, EPS_ADAM, WD, STEP = 0.001, 0.9, 0.999, 1e-08, 0.01, 10
VMEM_BIG = 48 << 20
FFN_PARTS = 2
FLASH_HEADS = 2


def _cparams(sem=None, vmem=None):
    kw = {}
    if sem is not None:
        kw["dimension_semantics"] = sem
    if vmem is not None:
        kw["vmem_limit_bytes"] = vmem
    return pltpu.CompilerParams(**kw)


def _rt(t):
    return min(512, t // 2)


def _pick(n, cands):
    for c in cands:
        if c <= n and n % c == 0:
            return c
    return n


def _tile2(r, c, nblocks, row_mult):
    tr, tc = r, c
    while 2 * nblocks * tr * tc * 4 > VMEM_BIG // 2 and tr % (2 * row_mult) == 0:
        tr //= 2
    while 2 * nblocks * tr * tc * 4 > VMEM_BIG // 2 and tc % 256 == 0:
        tc //= 2
    return tr, tc


def _sig(v):
    return 1.0 / (1.0 + jnp.exp(-v))


def _rms_r(v):
    return lax.rsqrt(jnp.mean(v * v, axis=-1, keepdims=True) + EPS)


def _rms_bwd(xv, g, dy):
    r = _rms_r(xv)
    xh = xv * r
    dg = jnp.sum(dy * xh, axis=0, keepdims=True)
    dxh = dy * g
    dx = r * (dxh - xh * jnp.mean(dxh * xh, axis=-1, keepdims=True))
    return dx, dg


def _dot(a, b):
    return jnp.dot(a, b, preferred_element_type=F32)


def _dot_nt(a, b):
    return lax.dot_general(a, b, (((1,), (1,)), ((), ())), preferred_element_type=F32)


def _dot_tn(a, b):
    return lax.dot_general(a, b, (((0,), (0,)), ((), ())), preferred_element_type=F32)


class _Comm:
    def __init__(self, ins, out_shapes, aliases, nsem, start, finish, mid=None):
        self.ins, self.out_shapes, self.aliases, self.nsem = list(ins), list(out_shapes), dict(aliases), nsem
        self.start, self.finish, self.mid = start, finish, mid


def _call(body, comm, first, last, *, name, grid, in_specs, args, out_specs, out_shape, scratch, sem, mid_at=None):
    in_specs, args, out_specs, out_shape, scratch = map(list, (in_specs, args, out_specs, out_shape, scratch))
    n_in, n_out, n_scr = len(args), len(out_shape), len(scratch)
    aliases = {}
    kern = body
    if comm is not None:
        nci, nco = len(comm.ins), len(comm.out_shapes)
        o0 = n_in + nci
        s0 = o0 + n_out + nco

        def kern(*refs):
            cins, couts = refs[n_in:o0], refs[o0 + n_out:s0]
            ssem, rsem = refs[s0 + n_scr:]
            pl.when(first())(lambda: comm.start(cins, couts, ssem, rsem))
            if comm.mid is not None and mid_at is not None:
                pl.when(mid_at())(lambda: comm.mid(cins, couts, ssem, rsem))
            body(*refs[:n_in], *refs[o0:o0 + n_out], *refs[s0:s0 + n_scr])

            def end():
                if comm.mid is not None and mid_at is None:
                    comm.mid(cins, couts, ssem, rsem)
                comm.finish(cins, couts, ssem, rsem)

            pl.when(last())(end)

        in_specs += [ANY] * nci
        args += comm.ins
        out_specs += [ANY] * nco
        out_shape += comm.out_shapes
        scratch += [pltpu.SemaphoreType.DMA((comm.nsem,))] * 2
        aliases = {n_in + i: n_out + o for i, o in comm.aliases.items()}
    outs = pl.pallas_call(kern, name=name, grid=grid, in_specs=in_specs, out_specs=out_specs, out_shape=out_shape,
                          scratch_shapes=scratch, input_output_aliases=aliases,
                          compiler_params=_cparams(sem, VMEM_BIG))(*args)
    return outs[:n_out], outs[n_out:]


class _SemView:
    def __init__(self, sems, base):
        self.sems, self.base = sems, base

    @property
    def at(self):
        return self

    def __getitem__(self, k):
        return self.sems.at[self.base + k]


def _join(*comms):
    ins, outs, aliases, spans, nsem = [], [], {}, [], 0
    for cm in comms:
        aliases.update({len(ins) + i: len(outs) + o for i, o in cm.aliases.items()})
        spans.append((len(ins), len(ins) + len(cm.ins), len(outs), len(outs) + len(cm.out_shapes), nsem))
        ins += cm.ins
        outs += cm.out_shapes
        nsem += cm.nsem

    def run(which, cins, couts, ssem, rsem):
        for cm, (i0, i1, o0, o1, base) in zip(comms, spans):
            if getattr(cm, which) is not None:
                getattr(cm, which)(cins[i0:i1], couts[o0:o1], _SemView(ssem, base), _SemView(rsem, base))

    joined = _Comm(ins, outs, aliases, nsem, functools.partial(run, "start"), functools.partial(run, "finish"),
                   functools.partial(run, "mid") if any(cm.mid is not None for cm in comms) else None)
    joined.spans = [(o0, o1) for _, _, o0, o1, _ in spans]
    return joined


def _comm_call(comm, name):
    once = lambda: pl.program_id(0) == 0
    return _call(lambda: None, comm, once, once, name=name, grid=(1,), in_specs=[], args=[], out_specs=[],
                 out_shape=[], scratch=[], sem=("arbitrary",))[1]


def _mm(a, b, mode, *, name, out_dtype=F32, res=None, a_g=False, comm=None):
    a2 = a.shape[1:] if a_g else a.shape
    if mode == "nn":
        (m, k), (k2, n) = a2, b.shape
    elif mode == "nt":
        (m, k), (n, k2) = a2, b.shape
    else:
        (k, m), (k2, n) = a2, b.shape
    assert k == k2, (name, a.shape, b.shape)
    g = a.shape[0] if a_g else 1
    tm = m if m <= 768 else _pick(m, (768, 512, 256, 128))
    tn = n if n <= 1280 else _pick(n, (1280, 1024, 768, 512, 256, 128))
    tk = k if k <= 1280 else _pick(k, (1280, 1024, 768, 512, 256, 128))
    fits = lambda kk, nn: 2 * (kk * tm * a.dtype.itemsize + kk * nn * b.dtype.itemsize + tm * nn * 4) <= VMEM_BIG - (8 << 20)
    if fits(k, n):
        tn, tk = n, k
    elif fits(k, tn):
        tk = k
    nk = k // tk
    dn = {"nn": (((1,), (0,)), ((), ())), "nt": (((1,), (1,)), ((), ())), "tn": (((0,), (0,)), ((), ()))}[mode]

    def body(*refs):
        a_ref, b_ref = refs[0], refs[1]
        res_ref = refs[2] if res is not None else None
        o_ref = refs[3] if res is not None else refs[2]
        p = lax.dot_general(a_ref[...].astype(BF16), b_ref[...].astype(BF16), dn, preferred_element_type=F32)

        def fin(v):
            if res_ref is not None:
                v = v + res_ref[...]
            o_ref[...] = v.astype(out_dtype)

        if nk == 1:
            fin(p)
        else:
            acc_ref = refs[-1]
            kk = pl.program_id(3)

            @pl.when(kk == 0)
            def _():
                acc_ref[...] = p

            @pl.when(kk > 0)
            def _():
                acc_ref[...] += p

            @pl.when(kk == nk - 1)
            def _():
                fin(acc_ref[...])

    a_block, a_idx = ((tk, tm), lambda gg, i, j, kk: (kk, i)) if mode == "tn" else ((tm, tk), lambda gg, i, j, kk: (i, kk))
    if a_g:
        a_spec = pl.BlockSpec((None,) + a_block, lambda gg, i, j, kk: (gg,) + a_idx(gg, i, j, kk))
    else:
        a_spec = pl.BlockSpec(a_block, a_idx)
    if mode == "nt":
        b_spec = pl.BlockSpec((tn, tk), lambda gg, i, j, kk: (j, kk))
    else:
        b_spec = pl.BlockSpec((tk, tn), lambda gg, i, j, kk: (kk, j))
    in_specs, args = [a_spec, b_spec], [a, b]
    if res is not None:
        in_specs.append(pl.BlockSpec((tm, tn), lambda gg, i, j, kk: (i, j)))
        args.append(res)
    if a_g:
        out_spec, oshape = pl.BlockSpec((None, tm, tn), lambda gg, i, j, kk: (gg, i, j)), (g, m, n)
    else:
        out_spec, oshape = pl.BlockSpec((tm, tn), lambda gg, i, j, kk: (i, j)), (m, n)
    grid = (g, m // tm, n // tn, nk)

    def at(corner):
        hit = pl.program_id(0) == corner[0]
        for ax in range(1, 4):
            hit = jnp.logical_and(hit, pl.program_id(ax) == corner[ax])
        return hit

    outs, couts = _call(body, comm, lambda: at((0, 0, 0, 0)), lambda: at(tuple(dim - 1 for dim in grid)), name=name,
                        grid=grid, in_specs=in_specs, args=args, out_specs=[out_spec], out_shape=[SDS(oshape, out_dtype)],
                        scratch=[pltpu.VMEM((tm, tn), F32)] if nk > 1 else [], sem=("arbitrary",) * 4)
    return outs[0] if comm is None else (outs[0], couts)


def _ffn_fwd(x, gains, l, wg, wu, wd, name, comm=None, mid_frac=0.75):
    t, d = x.shape
    f = wg.shape[-2]
    tt = _rt(t)

    def body(x_ref, g_ref, wg_ref, wu_ref, wd_ref, o_ref, gt_ref, up_ref, h_sc, acc_sc):
        j = pl.program_id(1)

        @pl.when(j == 0)
        def _():
            xv = x_ref[...]
            h_sc[...] = (xv * _rms_r(xv) * g_ref[l:l + 1, :]).astype(BF16)
            acc_sc[...] = jnp.zeros_like(acc_sc)

        for part in range(FFN_PARTS):
            rows = pl.ds(part * (tt // FFN_PARTS), tt // FFN_PARTS)
            h = h_sc[rows, :]
            gt = _dot_nt(h, wg_ref[...]).astype(BF16)
            up = _dot_nt(h, wu_ref[...]).astype(BF16)
            gt_ref[rows, :] = gt
            up_ref[rows, :] = up
            acc_sc[rows, :] += _dot(gt * _sig(gt) * up, wd_ref[...])

        @pl.when(j == NCHIP - 1)
        def _():
            o_ref[...] = x_ref[...] + 0.5 * acc_sc[...]

    wspec = pl.BlockSpec((None, f, d), lambda i, j: (j, 0, 0))
    row = pl.BlockSpec((tt, d), lambda i, j: (i, 0))
    sh = pl.BlockSpec((None, tt, f), lambda i, j: (j, i, 0))
    ni = t // tt
    first = lambda: jnp.logical_and(pl.program_id(0) == 0, pl.program_id(1) == 0)
    last = lambda: jnp.logical_and(pl.program_id(0) == ni - 1, pl.program_id(1) == NCHIP - 1)
    mid_at = lambda: pl.program_id(0) * NCHIP + pl.program_id(1) == int(mid_frac * ni * NCHIP)
    outs, couts = _call(
        body, comm, first, last, name=name, grid=(ni, NCHIP), mid_at=mid_at,
        in_specs=[row, pl.BlockSpec(gains.shape, lambda i, j: (0, 0)), wspec, wspec, wspec], args=[x, gains, wg, wu, wd],
        out_specs=[row, sh, sh], out_shape=[SDS((t, d), F32), SDS((NCHIP, t, f), BF16), SDS((NCHIP, t, f), BF16)],
        scratch=[pltpu.VMEM((tt, d), BF16), pltpu.VMEM((tt, d), F32)], sem=("arbitrary", "arbitrary"))
    return (*outs, couts)


def _ffn_bwd(x, gains, l, wg, wu, wd, gts, ups, dout, name, comm=None):
    t, d = x.shape
    f = wg.shape[-2]
    tt = _rt(t)

    def body(x_ref, g_ref, wg_ref, wu_ref, wd_ref, gt_ref, up_ref, do_ref,
             dx_ref, dg_ref, dgt_ref, dup_ref, act_ref, h_ref, dob_ref, dh_sc):
        i, j = pl.program_id(0), pl.program_id(1)

        @pl.when(j == 0)
        def _():
            xv = x_ref[...]
            h_ref[...] = (xv * _rms_r(xv) * g_ref[l:l + 1, :]).astype(BF16)
            dob_ref[...] = (0.5 * do_ref[...]).astype(BF16)
            dh_sc[...] = jnp.zeros_like(dh_sc)

        @pl.when(jnp.logical_and(i == 0, j == 0))
        def _():
            dg_ref[...] = jnp.zeros_like(dg_ref)

        for part in range(FFN_PARTS):
            rows = pl.ds(part * (tt // FFN_PARTS), tt // FFN_PARTS)
            dact = _dot_nt(dob_ref[rows, :], wd_ref[...]).astype(BF16)
            gt = gt_ref[rows, :]
            up = up_ref[rows, :]
            s = _sig(gt)
            sl = gt * s
            dup = dact * sl
            dgt = dact * up * (s + sl * (1.0 - s))
            dgt_ref[rows, :] = dgt
            dup_ref[rows, :] = dup
            act_ref[rows, :] = sl * up
            dh_sc[rows, :] += _dot(dgt, wg_ref[...]) + _dot(dup, wu_ref[...])

        @pl.when(j == NCHIP - 1)
        def _():
            dxn, dg = _rms_bwd(x_ref[...], g_ref[l:l + 1, :], dh_sc[...])
            dx_ref[...] = do_ref[...] + dxn
            dg_ref[...] += dg

    wspec = pl.BlockSpec((None, f, d), lambda i, j: (j, 0, 0))
    row = pl.BlockSpec((tt, d), lambda i, j: (i, 0))
    sh = pl.BlockSpec((None, tt, f), lambda i, j: (j, i, 0))
    ni = t // tt
    first = lambda: jnp.logical_and(pl.program_id(0) == 0, pl.program_id(1) == 0)
    last = lambda: jnp.logical_and(pl.program_id(0) == ni - 1, pl.program_id(1) == NCHIP - 1)
    outs, couts = _call(
        body, comm, first, last, name=name, grid=(ni, NCHIP),
        in_specs=[row, pl.BlockSpec(gains.shape, lambda i, j: (0, 0)), wspec, wspec, wspec, sh, sh, row],
        args=[x, gains, wg, wu, wd, gts, ups, dout],
        out_specs=[row, pl.BlockSpec((1, d), lambda i, j: (0, 0)), sh, sh, sh, row, row],
        out_shape=[SDS((t, d), F32), SDS((1, d), F32)] + [SDS((NCHIP, t, f), BF16)] * 3 + [SDS((t, d), BF16)] * 2,
        scratch=[pltpu.VMEM((tt, d), F32)], sem=("arbitrary", "arbitrary"))
    return (*outs, couts)


def _rms_fwd(x, gains, l, name):
    t, d = x.shape
    tt = _rt(t)

    def body(x_ref, g_ref, o_ref):
        xv = x_ref[...]
        o_ref[...] = (xv * _rms_r(xv) * g_ref[l:l + 1, :]).astype(BF16)

    row = pl.BlockSpec((tt, d), lambda i: (i, 0))
    return pl.pallas_call(body, name=name, grid=(t // tt,), in_specs=[row, pl.BlockSpec(gains.shape, lambda i: (0, 0))],
                          out_specs=row, out_shape=SDS((t, d), BF16), compiler_params=_cparams(("arbitrary",)))(x, gains)


def _rms_back(x, gains, l, dh, dres, name, comm=None):
    t, d = x.shape
    tt = _rt(t)

    def body(x_ref, g_ref, dh_ref, dr_ref, dx_ref, dg_ref):
        @pl.when(pl.program_id(0) == 0)
        def _():
            dg_ref[...] = jnp.zeros_like(dg_ref)

        dxn, dg = _rms_bwd(x_ref[...], g_ref[l:l + 1, :], dh_ref[...])
        dx_ref[...] = dr_ref[...] + dxn
        dg_ref[...] += dg

    row = pl.BlockSpec((tt, d), lambda i: (i, 0))
    ni = t // tt
    outs, couts = _call(body, comm, lambda: pl.program_id(0) == 0, lambda: pl.program_id(0) == ni - 1, name=name,
                        grid=(ni,), in_specs=[row, pl.BlockSpec(gains.shape, lambda i: (0, 0)), row, row],
                        args=[x, gains, dh, dres], out_specs=[row, pl.BlockSpec((1, d), lambda i: (0, 0))],
                        out_shape=[SDS((t, d), F32), SDS((1, d), F32)], scratch=[], sem=("arbitrary",))
    return (*outs, couts)


def _dwconv_fwd(proj, convw, convb, l, o_a, cc, name):
    t = proj.shape[0]
    r = min(256, t)
    nb = cc // HP

    def body(a_ref, gate_ref, w_ref, b_ref, uc_ref, u_sc):
        u_sc[0:CPAD, :] = jnp.zeros((CPAD, HP), F32)

        def fill(ci, carry):
            r0 = pl.multiple_of(ci * r, r)
            u_sc[pl.ds(CPAD + r0, r), :] = a_ref[pl.ds(r0, r), :].astype(F32) * _sig(gate_ref[pl.ds(r0, r), :].astype(F32))
            return carry

        lax.fori_loop(0, t // r, fill, 0)
        w = w_ref[l]
        bias = b_ref[l:l + 1, :]

        def conv(ci, carry):
            r0 = pl.multiple_of(ci * r, r)
            win = u_sc[pl.ds(r0, r + CPAD), :]
            acc = jnp.zeros((r, HP), F32) + bias
            for k in range(KW):
                off = CPAD - (KW - 1) + k
                acc = acc + win[off:off + r, :] * w[k:k + 1, :]
            uc_ref[pl.ds(r0, r), :] = acc
            return carry

        lax.fori_loop(0, t // r, conv, 0)

    col = lambda base: pl.BlockSpec((t, HP), lambda c: (0, base // HP + c))
    return pl.pallas_call(
        body, name=name, grid=(nb,),
        in_specs=[col(o_a), col(o_a + cc), pl.BlockSpec((convw.shape[0], CPAD, HP), lambda c: (0, 0, c)),
                  pl.BlockSpec((convb.shape[0], HP), lambda c: (0, c))],
        out_specs=pl.BlockSpec((t, HP), lambda c: (0, c)), out_shape=SDS((t, cc), F32),
        scratch_shapes=[pltpu.VMEM((t + CPAD, HP), F32)],
        compiler_params=_cparams(("arbitrary",), VMEM_BIG))(proj, proj, convw, convb)


def _dwconv_bwd(duc, proj, convw, l, o_a, cc, name):
    t = proj.shape[0]
    r = min(256, t)
    nb = cc // HP

    def body(d_ref, a_ref, gate_ref, w_ref, du_ref, dw_ref, u_sc, d_sc, dw_sc):
        u_sc[0:CPAD, :] = jnp.zeros((CPAD, HP), F32)
        d_sc[t:t + CPAD, :] = jnp.zeros((CPAD, HP), F32)
        dw_sc[...] = jnp.zeros_like(dw_sc)

        def fill(ci, carry):
            r0 = pl.multiple_of(ci * r, r)
            u_sc[pl.ds(CPAD + r0, r), :] = a_ref[pl.ds(r0, r), :].astype(F32) * _sig(gate_ref[pl.ds(r0, r), :].astype(F32))
            d_sc[pl.ds(r0, r), :] = d_ref[pl.ds(r0, r), :]
            return carry

        lax.fori_loop(0, t // r, fill, 0)
        w = w_ref[l]

        def conv(ci, carry):
            r0 = pl.multiple_of(ci * r, r)
            dwin = d_sc[pl.ds(r0, r + CPAD), :]
            uwin = u_sc[pl.ds(r0, r + CPAD), :]
            dcur = dwin[0:r, :]
            acc = jnp.zeros((r, HP), F32)
            for k in range(KW):
                acc = acc + dwin[KW - 1 - k:KW - 1 - k + r, :] * w[k:k + 1, :]
                off = CPAD - (KW - 1) + k
                part = (dcur * uwin[off:off + r, :]).reshape(r // 8, 8, HP).sum(axis=0)
                dw_sc[8 * k:8 * k + 8, :] += part
            du_ref[pl.ds(r0, r), :] = acc
            return carry

        lax.fori_loop(0, t // r, conv, 0)
        dw_ref[...] = jnp.zeros_like(dw_ref)
        for k in range(KW):
            dw_ref[k:k + 1, :] = jnp.sum(dw_sc[8 * k:8 * k + 8, :], axis=0, keepdims=True)

    col = lambda base: pl.BlockSpec((t, HP), lambda c: (0, base // HP + c))
    return pl.pallas_call(
        body, name=name, grid=(nb,),
        in_specs=[pl.BlockSpec((t, HP), lambda c: (0, c)), col(o_a), col(o_a + cc),
                  pl.BlockSpec((convw.shape[0], CPAD, HP), lambda c: (0, 0, c))],
        out_specs=[pl.BlockSpec((t, HP), lambda c: (0, c)), pl.BlockSpec((CPAD, HP), lambda c: (0, c))],
        out_shape=[SDS((t, cc), F32), SDS((CPAD, cc), F32)],
        scratch_shapes=[pltpu.VMEM((t + CPAD, HP), F32), pltpu.VMEM((t + CPAD, HP), F32), pltpu.VMEM((8 * CPAD, HP), F32)],
        compiler_params=_cparams(("arbitrary",), VMEM_BIG))(duc, proj, proj, convw)


def _ln_parts(uc, g, b):
    mu = jnp.mean(uc, axis=-1, keepdims=True)
    xc = uc - mu
    r = lax.rsqrt(jnp.mean(xc * xc, axis=-1, keepdims=True) + EPS)
    xh = xc * r
    return r, xh, xh * g + b


def _ln_silu(uc, ln_g, ln_b, l, name):
    t, cc = uc.shape
    tt = _rt(t)

    def body(u_ref, g_ref, b_ref, s_ref):
        _, _, yv = _ln_parts(u_ref[...], g_ref[l:l + 1, :], b_ref[l:l + 1, :])
        s_ref[...] = (yv * _sig(yv)).astype(BF16)

    row = pl.BlockSpec((tt, cc), lambda i: (i, 0))
    full = pl.BlockSpec(ln_g.shape, lambda i: (0, 0))
    return pl.pallas_call(body, name=name, grid=(t // tt,), in_specs=[row, full, full], out_specs=row,
                          out_shape=SDS((t, cc), BF16), compiler_params=_cparams(("arbitrary",)))(uc, ln_g, ln_b)


def _ln_silu_bwd(uc, ln_g, ln_b, l, ds, name):
    t, cc = uc.shape
    tt = _rt(t)

    def body(u_ref, g_ref, b_ref, ds_ref, du_ref, dg_ref, db_ref, dcb_ref):
        @pl.when(pl.program_id(0) == 0)
        def _():
            dg_ref[...] = jnp.zeros_like(dg_ref)
            db_ref[...] = jnp.zeros_like(db_ref)
            dcb_ref[...] = jnp.zeros_like(dcb_ref)

        g = g_ref[l:l + 1, :]
        r, xh, yv = _ln_parts(u_ref[...], g, b_ref[l:l + 1, :])
        sy = _sig(yv)
        dy = ds_ref[...] * (sy * (1.0 + yv * (1.0 - sy)))
        dg_ref[...] += jnp.sum(dy * xh, axis=0, keepdims=True)
        db_ref[...] += jnp.sum(dy, axis=0, keepdims=True)
        dxh = dy * g
        du = r * (dxh - jnp.mean(dxh, axis=-1, keepdims=True) - xh * jnp.mean(dxh * xh, axis=-1, keepdims=True))
        du_ref[...] = du
        dcb_ref[...] += jnp.sum(du, axis=0, keepdims=True)

    row = pl.BlockSpec((tt, cc), lambda i: (i, 0))
    full = pl.BlockSpec(ln_g.shape, lambda i: (0, 0))
    vec = pl.BlockSpec((1, cc), lambda i: (0, 0))
    return pl.pallas_call(body, name=name, grid=(t // tt,), in_specs=[row, full, full, row],
                          out_specs=[row, vec, vec, vec], out_shape=[SDS((t, cc), F32)] + [SDS((1, cc), F32)] * 3,
                          compiler_params=_cparams(("arbitrary",)))(uc, ln_g, ln_b, ds)


def _glu_bwd(du, proj, dproj, o_a, cc, name):
    t = du.shape[0]
    tt = _rt(t)

    def body(du_ref, a_ref, gate_ref, prev_ref, o_ref):
        sg = _sig(gate_ref[...].astype(F32))
        dv = du_ref[...]
        o_ref[:, 0:cc] = (dv * sg).astype(BF16)
        o_ref[:, cc:2 * cc] = (dv * a_ref[...] * sg * (1.0 - sg)).astype(BF16)

    return pl.pallas_call(
        body, name=name, grid=(t // tt,),
        in_specs=[pl.BlockSpec((tt, cc), lambda i: (i, 0)), pl.BlockSpec((tt, cc), lambda i: (i, o_a // cc)),
                  pl.BlockSpec((tt, cc), lambda i: (i, o_a // cc + 1)), ANY],
        out_specs=pl.BlockSpec((tt, 2 * cc), lambda i: (i, o_a // (2 * cc))),
        out_shape=SDS(dproj.shape, dproj.dtype), input_output_aliases={3: 0},
        compiler_params=_cparams(("arbitrary",)))(du, proj, proj, dproj)


def _rope(v, cs, s1, s2):
    return v * cs + pltpu.roll(v, HP - ROPE // 2, 1) * s1 + pltpu.roll(v, ROPE // 2, 1) * s2


def _rope_t(dv, cs, s1, s2):
    return dv * cs + pltpu.roll(dv * s1, ROPE // 2, 1) + pltpu.roll(dv * s2, HP - ROPE // 2, 1)


def _head_norm(v, g):
    r = lax.rsqrt(jnp.sum(v * v, axis=-1, keepdims=True) * (1.0 / QK) + EPS)
    return v * r * g, r


def _head_norm_bwd(v, r, g, dy):
    xh = v * r
    dg = jnp.sum(dy * xh, axis=0, keepdims=True)
    dxh = dy * g
    dx = r * (dxh - xh * (jnp.sum(dxh * xh, axis=-1, keepdims=True) * (1.0 / QK)))
    return dx, dg


def _mla_specs(tt, lay, cq_norm, ckv_norm, qn, kn, wuq, wukv):
    ql, kvl = cq_norm.shape[1], ckv_norm.shape[1]
    full = lambda a: pl.BlockSpec(a.shape, lambda i: (0,) * a.ndim)
    tab = pl.BlockSpec((tt, HP), lambda i: (i, 0))
    return [pl.BlockSpec((tt, ql), lambda i: (i, lay["cq"] // ql)),
            pl.BlockSpec((tt, kvl), lambda i: (i, lay["ckv"] // kvl)),
            pl.BlockSpec((tt, HP), lambda i: (i, lay["kr"] // HP)),
            full(cq_norm), full(ckv_norm), full(qn), full(kn), full(wuq), full(wukv), tab, tab, tab]


def _mla_pre_fwd(proj, lay, l, cq_norm, ckv_norm, qn, kn, wuq, wukv, tabs, name):
    t = proj.shape[0]
    tt = _rt(t)
    hw = N_HEADS * HP

    def body(cq_ref, ckv_ref, kr_ref, gq_ref, gkv_ref, qn_ref, kn_ref, wq_ref, wkv_ref, c_ref, s1_ref, s2_ref,
             q_ref, k_ref, v_ref):
        cq = cq_ref[...].astype(F32)
        cqn = (cq * _rms_r(cq) * gq_ref[l:l + 1, :]).astype(BF16)
        ckv = ckv_ref[...].astype(F32)
        ckvn = (ckv * _rms_r(ckv) * gkv_ref[l:l + 1, :]).astype(BF16)
        qraw = _dot_nt(cqn, wq_ref[...])
        kv = _dot(ckvn, wkv_ref[...])
        v_ref[...] = kv.astype(BF16)
        lane = lax.broadcasted_iota(jnp.int32, (tt, HP), 1)
        krs = pltpu.roll(jnp.where(lane < ROPE, kr_ref[...].astype(F32), 0.0), NOPE, 1)
        cs, s1, s2 = c_ref[...], s1_ref[...], s2_ref[...]
        gq, gk = qn_ref[l:l + 1, :], kn_ref[l:l + 1, :]
        for h in range(N_HEADS):
            sl = slice(h * HP, (h + 1) * HP)
            qh, _ = _head_norm(qraw[:, sl], gq)
            q_ref[:, sl] = (_rope(qh, cs, s1, s2) * QK ** -0.5).astype(BF16)
            kh, _ = _head_norm(jnp.where(lane < NOPE, kv[:, sl], krs), gk)
            k_ref[:, sl] = _rope(kh, cs, s1, s2).astype(BF16)

    row = pl.BlockSpec((tt, hw), lambda i: (i, 0))
    return pl.pallas_call(
        body, name=name, grid=(t // tt,),
        in_specs=_mla_specs(tt, lay, cq_norm, ckv_norm, qn, kn, wuq, wukv),
        out_specs=[row, row, row], out_shape=[SDS((t, hw), BF16)] * 3,
        compiler_params=_cparams(("arbitrary",), VMEM_BIG))(
            proj, proj, proj, cq_norm, ckv_norm, qn, kn, wuq, wukv, *tabs)


def _mla_pre_bwd(proj, lay, l, cq_norm, ckv_norm, qn, kn, wuq, wukv, tabs, dq, dk, dv, dproj, name):
    t = proj.shape[0]
    tt = _rt(t)
    hw = N_HEADS * HP
    ql, kvl = cq_norm.shape[1], ckv_norm.shape[1]
    wm = lay["wm"]

    def body(cq_ref, ckv_ref, kr_ref, gq_ref, gkv_ref, qn_ref, kn_ref, wq_ref, wkv_ref, c_ref, s1_ref, s2_ref,
             dq_ref, dk_ref, dv_ref, prev_ref,
             o_ref, dqr_ref, dkv_ref, cqn_ref, ckvn_ref, dgq_ref, dgkv_ref, dqn_ref, dkn_ref):
        @pl.when(pl.program_id(0) == 0)
        def _():
            for ref in (dgq_ref, dgkv_ref, dqn_ref, dkn_ref):
                ref[...] = jnp.zeros_like(ref)

        cq = cq_ref[...].astype(F32)
        cqn = (cq * _rms_r(cq) * gq_ref[l:l + 1, :]).astype(BF16)
        ckv = ckv_ref[...].astype(F32)
        ckvn = (ckv * _rms_r(ckv) * gkv_ref[l:l + 1, :]).astype(BF16)
        cqn_ref[...] = cqn
        ckvn_ref[...] = ckvn
        qraw = _dot_nt(cqn, wq_ref[...])
        kv = _dot(ckvn, wkv_ref[...])
        lane = lax.broadcasted_iota(jnp.int32, (tt, HP), 1)
        krs = pltpu.roll(jnp.where(lane < ROPE, kr_ref[...].astype(F32), 0.0), NOPE, 1)
        cs, s1, s2 = c_ref[...], s1_ref[...], s2_ref[...]
        gq, gk = qn_ref[l:l + 1, :], kn_ref[l:l + 1, :]
        dkr = jnp.zeros((tt, HP), F32)
        dgq = jnp.zeros((1, HP), F32)
        dgk = jnp.zeros((1, HP), F32)
        for h in range(N_HEADS):
            sl = slice(h * HP, (h + 1) * HP)
            qh = qraw[:, sl]
            _, rq = _head_norm(qh, gq)
            dqh, dg = _head_norm_bwd(qh, rq, gq, _rope_t(dq_ref[:, sl] * QK ** -0.5, cs, s1, s2))
            dgq = dgq + dg
            dqr_ref[:, sl] = dqh.astype(BF16)
            kp = jnp.where(lane < NOPE, kv[:, sl], krs)
            _, rk = _head_norm(kp, gk)
            dkp, dg = _head_norm_bwd(kp, rk, gk, _rope_t(dk_ref[:, sl], cs, s1, s2))
            dgk = dgk + dg
            dkv_ref[:, sl] = (jnp.where(lane < NOPE, dkp, 0.0) + dv_ref[:, sl]).astype(BF16)
            dkr = dkr + dkp
        dqn_ref[...] += dgq
        dkn_ref[...] += dgk
        dkr = jnp.where(lane < ROPE, pltpu.roll(dkr, HP - NOPE, 1), 0.0)
        dcq, dg = _rms_bwd(cq, gq_ref[l:l + 1, :], _dot(dqr_ref[...], wq_ref[...]))
        dgq_ref[...] += dg
        dckv, dg = _rms_bwd(ckv, gkv_ref[l:l + 1, :], _dot_nt(dkv_ref[...], wkv_ref[...]))
        dgkv_ref[...] += dg
        o_ref[:, 0:ql] = dcq.astype(BF16)
        o_ref[:, ql:ql + kvl] = dckv.astype(BF16)
        o_ref[:, ql + kvl:ql + kvl + HP] = dkr.astype(BF16)
        o_ref[:, ql + kvl + HP:wm] = jnp.zeros((tt, wm - ql - kvl - HP), BF16)

    row = pl.BlockSpec((tt, hw), lambda i: (i, 0))
    vec = lambda n: pl.BlockSpec((1, n), lambda i: (0, 0))
    return pl.pallas_call(
        body, name=name, grid=(t // tt,),
        in_specs=_mla_specs(tt, lay, cq_norm, ckv_norm, qn, kn, wuq, wukv) + [row, row, row, ANY],
        out_specs=[pl.BlockSpec((tt, wm), lambda i: (i, lay["cq"] // wm)), row, row,
                   pl.BlockSpec((tt, ql), lambda i: (i, 0)), pl.BlockSpec((tt, kvl), lambda i: (i, 0)),
                   vec(ql), vec(kvl), vec(HP), vec(HP)],
        out_shape=[SDS(dproj.shape, dproj.dtype), SDS((t, hw), BF16), SDS((t, hw), BF16), SDS((t, ql), BF16),
                   SDS((t, kvl), BF16), SDS((1, ql), F32), SDS((1, kvl), F32), SDS((1, HP), F32), SDS((1, HP), F32)],
        input_output_aliases={15: 0},
        compiler_params=_cparams(("arbitrary",), VMEM_BIG))(
            proj, proj, proj, cq_norm, ckv_norm, qn, kn, wuq, wukv, *tabs, dq, dk, dv, dproj)


def _comb_fwd(proj, gate_bias, l, yc, ym, name):
    t, d = yc.shape
    tt = _rt(t)

    def body(p_ref, b_ref, yc_ref, ym_ref, y_ref):
        b = b_ref[l]
        g0 = _sig(p_ref[:, 0:d] + b[0:1, :])
        g1 = _sig(p_ref[:, d:2 * d] + b[1:2, :])
        y_ref[...] = (g0 * yc_ref[...] + g1 * ym_ref[...]).astype(BF16)

    row = pl.BlockSpec((tt, d), lambda i: (i, 0))
    return pl.pallas_call(
        body, name=name, grid=(t // tt,),
        in_specs=[pl.BlockSpec((tt, 2 * d), lambda i: (i, 0)), pl.BlockSpec(gate_bias.shape, lambda i: (0, 0, 0)), row, row],
        out_specs=row, out_shape=SDS((t, d), BF16), compiler_params=_cparams(("arbitrary",)))(proj, gate_bias, yc, ym)


def _comb_bwd(proj, gate_bias, l, yc, ym, dy, dp_cols, name):
    t, d = yc.shape
    tt = _rt(t)

    def body(p_ref, b_ref, yc_ref, ym_ref, dy_ref, dyc_ref, dym_ref, dp_ref, db_ref):
        @pl.when(pl.program_id(0) == 0)
        def _():
            db_ref[...] = jnp.zeros_like(db_ref)

        b = b_ref[l]
        dyv = dy_ref[...].astype(F32)
        g0 = _sig(p_ref[:, 0:d] + b[0:1, :])
        g1 = _sig(p_ref[:, d:2 * d] + b[1:2, :])
        dyc_ref[...] = (dyv * g0).astype(BF16)
        dym_ref[...] = (dyv * g1).astype(BF16)
        dg0 = dyv * yc_ref[...] * g0 * (1.0 - g0)
        dg1 = dyv * ym_ref[...] * g1 * (1.0 - g1)
        dp_ref[:, 0:d] = dg0.astype(BF16)
        dp_ref[:, d:2 * d] = dg1.astype(BF16)
        db_ref[0:1, :] += jnp.sum(dg0, axis=0, keepdims=True)
        db_ref[1:2, :] += jnp.sum(dg1, axis=0, keepdims=True)

    row = pl.BlockSpec((tt, d), lambda i: (i, 0))
    wide = pl.BlockSpec((tt, 2 * d), lambda i: (i, 0))
    return pl.pallas_call(
        body, name=name, grid=(t // tt,),
        in_specs=[wide, pl.BlockSpec(gate_bias.shape, lambda i: (0, 0, 0)), row, row, row],
        out_specs=[row, row, wide, pl.BlockSpec((2, d), lambda i: (0, 0))],
        out_shape=[SDS((t, d), BF16), SDS((t, d), BF16), SDS((t, dp_cols), BF16), SDS((2, d), F32)],
        compiler_params=_cparams(("arbitrary",)))(proj, gate_bias, yc, ym, dy)


def _loss_grad(y, target, name):
    t, d = y.shape
    tt = _rt(t)
    nt = t // tt

    def body(y_ref, t_ref, dy_ref, loss_ref, acc_sc):
        i = pl.program_id(0)

        @pl.when(i == 0)
        def _():
            acc_sc[...] = jnp.zeros_like(acc_sc)

        diff = y_ref[...] - t_ref[...]
        dy_ref[...] = diff * (1.0 / d)
        acc_sc[...] += jnp.sum(diff * diff, axis=0, keepdims=True)

        @pl.when(i == nt - 1)
        def _():
            tot = jnp.sum(acc_sc[...], axis=1, keepdims=True) * (0.5 / d)
            loss_ref[...] = jnp.broadcast_to(tot, (1, HP))

    row = pl.BlockSpec((tt, d), lambda i: (i, 0))
    return pl.pallas_call(body, name=name, grid=(nt,), in_specs=[row, row],
                          out_specs=[row, pl.BlockSpec((1, HP), lambda i: (0, 0))],
                          out_shape=[SDS((t, d), F32), SDS((1, HP), F32)],
                          scratch_shapes=[pltpu.VMEM((1, d), F32)],
                          compiler_params=_cparams(("arbitrary",)))(y, target)


def _chunk_mask(tq):
    rows = lax.broadcasted_iota(jnp.int32, (tq, tq), 0) // CHUNK
    cols = lax.broadcasted_iota(jnp.int32, (tq, tq), 1) // CHUNK
    return cols <= rows


NEG = -1e30


def _flash_fwd(q, k, v, name, comm=None):
    t = q.shape[0]
    tq = _rt(t)
    nq = t // tq
    rep = tq // HP

    def body(q_ref, k_ref, v_ref, o_ref, lse_ref):
        qi = pl.program_id(1)

        def one(hh, ki, carry, masked):
            m_prev, l_prev, acc = carry
            lanes = slice(hh * HP, (hh + 1) * HP)
            r0 = pl.multiple_of(ki * tq, tq)
            s = _dot_nt(q_ref[:, lanes], k_ref[pl.ds(r0, tq), lanes])
            if masked:
                s = jnp.where(_chunk_mask(tq), s, NEG)
            m_new = jnp.maximum(m_prev, jnp.max(s, axis=-1, keepdims=True))
            a = jnp.exp(m_prev - m_new)
            p = jnp.exp(s - jnp.tile(m_new, (1, rep)))
            l_new = a * l_prev + jnp.sum(p, axis=-1, keepdims=True)
            acc = a * acc + _dot(p.astype(BF16), v_ref[pl.ds(r0, tq), lanes])
            return m_new, l_new, acc

        def step(ki, carries, masked):
            return tuple(one(hh, ki, carries[hh], masked) for hh in range(FLASH_HEADS))

        init = (jnp.full((tq, HP), NEG, F32), jnp.zeros((tq, HP), F32), jnp.zeros((tq, HP), F32))
        carries = lax.fori_loop(0, qi, lambda ki, cr: step(ki, cr, False), (init,) * FLASH_HEADS)
        for hh, (m_fin, l_fin, acc) in enumerate(step(qi, carries, True)):
            o_ref[:, hh * HP:(hh + 1) * HP] = (acc / l_fin).astype(BF16)
            lse_ref[hh] = m_fin + jnp.log(l_fin)

    wide = FLASH_HEADS * HP
    ng = N_HEADS // FLASH_HEADS
    qspec = pl.BlockSpec((tq, wide), lambda h, qi: (qi, h))
    head = pl.BlockSpec((t, wide), lambda h, qi: (0, h))
    first = lambda: jnp.logical_and(pl.program_id(0) == 0, pl.program_id(1) == 0)
    last = lambda: jnp.logical_and(pl.program_id(0) == ng - 1, pl.program_id(1) == nq - 1)
    mid_at = lambda: pl.program_id(0) * nq + pl.program_id(1) == (7 * ng * nq) // 8
    outs, couts = _call(
        body, comm, first, last, name=name, grid=(ng, nq), mid_at=mid_at, in_specs=[qspec, head, head], args=[q, k, v],
        out_specs=[qspec, pl.BlockSpec((FLASH_HEADS, tq, HP), lambda h, qi: (h, qi, 0))],
        out_shape=[SDS(q.shape, BF16), SDS((N_HEADS, t, HP), F32)], scratch=[], sem=("arbitrary",) * 2)
    return (*outs, couts)


def _flash_bwd(q, k, v, do, o, lse, name, comm=None):
    t = q.shape[0]
    tq = _rt(t)
    nq = t // tq
    rep = tq // HP

    def body(q_ref, k_ref, v_ref, do_ref, o_ref, lse_ref, dq_ref, dk_ref, dv_ref, delta_sc):
        def prep(qi, carry):
            r0 = pl.multiple_of(qi * tq, tq)
            rows = pl.ds(r0, tq)
            dlt = jnp.sum(do_ref[rows, :].astype(F32) * o_ref[rows, :].astype(F32), axis=-1, keepdims=True)
            delta_sc[rows, :] = jnp.broadcast_to(dlt, (tq, HP))
            dq_ref[rows, :] = jnp.zeros((tq, HP), F32)
            return carry

        lax.fori_loop(0, nq, prep, 0)

        def keys(ki, carry0):
            krows = pl.ds(pl.multiple_of(ki * tq, tq), tq)
            kt, vt = k_ref[krows, :], v_ref[krows, :]

            def step(qi, carry, masked):
                dk_acc, dv_acc = carry
                rows = pl.ds(pl.multiple_of(qi * tq, tq), tq)
                qt, dot_ = q_ref[rows, :], do_ref[rows, :]
                s = _dot_nt(qt, kt)
                if masked:
                    s = jnp.where(_chunk_mask(tq), s, NEG)
                p = jnp.exp(s - jnp.tile(lse_ref[rows, :], (1, rep)))
                ds = (p * (_dot_nt(dot_, vt) - jnp.tile(delta_sc[rows, :], (1, rep)))).astype(BF16)
                dv_acc = dv_acc + _dot_tn(p.astype(BF16), dot_)
                dk_acc = dk_acc + _dot_tn(ds, qt)
                dq_ref[rows, :] += _dot(ds, kt)
                return dk_acc, dv_acc

            zero = jnp.zeros((tq, HP), F32)
            carry = step(ki, (zero, zero), True)
            dk_acc, dv_acc = lax.fori_loop(ki + 1, nq, lambda qi, cr: step(qi, cr, False), carry)
            dk_ref[krows, :] = dk_acc
            dv_ref[krows, :] = dv_acc
            return carry0

        lax.fori_loop(0, nq, keys, 0)

    head = pl.BlockSpec((t, HP), lambda h: (0, h))
    outs, couts = _call(
        body, comm, lambda: pl.program_id(0) == 0, lambda: pl.program_id(0) == N_HEADS - 1, name=name, grid=(N_HEADS,),
        in_specs=[head] * 5 + [pl.BlockSpec((None, t, HP), lambda h: (h, 0, 0))], args=[q, k, v, do, o, lse],
        out_specs=[head] * 3, out_shape=[SDS(q.shape, F32)] * 3, scratch=[pltpu.VMEM((t, HP), F32)], sem=("arbitrary",))
    return (*outs, couts)


def _add_cast(gs, rs, place, name):
    n = len(gs)
    _, r, c = gs[0].shape
    tr, tc = _tile2(r, c // 2, 3 * n, 16)
    nct = c // 2 // tc

    def body(place_ref, *refs):
        for a in range(n):
            refs[2 * n + a][...] = (refs[a][...] + refs[n + a][...]).astype(BF16)

    gspec = pl.BlockSpec((None, tr, tc), lambda j, i, k, pr: (j, i, pr[0] * nct + k))
    rspec = pl.BlockSpec((None, tr, tc), lambda j, i, k, pr: (j, i, k))
    grid_spec = pltpu.PrefetchScalarGridSpec(num_scalar_prefetch=1, grid=(NCHIP, r // tr, nct),
                                             in_specs=[gspec] * n + [rspec] * n, out_specs=[rspec] * n)
    return pl.pallas_call(body, name=name, grid_spec=grid_spec, out_shape=[SDS((NCHIP, r, c // 2), BF16)] * n,
                          compiler_params=_cparams(("arbitrary",) * 3, VMEM_BIG))(place, *gs, *rs)


def _sum_chips(ss, qs, place, l, nl, prevs, name):
    n = len(ss)
    _, r, h = ss[0].shape
    tr, tc = _tile2(r, h, 3 * n, 16)
    nct = h // tc

    def body(place_ref, *refs):
        for a in range(n):
            acc = refs[n + a][...].astype(F32)
            for kk in range(NCHIP - 1):
                acc = acc + refs[a][kk].astype(F32)
            refs[-n + a][...] = acc

    in_specs = ([pl.BlockSpec((NCHIP - 1, tr, tc), lambda i, k, pr: (0, i, k))] * n
                + [pl.BlockSpec((None, tr, tc), lambda i, k, pr: (pr[1], i, k))] * n)
    args = [*ss, *qs]
    aliases = {}
    if prevs is not None:
        aliases = {1 + len(args) + a: a for a in range(n)}
        in_specs += [ANY] * n
        args += list(prevs)
    grid_spec = pltpu.PrefetchScalarGridSpec(
        num_scalar_prefetch=1, grid=(r // tr, nct), in_specs=in_specs,
        out_specs=[pl.BlockSpec((None, tr, tc), lambda i, k, pr: (l, i, pr[0] * nct + k))] * n)
    return pl.pallas_call(body, name=name, grid_spec=grid_spec, out_shape=[SDS((nl, r, 2 * h), F32)] * n,
                          input_output_aliases=aliases,
                          compiler_params=_cparams(("arbitrary",) * 2, VMEM_BIG))(place, *args)


def _cast_place(ws, place, name):
    n = len(ws)
    nl, r, c = ws[0].shape
    tr, tc = _tile2(r, c, 2 * n * nl, 16)

    def body(place_ref, *refs):
        for a in range(n * nl):
            refs[n * nl + a][...] = refs[a][...].astype(BF16)

    in_specs = [pl.BlockSpec((None, tr, tc), functools.partial(lambda l, i, k, pr: (l, i, k), l))
                for _ in range(n) for l in range(nl)]
    grid_spec = pltpu.PrefetchScalarGridSpec(
        num_scalar_prefetch=1, grid=(r // tr, c // tc), in_specs=in_specs,
        out_specs=[pl.BlockSpec((None, tr, tc), lambda i, k, pr: (pr[1], i, k))] * (n * nl))
    outs = pl.pallas_call(body, name=name, grid_spec=grid_spec, out_shape=[SDS((NCHIP, r, c), BF16)] * (n * nl),
                          compiler_params=_cparams(("arbitrary",) * 2, VMEM_BIG))(
                              place, *[w for w in ws for _ in range(nl)])
    return [outs[a * nl:(a + 1) * nl] for a in range(n)]


def _adamw(ws, gs, ms, vs, name):
    n = len(ws)
    r, c = ws[0].shape
    tr, tc = _tile2(r, c, 7 * n, 8)
    c1, c2 = 1.0 / (1.0 - B1 ** STEP), 1.0 / (1.0 - B2 ** STEP)

    def body(*refs):
        for a in range(n):
            w, g, m, v = (refs[kk * n + a][...] for kk in range(4))
            m2 = B1 * m + (1.0 - B1) * g
            v2 = B2 * v + (1.0 - B2) * (g * g)
            refs[4 * n + a][...] = -LR * ((m2 * c1) / (jnp.sqrt(v2 * c2) + EPS_ADAM) + WD * w)
            refs[5 * n + a][...] = m2
            refs[6 * n + a][...] = v2

    blk = pl.BlockSpec((tr, tc), lambda i, k: (i, k))
    outs = pl.pallas_call(body, name=name, grid=(r // tr, c // tc), in_specs=[blk] * (4 * n),
                          out_specs=[blk] * (3 * n), out_shape=[SDS((r, c), F32)] * (3 * n),
                          compiler_params=_cparams(("arbitrary",) * 2, VMEM_BIG))(*ws, *gs, *ms, *vs)
    return outs[:n], outs[n:2 * n], outs[2 * n:]


def _place():
    x, y, c = lax.axis_index("x"), lax.axis_index("y"), lax.axis_index("c")
    return x, y, c, [(1 - x, y), (x, 1 - y), (1 - x, 1 - y)]


def _rcopy(src, dst, ssem, rsem, k, dev):
    return pltpu.make_async_remote_copy(src_ref=src, dst_ref=dst, send_sem=ssem.at[k], recv_sem=rsem.at[k],
                                        device_id=dev, device_id_type=MESH)


def _half(ref, lead, cc):
    h = ref.shape[-1] // 2
    return ref.at[(*lead, slice(None), pl.ds(cc * h, h))]


def _per_core(fn):
    c = lax.axis_index("c")
    for cc in (0, 1):
        pl.when(c == cc)(functools.partial(fn, cc))


def _gather_comm(bufs):
    n = len(bufs)

    def plan(couts, ssem, rsem, cc):
        x, y, _, peers = _place()
        me, sib = 2 * x + y, (x, y, 1 - cc)
        send, recv, fwd, recv2 = [], [], [], []
        for a in range(n):
            for kk, (px, py) in enumerate(peers):
                mine = _half(couts[a], (me,), cc)
                got = _half(couts[a], (2 * px + py,), cc)
                other = _half(couts[a], (2 * px + py,), 1 - cc)
                send.append(_rcopy(mine, mine, ssem, rsem, a * 6 + kk, (px, py, cc)))
                recv.append(_rcopy(got, got, ssem, rsem, a * 6 + kk, (px, py, cc)))
                fwd.append(_rcopy(got, got, ssem, rsem, a * 6 + 3 + kk, sib))
                recv2.append(_rcopy(other, other, ssem, rsem, a * 6 + 3 + kk, sib))
        return send, recv, fwd, recv2

    def start(cins, couts, ssem, rsem):
        def go(cc):
            for d in plan(couts, ssem, rsem, cc)[0]:
                d.start()

        _per_core(go)

    def mid(cins, couts, ssem, rsem):
        def go(cc):
            _, recv, fwd, _ = plan(couts, ssem, rsem, cc)
            for dr, df in zip(recv, fwd):
                dr.wait_recv()
                df.start()

        _per_core(go)

    def finish(cins, couts, ssem, rsem):
        def go(cc):
            send, _, fwd, recv2 = plan(couts, ssem, rsem, cc)
            for d in recv2:
                d.wait_recv()
            for d in send + fwd:
                d.wait_send()

        _per_core(go)

    return _Comm(bufs, [SDS(b.shape, b.dtype) for b in bufs], {a: a for a in range(n)}, 6 * n, start, finish, mid)


def _pair_comm(gs):
    n = len(gs)
    halves = [g.shape[-1] // 2 for g in gs]

    def plan(cins, couts, ssem, rsem, cc):
        x, y, _, _ = _place()
        return [_rcopy(cins[a].at[:, :, pl.ds((1 - cc) * halves[a], halves[a])], couts[a], ssem, rsem, a, (x, y, 1 - cc))
                for a in range(n)]

    def start(cins, couts, ssem, rsem):
        def go(cc):
            for d in plan(cins, couts, ssem, rsem, cc):
                d.start()

        _per_core(go)

    def finish(cins, couts, ssem, rsem):
        def go(cc):
            ds = plan(cins, couts, ssem, rsem, cc)
            for d in ds:
                d.wait_recv()
            for d in ds:
                d.wait_send()

        _per_core(go)

    return _Comm(gs, [SDS(g.shape[:-1] + (g.shape[-1] // 2,), g.dtype) for g in gs], {}, n, start, finish)


def _chips_comm(qs):
    n = len(qs)

    def plan(cins, couts, ssem, rsem):
        x, y, c, peers = _place()
        return [_rcopy(cins[a].at[2 * px + py], couts[a].at[kk], ssem, rsem, a * 3 + kk, (px, py, c))
                for a in range(n) for kk, (px, py) in enumerate(peers)]

    def start(cins, couts, ssem, rsem):
        for d in plan(cins, couts, ssem, rsem):
            d.start()

    def finish(cins, couts, ssem, rsem):
        ds = plan(cins, couts, ssem, rsem)
        for d in ds:
            d.wait_recv()
        for d in ds:
            d.wait_send()

    return _Comm(qs, [SDS((NCHIP - 1,) + q.shape[1:], q.dtype) for q in qs], {}, 3 * n, start, finish)


def _share_comm(fs, l):
    n = len(fs)

    def plan(couts, ssem, rsem, cc, which):
        x, y, _, _ = _place()
        out = []
        for a in range(n):
            piece = _half(couts[a], (l,), which)
            out.append(_rcopy(piece, piece, ssem, rsem, a, (x, y, 1 - cc)))
        return out

    def start(cins, couts, ssem, rsem):
        def go(cc):
            for d in plan(couts, ssem, rsem, cc, cc):
                d.start()

        _per_core(go)

    def finish(cins, couts, ssem, rsem):
        def go(cc):
            for d in plan(couts, ssem, rsem, cc, 1 - cc):
                d.wait_recv()
            for d in plan(couts, ssem, rsem, cc, cc):
                d.wait_send()

        _per_core(go)

    return _Comm(fs, [SDS(f.shape, f.dtype) for f in fs], {a: a for a in range(n)}, n, start, finish)


def _spread_comm(buf):
    def plan(cins, couts, ssem, rsem):
        x, y, c, _ = _place()
        me = 4 * x + 2 * y + c
        send, recv = [], []
        for rel in range(1, N_DEV):
            px, py, pc = (1 - x if rel & 4 else x, 1 - y if rel & 2 else y, 1 - c if rel & 1 else c)
            slot = couts[0].at[4 * px + 2 * py + pc]
            send.append(_rcopy(cins[0], couts[0].at[me], ssem, rsem, rel - 1, (px, py, pc)))
            recv.append(_rcopy(slot, slot, ssem, rsem, rel - 1, (px, py, pc)))
        own = pltpu.make_async_copy(cins[0], couts[0].at[me], ssem.at[N_DEV - 1])
        return send, recv, own

    def start(cins, couts, ssem, rsem):
        send, _, own = plan(cins, couts, ssem, rsem)
        for dsc in send + [own]:
            dsc.start()

    def finish(cins, couts, ssem, rsem):
        send, recv, own = plan(cins, couts, ssem, rsem)
        for dsc in recv:
            dsc.wait_recv()
        for dsc in send:
            dsc.wait_send()
        own.wait()

    return _Comm([buf], [SDS((N_DEV,) + buf.shape, buf.dtype)], {}, N_DEV, start, finish)


def _sum_devices(blocks, name):
    _, r, w = blocks.shape

    def body(in_ref, out_ref):
        acc = in_ref[0]
        for dv in range(1, N_DEV):
            acc = acc + in_ref[dv]
        out_ref[...] = acc

    vm = pl.BlockSpec(memory_space=pltpu.VMEM)
    return pl.pallas_call(body, name=name, in_specs=[vm], out_specs=vm, out_shape=SDS((r, w), F32),
                          compiler_params=_cparams(None, VMEM_BIG))(blocks)


def _cols_full(g):
    _, k, ns = g.shape
    return g.transpose(1, 0, 2).reshape(k, NCHIP * ns)


def _cols_shards(w):
    k, n = w.shape
    return w.reshape(k, NCHIP, n // NCHIP).transpose(1, 0, 2)


def _rows_pad(rows, width):
    out = jnp.concatenate([jnp.pad(a, ((0, 0), (0, width - a.shape[1]))) for a in rows], axis=0)
    return jnp.pad(out, ((0, -out.shape[0] % 8), (0, 0)))


def kernel(x, positions, ffn1_norm, ffn1_w_gate, ffn1_w_up, ffn1_w_down, mix_norm, w_in, gate_bias, conv_w, conv_b, conv_ln_g, conv_ln_b, w_conv_out, cq_norm, ckv_norm, w_uq, w_ukv, q_norm, k_norm, w_mla_out, w_out, ffn2_norm, ffn2_w_gate, ffn2_w_up, ffn2_w_down, loss_target, m_ffn1_norm, m_ffn1_w_gate, m_ffn1_w_up, m_ffn1_w_down, m_mix_norm, m_w_in, m_gate_bias, m_conv_w, m_conv_b, m_conv_ln_g, m_conv_ln_b, m_w_conv_out, m_cq_norm, m_ckv_norm, m_w_uq, m_w_ukv, m_q_norm, m_k_norm, m_w_mla_out, m_w_out, m_ffn2_norm, m_ffn2_w_gate, m_ffn2_w_up, m_ffn2_w_down, v_ffn1_norm, v_ffn1_w_gate, v_ffn1_w_up, v_ffn1_w_down, v_mix_norm, v_w_in, v_gate_bias, v_conv_w, v_conv_b, v_conv_ln_g, v_conv_ln_b, v_w_conv_out, v_cq_norm, v_ckv_norm, v_w_uq, v_w_ukv, v_q_norm, v_k_norm, v_w_mla_out, v_w_out, v_ffn2_norm, v_ffn2_w_gate, v_ffn2_w_up, v_ffn2_w_down):
    names = ["ffn1_norm", "ffn1_w_gate", "ffn1_w_up", "ffn1_w_down", "mix_norm", "w_in", "gate_bias", "conv_w",
             "conv_b", "conv_ln_g", "conv_ln_b", "w_conv_out", "cq_norm", "ckv_norm", "w_uq", "w_ukv", "q_norm",
             "k_norm", "w_mla_out", "w_out", "ffn2_norm", "ffn2_w_gate", "ffn2_w_up", "ffn2_w_down"]
    env = dict(locals())
    turned = ("ffn1_w_gate", "ffn1_w_up", "ffn2_w_gate", "ffn2_w_up", "w_in", "w_uq")
    view = lambda nm, a: jnp.swapaxes(a, 1, 2) if nm in turned else a
    wts = {nm: view(nm, env[nm]) for nm in names}
    mom = {nm: view(nm, env["m_" + nm]) for nm in names}
    var = {nm: view(nm, env["v_" + nm]) for nm in names}

    t, d = x.shape[1], x.shape[2]
    nl = ffn1_norm.shape[0]
    cc = conv_b.shape[1]
    ql, kvl = cq_norm.shape[1], ckv_norm.shape[1]
    vw = N_HEADS * V_DIM
    hw = N_HEADS * HP
    lay = {"a": 2 * d, "cq": 2 * d + 2 * cc, "ckv": 2 * d + 2 * cc + ql, "kr": 2 * d + 2 * cc + ql + kvl,
           "wm": ql + kvl + 2 * HP}
    dp = lay["cq"] + lay["wm"]
    nat_g = 2 * cc + ql + kvl + ROPE
    assert lay["cq"] % lay["wm"] == 0 and lay["cq"] % ql == 0 and lay["ckv"] % kvl == 0 and lay["a"] % (2 * cc) == 0
    assert cc % HP == 0 and d % HP == 0 and t % (2 * CHUNK) == 0 and w_in.shape[2] * NCHIP == nat_g + 2 * d
    assert nl == 2

    x0, target = x[0], loss_target[0]
    chip = 2 * lax.axis_index("x") + lax.axis_index("y")
    place = jnp.stack([lax.axis_index("c"), chip]).astype(jnp.int32)

    inv_freq = ROPE_THETA ** (-jnp.arange(0, ROPE, 2, dtype=F32) / ROPE)
    ang = positions[0].astype(F32)[:, None] * inv_freq
    cos, sin, z = jnp.cos(ang), jnp.sin(ang), jnp.zeros((t, ROPE // 2), F32)
    tabs = (jnp.concatenate([jnp.ones((t, NOPE), F32), cos, cos, jnp.zeros((t, HP - QK), F32)], axis=1),
            jnp.concatenate([jnp.zeros((t, NOPE), F32), -sin, z, jnp.zeros((t, HP - QK), F32)], axis=1),
            jnp.concatenate([jnp.zeros((t, NOPE), F32), z, sin, jnp.zeros((t, HP - QK), F32)], axis=1))

    big = ["ffn1_w_gate", "ffn1_w_up", "ffn1_w_down", "ffn2_w_gate", "ffn2_w_up", "ffn2_w_down",
           "w_in", "w_conv_out", "w_uq", "w_ukv", "w_mla_out", "w_out"]
    ffn1_w, ffn2_w, mix_w = big[0:3], big[3:6], big[6:]
    bufs, like = {}, {}
    for nm in big:
        like.setdefault(wts[nm].shape, []).append(nm)
    for gi, grp in enumerate(like.values()):
        for nm, per_layer in zip(grp, _cast_place([wts[nm] for nm in grp], place, f"cast_place_{gi}")):
            for l in range(nl):
                bufs[nm, l] = per_layer[l]

    def gather(keys):
        return _gather_comm([bufs[key] for key in keys])

    def landed(keys, outs):
        for key, o in zip(keys, outs):
            bufs[key] = o

    chunk = lambda nms, l: [(nm, l) for nm in nms]
    first_keys = chunk(ffn1_w, 0)
    landed(first_keys, _comm_call(gather(first_keys), "gather_first"))
    ride = {("ffn1", 0): chunk(mix_w, 0), ("flash", 0): chunk(ffn2_w, 0), ("ffn2", 0): chunk(ffn1_w, 1),
            ("ffn1", 1): chunk(mix_w, 1), ("flash", 1): chunk(ffn2_w, 1)}

    def riding(kind, l):
        keys = ride.get((kind, l))
        return keys or [], (None if keys is None else gather(keys))

    def mixer_weights(l):
        w_in_nat = bufs["w_in", l].reshape(nat_g + 2 * d, d)
        w_in_k = jnp.concatenate([w_in_nat[nat_g:], w_in_nat[:nat_g], jnp.zeros((dp - nat_g - 2 * d, d), BF16)], axis=0)
        w_uq_k = jnp.pad(bufs["w_uq", l].reshape(N_HEADS, QK, ql), ((0, 0), (0, HP - QK), (0, 0))).reshape(hw, ql)
        w_mo_k = jnp.pad(_cols_full(bufs["w_mla_out", l]).reshape(N_HEADS, V_DIM, d),
                         ((0, 0), (HP - V_DIM, 0), (0, 0))).reshape(hw, d)
        return dict(w_in=w_in_k, w_co=_cols_full(bufs["w_conv_out", l]), w_uq=w_uq_k,
                    w_ukv=_cols_full(bufs["w_ukv", l]), w_mo=w_mo_k, w_out=bufs["w_out", l].reshape(d, d))

    qn_k = jnp.pad(q_norm, ((0, 0), (0, HP - QK)))
    kn_k = jnp.pad(k_norm, ((0, 0), (0, HP - QK)))
    small = _rows_pad([gate_bias.reshape(nl * 2, d // NCHIP), conv_w.reshape(nl * KW, cc // NCHIP)], d)

    saved, mixw = [], []
    xc = x0
    for l in range(nl):
        keys, comm = riding("ffn1", l)
        if l == 0:
            comm = _join(comm, _spread_comm(small))
        x1, gt1, up1, got = _ffn_fwd(xc, ffn1_norm, l, *[bufs[nm, l] for nm in ffn1_w], f"ffn1_fwd_{l}", comm)
        landed(keys, got)
        if l == 0:
            everyone = got[-1][0::2]
            gb_k = everyone[:, :nl * 2, :d // NCHIP].reshape(NCHIP, nl, 2, d // NCHIP).transpose(1, 2, 0, 3).reshape(nl, 2, d)
            cw = everyone[:, nl * 2:nl * 2 + nl * KW, :cc // NCHIP].reshape(NCHIP, nl, KW, cc // NCHIP)
            cw_k = jnp.pad(cw.transpose(1, 2, 0, 3).reshape(nl, KW, cc), ((0, 0), (0, CPAD - KW), (0, 0)))
        mw = mixer_weights(l)
        mixw.append(mw)
        hm = _rms_fwd(x1, mix_norm, l, f"mix_norm_{l}")
        proj = _mm(hm, mw["w_in"], "nt", name=f"proj_{l}", out_dtype=BF16)
        uc = _dwconv_fwd(proj, cw_k, conv_b, l, lay["a"], cc, f"dwconv_{l}")
        sc = _ln_silu(uc, conv_ln_g, conv_ln_b, l, f"conv_ln_{l}")
        yc = _mm(sc, mw["w_co"], "nn", name=f"conv_out_{l}", out_dtype=BF16)
        q, k, kv = _mla_pre_fwd(proj, lay, l, cq_norm, ckv_norm, qn_k, kn_k, mw["w_uq"], mw["w_ukv"], tabs, f"mla_pre_{l}")
        keys, comm = riding("flash", l)
        o, lse, got = _flash_fwd(q, k, kv, f"flash_{l}", comm)
        landed(keys, got)
        ym = _mm(o, mw["w_mo"], "nn", name=f"mla_out_{l}", out_dtype=BF16)
        yv = _comb_fwd(proj, gb_k, l, yc, ym, f"combine_{l}")
        x2 = _mm(yv, mw["w_out"], "nn", name=f"mix_out_{l}", res=x1)
        keys, comm = riding("ffn2", l)
        x3, gt2, up2, got = _ffn_fwd(x2, ffn2_norm, l, *[bufs[nm, l] for nm in ffn2_w], f"ffn2_fwd_{l}", comm,
                                     0.9)
        landed(keys, got)
        saved.append(dict(x0=xc, x1=x1, gt1=gt1, up1=up1, hm=hm, proj=proj, uc=uc, sc=sc, yc=yc, q=q, k=k, kv=kv,
                          o=o, lse=lse, ym=ym, yv=yv, x2=x2, gt2=gt2, up2=up2))
        xc = x3
    dx, loss_row = _loss_grad(xc, target, "loss")

    gw = {nm: [None] * nl for nm in names}
    red = {}

    def shard_groups(nms, l):
        same = {}
        for nm in nms:
            same.setdefault(gw[nm][l].shape, []).append(nm)
        return list(same.values())

    def add_parts(nms, l, sib_part, tag):
        qb = {}
        for gi, grp in enumerate(shard_groups(nms, l)):
            outs = _add_cast([gw[nm][l] for nm in grp], [sib_part[nm] for nm in grp], place, f"rs_add_{tag}_{gi}")
            qb.update(zip(grp, outs))
        return qb

    def sum_parts(nms, l, qb, parts, tag):
        for gi, grp in enumerate(shard_groups(nms, l)):
            prevs = [red[nm] for nm in grp] if grp[0] in red else None
            outs = _sum_chips([parts[nm] for nm in grp], [qb[nm] for nm in grp], place, l, nl, prevs, f"rs_sum_{tag}_{gi}")
            red.update(zip(grp, outs))

    pair = lambda nms, l: _pair_comm([gw[nm][l] for nm in nms])
    chips = lambda nms, qb: _chips_comm([qb[nm] for nm in nms])
    share = lambda nms, l: _share_comm([red[nm] for nm in nms], l)

    def split(joined, got):
        return [got[o0:o1] for o0, o1 in joined.spans]

    mix_rest = [nm for nm in mix_w if nm != "w_in"]
    mix_parts = [["w_out", "w_conv_out"], ["w_mla_out", "w_ukv", "w_uq"], []]

    for l in reversed(range(nl)):
        s, mw = saved[l], mixw[l]
        riding_rs = l == 0

        def ffn_back(tag, xin, gains, gts, ups, dout, comm=None):
            wg, wu, wd = (bufs[f"{tag}_w_{p}", l] for p in ("gate", "up", "down"))
            dxi, dgain, dgt, dup, act, hb, dob, got = _ffn_bwd(xin, gains, l, wg, wu, wd, gts, ups, dout,
                                                              f"{tag}_bwd_{l}", comm)
            for p, lhs, rhs in zip(("gate", "up", "down"), (dgt, dup, act), (hb, hb, dob)):
                gw[f"{tag}_w_{p}"][l] = _mm(lhs, rhs, "tn", name=f"{tag}_dw_{p}_{l}", a_g=True)
            gw[f"{tag}_norm"][l] = dgain
            return dxi, got

        dx2, got = ffn_back("ffn2", s["x2"], ffn2_norm, s["gt2"], s["up2"], dx, pair(big, 1) if riding_rs else None)
        if riding_rs:
            qb1 = add_parts(big, 1, dict(zip(big, got)), "l1")
        dyv = _mm(dx2, mw["w_out"], "nt", name=f"mix_out_dy_{l}", out_dtype=BF16)
        gw["w_out"][l] = _mm(s["yv"], dx2, "tn", name=f"mix_out_dw_{l}").reshape(NCHIP, d // NCHIP, d)
        dyc, dym, dproj, dgb = _comb_bwd(s["proj"], gb_k, l, s["yc"], s["ym"], dyv, dp, f"combine_bwd_{l}")
        gw["gate_bias"][l] = dgb
        dsc = _mm(dyc, mw["w_co"], "nt", name=f"conv_out_ds_{l}")
        gw["w_conv_out"][l] = _cols_shards(_mm(s["sc"], dyc, "tn", name=f"conv_out_dw_{l}"))
        duc, gw["conv_ln_g"][l], gw["conv_ln_b"][l], gw["conv_b"][l] = _ln_silu_bwd(
            s["uc"], conv_ln_g, conv_ln_b, l, dsc, f"conv_ln_bwd_{l}")
        du, dcw = _dwconv_bwd(duc, s["proj"], cw_k, l, lay["a"], cc, f"dwconv_bwd_{l}")
        gw["conv_w"][l] = dcw[:KW]
        dproj = _glu_bwd(du, s["proj"], dproj, lay["a"], cc, f"glu_bwd_{l}")
        do = _mm(dym, mw["w_mo"], "nt", name=f"mla_out_do_{l}", out_dtype=BF16)
        dwmo = _mm(s["o"], dym, "tn", name=f"mla_out_dw_{l}").reshape(N_HEADS, HP, d)[:, HP - V_DIM:].reshape(vw, d)
        gw["w_mla_out"][l] = _cols_shards(dwmo)
        comm = _join(chips(big, qb1), pair(ffn2_w, 0)) if riding_rs else None
        dq, dk, dv, got = _flash_bwd(s["q"], s["k"], s["kv"], do, s["o"], s["lse"], f"flash_bwd_{l}", comm)
        if riding_rs:
            parts1, sib_f2 = split(comm, got)
            sum_parts(big, 1, qb1, dict(zip(big, parts1)), "l1")
            qb_f2 = add_parts(ffn2_w, 0, dict(zip(ffn2_w, sib_f2)), "f2")
        dproj, dqr, dkv, cqn, ckvn, gw["cq_norm"][l], gw["ckv_norm"][l], dqn, dkn = _mla_pre_bwd(
            s["proj"], lay, l, cq_norm, ckv_norm, qn_k, kn_k, mw["w_uq"], mw["w_ukv"], tabs, dq, dk, dv, dproj,
            f"mla_pre_bwd_{l}")
        gw["q_norm"][l], gw["k_norm"][l] = dqn[:, :QK], dkn[:, :QK]
        dwuq = _mm(dqr, cqn, "tn", name=f"uq_dw_{l}").reshape(N_HEADS, HP, ql)[:, :QK]
        gw["w_uq"][l] = dwuq.reshape(NCHIP, N_HEADS * QK // NCHIP, ql)
        gw["w_ukv"][l] = _cols_shards(_mm(ckvn, dkv, "tn", name=f"ukv_dw_{l}"))
        dhm = _mm(dproj, mw["w_in"], "nn", name=f"proj_dh_{l}")
        dwin = _mm(dproj, s["hm"], "tn", name=f"proj_dw_{l}")
        gw["w_in"][l] = jnp.concatenate([dwin[2 * d:2 * d + nat_g], dwin[:2 * d]], axis=0).reshape(
            NCHIP, (nat_g + 2 * d) // NCHIP, d)
        dx1, gw["mix_norm"][l], got = _rms_back(s["x1"], mix_norm, l, dhm, dx2, f"mix_norm_bwd_{l}",
                                                pair(["w_in"], 0) if riding_rs else None)
        if not riding_rs:
            dx, _ = ffn_back("ffn1", s["x0"], ffn1_norm, s["gt1"], s["up1"], dx1)
            continue
        qb_win = add_parts(["w_in"], 0, {"w_in": got[0]}, "win")
        comm = _join(share(big, 1), chips(ffn2_w, qb_f2), chips(["w_in"], qb_win), pair(mix_rest, 0))
        wg, wu, wd = (bufs[f"ffn1_w_{p}", l] for p in ("gate", "up", "down"))
        dx, dgain, dgt, dup, act, hb, dob, got = _ffn_bwd(s["x0"], ffn1_norm, l, wg, wu, wd, s["gt1"], s["up1"], dx1,
                                                         f"ffn1_bwd_{l}", comm)
        gw["ffn1_norm"][l] = dgain
        shared, parts_f2, parts_win, sib_mx = split(comm, got)
        red.update(zip(big, shared))
        sum_parts(ffn2_w, 0, qb_f2, dict(zip(ffn2_w, parts_f2)), "f2")
        sum_parts(["w_in"], 0, qb_win, {"w_in": parts_win[0]}, "win")
        qb_mx = add_parts(mix_rest, 0, dict(zip(mix_rest, sib_mx)), "mx")
        vec_names = ["ffn1_norm", "mix_norm", "ffn2_norm", "conv_b", "conv_ln_g", "conv_ln_b", "cq_norm", "ckv_norm",
                     "q_norm", "k_norm"]
        rows = [jnp.concatenate(gw[nm], axis=0) for nm in vec_names]
        rows += [jnp.concatenate(gw["gate_bias"], axis=0), jnp.concatenate(gw["conv_w"], axis=0), loss_row]
        parts_mx, f1 = {}, [f"ffn1_w_{p}" for p in ("gate", "up", "down")]
        qb_f1 = {}
        for part, (lhs, rhs) in enumerate(zip((dgt, dup, act), (hb, hb, dob))):
            cms = [_spread_comm(_rows_pad(rows, d))] if part == 0 else [pair(f1[part - 1:part], 0)]
            if mix_parts[part]:
                cms.append(chips(mix_parts[part], qb_mx))
            if part == 2:
                cms.append(chips(f1[0:1], qb_f1))
            comm = _join(*cms)
            gw[f1[part]][l], got = _mm(lhs, rhs, "tn", name=f"{f1[part]}_dw_{l}", a_g=True, comm=comm)
            pieces = split(comm, got)
            if part == 0:
                small_blocks = pieces[0][0]
            else:
                qb_f1.update(add_parts(f1[part - 1:part], 0, dict(zip(f1[part - 1:part], pieces[0])), f"f1{part - 1}"))
            if mix_parts[part]:
                parts_mx.update(zip(mix_parts[part], pieces[1]))
            if part == 2:
                sum_parts(f1[0:1], 0, qb_f1, dict(zip(f1[0:1], pieces[-1])), "f10")
        sum_parts(mix_rest, 0, qb_mx, parts_mx, "mx")

    sib = dict(zip(f1[2:3], _comm_call(pair(f1[2:3], 0), "rs_pair")))
    qb_f1.update(add_parts(f1[2:3], 0, sib, "f12"))
    parts_f1 = dict(zip(f1[1:3], _comm_call(chips(f1[1:3], qb_f1), "rs_chips")))
    sum_parts(f1[1:3], 0, qb_f1, parts_f1, "f1")
    red = dict(zip(big, _comm_call(share(big, 0), "rs_share")))
    grads = {nm: g.reshape(wts[nm].shape) for nm, g in red.items()}

    total = _sum_devices(small_blocks, "allreduce_small")
    r0 = 0
    for nm in vec_names:
        grads[nm] = total[r0:r0 + nl, :wts[nm].shape[1]]
        r0 += nl
    gb_all = total[r0:r0 + 2 * nl, :d].reshape(nl, 2, NCHIP, d // NCHIP)
    r0 += 2 * nl
    cw_all = total[r0:r0 + KW * nl, :cc].reshape(nl, KW, NCHIP, cc // NCHIP)
    r0 += KW * nl
    grads["gate_bias"] = lax.dynamic_index_in_dim(gb_all, chip, axis=2, keepdims=False)
    grads["conv_w"] = lax.dynamic_index_in_dim(cw_all, chip, axis=2, keepdims=False)
    loss = total[r0, 0]

    delta, new_m, new_v = {}, {}, {}
    by_shape = {}
    for nm in names:
        shp = wts[nm].shape
        by_shape.setdefault((shp[0] * (shp[1] if len(shp) == 3 else 1), shp[-1]), []).append(nm)
    for gi, (shp2, grp) in enumerate(by_shape.items()):
        to2 = lambda a: a.reshape(shp2)
        ds_, ms_, vs_ = _adamw([to2(wts[nm]) for nm in grp], [to2(grads[nm]) for nm in grp],
                               [to2(mom[nm]) for nm in grp], [to2(var[nm]) for nm in grp], f"adamw_{gi}")
        for nm, dd, mm_, vv in zip(grp, ds_, ms_, vs_):
            delta[nm], new_m[nm], new_v[nm] = (a.reshape(wts[nm].shape) for a in (dd, mm_, vv))

    return (loss, dx[None], *[view(nm, grads[nm]) for nm in names], *[view(nm, delta[nm]) for nm in names],
            *[view(nm, new_m[nm]) for nm in names], *[view(nm, new_v[nm]) for nm in names])
```

```python
import functools

import jax
import jax.numpy as jnp
from jax import lax
from jax.experimental import pallas as pl
from jax.experimental.pallas import tpu as pltpu

F32, BF16 = jnp.float32, jnp.bfloat16
SDS = jax.ShapeDtypeStruct
MESH = pl.DeviceIdType.MESH
ANY = pl.BlockSpec(memory_space=pl.ANY)

NCHIP = 4
N_DEV = 8
N_HEADS, NOPE, ROPE, V_DIM = 8, 64, 32, 64
QK = NOPE + ROPE
HP = 128
CHUNK = 64
KW = 31
CPAD = 32
ROPE_THETA = 10000.0
EPS = 1e-6
LR, B1, B2, EPS_ADAM, WD, STEP = 0.001, 0.9, 0.999, 1e-08, 0.01, 10
VMEM_BIG = 48 << 20
FFN_PARTS = 2
FLASH_HEADS = 2


def _cparams(sem=None, vmem=None):
    kw = {}
    if sem is not None:
        kw["dimension_semantics"] = sem
    if vmem is not None:
        kw["vmem_limit_bytes"] = vmem
    return pltpu.CompilerParams(**kw)


def _rt(t):
    return min(512, t // 2)


def _pick(n, cands):
    for c in cands:
        if c <= n and n % c == 0:
            return c
    return n


def _tile2(r, c, nblocks, row_mult):
    tr, tc = r, c
    while 2 * nblocks * tr * tc * 4 > VMEM_BIG // 2 and tr % (2 * row_mult) == 0:
        tr //= 2
    while 2 * nblocks * tr * tc * 4 > VMEM_BIG // 2 and tc % 256 == 0:
        tc //= 2
    return tr, tc


def _sig(v):
    return 1.0 / (1.0 + jnp.exp(-v))


def _rms_r(v):
    return lax.rsqrt(jnp.mean(v * v, axis=-1, keepdims=True) + EPS)


def _rms_bwd(xv, g, dy):
    r = _rms_r(xv)
    xh = xv * r
    dg = jnp.sum(dy * xh, axis=0, keepdims=True)
    dxh = dy * g
    dx = r * (dxh - xh * jnp.mean(dxh * xh, axis=-1, keepdims=True))
    return dx, dg


def _dot(a, b):
    return jnp.dot(a, b, preferred_element_type=F32)


def _dot_nt(a, b):
    return lax.dot_general(a, b, (((1,), (1,)), ((), ())), preferred_element_type=F32)


def _dot_tn(a, b):
    return lax.dot_general(a, b, (((0,), (0,)), ((), ())), preferred_element_type=F32)


class _Comm:
    def __init__(self, ins, out_shapes, aliases, nsem, start, finish, mid=None):
        self.ins, self.out_shapes, self.aliases, self.nsem = list(ins), list(out_shapes), dict(aliases), nsem
        self.start, self.finish, self.mid = start, finish, mid


def _call(body, comm, first, last, *, name, grid, in_specs, args, out_specs, out_shape, scratch, sem, mid_at=None):
    in_specs, args, out_specs, out_shape, scratch = map(list, (in_specs, args, out_specs, out_shape, scratch))
    n_in, n_out, n_scr = len(args), len(out_shape), len(scratch)
    aliases = {}
    kern = body
    if comm is not None:
        nci, nco = len(comm.ins), len(comm.out_shapes)
        o0 = n_in + nci
        s0 = o0 + n_out + nco

        def kern(*refs):
            cins, couts = refs[n_in:o0], refs[o0 + n_out:s0]
            ssem, rsem = refs[s0 + n_scr:]
            pl.when(first())(lambda: comm.start(cins, couts, ssem, rsem))
            if comm.mid is not None and mid_at is not None:
                pl.when(mid_at())(lambda: comm.mid(cins, couts, ssem, rsem))
            body(*refs[:n_in], *refs[o0:o0 + n_out], *refs[s0:s0 + n_scr])

            def end():
                if comm.mid is not None and mid_at is None:
                    comm.mid(cins, couts, ssem, rsem)
                comm.finish(cins, couts, ssem, rsem)

            pl.when(last())(end)

        in_specs += [ANY] * nci
        args += comm.ins
        out_specs += [ANY] * nco
        out_shape += comm.out_shapes
        scratch += [pltpu.SemaphoreType.DMA((comm.nsem,))] * 2
        aliases = {n_in + i: n_out + o for i, o in comm.aliases.items()}
    outs = pl.pallas_call(kern, name=name, grid=grid, in_specs=in_specs, out_specs=out_specs, out_shape=out_shape,
                          scratch_shapes=scratch, input_output_aliases=aliases,
                          compiler_params=_cparams(sem, VMEM_BIG))(*args)
    return outs[:n_out], outs[n_out:]


class _SemView:
    def __init__(self, sems, base):
        self.sems, self.base = sems, base

    @property
    def at(self):
        return self

    def __getitem__(self, k):
        return self.sems.at[self.base + k]


def _join(*comms):
    ins, outs, aliases, spans, nsem = [], [], {}, [], 0
    for cm in comms:
        aliases.update({len(ins) + i: len(outs) + o for i, o in cm.aliases.items()})
        spans.append((len(ins), len(ins) + len(cm.ins), len(outs), len(outs) + len(cm.out_shapes), nsem))
        ins += cm.ins
        outs += cm.out_shapes
        nsem += cm.nsem

    def run(which, cins, couts, ssem, rsem):
        for cm, (i0, i1, o0, o1, base) in zip(comms, spans):
            if getattr(cm, which) is not None:
                getattr(cm, which)(cins[i0:i1], couts[o0:o1], _SemView(ssem, base), _SemView(rsem, base))

    joined = _Comm(ins, outs, aliases, nsem, functools.partial(run, "start"), functools.partial(run, "finish"),
                   functools.partial(run, "mid") if any(cm.mid is not None for cm in comms) else None)
    joined.spans = [(o0, o1) for _, _, o0, o1, _ in spans]
    return joined


def _comm_call(comm, name):
    once = lambda: pl.program_id(0) == 0
    return _call(lambda: None, comm, once, once, name=name, grid=(1,), in_specs=[], args=[], out_specs=[],
                 out_shape=[], scratch=[], sem=("arbitrary",))[1]


def _mm(a, b, mode, *, name, out_dtype=F32, res=None, a_g=False, comm=None):
    a2 = a.shape[1:] if a_g else a.shape
    if mode == "nn":
        (m, k), (k2, n) = a2, b.shape
    elif mode == "nt":
        (m, k), (n, k2) = a2, b.shape
    else:
        (k, m), (k2, n) = a2, b.shape
    assert k == k2, (name, a.shape, b.shape)
    g = a.shape[0] if a_g else 1
    tm = m if m <= 768 else _pick(m, (768, 512, 256, 128))
    tn = n if n <= 1280 else _pick(n, (1280, 1024, 768, 512, 256, 128))
    tk = k if k <= 1280 else _pick(k, (1280, 1024, 768, 512, 256, 128))
    fits = lambda kk, nn: 2 * (kk * tm * a.dtype.itemsize + kk * nn * b.dtype.itemsize + tm * nn * 4) <= VMEM_BIG - (8 << 20)
    if fits(k, n):
        tn, tk = n, k
    elif fits(k, tn):
        tk = k
    nk = k // tk
    dn = {"nn": (((1,), (0,)), ((), ())), "nt": (((1,), (1,)), ((), ())), "tn": (((0,), (0,)), ((), ()))}[mode]

    def body(*refs):
        a_ref, b_ref = refs[0], refs[1]
        res_ref = refs[2] if res is not None else None
        o_ref = refs[3] if res is not None else refs[2]
        p = lax.dot_general(a_ref[...].astype(BF16), b_ref[...].astype(BF16), dn, preferred_element_type=F32)

        def fin(v):
            if res_ref is not None:
                v = v + res_ref[...]
            o_ref[...] = v.astype(out_dtype)

        if nk == 1:
            fin(p)
        else:
            acc_ref = refs[-1]
            kk = pl.program_id(3)

            @pl.when(kk == 0)
            def _():
                acc_ref[...] = p

            @pl.when(kk > 0)
            def _():
                acc_ref[...] += p

            @pl.when(kk == nk - 1)
            def _():
                fin(acc_ref[...])

    a_block, a_idx = ((tk, tm), lambda gg, i, j, kk: (kk, i)) if mode == "tn" else ((tm, tk), lambda gg, i, j, kk: (i, kk))
    if a_g:
        a_spec = pl.BlockSpec((None,) + a_block, lambda gg, i, j, kk: (gg,) + a_idx(gg, i, j, kk))
    else:
        a_spec = pl.BlockSpec(a_block, a_idx)
    if mode == "nt":
        b_spec = pl.BlockSpec((tn, tk), lambda gg, i, j, kk: (j, kk))
    else:
        b_spec = pl.BlockSpec((tk, tn), lambda gg, i, j, kk: (kk, j))
    in_specs, args = [a_spec, b_spec], [a, b]
    if res is not None:
        in_specs.append(pl.BlockSpec((tm, tn), lambda gg, i, j, kk: (i, j)))
        args.append(res)
    if a_g:
        out_spec, oshape = pl.BlockSpec((None, tm, tn), lambda gg, i, j, kk: (gg, i, j)), (g, m, n)
    else:
        out_spec, oshape = pl.BlockSpec((tm, tn), lambda gg, i, j, kk: (i, j)), (m, n)
    grid = (g, m // tm, n // tn, nk)

    def at(corner):
        hit = pl.program_id(0) == corner[0]
        for ax in range(1, 4):
            hit = jnp.logical_and(hit, pl.program_id(ax) == corner[ax])
        return hit

    outs, couts = _call(body, comm, lambda: at((0, 0, 0, 0)), lambda: at(tuple(dim - 1 for dim in grid)), name=name,
                        grid=grid, in_specs=in_specs, args=args, out_specs=[out_spec], out_shape=[SDS(oshape, out_dtype)],
                        scratch=[pltpu.VMEM((tm, tn), F32)] if nk > 1 else [], sem=("arbitrary",) * 4)
    return outs[0] if comm is None else (outs[0], couts)


def _ffn_fwd(x, gains, l, wg, wu, wd, name, comm=None, mid_frac=0.75):
    t, d = x.shape
    f = wg.shape[-2]
    tt = _rt(t)

    def body(x_ref, g_ref, wg_ref, wu_ref, wd_ref, o_ref, gt_ref, up_ref, h_sc, acc_sc):
        j = pl.program_id(1)

        @pl.when(j == 0)
        def _():
            xv = x_ref[...]
            h_sc[...] = (xv * _rms_r(xv) * g_ref[l:l + 1, :]).astype(BF16)
            acc_sc[...] = jnp.zeros_like(acc_sc)

        for part in range(FFN_PARTS):
            rows = pl.ds(part * (tt // FFN_PARTS), tt // FFN_PARTS)
            h = h_sc[rows, :]
            gt = _dot_nt(h, wg_ref[...]).astype(BF16)
            up = _dot_nt(h, wu_ref[...]).astype(BF16)
            gt_ref[rows, :] = gt
            up_ref[rows, :] = up
            acc_sc[rows, :] += _dot(gt * _sig(gt) * up, wd_ref[...])

        @pl.when(j == NCHIP - 1)
        def _():
            o_ref[...] = x_ref[...] + 0.5 * acc_sc[...]

    wspec = pl.BlockSpec((None, f, d), lambda i, j: (j, 0, 0))
    row = pl.BlockSpec((tt, d), lambda i, j: (i, 0))
    sh = pl.BlockSpec((None, tt, f), lambda i, j: (j, i, 0))
    ni = t // tt
    first = lambda: jnp.logical_and(pl.program_id(0) == 0, pl.program_id(1) == 0)
    last = lambda: jnp.logical_and(pl.program_id(0) == ni - 1, pl.program_id(1) == NCHIP - 1)
    mid_at = lambda: pl.program_id(0) * NCHIP + pl.program_id(1) == int(mid_frac * ni * NCHIP)
    outs, couts = _call(
        body, comm, first, last, name=name, grid=(ni, NCHIP), mid_at=mid_at,
        in_specs=[row, pl.BlockSpec(gains.shape, lambda i, j: (0, 0)), wspec, wspec, wspec], args=[x, gains, wg, wu, wd],
        out_specs=[row, sh, sh], out_shape=[SDS((t, d), F32), SDS((NCHIP, t, f), BF16), SDS((NCHIP, t, f), BF16)],
        scratch=[pltpu.VMEM((tt, d), BF16), pltpu.VMEM((tt, d), F32)], sem=("arbitrary", "arbitrary"))
    return (*outs, couts)


def _ffn_bwd(x, gains, l, wg, wu, wd, gts, ups, dout, name, comm=None):
    t, d = x.shape
    f = wg.shape[-2]
    tt = _rt(t)

    def body(x_ref, g_ref, wg_ref, wu_ref, wd_ref, gt_ref, up_ref, do_ref,
             dx_ref, dg_ref, dgt_ref, dup_ref, act_ref, h_ref, dob_ref, dh_sc):
        i, j = pl.program_id(0), pl.program_id(1)

        @pl.when(j == 0)
        def _():
            xv = x_ref[...]
            h_ref[...] = (xv * _rms_r(xv) * g_ref[l:l + 1, :]).astype(BF16)
            dob_ref[...] = (0.5 * do_ref[...]).astype(BF16)
            dh_sc[...] = jnp.zeros_like(dh_sc)

        @pl.when(jnp.logical_and(i == 0, j == 0))
        def _():
            dg_ref[...] = jnp.zeros_like(dg_ref)

        for part in range(FFN_PARTS):
            rows = pl.ds(part * (tt // FFN_PARTS), tt // FFN_PARTS)
            dact = _dot_nt(dob_ref[rows, :], wd_ref[...]).astype(BF16)
            gt = gt_ref[rows, :]
            up = up_ref[rows, :]
            s = _sig(gt)
            sl = gt * s
            dup = dact * sl
            dgt = dact * up * (s + sl * (1.0 - s))
            dgt_ref[rows, :] = dgt
            dup_ref[rows, :] = dup
            act_ref[rows, :] = sl * up
            dh_sc[rows, :] += _dot(dgt, wg_ref[...]) + _dot(dup, wu_ref[...])

        @pl.when(j == NCHIP - 1)
        def _():
            dxn, dg = _rms_bwd(x_ref[...], g_ref[l:l + 1, :], dh_sc[...])
            dx_ref[...] = do_ref[...] + dxn
            dg_ref[...] += dg

    wspec = pl.BlockSpec((None, f, d), lambda i, j: (j, 0, 0))
    row = pl.BlockSpec((tt, d), lambda i, j: (i, 0))
    sh = pl.BlockSpec((None, tt, f), lambda i, j: (j, i, 0))
    ni = t // tt
    first = lambda: jnp.logical_and(pl.program_id(0) == 0, pl.program_id(1) == 0)
    last = lambda: jnp.logical_and(pl.program_id(0) == ni - 1, pl.program_id(1) == NCHIP - 1)
    outs, couts = _call(
        body, comm, first, last, name=name, grid=(ni, NCHIP),
        in_specs=[row, pl.BlockSpec(gains.shape, lambda i, j: (0, 0)), wspec, wspec, wspec, sh, sh, row],
        args=[x, gains, wg, wu, wd, gts, ups, dout],
        out_specs=[row, pl.BlockSpec((1, d), lambda i, j: (0, 0)), sh, sh, sh, row, row],
        out_shape=[SDS((t, d), F32), SDS((1, d), F32)] + [SDS((NCHIP, t, f), BF16)] * 3 + [SDS((t, d), BF16)] * 2,
        scratch=[pltpu.VMEM((tt, d), F32)], sem=("arbitrary", "arbitrary"))
    return (*outs, couts)


def _rms_fwd(x, gains, l, name):
    t, d = x.shape
    tt = _rt(t)

    def body(x_ref, g_ref, o_ref):
        xv = x_ref[...]
        o_ref[...] = (xv * _rms_r(xv) * g_ref[l:l + 1, :]).astype(BF16)

    row = pl.BlockSpec((tt, d), lambda i: (i, 0))
    return pl.pallas_call(body, name=name, grid=(t // tt,), in_specs=[row, pl.BlockSpec(gains.shape, lambda i: (0, 0))],
                          out_specs=row, out_shape=SDS((t, d), BF16), compiler_params=_cparams(("arbitrary",)))(x, gains)


def _rms_back(x, gains, l, dh, dres, name, comm=None):
    t, d = x.shape
    tt = _rt(t)

    def body(x_ref, g_ref, dh_ref, dr_ref, dx_ref, dg_ref):
        @pl.when(pl.program_id(0) == 0)
        def _():
            dg_ref[...] = jnp.zeros_like(dg_ref)

        dxn, dg = _rms_bwd(x_ref[...], g_ref[l:l + 1, :], dh_ref[...])
        dx_ref[...] = dr_ref[...] + dxn
        dg_ref[...] += dg

    row = pl.BlockSpec((tt, d), lambda i: (i, 0))
    ni = t // tt
    outs, couts = _call(body, comm, lambda: pl.program_id(0) == 0, lambda: pl.program_id(0) == ni - 1, name=name,
                        grid=(ni,), in_specs=[row, pl.BlockSpec(gains.shape, lambda i: (0, 0)), row, row],
                        args=[x, gains, dh, dres], out_specs=[row, pl.BlockSpec((1, d), lambda i: (0, 0))],
                        out_shape=[SDS((t, d), F32), SDS((1, d), F32)], scratch=[], sem=("arbitrary",))
    return (*outs, couts)


def _dwconv_fwd(proj, convw, convb, l, o_a, cc, name):
    t = proj.shape[0]
    r = min(256, t)
    nb = cc // HP

    def body(a_ref, gate_ref, w_ref, b_ref, uc_ref, u_sc):
        u_sc[0:CPAD, :] = jnp.zeros((CPAD, HP), F32)

        def fill(ci, carry):
            r0 = pl.multiple_of(ci * r, r)
            u_sc[pl.ds(CPAD + r0, r), :] = a_ref[pl.ds(r0, r), :].astype(F32) * _sig(gate_ref[pl.ds(r0, r), :].astype(F32))
            return carry

        lax.fori_loop(0, t // r, fill, 0)
        w = w_ref[l]
        bias = b_ref[l:l + 1, :]

        def conv(ci, carry):
            r0 = pl.multiple_of(ci * r, r)
            win = u_sc[pl.ds(r0, r + CPAD), :]
            acc = jnp.zeros((r, HP), F32) + bias
            for k in range(KW):
                off = CPAD - (KW - 1) + k
                acc = acc + win[off:off + r, :] * w[k:k + 1, :]
            uc_ref[pl.ds(r0, r), :] = acc
            return carry

        lax.fori_loop(0, t // r, conv, 0)

    col = lambda base: pl.BlockSpec((t, HP), lambda c: (0, base // HP + c))
    return pl.pallas_call(
        body, name=name, grid=(nb,),
        in_specs=[col(o_a), col(o_a + cc), pl.BlockSpec((convw.shape[0], CPAD, HP), lambda c: (0, 0, c)),
                  pl.BlockSpec((convb.shape[0], HP), lambda c: (0, c))],
        out_specs=pl.BlockSpec((t, HP), lambda c: (0, c)), out_shape=SDS((t, cc), F32),
        scratch_shapes=[pltpu.VMEM((t + CPAD, HP), F32)],
        compiler_params=_cparams(("arbitrary",), VMEM_BIG))(proj, proj, convw, convb)


def _dwconv_bwd(duc, proj, convw, l, o_a, cc, name):
    t = proj.shape[0]
    r = min(256, t)
    nb = cc // HP

    def body(d_ref, a_ref, gate_ref, w_ref, du_ref, dw_ref, u_sc, d_sc, dw_sc):
        u_sc[0:CPAD, :] = jnp.zeros((CPAD, HP), F32)
        d_sc[t:t + CPAD, :] = jnp.zeros((CPAD, HP), F32)
        dw_sc[...] = jnp.zeros_like(dw_sc)

        def fill(ci, carry):
            r0 = pl.multiple_of(ci * r, r)
            u_sc[pl.ds(CPAD + r0, r), :] = a_ref[pl.ds(r0, r), :].astype(F32) * _sig(gate_ref[pl.ds(r0, r), :].astype(F32))
            d_sc[pl.ds(r0, r), :] = d_ref[pl.ds(r0, r), :]
            return carry

        lax.fori_loop(0, t // r, fill, 0)
        w = w_ref[l]

        def conv(ci, carry):
            r0 = pl.multiple_of(ci * r, r)
            dwin = d_sc[pl.ds(r0, r + CPAD), :]
            uwin = u_sc[pl.ds(r0, r + CPAD), :]
            dcur = dwin[0:r, :]
            acc = jnp.zeros((r, HP), F32)
            for k in range(KW):
                acc = acc + dwin[KW - 1 - k:KW - 1 - k + r, :] * w[k:k + 1, :]
                off = CPAD - (KW - 1) + k
                part = (dcur * uwin[off:off + r, :]).reshape(r // 8, 8, HP).sum(axis=0)
                dw_sc[8 * k:8 * k + 8, :] += part
            du_ref[pl.ds(r0, r), :] = acc
            return carry

        lax.fori_loop(0, t // r, conv, 0)
        dw_ref[...] = jnp.zeros_like(dw_ref)
        for k in range(KW):
            dw_ref[k:k + 1, :] = jnp.sum(dw_sc[8 * k:8 * k + 8, :], axis=0, keepdims=True)

    col = lambda base: pl.BlockSpec((t, HP), lambda c: (0, base // HP + c))
    return pl.pallas_call(
        body, name=name, grid=(nb,),
        in_specs=[pl.BlockSpec((t, HP), lambda c: (0, c)), col(o_a), col(o_a + cc),
                  pl.BlockSpec((convw.shape[0], CPAD, HP), lambda c: (0, 0, c))],
        out_specs=[pl.BlockSpec((t, HP), lambda c: (0, c)), pl.BlockSpec((CPAD, HP), lambda c: (0, c))],
        out_shape=[SDS((t, cc), F32), SDS((CPAD, cc), F32)],
        scratch_shapes=[pltpu.VMEM((t + CPAD, HP), F32), pltpu.VMEM((t + CPAD, HP), F32), pltpu.VMEM((8 * CPAD, HP), F32)],
        compiler_params=_cparams(("arbitrary",), VMEM_BIG))(duc, proj, proj, convw)


def _ln_parts(uc, g, b):
    mu = jnp.mean(uc, axis=-1, keepdims=True)
    xc = uc - mu
    r = lax.rsqrt(jnp.mean(xc * xc, axis=-1, keepdims=True) + EPS)
    xh = xc * r
    return r, xh, xh * g + b


def _ln_silu(uc, ln_g, ln_b, l, name):
    t, cc = uc.shape
    tt = _rt(t)

    def body(u_ref, g_ref, b_ref, s_ref):
        _, _, yv = _ln_parts(u_ref[...], g_ref[l:l + 1, :], b_ref[l:l + 1, :])
        s_ref[...] = (yv * _sig(yv)).astype(BF16)

    row = pl.BlockSpec((tt, cc), lambda i: (i, 0))
    full = pl.BlockSpec(ln_g.shape, lambda i: (0, 0))
    return pl.pallas_call(body, name=name, grid=(t // tt,), in_specs=[row, full, full], out_specs=row,
                          out_shape=SDS((t, cc), BF16), compiler_params=_cparams(("arbitrary",)))(uc, ln_g, ln_b)


def _ln_silu_bwd(uc, ln_g, ln_b, l, ds, name):
    t, cc = uc.shape
    tt = _rt(t)

    def body(u_ref, g_ref, b_ref, ds_ref, du_ref, dg_ref, db_ref, dcb_ref):
        @pl.when(pl.program_id(0) == 0)
        def _():
            dg_ref[...] = jnp.zeros_like(dg_ref)
            db_ref[...] = jnp.zeros_like(db_ref)
            dcb_ref[...] = jnp.zeros_like(dcb_ref)

        g = g_ref[l:l + 1, :]
        r, xh, yv = _ln_parts(u_ref[...], g, b_ref[l:l + 1, :])
        sy = _sig(yv)
        dy = ds_ref[...] * (sy * (1.0 + yv * (1.0 - sy)))
        dg_ref[...] += jnp.sum(dy * xh, axis=0, keepdims=True)
        db_ref[...] += jnp.sum(dy, axis=0, keepdims=True)
        dxh = dy * g
        du = r * (dxh - jnp.mean(dxh, axis=-1, keepdims=True) - xh * jnp.mean(dxh * xh, axis=-1, keepdims=True))
        du_ref[...] = du
        dcb_ref[...] += jnp.sum(du, axis=0, keepdims=True)

    row = pl.BlockSpec((tt, cc), lambda i: (i, 0))
    full = pl.BlockSpec(ln_g.shape, lambda i: (0, 0))
    vec = pl.BlockSpec((1, cc), lambda i: (0, 0))
    return pl.pallas_call(body, name=name, grid=(t // tt,), in_specs=[row, full, full, row],
                          out_specs=[row, vec, vec, vec], out_shape=[SDS((t, cc), F32)] + [SDS((1, cc), F32)] * 3,
                          compiler_params=_cparams(("arbitrary",)))(uc, ln_g, ln_b, ds)


def _glu_bwd(du, proj, dproj, o_a, cc, name):
    t = du.shape[0]
    tt = _rt(t)

    def body(du_ref, a_ref, gate_ref, prev_ref, o_ref):
        sg = _sig(gate_ref[...].astype(F32))
        dv = du_ref[...]
        o_ref[:, 0:cc] = (dv * sg).astype(BF16)
        o_ref[:, cc:2 * cc] = (dv * a_ref[...] * sg * (1.0 - sg)).astype(BF16)

    return pl.pallas_call(
        body, name=name, grid=(t // tt,),
        in_specs=[pl.BlockSpec((tt, cc), lambda i: (i, 0)), pl.BlockSpec((tt, cc), lambda i: (i, o_a // cc)),
                  pl.BlockSpec((tt, cc), lambda i: (i, o_a // cc + 1)), ANY],
        out_specs=pl.BlockSpec((tt, 2 * cc), lambda i: (i, o_a // (2 * cc))),
        out_shape=SDS(dproj.shape, dproj.dtype), input_output_aliases={3: 0},
        compiler_params=_cparams(("arbitrary",)))(du, proj, proj, dproj)


def _rope(v, cs, s1, s2):
    return v * cs + pltpu.roll(v, HP - ROPE // 2, 1) * s1 + pltpu.roll(v, ROPE // 2, 1) * s2


def _rope_t(dv, cs, s1, s2):
    return dv * cs + pltpu.roll(dv * s1, ROPE // 2, 1) + pltpu.roll(dv * s2, HP - ROPE // 2, 1)


def _head_norm(v, g):
    r = lax.rsqrt(jnp.sum(v * v, axis=-1, keepdims=True) * (1.0 / QK) + EPS)
    return v * r * g, r


def _head_norm_bwd(v, r, g, dy):
    xh = v * r
    dg = jnp.sum(dy * xh, axis=0, keepdims=True)
    dxh = dy * g
    dx = r * (dxh - xh * (jnp.sum(dxh * xh, axis=-1, keepdims=True) * (1.0 / QK)))
    return dx, dg


def _mla_specs(tt, lay, cq_norm, ckv_norm, qn, kn, wuq, wukv):
    ql, kvl = cq_norm.shape[1], ckv_norm.shape[1]
    full = lambda a: pl.BlockSpec(a.shape, lambda i: (0,) * a.ndim)
    tab = pl.BlockSpec((tt, HP), lambda i: (i, 0))
    return [pl.BlockSpec((tt, ql), lambda i: (i, lay["cq"] // ql)),
            pl.BlockSpec((tt, kvl), lambda i: (i, lay["ckv"] // kvl)),
            pl.BlockSpec((tt, HP), lambda i: (i, lay["kr"] // HP)),
            full(cq_norm), full(ckv_norm), full(qn), full(kn), full(wuq), full(wukv), tab, tab, tab]


def _mla_pre_fwd(proj, lay, l, cq_norm, ckv_norm, qn, kn, wuq, wukv, tabs, name):
    t = proj.shape[0]
    tt = _rt(t)
    hw = N_HEADS * HP

    def body(cq_ref, ckv_ref, kr_ref, gq_ref, gkv_ref, qn_ref, kn_ref, wq_ref, wkv_ref, c_ref, s1_ref, s2_ref,
             q_ref, k_ref, v_ref):
        cq = cq_ref[...].astype(F32)
        cqn = (cq * _rms_r(cq) * gq_ref[l:l + 1, :]).astype(BF16)
        ckv = ckv_ref[...].astype(F32)
        ckvn = (ckv * _rms_r(ckv) * gkv_ref[l:l + 1, :]).astype(BF16)
        qraw = _dot_nt(cqn, wq_ref[...])
        kv = _dot(ckvn, wkv_ref[...])
        v_ref[...] = kv.astype(BF16)
        lane = lax.broadcasted_iota(jnp.int32, (tt, HP), 1)
        krs = pltpu.roll(jnp.where(lane < ROPE, kr_ref[...].astype(F32), 0.0), NOPE, 1)
        cs, s1, s2 = c_ref[...], s1_ref[...], s2_ref[...]
        gq, gk = qn_ref[l:l + 1, :], kn_ref[l:l + 1, :]
        for h in range(N_HEADS):
            sl = slice(h * HP, (h + 1) * HP)
            qh, _ = _head_norm(qraw[:, sl], gq)
            q_ref[:, sl] = (_rope(qh, cs, s1, s2) * QK ** -0.5).astype(BF16)
            kh, _ = _head_norm(jnp.where(lane < NOPE, kv[:, sl], krs), gk)
            k_ref[:, sl] = _rope(kh, cs, s1, s2).astype(BF16)

    row = pl.BlockSpec((tt, hw), lambda i: (i, 0))
    return pl.pallas_call(
        body, name=name, grid=(t // tt,),
        in_specs=_mla_specs(tt, lay, cq_norm, ckv_norm, qn, kn, wuq, wukv),
        out_specs=[row, row, row], out_shape=[SDS((t, hw), BF16)] * 3,
        compiler_params=_cparams(("arbitrary",), VMEM_BIG))(
            proj, proj, proj, cq_norm, ckv_norm, qn, kn, wuq, wukv, *tabs)


def _mla_pre_bwd(proj, lay, l, cq_norm, ckv_norm, qn, kn, wuq, wukv, tabs, dq, dk, dv, dproj, name):
    t = proj.shape[0]
    tt = _rt(t)
    hw = N_HEADS * HP
    ql, kvl = cq_norm.shape[1], ckv_norm.shape[1]
    wm = lay["wm"]

    def body(cq_ref, ckv_ref, kr_ref, gq_ref, gkv_ref, qn_ref, kn_ref, wq_ref, wkv_ref, c_ref, s1_ref, s2_ref,
             dq_ref, dk_ref, dv_ref, prev_ref,
             o_ref, dqr_ref, dkv_ref, cqn_ref, ckvn_ref, dgq_ref, dgkv_ref, dqn_ref, dkn_ref):
        @pl.when(pl.program_id(0) == 0)
        def _():
            for ref in (dgq_ref, dgkv_ref, dqn_ref, dkn_ref):
                ref[...] = jnp.zeros_like(ref)

        cq = cq_ref[...].astype(F32)
        cqn = (cq * _rms_r(cq) * gq_ref[l:l + 1, :]).astype(BF16)
        ckv = ckv_ref[...].astype(F32)
        ckvn = (ckv * _rms_r(ckv) * gkv_ref[l:l + 1, :]).astype(BF16)
        cqn_ref[...] = cqn
        ckvn_ref[...] = ckvn
        qraw = _dot_nt(cqn, wq_ref[...])
        kv = _dot(ckvn, wkv_ref[...])
        lane = lax.broadcasted_iota(jnp.int32, (tt, HP), 1)
        krs = pltpu.roll(jnp.where(lane < ROPE, kr_ref[...].astype(F32), 0.0), NOPE, 1)
        cs, s1, s2 = c_ref[...], s1_ref[...], s2_ref[...]
        gq, gk = qn_ref[l:l + 1, :], kn_ref[l:l + 1, :]
        dkr = jnp.zeros((tt, HP), F32)
        dgq = jnp.zeros((1, HP), F32)
        dgk = jnp.zeros((1, HP), F32)
        for h in range(N_HEADS):
            sl = slice(h * HP, (h + 1) * HP)
            qh = qraw[:, sl]
            _, rq = _head_norm(qh, gq)
            dqh, dg = _head_norm_bwd(qh, rq, gq, _rope_t(dq_ref[:, sl] * QK ** -0.5, cs, s1, s2))
            dgq = dgq + dg
            dqr_ref[:, sl] = dqh.astype(BF16)
            kp = jnp.where(lane < NOPE, kv[:, sl], krs)
            _, rk = _head_norm(kp, gk)
            dkp, dg = _head_norm_bwd(kp, rk, gk, _rope_t(dk_ref[:, sl], cs, s1, s2))
            dgk = dgk + dg
            dkv_ref[:, sl] = (jnp.where(lane < NOPE, dkp, 0.0) + dv_ref[:, sl]).astype(BF16)
            dkr = dkr + dkp
        dqn_ref[...] += dgq
        dkn_ref[...] += dgk
        dkr = jnp.where(lane < ROPE, pltpu.roll(dkr, HP - NOPE, 1), 0.0)
        dcq, dg = _rms_bwd(cq, gq_ref[l:l + 1, :], _dot(dqr_ref[...], wq_ref[...]))
        dgq_ref[...] += dg
        dckv, dg = _rms_bwd(ckv, gkv_ref[l:l + 1, :], _dot_nt(dkv_ref[...], wkv_ref[...]))
        dgkv_ref[...] += dg
        o_ref[:, 0:ql] = dcq.astype(BF16)
        o_ref[:, ql:ql + kvl] = dckv.astype(BF16)
        o_ref[:, ql + kvl:ql + kvl + HP] = dkr.astype(BF16)
        o_ref[:, ql + kvl + HP:wm] = jnp.zeros((tt, wm - ql - kvl - HP), BF16)

    row = pl.BlockSpec((tt, hw), lambda i: (i, 0))
    vec = lambda n: pl.BlockSpec((1, n), lambda i: (0, 0))
    return pl.pallas_call(
        body, name=name, grid=(t // tt,),
        in_specs=_mla_specs(tt, lay, cq_norm, ckv_norm, qn, kn, wuq, wukv) + [row, row, row, ANY],
        out_specs=[pl.BlockSpec((tt, wm), lambda i: (i, lay["cq"] // wm)), row, row,
                   pl.BlockSpec((tt, ql), lambda i: (i, 0)), pl.BlockSpec((tt, kvl), lambda i: (i, 0)),
                   vec(ql), vec(kvl), vec(HP), vec(HP)],
        out_shape=[SDS(dproj.shape, dproj.dtype), SDS((t, hw), BF16), SDS((t, hw), BF16), SDS((t, ql), BF16),
                   SDS((t, kvl), BF16), SDS((1, ql), F32), SDS((1, kvl), F32), SDS((1, HP), F32), SDS((1, HP), F32)],
        input_output_aliases={15: 0},
        compiler_params=_cparams(("arbitrary",), VMEM_BIG))(
            proj, proj, proj, cq_norm, ckv_norm, qn, kn, wuq, wukv, *tabs, dq, dk, dv, dproj)


def _comb_fwd(proj, gate_bias, l, yc, ym, name):
    t, d = yc.shape
    tt = _rt(t)

    def body(p_ref, b_ref, yc_ref, ym_ref, y_ref):
        b = b_ref[l]
        g0 = _sig(p_ref[:, 0:d] + b[0:1, :])
        g1 = _sig(p_ref[:, d:2 * d] + b[1:2, :])
        y_ref[...] = (g0 * yc_ref[...] + g1 * ym_ref[...]).astype(BF16)

    row = pl.BlockSpec((tt, d), lambda i: (i, 0))
    return pl.pallas_call(
        body, name=name, grid=(t // tt,),
        in_specs=[pl.BlockSpec((tt, 2 * d), lambda i: (i, 0)), pl.BlockSpec(gate_bias.shape, lambda i: (0, 0, 0)), row, row],
        out_specs=row, out_shape=SDS((t, d), BF16), compiler_params=_cparams(("arbitrary",)))(proj, gate_bias, yc, ym)


def _comb_bwd(proj, gate_bias, l, yc, ym, dy, dp_cols, name):
    t, d = yc.shape
    tt = _rt(t)

    def body(p_ref, b_ref, yc_ref, ym_ref, dy_ref, dyc_ref, dym_ref, dp_ref, db_ref):
        @pl.when(pl.program_id(0) == 0)
        def _():
            db_ref[...] = jnp.zeros_like(db_ref)

        b = b_ref[l]
        dyv = dy_ref[...].astype(F32)
        g0 = _sig(p_ref[:, 0:d] + b[0:1, :])
        g1 = _sig(p_ref[:, d:2 * d] + b[1:2, :])
        dyc_ref[...] = (dyv * g0).astype(BF16)
        dym_ref[...] = (dyv * g1).astype(BF16)
        dg0 = dyv * yc_ref[...] * g0 * (1.0 - g0)
        dg1 = dyv * ym_ref[...] * g1 * (1.0 - g1)
        dp_ref[:, 0:d] = dg0.astype(BF16)
        dp_ref[:, d:2 * d] = dg1.astype(BF16)
        db_ref[0:1, :] += jnp.sum(dg0, axis=0, keepdims=True)
        db_ref[1:2, :] += jnp.sum(dg1, axis=0, keepdims=True)

    row = pl.BlockSpec((tt, d), lambda i: (i, 0))
    wide = pl.BlockSpec((tt, 2 * d), lambda i: (i, 0))
    return pl.pallas_call(
        body, name=name, grid=(t // tt,),
        in_specs=[wide, pl.BlockSpec(gate_bias.shape, lambda i: (0, 0, 0)), row, row, row],
        out_specs=[row, row, wide, pl.BlockSpec((2, d), lambda i: (0, 0))],
        out_shape=[SDS((t, d), BF16), SDS((t, d), BF16), SDS((t, dp_cols), BF16), SDS((2, d), F32)],
        compiler_params=_cparams(("arbitrary",)))(proj, gate_bias, yc, ym, dy)


def _loss_grad(y, target, name):
    t, d = y.shape
    tt = _rt(t)
    nt = t // tt

    def body(y_ref, t_ref, dy_ref, loss_ref, acc_sc):
        i = pl.program_id(0)

        @pl.when(i == 0)
        def _():
            acc_sc[...] = jnp.zeros_like(acc_sc)

        diff = y_ref[...] - t_ref[...]
        dy_ref[...] = diff * (1.0 / d)
        acc_sc[...] += jnp.sum(diff * diff, axis=0, keepdims=True)

        @pl.when(i == nt - 1)
        def _():
            tot = jnp.sum(acc_sc[...], axis=1, keepdims=True) * (0.5 / d)
            loss_ref[...] = jnp.broadcast_to(tot, (1, HP))

    row = pl.BlockSpec((tt, d), lambda i: (i, 0))
    return pl.pallas_call(body, name=name, grid=(nt,), in_specs=[row, row],
                          out_specs=[row, pl.BlockSpec((1, HP), lambda i: (0, 0))],
                          out_shape=[SDS((t, d), F32), SDS((1, HP), F32)],
                          scratch_shapes=[pltpu.VMEM((1, d), F32)],
                          compiler_params=_cparams(("arbitrary",)))(y, target)


def _chunk_mask(tq):
    rows = lax.broadcasted_iota(jnp.int32, (tq, tq), 0) // CHUNK
    cols = lax.broadcasted_iota(jnp.int32, (tq, tq), 1) // CHUNK
    return cols <= rows


NEG = -1e30


def _flash_fwd(q, k, v, name, comm=None):
    t = q.shape[0]
    tq = _rt(t)
    nq = t // tq
    rep = tq // HP

    def body(q_ref, k_ref, v_ref, o_ref, lse_ref):
        qi = pl.program_id(1)

        def one(hh, ki, carry, masked):
            m_prev, l_prev, acc = carry
            lanes = slice(hh * HP, (hh + 1) * HP)
            r0 = pl.multiple_of(ki * tq, tq)
            s = _dot_nt(q_ref[:, lanes], k_ref[pl.ds(r0, tq), lanes])
            if masked:
                s = jnp.where(_chunk_mask(tq), s, NEG)
            m_new = jnp.maximum(m_prev, jnp.max(s, axis=-1, keepdims=True))
            a = jnp.exp(m_prev - m_new)
            p = jnp.exp(s - jnp.tile(m_new, (1, rep)))
            l_new = a * l_prev + jnp.sum(p, axis=-1, keepdims=True)
            acc = a * acc + _dot(p.astype(BF16), v_ref[pl.ds(r0, tq), lanes])
            return m_new, l_new, acc

        def step(ki, carries, masked):
            return tuple(one(hh, ki, carries[hh], masked) for hh in range(FLASH_HEADS))

        init = (jnp.full((tq, HP), NEG, F32), jnp.zeros((tq, HP), F32), jnp.zeros((tq, HP), F32))
        carries = lax.fori_loop(0, qi, lambda ki, cr: step(ki, cr, False), (init,) * FLASH_HEADS)
        for hh, (m_fin, l_fin, acc) in enumerate(step(qi, carries, True)):
            o_ref[:, hh * HP:(hh + 1) * HP] = (acc / l_fin).astype(BF16)
            lse_ref[hh] = m_fin + jnp.log(l_fin)

    wide = FLASH_HEADS * HP
    ng = N_HEADS // FLASH_HEADS
    qspec = pl.BlockSpec((tq, wide), lambda h, qi: (qi, h))
    head = pl.BlockSpec((t, wide), lambda h, qi: (0, h))
    first = lambda: jnp.logical_and(pl.program_id(0) == 0, pl.program_id(1) == 0)
    last = lambda: jnp.logical_and(pl.program_id(0) == ng - 1, pl.program_id(1) == nq - 1)
    mid_at = lambda: pl.program_id(0) * nq + pl.program_id(1) == (3 * ng * nq) // 4
    outs, couts = _call(
        body, comm, first, last, name=name, grid=(ng, nq), mid_at=mid_at, in_specs=[qspec, head, head], args=[q, k, v],
        out_specs=[qspec, pl.BlockSpec((FLASH_HEADS, tq, HP), lambda h, qi: (h, qi, 0))],
        out_shape=[SDS(q.shape, BF16), SDS((N_HEADS, t, HP), F32)], scratch=[], sem=("arbitrary",) * 2)
    return (*outs, couts)


def _flash_bwd(q, k, v, do, o, lse, name, comm=None):
    t = q.shape[0]
    tq = _rt(t)
    nq = t // tq
    rep = tq // HP

    def body(q_ref, k_ref, v_ref, do_ref, o_ref, lse_ref, dq_ref, dk_ref, dv_ref, delta_sc):
        def prep(qi, carry):
            r0 = pl.multiple_of(qi * tq, tq)
            rows = pl.ds(r0, tq)
            dlt = jnp.sum(do_ref[rows, :].astype(F32) * o_ref[rows, :].astype(F32), axis=-1, keepdims=True)
            delta_sc[rows, :] = jnp.broadcast_to(dlt, (tq, HP))
            dq_ref[rows, :] = jnp.zeros((tq, HP), F32)
            return carry

        lax.fori_loop(0, nq, prep, 0)

        def keys(ki, carry0):
            krows = pl.ds(pl.multiple_of(ki * tq, tq), tq)
            kt, vt = k_ref[krows, :], v_ref[krows, :]

            def step(qi, carry, masked):
                dk_acc, dv_acc = carry
                rows = pl.ds(pl.multiple_of(qi * tq, tq), tq)
                qt, dot_ = q_ref[rows, :], do_ref[rows, :]
                s = _dot_nt(qt, kt)
                if masked:
                    s = jnp.where(_chunk_mask(tq), s, NEG)
                p = jnp.exp(s - jnp.tile(lse_ref[rows, :], (1, rep)))
                ds = (p * (_dot_nt(dot_, vt) - jnp.tile(delta_sc[rows, :], (1, rep)))).astype(BF16)
                dv_acc = dv_acc + _dot_tn(p.astype(BF16), dot_)
                dk_acc = dk_acc + _dot_tn(ds, qt)
                dq_ref[rows, :] += _dot(ds, kt)
                return dk_acc, dv_acc

            zero = jnp.zeros((tq, HP), F32)
            carry = step(ki, (zero, zero), True)
            dk_acc, dv_acc = lax.fori_loop(ki + 1, nq, lambda qi, cr: step(qi, cr, False), carry)
            dk_ref[krows, :] = dk_acc
            dv_ref[krows, :] = dv_acc
            return carry0

        lax.fori_loop(0, nq, keys, 0)

    head = pl.BlockSpec((t, HP), lambda h: (0, h))
    outs, couts = _call(
        body, comm, lambda: pl.program_id(0) == 0, lambda: pl.program_id(0) == N_HEADS - 1, name=name, grid=(N_HEADS,),
        in_specs=[head] * 5 + [pl.BlockSpec((None, t, HP), lambda h: (h, 0, 0))], args=[q, k, v, do, o, lse],
        out_specs=[head] * 3, out_shape=[SDS(q.shape, F32)] * 3, scratch=[pltpu.VMEM((t, HP), F32)], sem=("arbitrary",))
    return (*outs, couts)


def _add_cast(gs, rs, place, name):
    n = len(gs)
    _, r, c = gs[0].shape
    tr, tc = _tile2(r, c // 2, 3 * n, 16)
    nct = c // 2 // tc

    def body(place_ref, *refs):
        for a in range(n):
            refs[2 * n + a][...] = (refs[a][...] + refs[n + a][...]).astype(BF16)

    gspec = pl.BlockSpec((None, tr, tc), lambda j, i, k, pr: (j, i, pr[0] * nct + k))
    rspec = pl.BlockSpec((None, tr, tc), lambda j, i, k, pr: (j, i, k))
    grid_spec = pltpu.PrefetchScalarGridSpec(num_scalar_prefetch=1, grid=(NCHIP, r // tr, nct),
                                             in_specs=[gspec] * n + [rspec] * n, out_specs=[rspec] * n)
    return pl.pallas_call(body, name=name, grid_spec=grid_spec, out_shape=[SDS((NCHIP, r, c // 2), BF16)] * n,
                          compiler_params=_cparams(("arbitrary",) * 3, VMEM_BIG))(place, *gs, *rs)


def _sum_chips(ss, qs, place, l, nl, prevs, name):
    n = len(ss)
    _, r, h = ss[0].shape
    tr, tc = _tile2(r, h, 3 * n, 16)
    nct = h // tc

    def body(place_ref, *refs):
        for a in range(n):
            acc = refs[n + a][...].astype(F32)
            for kk in range(NCHIP - 1):
                acc = acc + refs[a][kk].astype(F32)
            refs[-n + a][...] = acc

    in_specs = ([pl.BlockSpec((NCHIP - 1, tr, tc), lambda i, k, pr: (0, i, k))] * n
                + [pl.BlockSpec((None, tr, tc), lambda i, k, pr: (pr[1], i, k))] * n)
    args = [*ss, *qs]
    aliases = {}
    if prevs is not None:
        aliases = {1 + len(args) + a: a for a in range(n)}
        in_specs += [ANY] * n
        args += list(prevs)
    grid_spec = pltpu.PrefetchScalarGridSpec(
        num_scalar_prefetch=1, grid=(r // tr, nct), in_specs=in_specs,
        out_specs=[pl.BlockSpec((None, tr, tc), lambda i, k, pr: (l, i, pr[0] * nct + k))] * n)
    return pl.pallas_call(body, name=name, grid_spec=grid_spec, out_shape=[SDS((nl, r, 2 * h), F32)] * n,
                          input_output_aliases=aliases,
                          compiler_params=_cparams(("arbitrary",) * 2, VMEM_BIG))(place, *args)


def _cast_place(ws, place, name):
    n = len(ws)
    nl, r, c = ws[0].shape
    tr, tc = _tile2(r, c, 2 * n * nl, 16)

    def body(place_ref, *refs):
        for a in range(n * nl):
            refs[n * nl + a][...] = refs[a][...].astype(BF16)

    in_specs = [pl.BlockSpec((None, tr, tc), functools.partial(lambda l, i, k, pr: (l, i, k), l))
                for _ in range(n) for l in range(nl)]
    grid_spec = pltpu.PrefetchScalarGridSpec(
        num_scalar_prefetch=1, grid=(r // tr, c // tc), in_specs=in_specs,
        out_specs=[pl.BlockSpec((None, tr, tc), lambda i, k, pr: (pr[1], i, k))] * (n * nl))
    outs = pl.pallas_call(body, name=name, grid_spec=grid_spec, out_shape=[SDS((NCHIP, r, c), BF16)] * (n * nl),
                          compiler_params=_cparams(("arbitrary",) * 2, VMEM_BIG))(
                              place, *[w for w in ws for _ in range(nl)])
    return [outs[a * nl:(a + 1) * nl] for a in range(n)]


def _adamw(ws, gs, ms, vs, name):
    n = len(ws)
    r, c = ws[0].shape
    tr, tc = _tile2(r, c, 7 * n, 8)
    c1, c2 = 1.0 / (1.0 - B1 ** STEP), 1.0 / (1.0 - B2 ** STEP)

    def body(*refs):
        for a in range(n):
            w, g, m, v = (refs[kk * n + a][...] for kk in range(4))
            m2 = B1 * m + (1.0 - B1) * g
            v2 = B2 * v + (1.0 - B2) * (g * g)
            refs[4 * n + a][...] = -LR * ((m2 * c1) / (jnp.sqrt(v2 * c2) + EPS_ADAM) + WD * w)
            refs[5 * n + a][...] = m2
            refs[6 * n + a][...] = v2

    blk = pl.BlockSpec((tr, tc), lambda i, k: (i, k))
    outs = pl.pallas_call(body, name=name, grid=(r // tr, c // tc), in_specs=[blk] * (4 * n),
                          out_specs=[blk] * (3 * n), out_shape=[SDS((r, c), F32)] * (3 * n),
                          compiler_params=_cparams(("arbitrary",) * 2, VMEM_BIG))(*ws, *gs, *ms, *vs)
    return outs[:n], outs[n:2 * n], outs[2 * n:]


def _place():
    x, y, c = lax.axis_index("x"), lax.axis_index("y"), lax.axis_index("c")
    return x, y, c, [(1 - x, y), (x, 1 - y), (1 - x, 1 - y)]


def _rcopy(src, dst, ssem, rsem, k, dev):
    return pltpu.make_async_remote_copy(src_ref=src, dst_ref=dst, send_sem=ssem.at[k], recv_sem=rsem.at[k],
                                        device_id=dev, device_id_type=MESH)


def _half(ref, lead, cc):
    h = ref.shape[-1] // 2
    return ref.at[(*lead, slice(None), pl.ds(cc * h, h))]


def _per_core(fn):
    c = lax.axis_index("c")
    for cc in (0, 1):
        pl.when(c == cc)(functools.partial(fn, cc))


def _gather_comm(bufs):
    n = len(bufs)

    def plan(couts, ssem, rsem, cc):
        x, y, _, peers = _place()
        me, sib = 2 * x + y, (x, y, 1 - cc)
        send, recv, fwd, recv2 = [], [], [], []
        for a in range(n):
            for kk, (px, py) in enumerate(peers):
                mine = _half(couts[a], (me,), cc)
                got = _half(couts[a], (2 * px + py,), cc)
                other = _half(couts[a], (2 * px + py,), 1 - cc)
                send.append(_rcopy(mine, mine, ssem, rsem, a * 6 + kk, (px, py, cc)))
                recv.append(_rcopy(got, got, ssem, rsem, a * 6 + kk, (px, py, cc)))
                fwd.append(_rcopy(got, got, ssem, rsem, a * 6 + 3 + kk, sib))
                recv2.append(_rcopy(other, other, ssem, rsem, a * 6 + 3 + kk, sib))
        return send, recv, fwd, recv2

    def start(cins, couts, ssem, rsem):
        def go(cc):
            for d in plan(couts, ssem, rsem, cc)[0]:
                d.start()

        _per_core(go)

    def mid(cins, couts, ssem, rsem):
        def go(cc):
            _, recv, fwd, _ = plan(couts, ssem, rsem, cc)
            for dr, df in zip(recv, fwd):
                dr.wait_recv()
                df.start()

        _per_core(go)

    def finish(cins, couts, ssem, rsem):
        def go(cc):
            send, _, fwd, recv2 = plan(couts, ssem, rsem, cc)
            for d in recv2:
                d.wait_recv()
            for d in send + fwd:
                d.wait_send()

        _per_core(go)

    return _Comm(bufs, [SDS(b.shape, b.dtype) for b in bufs], {a: a for a in range(n)}, 6 * n, start, finish, mid)


def _pair_comm(gs):
    n = len(gs)
    halves = [g.shape[-1] // 2 for g in gs]

    def plan(cins, couts, ssem, rsem, cc):
        x, y, _, _ = _place()
        return [_rcopy(cins[a].at[:, :, pl.ds((1 - cc) * halves[a], halves[a])], couts[a], ssem, rsem, a, (x, y, 1 - cc))
                for a in range(n)]

    def start(cins, couts, ssem, rsem):
        def go(cc):
            for d in plan(cins, couts, ssem, rsem, cc):
                d.start()

        _per_core(go)

    def finish(cins, couts, ssem, rsem):
        def go(cc):
            ds = plan(cins, couts, ssem, rsem, cc)
            for d in ds:
                d.wait_recv()
            for d in ds:
                d.wait_send()

        _per_core(go)

    return _Comm(gs, [SDS(g.shape[:-1] + (g.shape[-1] // 2,), g.dtype) for g in gs], {}, n, start, finish)


def _chips_comm(qs):
    n = len(qs)

    def plan(cins, couts, ssem, rsem):
        x, y, c, peers = _place()
        return [_rcopy(cins[a].at[2 * px + py], couts[a].at[kk], ssem, rsem, a * 3 + kk, (px, py, c))
                for a in range(n) for kk, (px, py) in enumerate(peers)]

    def start(cins, couts, ssem, rsem):
        for d in plan(cins, couts, ssem, rsem):
            d.start()

    def finish(cins, couts, ssem, rsem):
        ds = plan(cins, couts, ssem, rsem)
        for d in ds:
            d.wait_recv()
        for d in ds:
            d.wait_send()

    return _Comm(qs, [SDS((NCHIP - 1,) + q.shape[1:], q.dtype) for q in qs], {}, 3 * n, start, finish)


def _share_comm(fs, l):
    n = len(fs)

    def plan(couts, ssem, rsem, cc, which):
        x, y, _, _ = _place()
        out = []
        for a in range(n):
            piece = _half(couts[a], (l,), which)
            out.append(_rcopy(piece, piece, ssem, rsem, a, (x, y, 1 - cc)))
        return out

    def start(cins, couts, ssem, rsem):
        def go(cc):
            for d in plan(couts, ssem, rsem, cc, cc):
                d.start()

        _per_core(go)

    def finish(cins, couts, ssem, rsem):
        def go(cc):
            for d in plan(couts, ssem, rsem, cc, 1 - cc):
                d.wait_recv()
            for d in plan(couts, ssem, rsem, cc, cc):
                d.wait_send()

        _per_core(go)

    return _Comm(fs, [SDS(f.shape, f.dtype) for f in fs], {a: a for a in range(n)}, n, start, finish)


def _spread_comm(buf):
    def plan(cins, couts, ssem, rsem):
        x, y, c, _ = _place()
        me = 4 * x + 2 * y + c
        send, recv = [], []
        for rel in range(1, N_DEV):
            px, py, pc = (1 - x if rel & 4 else x, 1 - y if rel & 2 else y, 1 - c if rel & 1 else c)
            slot = couts[0].at[4 * px + 2 * py + pc]
            send.append(_rcopy(cins[0], couts[0].at[me], ssem, rsem, rel - 1, (px, py, pc)))
            recv.append(_rcopy(slot, slot, ssem, rsem, rel - 1, (px, py, pc)))
        own = pltpu.make_async_copy(cins[0], couts[0].at[me], ssem.at[N_DEV - 1])
        return send, recv, own

    def start(cins, couts, ssem, rsem):
        send, _, own = plan(cins, couts, ssem, rsem)
        for dsc in send + [own]:
            dsc.start()

    def finish(cins, couts, ssem, rsem):
        send, recv, own = plan(cins, couts, ssem, rsem)
        for dsc in recv:
            dsc.wait_recv()
        for dsc in send:
            dsc.wait_send()
        own.wait()

    return _Comm([buf], [SDS((N_DEV,) + buf.shape, buf.dtype)], {}, N_DEV, start, finish)


def _sum_devices(blocks, name):
    _, r, w = blocks.shape

    def body(in_ref, out_ref):
        acc = in_ref[0]
        for dv in range(1, N_DEV):
            acc = acc + in_ref[dv]
        out_ref[...] = acc

    vm = pl.BlockSpec(memory_space=pltpu.VMEM)
    return pl.pallas_call(body, name=name, in_specs=[vm], out_specs=vm, out_shape=SDS((r, w), F32),
                          compiler_params=_cparams(None, VMEM_BIG))(blocks)


def _cols_full(g):
    _, k, ns = g.shape
    return g.transpose(1, 0, 2).reshape(k, NCHIP * ns)


def _cols_shards(w):
    k, n = w.shape
    return w.reshape(k, NCHIP, n // NCHIP).transpose(1, 0, 2)


def _rows_pad(rows, width):
    out = jnp.concatenate([jnp.pad(a, ((0, 0), (0, width - a.shape[1]))) for a in rows], axis=0)
    return jnp.pad(out, ((0, -out.shape[0] % 8), (0, 0)))


def kernel(x, positions, ffn1_norm, ffn1_w_gate, ffn1_w_up, ffn1_w_down, mix_norm, w_in, gate_bias, conv_w, conv_b, conv_ln_g, conv_ln_b, w_conv_out, cq_norm, ckv_norm, w_uq, w_ukv, q_norm, k_norm, w_mla_out, w_out, ffn2_norm, ffn2_w_gate, ffn2_w_up, ffn2_w_down, loss_target, m_ffn1_norm, m_ffn1_w_gate, m_ffn1_w_up, m_ffn1_w_down, m_mix_norm, m_w_in, m_gate_bias, m_conv_w, m_conv_b, m_conv_ln_g, m_conv_ln_b, m_w_conv_out, m_cq_norm, m_ckv_norm, m_w_uq, m_w_ukv, m_q_norm, m_k_norm, m_w_mla_out, m_w_out, m_ffn2_norm, m_ffn2_w_gate, m_ffn2_w_up, m_ffn2_w_down, v_ffn1_norm, v_ffn1_w_gate, v_ffn1_w_up, v_ffn1_w_down, v_mix_norm, v_w_in, v_gate_bias, v_conv_w, v_conv_b, v_conv_ln_g, v_conv_ln_b, v_w_conv_out, v_cq_norm, v_ckv_norm, v_w_uq, v_w_ukv, v_q_norm, v_k_norm, v_w_mla_out, v_w_out, v_ffn2_norm, v_ffn2_w_gate, v_ffn2_w_up, v_ffn2_w_down):
    names = ["ffn1_norm", "ffn1_w_gate", "ffn1_w_up", "ffn1_w_down", "mix_norm", "w_in", "gate_bias", "conv_w",
             "conv_b", "conv_ln_g", "conv_ln_b", "w_conv_out", "cq_norm", "ckv_norm", "w_uq", "w_ukv", "q_norm",
             "k_norm", "w_mla_out", "w_out", "ffn2_norm", "ffn2_w_gate", "ffn2_w_up", "ffn2_w_down"]
    env = dict(locals())
    turned = ("ffn1_w_gate", "ffn1_w_up", "ffn2_w_gate", "ffn2_w_up", "w_in", "w_uq")
    view = lambda nm, a: jnp.swapaxes(a, 1, 2) if nm in turned else a
    wts = {nm: view(nm, env[nm]) for nm in names}
    mom = {nm: view(nm, env["m_" + nm]) for nm in names}
    var = {nm: view(nm, env["v_" + nm]) for nm in names}

    t, d = x.shape[1], x.shape[2]
    nl = ffn1_norm.shape[0]
    cc = conv_b.shape[1]
    ql, kvl = cq_norm.shape[1], ckv_norm.shape[1]
    vw = N_HEADS * V_DIM
    hw = N_HEADS * HP
    lay = {"a": 2 * d, "cq": 2 * d + 2 * cc, "ckv": 2 * d + 2 * cc + ql, "kr": 2 * d + 2 * cc + ql + kvl,
           "wm": ql + kvl + 2 * HP}
    dp = lay["cq"] + lay["wm"]
    nat_g = 2 * cc + ql + kvl + ROPE
    assert lay["cq"] % lay["wm"] == 0 and lay["cq"] % ql == 0 and lay["ckv"] % kvl == 0 and lay["a"] % (2 * cc) == 0
    assert cc % HP == 0 and d % HP == 0 and t % (2 * CHUNK) == 0 and w_in.shape[2] * NCHIP == nat_g + 2 * d
    assert nl == 2

    x0, target = x[0], loss_target[0]
    chip = 2 * lax.axis_index("x") + lax.axis_index("y")
    place = jnp.stack([lax.axis_index("c"), chip]).astype(jnp.int32)

    inv_freq = ROPE_THETA ** (-jnp.arange(0, ROPE, 2, dtype=F32) / ROPE)
    ang = positions[0].astype(F32)[:, None] * inv_freq
    cos, sin, z = jnp.cos(ang), jnp.sin(ang), jnp.zeros((t, ROPE // 2), F32)
    tabs = (jnp.concatenate([jnp.ones((t, NOPE), F32), cos, cos, jnp.zeros((t, HP - QK), F32)], axis=1),
            jnp.concatenate([jnp.zeros((t, NOPE), F32), -sin, z, jnp.zeros((t, HP - QK), F32)], axis=1),
            jnp.concatenate([jnp.zeros((t, NOPE), F32), z, sin, jnp.zeros((t, HP - QK), F32)], axis=1))

    big = ["ffn1_w_gate", "ffn1_w_up", "ffn1_w_down", "ffn2_w_gate", "ffn2_w_up", "ffn2_w_down",
           "w_in", "w_conv_out", "w_uq", "w_ukv", "w_mla_out", "w_out"]
    ffn1_w, ffn2_w, mix_w = big[0:3], big[3:6], big[6:]
    bufs, like = {}, {}
    for nm in big:
        like.setdefault(wts[nm].shape, []).append(nm)
    for gi, grp in enumerate(like.values()):
        for nm, per_layer in zip(grp, _cast_place([wts[nm] for nm in grp], place, f"cast_place_{gi}")):
            for l in range(nl):
                bufs[nm, l] = per_layer[l]

    def gather(keys):
        return _gather_comm([bufs[key] for key in keys])

    def landed(keys, outs):
        for key, o in zip(keys, outs):
            bufs[key] = o

    chunk = lambda nms, l: [(nm, l) for nm in nms]
    first_keys = chunk(ffn1_w, 0)
    landed(first_keys, _comm_call(gather(first_keys), "gather_first"))
    ride = {("ffn1", 0): chunk(mix_w, 0), ("flash", 0): chunk(ffn2_w, 0), ("ffn2", 0): chunk(ffn1_w, 1),
            ("ffn1", 1): chunk(mix_w, 1), ("flash", 1): chunk(ffn2_w, 1)}

    def riding(kind, l):
        keys = ride.get((kind, l))
        return keys or [], (None if keys is None else gather(keys))

    def mixer_weights(l):
        w_in_nat = bufs["w_in", l].reshape(nat_g + 2 * d, d)
        w_in_k = jnp.concatenate([w_in_nat[nat_g:], w_in_nat[:nat_g], jnp.zeros((dp - nat_g - 2 * d, d), BF16)], axis=0)
        w_uq_k = jnp.pad(bufs["w_uq", l].reshape(N_HEADS, QK, ql), ((0, 0), (0, HP - QK), (0, 0))).reshape(hw, ql)
        w_mo_k = jnp.pad(_cols_full(bufs["w_mla_out", l]).reshape(N_HEADS, V_DIM, d),
                         ((0, 0), (HP - V_DIM, 0), (0, 0))).reshape(hw, d)
        return dict(w_in=w_in_k, w_co=_cols_full(bufs["w_conv_out", l]), w_uq=w_uq_k,
                    w_ukv=_cols_full(bufs["w_ukv", l]), w_mo=w_mo_k, w_out=bufs["w_out", l].reshape(d, d))

    qn_k = jnp.pad(q_norm, ((0, 0), (0, HP - QK)))
    kn_k = jnp.pad(k_norm, ((0, 0), (0, HP - QK)))
    small = _rows_pad([gate_bias.reshape(nl * 2, d // NCHIP), conv_w.reshape(nl * KW, cc // NCHIP)], d)

    saved, mixw = [], []
    xc = x0
    for l in range(nl):
        keys, comm = riding("ffn1", l)
        if l == 0:
            comm = _join(comm, _spread_comm(small))
        x1, gt1, up1, got = _ffn_fwd(xc, ffn1_norm, l, *[bufs[nm, l] for nm in ffn1_w], f"ffn1_fwd_{l}", comm)
        landed(keys, got)
        if l == 0:
            everyone = got[-1][0::2]
            gb_k = everyone[:, :nl * 2, :d // NCHIP].reshape(NCHIP, nl, 2, d // NCHIP).transpose(1, 2, 0, 3).reshape(nl, 2, d)
            cw = everyone[:, nl * 2:nl * 2 + nl * KW, :cc // NCHIP].reshape(NCHIP, nl, KW, cc // NCHIP)
            cw_k = jnp.pad(cw.transpose(1, 2, 0, 3).reshape(nl, KW, cc), ((0, 0), (0, CPAD - KW), (0, 0)))
        mw = mixer_weights(l)
        mixw.append(mw)
        hm = _rms_fwd(x1, mix_norm, l, f"mix_norm_{l}")
        proj = _mm(hm, mw["w_in"], "nt", name=f"proj_{l}", out_dtype=BF16)
        uc = _dwconv_fwd(proj, cw_k, conv_b, l, lay["a"], cc, f"dwconv_{l}")
        sc = _ln_silu(uc, conv_ln_g, conv_ln_b, l, f"conv_ln_{l}")
        yc = _mm(sc, mw["w_co"], "nn", name=f"conv_out_{l}", out_dtype=BF16)
        q, k, kv = _mla_pre_fwd(proj, lay, l, cq_norm, ckv_norm, qn_k, kn_k, mw["w_uq"], mw["w_ukv"], tabs, f"mla_pre_{l}")
        keys, comm = riding("flash", l)
        o, lse, got = _flash_fwd(q, k, kv, f"flash_{l}", comm)
        landed(keys, got)
        ym = _mm(o, mw["w_mo"], "nn", name=f"mla_out_{l}", out_dtype=BF16)
        yv = _comb_fwd(proj, gb_k, l, yc, ym, f"combine_{l}")
        x2 = _mm(yv, mw["w_out"], "nn", name=f"mix_out_{l}", res=x1)
        keys, comm = riding("ffn2", l)
        x3, gt2, up2, got = _ffn_fwd(x2, ffn2_norm, l, *[bufs[nm, l] for nm in ffn2_w], f"ffn2_fwd_{l}", comm,
                                     0.9)
        landed(keys, got)
        saved.append(dict(x0=xc, x1=x1, gt1=gt1, up1=up1, hm=hm, proj=proj, uc=uc, sc=sc, yc=yc, q=q, k=k, kv=kv,
                          o=o, lse=lse, ym=ym, yv=yv, x2=x2, gt2=gt2, up2=up2))
        xc = x3
    dx, loss_row = _loss_grad(xc, target, "loss")

    gw = {nm: [None] * nl for nm in names}
    red = {}

    def shard_groups(nms, l):
        same = {}
        for nm in nms:
            same.setdefault(gw[nm][l].shape, []).append(nm)
        return list(same.values())

    def add_parts(nms, l, sib_part, tag):
        qb = {}
        for gi, grp in enumerate(shard_groups(nms, l)):
            outs = _add_cast([gw[nm][l] for nm in grp], [sib_part[nm] for nm in grp], place, f"rs_add_{tag}_{gi}")
            qb.update(zip(grp, outs))
        return qb

    def sum_parts(nms, l, qb, parts, tag):
        for gi, grp in enumerate(shard_groups(nms, l)):
            prevs = [red[nm] for nm in grp] if grp[0] in red else None
            outs = _sum_chips([parts[nm] for nm in grp], [qb[nm] for nm in grp], place, l, nl, prevs, f"rs_sum_{tag}_{gi}")
            red.update(zip(grp, outs))

    pair = lambda nms, l: _pair_comm([gw[nm][l] for nm in nms])
    chips = lambda nms, qb: _chips_comm([qb[nm] for nm in nms])
    share = lambda nms, l: _share_comm([red[nm] for nm in nms], l)

    def split(joined, got):
        return [got[o0:o1] for o0, o1 in joined.spans]

    mix_rest = [nm for nm in mix_w if nm != "w_in"]
    mix_parts = [["w_out", "w_conv_out"], ["w_mla_out", "w_ukv", "w_uq"], []]

    for l in reversed(range(nl)):
        s, mw = saved[l], mixw[l]
        riding_rs = l == 0

        def ffn_back(tag, xin, gains, gts, ups, dout, comm=None):
            wg, wu, wd = (bufs[f"{tag}_w_{p}", l] for p in ("gate", "up", "down"))
            dxi, dgain, dgt, dup, act, hb, dob, got = _ffn_bwd(xin, gains, l, wg, wu, wd, gts, ups, dout,
                                                              f"{tag}_bwd_{l}", comm)
            for p, lhs, rhs in zip(("gate", "up", "down"), (dgt, dup, act), (hb, hb, dob)):
                gw[f"{tag}_w_{p}"][l] = _mm(lhs, rhs, "tn", name=f"{tag}_dw_{p}_{l}", a_g=True)
            gw[f"{tag}_norm"][l] = dgain
            return dxi, got

        dx2, got = ffn_back("ffn2", s["x2"], ffn2_norm, s["gt2"], s["up2"], dx, pair(big, 1) if riding_rs else None)
        if riding_rs:
            qb1 = add_parts(big, 1, dict(zip(big, got)), "l1")
        dyv = _mm(dx2, mw["w_out"], "nt", name=f"mix_out_dy_{l}", out_dtype=BF16)
        gw["w_out"][l] = _mm(s["yv"], dx2, "tn", name=f"mix_out_dw_{l}").reshape(NCHIP, d // NCHIP, d)
        dyc, dym, dproj, dgb = _comb_bwd(s["proj"], gb_k, l, s["yc"], s["ym"], dyv, dp, f"combine_bwd_{l}")
        gw["gate_bias"][l] = dgb
        dsc = _mm(dyc, mw["w_co"], "nt", name=f"conv_out_ds_{l}")
        gw["w_conv_out"][l] = _cols_shards(_mm(s["sc"], dyc, "tn", name=f"conv_out_dw_{l}"))
        duc, gw["conv_ln_g"][l], gw["conv_ln_b"][l], gw["conv_b"][l] = _ln_silu_bwd(
            s["uc"], conv_ln_g, conv_ln_b, l, dsc, f"conv_ln_bwd_{l}")
        du, dcw = _dwconv_bwd(duc, s["proj"], cw_k, l, lay["a"], cc, f"dwconv_bwd_{l}")
        gw["conv_w"][l] = dcw[:KW]
        dproj = _glu_bwd(du, s["proj"], dproj, lay["a"], cc, f"glu_bwd_{l}")
        do = _mm(dym, mw["w_mo"], "nt", name=f"mla_out_do_{l}", out_dtype=BF16)
        dwmo = _mm(s["o"], dym, "tn", name=f"mla_out_dw_{l}").reshape(N_HEADS, HP, d)[:, HP - V_DIM:].reshape(vw, d)
        gw["w_mla_out"][l] = _cols_shards(dwmo)
        comm = _join(chips(big, qb1), pair(ffn2_w, 0)) if riding_rs else None
        dq, dk, dv, got = _flash_bwd(s["q"], s["k"], s["kv"], do, s["o"], s["lse"], f"flash_bwd_{l}", comm)
        if riding_rs:
            parts1, sib_f2 = split(comm, got)
            sum_parts(big, 1, qb1, dict(zip(big, parts1)), "l1")
            qb_f2 = add_parts(ffn2_w, 0, dict(zip(ffn2_w, sib_f2)), "f2")
        dproj, dqr, dkv, cqn, ckvn, gw["cq_norm"][l], gw["ckv_norm"][l], dqn, dkn = _mla_pre_bwd(
            s["proj"], lay, l, cq_norm, ckv_norm, qn_k, kn_k, mw["w_uq"], mw["w_ukv"], tabs, dq, dk, dv, dproj,
            f"mla_pre_bwd_{l}")
        gw["q_norm"][l], gw["k_norm"][l] = dqn[:, :QK], dkn[:, :QK]
        dwuq = _mm(dqr, cqn, "tn", name=f"uq_dw_{l}").reshape(N_HEADS, HP, ql)[:, :QK]
        gw["w_uq"][l] = dwuq.reshape(NCHIP, N_HEADS * QK // NCHIP, ql)
        gw["w_ukv"][l] = _cols_shards(_mm(ckvn, dkv, "tn", name=f"ukv_dw_{l}"))
        dhm = _mm(dproj, mw["w_in"], "nn", name=f"proj_dh_{l}")
        dwin = _mm(dproj, s["hm"], "tn", name=f"proj_dw_{l}")
        gw["w_in"][l] = jnp.concatenate([dwin[2 * d:2 * d + nat_g], dwin[:2 * d]], axis=0).reshape(
            NCHIP, (nat_g + 2 * d) // NCHIP, d)
        dx1, gw["mix_norm"][l], got = _rms_back(s["x1"], mix_norm, l, dhm, dx2, f"mix_norm_bwd_{l}",
                                                pair(["w_in"], 0) if riding_rs else None)
        if not riding_rs:
            dx, _ = ffn_back("ffn1", s["x0"], ffn1_norm, s["gt1"], s["up1"], dx1)
            continue
        qb_win = add_parts(["w_in"], 0, {"w_in": got[0]}, "win")
        comm = _join(share(big, 1), chips(ffn2_w, qb_f2), chips(["w_in"], qb_win), pair(mix_rest, 0))
        wg, wu, wd = (bufs[f"ffn1_w_{p}", l] for p in ("gate", "up", "down"))
        dx, dgain, dgt, dup, act, hb, dob, got = _ffn_bwd(s["x0"], ffn1_norm, l, wg, wu, wd, s["gt1"], s["up1"], dx1,
                                                         f"ffn1_bwd_{l}", comm)
        gw["ffn1_norm"][l] = dgain
        shared, parts_f2, parts_win, sib_mx = split(comm, got)
        red.update(zip(big, shared))
        sum_parts(ffn2_w, 0, qb_f2, dict(zip(ffn2_w, parts_f2)), "f2")
        sum_parts(["w_in"], 0, qb_win, {"w_in": parts_win[0]}, "win")
        qb_mx = add_parts(mix_rest, 0, dict(zip(mix_rest, sib_mx)), "mx")
        vec_names = ["ffn1_norm", "mix_norm", "ffn2_norm", "conv_b", "conv_ln_g", "conv_ln_b", "cq_norm", "ckv_norm",
                     "q_norm", "k_norm"]
        rows = [jnp.concatenate(gw[nm], axis=0) for nm in vec_names]
        rows += [jnp.concatenate(gw["gate_bias"], axis=0), jnp.concatenate(gw["conv_w"], axis=0), loss_row]
        parts_mx, f1 = {}, [f"ffn1_w_{p}" for p in ("gate", "up", "down")]
        qb_f1 = {}
        for part, (lhs, rhs) in enumerate(zip((dgt, dup, act), (hb, hb, dob))):
            cms = [_spread_comm(_rows_pad(rows, d))] if part == 0 else [pair(f1[part - 1:part], 0)]
            if mix_parts[part]:
                cms.append(chips(mix_parts[part], qb_mx))
            if part == 2:
                cms.append(chips(f1[0:1], qb_f1))
            comm = _join(*cms)
            gw[f1[part]][l], got = _mm(lhs, rhs, "tn", name=f"{f1[part]}_dw_{l}", a_g=True, comm=comm)
            pieces = split(comm, got)
            if part == 0:
                small_blocks = pieces[0][0]
            else:
                qb_f1.update(add_parts(f1[part - 1:part], 0, dict(zip(f1[part - 1:part], pieces[0])), f"f1{part - 1}"))
            if mix_parts[part]:
                parts_mx.update(zip(mix_parts[part], pieces[1]))
            if part == 2:
                sum_parts(f1[0:1], 0, qb_f1, dict(zip(f1[0:1], pieces[-1])), "f10")
        sum_parts(mix_rest, 0, qb_mx, parts_mx, "mx")

    sib = dict(zip(f1[2:3], _comm_call(pair(f1[2:3], 0), "rs_pair")))
    qb_f1.update(add_parts(f1[2:3], 0, sib, "f12"))
    parts_f1 = dict(zip(f1[1:3], _comm_call(chips(f1[1:3], qb_f1), "rs_chips")))
    sum_parts(f1[1:3], 0, qb_f1, parts_f1, "f1")
    red = dict(zip(big, _comm_call(share(big, 0), "rs_share")))
    grads = {nm: g.reshape(wts[nm].shape) for nm, g in red.items()}

    total = _sum_devices(small_blocks, "allreduce_small")
    r0 = 0
    for nm in vec_names:
        grads[nm] = total[r0:r0 + nl, :wts[nm].shape[1]]
        r0 += nl
    gb_all = total[r0:r0 + 2 * nl, :d].reshape(nl, 2, NCHIP, d // NCHIP)
    r0 += 2 * nl
    cw_all = total[r0:r0 + KW * nl, :cc].reshape(nl, KW, NCHIP, cc // NCHIP)
    r0 += KW * nl
    grads["gate_bias"] = lax.dynamic_index_in_dim(gb_all, chip, axis=2, keepdims=False)
    grads["conv_w"] = lax.dynamic_index_in_dim(cw_all, chip, axis=2, keepdims=False)
    loss = total[r0, 0]

    delta, new_m, new_v = {}, {}, {}
    by_shape = {}
    for nm in names:
        shp = wts[nm].shape
        by_shape.setdefault((shp[0] * (shp[1] if len(shp) == 3 else 1), shp[-1]), []).append(nm)
    for gi, (shp2, grp) in enumerate(by_shape.items()):
        to2 = lambda a: a.reshape(shp2)
        ds_, ms_, vs_ = _adamw([to2(wts[nm]) for nm in grp], [to2(grads[nm]) for nm in grp],
                               [to2(mom[nm]) for nm in grp], [to2(var[nm]) for nm in grp], f"adamw_{gi}")
        for nm, dd, mm_, vv in zip(grp, ds_, ms_, vs_):
            delta[nm], new_m[nm], new_v[nm] = (a.reshape(wts[nm].shape) for a in (dd, mm_, vv))

    return (loss, dx[None], *[view(nm, grads[nm]) for nm in names], *[view(nm, delta[nm]) for nm in names],
            *[view(nm, new_m[nm]) for nm in names], *[view(nm, new_v[nm]) for nm in names])
```

```python
import functools

import jax
import jax.numpy as jnp
from jax import lax
from jax.experimental import pallas as pl
from jax.experimental.pallas import tpu as pltpu

F32, BF16 = jnp.float32, jnp.bfloat16
SDS = jax.ShapeDtypeStruct
MESH = pl.DeviceIdType.MESH
ANY = pl.BlockSpec(memory_space=pl.ANY)

NCHIP = 4
N_DEV = 8
N_HEADS, NOPE, ROPE, V_DIM = 8, 64, 32, 64
QK = NOPE + ROPE
HP = 128
CHUNK = 64
KW = 31
CPAD = 32
ROPE_THETA = 10000.0
EPS = 1e-6
LR, B1, B2, EPS_ADAM, WD, STEP = 0.001, 0.9, 0.999, 1e-08, 0.01, 10
VMEM_BIG = 48 << 20
FFN_PARTS = 2
FLASH_HEADS = 2


def _cparams(sem=None, vmem=None):
    kw = {}
    if sem is not None:
        kw["dimension_semantics"] = sem
    if vmem is not None:
        kw["vmem_limit_bytes"] = vmem
    return pltpu.CompilerParams(**kw)


def _rt(t):
    return min(512, t // 2)


def _pick(n, cands):
    for c in cands:
        if c <= n and n % c == 0:
            return c
    return n


def _tile2(r, c, nblocks, row_mult):
    tr, tc = r, c
    while 2 * nblocks * tr * tc * 4 > VMEM_BIG // 2 and tr % (2 * row_mult) == 0:
        tr //= 2
    while 2 * nblocks * tr * tc * 4 > VMEM_BIG // 2 and tc % 256 == 0:
        tc //= 2
    return tr, tc


def _sig(v):
    return 1.0 / (1.0 + jnp.exp(-v))


def _rms_r(v):
    return lax.rsqrt(jnp.mean(v * v, axis=-1, keepdims=True) + EPS)


def _rms_bwd(xv, g, dy):
    r = _rms_r(xv)
    xh = xv * r
    dg = jnp.sum(dy * xh, axis=0, keepdims=True)
    dxh = dy * g
    dx = r * (dxh - xh * jnp.mean(dxh * xh, axis=-1, keepdims=True))
    return dx, dg


def _dot(a, b):
    return jnp.dot(a, b, preferred_element_type=F32)


def _dot_nt(a, b):
    return lax.dot_general(a, b, (((1,), (1,)), ((), ())), preferred_element_type=F32)


def _dot_tn(a, b):
    return lax.dot_general(a, b, (((0,), (0,)), ((), ())), preferred_element_type=F32)


class _Comm:
    def __init__(self, ins, out_shapes, aliases, nsem, start, finish, mid=None, parts=None):
        self.ins, self.out_shapes, self.aliases, self.nsem = list(ins), list(out_shapes), dict(aliases), nsem
        self.start, self.finish, self.mid = start, finish, mid
        self.parts = parts


def _call(body, comm, first, last, *, name, grid, in_specs, args, out_specs, out_shape, scratch, sem, mid_at=None, step=None, nsteps=0):
    in_specs, args, out_specs, out_shape, scratch = map(list, (in_specs, args, out_specs, out_shape, scratch))
    n_in, n_out, n_scr = len(args), len(out_shape), len(scratch)
    aliases = {}
    kern = body
    if comm is not None:
        nci, nco = len(comm.ins), len(comm.out_shapes)
        o0 = n_in + nci
        s0 = o0 + n_out + nco

        def kern(*refs):
            cins, couts = refs[n_in:o0], refs[o0 + n_out:s0]
            ssem, rsem = refs[s0 + n_scr:]
            if comm.parts is not None and step is not None and comm.parts[0] <= nsteps:
                for part in range(comm.parts[0]):
                    pl.when(step() == part)(functools.partial(comm.parts[1], part, cins, couts, ssem, rsem))
            else:
                pl.when(first())(lambda: comm.start(cins, couts, ssem, rsem))
            if comm.mid is not None and mid_at is not None:
                pl.when(mid_at())(lambda: comm.mid(cins, couts, ssem, rsem))
            body(*refs[:n_in], *refs[o0:o0 + n_out], *refs[s0:s0 + n_scr])

            def end():
                if comm.mid is not None and mid_at is None:
                    comm.mid(cins, couts, ssem, rsem)
                comm.finish(cins, couts, ssem, rsem)

            pl.when(last())(end)

        in_specs += [ANY] * nci
        args += comm.ins
        out_specs += [ANY] * nco
        out_shape += comm.out_shapes
        scratch += [pltpu.SemaphoreType.DMA((comm.nsem,))] * 2
        aliases = {n_in + i: n_out + o for i, o in comm.aliases.items()}
    outs = pl.pallas_call(kern, name=name, grid=grid, in_specs=in_specs, out_specs=out_specs, out_shape=out_shape,
                          scratch_shapes=scratch, input_output_aliases=aliases,
                          compiler_params=_cparams(sem, VMEM_BIG))(*args)
    return outs[:n_out], outs[n_out:]


class _SemView:
    def __init__(self, sems, base):
        self.sems, self.base = sems, base

    @property
    def at(self):
        return self

    def __getitem__(self, k):
        return self.sems.at[self.base + k]


def _join(*comms):
    ins, outs, aliases, spans, nsem = [], [], {}, [], 0
    for cm in comms:
        aliases.update({len(ins) + i: len(outs) + o for i, o in cm.aliases.items()})
        spans.append((len(ins), len(ins) + len(cm.ins), len(outs), len(outs) + len(cm.out_shapes), nsem))
        ins += cm.ins
        outs += cm.out_shapes
        nsem += cm.nsem

    def run(which, cins, couts, ssem, rsem):
        for cm, (i0, i1, o0, o1, base) in zip(comms, spans):
            if getattr(cm, which) is not None:
                getattr(cm, which)(cins[i0:i1], couts[o0:o1], _SemView(ssem, base), _SemView(rsem, base))

    joined = _Comm(ins, outs, aliases, nsem, functools.partial(run, "start"), functools.partial(run, "finish"),
                   functools.partial(run, "mid") if any(cm.mid is not None for cm in comms) else None)
    joined.spans = [(o0, o1) for _, _, o0, o1, _ in spans]
    return joined


def _comm_call(comm, name):
    once = lambda: pl.program_id(0) == 0
    return _call(lambda: None, comm, once, once, name=name, grid=(1,), in_specs=[], args=[], out_specs=[],
                 out_shape=[], scratch=[], sem=("arbitrary",))[1]


def _mm(a, b, mode, *, name, out_dtype=F32, res=None, a_g=False, comm=None):
    a2 = a.shape[1:] if a_g else a.shape
    if mode == "nn":
        (m, k), (k2, n) = a2, b.shape
    elif mode == "nt":
        (m, k), (n, k2) = a2, b.shape
    else:
        (k, m), (k2, n) = a2, b.shape
    assert k == k2, (name, a.shape, b.shape)
    g = a.shape[0] if a_g else 1
    tm = m if m <= 768 else _pick(m, (768, 512, 256, 128))
    tn = n if n <= 1280 else _pick(n, (1280, 1024, 768, 512, 256, 128))
    tk = k if k <= 1280 else _pick(k, (1280, 1024, 768, 512, 256, 128))
    fits = lambda kk, nn: 2 * (kk * tm * a.dtype.itemsize + kk * nn * b.dtype.itemsize + tm * nn * 4) <= VMEM_BIG - (8 << 20)
    if fits(k, n):
        tn, tk = n, k
    elif fits(k, tn):
        tk = k
    nk = k // tk
    dn = {"nn": (((1,), (0,)), ((), ())), "nt": (((1,), (1,)), ((), ())), "tn": (((0,), (0,)), ((), ()))}[mode]

    def body(*refs):
        a_ref, b_ref = refs[0], refs[1]
        res_ref = refs[2] if res is not None else None
        o_ref = refs[3] if res is not None else refs[2]
        p = lax.dot_general(a_ref[...].astype(BF16), b_ref[...].astype(BF16), dn, preferred_element_type=F32)

        def fin(v):
            if res_ref is not None:
                v = v + res_ref[...]
            o_ref[...] = v.astype(out_dtype)

        if nk == 1:
            fin(p)
        else:
            acc_ref = refs[-1]
            kk = pl.program_id(3)

            @pl.when(kk == 0)
            def _():
                acc_ref[...] = p

            @pl.when(kk > 0)
            def _():
                acc_ref[...] += p

            @pl.when(kk == nk - 1)
            def _():
                fin(acc_ref[...])

    a_block, a_idx = ((tk, tm), lambda gg, i, j, kk: (kk, i)) if mode == "tn" else ((tm, tk), lambda gg, i, j, kk: (i, kk))
    if a_g:
        a_spec = pl.BlockSpec((None,) + a_block, lambda gg, i, j, kk: (gg,) + a_idx(gg, i, j, kk))
    else:
        a_spec = pl.BlockSpec(a_block, a_idx)
    if mode == "nt":
        b_spec = pl.BlockSpec((tn, tk), lambda gg, i, j, kk: (j, kk))
    else:
        b_spec = pl.BlockSpec((tk, tn), lambda gg, i, j, kk: (kk, j))
    in_specs, args = [a_spec, b_spec], [a, b]
    if res is not None:
        in_specs.append(pl.BlockSpec((tm, tn), lambda gg, i, j, kk: (i, j)))
        args.append(res)
    if a_g:
        out_spec, oshape = pl.BlockSpec((None, tm, tn), lambda gg, i, j, kk: (gg, i, j)), (g, m, n)
    else:
        out_spec, oshape = pl.BlockSpec((tm, tn), lambda gg, i, j, kk: (i, j)), (m, n)
    grid = (g, m // tm, n // tn, nk)

    def at(corner):
        hit = pl.program_id(0) == corner[0]
        for ax in range(1, 4):
            hit = jnp.logical_and(hit, pl.program_id(ax) == corner[ax])
        return hit

    outs, couts = _call(body, comm, lambda: at((0, 0, 0, 0)), lambda: at(tuple(dim - 1 for dim in grid)), name=name,
                        grid=grid, in_specs=in_specs, args=args, out_specs=[out_spec], out_shape=[SDS(oshape, out_dtype)],
                        scratch=[pltpu.VMEM((tm, tn), F32)] if nk > 1 else [], sem=("arbitrary",) * 4)
    return outs[0] if comm is None else (outs[0], couts)


def _ffn_fwd(x, gains, l, wg, wu, wd, name, comm=None, mid_frac=0.75):
    t, d = x.shape
    f = wg.shape[-2]
    tt = _rt(t)

    def body(x_ref, g_ref, wg_ref, wu_ref, wd_ref, o_ref, gt_ref, up_ref, h_sc, acc_sc):
        j = pl.program_id(1)

        @pl.when(j == 0)
        def _():
            xv = x_ref[...]
            h_sc[...] = (xv * _rms_r(xv) * g_ref[l:l + 1, :]).astype(BF16)
            acc_sc[...] = jnp.zeros_like(acc_sc)

        for part in range(FFN_PARTS):
            rows = pl.ds(part * (tt // FFN_PARTS), tt // FFN_PARTS)
            h = h_sc[rows, :]
            gt = _dot_nt(h, wg_ref[...]).astype(BF16)
            up = _dot_nt(h, wu_ref[...]).astype(BF16)
            gt_ref[rows, :] = gt
            up_ref[rows, :] = up
            acc_sc[rows, :] += _dot(gt * _sig(gt) * up, wd_ref[...])

        @pl.when(j == NCHIP - 1)
        def _():
            o_ref[...] = x_ref[...] + 0.5 * acc_sc[...]

    wspec = pl.BlockSpec((None, f, d), lambda i, j: (j, 0, 0))
    row = pl.BlockSpec((tt, d), lambda i, j: (i, 0))
    sh = pl.BlockSpec((None, tt, f), lambda i, j: (j, i, 0))
    ni = t // tt
    first = lambda: jnp.logical_and(pl.program_id(0) == 0, pl.program_id(1) == 0)
    last = lambda: jnp.logical_and(pl.program_id(0) == ni - 1, pl.program_id(1) == NCHIP - 1)
    mid_at = lambda: pl.program_id(0) * NCHIP + pl.program_id(1) == int(mid_frac * ni * NCHIP)
    outs, couts = _call(
        body, comm, first, last, name=name, grid=(ni, NCHIP), mid_at=mid_at,
        in_specs=[row, pl.BlockSpec(gains.shape, lambda i, j: (0, 0)), wspec, wspec, wspec], args=[x, gains, wg, wu, wd],
        out_specs=[row, sh, sh], out_shape=[SDS((t, d), F32), SDS((NCHIP, t, f), BF16), SDS((NCHIP, t, f), BF16)],
        scratch=[pltpu.VMEM((tt, d), BF16), pltpu.VMEM((tt, d), F32)], sem=("arbitrary", "arbitrary"))
    return (*outs, couts)


def _ffn_bwd(x, gains, l, wg, wu, wd, gts, ups, dout, name, comm=None):
    t, d = x.shape
    f = wg.shape[-2]
    tt = _rt(t)

    def body(x_ref, g_ref, wg_ref, wu_ref, wd_ref, gt_ref, up_ref, do_ref,
             dx_ref, dg_ref, dgt_ref, dup_ref, act_ref, h_ref, dob_ref, dh_sc):
        i, j = pl.program_id(0), pl.program_id(1)

        @pl.when(j == 0)
        def _():
            xv = x_ref[...]
            h_ref[...] = (xv * _rms_r(xv) * g_ref[l:l + 1, :]).astype(BF16)
            dob_ref[...] = (0.5 * do_ref[...]).astype(BF16)
            dh_sc[...] = jnp.zeros_like(dh_sc)

        @pl.when(jnp.logical_and(i == 0, j == 0))
        def _():
            dg_ref[...] = jnp.zeros_like(dg_ref)

        for part in range(FFN_PARTS):
            rows = pl.ds(part * (tt // FFN_PARTS), tt // FFN_PARTS)
            dact = _dot_nt(dob_ref[rows, :], wd_ref[...]).astype(BF16)
            gt = gt_ref[rows, :]
            up = up_ref[rows, :]
            s = _sig(gt)
            sl = gt * s
            dup = dact * sl
            dgt = dact * up * (s + sl * (1.0 - s))
            dgt_ref[rows, :] = dgt
            dup_ref[rows, :] = dup
            act_ref[rows, :] = sl * up
            dh_sc[rows, :] += _dot(dgt, wg_ref[...]) + _dot(dup, wu_ref[...])

        @pl.when(j == NCHIP - 1)
        def _():
            dxn, dg = _rms_bwd(x_ref[...], g_ref[l:l + 1, :], dh_sc[...])
            dx_ref[...] = do_ref[...] + dxn
            dg_ref[...] += dg

    wspec = pl.BlockSpec((None, f, d), lambda i, j: (j, 0, 0))
    row = pl.BlockSpec((tt, d), lambda i, j: (i, 0))
    sh = pl.BlockSpec((None, tt, f), lambda i, j: (j, i, 0))
    ni = t // tt
    first = lambda: jnp.logical_and(pl.program_id(0) == 0, pl.program_id(1) == 0)
    last = lambda: jnp.logical_and(pl.program_id(0) == ni - 1, pl.program_id(1) == NCHIP - 1)
    outs, couts = _call(
        body, comm, first, last, name=name, grid=(ni, NCHIP),
        in_specs=[row, pl.BlockSpec(gains.shape, lambda i, j: (0, 0)), wspec, wspec, wspec, sh, sh, row],
        args=[x, gains, wg, wu, wd, gts, ups, dout],
        out_specs=[row, pl.BlockSpec((1, d), lambda i, j: (0, 0)), sh, sh, sh, row, row],
        out_shape=[SDS((t, d), F32), SDS((1, d), F32)] + [SDS((NCHIP, t, f), BF16)] * 3 + [SDS((t, d), BF16)] * 2,
        scratch=[pltpu.VMEM((tt, d), F32)], sem=("arbitrary", "arbitrary"),
        step=lambda: pl.program_id(0) * NCHIP + pl.program_id(1), nsteps=ni * NCHIP)
    return (*outs, couts)


def _rms_fwd(x, gains, l, name):
    t, d = x.shape
    tt = _rt(t)

    def body(x_ref, g_ref, o_ref):
        xv = x_ref[...]
        o_ref[...] = (xv * _rms_r(xv) * g_ref[l:l + 1, :]).astype(BF16)

    row = pl.BlockSpec((tt, d), lambda i: (i, 0))
    return pl.pallas_call(body, name=name, grid=(t // tt,), in_specs=[row, pl.BlockSpec(gains.shape, lambda i: (0, 0))],
                          out_specs=row, out_shape=SDS((t, d), BF16), compiler_params=_cparams(("arbitrary",)))(x, gains)


def _rms_back(x, gains, l, dh, dres, name, comm=None):
    t, d = x.shape
    tt = _rt(t)

    def body(x_ref, g_ref, dh_ref, dr_ref, dx_ref, dg_ref):
        @pl.when(pl.program_id(0) == 0)
        def _():
            dg_ref[...] = jnp.zeros_like(dg_ref)

        dxn, dg = _rms_bwd(x_ref[...], g_ref[l:l + 1, :], dh_ref[...])
        dx_ref[...] = dr_ref[...] + dxn
        dg_ref[...] += dg

    row = pl.BlockSpec((tt, d), lambda i: (i, 0))
    ni = t // tt
    outs, couts = _call(body, comm, lambda: pl.program_id(0) == 0, lambda: pl.program_id(0) == ni - 1, name=name,
                        grid=(ni,), in_specs=[row, pl.BlockSpec(gains.shape, lambda i: (0, 0)), row, row],
                        args=[x, gains, dh, dres], out_specs=[row, pl.BlockSpec((1, d), lambda i: (0, 0))],
                        out_shape=[SDS((t, d), F32), SDS((1, d), F32)], scratch=[], sem=("arbitrary",))
    return (*outs, couts)


def _dwconv_fwd(proj, convw, convb, l, o_a, cc, name):
    t = proj.shape[0]
    r = min(256, t)
    nb = cc // HP

    def body(a_ref, gate_ref, w_ref, b_ref, uc_ref, u_sc):
        u_sc[0:CPAD, :] = jnp.zeros((CPAD, HP), F32)

        def fill(ci, carry):
            r0 = pl.multiple_of(ci * r, r)
            u_sc[pl.ds(CPAD + r0, r), :] = a_ref[pl.ds(r0, r), :].astype(F32) * _sig(gate_ref[pl.ds(r0, r), :].astype(F32))
            return carry

        lax.fori_loop(0, t // r, fill, 0)
        w = w_ref[l]
        bias = b_ref[l:l + 1, :]

        def conv(ci, carry):
            r0 = pl.multiple_of(ci * r, r)
            win = u_sc[pl.ds(r0, r + CPAD), :]
            acc = jnp.zeros((r, HP), F32) + bias
            for k in range(KW):
                off = CPAD - (KW - 1) + k
                acc = acc + win[off:off + r, :] * w[k:k + 1, :]
            uc_ref[pl.ds(r0, r), :] = acc
            return carry

        lax.fori_loop(0, t // r, conv, 0)

    col = lambda base: pl.BlockSpec((t, HP), lambda c: (0, base // HP + c))
    return pl.pallas_call(
        body, name=name, grid=(nb,),
        in_specs=[col(o_a), col(o_a + cc), pl.BlockSpec((convw.shape[0], CPAD, HP), lambda c: (0, 0, c)),
                  pl.BlockSpec((convb.shape[0], HP), lambda c: (0, c))],
        out_specs=pl.BlockSpec((t, HP), lambda c: (0, c)), out_shape=SDS((t, cc), F32),
        scratch_shapes=[pltpu.VMEM((t + CPAD, HP), F32)],
        compiler_params=_cparams(("arbitrary",), VMEM_BIG))(proj, proj, convw, convb)


def _dwconv_bwd(duc, proj, convw, l, o_a, cc, name):
    t = proj.shape[0]
    r = min(256, t)
    nb = cc // HP

    def body(d_ref, a_ref, gate_ref, w_ref, du_ref, dw_ref, u_sc, d_sc, dw_sc):
        u_sc[0:CPAD, :] = jnp.zeros((CPAD, HP), F32)
        d_sc[t:t + CPAD, :] = jnp.zeros((CPAD, HP), F32)
        dw_sc[...] = jnp.zeros_like(dw_sc)

        def fill(ci, carry):
            r0 = pl.multiple_of(ci * r, r)
            u_sc[pl.ds(CPAD + r0, r), :] = a_ref[pl.ds(r0, r), :].astype(F32) * _sig(gate_ref[pl.ds(r0, r), :].astype(F32))
            d_sc[pl.ds(r0, r), :] = d_ref[pl.ds(r0, r), :]
            return carry

        lax.fori_loop(0, t // r, fill, 0)
        w = w_ref[l]

        def conv(ci, carry):
            r0 = pl.multiple_of(ci * r, r)
            dwin = d_sc[pl.ds(r0, r + CPAD), :]
            uwin = u_sc[pl.ds(r0, r + CPAD), :]
            dcur = dwin[0:r, :]
            acc = jnp.zeros((r, HP), F32)
            for k in range(KW):
                acc = acc + dwin[KW - 1 - k:KW - 1 - k + r, :] * w[k:k + 1, :]
                off = CPAD - (KW - 1) + k
                part = (dcur * uwin[off:off + r, :]).reshape(r // 8, 8, HP).sum(axis=0)
                dw_sc[8 * k:8 * k + 8, :] += part
            du_ref[pl.ds(r0, r), :] = acc
            return carry

        lax.fori_loop(0, t // r, conv, 0)
        dw_ref[...] = jnp.zeros_like(dw_ref)
        for k in range(KW):
            dw_ref[k:k + 1, :] = jnp.sum(dw_sc[8 * k:8 * k + 8, :], axis=0, keepdims=True)

    col = lambda base: pl.BlockSpec((t, HP), lambda c: (0, base // HP + c))
    return pl.pallas_call(
        body, name=name, grid=(nb,),
        in_specs=[pl.BlockSpec((t, HP), lambda c: (0, c)), col(o_a), col(o_a + cc),
                  pl.BlockSpec((convw.shape[0], CPAD, HP), lambda c: (0, 0, c))],
        out_specs=[pl.BlockSpec((t, HP), lambda c: (0, c)), pl.BlockSpec((CPAD, HP), lambda c: (0, c))],
        out_shape=[SDS((t, cc), F32), SDS((CPAD, cc), F32)],
        scratch_shapes=[pltpu.VMEM((t + CPAD, HP), F32), pltpu.VMEM((t + CPAD, HP), F32), pltpu.VMEM((8 * CPAD, HP), F32)],
        compiler_params=_cparams(("arbitrary",), VMEM_BIG))(duc, proj, proj, convw)


def _ln_parts(uc, g, b):
    mu = jnp.mean(uc, axis=-1, keepdims=True)
    xc = uc - mu
    r = lax.rsqrt(jnp.mean(xc * xc, axis=-1, keepdims=True) + EPS)
    xh = xc * r
    return r, xh, xh * g + b


def _ln_silu(uc, ln_g, ln_b, l, name):
    t, cc = uc.shape
    tt = _rt(t)

    def body(u_ref, g_ref, b_ref, s_ref):
        _, _, yv = _ln_parts(u_ref[...], g_ref[l:l + 1, :], b_ref[l:l + 1, :])
        s_ref[...] = (yv * _sig(yv)).astype(BF16)

    row = pl.BlockSpec((tt, cc), lambda i: (i, 0))
    full = pl.BlockSpec(ln_g.shape, lambda i: (0, 0))
    return pl.pallas_call(body, name=name, grid=(t // tt,), in_specs=[row, full, full], out_specs=row,
                          out_shape=SDS((t, cc), BF16), compiler_params=_cparams(("arbitrary",)))(uc, ln_g, ln_b)


def _ln_silu_bwd(uc, ln_g, ln_b, l, ds, name):
    t, cc = uc.shape
    tt = _rt(t)

    def body(u_ref, g_ref, b_ref, ds_ref, du_ref, dg_ref, db_ref, dcb_ref):
        @pl.when(pl.program_id(0) == 0)
        def _():
            dg_ref[...] = jnp.zeros_like(dg_ref)
            db_ref[...] = jnp.zeros_like(db_ref)
            dcb_ref[...] = jnp.zeros_like(dcb_ref)

        g = g_ref[l:l + 1, :]
        r, xh, yv = _ln_parts(u_ref[...], g, b_ref[l:l + 1, :])
        sy = _sig(yv)
        dy = ds_ref[...] * (sy * (1.0 + yv * (1.0 - sy)))
        dg_ref[...] += jnp.sum(dy * xh, axis=0, keepdims=True)
        db_ref[...] += jnp.sum(dy, axis=0, keepdims=True)
        dxh = dy * g
        du = r * (dxh - jnp.mean(dxh, axis=-1, keepdims=True) - xh * jnp.mean(dxh * xh, axis=-1, keepdims=True))
        du_ref[...] = du
        dcb_ref[...] += jnp.sum(du, axis=0, keepdims=True)

    row = pl.BlockSpec((tt, cc), lambda i: (i, 0))
    full = pl.BlockSpec(ln_g.shape, lambda i: (0, 0))
    vec = pl.BlockSpec((1, cc), lambda i: (0, 0))
    return pl.pallas_call(body, name=name, grid=(t // tt,), in_specs=[row, full, full, row],
                          out_specs=[row, vec, vec, vec], out_shape=[SDS((t, cc), F32)] + [SDS((1, cc), F32)] * 3,
                          compiler_params=_cparams(("arbitrary",)))(uc, ln_g, ln_b, ds)


def _glu_bwd(du, proj, dproj, o_a, cc, name):
    t = du.shape[0]
    tt = _rt(t)

    def body(du_ref, a_ref, gate_ref, prev_ref, o_ref):
        sg = _sig(gate_ref[...].astype(F32))
        dv = du_ref[...]
        o_ref[:, 0:cc] = (dv * sg).astype(BF16)
        o_ref[:, cc:2 * cc] = (dv * a_ref[...] * sg * (1.0 - sg)).astype(BF16)

    return pl.pallas_call(
        body, name=name, grid=(t // tt,),
        in_specs=[pl.BlockSpec((tt, cc), lambda i: (i, 0)), pl.BlockSpec((tt, cc), lambda i: (i, o_a // cc)),
                  pl.BlockSpec((tt, cc), lambda i: (i, o_a // cc + 1)), ANY],
        out_specs=pl.BlockSpec((tt, 2 * cc), lambda i: (i, o_a // (2 * cc))),
        out_shape=SDS(dproj.shape, dproj.dtype), input_output_aliases={3: 0},
        compiler_params=_cparams(("arbitrary",)))(du, proj, proj, dproj)


def _rope(v, cs, s1, s2):
    return v * cs + pltpu.roll(v, HP - ROPE // 2, 1) * s1 + pltpu.roll(v, ROPE // 2, 1) * s2


def _rope_t(dv, cs, s1, s2):
    return dv * cs + pltpu.roll(dv * s1, ROPE // 2, 1) + pltpu.roll(dv * s2, HP - ROPE // 2, 1)


def _head_norm(v, g):
    r = lax.rsqrt(jnp.sum(v * v, axis=-1, keepdims=True) * (1.0 / QK) + EPS)
    return v * r * g, r


def _head_norm_bwd(v, r, g, dy):
    xh = v * r
    dg = jnp.sum(dy * xh, axis=0, keepdims=True)
    dxh = dy * g
    dx = r * (dxh - xh * (jnp.sum(dxh * xh, axis=-1, keepdims=True) * (1.0 / QK)))
    return dx, dg


def _mla_specs(tt, lay, cq_norm, ckv_norm, qn, kn, wuq, wukv):
    ql, kvl = cq_norm.shape[1], ckv_norm.shape[1]
    full = lambda a: pl.BlockSpec(a.shape, lambda i: (0,) * a.ndim)
    tab = pl.BlockSpec((tt, HP), lambda i: (i, 0))
    return [pl.BlockSpec((tt, ql), lambda i: (i, lay["cq"] // ql)),
            pl.BlockSpec((tt, kvl), lambda i: (i, lay["ckv"] // kvl)),
            pl.BlockSpec((tt, HP), lambda i: (i, lay["kr"] // HP)),
            full(cq_norm), full(ckv_norm), full(qn), full(kn), full(wuq), full(wukv), tab, tab, tab]


def _mla_pre_fwd(proj, lay, l, cq_norm, ckv_norm, qn, kn, wuq, wukv, tabs, name):
    t = proj.shape[0]
    tt = _rt(t)
    hw = N_HEADS * HP

    def body(cq_ref, ckv_ref, kr_ref, gq_ref, gkv_ref, qn_ref, kn_ref, wq_ref, wkv_ref, c_ref, s1_ref, s2_ref,
             q_ref, k_ref, v_ref):
        cq = cq_ref[...].astype(F32)
        cqn = (cq * _rms_r(cq) * gq_ref[l:l + 1, :]).astype(BF16)
        ckv = ckv_ref[...].astype(F32)
        ckvn = (ckv * _rms_r(ckv) * gkv_ref[l:l + 1, :]).astype(BF16)
        qraw = _dot_nt(cqn, wq_ref[...])
        kv = _dot(ckvn, wkv_ref[...])
        v_ref[...] = kv.astype(BF16)
        lane = lax.broadcasted_iota(jnp.int32, (tt, HP), 1)
        krs = pltpu.roll(jnp.where(lane < ROPE, kr_ref[...].astype(F32), 0.0), NOPE, 1)
        cs, s1, s2 = c_ref[...], s1_ref[...], s2_ref[...]
        gq, gk = qn_ref[l:l + 1, :], kn_ref[l:l + 1, :]
        for h in range(N_HEADS):
            sl = slice(h * HP, (h + 1) * HP)
            qh, _ = _head_norm(qraw[:, sl], gq)
            q_ref[:, sl] = (_rope(qh, cs, s1, s2) * QK ** -0.5).astype(BF16)
            kh, _ = _head_norm(jnp.where(lane < NOPE, kv[:, sl], krs), gk)
            k_ref[:, sl] = _rope(kh, cs, s1, s2).astype(BF16)

    row = pl.BlockSpec((tt, hw), lambda i: (i, 0))
    return pl.pallas_call(
        body, name=name, grid=(t // tt,),
        in_specs=_mla_specs(tt, lay, cq_norm, ckv_norm, qn, kn, wuq, wukv),
        out_specs=[row, row, row], out_shape=[SDS((t, hw), BF16)] * 3,
        compiler_params=_cparams(("arbitrary",), VMEM_BIG))(
            proj, proj, proj, cq_norm, ckv_norm, qn, kn, wuq, wukv, *tabs)


def _mla_pre_bwd(proj, lay, l, cq_norm, ckv_norm, qn, kn, wuq, wukv, tabs, dq, dk, dv, dproj, name):
    t = proj.shape[0]
    tt = _rt(t)
    hw = N_HEADS * HP
    ql, kvl = cq_norm.shape[1], ckv_norm.shape[1]
    wm = lay["wm"]

    def body(cq_ref, ckv_ref, kr_ref, gq_ref, gkv_ref, qn_ref, kn_ref, wq_ref, wkv_ref, c_ref, s1_ref, s2_ref,
             dq_ref, dk_ref, dv_ref, prev_ref,
             o_ref, dqr_ref, dkv_ref, cqn_ref, ckvn_ref, dgq_ref, dgkv_ref, dqn_ref, dkn_ref):
        @pl.when(pl.program_id(0) == 0)
        def _():
            for ref in (dgq_ref, dgkv_ref, dqn_ref, dkn_ref):
                ref[...] = jnp.zeros_like(ref)

        cq = cq_ref[...].astype(F32)
        cqn = (cq * _rms_r(cq) * gq_ref[l:l + 1, :]).astype(BF16)
        ckv = ckv_ref[...].astype(F32)
        ckvn = (ckv * _rms_r(ckv) * gkv_ref[l:l + 1, :]).astype(BF16)
        cqn_ref[...] = cqn
        ckvn_ref[...] = ckvn
        qraw = _dot_nt(cqn, wq_ref[...])
        kv = _dot(ckvn, wkv_ref[...])
        lane = lax.broadcasted_iota(jnp.int32, (tt, HP), 1)
        krs = pltpu.roll(jnp.where(lane < ROPE, kr_ref[...].astype(F32), 0.0), NOPE, 1)
        cs, s1, s2 = c_ref[...], s1_ref[...], s2_ref[...]
        gq, gk = qn_ref[l:l + 1, :], kn_ref[l:l + 1, :]
        dkr = jnp.zeros((tt, HP), F32)
        dgq = jnp.zeros((1, HP), F32)
        dgk = jnp.zeros((1, HP), F32)
        for h in range(N_HEADS):
            sl = slice(h * HP, (h + 1) * HP)
            qh = qraw[:, sl]
            _, rq = _head_norm(qh, gq)
            dqh, dg = _head_norm_bwd(qh, rq, gq, _rope_t(dq_ref[:, sl] * QK ** -0.5, cs, s1, s2))
            dgq = dgq + dg
            dqr_ref[:, sl] = dqh.astype(BF16)
            kp = jnp.where(lane < NOPE, kv[:, sl], krs)
            _, rk = _head_norm(kp, gk)
            dkp, dg = _head_norm_bwd(kp, rk, gk, _rope_t(dk_ref[:, sl], cs, s1, s2))
            dgk = dgk + dg
            dkv_ref[:, sl] = (jnp.where(lane < NOPE, dkp, 0.0) + dv_ref[:, sl]).astype(BF16)
            dkr = dkr + dkp
        dqn_ref[...] += dgq
        dkn_ref[...] += dgk
        dkr = jnp.where(lane < ROPE, pltpu.roll(dkr, HP - NOPE, 1), 0.0)
        dcq, dg = _rms_bwd(cq, gq_ref[l:l + 1, :], _dot(dqr_ref[...], wq_ref[...]))
        dgq_ref[...] += dg
        dckv, dg = _rms_bwd(ckv, gkv_ref[l:l + 1, :], _dot_nt(dkv_ref[...], wkv_ref[...]))
        dgkv_ref[...] += dg
        o_ref[:, 0:ql] = dcq.astype(BF16)
        o_ref[:, ql:ql + kvl] = dckv.astype(BF16)
        o_ref[:, ql + kvl:ql + kvl + HP] = dkr.astype(BF16)
        o_ref[:, ql + kvl + HP:wm] = jnp.zeros((tt, wm - ql - kvl - HP), BF16)

    row = pl.BlockSpec((tt, hw), lambda i: (i, 0))
    vec = lambda n: pl.BlockSpec((1, n), lambda i: (0, 0))
    return pl.pallas_call(
        body, name=name, grid=(t // tt,),
        in_specs=_mla_specs(tt, lay, cq_norm, ckv_norm, qn, kn, wuq, wukv) + [row, row, row, ANY],
        out_specs=[pl.BlockSpec((tt, wm), lambda i: (i, lay["cq"] // wm)), row, row,
                   pl.BlockSpec((tt, ql), lambda i: (i, 0)), pl.BlockSpec((tt, kvl), lambda i: (i, 0)),
                   vec(ql), vec(kvl), vec(HP), vec(HP)],
        out_shape=[SDS(dproj.shape, dproj.dtype), SDS((t, hw), BF16), SDS((t, hw), BF16), SDS((t, ql), BF16),
                   SDS((t, kvl), BF16), SDS((1, ql), F32), SDS((1, kvl), F32), SDS((1, HP), F32), SDS((1, HP), F32)],
        input_output_aliases={15: 0},
        compiler_params=_cparams(("arbitrary",), VMEM_BIG))(
            proj, proj, proj, cq_norm, ckv_norm, qn, kn, wuq, wukv, *tabs, dq, dk, dv, dproj)


def _comb_fwd(proj, gate_bias, l, yc, ym, name):
    t, d = yc.shape
    tt = _rt(t)

    def body(p_ref, b_ref, yc_ref, ym_ref, y_ref):
        b = b_ref[l]
        g0 = _sig(p_ref[:, 0:d] + b[0:1, :])
        g1 = _sig(p_ref[:, d:2 * d] + b[1:2, :])
        y_ref[...] = (g0 * yc_ref[...] + g1 * ym_ref[...]).astype(BF16)

    row = pl.BlockSpec((tt, d), lambda i: (i, 0))
    return pl.pallas_call(
        body, name=name, grid=(t // tt,),
        in_specs=[pl.BlockSpec((tt, 2 * d), lambda i: (i, 0)), pl.BlockSpec(gate_bias.shape, lambda i: (0, 0, 0)), row, row],
        out_specs=row, out_shape=SDS((t, d), BF16), compiler_params=_cparams(("arbitrary",)))(proj, gate_bias, yc, ym)


def _comb_bwd(proj, gate_bias, l, yc, ym, dy, dp_cols, name):
    t, d = yc.shape
    tt = _rt(t)

    def body(p_ref, b_ref, yc_ref, ym_ref, dy_ref, dyc_ref, dym_ref, dp_ref, db_ref):
        @pl.when(pl.program_id(0) == 0)
        def _():
            db_ref[...] = jnp.zeros_like(db_ref)

        b = b_ref[l]
        dyv = dy_ref[...].astype(F32)
        g0 = _sig(p_ref[:, 0:d] + b[0:1, :])
        g1 = _sig(p_ref[:, d:2 * d] + b[1:2, :])
        dyc_ref[...] = (dyv * g0).astype(BF16)
        dym_ref[...] = (dyv * g1).astype(BF16)
        dg0 = dyv * yc_ref[...] * g0 * (1.0 - g0)
        dg1 = dyv * ym_ref[...] * g1 * (1.0 - g1)
        dp_ref[:, 0:d] = dg0.astype(BF16)
        dp_ref[:, d:2 * d] = dg1.astype(BF16)
        db_ref[0:1, :] += jnp.sum(dg0, axis=0, keepdims=True)
        db_ref[1:2, :] += jnp.sum(dg1, axis=0, keepdims=True)

    row = pl.BlockSpec((tt, d), lambda i: (i, 0))
    wide = pl.BlockSpec((tt, 2 * d), lambda i: (i, 0))
    return pl.pallas_call(
        body, name=name, grid=(t // tt,),
        in_specs=[wide, pl.BlockSpec(gate_bias.shape, lambda i: (0, 0, 0)), row, row, row],
        out_specs=[row, row, wide, pl.BlockSpec((2, d), lambda i: (0, 0))],
        out_shape=[SDS((t, d), BF16), SDS((t, d), BF16), SDS((t, dp_cols), BF16), SDS((2, d), F32)],
        compiler_params=_cparams(("arbitrary",)))(proj, gate_bias, yc, ym, dy)


def _loss_grad(y, target, name):
    t, d = y.shape
    tt = _rt(t)
    nt = t // tt

    def body(y_ref, t_ref, dy_ref, loss_ref, acc_sc):
        i = pl.program_id(0)

        @pl.when(i == 0)
        def _():
            acc_sc[...] = jnp.zeros_like(acc_sc)

        diff = y_ref[...] - t_ref[...]
        dy_ref[...] = diff * (1.0 / d)
        acc_sc[...] += jnp.sum(diff * diff, axis=0, keepdims=True)

        @pl.when(i == nt - 1)
        def _():
            tot = jnp.sum(acc_sc[...], axis=1, keepdims=True) * (0.5 / d)
            loss_ref[...] = jnp.broadcast_to(tot, (1, HP))

    row = pl.BlockSpec((tt, d), lambda i: (i, 0))
    return pl.pallas_call(body, name=name, grid=(nt,), in_specs=[row, row],
                          out_specs=[row, pl.BlockSpec((1, HP), lambda i: (0, 0))],
                          out_shape=[SDS((t, d), F32), SDS((1, HP), F32)],
                          scratch_shapes=[pltpu.VMEM((1, d), F32)],
                          compiler_params=_cparams(("arbitrary",)))(y, target)


def _chunk_mask(tq):
    rows = lax.broadcasted_iota(jnp.int32, (tq, tq), 0) // CHUNK
    cols = lax.broadcasted_iota(jnp.int32, (tq, tq), 1) // CHUNK
    return cols <= rows


NEG = -1e30


def _flash_fwd(q, k, v, name, comm=None):
    t = q.shape[0]
    tq = _rt(t)
    nq = t // tq
    rep = tq // HP

    def body(q_ref, k_ref, v_ref, o_ref, lse_ref):
        qi = pl.program_id(1)

        def one(hh, ki, carry, masked):
            m_prev, l_prev, acc = carry
            lanes = slice(hh * HP, (hh + 1) * HP)
            r0 = pl.multiple_of(ki * tq, tq)
            s = _dot_nt(q_ref[:, lanes], k_ref[pl.ds(r0, tq), lanes])
            if masked:
                s = jnp.where(_chunk_mask(tq), s, NEG)
            m_new = jnp.maximum(m_prev, jnp.max(s, axis=-1, keepdims=True))
            a = jnp.exp(m_prev - m_new)
            p = jnp.exp(s - jnp.tile(m_new, (1, rep)))
            l_new = a * l_prev + jnp.sum(p, axis=-1, keepdims=True)
            acc = a * acc + _dot(p.astype(BF16), v_ref[pl.ds(r0, tq), lanes])
            return m_new, l_new, acc

        def step(ki, carries, masked):
            return tuple(one(hh, ki, carries[hh], masked) for hh in range(FLASH_HEADS))

        init = (jnp.full((tq, HP), NEG, F32), jnp.zeros((tq, HP), F32), jnp.zeros((tq, HP), F32))
        carries = lax.fori_loop(0, qi, lambda ki, cr: step(ki, cr, False), (init,) * FLASH_HEADS)
        for hh, (m_fin, l_fin, acc) in enumerate(step(qi, carries, True)):
            o_ref[:, hh * HP:(hh + 1) * HP] = (acc / l_fin).astype(BF16)
            lse_ref[hh] = m_fin + jnp.log(l_fin)

    wide = FLASH_HEADS * HP
    ng = N_HEADS // FLASH_HEADS
    qspec = pl.BlockSpec((tq, wide), lambda h, qi: (qi, h))
    head = pl.BlockSpec((t, wide), lambda h, qi: (0, h))
    first = lambda: jnp.logical_and(pl.program_id(0) == 0, pl.program_id(1) == 0)
    last = lambda: jnp.logical_and(pl.program_id(0) == ng - 1, pl.program_id(1) == nq - 1)
    mid_at = lambda: pl.program_id(0) * nq + pl.program_id(1) == (3 * ng * nq) // 4
    outs, couts = _call(
        body, comm, first, last, name=name, grid=(ng, nq), mid_at=mid_at, in_specs=[qspec, head, head], args=[q, k, v],
        out_specs=[qspec, pl.BlockSpec((FLASH_HEADS, tq, HP), lambda h, qi: (h, qi, 0))],
        out_shape=[SDS(q.shape, BF16), SDS((N_HEADS, t, HP), F32)], scratch=[], sem=("arbitrary",) * 2)
    return (*outs, couts)


def _flash_bwd(q, k, v, do, o, lse, name, comm=None):
    t = q.shape[0]
    tq = _rt(t)
    nq = t // tq
    rep = tq // HP

    def body(q_ref, k_ref, v_ref, do_ref, o_ref, lse_ref, dq_ref, dk_ref, dv_ref, delta_sc):
        def prep(qi, carry):
            r0 = pl.multiple_of(qi * tq, tq)
            rows = pl.ds(r0, tq)
            dlt = jnp.sum(do_ref[rows, :].astype(F32) * o_ref[rows, :].astype(F32), axis=-1, keepdims=True)
            delta_sc[rows, :] = jnp.broadcast_to(dlt, (tq, HP))
            dq_ref[rows, :] = jnp.zeros((tq, HP), F32)
            return carry

        lax.fori_loop(0, nq, prep, 0)

        def keys(ki, carry0):
            krows = pl.ds(pl.multiple_of(ki * tq, tq), tq)
            kt, vt = k_ref[krows, :], v_ref[krows, :]

            def step(qi, carry, masked):
                dk_acc, dv_acc = carry
                rows = pl.ds(pl.multiple_of(qi * tq, tq), tq)
                qt, dot_ = q_ref[rows, :], do_ref[rows, :]
                s = _dot_nt(qt, kt)
                if masked:
                    s = jnp.where(_chunk_mask(tq), s, NEG)
                p = jnp.exp(s - jnp.tile(lse_ref[rows, :], (1, rep)))
                ds = (p * (_dot_nt(dot_, vt) - jnp.tile(delta_sc[rows, :], (1, rep)))).astype(BF16)
                dv_acc = dv_acc + _dot_tn(p.astype(BF16), dot_)
                dk_acc = dk_acc + _dot_tn(ds, qt)
                dq_ref[rows, :] += _dot(ds, kt)
                return dk_acc, dv_acc

            zero = jnp.zeros((tq, HP), F32)
            carry = step(ki, (zero, zero), True)
            dk_acc, dv_acc = lax.fori_loop(ki + 1, nq, lambda qi, cr: step(qi, cr, False), carry)
            dk_ref[krows, :] = dk_acc
            dv_ref[krows, :] = dv_acc
            return carry0

        lax.fori_loop(0, nq, keys, 0)

    head = pl.BlockSpec((t, HP), lambda h: (0, h))
    outs, couts = _call(
        body, comm, lambda: pl.program_id(0) == 0, lambda: pl.program_id(0) == N_HEADS - 1, name=name, grid=(N_HEADS,),
        in_specs=[head] * 5 + [pl.BlockSpec((None, t, HP), lambda h: (h, 0, 0))], args=[q, k, v, do, o, lse],
        out_specs=[head] * 3, out_shape=[SDS(q.shape, F32)] * 3, scratch=[pltpu.VMEM((t, HP), F32)], sem=("arbitrary",))
    return (*outs, couts)


def _add_cast(gs, rs, place, name):
    n = len(gs)
    _, r, c = gs[0].shape
    tr, tc = _tile2(r, c // 2, 3 * n, 16)
    nct = c // 2 // tc

    def body(place_ref, *refs):
        for a in range(n):
            refs[2 * n + a][...] = (refs[a][...] + refs[n + a][...]).astype(BF16)

    gspec = pl.BlockSpec((None, tr, tc), lambda j, i, k, pr: (j, i, pr[0] * nct + k))
    rspec = pl.BlockSpec((None, tr, tc), lambda j, i, k, pr: (j, i, k))
    grid_spec = pltpu.PrefetchScalarGridSpec(num_scalar_prefetch=1, grid=(NCHIP, r // tr, nct),
                                             in_specs=[gspec] * n + [rspec] * n, out_specs=[rspec] * n)
    return pl.pallas_call(body, name=name, grid_spec=grid_spec, out_shape=[SDS((NCHIP, r, c // 2), BF16)] * n,
                          compiler_params=_cparams(("arbitrary",) * 3, VMEM_BIG))(place, *gs, *rs)


def _sum_chips(ss, qs, place, l, nl, prevs, name):
    n = len(ss)
    _, r, h = ss[0].shape
    tr, tc = _tile2(r, h, 3 * n, 16)
    nct = h // tc

    def body(place_ref, *refs):
        for a in range(n):
            acc = refs[n + a][...].astype(F32)
            for kk in range(NCHIP - 1):
                acc = acc + refs[a][kk].astype(F32)
            refs[-n + a][...] = acc

    in_specs = ([pl.BlockSpec((NCHIP - 1, tr, tc), lambda i, k, pr: (0, i, k))] * n
                + [pl.BlockSpec((None, tr, tc), lambda i, k, pr: (pr[1], i, k))] * n)
    args = [*ss, *qs]
    aliases = {}
    if prevs is not None:
        aliases = {1 + len(args) + a: a for a in range(n)}
        in_specs += [ANY] * n
        args += list(prevs)
    grid_spec = pltpu.PrefetchScalarGridSpec(
        num_scalar_prefetch=1, grid=(r // tr, nct), in_specs=in_specs,
        out_specs=[pl.BlockSpec((None, tr, tc), lambda i, k, pr: (l, i, pr[0] * nct + k))] * n)
    return pl.pallas_call(body, name=name, grid_spec=grid_spec, out_shape=[SDS((nl, r, 2 * h), F32)] * n,
                          input_output_aliases=aliases,
                          compiler_params=_cparams(("arbitrary",) * 2, VMEM_BIG))(place, *args)


def _cast_place(ws, place, name):
    n = len(ws)
    nl, r, c = ws[0].shape
    tr, tc = _tile2(r, c, 2 * n * nl, 16)

    def body(place_ref, *refs):
        for a in range(n * nl):
            refs[n * nl + a][...] = refs[a][...].astype(BF16)

    in_specs = [pl.BlockSpec((None, tr, tc), functools.partial(lambda l, i, k, pr: (l, i, k), l))
                for _ in range(n) for l in range(nl)]
    grid_spec = pltpu.PrefetchScalarGridSpec(
        num_scalar_prefetch=1, grid=(r // tr, c // tc), in_specs=in_specs,
        out_specs=[pl.BlockSpec((None, tr, tc), lambda i, k, pr: (pr[1], i, k))] * (n * nl))
    outs = pl.pallas_call(body, name=name, grid_spec=grid_spec, out_shape=[SDS((NCHIP, r, c), BF16)] * (n * nl),
                          compiler_params=_cparams(("arbitrary",) * 2, VMEM_BIG))(
                              place, *[w for w in ws for _ in range(nl)])
    return [outs[a * nl:(a + 1) * nl] for a in range(n)]


def _adamw(ws, gs, ms, vs, name):
    n = len(ws)
    r, c = ws[0].shape
    tr, tc = _tile2(r, c, 7 * n, 8)
    c1, c2 = 1.0 / (1.0 - B1 ** STEP), 1.0 / (1.0 - B2 ** STEP)

    def body(*refs):
        for a in range(n):
            w, g, m, v = (refs[kk * n + a][...] for kk in range(4))
            m2 = B1 * m + (1.0 - B1) * g
            v2 = B2 * v + (1.0 - B2) * (g * g)
            refs[4 * n + a][...] = -LR * ((m2 * c1) / (jnp.sqrt(v2 * c2) + EPS_ADAM) + WD * w)
            refs[5 * n + a][...] = m2
            refs[6 * n + a][...] = v2

    blk = pl.BlockSpec((tr, tc), lambda i, k: (i, k))
    outs = pl.pallas_call(body, name=name, grid=(r // tr, c // tc), in_specs=[blk] * (4 * n),
                          out_specs=[blk] * (3 * n), out_shape=[SDS((r, c), F32)] * (3 * n),
                          compiler_params=_cparams(("arbitrary",) * 2, VMEM_BIG))(*ws, *gs, *ms, *vs)
    return outs[:n], outs[n:2 * n], outs[2 * n:]


def _place():
    x, y, c = lax.axis_index("x"), lax.axis_index("y"), lax.axis_index("c")
    return x, y, c, [(1 - x, y), (x, 1 - y), (1 - x, 1 - y)]


def _rcopy(src, dst, ssem, rsem, k, dev):
    return pltpu.make_async_remote_copy(src_ref=src, dst_ref=dst, send_sem=ssem.at[k], recv_sem=rsem.at[k],
                                        device_id=dev, device_id_type=MESH)


def _half(ref, lead, cc):
    h = ref.shape[-1] // 2
    return ref.at[(*lead, slice(None), pl.ds(cc * h, h))]


def _per_core(fn):
    c = lax.axis_index("c")
    for cc in (0, 1):
        pl.when(c == cc)(functools.partial(fn, cc))


def _gather_comm(bufs):
    n = len(bufs)

    def plan(couts, ssem, rsem, cc):
        x, y, _, peers = _place()
        me, sib = 2 * x + y, (x, y, 1 - cc)
        send, recv, fwd, recv2 = [], [], [], []
        for a in range(n):
            for kk, (px, py) in enumerate(peers):
                mine = _half(couts[a], (me,), cc)
                got = _half(couts[a], (2 * px + py,), cc)
                other = _half(couts[a], (2 * px + py,), 1 - cc)
                send.append(_rcopy(mine, mine, ssem, rsem, a * 6 + kk, (px, py, cc)))
                recv.append(_rcopy(got, got, ssem, rsem, a * 6 + kk, (px, py, cc)))
                fwd.append(_rcopy(got, got, ssem, rsem, a * 6 + 3 + kk, sib))
                recv2.append(_rcopy(other, other, ssem, rsem, a * 6 + 3 + kk, sib))
        return send, recv, fwd, recv2

    def start(cins, couts, ssem, rsem):
        def go(cc):
            for d in plan(couts, ssem, rsem, cc)[0]:
                d.start()

        _per_core(go)

    def mid(cins, couts, ssem, rsem):
        def go(cc):
            _, recv, fwd, _ = plan(couts, ssem, rsem, cc)
            for dr, df in zip(recv, fwd):
                dr.wait_recv()
                df.start()

        _per_core(go)

    def finish(cins, couts, ssem, rsem):
        def go(cc):
            send, _, fwd, recv2 = plan(couts, ssem, rsem, cc)
            for d in recv2:
                d.wait_recv()
            for d in send + fwd:
                d.wait_send()

        _per_core(go)

    return _Comm(bufs, [SDS(b.shape, b.dtype) for b in bufs], {a: a for a in range(n)}, 6 * n, start, finish, mid)


def _pair_comm(gs):
    n = len(gs)
    halves = [g.shape[-1] // 2 for g in gs]

    def plan(cins, couts, ssem, rsem, cc):
        x, y, _, _ = _place()
        return [_rcopy(cins[a].at[:, :, pl.ds((1 - cc) * halves[a], halves[a])], couts[a], ssem, rsem, a, (x, y, 1 - cc))
                for a in range(n)]

    def start(cins, couts, ssem, rsem):
        def go(cc):
            for d in plan(cins, couts, ssem, rsem, cc):
                d.start()

        _per_core(go)

    def finish(cins, couts, ssem, rsem):
        def go(cc):
            ds = plan(cins, couts, ssem, rsem, cc)
            for d in ds:
                d.wait_recv()
            for d in ds:
                d.wait_send()

        _per_core(go)

    def start_part(part, cins, couts, ssem, rsem):
        _per_core(lambda cc: plan(cins, couts, ssem, rsem, cc)[part].start())

    return _Comm(gs, [SDS(g.shape[:-1] + (g.shape[-1] // 2,), g.dtype) for g in gs], {}, n, start, finish,
                 parts=(n, start_part))


def _chips_comm(qs):
    n = len(qs)

    def plan(cins, couts, ssem, rsem):
        x, y, c, peers = _place()
        return [_rcopy(cins[a].at[2 * px + py], couts[a].at[kk], ssem, rsem, a * 3 + kk, (px, py, c))
                for a in range(n) for kk, (px, py) in enumerate(peers)]

    def start(cins, couts, ssem, rsem):
        for d in plan(cins, couts, ssem, rsem):
            d.start()

    def finish(cins, couts, ssem, rsem):
        ds = plan(cins, couts, ssem, rsem)
        for d in ds:
            d.wait_recv()
        for d in ds:
            d.wait_send()

    return _Comm(qs, [SDS((NCHIP - 1,) + q.shape[1:], q.dtype) for q in qs], {}, 3 * n, start, finish)


def _share_comm(fs, l):
    n = len(fs)

    def plan(couts, ssem, rsem, cc, which):
        x, y, _, _ = _place()
        out = []
        for a in range(n):
            piece = _half(couts[a], (l,), which)
            out.append(_rcopy(piece, piece, ssem, rsem, a, (x, y, 1 - cc)))
        return out

    def start(cins, couts, ssem, rsem):
        def go(cc):
            for d in plan(couts, ssem, rsem, cc, cc):
                d.start()

        _per_core(go)

    def finish(cins, couts, ssem, rsem):
        def go(cc):
            for d in plan(couts, ssem, rsem, cc, 1 - cc):
                d.wait_recv()
            for d in plan(couts, ssem, rsem, cc, cc):
                d.wait_send()

        _per_core(go)

    return _Comm(fs, [SDS(f.shape, f.dtype) for f in fs], {a: a for a in range(n)}, n, start, finish)


def _spread_comm(buf):
    def plan(cins, couts, ssem, rsem):
        x, y, c, _ = _place()
        me = 4 * x + 2 * y + c
        send, recv = [], []
        for rel in range(1, N_DEV):
            px, py, pc = (1 - x if rel & 4 else x, 1 - y if rel & 2 else y, 1 - c if rel & 1 else c)
            slot = couts[0].at[4 * px + 2 * py + pc]
            send.append(_rcopy(cins[0], couts[0].at[me], ssem, rsem, rel - 1, (px, py, pc)))
            recv.append(_rcopy(slot, slot, ssem, rsem, rel - 1, (px, py, pc)))
        own = pltpu.make_async_copy(cins[0], couts[0].at[me], ssem.at[N_DEV - 1])
        return send, recv, own

    def start(cins, couts, ssem, rsem):
        send, _, own = plan(cins, couts, ssem, rsem)
        for dsc in send + [own]:
            dsc.start()

    def finish(cins, couts, ssem, rsem):
        send, recv, own = plan(cins, couts, ssem, rsem)
        for dsc in recv:
            dsc.wait_recv()
        for dsc in send:
            dsc.wait_send()
        own.wait()

    return _Comm([buf], [SDS((N_DEV,) + buf.shape, buf.dtype)], {}, N_DEV, start, finish)


def _sum_devices(blocks, name):
    _, r, w = blocks.shape

    def body(in_ref, out_ref):
        acc = in_ref[0]
        for dv in range(1, N_DEV):
            acc = acc + in_ref[dv]
        out_ref[...] = acc

    vm = pl.BlockSpec(memory_space=pltpu.VMEM)
    return pl.pallas_call(body, name=name, in_specs=[vm], out_specs=vm, out_shape=SDS((r, w), F32),
                          compiler_params=_cparams(None, VMEM_BIG))(blocks)


def _cols_full(g):
    _, k, ns = g.shape
    return g.transpose(1, 0, 2).reshape(k, NCHIP * ns)


def _cols_shards(w):
    k, n = w.shape
    return w.reshape(k, NCHIP, n // NCHIP).transpose(1, 0, 2)


def _rows_pad(rows, width):
    out = jnp.concatenate([jnp.pad(a, ((0, 0), (0, width - a.shape[1]))) for a in rows], axis=0)
    return jnp.pad(out, ((0, -out.shape[0] % 8), (0, 0)))


def kernel(x, positions, ffn1_norm, ffn1_w_gate, ffn1_w_up, ffn1_w_down, mix_norm, w_in, gate_bias, conv_w, conv_b, conv_ln_g, conv_ln_b, w_conv_out, cq_norm, ckv_norm, w_uq, w_ukv, q_norm, k_norm, w_mla_out, w_out, ffn2_norm, ffn2_w_gate, ffn2_w_up, ffn2_w_down, loss_target, m_ffn1_norm, m_ffn1_w_gate, m_ffn1_w_up, m_ffn1_w_down, m_mix_norm, m_w_in, m_gate_bias, m_conv_w, m_conv_b, m_conv_ln_g, m_conv_ln_b, m_w_conv_out, m_cq_norm, m_ckv_norm, m_w_uq, m_w_ukv, m_q_norm, m_k_norm, m_w_mla_out, m_w_out, m_ffn2_norm, m_ffn2_w_gate, m_ffn2_w_up, m_ffn2_w_down, v_ffn1_norm, v_ffn1_w_gate, v_ffn1_w_up, v_ffn1_w_down, v_mix_norm, v_w_in, v_gate_bias, v_conv_w, v_conv_b, v_conv_ln_g, v_conv_ln_b, v_w_conv_out, v_cq_norm, v_ckv_norm, v_w_uq, v_w_ukv, v_q_norm, v_k_norm, v_w_mla_out, v_w_out, v_ffn2_norm, v_ffn2_w_gate, v_ffn2_w_up, v_ffn2_w_down):
    names = ["ffn1_norm", "ffn1_w_gate", "ffn1_w_up", "ffn1_w_down", "mix_norm", "w_in", "gate_bias", "conv_w",
             "conv_b", "conv_ln_g", "conv_ln_b", "w_conv_out", "cq_norm", "ckv_norm", "w_uq", "w_ukv", "q_norm",
             "k_norm", "w_mla_out", "w_out", "ffn2_norm", "ffn2_w_gate", "ffn2_w_up", "ffn2_w_down"]
    env = dict(locals())
    turned = ("ffn1_w_gate", "ffn1_w_up", "ffn2_w_gate", "ffn2_w_up", "w_in", "w_uq")
    view = lambda nm, a: jnp.swapaxes(a, 1, 2) if nm in turned else a
    wts = {nm: view(nm, env[nm]) for nm in names}
    mom = {nm: view(nm, env["m_" + nm]) for nm in names}
    var = {nm: view(nm, env["v_" + nm]) for nm in names}

    t, d = x.shape[1], x.shape[2]
    nl = ffn1_norm.shape[0]
    cc = conv_b.shape[1]
    ql, kvl = cq_norm.shape[1], ckv_norm.shape[1]
    vw = N_HEADS * V_DIM
    hw = N_HEADS * HP
    lay = {"a": 2 * d, "cq": 2 * d + 2 * cc, "ckv": 2 * d + 2 * cc + ql, "kr": 2 * d + 2 * cc + ql + kvl,
           "wm": ql + kvl + 2 * HP}
    dp = lay["cq"] + lay["wm"]
    nat_g = 2 * cc + ql + kvl + ROPE
    assert lay["cq"] % lay["wm"] == 0 and lay["cq"] % ql == 0 and lay["ckv"] % kvl == 0 and lay["a"] % (2 * cc) == 0
    assert cc % HP == 0 and d % HP == 0 and t % (2 * CHUNK) == 0 and w_in.shape[2] * NCHIP == nat_g + 2 * d
    assert nl == 2

    x0, target = x[0], loss_target[0]
    chip = 2 * lax.axis_index("x") + lax.axis_index("y")
    place = jnp.stack([lax.axis_index("c"), chip]).astype(jnp.int32)

    inv_freq = ROPE_THETA ** (-jnp.arange(0, ROPE, 2, dtype=F32) / ROPE)
    ang = positions[0].astype(F32)[:, None] * inv_freq
    cos, sin, z = jnp.cos(ang), jnp.sin(ang), jnp.zeros((t, ROPE // 2), F32)
    tabs = (jnp.concatenate([jnp.ones((t, NOPE), F32), cos, cos, jnp.zeros((t, HP - QK), F32)], axis=1),
            jnp.concatenate([jnp.zeros((t, NOPE), F32), -sin, z, jnp.zeros((t, HP - QK), F32)], axis=1),
            jnp.concatenate([jnp.zeros((t, NOPE), F32), z, sin, jnp.zeros((t, HP - QK), F32)], axis=1))

    big = ["ffn1_w_gate", "ffn1_w_up", "ffn1_w_down", "ffn2_w_gate", "ffn2_w_up", "ffn2_w_down",
           "w_in", "w_conv_out", "w_uq", "w_ukv", "w_mla_out", "w_out"]
    ffn1_w, ffn2_w, mix_w = big[0:3], big[3:6], big[6:]
    bufs, like = {}, {}
    for nm in big:
        like.setdefault(wts[nm].shape, []).append(nm)
    for gi, grp in enumerate(like.values()):
        for nm, per_layer in zip(grp, _cast_place([wts[nm] for nm in grp], place, f"cast_place_{gi}")):
            for l in range(nl):
                bufs[nm, l] = per_layer[l]

    def gather(keys):
        return _gather_comm([bufs[key] for key in keys])

    def landed(keys, outs):
        for key, o in zip(keys, outs):
            bufs[key] = o

    chunk = lambda nms, l: [(nm, l) for nm in nms]
    first_keys = chunk(ffn1_w, 0)
    landed(first_keys, _comm_call(gather(first_keys), "gather_first"))
    ride = {("ffn1", 0): chunk(mix_w, 0), ("flash", 0): chunk(ffn2_w, 0), ("ffn2", 0): chunk(ffn1_w, 1),
            ("ffn1", 1): chunk(mix_w, 1), ("flash", 1): chunk(ffn2_w, 1)}

    def riding(kind, l):
        keys = ride.get((kind, l))
        return keys or [], (None if keys is None else gather(keys))

    def mixer_weights(l):
        w_in_nat = bufs["w_in", l].reshape(nat_g + 2 * d, d)
        w_in_k = jnp.concatenate([w_in_nat[nat_g:], w_in_nat[:nat_g], jnp.zeros((dp - nat_g - 2 * d, d), BF16)], axis=0)
        w_uq_k = jnp.pad(bufs["w_uq", l].reshape(N_HEADS, QK, ql), ((0, 0), (0, HP - QK), (0, 0))).reshape(hw, ql)
        w_mo_k = jnp.pad(_cols_full(bufs["w_mla_out", l]).reshape(N_HEADS, V_DIM, d),
                         ((0, 0), (HP - V_DIM, 0), (0, 0))).reshape(hw, d)
        return dict(w_in=w_in_k, w_co=_cols_full(bufs["w_conv_out", l]), w_uq=w_uq_k,
                    w_ukv=_cols_full(bufs["w_ukv", l]), w_mo=w_mo_k, w_out=bufs["w_out", l].reshape(d, d))

    qn_k = jnp.pad(q_norm, ((0, 0), (0, HP - QK)))
    kn_k = jnp.pad(k_norm, ((0, 0), (0, HP - QK)))
    small = _rows_pad([gate_bias.reshape(nl * 2, d // NCHIP), conv_w.reshape(nl * KW, cc // NCHIP)], d)

    saved, mixw = [], []
    xc = x0
    for l in range(nl):
        keys, comm = riding("ffn1", l)
        if l == 0:
            comm = _join(comm, _spread_comm(small))
        x1, gt1, up1, got = _ffn_fwd(xc, ffn1_norm, l, *[bufs[nm, l] for nm in ffn1_w], f"ffn1_fwd_{l}", comm)
        landed(keys, got)
        if l == 0:
            everyone = got[-1][0::2]
            gb_k = everyone[:, :nl * 2, :d // NCHIP].reshape(NCHIP, nl, 2, d // NCHIP).transpose(1, 2, 0, 3).reshape(nl, 2, d)
            cw = everyone[:, nl * 2:nl * 2 + nl * KW, :cc // NCHIP].reshape(NCHIP, nl, KW, cc // NCHIP)
            cw_k = jnp.pad(cw.transpose(1, 2, 0, 3).reshape(nl, KW, cc), ((0, 0), (0, CPAD - KW), (0, 0)))
        mw = mixer_weights(l)
        mixw.append(mw)
        hm = _rms_fwd(x1, mix_norm, l, f"mix_norm_{l}")
        proj = _mm(hm, mw["w_in"], "nt", name=f"proj_{l}", out_dtype=BF16)
        uc = _dwconv_fwd(proj, cw_k, conv_b, l, lay["a"], cc, f"dwconv_{l}")
        sc = _ln_silu(uc, conv_ln_g, conv_ln_b, l, f"conv_ln_{l}")
        yc = _mm(sc, mw["w_co"], "nn", name=f"conv_out_{l}", out_dtype=BF16)
        q, k, kv = _mla_pre_fwd(proj, lay, l, cq_norm, ckv_norm, qn_k, kn_k, mw["w_uq"], mw["w_ukv"], tabs, f"mla_pre_{l}")
        keys, comm = riding("flash", l)
        o, lse, got = _flash_fwd(q, k, kv, f"flash_{l}", comm)
        landed(keys, got)
        ym = _mm(o, mw["w_mo"], "nn", name=f"mla_out_{l}", out_dtype=BF16)
        yv = _comb_fwd(proj, gb_k, l, yc, ym, f"combine_{l}")
        x2 = _mm(yv, mw["w_out"], "nn", name=f"mix_out_{l}", res=x1)
        keys, comm = riding("ffn2", l)
        x3, gt2, up2, got = _ffn_fwd(x2, ffn2_norm, l, *[bufs[nm, l] for nm in ffn2_w], f"ffn2_fwd_{l}", comm,
                                     0.9)
        landed(keys, got)
        saved.append(dict(x0=xc, x1=x1, gt1=gt1, up1=up1, hm=hm, proj=proj, uc=uc, sc=sc, yc=yc, q=q, k=k, kv=kv,
                          o=o, lse=lse, ym=ym, yv=yv, x2=x2, gt2=gt2, up2=up2))
        xc = x3
    dx, loss_row = _loss_grad(xc, target, "loss")

    gw = {nm: [None] * nl for nm in names}
    red = {}

    def shard_groups(nms, l):
        same = {}
        for nm in nms:
            same.setdefault(gw[nm][l].shape, []).append(nm)
        return list(same.values())

    def add_parts(nms, l, sib_part, tag):
        qb = {}
        for gi, grp in enumerate(shard_groups(nms, l)):
            outs = _add_cast([gw[nm][l] for nm in grp], [sib_part[nm] for nm in grp], place, f"rs_add_{tag}_{gi}")
            qb.update(zip(grp, outs))
        return qb

    def sum_parts(nms, l, qb, parts, tag):
        for gi, grp in enumerate(shard_groups(nms, l)):
            prevs = [red[nm] for nm in grp] if grp[0] in red else None
            outs = _sum_chips([parts[nm] for nm in grp], [qb[nm] for nm in grp], place, l, nl, prevs, f"rs_sum_{tag}_{gi}")
            red.update(zip(grp, outs))

    pair = lambda nms, l: _pair_comm([gw[nm][l] for nm in nms])
    chips = lambda nms, qb: _chips_comm([qb[nm] for nm in nms])
    share = lambda nms, l: _share_comm([red[nm] for nm in nms], l)

    def split(joined, got):
        return [got[o0:o1] for o0, o1 in joined.spans]

    mix_rest = [nm for nm in mix_w if nm != "w_in"]
    mix_parts = [["w_out", "w_conv_out"], ["w_mla_out", "w_ukv", "w_uq"], []]

    for l in reversed(range(nl)):
        s, mw = saved[l], mixw[l]
        riding_rs = l == 0

        def ffn_back(tag, xin, gains, gts, ups, dout, comm=None):
            wg, wu, wd = (bufs[f"{tag}_w_{p}", l] for p in ("gate", "up", "down"))
            dxi, dgain, dgt, dup, act, hb, dob, got = _ffn_bwd(xin, gains, l, wg, wu, wd, gts, ups, dout,
                                                              f"{tag}_bwd_{l}", comm)
            for p, lhs, rhs in zip(("gate", "up", "down"), (dgt, dup, act), (hb, hb, dob)):
                gw[f"{tag}_w_{p}"][l] = _mm(lhs, rhs, "tn", name=f"{tag}_dw_{p}_{l}", a_g=True)
            gw[f"{tag}_norm"][l] = dgain
            return dxi, got

        dx2, got = ffn_back("ffn2", s["x2"], ffn2_norm, s["gt2"], s["up2"], dx, pair(big, 1) if riding_rs else None)
        if riding_rs:
            qb1 = add_parts(big, 1, dict(zip(big, got)), "l1")
        dyv = _mm(dx2, mw["w_out"], "nt", name=f"mix_out_dy_{l}", out_dtype=BF16)
        gw["w_out"][l] = _mm(s["yv"], dx2, "tn", name=f"mix_out_dw_{l}").reshape(NCHIP, d // NCHIP, d)
        dyc, dym, dproj, dgb = _comb_bwd(s["proj"], gb_k, l, s["yc"], s["ym"], dyv, dp, f"combine_bwd_{l}")
        gw["gate_bias"][l] = dgb
        dsc = _mm(dyc, mw["w_co"], "nt", name=f"conv_out_ds_{l}")
        gw["w_conv_out"][l] = _cols_shards(_mm(s["sc"], dyc, "tn", name=f"conv_out_dw_{l}"))
        duc, gw["conv_ln_g"][l], gw["conv_ln_b"][l], gw["conv_b"][l] = _ln_silu_bwd(
            s["uc"], conv_ln_g, conv_ln_b, l, dsc, f"conv_ln_bwd_{l}")
        du, dcw = _dwconv_bwd(duc, s["proj"], cw_k, l, lay["a"], cc, f"dwconv_bwd_{l}")
        gw["conv_w"][l] = dcw[:KW]
        dproj = _glu_bwd(du, s["proj"], dproj, lay["a"], cc, f"glu_bwd_{l}")
        do = _mm(dym, mw["w_mo"], "nt", name=f"mla_out_do_{l}", out_dtype=BF16)
        dwmo = _mm(s["o"], dym, "tn", name=f"mla_out_dw_{l}").reshape(N_HEADS, HP, d)[:, HP - V_DIM:].reshape(vw, d)
        gw["w_mla_out"][l] = _cols_shards(dwmo)
        comm = _join(chips(big, qb1), pair(ffn2_w, 0)) if riding_rs else None
        dq, dk, dv, got = _flash_bwd(s["q"], s["k"], s["kv"], do, s["o"], s["lse"], f"flash_bwd_{l}", comm)
        if riding_rs:
            parts1, sib_f2 = split(comm, got)
            sum_parts(big, 1, qb1, dict(zip(big, parts1)), "l1")
            qb_f2 = add_parts(ffn2_w, 0, dict(zip(ffn2_w, sib_f2)), "f2")
        dproj, dqr, dkv, cqn, ckvn, gw["cq_norm"][l], gw["ckv_norm"][l], dqn, dkn = _mla_pre_bwd(
            s["proj"], lay, l, cq_norm, ckv_norm, qn_k, kn_k, mw["w_uq"], mw["w_ukv"], tabs, dq, dk, dv, dproj,
            f"mla_pre_bwd_{l}")
        gw["q_norm"][l], gw["k_norm"][l] = dqn[:, :QK], dkn[:, :QK]
        dwuq = _mm(dqr, cqn, "tn", name=f"uq_dw_{l}").reshape(N_HEADS, HP, ql)[:, :QK]
        gw["w_uq"][l] = dwuq.reshape(NCHIP, N_HEADS * QK // NCHIP, ql)
        gw["w_ukv"][l] = _cols_shards(_mm(ckvn, dkv, "tn", name=f"ukv_dw_{l}"))
        dhm = _mm(dproj, mw["w_in"], "nn", name=f"proj_dh_{l}")
        dwin = _mm(dproj, s["hm"], "tn", name=f"proj_dw_{l}")
        gw["w_in"][l] = jnp.concatenate([dwin[2 * d:2 * d + nat_g], dwin[:2 * d]], axis=0).reshape(
            NCHIP, (nat_g + 2 * d) // NCHIP, d)
        dx1, gw["mix_norm"][l], got = _rms_back(s["x1"], mix_norm, l, dhm, dx2, f"mix_norm_bwd_{l}",
                                                pair(["w_in"], 0) if riding_rs else None)
        if not riding_rs:
            dx, _ = ffn_back("ffn1", s["x0"], ffn1_norm, s["gt1"], s["up1"], dx1)
            continue
        qb_win = add_parts(["w_in"], 0, {"w_in": got[0]}, "win")
        comm = _join(share(big, 1), chips(ffn2_w, qb_f2), chips(["w_in"], qb_win), pair(mix_rest, 0))
        wg, wu, wd = (bufs[f"ffn1_w_{p}", l] for p in ("gate", "up", "down"))
        dx, dgain, dgt, dup, act, hb, dob, got = _ffn_bwd(s["x0"], ffn1_norm, l, wg, wu, wd, s["gt1"], s["up1"], dx1,
                                                         f"ffn1_bwd_{l}", comm)
        gw["ffn1_norm"][l] = dgain
        shared, parts_f2, parts_win, sib_mx = split(comm, got)
        red.update(zip(big, shared))
        sum_parts(ffn2_w, 0, qb_f2, dict(zip(ffn2_w, parts_f2)), "f2")
        sum_parts(["w_in"], 0, qb_win, {"w_in": parts_win[0]}, "win")
        qb_mx = add_parts(mix_rest, 0, dict(zip(mix_rest, sib_mx)), "mx")
        vec_names = ["ffn1_norm", "mix_norm", "ffn2_norm", "conv_b", "conv_ln_g", "conv_ln_b", "cq_norm", "ckv_norm",
                     "q_norm", "k_norm"]
        rows = [jnp.concatenate(gw[nm], axis=0) for nm in vec_names]
        rows += [jnp.concatenate(gw["gate_bias"], axis=0), jnp.concatenate(gw["conv_w"], axis=0), loss_row]
        parts_mx, f1 = {}, [f"ffn1_w_{p}" for p in ("gate", "up", "down")]
        qb_f1 = {}
        for part, (lhs, rhs) in enumerate(zip((dgt, dup, act), (hb, hb, dob))):
            cms = [_spread_comm(_rows_pad(rows, d))] if part == 0 else [pair(f1[part - 1:part], 0)]
            if mix_parts[part]:
                cms.append(chips(mix_parts[part], qb_mx))
            if part == 2:
                cms.append(chips(f1[0:1], qb_f1))
            comm = _join(*cms)
            gw[f1[part]][l], got = _mm(lhs, rhs, "tn", name=f"{f1[part]}_dw_{l}", a_g=True, comm=comm)
            pieces = split(comm, got)
            if part == 0:
                small_blocks = pieces[0][0]
            else:
                qb_f1.update(add_parts(f1[part - 1:part], 0, dict(zip(f1[part - 1:part], pieces[0])), f"f1{part - 1}"))
            if mix_parts[part]:
                parts_mx.update(zip(mix_parts[part], pieces[1]))
            if part == 2:
                sum_parts(f1[0:1], 0, qb_f1, dict(zip(f1[0:1], pieces[-1])), "f10")
        sum_parts(mix_rest, 0, qb_mx, parts_mx, "mx")

    sib = dict(zip(f1[2:3], _comm_call(pair(f1[2:3], 0), "rs_pair")))
    qb_f1.update(add_parts(f1[2:3], 0, sib, "f12"))
    parts_f1 = dict(zip(f1[1:3], _comm_call(chips(f1[1:3], qb_f1), "rs_chips")))
    sum_parts(f1[1:3], 0, qb_f1, parts_f1, "f1")
    red = dict(zip(big, _comm_call(share(big, 0), "rs_share")))
    grads = {nm: g.reshape(wts[nm].shape) for nm, g in red.items()}

    total = _sum_devices(small_blocks, "allreduce_small")
    r0 = 0
    for nm in vec_names:
        grads[nm] = total[r0:r0 + nl, :wts[nm].shape[1]]
        r0 += nl
    gb_all = total[r0:r0 + 2 * nl, :d].reshape(nl, 2, NCHIP, d // NCHIP)
    r0 += 2 * nl
    cw_all = total[r0:r0 + KW * nl, :cc].reshape(nl, KW, NCHIP, cc // NCHIP)
    r0 += KW * nl
    grads["gate_bias"] = lax.dynamic_index_in_dim(gb_all, chip, axis=2, keepdims=False)
    grads["conv_w"] = lax.dynamic_index_in_dim(cw_all, chip, axis=2, keepdims=False)
    loss = total[r0, 0]

    delta, new_m, new_v = {}, {}, {}
    by_shape = {}
    for nm in names:
        shp = wts[nm].shape
        by_shape.setdefault((shp[0] * (shp[1] if len(shp) == 3 else 1), shp[-1]), []).append(nm)
    for gi, (shp2, grp) in enumerate(by_shape.items()):
        to2 = lambda a: a.reshape(shp2)
        ds_, ms_, vs_ = _adamw([to2(wts[nm]) for nm in grp], [to2(grads[nm]) for nm in grp],
                               [to2(mom[nm]) for nm in grp], [to2(var[nm]) for nm in grp], f"adamw_{gi}")
        for nm, dd, mm_, vv in zip(grp, ds_, ms_, vs_):
            delta[nm], new_m[nm], new_v[nm] = (a.reshape(wts[nm].shape) for a in (dd, mm_, vv))

    return (loss, dx[None], *[view(nm, grads[nm]) for nm in names], *[view(nm, delta[nm]) for nm in names],
            *[view(nm, new_m[nm]) for nm in names], *[view(nm, new_v[nm]) for nm in names])
```
